```python
import math
import jax
import jax.numpy as jnp
from jax import lax
import numpy as np

D_MODEL = 1024
BATCH = 8
SEQ = 8192
DEPTH = 4

N_MIXERS = 4
N_MEM = 256
EPS = 1e-6
ROPE_THETA = 10000.0
POS_OFFSET_MAX = 4096
CHUNK = 64
D_MIX = 768
XA_HEADS = 4
XA_HEAD_DIM = 64
D_XA = XA_HEADS * XA_HEAD_DIM
GLA_HEADS = 4
GLA_DK = D_MIX // 2 // GLA_HEADS
GLA_DV = D_MIX // GLA_HEADS
GLA_RANK = 16
GLA_GATE_NORM = 16.0
DIL_GROUPS = ((128, 1), (512, 4), (2048, 16))
DIL_HEADS = 4
DIL_HEAD_DIM = 128
DIL_BLOCK = 128
D_DIL = DIL_HEADS * DIL_HEAD_DIM
SSM_HEAD_DIM = 64
SSM_HEADS = D_MIX // SSM_HEAD_DIM
SSM_GROUPS = 2
SSM_STATE = 128
SSM_CONV = 4
HGRN_EXPAND = 128
HGRN_HEADS = D_MIX // HGRN_EXPAND
HGRN_DK = HGRN_EXPAND
HGRN_DV = D_MIX // HGRN_HEADS
D_FF = 2816
FFN_CONV = 3

kernel_name = "hybrid_interleaved_gla_dilated_ssd_hgrn2_block"

F32 = jnp.float32


def rms_norm(x, g):
    xf = x.astype(F32)
    y = xf * lax.rsqrt(jnp.mean(xf * xf, axis=-1, keepdims=True) + EPS)
    return (y * g.astype(F32)).astype(x.dtype)


def split_cols(y, sizes):
    return jnp.split(y, [int(c) for c in np.cumsum(sizes)[:-1]], axis=-1)


def causal_dwconv(x, w, b):
    K, S = w.shape[0], x.shape[1]
    xp = jnp.pad(x, ((0, 0), (K - 1, 0), (0, 0)))
    y = b
    for j in range(K):
        y = y + xp[:, j:j + S] * w[j]
    return y


def rope_tables(positions, dim):
    half = dim // 2
    inv_freq = ROPE_THETA ** (-jnp.arange(half, dtype=F32) / half)
    ang = positions.astype(F32)[..., None] * inv_freq
    return jnp.cos(ang)[:, :, None], jnp.sin(ang)[:, :, None]


def apply_rope(x, cos, sin):
    half = x.shape[-1] // 2
    x1, x2 = x[..., :half].astype(F32), x[..., half:].astype(F32)
    return jnp.concatenate([x1 * cos - x2 * sin, x2 * cos + x1 * sin], axis=-1).astype(x.dtype)


def chunked_gla(q, k, v, log_a):
    Bsz, S, H, K = q.shape
    V = v.shape[-1]
    C = CHUNK
    n = S // C
    q, k, v, log_a = (t.astype(F32).reshape(Bsz, n, C, H, -1) for t in (q, k, v, log_a))
    b = jnp.cumsum(log_a, axis=2)
    b_last = b[:, :, -1:]
    b_ref = b[:, :, C // 2 - 1:C // 2]
    att = jnp.einsum('bnihk,bnjhk->bnhij', q * jnp.exp(b - b_ref), k * jnp.exp(b_ref - b))
    att = jnp.where(jnp.tril(jnp.ones((C, C), bool)), att, 0.0)
    o_intra = jnp.einsum('bnhij,bnjhv->bnihv', att, v)
    q_inter = q * jnp.exp(b)
    k_state = k * jnp.exp(b_last - b)
    decay = jnp.exp(b_last[:, :, 0])

    def step(state, xs):
        qc, kc, vc, dc = xs
        o = jnp.einsum('bchk,bhkv->bchv', qc, state)
        state = dc[..., None] * state + jnp.einsum('bchk,bchv->bhkv', kc, vc)
        return state, o

    tm = lambda t: jnp.moveaxis(t, 1, 0)
    _, o_inter = lax.scan(step, jnp.zeros((Bsz, H, K, V), F32),
                          (tm(q_inter), tm(k_state), tm(v), tm(decay)))
    return (o_intra + jnp.moveaxis(o_inter, 0, 1)).reshape(Bsz, S, H, V)


def ssd_chunked(x, dt, A, Bm, Cm):
    Bsz, S, G, Hg, P = x.shape
    N = Bm.shape[-1]
    C = CHUNK
    n = S // C
    a = (dt * A).reshape(Bsz, n, C, G, Hg)
    xdt = (x * dt[..., None]).reshape(Bsz, n, C, G, Hg, P)
    Bc = Bm.astype(F32).reshape(Bsz, n, C, G, N)
    Cc = Cm.astype(F32).reshape(Bsz, n, C, G, N)
    acs = jnp.cumsum(a, axis=2)
    seg = acs[:, :, :, None] - acs[:, :, None]
    causal = jnp.tril(jnp.ones((C, C), bool))[:, :, None, None]
    Lmat = jnp.exp(jnp.where(causal, seg, -jnp.inf))
    cb = jnp.einsum('bnlgk,bnsgk->bnlsg', Cc, Bc)
    y_diag = jnp.einsum('bnlsgh,bnsghp->bnlghp', cb[..., None] * Lmat, xdt)
    x_end = xdt * jnp.exp(acs[:, :, -1:] - acs)[..., None]
    in_decay = jnp.exp(acs)
    chunk_decay = jnp.exp(acs[:, :, -1])

    def step(hst, xs):
        c_c, b_c, xe_c, ind_c, cd_c = xs
        y_off = jnp.einsum('blgk,bghpk->blghp', c_c, hst) * ind_c[..., None]
        hst = cd_c[..., None, None] * hst + jnp.einsum('bsgk,bsghp->bghpk', b_c, xe_c)
        return hst, y_off

    tm = lambda t: jnp.moveaxis(t, 1, 0)
    _, y_off = lax.scan(step, jnp.zeros((Bsz, G, Hg, P, N), F32),
                        (tm(Cc), tm(Bc), tm(x_end), tm(in_decay), tm(chunk_decay)))
    return (y_diag + jnp.moveaxis(y_off, 0, 1)).reshape(Bsz, S, G, Hg, P)


def dilated_window_attention(q, k, v, window, dilation):
    Bsz, S, H, hd = q.shape
    r = dilation
    W = window // dilation
    Q = DIL_BLOCK
    L = S // r
    nb = -(-L // Q)
    Lp = nb * Q

    def to_blocks(t):
        t = t.reshape(Bsz, L, r, H, hd).transpose(0, 2, 1, 3, 4)
        t = jnp.pad(t, ((0, 0), (0, 0), (0, Lp - L), (0, 0), (0, 0)))
        return t.reshape(Bsz, r, nb, Q, H, hd)

    def with_prev(t):
        prev = jnp.pad(t, ((0, 0), (0, 0), (1, 0), (0, 0), (0, 0), (0, 0)))[:, :, :-1]
        return jnp.concatenate([prev, t], axis=3)

    qb = to_blocks(q)
    kk = with_prev(to_blocks(k))
    vv = with_prev(to_blocks(v))
    i = jnp.arange(Q)[:, None]
    j = jnp.arange(2 * Q)[None, :]
    dist = Q + i - j
    band = (dist >= 0) & (dist <= W)
    valid = (jnp.arange(nb)[:, None, None] > 0) | (j >= Q)[None]
    mask = band[None] & valid
    s = jnp.einsum('brnqhd,brnkhd->brnhqk', qb, kk).astype(F32) * (hd ** -0.5)
    s = jnp.where(mask[None, None, :, None], s, -jnp.inf)
    m = jnp.max(s, axis=-1, keepdims=True)
    p = jnp.exp(s - m)
    l = jnp.sum(p, axis=-1, keepdims=True)
    o = jnp.einsum('brnhqk,brnkhd->brnqhd', p / l, vv.astype(F32))
    lse = (m + jnp.log(l))[..., 0].transpose(0, 1, 2, 4, 3)
    o = o.reshape(Bsz, r, Lp, H, hd)[:, :, :L].transpose(0, 2, 1, 3, 4).reshape(Bsz, S, H, hd)
    lse = lse.reshape(Bsz, r, Lp, H)[:, :, :L].transpose(0, 2, 1, 3).reshape(Bsz, S, H)
    return o, lse


def gla_mixer(h, w_in, w_gate2, b_gate, o_norm):
    Bsz, S, _ = h.shape
    q, k, v, glr, og, xq = split_cols(
        h @ w_in, [GLA_HEADS * GLA_DK, GLA_HEADS * GLA_DK, D_MIX, GLA_RANK, D_MIX, D_XA])
    log_a = jax.nn.log_sigmoid((glr @ w_gate2 + b_gate).astype(F32)) / GLA_GATE_NORM
    hs = lambda t: t.reshape(Bsz, S, GLA_HEADS, -1)
    o = chunked_gla(hs(q) * (GLA_DK ** -0.5), hs(k), hs(v), hs(log_a))
    o = rms_norm(o, o_norm).reshape(Bsz, S, D_MIX) * jax.nn.silu(og.astype(F32))
    return o.astype(h.dtype), xq


def dilated_mixer(h, w_in, q_norm, k_norm, positions):
    Bsz, S, _ = h.shape
    nh = len(DIL_GROUPS) * DIL_HEADS
    q, k, v, xq = split_cols(h @ w_in, [nh * DIL_HEAD_DIM] * 3 + [D_XA])
    hs = lambda t: t.reshape(Bsz, S, nh, DIL_HEAD_DIM)
    cos, sin = rope_tables(positions, DIL_HEAD_DIM)
    q = apply_rope(rms_norm(hs(q), q_norm), cos, sin)
    k = apply_rope(rms_norm(hs(k), k_norm), cos, sin)
    v = hs(v)
    outs, lses = [], []
    for g, (window, dilation) in enumerate(DIL_GROUPS):
        sl = slice(g * DIL_HEADS, (g + 1) * DIL_HEADS)
        o, lse = dilated_window_attention(q[:, :, sl], k[:, :, sl], v[:, :, sl], window, dilation)
        outs.append(o)
        lses.append(lse)
    wts = jax.nn.softmax(jnp.stack(lses), axis=0)
    o = jnp.sum(wts[..., None] * jnp.stack(outs), axis=0)
    return o.reshape(Bsz, S, D_DIL).astype(h.dtype), xq


def mamba2_mixer(h, w_in, conv_w, conv_b, dt_bias, a_log, d_skip, norm_g):
    Bsz, S, _ = h.shape
    GN = SSM_GROUPS * SSM_STATE
    hg = SSM_HEADS // SSM_GROUPS
    z, xbc, dt, xq = split_cols(h @ w_in, [D_MIX, D_MIX + 2 * GN, SSM_HEADS, D_XA])
    xbc = jax.nn.silu(causal_dwconv(xbc, conv_w, conv_b))
    xs, bm, cm = split_cols(xbc, [D_MIX, GN, GN])
    dt = jax.nn.softplus(dt.astype(F32) + dt_bias.astype(F32)).reshape(Bsz, S, SSM_GROUPS, hg)
    A = -jnp.exp(a_log.astype(F32)).reshape(SSM_GROUPS, hg)
    xs = xs.astype(F32).reshape(Bsz, S, SSM_GROUPS, hg, SSM_HEAD_DIM)
    y = ssd_chunked(xs, dt, A, bm.reshape(Bsz, S, SSM_GROUPS, SSM_STATE),
                    cm.reshape(Bsz, S, SSM_GROUPS, SSM_STATE))
    y = y + d_skip.astype(F32).reshape(SSM_GROUPS, hg)[..., None] * xs
    y = y.reshape(Bsz, S, D_MIX) * jax.nn.silu(z.astype(F32))
    y = rms_norm(y.reshape(Bsz, S, SSM_GROUPS, -1), norm_g.reshape(SSM_GROUPS, -1))
    return y.reshape(Bsz, S, D_MIX).astype(h.dtype), xq


def hgrn2_mixer(h, w_in, lower_bounds, o_norm, layer):
    Bsz, S, _ = h.shape
    q, f, i, og, xq = split_cols(
        h @ w_in, [HGRN_HEADS * HGRN_DK, HGRN_HEADS * HGRN_DK, D_MIX, D_MIX, D_XA])
    lbs = jnp.cumsum(jax.nn.softmax(lower_bounds.astype(F32), axis=0), axis=0)
    lb = lbs[layer] - lbs[0]
    fg = lb + (1.0 - lb) * jax.nn.sigmoid(f.astype(F32))
    hk = lambda t: t.reshape(Bsz, S, HGRN_HEADS, HGRN_DK)
    o = chunked_gla(hk(jax.nn.silu(q.astype(F32))), hk(1.0 - fg),
                    i.reshape(Bsz, S, HGRN_HEADS, HGRN_DV), hk(jnp.log(fg)))
    o = rms_norm(o, o_norm).reshape(Bsz, S, D_MIX) * jax.nn.sigmoid(og.astype(F32))
    return o.astype(h.dtype), xq


def memory_cross_attention(xq, mem_n, w_kv, q_norm, k_norm):
    Bsz, S, _ = xq.shape
    M = mem_n.shape[1]
    k, v = jnp.split(mem_n @ w_kv, 2, axis=-1)
    q = rms_norm(xq.reshape(Bsz, S, XA_HEADS, XA_HEAD_DIM), q_norm).astype(F32)
    k = rms_norm(k.reshape(Bsz, M, XA_HEADS, XA_HEAD_DIM), k_norm).astype(F32)
    v = v.reshape(Bsz, M, XA_HEADS, XA_HEAD_DIM).astype(F32)
    p = jax.nn.softmax(jnp.einsum('bshd,bmhd->bhsm', q, k) * (XA_HEAD_DIM ** -0.5), axis=-1)
    o = jnp.einsum('bhsm,bmhd->bshd', p, v)
    return o.reshape(Bsz, S, D_XA).astype(xq.dtype)


def conv_ffn(h, w_up, conv_w, conv_b, w_down):
    u = causal_dwconv(h @ w_up, conv_w, conv_b)
    gate, val = jnp.split(u, 2, axis=-1)
    return (jax.nn.silu(gate) * val) @ w_down


def _fwd_setup_inputs(seed: int = 0) -> dict:
    key = jax.random.key(seed)
    ks = iter(jax.random.split(key, 40))

    def nrm(shape, scale=1.0):
        return jax.random.normal(next(ks), shape, F32) * scale

    def gain(shape):
        return 1.0 + nrm(shape, 0.02)

    D, F = D_MODEL, D_FF
    GN = SSM_GROUPS * SSM_STATE
    a_cols = 2 * GLA_HEADS * GLA_DK + 2 * D_MIX + GLA_RANK + D_XA
    b_cols = 3 * len(DIL_GROUPS) * D_DIL + D_XA
    c_cols = 2 * D_MIX + 2 * GN + SSM_HEADS + D_XA
    d_cols = 2 * HGRN_HEADS * HGRN_DK + 2 * D_MIX + D_XA
    a_out_in, b_out_in = D_MIX + D_XA, D_DIL + D_XA
    c_out_in, d_out_in = D_MIX + D_XA, D_MIX + D_XA
    inp = {}
    inp['x'] = nrm((BATCH, SEQ, D))
    inp['mem'] = nrm((BATCH, N_MEM, D))
    inp['positions'] = (jnp.arange(SEQ, dtype=jnp.int32)[None, :]
                        + jax.random.randint(next(ks), (BATCH, 1), 0, POS_OFFSET_MAX, dtype=jnp.int32))
    inp['mem_norm'] = gain((D,))
    inp['mix_norm'] = gain((DEPTH, D))
    inp['xa_w_kv'] = nrm((DEPTH, D, 2 * D_XA), D ** -0.5)
    inp['xa_q_norm'] = gain((DEPTH, XA_HEAD_DIM))
    inp['xa_k_norm'] = gain((DEPTH, XA_HEAD_DIM))
    inp['ffn_norm'] = gain((DEPTH, D))
    inp['ffn_w_up'] = nrm((DEPTH, D, 2 * F), D ** -0.5)
    inp['ffn_conv_w'] = nrm((DEPTH, FFN_CONV, 2 * F), FFN_CONV ** -0.5)
    inp['ffn_conv_b'] = nrm((DEPTH, 2 * F), 0.02)
    inp['ffn_w_down'] = nrm((DEPTH, F, D), 0.5 * F ** -0.5)
    inp['a_w_in'] = nrm((D, a_cols), D ** -0.5)
    inp['a_w_gate2'] = nrm((GLA_RANK, GLA_HEADS * GLA_DK), GLA_RANK ** -0.5)
    inp['a_b_gate'] = nrm((GLA_HEADS * GLA_DK,), 0.1)
    inp['a_o_norm'] = gain((GLA_DV,))
    inp['a_w_out'] = nrm((a_out_in, D), 0.5 * a_out_in ** -0.5)
    inp['b_w_in'] = nrm((D, b_cols), D ** -0.5)
    inp['b_q_norm'] = gain((DIL_HEAD_DIM,))
    inp['b_k_norm'] = gain((DIL_HEAD_DIM,))
    inp['b_w_out'] = nrm((b_out_in, D), 0.5 * b_out_in ** -0.5)
    inp['c_w_in'] = nrm((D, c_cols), D ** -0.5)
    inp['c_conv_w'] = nrm((SSM_CONV, D_MIX + 2 * GN), 0.5)
    inp['c_conv_b'] = nrm((D_MIX + 2 * GN,), 0.02)
    dt0 = jnp.exp(jax.random.uniform(next(ks), (SSM_HEADS,), F32, math.log(1e-3), math.log(1e-1)))
    inp['c_dt_bias'] = dt0 + jnp.log(-jnp.expm1(-dt0))
    inp['c_a_log'] = jnp.log(jax.random.uniform(next(ks), (SSM_HEADS,), F32, 1.0, 16.0))
    inp['c_d'] = gain((SSM_HEADS,))
    inp['c_norm'] = gain((D_MIX,))
    inp['c_w_out'] = nrm((c_out_in, D), 0.5 * c_out_in ** -0.5)
    inp['d_w_in'] = nrm((D, d_cols), D ** -0.5)
    inp['d_lower_bounds'] = nrm((DEPTH, HGRN_HEADS * HGRN_DK), 0.02)
    inp['d_o_norm'] = gain((HGRN_DV,))
    inp['d_w_out'] = nrm((d_out_in, D), 0.5 * d_out_in ** -0.5)
    return inp


def _fwd_reference(x, mem, positions, mem_norm, mix_norm, xa_w_kv, xa_q_norm, xa_k_norm,
              ffn_norm, ffn_w_up, ffn_conv_w, ffn_conv_b, ffn_w_down,
              a_w_in, a_w_gate2, a_b_gate, a_o_norm, a_w_out,
              b_w_in, b_q_norm, b_k_norm, b_w_out,
              c_w_in, c_conv_w, c_conv_b, c_dt_bias, c_a_log, c_d, c_norm, c_w_out,
              d_w_in, d_lower_bounds, d_o_norm, d_w_out):
    mem_n = rms_norm(mem, mem_norm)
    for i in range(DEPTH):
        h = rms_norm(x, mix_norm[i])
        kind = i % N_MIXERS
        if kind == 0:
            tok, xq = gla_mixer(h, a_w_in, a_w_gate2, a_b_gate, a_o_norm)
            w_out = a_w_out
        elif kind == 1:
            tok, xq = dilated_mixer(h, b_w_in, b_q_norm, b_k_norm, positions)
            w_out = b_w_out
        elif kind == 2:
            tok, xq = mamba2_mixer(h, c_w_in, c_conv_w, c_conv_b, c_dt_bias, c_a_log, c_d, c_norm)
            w_out = c_w_out
        else:
            tok, xq = hgrn2_mixer(h, d_w_in, d_lower_bounds, d_o_norm, i)
            w_out = d_w_out
        xa = memory_cross_attention(xq, mem_n, xa_w_kv[i], xa_q_norm[i], xa_k_norm[i])
        x = x + jnp.concatenate([tok, xa], axis=-1) @ w_out
        x = x + conv_ffn(rms_norm(x, ffn_norm[i]), ffn_w_up[i], ffn_conv_w[i], ffn_conv_b[i],
                         ffn_w_down[i])
    return x


import jax as _jax
import jax.numpy as _jnp

TWIN_FORMAT = 'train_step'
FWD_PARAMS = ['x', 'mem', 'positions', 'mem_norm', 'mix_norm', 'xa_w_kv', 'xa_q_norm', 'xa_k_norm', 'ffn_norm', 'ffn_w_up', 'ffn_conv_w', 'ffn_conv_b', 'ffn_w_down', 'a_w_in', 'a_w_gate2', 'a_b_gate', 'a_o_norm', 'a_w_out', 'b_w_in', 'b_q_norm', 'b_k_norm', 'b_w_out', 'c_w_in', 'c_conv_w', 'c_conv_b', 'c_dt_bias', 'c_a_log', 'c_d', 'c_norm', 'c_w_out', 'd_w_in', 'd_lower_bounds', 'd_o_norm', 'd_w_out']
TWIN_WEIGHTS = ['mem_norm', 'mix_norm', 'xa_w_kv', 'xa_q_norm', 'xa_k_norm', 'ffn_norm', 'ffn_w_up', 'ffn_conv_w', 'ffn_conv_b', 'ffn_w_down', 'a_w_in', 'a_w_gate2', 'a_b_gate', 'a_o_norm', 'a_w_out', 'b_w_in', 'b_q_norm', 'b_k_norm', 'b_w_out', 'c_w_in', 'c_conv_w', 'c_conv_b', 'c_dt_bias', 'c_a_log', 'c_d', 'c_norm', 'c_w_out', 'd_w_in', 'd_lower_bounds', 'd_o_norm', 'd_w_out']
TWIN_DIFF_INPUT = 'x'
TWIN_INPUTS = ['x', 'mem', 'positions', 'mem_norm', 'mix_norm', 'xa_w_kv', 'xa_q_norm', 'xa_k_norm', 'ffn_norm', 'ffn_w_up', 'ffn_conv_w', 'ffn_conv_b', 'ffn_w_down', 'a_w_in', 'a_w_gate2', 'a_b_gate', 'a_o_norm', 'a_w_out', 'b_w_in', 'b_q_norm', 'b_k_norm', 'b_w_out', 'c_w_in', 'c_conv_w', 'c_conv_b', 'c_dt_bias', 'c_a_log', 'c_d', 'c_norm', 'c_w_out', 'd_w_in', 'd_lower_bounds', 'd_o_norm', 'd_w_out', 'loss_target', 'm_mem_norm', 'm_mix_norm', 'm_xa_w_kv', 'm_xa_q_norm', 'm_xa_k_norm', 'm_ffn_norm', 'm_ffn_w_up', 'm_ffn_conv_w', 'm_ffn_conv_b', 'm_ffn_w_down', 'm_a_w_in', 'm_a_w_gate2', 'm_a_b_gate', 'm_a_o_norm', 'm_a_w_out', 'm_b_w_in', 'm_b_q_norm', 'm_b_k_norm', 'm_b_w_out', 'm_c_w_in', 'm_c_conv_w', 'm_c_conv_b', 'm_c_dt_bias', 'm_c_a_log', 'm_c_d', 'm_c_norm', 'm_c_w_out', 'm_d_w_in', 'm_d_lower_bounds', 'm_d_o_norm', 'm_d_w_out', 'v_mem_norm', 'v_mix_norm', 'v_xa_w_kv', 'v_xa_q_norm', 'v_xa_k_norm', 'v_ffn_norm', 'v_ffn_w_up', 'v_ffn_conv_w', 'v_ffn_conv_b', 'v_ffn_w_down', 'v_a_w_in', 'v_a_w_gate2', 'v_a_b_gate', 'v_a_o_norm', 'v_a_w_out', 'v_b_w_in', 'v_b_q_norm', 'v_b_k_norm', 'v_b_w_out', 'v_c_w_in', 'v_c_conv_w', 'v_c_conv_b', 'v_c_dt_bias', 'v_c_a_log', 'v_c_d', 'v_c_norm', 'v_c_w_out', 'v_d_w_in', 'v_d_lower_bounds', 'v_d_o_norm', 'v_d_w_out']
TWIN_OUTPUTS = ['loss', 'grad_x', 'grad_mem_norm', 'grad_mix_norm', 'grad_xa_w_kv', 'grad_xa_q_norm', 'grad_xa_k_norm', 'grad_ffn_norm', 'grad_ffn_w_up', 'grad_ffn_conv_w', 'grad_ffn_conv_b', 'grad_ffn_w_down', 'grad_a_w_in', 'grad_a_w_gate2', 'grad_a_b_gate', 'grad_a_o_norm', 'grad_a_w_out', 'grad_b_w_in', 'grad_b_q_norm', 'grad_b_k_norm', 'grad_b_w_out', 'grad_c_w_in', 'grad_c_conv_w', 'grad_c_conv_b', 'grad_c_dt_bias', 'grad_c_a_log', 'grad_c_d', 'grad_c_norm', 'grad_c_w_out', 'grad_d_w_in', 'grad_d_lower_bounds', 'grad_d_o_norm', 'grad_d_w_out', 'delta_mem_norm', 'delta_mix_norm', 'delta_xa_w_kv', 'delta_xa_q_norm', 'delta_xa_k_norm', 'delta_ffn_norm', 'delta_ffn_w_up', 'delta_ffn_conv_w', 'delta_ffn_conv_b', 'delta_ffn_w_down', 'delta_a_w_in', 'delta_a_w_gate2', 'delta_a_b_gate', 'delta_a_o_norm', 'delta_a_w_out', 'delta_b_w_in', 'delta_b_q_norm', 'delta_b_k_norm', 'delta_b_w_out', 'delta_c_w_in', 'delta_c_conv_w', 'delta_c_conv_b', 'delta_c_dt_bias', 'delta_c_a_log', 'delta_c_d', 'delta_c_norm', 'delta_c_w_out', 'delta_d_w_in', 'delta_d_lower_bounds', 'delta_d_o_norm', 'delta_d_w_out', 'new_m_mem_norm', 'new_m_mix_norm', 'new_m_xa_w_kv', 'new_m_xa_q_norm', 'new_m_xa_k_norm', 'new_m_ffn_norm', 'new_m_ffn_w_up', 'new_m_ffn_conv_w', 'new_m_ffn_conv_b', 'new_m_ffn_w_down', 'new_m_a_w_in', 'new_m_a_w_gate2', 'new_m_a_b_gate', 'new_m_a_o_norm', 'new_m_a_w_out', 'new_m_b_w_in', 'new_m_b_q_norm', 'new_m_b_k_norm', 'new_m_b_w_out', 'new_m_c_w_in', 'new_m_c_conv_w', 'new_m_c_conv_b', 'new_m_c_dt_bias', 'new_m_c_a_log', 'new_m_c_d', 'new_m_c_norm', 'new_m_c_w_out', 'new_m_d_w_in', 'new_m_d_lower_bounds', 'new_m_d_o_norm', 'new_m_d_w_out', 'new_v_mem_norm', 'new_v_mix_norm', 'new_v_xa_w_kv', 'new_v_xa_q_norm', 'new_v_xa_k_norm', 'new_v_ffn_norm', 'new_v_ffn_w_up', 'new_v_ffn_conv_w', 'new_v_ffn_conv_b', 'new_v_ffn_w_down', 'new_v_a_w_in', 'new_v_a_w_gate2', 'new_v_a_b_gate', 'new_v_a_o_norm', 'new_v_a_w_out', 'new_v_b_w_in', 'new_v_b_q_norm', 'new_v_b_k_norm', 'new_v_b_w_out', 'new_v_c_w_in', 'new_v_c_conv_w', 'new_v_c_conv_b', 'new_v_c_dt_bias', 'new_v_c_a_log', 'new_v_c_d', 'new_v_c_norm', 'new_v_c_w_out', 'new_v_d_w_in', 'new_v_d_lower_bounds', 'new_v_d_o_norm', 'new_v_d_w_out']
TWIN_LEAF_KINDS = {'loss': 'loss', 'grad_x': 'grad_x', 'grad_mem_norm': 'grad_w', 'grad_mix_norm': 'grad_w', 'grad_xa_w_kv': 'grad_w', 'grad_xa_q_norm': 'grad_w', 'grad_xa_k_norm': 'grad_w', 'grad_ffn_norm': 'grad_w', 'grad_ffn_w_up': 'grad_w', 'grad_ffn_conv_w': 'grad_w', 'grad_ffn_conv_b': 'grad_w', 'grad_ffn_w_down': 'grad_w', 'grad_a_w_in': 'grad_w', 'grad_a_w_gate2': 'grad_w', 'grad_a_b_gate': 'grad_w', 'grad_a_o_norm': 'grad_w', 'grad_a_w_out': 'grad_w', 'grad_b_w_in': 'grad_w', 'grad_b_q_norm': 'grad_w', 'grad_b_k_norm': 'grad_w', 'grad_b_w_out': 'grad_w', 'grad_c_w_in': 'grad_w', 'grad_c_conv_w': 'grad_w', 'grad_c_conv_b': 'grad_w', 'grad_c_dt_bias': 'grad_w', 'grad_c_a_log': 'grad_w', 'grad_c_d': 'grad_w', 'grad_c_norm': 'grad_w', 'grad_c_w_out': 'grad_w', 'grad_d_w_in': 'grad_w', 'grad_d_lower_bounds': 'grad_w', 'grad_d_o_norm': 'grad_w', 'grad_d_w_out': 'grad_w', 'delta_mem_norm': 'delta_w', 'delta_mix_norm': 'delta_w', 'delta_xa_w_kv': 'delta_w', 'delta_xa_q_norm': 'delta_w', 'delta_xa_k_norm': 'delta_w', 'delta_ffn_norm': 'delta_w', 'delta_ffn_w_up': 'delta_w', 'delta_ffn_conv_w': 'delta_w', 'delta_ffn_conv_b': 'delta_w', 'delta_ffn_w_down': 'delta_w', 'delta_a_w_in': 'delta_w', 'delta_a_w_gate2': 'delta_w', 'delta_a_b_gate': 'delta_w', 'delta_a_o_norm': 'delta_w', 'delta_a_w_out': 'delta_w', 'delta_b_w_in': 'delta_w', 'delta_b_q_norm': 'delta_w', 'delta_b_k_norm': 'delta_w', 'delta_b_w_out': 'delta_w', 'delta_c_w_in': 'delta_w', 'delta_c_conv_w': 'delta_w', 'delta_c_conv_b': 'delta_w', 'delta_c_dt_bias': 'delta_w', 'delta_c_a_log': 'delta_w', 'delta_c_d': 'delta_w', 'delta_c_norm': 'delta_w', 'delta_c_w_out': 'delta_w', 'delta_d_w_in': 'delta_w', 'delta_d_lower_bounds': 'delta_w', 'delta_d_o_norm': 'delta_w', 'delta_d_w_out': 'delta_w', 'new_m_mem_norm': 'new_m', 'new_m_mix_norm': 'new_m', 'new_m_xa_w_kv': 'new_m', 'new_m_xa_q_norm': 'new_m', 'new_m_xa_k_norm': 'new_m', 'new_m_ffn_norm': 'new_m', 'new_m_ffn_w_up': 'new_m', 'new_m_ffn_conv_w': 'new_m', 'new_m_ffn_conv_b': 'new_m', 'new_m_ffn_w_down': 'new_m', 'new_m_a_w_in': 'new_m', 'new_m_a_w_gate2': 'new_m', 'new_m_a_b_gate': 'new_m', 'new_m_a_o_norm': 'new_m', 'new_m_a_w_out': 'new_m', 'new_m_b_w_in': 'new_m', 'new_m_b_q_norm': 'new_m', 'new_m_b_k_norm': 'new_m', 'new_m_b_w_out': 'new_m', 'new_m_c_w_in': 'new_m', 'new_m_c_conv_w': 'new_m', 'new_m_c_conv_b': 'new_m', 'new_m_c_dt_bias': 'new_m', 'new_m_c_a_log': 'new_m', 'new_m_c_d': 'new_m', 'new_m_c_norm': 'new_m', 'new_m_c_w_out': 'new_m', 'new_m_d_w_in': 'new_m', 'new_m_d_lower_bounds': 'new_m', 'new_m_d_o_norm': 'new_m', 'new_m_d_w_out': 'new_m', 'new_v_mem_norm': 'new_v', 'new_v_mix_norm': 'new_v', 'new_v_xa_w_kv': 'new_v', 'new_v_xa_q_norm': 'new_v', 'new_v_xa_k_norm': 'new_v', 'new_v_ffn_norm': 'new_v', 'new_v_ffn_w_up': 'new_v', 'new_v_ffn_conv_w': 'new_v', 'new_v_ffn_conv_b': 'new_v', 'new_v_ffn_w_down': 'new_v', 'new_v_a_w_in': 'new_v', 'new_v_a_w_gate2': 'new_v', 'new_v_a_b_gate': 'new_v', 'new_v_a_o_norm': 'new_v', 'new_v_a_w_out': 'new_v', 'new_v_b_w_in': 'new_v', 'new_v_b_q_norm': 'new_v', 'new_v_b_k_norm': 'new_v', 'new_v_b_w_out': 'new_v', 'new_v_c_w_in': 'new_v', 'new_v_c_conv_w': 'new_v', 'new_v_c_conv_b': 'new_v', 'new_v_c_dt_bias': 'new_v', 'new_v_c_a_log': 'new_v', 'new_v_c_d': 'new_v', 'new_v_c_norm': 'new_v', 'new_v_c_w_out': 'new_v', 'new_v_d_w_in': 'new_v', 'new_v_d_lower_bounds': 'new_v', 'new_v_d_o_norm': 'new_v', 'new_v_d_w_out': 'new_v'}


def _forward(args):
    return _fwd_reference(*[args[k] for k in FWD_PARAMS])


def _output_shape():
    out = _jax.eval_shape(lambda: _forward(_fwd_setup_inputs(0)))
    return out.shape, out.dtype

N_MICROBATCH = 1
ADAM_LR = 0.001
ADAM_B1 = 0.9
ADAM_B2 = 0.999
ADAM_EPS = 1e-08
ADAM_WD = 0.01
ADAM_STEP = 10
PER_EXAMPLE_BATCH_AXIS = {'x': 0, 'mem': 0, 'positions': 0, 'loss_target': 0}
SHARED_INPUTS = []
_WEIGHT_DTYPES = {'mem_norm': _jnp.float32, 'mix_norm': _jnp.float32, 'xa_w_kv': _jnp.float32, 'xa_q_norm': _jnp.float32, 'xa_k_norm': _jnp.float32, 'ffn_norm': _jnp.float32, 'ffn_w_up': _jnp.float32, 'ffn_conv_w': _jnp.float32, 'ffn_conv_b': _jnp.float32, 'ffn_w_down': _jnp.float32, 'a_w_in': _jnp.float32, 'a_w_gate2': _jnp.float32, 'a_b_gate': _jnp.float32, 'a_o_norm': _jnp.float32, 'a_w_out': _jnp.float32, 'b_w_in': _jnp.float32, 'b_q_norm': _jnp.float32, 'b_k_norm': _jnp.float32, 'b_w_out': _jnp.float32, 'c_w_in': _jnp.float32, 'c_conv_w': _jnp.float32, 'c_conv_b': _jnp.float32, 'c_dt_bias': _jnp.float32, 'c_a_log': _jnp.float32, 'c_d': _jnp.float32, 'c_norm': _jnp.float32, 'c_w_out': _jnp.float32, 'd_w_in': _jnp.float32, 'd_lower_bounds': _jnp.float32, 'd_o_norm': _jnp.float32, 'd_w_out': _jnp.float32}
MOMENT_SCALE = {'mem_norm': 1.877667e-01, 'mix_norm': 1.877173e+00, 'xa_w_kv': 8.022241e-02, 'xa_q_norm': 4.917234e-01, 'xa_k_norm': 4.951179e-01, 'ffn_norm': 1.305277e+01, 'ffn_w_up': 1.387727e-01, 'ffn_conv_w': 1.674595e+00, 'ffn_conv_b': 1.555156e+00, 'ffn_w_down': 4.021251e-01, 'a_w_in': 2.423982e-01, 'a_w_gate2': 3.936938e-02, 'a_b_gate': 1.553224e-01, 'a_o_norm': 2.268742e+01, 'a_w_out': 3.895224e-01, 'b_w_in': 3.824977e-02, 'b_q_norm': 2.796247e-01, 'b_k_norm': 2.746880e-01, 'b_w_out': 1.189206e-01, 'c_w_in': 2.504802e-01, 'c_conv_w': 4.236293e-01, 'c_conv_b': 1.481060e+00, 'c_dt_bias': 7.138776e-01, 'c_a_log': 2.903133e+00, 'c_d': 1.888503e+00, 'c_norm': 2.034517e+01, 'c_w_out': 1.385385e+00, 'd_w_in': 1.683563e-01, 'd_lower_bounds': 1.447050e-02, 'd_o_norm': 2.741337e+01, 'd_w_out': 6.221289e-01}


def _to_microbatches(a, axis):
    t = _jnp.moveaxis(a, axis, 0)
    t = t.reshape((N_MICROBATCH, t.shape[0] // N_MICROBATCH) + t.shape[1:])
    return _jnp.moveaxis(t, 1, axis + 1)


def setup_inputs(seed: int = 0) -> dict:
    inp = _fwd_setup_inputs(seed)
    key = _jax.random.fold_in(_jax.random.key(seed), 7919)
    shape, _ = _output_shape()
    out = dict(inp)
    out["loss_target"] = _jax.random.normal(_jax.random.fold_in(key, 0), shape, _jnp.float32)
    for i, name in enumerate(TWIN_WEIGHTS):
        w = inp[name].astype(_jnp.float32)
        if MOMENT_SCALE is None:
            s = _jnp.sqrt(_jnp.mean(_jnp.square(w)) + 1e-30)
        else:
            s = MOMENT_SCALE[name]
        km, kv = _jax.random.split(_jax.random.fold_in(key, i + 1))
        out[name] = w
        out["m_" + name] = s * _jax.random.normal(km, w.shape, _jnp.float32)
        out["v_" + name] = (s * s) * _jax.random.uniform(kv, w.shape, _jnp.float32, 0.5, 1.5)
    if N_MICROBATCH > 1:
        for name, axis in PER_EXAMPLE_BATCH_AXIS.items():
            out[name] = _to_microbatches(out[name], axis)
    return {'x': out['x'], 'mem': out['mem'], 'positions': out['positions'], 'mem_norm': out['mem_norm'], 'mix_norm': out['mix_norm'], 'xa_w_kv': out['xa_w_kv'], 'xa_q_norm': out['xa_q_norm'], 'xa_k_norm': out['xa_k_norm'], 'ffn_norm': out['ffn_norm'], 'ffn_w_up': out['ffn_w_up'], 'ffn_conv_w': out['ffn_conv_w'], 'ffn_conv_b': out['ffn_conv_b'], 'ffn_w_down': out['ffn_w_down'], 'a_w_in': out['a_w_in'], 'a_w_gate2': out['a_w_gate2'], 'a_b_gate': out['a_b_gate'], 'a_o_norm': out['a_o_norm'], 'a_w_out': out['a_w_out'], 'b_w_in': out['b_w_in'], 'b_q_norm': out['b_q_norm'], 'b_k_norm': out['b_k_norm'], 'b_w_out': out['b_w_out'], 'c_w_in': out['c_w_in'], 'c_conv_w': out['c_conv_w'], 'c_conv_b': out['c_conv_b'], 'c_dt_bias': out['c_dt_bias'], 'c_a_log': out['c_a_log'], 'c_d': out['c_d'], 'c_norm': out['c_norm'], 'c_w_out': out['c_w_out'], 'd_w_in': out['d_w_in'], 'd_lower_bounds': out['d_lower_bounds'], 'd_o_norm': out['d_o_norm'], 'd_w_out': out['d_w_out'], 'loss_target': out['loss_target'], 'm_mem_norm': out['m_mem_norm'], 'm_mix_norm': out['m_mix_norm'], 'm_xa_w_kv': out['m_xa_w_kv'], 'm_xa_q_norm': out['m_xa_q_norm'], 'm_xa_k_norm': out['m_xa_k_norm'], 'm_ffn_norm': out['m_ffn_norm'], 'm_ffn_w_up': out['m_ffn_w_up'], 'm_ffn_conv_w': out['m_ffn_conv_w'], 'm_ffn_conv_b': out['m_ffn_conv_b'], 'm_ffn_w_down': out['m_ffn_w_down'], 'm_a_w_in': out['m_a_w_in'], 'm_a_w_gate2': out['m_a_w_gate2'], 'm_a_b_gate': out['m_a_b_gate'], 'm_a_o_norm': out['m_a_o_norm'], 'm_a_w_out': out['m_a_w_out'], 'm_b_w_in': out['m_b_w_in'], 'm_b_q_norm': out['m_b_q_norm'], 'm_b_k_norm': out['m_b_k_norm'], 'm_b_w_out': out['m_b_w_out'], 'm_c_w_in': out['m_c_w_in'], 'm_c_conv_w': out['m_c_conv_w'], 'm_c_conv_b': out['m_c_conv_b'], 'm_c_dt_bias': out['m_c_dt_bias'], 'm_c_a_log': out['m_c_a_log'], 'm_c_d': out['m_c_d'], 'm_c_norm': out['m_c_norm'], 'm_c_w_out': out['m_c_w_out'], 'm_d_w_in': out['m_d_w_in'], 'm_d_lower_bounds': out['m_d_lower_bounds'], 'm_d_o_norm': out['m_d_o_norm'], 'm_d_w_out': out['m_d_w_out'], 'v_mem_norm': out['v_mem_norm'], 'v_mix_norm': out['v_mix_norm'], 'v_xa_w_kv': out['v_xa_w_kv'], 'v_xa_q_norm': out['v_xa_q_norm'], 'v_xa_k_norm': out['v_xa_k_norm'], 'v_ffn_norm': out['v_ffn_norm'], 'v_ffn_w_up': out['v_ffn_w_up'], 'v_ffn_conv_w': out['v_ffn_conv_w'], 'v_ffn_conv_b': out['v_ffn_conv_b'], 'v_ffn_w_down': out['v_ffn_w_down'], 'v_a_w_in': out['v_a_w_in'], 'v_a_w_gate2': out['v_a_w_gate2'], 'v_a_b_gate': out['v_a_b_gate'], 'v_a_o_norm': out['v_a_o_norm'], 'v_a_w_out': out['v_a_w_out'], 'v_b_w_in': out['v_b_w_in'], 'v_b_q_norm': out['v_b_q_norm'], 'v_b_k_norm': out['v_b_k_norm'], 'v_b_w_out': out['v_b_w_out'], 'v_c_w_in': out['v_c_w_in'], 'v_c_conv_w': out['v_c_conv_w'], 'v_c_conv_b': out['v_c_conv_b'], 'v_c_dt_bias': out['v_c_dt_bias'], 'v_c_a_log': out['v_c_a_log'], 'v_c_d': out['v_c_d'], 'v_c_norm': out['v_c_norm'], 'v_c_w_out': out['v_c_w_out'], 'v_d_w_in': out['v_d_w_in'], 'v_d_lower_bounds': out['v_d_lower_bounds'], 'v_d_o_norm': out['v_d_o_norm'], 'v_d_w_out': out['v_d_w_out']}


def _loss(weights, diff, rest, loss_target):
    with _jax.named_scope("forward"):
        args = {**rest, TWIN_DIFF_INPUT: diff, **{k: w.astype(_WEIGHT_DTYPES[k]) for k, w in weights.items()}}
        y = _forward(args)
    with _jax.named_scope("loss_head"):
        err = _jnp.square(y.astype(_jnp.float32) - loss_target)
        return 0.5 * _jnp.sum(_jnp.mean(err, axis=-1)) if err.ndim else 0.5 * err


def _adamw(w, g, m, v):
    m = ADAM_B1 * m + (1.0 - ADAM_B1) * g
    v = ADAM_B2 * v + (1.0 - ADAM_B2) * _jnp.square(g)
    m_hat = m / (1.0 - ADAM_B1 ** ADAM_STEP)
    v_hat = v / (1.0 - ADAM_B2 ** ADAM_STEP)
    delta = -ADAM_LR * (m_hat / (_jnp.sqrt(v_hat) + ADAM_EPS) + ADAM_WD * w)
    return delta, m, v


def reference(x, mem, positions, mem_norm, mix_norm, xa_w_kv, xa_q_norm, xa_k_norm, ffn_norm, ffn_w_up, ffn_conv_w, ffn_conv_b, ffn_w_down, a_w_in, a_w_gate2, a_b_gate, a_o_norm, a_w_out, b_w_in, b_q_norm, b_k_norm, b_w_out, c_w_in, c_conv_w, c_conv_b, c_dt_bias, c_a_log, c_d, c_norm, c_w_out, d_w_in, d_lower_bounds, d_o_norm, d_w_out, loss_target, m_mem_norm, m_mix_norm, m_xa_w_kv, m_xa_q_norm, m_xa_k_norm, m_ffn_norm, m_ffn_w_up, m_ffn_conv_w, m_ffn_conv_b, m_ffn_w_down, m_a_w_in, m_a_w_gate2, m_a_b_gate, m_a_o_norm, m_a_w_out, m_b_w_in, m_b_q_norm, m_b_k_norm, m_b_w_out, m_c_w_in, m_c_conv_w, m_c_conv_b, m_c_dt_bias, m_c_a_log, m_c_d, m_c_norm, m_c_w_out, m_d_w_in, m_d_lower_bounds, m_d_o_norm, m_d_w_out, v_mem_norm, v_mix_norm, v_xa_w_kv, v_xa_q_norm, v_xa_k_norm, v_ffn_norm, v_ffn_w_up, v_ffn_conv_w, v_ffn_conv_b, v_ffn_w_down, v_a_w_in, v_a_w_gate2, v_a_b_gate, v_a_o_norm, v_a_w_out, v_b_w_in, v_b_q_norm, v_b_k_norm, v_b_w_out, v_c_w_in, v_c_conv_w, v_c_conv_b, v_c_dt_bias, v_c_a_log, v_c_d, v_c_norm, v_c_w_out, v_d_w_in, v_d_lower_bounds, v_d_o_norm, v_d_w_out):
    given = dict(x=x, mem=mem, positions=positions, mem_norm=mem_norm, mix_norm=mix_norm, xa_w_kv=xa_w_kv, xa_q_norm=xa_q_norm, xa_k_norm=xa_k_norm, ffn_norm=ffn_norm, ffn_w_up=ffn_w_up, ffn_conv_w=ffn_conv_w, ffn_conv_b=ffn_conv_b, ffn_w_down=ffn_w_down, a_w_in=a_w_in, a_w_gate2=a_w_gate2, a_b_gate=a_b_gate, a_o_norm=a_o_norm, a_w_out=a_w_out, b_w_in=b_w_in, b_q_norm=b_q_norm, b_k_norm=b_k_norm, b_w_out=b_w_out, c_w_in=c_w_in, c_conv_w=c_conv_w, c_conv_b=c_conv_b, c_dt_bias=c_dt_bias, c_a_log=c_a_log, c_d=c_d, c_norm=c_norm, c_w_out=c_w_out, d_w_in=d_w_in, d_lower_bounds=d_lower_bounds, d_o_norm=d_o_norm, d_w_out=d_w_out, loss_target=loss_target, m_mem_norm=m_mem_norm, m_mix_norm=m_mix_norm, m_xa_w_kv=m_xa_w_kv, m_xa_q_norm=m_xa_q_norm, m_xa_k_norm=m_xa_k_norm, m_ffn_norm=m_ffn_norm, m_ffn_w_up=m_ffn_w_up, m_ffn_conv_w=m_ffn_conv_w, m_ffn_conv_b=m_ffn_conv_b, m_ffn_w_down=m_ffn_w_down, m_a_w_in=m_a_w_in, m_a_w_gate2=m_a_w_gate2, m_a_b_gate=m_a_b_gate, m_a_o_norm=m_a_o_norm, m_a_w_out=m_a_w_out, m_b_w_in=m_b_w_in, m_b_q_norm=m_b_q_norm, m_b_k_norm=m_b_k_norm, m_b_w_out=m_b_w_out, m_c_w_in=m_c_w_in, m_c_conv_w=m_c_conv_w, m_c_conv_b=m_c_conv_b, m_c_dt_bias=m_c_dt_bias, m_c_a_log=m_c_a_log, m_c_d=m_c_d, m_c_norm=m_c_norm, m_c_w_out=m_c_w_out, m_d_w_in=m_d_w_in, m_d_lower_bounds=m_d_lower_bounds, m_d_o_norm=m_d_o_norm, m_d_w_out=m_d_w_out, v_mem_norm=v_mem_norm, v_mix_norm=v_mix_norm, v_xa_w_kv=v_xa_w_kv, v_xa_q_norm=v_xa_q_norm, v_xa_k_norm=v_xa_k_norm, v_ffn_norm=v_ffn_norm, v_ffn_w_up=v_ffn_w_up, v_ffn_conv_w=v_ffn_conv_w, v_ffn_conv_b=v_ffn_conv_b, v_ffn_w_down=v_ffn_w_down, v_a_w_in=v_a_w_in, v_a_w_gate2=v_a_w_gate2, v_a_b_gate=v_a_b_gate, v_a_o_norm=v_a_o_norm, v_a_w_out=v_a_w_out, v_b_w_in=v_b_w_in, v_b_q_norm=v_b_q_norm, v_b_k_norm=v_b_k_norm, v_b_w_out=v_b_w_out, v_c_w_in=v_c_w_in, v_c_conv_w=v_c_conv_w, v_c_conv_b=v_c_conv_b, v_c_dt_bias=v_c_dt_bias, v_c_a_log=v_c_a_log, v_c_d=v_c_d, v_c_norm=v_c_norm, v_c_w_out=v_c_w_out, v_d_w_in=v_d_w_in, v_d_lower_bounds=v_d_lower_bounds, v_d_o_norm=v_d_o_norm, v_d_w_out=v_d_w_out)
    weights = {n: given[n] for n in TWIN_WEIGHTS}
    shared = {n: given[n] for n in SHARED_INPUTS}
    per_example = {n: given[n] for n in ['x', 'mem', 'positions']}
    grad_fn = _jax.value_and_grad(_loss, argnums=(0, 1))

    def one_microbatch(ex, loss_target):
        ex = dict(ex)
        diff = ex.pop(TWIN_DIFF_INPUT)
        return grad_fn(weights, diff, {**shared, **ex}, loss_target)

    if N_MICROBATCH == 1:
        loss, (grad_w, grad_x) = one_microbatch(per_example, given["loss_target"])
    else:
        def body(carry, xs):
            loss_sum, grad_sum = carry
            l_k, (gw_k, gx_k) = one_microbatch(xs[0], xs[1])
            with _jax.named_scope("update"):
                return (loss_sum + l_k, _jax.tree.map(_jnp.add, grad_sum, gw_k)), gx_k

        init = (_jnp.zeros((), _jnp.float32), _jax.tree.map(_jnp.zeros_like, weights))
        (loss, grad_w), grad_x = _jax.lax.scan(body, init, (per_example, given["loss_target"]))
    with _jax.named_scope("update"):
        delta_w, new_m, new_v = {}, {}, {}
        for n in TWIN_WEIGHTS:
            delta_w[n], new_m[n], new_v[n] = _adamw(weights[n], grad_w[n], given["m_" + n], given["v_" + n])
    return (loss, grad_x, *[grad_w[n] for n in TWIN_WEIGHTS], *[delta_w[n] for n in TWIN_WEIGHTS],
            *[new_m[n] for n in TWIN_WEIGHTS], *[new_v[n] for n in TWIN_WEIGHTS])
```

```python
import functools
import math

import jax
import jax.numpy as jnp
import numpy as np
from jax import lax
from jax.experimental import pallas as pl
from jax.experimental.pallas import tpu as pltpu

F32 = jnp.float32
MXU_DTYPE = jnp.bfloat16
VMEM_LIMIT_V7X = 56 * 1024 * 1024
LANE = 128
SUBLANE = 8

D_MODEL = 1024
N_MEM = 256
EPS = 1e-6
ROPE_THETA = 10000.0
CHUNK = 64
XA_HEADS, XA_HD = 4, 64
GLA_HEADS, GLA_DK, GLA_DV, GLA_RANK, GLA_GATE_NORM = 4, 96, 192, 16, 16.0
DIL_GROUPS = ((128, 1), (512, 4), (2048, 16))
DIL_HEADS, DIL_HD, DIL_BLOCK = 4, 128, 128
SSM_HD, SSM_HEADS, SSM_GROUPS, SSM_STATE, SSM_CONV = 64, 12, 2, 128, 4
HGRN_HEADS, HGRN_DK = 6, 128
D_FF = 2816
FFN_CONV = 3
ADAM_LR, ADAM_B1, ADAM_B2, ADAM_EPS, ADAM_WD, ADAM_STEP = 0.001, 0.9, 0.999, 1e-08, 0.01, 10

ROW_BLOCK = 256
SCAN_CHUNKS = 2
PACK_COLS = 1024
PACK_ROWS = 32

WEIGHTS = ['mem_norm', 'mix_norm', 'xa_w_kv', 'xa_q_norm', 'xa_k_norm', 'ffn_norm', 'ffn_w_up', 'ffn_conv_w',
           'ffn_conv_b', 'ffn_w_down', 'a_w_in', 'a_w_gate2', 'a_b_gate', 'a_o_norm', 'a_w_out', 'b_w_in', 'b_q_norm',
           'b_k_norm', 'b_w_out', 'c_w_in', 'c_conv_w', 'c_conv_b', 'c_dt_bias', 'c_a_log', 'c_d', 'c_norm', 'c_w_out',
           'd_w_in', 'd_lower_bounds', 'd_o_norm', 'd_w_out']
SHARD_AXIS = {'xa_w_kv': 1, 'ffn_w_up': 2, 'ffn_conv_w': 2, 'ffn_w_down': 1, 'a_w_in': 1, 'a_w_gate2': 1, 'a_w_out': 0,
              'b_w_in': 1, 'b_w_out': 1, 'c_w_in': 1, 'c_conv_w': 1, 'c_w_out': 0, 'd_w_in': 1, 'd_w_out': 0}
BIG = ['xa_w_kv', 'ffn_w_up', 'ffn_w_down', 'a_w_in', 'a_w_gate2', 'a_w_out', 'b_w_in', 'b_w_out', 'c_w_in', 'c_w_out',
       'd_w_in', 'd_w_out']
SMALL = [n for n in WEIGHTS if n not in BIG]
N_CHIPS = 4
N_DEV = 8


class _MatmulSet:
    def __init__(self, cast, precision):
        def dot(a, b, dims):
            if cast:
                a = a.astype(MXU_DTYPE)
                b = b.astype(MXU_DTYPE)
            return lax.dot_general(a, b, (dims, ((), ())), precision=precision, preferred_element_type=F32)

        @jax.custom_vjp
        def nn(a, b):
            return dot(a, b, ((1,), (0,)))

        @jax.custom_vjp
        def nt(a, b):
            return dot(a, b, ((1,), (1,)))

        @jax.custom_vjp
        def tn(a, b):
            return dot(a, b, ((0,), (0,)))

        nn.defvjp(lambda a, b: (nn(a, b), (a, b)), lambda r, g: (nt(g, r[1]), tn(r[0], g)))
        nt.defvjp(lambda a, b: (nt(a, b), (a, b)), lambda r, g: (nn(g, r[1]), tn(g, r[0])))
        tn.defvjp(lambda a, b: (tn(a, b), (a, b)), lambda r, g: (nt(r[1], g), nn(r[0], g)))
        self.nn, self.nt, self.tn = nn, nt, tn


mm = _MatmulSet(True, None)
hi = _MatmulSet(False, lax.Precision.HIGHEST)


def _sigmoid(x):
    return jax.nn.sigmoid(x)


def _silu(x):
    return x * jax.nn.sigmoid(x)


def _softplus(x):
    return jnp.maximum(x, 0.0) + jnp.log1p(jnp.exp(-jnp.abs(x)))


def _rms(x, g, n_real=None):
    n = n_real or x.shape[-1]
    ms = jnp.sum(x * x, axis=-1, keepdims=True) * (1.0 / n)
    return x * lax.rsqrt(ms + EPS) * g


@jax.custom_vjp
def _swap_halves(x):
    return pltpu.roll(x, 64, 1)


_swap_halves.defvjp(lambda x: (_swap_halves(x), None), lambda _, g: (_swap_halves(g),))


def _tile(n, cands):
    for c in cands:
        if n % c == 0:
            return c
    raise ValueError(f"no tile for {n} among {cands}")


def _cparams(sem):
    return pltpu.CompilerParams(dimension_semantics=sem, vmem_limit_bytes=VMEM_LIMIT_V7X)


def _f32(v):
    return v.astype(F32) if jnp.issubdtype(v.dtype, jnp.floating) else v


def matmul(a, b, *, ta=False, tb=False, add=None, out_dtype=F32):
    m, k = (a.shape[1], a.shape[0]) if ta else a.shape
    n = b.shape[0] if tb else b.shape[1]
    assert k == (b.shape[1] if tb else b.shape[0]), (a.shape, b.shape, ta, tb)
    tm = _tile(m, (512, 256, 128))
    tn = _tile(n, (512, 384, 256, 128))
    tk = _tile(k, (1024, 512, 256, 128))
    nk = k // tk
    dims = (((0,) if ta else (1,)), ((1,) if tb else (0,)))

    def body(*refs):
        a_ref, b_ref = refs[0], refs[1]
        add_ref = refs[2] if add is not None else None
        o_ref, acc = refs[-2], refs[-1]
        kk = pl.program_id(2)

        @pl.when(kk == 0)
        def _():
            acc[...] = jnp.zeros_like(acc)

        acc[...] += lax.dot_general(a_ref[...].astype(MXU_DTYPE), b_ref[...].astype(MXU_DTYPE), (dims, ((), ())),
                                    preferred_element_type=F32)

        @pl.when(kk == nk - 1)
        def _():
            r = acc[...]
            if add_ref is not None:
                r = r + add_ref[...].astype(F32)
            o_ref[...] = r.astype(o_ref.dtype)

    a_spec = pl.BlockSpec((tk, tm), lambda i, j, q: (q, i)) if ta else pl.BlockSpec((tm, tk), lambda i, j, q: (i, q))
    b_spec = pl.BlockSpec((tn, tk), lambda i, j, q: (j, q)) if tb else pl.BlockSpec((tk, tn), lambda i, j, q: (q, j))
    o_spec = pl.BlockSpec((tm, tn), lambda i, j, q: (i, j))
    ins, specs = [a, b], [a_spec, b_spec]
    if add is not None:
        ins.append(add)
        specs.append(o_spec)
    return pl.pallas_call(
        body, name=f"mm_{m}x{k}x{n}_{int(ta)}{int(tb)}{int(add is not None)}",
        grid=(m // tm, n // tn, nk), in_specs=specs, out_specs=o_spec,
        out_shape=jax.ShapeDtypeStruct((m, n), out_dtype),
        scratch_shapes=[pltpu.VMEM((tm, tn), F32)],
        compiler_params=_cparams(("parallel", "parallel", "arbitrary")),
    )(*ins)


def _row_spec(a, block):
    return pl.BlockSpec((block, a.shape[1]), lambda i: (i, 0))


def _whole_spec(a):
    return pl.BlockSpec(a.shape, lambda i: (0,) * a.ndim)


def tmap(name, fn, rows, params, outs, block=ROW_BLOCK):
    s = rows[0].shape[0]
    block = min(block, s)
    nr, npar = len(rows), len(params)

    def body(*refs):
        res = fn(*[_f32(r[...]) for r in refs[:nr]], *[_f32(p[...]) for p in refs[nr:nr + npar]])
        for o_ref, v in zip(refs[nr + npar:], res, strict=True):
            o_ref[...] = v.astype(o_ref.dtype)

    return pl.pallas_call(
        body, name=name, grid=(s // block,),
        in_specs=[_row_spec(a, block) for a in rows] + [_whole_spec(p) for p in params],
        out_specs=[pl.BlockSpec((block, w), lambda i: (i, 0)) for w, _ in outs],
        out_shape=[jax.ShapeDtypeStruct((s, w), dt) for w, dt in outs],
        compiler_params=_cparams(("parallel",)),
    )(*rows, *params)


def tmap_bwd(name, fn, rows, params, douts, row_grad, row_add=None, grad_dtype=None, block=ROW_BLOCK):
    s = rows[0].shape[0]
    block = min(block, s)
    grad_dtype = grad_dtype or MXU_DTYPE
    nr, npar, nd = len(rows), len(params), len(douts)
    gr = [i for i in range(nr) if row_grad[i]]
    row_add = row_add or {}
    adds = [row_add[i] for i in gr if i in row_add]

    def body(*refs):
        rv = [_f32(r[...]) for r in refs[:nr]]
        pv = [_f32(p[...]) for p in refs[nr:nr + npar]]
        dv = tuple(_f32(d[...]) for d in refs[nr + npar:nr + npar + nd])
        add_refs = list(refs[nr + npar + nd:nr + npar + nd + len(adds)])
        out_refs = refs[nr + npar + nd + len(adds):]

        def f(*diff):
            rr = list(rv)
            for n_, i_ in enumerate(gr):
                rr[i_] = diff[n_]
            return tuple(fn(*rr, *diff[len(gr):]))

        _, vjp = jax.vjp(f, *[rv[i_] for i_ in gr], *pv)
        g = vjp(dv)
        for n_, i_ in enumerate(gr):
            v = g[n_]
            if i_ in row_add:
                v = v + add_refs.pop(0)[...].astype(F32)
            out_refs[n_][...] = v.astype(out_refs[n_].dtype)
        first = pl.program_id(0) == 0
        for n_ in range(npar):
            ref = out_refs[len(gr) + n_]

            @pl.when(first)
            def _(ref=ref):
                ref[...] = jnp.zeros_like(ref)

            ref[...] += g[len(gr) + n_]

    res = pl.pallas_call(
        body, name=name, grid=(s // block,),
        in_specs=[_row_spec(a, block) for a in rows] + [_whole_spec(p) for p in params]
        + [_row_spec(d, block) for d in douts] + [_row_spec(a, block) for a in adds],
        out_specs=[_row_spec(rows[i], block) for i in gr] + [_whole_spec(p) for p in params],
        out_shape=[jax.ShapeDtypeStruct(rows[i].shape, F32 if i in row_add else grad_dtype) for i in gr]
        + [jax.ShapeDtypeStruct(p.shape, F32) for p in params],
        compiler_params=_cparams(("arbitrary",)),
    )(*rows, *params, *douts, *adds)
    return list(res[:len(gr)]), list(res[len(gr):])


def rscan(name, fn, state_shapes, rows, params, outs, block):
    s = rows[0].shape[0]
    nsteps = s // block
    nr, npar, no, ns = len(rows), len(params), len(outs), len(state_shapes)

    def body(*refs):
        out_refs = refs[nr + npar:nr + npar + no]
        sav_refs = refs[nr + npar + no:nr + npar + no + ns]
        st_refs = refs[nr + npar + no + ns:]

        @pl.when(pl.program_id(0) == 0)
        def _():
            for st in st_refs:
                st[...] = jnp.zeros_like(st)

        sts = tuple(st[...] for st in st_refs)
        for sv, v in zip(sav_refs, sts):
            sv[...] = v
        new, res = fn(sts, *[_f32(r[...]) for r in refs[:nr]], *[_f32(p[...]) for p in refs[nr:nr + npar]])
        for st, v in zip(st_refs, new, strict=True):
            st[...] = v
        for o_ref, v in zip(out_refs, res, strict=True):
            o_ref[...] = v.astype(o_ref.dtype)

    res = pl.pallas_call(
        body, name=name, grid=(nsteps,),
        in_specs=[_row_spec(a, block) for a in rows] + [_whole_spec(p) for p in params],
        out_specs=[pl.BlockSpec((block, w), lambda i: (i, 0)) for w, _ in outs]
        + [pl.BlockSpec(sh, lambda i: (i, 0)) for sh in state_shapes],
        out_shape=[jax.ShapeDtypeStruct((s, w), dt) for w, dt in outs]
        + [jax.ShapeDtypeStruct((nsteps * sh[0], sh[1]), F32) for sh in state_shapes],
        scratch_shapes=[pltpu.VMEM(sh, F32) for sh in state_shapes],
        compiler_params=_cparams(("arbitrary",)),
    )(*rows, *params)
    return list(res[:no]), list(res[no:])


def rscan_bwd(name, fn, saved, rows, params, douts, block, grad_dtype=None):
    s = rows[0].shape[0]
    nsteps = s // block
    grad_dtype = grad_dtype or MXU_DTYPE
    nr, npar, nd, ns = len(rows), len(params), len(douts), len(saved)
    state_shapes = [(sv.shape[0] // nsteps, sv.shape[1]) for sv in saved]

    def body(*refs):
        rv = [_f32(r[...]) for r in refs[:nr]]
        pv = [_f32(p[...]) for p in refs[nr:nr + npar]]
        dv = tuple(_f32(d[...]) for d in refs[nr + npar:nr + npar + nd])
        sv = tuple(x[...] for x in refs[nr + npar + nd:nr + npar + nd + ns])
        out_refs = refs[nr + npar + nd + ns:nr + npar + nd + ns + nr + npar]
        dst_refs = refs[nr + npar + nd + ns + nr + npar:]
        first = pl.program_id(0) == 0

        @pl.when(first)
        def _():
            for d in dst_refs:
                d[...] = jnp.zeros_like(d)

        def f(sts, *args):
            return fn(sts, *args)

        _, vjp = jax.vjp(f, sv, *rv, *pv)
        g = vjp((tuple(d[...] for d in dst_refs), dv))
        for d, v in zip(dst_refs, g[0], strict=True):
            d[...] = v
        for n_ in range(nr):
            out_refs[n_][...] = g[1 + n_].astype(out_refs[n_].dtype)
        for n_ in range(npar):
            ref = out_refs[nr + n_]

            @pl.when(first)
            def _(ref=ref):
                ref[...] = jnp.zeros_like(ref)

            ref[...] += g[1 + nr + n_]

    rev = lambda i: (nsteps - 1 - i, 0)
    res = pl.pallas_call(
        body, name=name, grid=(nsteps,),
        in_specs=[pl.BlockSpec((block, a.shape[1]), rev) for a in rows] + [_whole_spec(p) for p in params]
        + [pl.BlockSpec((block, d.shape[1]), rev) for d in douts] + [pl.BlockSpec(sh, rev) for sh in state_shapes],
        out_specs=[pl.BlockSpec((block, a.shape[1]), rev) for a in rows] + [_whole_spec(p) for p in params],
        out_shape=[jax.ShapeDtypeStruct(a.shape, grad_dtype) for a in rows]
        + [jax.ShapeDtypeStruct(p.shape, F32) for p in params],
        scratch_shapes=[pltpu.VMEM(sh, F32) for sh in state_shapes],
        compiler_params=_cparams(("arbitrary",)),
    )(*rows, *params, *douts, *saved)
    return list(res[:nr]), list(res[nr:])


def _norm_stage(x, g):
    return (_rms(x, g),)


def _tril():
    r = lax.broadcasted_iota(jnp.int32, (CHUNK, CHUNK), 0)
    c = lax.broadcasted_iota(jnp.int32, (CHUNK, CHUNK), 1)
    return r >= c


def _gla_chunk(st, q, k, v, la):
    tril = _tril()
    b = hi.nn(tril.astype(F32), la)
    rowi = lax.broadcasted_iota(jnp.int32, (CHUNK, 1), 0)
    b_last = jnp.sum(la, axis=0, keepdims=True)
    b_ref = jnp.sum(jnp.where(rowi < CHUNK // 2, la, 0.0), axis=0, keepdims=True)
    att = mm.nt(q * jnp.exp(b - b_ref), k * jnp.exp(b_ref - b))
    att = jnp.where(tril, att, 0.0)
    o = mm.nn(att, v) + mm.nn(q * jnp.exp(b), st)
    decay = jnp.exp(hi.tn(la, jnp.ones((CHUNK, v.shape[1]), F32)))
    st2 = decay * st + mm.tn(k * jnp.exp(b_last - b), v)
    return st2, o


def _gla_step(heads, kp, vp, scale):
    def fn(states, q, k, v, la):
        sts = list(states)
        rows = []
        for c in range(q.shape[0] // CHUNK):
            r = slice(c * CHUNK, (c + 1) * CHUNK)
            oh = []
            for h in range(heads):
                ks, vs = slice(h * kp, (h + 1) * kp), slice(h * vp, (h + 1) * vp)
                qh = q[r, ks] * scale if scale != 1.0 else q[r, ks]
                sts[h], o = _gla_chunk(sts[h], qh, k[r, ks], v[r, vs], la[r, ks])
                oh.append(o)
            rows.append(jnp.concatenate(oh, axis=1))
        return tuple(sts), (jnp.concatenate(rows, axis=0),)

    return fn


def _ssd_step(states, xa, dtr, dtb, alog, dsk):
    sts = list(states)
    tril = _tril()
    trif = tril.astype(F32)
    avg = jnp.full((CHUNK, LANE), 1.0 / LANE, F32)
    hg = SSM_HEADS // SSM_GROUPS
    xw = SSM_HEADS * LANE
    rows = []
    for c in range(xa.shape[0] // CHUNK):
        r = slice(c * CHUNK, (c + 1) * CHUNK)
        yh = []
        for g in range(SSM_GROUPS):
            bm = xa[r, xw + g * LANE:xw + (g + 1) * LANE]
            cm = xa[r, xw + (SSM_GROUPS + g) * LANE:xw + (SSM_GROUPS + g + 1) * LANE]
            cb = mm.nt(cm, bm)
            for hh in range(hg):
                h = g * hg + hh
                ls = slice(h * LANE, (h + 1) * LANE)
                xs = xa[r, ls]
                dt = _softplus(dtr[r, ls] + dtb[:, ls])
                a = dt * (-jnp.exp(alog[:, ls]))
                xdt = xs * dt
                acs = hi.nn(trif, a)
                acs_last = jnp.sum(a, axis=0, keepdims=True)
                seg = hi.nt(acs, avg) - hi.nt(avg, acs)
                lmat = jnp.exp(jnp.where(tril, seg, -1e30))
                y = mm.nn(cb * lmat, xdt) + mm.nn(cm, sts[h]) * jnp.exp(acs) + dsk[:, ls] * xs
                sts[h] = jnp.exp(acs_last) * sts[h] + mm.tn(bm, xdt * jnp.exp(acs_last - acs))
                yh.append(y)
        rows.append(jnp.concatenate(yh, axis=1))
    return tuple(sts), (jnp.concatenate(rows, axis=0),)


def _gla_pre(glr, w2, bg):
    z = mm.nn(glr, w2) + bg
    return (-_softplus(-z) * (1.0 / GLA_GATE_NORM),)


def _gla_post(o, og, g):
    w = 2 * LANE
    return (jnp.concatenate([_rms(o[:, h * w:(h + 1) * w], g, GLA_DV) * _silu(og[:, h * w:(h + 1) * w])
                             for h in range(GLA_HEADS)], axis=1),)


def _hgrn_pre(q, f, lbnd):
    e = jnp.exp(lbnd - jnp.max(lbnd, axis=0, keepdims=True))
    rowi = lax.broadcasted_iota(jnp.int32, e.shape, 0)
    lb = jnp.sum(jnp.where(rowi >= 1, e, 0.0), axis=0, keepdims=True) / jnp.sum(e, axis=0, keepdims=True)
    fg = lb + (1.0 - lb) * _sigmoid(f)
    return _silu(q), 1.0 - fg, jnp.log(fg)


def _hgrn_post(o, og, g):
    return (jnp.concatenate([_rms(o[:, h * LANE:(h + 1) * LANE], g) for h in range(HGRN_HEADS)], axis=1)
            * _sigmoid(og),)


def _mamba_post(y, z, g):
    v = y * _silu(z)
    w = (SSM_HEADS // SSM_GROUPS) * LANE
    n_real = (SSM_HEADS // SSM_GROUPS) * SSM_HD
    return (jnp.concatenate([_rms(v[:, i * w:(i + 1) * w], g[:, i * w:(i + 1) * w], n_real)
                             for i in range(SSM_GROUPS)], axis=1),)


def _dil_pre(q, k, cosf, sinf, qg, kg):
    def one(x, g):
        hs = []
        for h in range(DIL_HEADS):
            n = _rms(x[:, h * LANE:(h + 1) * LANE], g)
            hs.append(n * cosf + _swap_halves(n) * sinf)
        return jnp.concatenate(hs, axis=1)

    return one(q, qg), one(k, kg)


def _dil_merge(o0, o1, o2, l0, l1, l2):
    m = jnp.maximum(jnp.maximum(l0, l1), l2)
    e0, e1, e2 = jnp.exp(l0 - m), jnp.exp(l1 - m), jnp.exp(l2 - m)
    return ((e0 * o0 + e1 * o1 + e2 * o2) / (e0 + e1 + e2),)


def _dil_block(q, kp, kc, vp, vc, lim):
    kk = jnp.concatenate([kp, kc], axis=0)
    vv = jnp.concatenate([vp, vc], axis=0)
    s = mm.nt(q, kk) * (DIL_HD ** -0.5)
    i = lax.broadcasted_iota(jnp.int32, s.shape, 0)
    j = lax.broadcasted_iota(jnp.int32, s.shape, 1)
    dist = DIL_BLOCK + i - j
    s = jnp.where((dist >= 0) & (dist <= DIL_BLOCK) & (j >= lim), s, -1e30)
    m = jnp.max(s, axis=-1, keepdims=True)
    p = jnp.exp(s - m)
    l = jnp.sum(p, axis=-1, keepdims=True)
    return mm.nn(p / l, vv), jnp.broadcast_to(m + jnp.log(l), (q.shape[0], LANE))


def _xattn(xq, kv, qg, kg):
    w = XA_HEADS * LANE
    os_ = []
    for h in range(XA_HEADS):
        ls = slice(h * LANE, (h + 1) * LANE)
        q = _rms(xq[:, ls], qg, XA_HD)
        k = _rms(kv[:, ls], kg, XA_HD)
        s = mm.nt(q, k) * (XA_HD ** -0.5)
        p = jnp.exp(s - jnp.max(s, axis=-1, keepdims=True))
        p = p / jnp.sum(p, axis=-1, keepdims=True)
        os_.append(mm.nn(p, kv[:, w + h * LANE:w + (h + 1) * LANE]))
    return (jnp.concatenate(os_, axis=1),)


def dil_attn(name, q, k, v, r):
    s, w = q.shape
    l = s // r
    nb = l // DIL_BLOCK
    q2, k2, v2 = (t.reshape(l, r * w) for t in (q, k, v))

    def body(q_r, kp_r, kc_r, vp_r, vc_r, o_r, l_r):
        lim = jnp.where(pl.program_id(1) == 0, DIL_BLOCK, 0)
        for h in range(DIL_HEADS):
            ls = slice(h * LANE, (h + 1) * LANE)
            o, lse = _dil_block(q_r[:, ls], kp_r[:, ls], kc_r[:, ls], vp_r[:, ls], vc_r[:, ls], lim)
            o_r[:, ls] = o
            l_r[:, ls] = lse

    cur = pl.BlockSpec((DIL_BLOCK, w), lambda res, n: (n, res))
    prev = pl.BlockSpec((DIL_BLOCK, w), lambda res, n: (jnp.maximum(n - 1, 0), res))
    o, lse = pl.pallas_call(
        body, name=name, grid=(r, nb), in_specs=[cur, prev, cur, prev, cur], out_specs=[cur, cur],
        out_shape=[jax.ShapeDtypeStruct((l, r * w), F32)] * 2,
        compiler_params=_cparams(("parallel", "parallel")),
    )(q2, k2, k2, v2, v2)
    return o.reshape(s, w), lse.reshape(s, w)


def dil_attn_bwd(name, q, k, v, do, dlse, r):
    s, w = q.shape
    l = s // r
    nb = l // DIL_BLOCK
    q2, k2, v2, do2, dl2 = (t.reshape(l, r * w) for t in (q, k, v, do, dlse))

    def body(q_r, kp_r, kc_r, vp_r, vc_r, do_r, dl_r, dq_r, dk_r, dv_r, ck, cv):
        i = pl.program_id(1)
        lim = jnp.where(i == nb - 1, DIL_BLOCK, 0)

        @pl.when(i == 0)
        def _():
            ck[...] = jnp.zeros_like(ck)
            cv[...] = jnp.zeros_like(cv)

        for h in range(DIL_HEADS):
            ls = slice(h * LANE, (h + 1) * LANE)
            _, vjp = jax.vjp(functools.partial(_dil_block, lim=lim),
                             q_r[:, ls], kp_r[:, ls], kc_r[:, ls], vp_r[:, ls], vc_r[:, ls])
            gq, gkp, gkc, gvp, gvc = vjp((do_r[:, ls], dl_r[:, ls]))
            dq_r[:, ls] = gq.astype(dq_r.dtype)
            dk_r[:, ls] = (gkc + ck[:, ls]).astype(dk_r.dtype)
            dv_r[:, ls] = (gvc + cv[:, ls]).astype(dv_r.dtype)
            ck[:, ls] = gkp
            cv[:, ls] = gvp

    cur = pl.BlockSpec((DIL_BLOCK, w), lambda res, i: (nb - 1 - i, res))
    prev = pl.BlockSpec((DIL_BLOCK, w), lambda res, i: (jnp.maximum(nb - 2 - i, 0), res))
    dq, dk, dv = pl.pallas_call(
        body, name=name, grid=(r, nb), in_specs=[cur, prev, cur, prev, cur, cur, cur], out_specs=[cur, cur, cur],
        out_shape=[jax.ShapeDtypeStruct((l, r * w), F32), jax.ShapeDtypeStruct((l, r * w), F32),
                   jax.ShapeDtypeStruct((l, r * w), MXU_DTYPE)],
        scratch_shapes=[pltpu.VMEM((DIL_BLOCK, w), F32)] * 2,
        compiler_params=_cparams(("parallel", "arbitrary")),
    )(q2, k2, k2, v2, v2, do2, dl2)
    return dq.reshape(s, w), dk.reshape(s, w), dv.reshape(s, w)


def _dsilu(u):
    sg = _sigmoid(u)
    return sg * (1.0 + u * (1.0 - sg))


def _taps(xp, w_ref, ntap, lo, hi_):
    acc = None
    for j in range(ntap):
        sh = ntap - 1 - j
        term = w_ref[j:j + 1, :] * (pltpu.roll(xp, sh, 0) if sh else xp)[lo:hi_]
        acc = term if acc is None else acc + term
    return acc


def conv_fwd(name, x, w, b, mode, out_dtype, tc, block=ROW_BLOCK):
    s, c = x.shape
    ntap = w.shape[0]
    block = min(block, s)
    f = c // 2 if mode == 'glu' else c
    nh = 2 if mode == 'glu' else 1
    off = f // tc

    def body(*refs):
        first = pl.program_id(1) == 0
        us = []
        for hlf in range(nh):
            prev_r, cur_r, w_r, b_r = refs[4 * hlf:4 * hlf + 4]
            xp = jnp.concatenate([jnp.where(first, 0.0, prev_r[...]), cur_r[...]], axis=0)
            us.append(b_r[...] + _taps(xp, w_r, ntap, SUBLANE, SUBLANE + block))
        o_ref = refs[-1]
        o_ref[...] = (_silu(us[0]) * us[1] if mode == 'glu' else _silu(us[0])).astype(o_ref.dtype)

    rb = block // SUBLANE
    ins, specs = [], []
    for hlf in range(nh):
        o = hlf * off
        ins += [x, x, w, b]
        specs += [pl.BlockSpec((SUBLANE, tc), lambda j, i, o=o: (jnp.maximum(i * rb - 1, 0), j + o)),
                  pl.BlockSpec((block, tc), lambda j, i, o=o: (i, j + o)),
                  pl.BlockSpec((ntap, tc), lambda j, i, o=o: (0, j + o)),
                  pl.BlockSpec((1, tc), lambda j, i, o=o: (0, j + o))]
    return pl.pallas_call(
        body, name=name, grid=(f // tc, s // block), in_specs=specs,
        out_specs=pl.BlockSpec((block, tc), lambda j, i: (i, j)),
        out_shape=jax.ShapeDtypeStruct((s, f), out_dtype),
        compiler_params=_cparams(("parallel", "parallel")),
    )(*ins)


def conv_bwd(name, x, w, b, dout, mode, tc, block=ROW_BLOCK):
    s, c = x.shape
    ntap = w.shape[0]
    block = min(block, s)
    nblk = s // block
    f = c // 2 if mode == 'glu' else c
    nh = 2 if mode == 'glu' else 1
    off = f // tc
    ext = block + SUBLANE

    def body(*refs):
        i = pl.program_id(1)
        first, last = i == 0, i == nblk - 1
        dcur_r, dnext_r = refs[5 * nh], refs[5 * nh + 1]
        outs = refs[5 * nh + 2:]
        d_e = jnp.concatenate([dcur_r[...], jnp.where(last, 0.0, dnext_r[...])], axis=0)
        xps, us = [], []
        for hlf in range(nh):
            prev_r, cur_r, next_r, w_r, b_r = refs[5 * hlf:5 * hlf + 5]
            xp = jnp.concatenate([jnp.where(first, 0.0, prev_r[...]), cur_r[...], jnp.where(last, 0.0, next_r[...])],
                                 axis=0)
            xps.append(xp)
            us.append(b_r[...] + _taps(xp, w_r, ntap, SUBLANE, SUBLANE + ext))
        if mode == 'glu':
            dus = [d_e * us[1] * _dsilu(us[0]), d_e * _silu(us[0])]
        else:
            dus = [d_e * _dsilu(us[0])]
        for hlf in range(nh):
            w_r = refs[5 * hlf + 3]
            dx_r, dw_r, db_r = outs[3 * hlf:3 * hlf + 3]
            du, xp = dus[hlf], xps[hlf]

            @pl.when(first)
            def _(dw_r=dw_r, db_r=db_r):
                dw_r[...] = jnp.zeros_like(dw_r)
                db_r[...] = jnp.zeros_like(db_r)

            db_r[...] += jnp.sum(du[:block], axis=0, keepdims=True)
            dx = None
            for j in range(ntap):
                sh = ntap - 1 - j
                xs = (pltpu.roll(xp, sh, 0) if sh else xp)[SUBLANE:SUBLANE + block]
                dw_r[j:j + 1, :] += jnp.sum(du[:block] * xs, axis=0, keepdims=True)
                term = w_r[j:j + 1, :] * (pltpu.roll(du, ext - sh, 0) if sh else du)[:block]
                dx = term if dx is None else dx + term
            dx_r[...] = dx.astype(dx_r.dtype)

    rb = block // SUBLANE
    nrow8 = s // SUBLANE
    ins, specs = [], []
    for hlf in range(nh):
        o = hlf * off
        ins += [x, x, x, w, b]
        specs += [pl.BlockSpec((SUBLANE, tc), lambda j, i, o=o: (jnp.maximum(i * rb - 1, 0), j + o)),
                  pl.BlockSpec((block, tc), lambda j, i, o=o: (i, j + o)),
                  pl.BlockSpec((SUBLANE, tc), lambda j, i, o=o: (jnp.minimum((i + 1) * rb, nrow8 - 1), j + o)),
                  pl.BlockSpec((ntap, tc), lambda j, i, o=o: (0, j + o)),
                  pl.BlockSpec((1, tc), lambda j, i, o=o: (0, j + o))]
    ins += [dout, dout]
    specs += [pl.BlockSpec((block, tc), lambda j, i: (i, j)),
              pl.BlockSpec((SUBLANE, tc), lambda j, i: (jnp.minimum((i + 1) * rb, nrow8 - 1), j))]
    out_specs, out_shape = [], []
    for hlf in range(nh):
        out_specs += [pl.BlockSpec((block, tc), lambda j, i: (i, j)), pl.BlockSpec((ntap, tc), lambda j, i: (0, j)),
                      pl.BlockSpec((1, tc), lambda j, i: (0, j))]
        out_shape += [jax.ShapeDtypeStruct((s, f), MXU_DTYPE), jax.ShapeDtypeStruct((ntap, f), F32),
                      jax.ShapeDtypeStruct((1, f), F32)]
    res = pl.pallas_call(
        body, name=name, grid=(f // tc, nblk), in_specs=specs, out_specs=out_specs, out_shape=out_shape,
        compiler_params=_cparams(("parallel", "arbitrary")),
    )(*ins)
    if nh == 1:
        return res[0], res[1], res[2]
    return (jnp.concatenate([res[0], res[3]], axis=1), jnp.concatenate([res[1], res[4]], axis=1),
            jnp.concatenate([res[2], res[5]], axis=1))


def loss_head(y, target, block=ROW_BLOCK):
    s, d = y.shape
    block = min(block, s)

    def body(y_r, t_r, acc_r, dy_r):
        e = y_r[...] - t_r[...]
        dy_r[...] = e * (1.0 / d)

        @pl.when(pl.program_id(0) == 0)
        def _():
            acc_r[...] = jnp.zeros_like(acc_r)

        acc_r[...] += jnp.sum((e * e).reshape(block // SUBLANE, SUBLANE, d), axis=0) * (0.5 / d)

    return pl.pallas_call(
        body, name="loss_head", grid=(s // block,),
        in_specs=[pl.BlockSpec((block, d), lambda i: (i, 0))] * 2,
        out_specs=[pl.BlockSpec((SUBLANE, d), lambda i: (0, 0)), pl.BlockSpec((block, d), lambda i: (i, 0))],
        out_shape=[jax.ShapeDtypeStruct((SUBLANE, d), F32), jax.ShapeDtypeStruct((s, d), F32)],
        compiler_params=_cparams(("arbitrary",)),
    )(y, target)


def adamw(name, w, g, m, v):
    r, c = w.shape
    tr = r if r <= 512 else _tile(r, (512, 256, 128, 64, 32, 16, 8))
    if c * tr * 4 > (1 << 21):
        tr = _tile(r, (256, 128, 64, 32, 16, 8))

    def body(w_r, g_r, m_r, v_r, d_r, nm_r, nv_r):
        gg = g_r[...]
        nm = ADAM_B1 * m_r[...] + (1.0 - ADAM_B1) * gg
        nv = ADAM_B2 * v_r[...] + (1.0 - ADAM_B2) * (gg * gg)
        m_hat = nm / (1.0 - ADAM_B1 ** ADAM_STEP)
        v_hat = nv / (1.0 - ADAM_B2 ** ADAM_STEP)
        d_r[...] = -ADAM_LR * (m_hat / (jnp.sqrt(v_hat) + ADAM_EPS) + ADAM_WD * w_r[...])
        nm_r[...] = nm
        nv_r[...] = nv

    spec = pl.BlockSpec((tr, c), lambda i: (i, 0))
    return pl.pallas_call(
        body, name=name, grid=(r // tr,), in_specs=[spec] * 4, out_specs=[spec] * 3,
        out_shape=[jax.ShapeDtypeStruct((r, c), F32)] * 3, compiler_params=_cparams(("parallel",)),
    )(w, g, m, v)


MESH = pl.DeviceIdType.MESH
_ANY = pl.BlockSpec(memory_space=pl.ANY)


def _place():
    return lax.axis_index("x"), lax.axis_index("y"), lax.axis_index("c")


def allgather_chips(shard):
    r, c = shard.shape
    hr = r // 2

    def body(w_ref, out_ref, send_sems, recv_sems, local_sem):
        x, y, cc = _place()
        sibling = (x, y, 1 - cc)
        chips = [(1 - x, y), (x, 1 - y), (1 - x, 1 - y)]

        def half(chip, core):
            return out_ref.at[2 * chip[0] + chip[1], pl.ds(core * hr, hr), :]

        def copy(k, chip, core, to, src=None):
            return pltpu.make_async_remote_copy(
                src_ref=half(chip, core) if src is None else src, dst_ref=half(chip, core),
                send_sem=send_sems.at[k], recv_sem=recv_sems.at[k], device_id=to, device_id_type=MESH)

        mine = pltpu.make_async_copy(w_ref, out_ref.at[2 * x + y], local_sem)
        mine.start()
        my_half = w_ref.at[pl.ds(cc * hr, hr), :]
        first = [copy(j, (x, y), cc, (*chip, cc), src=my_half) for j, chip in enumerate(chips)]
        for cp in first:
            cp.start()
        passed = [copy(3 + j, chip, cc, sibling) for j, chip in enumerate(chips)]
        for j, chip in enumerate(chips):
            copy(j, chip, cc, (x, y, cc)).wait_recv()
            passed[j].start()
        for j, chip in enumerate(chips):
            copy(3 + j, chip, 1 - cc, (x, y, cc)).wait_recv()
        for cp in first + passed:
            cp.wait_send()
        mine.wait()

    return pl.pallas_call(
        body, name="allgather_chips", in_specs=[_ANY], out_specs=_ANY,
        out_shape=jax.ShapeDtypeStruct((N_CHIPS, r, c), shard.dtype),
        scratch_shapes=[pltpu.SemaphoreType.DMA((6,)), pltpu.SemaphoreType.DMA((6,)), pltpu.SemaphoreType.DMA],
    )(shard)


def allgather_devices(buf):
    r, c = buf.shape

    def body(b_ref, out_ref, send_sems, recv_sems, local_sem):
        x, y, cc = _place()
        me = 4 * x + 2 * y + cc
        mine = pltpu.make_async_copy(b_ref, out_ref.at[me], local_sem)
        mine.start()
        copies = []
        for k in range(1, N_DEV):
            px, py, pc = x ^ (k >> 2), y ^ ((k >> 1) & 1), cc ^ (k & 1)
            cp = pltpu.make_async_remote_copy(src_ref=b_ref, dst_ref=out_ref.at[me], send_sem=send_sems.at[k - 1],
                                              recv_sem=recv_sems.at[k - 1], device_id=(px, py, pc), device_id_type=MESH)
            cp.start()
            copies.append((cp, 4 * px + 2 * py + pc))
        for k, (cp, peer) in enumerate(copies):
            pltpu.make_async_remote_copy(src_ref=b_ref, dst_ref=out_ref.at[peer], send_sem=send_sems.at[k],
                                         recv_sem=recv_sems.at[k], device_id=(x, y, cc), device_id_type=MESH).wait_recv()
        for cp, _ in copies:
            cp.wait_send()
        mine.wait()

    return pl.pallas_call(
        body, name="allgather_devices", in_specs=[_ANY], out_specs=_ANY,
        out_shape=jax.ShapeDtypeStruct((N_DEV, r, c), buf.dtype),
        scratch_shapes=[pltpu.SemaphoreType.DMA((N_DEV - 1,)), pltpu.SemaphoreType.DMA((N_DEV - 1,)),
                        pltpu.SemaphoreType.DMA],
    )(buf)


def swap_halves_sibling(g):
    n, _, r, c = g.shape

    def body(g_ref, out_ref, send_sem, recv_sem):
        x, y, cc = _place()
        cp = pltpu.make_async_remote_copy(src_ref=g_ref.at[:, 1 - cc], dst_ref=out_ref, send_sem=send_sem,
                                          recv_sem=recv_sem, device_id=(x, y, 1 - cc), device_id_type=MESH)
        cp.start()
        cp.wait()

    return pl.pallas_call(
        body, name="swap_halves_sibling", in_specs=[_ANY], out_specs=_ANY,
        out_shape=jax.ShapeDtypeStruct((n, r, c), g.dtype),
        scratch_shapes=[pltpu.SemaphoreType.DMA, pltpu.SemaphoreType.DMA],
    )(g)


def exchange_chips(p):
    n, r, c = p.shape

    def body(p_ref, out_ref, send_sems, recv_sems, local_sem):
        x, y, cc = _place()
        me = 2 * x + y
        mine = pltpu.make_async_copy(p_ref.at[me], out_ref.at[me], local_sem)
        mine.start()
        chips = [(1 - x, y), (x, 1 - y), (1 - x, 1 - y)]
        cps = []
        for j, chip in enumerate(chips):
            cp = pltpu.make_async_remote_copy(src_ref=p_ref.at[2 * chip[0] + chip[1]], dst_ref=out_ref.at[me],
                                              send_sem=send_sems.at[j], recv_sem=recv_sems.at[j],
                                              device_id=(*chip, cc), device_id_type=MESH)
            cp.start()
            cps.append(cp)
        for j, chip in enumerate(chips):
            pltpu.make_async_remote_copy(src_ref=p_ref.at[me], dst_ref=out_ref.at[2 * chip[0] + chip[1]],
                                         send_sem=send_sems.at[j], recv_sem=recv_sems.at[j],
                                         device_id=(x, y, cc), device_id_type=MESH).wait_recv()
        for cp in cps:
            cp.wait_send()
        mine.wait()

    return pl.pallas_call(
        body, name="exchange_chips", in_specs=[_ANY], out_specs=_ANY,
        out_shape=jax.ShapeDtypeStruct((n, r, c), p.dtype),
        scratch_shapes=[pltpu.SemaphoreType.DMA((3,)), pltpu.SemaphoreType.DMA((3,)), pltpu.SemaphoreType.DMA],
    )(p)


def join_halves_sibling(h):
    r, c = h.shape

    def body(h_ref, out_ref, send_sem, recv_sem, local_sem):
        x, y, cc = _place()
        mine = pltpu.make_async_copy(h_ref, out_ref.at[cc], local_sem)
        mine.start()
        cp = pltpu.make_async_remote_copy(src_ref=h_ref, dst_ref=out_ref.at[cc], send_sem=send_sem, recv_sem=recv_sem,
                                          device_id=(x, y, 1 - cc), device_id_type=MESH)
        cp.start()
        pltpu.make_async_remote_copy(src_ref=h_ref, dst_ref=out_ref.at[1 - cc], send_sem=send_sem, recv_sem=recv_sem,
                                     device_id=(x, y, cc), device_id_type=MESH).wait_recv()
        cp.wait_send()
        mine.wait()

    return pl.pallas_call(
        body, name="join_halves_sibling", in_specs=[_ANY], out_specs=_ANY,
        out_shape=jax.ShapeDtypeStruct((2, r, c), h.dtype),
        scratch_shapes=[pltpu.SemaphoreType.DMA, pltpu.SemaphoreType.DMA, pltpu.SemaphoreType.DMA],
    )(h)


def add_own_half(g, got, core):
    n, _, r, c = g.shape
    tr = _tile(r, (256, 128, 64, 32, 16, 8))

    def body(c_ref, g_r, o_r, out_r):
        out_r[...] = g_r[...] + o_r[...]

    return pl.pallas_call(
        body, name="add_own_half",
        grid_spec=pltpu.PrefetchScalarGridSpec(
            num_scalar_prefetch=1, grid=(n, r // tr),
            in_specs=[pl.BlockSpec((None, None, tr, c), lambda i, j, c_ref: (i, c_ref[0], j, 0)),
                      pl.BlockSpec((None, tr, c), lambda i, j, c_ref: (i, j, 0))],
            out_specs=pl.BlockSpec((None, tr, c), lambda i, j, c_ref: (i, j, 0))),
        out_shape=jax.ShapeDtypeStruct((n, r, c), F32),
        compiler_params=_cparams(("parallel", "parallel")),
    )(core, g, got)


def sum_slabs(p, name):
    n, r, c = p.shape
    tr = _tile(r, (256, 128, 64, 32, 16, 8))

    def body(p_r, out_r):
        acc = p_r[0]
        for k in range(1, n):
            acc = acc + p_r[k]
        out_r[...] = acc

    return pl.pallas_call(
        body, name=name, grid=(r // tr,), in_specs=[pl.BlockSpec((n, tr, c), lambda i: (0, i, 0))],
        out_specs=pl.BlockSpec((tr, c), lambda i: (i, 0)), out_shape=jax.ShapeDtypeStruct((r, c), F32),
        compiler_params=_cparams(("parallel",)),
    )(p)


def _lay(arr, axis, pieces, total, reps=()):
    items = [(d, n, lax.slice_in_dim(arr, s0, s0 + n, axis=axis)) for s0, n, d in pieces]
    items += [(d, n, jnp.repeat(lax.slice_in_dim(arr, s0, s0 + 1, axis=axis), n, axis=axis)) for s0, d, n in reps]
    items.sort(key=lambda t: t[0])
    parts, pos = [], 0

    def zeros(n):
        sh = list(arr.shape)
        sh[axis] = n
        return jnp.zeros(sh, arr.dtype)

    for d, n, v in items:
        if d > pos:
            parts.append(zeros(d - pos))
        parts.append(v)
        pos = d + n
    if total > pos:
        parts.append(zeros(total - pos))
    return jnp.concatenate(parts, axis=axis) if len(parts) > 1 else parts[0]


def _unlay_parts(g, axis, pieces, reps=()):
    out = [(s0, lax.slice_in_dim(g, d, d + n, axis=axis)) for s0, n, d in pieces]
    out += [(s0, jnp.sum(lax.slice_in_dim(g, d, d + n, axis=axis), axis=axis, keepdims=True)) for s0, d, n in reps]
    return out


def _join(parts, axis):
    parts = sorted(parts, key=lambda t: t[0])
    return jnp.concatenate([p for _, p in parts], axis=axis)


def _heads(src0, n_heads, width, padded, dst0=0):
    return [(src0 + h * width, width, dst0 + h * padded) for h in range(n_heads)]


_XQ = lambda src0: _heads(src0, XA_HEADS, XA_HD, LANE)
_XA_W = XA_HEADS * LANE

LAYOUT = {
    'a': dict(
        segs=dict(q=(_heads(0, 4, 96, LANE), 512, ()), k=(_heads(384, 4, 96, LANE), 512, ()),
                  v=(_heads(768, 4, 192, 256), 1024, ()), glr=([(1536, 16, 0)], LANE, ()),
                  og=(_heads(1552, 4, 192, 256), 1024, ()), xq=(_XQ(2320), _XA_W, ())),
        tok=(_heads(0, 4, 192, 256), 1024), xa=(_XQ(768), _XA_W)),
    'b': dict(
        segs=dict(**{f"q{g}": ([(512 * g, 512, 0)], 512, ()) for g in range(3)},
                  **{f"k{g}": ([(1536 + 512 * g, 512, 0)], 512, ()) for g in range(3)},
                  **{f"v{g}": ([(3072 + 512 * g, 512, 0)], 512, ()) for g in range(3)},
                  xq=(_XQ(4608), _XA_W, ())),
        tok=([(0, 512, 0)], 512), xa=(_XQ(512), _XA_W)),
    'c': dict(
        segs=dict(z=(_heads(0, 12, 64, LANE), 1536, ()),
                  xbc=(_heads(768, 12, 64, LANE) + [(1536, 256, 1536), (1792, 256, 1792)], 2048, ()),
                  dt=([], 1536, tuple((2048 + h, h * LANE, LANE) for h in range(12))),
                  xq=(_XQ(2060), _XA_W, ())),
        tok=(_heads(0, 12, 64, LANE), 1536), xa=(_XQ(768), _XA_W)),
    'd': dict(
        segs=dict(q=([(0, 768, 0)], 768, ()), f=([(768, 768, 0)], 768, ()), i=([(1536, 768, 0)], 768, ()),
                  og=([(2304, 768, 0)], 768, ()), xq=(_XQ(3072), _XA_W, ())),
        tok=([(0, 768, 0)], 768), xa=(_XQ(768), _XA_W)),
}
KINDS = 'abcd'
_XS_PIECES = _heads(0, 12, 64, LANE)
_XBC_PIECES = _XS_PIECES + [(768, 256, 1536), (1024, 256, 1792)]
_HEAD_REPS = tuple((h, h * LANE, LANE) for h in range(12))


def _row(v):
    return v.reshape(1, -1)


def local_step(x, mem, positions, target, W):
    s = x.shape[0]
    grads = {}
    scan_block = CHUNK * SCAN_CHUNKS

    inv_freq = ROPE_THETA ** (-jnp.arange(DIL_HD // 2, dtype=F32) / (DIL_HD // 2))
    ang = positions.astype(F32)[:, None] * inv_freq
    cosf = jnp.concatenate([jnp.cos(ang), jnp.cos(ang)], axis=-1)
    sinf = jnp.concatenate([-jnp.sin(ang), jnp.sin(ang)], axis=-1)

    mem_g = _row(W['mem_norm'])
    (mem_n,) = tmap("mem_norm", _norm_stage, [mem], [mem_g], [(D_MODEL, MXU_DTYPE)])
    kv_lay = _heads(0, 4, 64, LANE) + _heads(256, 4, 64, LANE, dst0=_XA_W)

    saved = []
    for i in range(4):
        kind = KINDS[i]
        lay = LAYOUT[kind]
        sv = dict(x0=x)
        w_in = W[f'{kind}_w_in']
        w_out = W[f'{kind}_w_out']
        sv['w_seg'] = {n: _lay(w_in, 1, p, t, r).astype(MXU_DTYPE) for n, (p, t, r) in lay['segs'].items()}
        sv['wo_tok'] = _lay(w_out, 0, *lay['tok']).astype(MXU_DTYPE)
        sv['wo_xa'] = _lay(w_out, 0, *lay['xa']).astype(MXU_DTYPE)
        sv['w_kv'] = _lay(W['xa_w_kv'][i], 1, kv_lay, 2 * _XA_W).astype(MXU_DTYPE)
        sv['g1'] = _row(W['mix_norm'][i])
        (h,) = tmap(f"mix_norm_{i}", _norm_stage, [x], [sv['g1']], [(D_MODEL, MXU_DTYPE)])
        sv['h'] = h
        seg = {n: matmul(h, w) for n, w in sv['w_seg'].items()}
        sv['seg'] = seg

        if kind == 'a':
            sv['w2'] = _lay(_lay(W['a_w_gate2'], 1, _heads(0, 4, 96, LANE), 512), 0, [(0, 16, 0)], LANE)
            sv['bg'] = _row(_lay(W['a_b_gate'], 0, _heads(0, 4, 96, LANE), 512))
            sv['on'] = _row(_lay(W['a_o_norm'], 0, [(0, 192, 0)], 256))
            (la,) = tmap("gla_pre", _gla_pre, [seg['glr']], [sv['w2'], sv['bg']], [(512, F32)])
            sv['la'] = la
            sv['scan_fn'] = _gla_step(GLA_HEADS, LANE, 2 * LANE, GLA_DK ** -0.5)
            sv['scan_rows'] = [seg['q'], seg['k'], seg['v'], la]
            (o,), sv['states'] = rscan("gla_scan", sv['scan_fn'], [(LANE, 2 * LANE)] * GLA_HEADS, sv['scan_rows'], [],
                                       [(1024, F32)], scan_block)
            sv['o'] = o
            (tok,) = tmap("gla_post", _gla_post, [o, seg['og']], [sv['on']], [(1024, MXU_DTYPE)])
        elif kind == 'b':
            sv['qg'], sv['kg'] = _row(W['b_q_norm']), _row(W['b_k_norm'])
            sv['qn'], sv['kn'], os_, ls_ = [], [], [], []
            for g, (window, r) in enumerate(DIL_GROUPS):
                assert window // r == DIL_BLOCK and (s // r) % DIL_BLOCK == 0
                qn, kn = tmap(f"dil_pre_{g}", _dil_pre, [seg[f'q{g}'], seg[f'k{g}'], cosf, sinf], [sv['qg'], sv['kg']],
                              [(512, F32), (512, F32)])
                o, lse = dil_attn(f"dil_attn_{g}", qn, kn, seg[f'v{g}'], r)
                sv['qn'].append(qn)
                sv['kn'].append(kn)
                os_.append(o)
                ls_.append(lse)
            sv['os'], sv['ls'] = os_, ls_
            (tok,) = tmap("dil_merge", _dil_merge, os_ + ls_, [], [(512, MXU_DTYPE)])
        elif kind == 'c':
            sv['cw'] = _lay(W['c_conv_w'], 1, _XBC_PIECES, 2048)
            sv['cb'] = _row(_lay(W['c_conv_b'], 0, _XBC_PIECES, 2048))
            sv['dtb'] = _row(_lay(W['c_dt_bias'], 0, [], 1536, _HEAD_REPS))
            sv['alog'] = _row(_lay(W['c_a_log'], 0, [], 1536, _HEAD_REPS))
            sv['dsk'] = _row(_lay(W['c_d'], 0, [], 1536, _HEAD_REPS))
            sv['cn'] = _row(_lay(W['c_norm'], 0, _XS_PIECES, 1536))
            xact = conv_fwd("ssm_conv", seg['xbc'], sv['cw'], sv['cb'], 'silu', F32, 512)
            sv['xact'] = xact
            sv['scan_rows'] = [xact, seg['dt']]
            sv['scan_params'] = [sv['dtb'], sv['alog'], sv['dsk']]
            (yv,), sv['states'] = rscan("ssd_scan", _ssd_step, [(LANE, LANE)] * SSM_HEADS, sv['scan_rows'],
                                        sv['scan_params'], [(1536, F32)], scan_block)
            sv['y'] = yv
            (tok,) = tmap("ssd_post", _mamba_post, [yv, seg['z']], [sv['cn']], [(1536, MXU_DTYPE)])
        else:
            sv['lbnd'] = W['d_lower_bounds']
            sv['on'] = _row(W['d_o_norm'])
            qq, kk, la = tmap("hgrn_pre", _hgrn_pre, [seg['q'], seg['f']], [sv['lbnd']], [(768, F32)] * 3)
            sv['scan_fn'] = _gla_step(HGRN_HEADS, LANE, LANE, 1.0)
            sv['scan_rows'] = [qq, kk, seg['i'], la]
            (o,), sv['states'] = rscan("hgrn_scan", sv['scan_fn'], [(LANE, LANE)] * HGRN_HEADS, sv['scan_rows'], [],
                                       [(768, F32)], scan_block)
            sv['o'] = o
            (tok,) = tmap("hgrn_post", _hgrn_post, [o, seg['og']], [sv['on']], [(768, MXU_DTYPE)])
        sv['tok'] = tok

        kv = matmul(mem_n, sv['w_kv'])
        sv['kv'] = kv
        sv['xqg'] = _row(_lay(W['xa_q_norm'][i], 0, [(0, 64, 0)], LANE))
        sv['xkg'] = _row(_lay(W['xa_k_norm'][i], 0, [(0, 64, 0)], LANE))
        (xa,) = tmap(f"xattn_{i}", _xattn, [seg['xq']], [kv, sv['xqg'], sv['xkg']], [(_XA_W, MXU_DTYPE)])
        sv['xa'] = xa
        x = matmul(tok, sv['wo_tok'], add=x)
        x = matmul(xa, sv['wo_xa'], add=x)
        sv['x1'] = x

        sv['g2'] = _row(W['ffn_norm'][i])
        sv['w_up'] = W['ffn_w_up'][i].astype(MXU_DTYPE)
        sv['w_down'] = W['ffn_w_down'][i].astype(MXU_DTYPE)
        sv['fcw'] = W['ffn_conv_w'][i]
        sv['fcb'] = _row(W['ffn_conv_b'][i])
        (h2,) = tmap(f"ffn_norm_{i}", _norm_stage, [x], [sv['g2']], [(D_MODEL, MXU_DTYPE)])
        sv['h2'] = h2
        u0 = matmul(h2, sv['w_up'])
        sv['u0'] = u0
        act = conv_fwd("ffn_conv", u0, sv['fcw'], sv['fcb'], 'glu', MXU_DTYPE, 1408)
        sv['act'] = act
        x = matmul(act, sv['w_down'], add=x)
        saved.append(sv)

    loss_acc, dx = loss_head(x, target)

    g_stack = {n: [None] * 4 for n in ('mix_norm', 'xa_w_kv', 'xa_q_norm', 'xa_k_norm', 'ffn_norm', 'ffn_w_up',
                                        'ffn_conv_w', 'ffn_conv_b', 'ffn_w_down')}
    d_memn = None
    for i in reversed(range(4)):
        kind = KINDS[i]
        lay = LAYOUT[kind]
        sv = saved[i]
        seg = sv['seg']
        dact = matmul(dx, sv['w_down'], tb=True)
        g_stack['ffn_w_down'][i] = matmul(sv['act'], dx, ta=True)
        du0, dcw, dcb = conv_bwd("ffn_conv_bwd", sv['u0'], sv['fcw'], sv['fcb'], dact, 'glu', 1408)
        g_stack['ffn_conv_w'][i], g_stack['ffn_conv_b'][i] = dcw, dcb[0]
        dh2 = matmul(du0, sv['w_up'], tb=True)
        g_stack['ffn_w_up'][i] = matmul(sv['h2'], du0, ta=True)
        (dx,), (dg2,) = tmap_bwd(f"ffn_norm_bwd_{i}", _norm_stage, [sv['x1']], [sv['g2']], [dh2], [True], {0: dx})
        g_stack['ffn_norm'][i] = dg2[0]
        dtok = matmul(dx, sv['wo_tok'], tb=True)
        dxa = matmul(dx, sv['wo_xa'], tb=True)
        g_wo = _unlay_parts(matmul(sv['tok'], dx, ta=True), 0, lay['tok'][0]) \
            + _unlay_parts(matmul(sv['xa'], dx, ta=True), 0, lay['xa'][0])
        grads[f'{kind}_w_out'] = _join(g_wo, 0)
        (dxq,), (dkv, dqg, dkg) = tmap_bwd(f"xattn_bwd_{i}", _xattn, [seg['xq']], [sv['kv'], sv['xqg'], sv['xkg']],
                                           [dxa], [True])
        g_stack['xa_q_norm'][i], g_stack['xa_k_norm'][i] = dqg[0, :XA_HD], dkg[0, :XA_HD]
        g_stack['xa_w_kv'][i] = _join(_unlay_parts(matmul(mem_n, dkv, ta=True), 1, kv_lay), 1)
        d_memn = matmul(dkv, sv['w_kv'], tb=True, add=d_memn)
        dseg = dict(xq=dxq)
        if kind == 'a':
            (do, dog), (don,) = tmap_bwd("gla_post_bwd", _gla_post, [sv['o'], seg['og']], [sv['on']], [dtok],
                                         [True, True], grad_dtype=F32)
            grads['a_o_norm'] = don[0, :GLA_DV]
            (dq, dk, dv, dla), _ = rscan_bwd("gla_scan_bwd", sv['scan_fn'], sv['states'], sv['scan_rows'], [], [do],
                                             scan_block, grad_dtype=F32)
            (dglr,), (dw2, dbg) = tmap_bwd("gla_pre_bwd", _gla_pre, [seg['glr']], [sv['w2'], sv['bg']], [dla], [True])
            grads['a_w_gate2'] = _join(_unlay_parts(dw2[:GLA_RANK], 1, _heads(0, 4, 96, LANE)), 1)
            grads['a_b_gate'] = _join(_unlay_parts(dbg[0], 0, _heads(0, 4, 96, LANE)), 0)
            dseg.update(q=dq, k=dk, v=dv, glr=dglr, og=dog)
        elif kind == 'b':
            res, _ = tmap_bwd("dil_merge_bwd", _dil_merge, sv['os'] + sv['ls'], [], [dtok], [True] * 6, grad_dtype=F32)
            dqg = dkg = 0.0
            for g, (_, r) in enumerate(DIL_GROUPS):
                dqn, dkn, dv = dil_attn_bwd(f"dil_attn_bwd_{g}", sv['qn'][g], sv['kn'][g], seg[f'v{g}'], res[g],
                                            res[3 + g], r)
                (dq, dk), (a_, b_) = tmap_bwd(f"dil_pre_bwd_{g}", _dil_pre, [seg[f'q{g}'], seg[f'k{g}'], cosf, sinf],
                                              [sv['qg'], sv['kg']], [dqn, dkn], [True, True, False, False])
                dqg, dkg = dqg + a_, dkg + b_
                dseg.update({f'q{g}': dq, f'k{g}': dk, f'v{g}': dv})
            grads['b_q_norm'], grads['b_k_norm'] = dqg[0], dkg[0]
        elif kind == 'c':
            (dy, dz), (dcn,) = tmap_bwd("ssd_post_bwd", _mamba_post, [sv['y'], seg['z']], [sv['cn']], [dtok],
                                        [True, True], grad_dtype=F32)
            grads['c_norm'] = _join(_unlay_parts(dcn[0], 0, _XS_PIECES), 0)
            (dxact, ddt), (ddtb, dalog, ddsk) = rscan_bwd("ssd_scan_bwd", _ssd_step, sv['states'], sv['scan_rows'],
                                                          sv['scan_params'], [dy], scan_block, grad_dtype=F32)
            for nm, gv in (('c_dt_bias', ddtb), ('c_a_log', dalog), ('c_d', ddsk)):
                grads[nm] = _join(_unlay_parts(gv[0], 0, [], _HEAD_REPS), 0)
            dxbc, dcw, dcb = conv_bwd("ssm_conv_bwd", seg['xbc'], sv['cw'], sv['cb'], dxact, 'silu', 512)
            grads['c_conv_w'] = _join(_unlay_parts(dcw, 1, _XBC_PIECES), 1)
            grads['c_conv_b'] = _join(_unlay_parts(dcb[0], 0, _XBC_PIECES), 0)
            dseg.update(z=dz, xbc=dxbc, dt=ddt)
        else:
            (do, dog), (don,) = tmap_bwd("hgrn_post_bwd", _hgrn_post, [sv['o'], seg['og']], [sv['on']], [dtok],
                                         [True, True], grad_dtype=F32)
            grads['d_o_norm'] = don[0]
            (dqq, dkk, di, dla), _ = rscan_bwd("hgrn_scan_bwd", sv['scan_fn'], sv['states'], sv['scan_rows'], [], [do],
                                               scan_block, grad_dtype=F32)
            (dq, df), (dlb,) = tmap_bwd("hgrn_pre_bwd", _hgrn_pre, [seg['q'], seg['f']], [sv['lbnd']], [dqq, dkk, dla],
                                        [True, True])
            grads['d_lower_bounds'] = dlb
            dseg.update(q=dq, f=df, i=di, og=dog)
        dh = None
        g_in = []
        for n, (p, t, rp) in lay['segs'].items():
            dh = matmul(dseg[n], sv['w_seg'][n], tb=True, add=dh)
            g_in += _unlay_parts(matmul(sv['h'], dseg[n], ta=True), 1, p, rp)
        grads[f'{kind}_w_in'] = _join(g_in, 1)
        (dx,), (dg1,) = tmap_bwd(f"mix_norm_bwd_{i}", _norm_stage, [sv['x0']], [sv['g1']], [dh], [True], {0: dx})
        g_stack['mix_norm'][i] = dg1[0]

    _, (dmg,) = tmap_bwd("mem_norm_bwd", _norm_stage, [mem], [mem_g], [d_memn], [False])
    grads['mem_norm'] = dmg[0]
    for n, parts in g_stack.items():
        grads[n] = jnp.stack(parts)
    return loss_acc, dx, grads


def _pack(arrs, dtype):
    parts = []
    for a in arrs:
        f = a.reshape(-1).astype(dtype)
        unit = PACK_ROWS * PACK_COLS
        pad = (-f.shape[0]) % unit
        if pad:
            f = jnp.concatenate([f, jnp.zeros((pad,), dtype)])
        parts.append(f.reshape(-1, PACK_COLS))
    return jnp.concatenate(parts, axis=0)


def _unpack(buf, shapes):
    out, row = [], 0
    for sh in shapes:
        n = int(np.prod(sh))
        rows = -(-n // (PACK_ROWS * PACK_COLS)) * PACK_ROWS
        out.append(buf[row:row + rows].reshape(-1)[:n].reshape(sh))
        row += rows
    return out


def _shard_shape(full, axis):
    sh = list(full)
    sh[axis] //= N_CHIPS
    return tuple(sh)


def _split_chips(a, axis):
    sh = a.shape
    return jnp.moveaxis(a.reshape(sh[:axis] + (N_CHIPS, sh[axis] // N_CHIPS) + sh[axis + 1:]), axis, 0)


def _merge_chips(a, axis):
    a = jnp.moveaxis(a, 0, axis)
    sh = a.shape
    return a.reshape(sh[:axis] + (sh[axis] * sh[axis + 1],) + sh[axis + 2:])


def kernel(x, mem, positions, mem_norm, mix_norm, xa_w_kv, xa_q_norm, xa_k_norm, ffn_norm, ffn_w_up, ffn_conv_w, ffn_conv_b, ffn_w_down, a_w_in, a_w_gate2, a_b_gate, a_o_norm, a_w_out, b_w_in, b_q_norm, b_k_norm, b_w_out, c_w_in, c_conv_w, c_conv_b, c_dt_bias, c_a_log, c_d, c_norm, c_w_out, d_w_in, d_lower_bounds, d_o_norm, d_w_out, loss_target, m_mem_norm, m_mix_norm, m_xa_w_kv, m_xa_q_norm, m_xa_k_norm, m_ffn_norm, m_ffn_w_up, m_ffn_conv_w, m_ffn_conv_b, m_ffn_w_down, m_a_w_in, m_a_w_gate2, m_a_b_gate, m_a_o_norm, m_a_w_out, m_b_w_in, m_b_q_norm, m_b_k_norm, m_b_w_out, m_c_w_in, m_c_conv_w, m_c_conv_b, m_c_dt_bias, m_c_a_log, m_c_d, m_c_norm, m_c_w_out, m_d_w_in, m_d_lower_bounds, m_d_o_norm, m_d_w_out, v_mem_norm, v_mix_norm, v_xa_w_kv, v_xa_q_norm, v_xa_k_norm, v_ffn_norm, v_ffn_w_up, v_ffn_conv_w, v_ffn_conv_b, v_ffn_w_down, v_a_w_in, v_a_w_gate2, v_a_b_gate, v_a_o_norm, v_a_w_out, v_b_w_in, v_b_q_norm, v_b_k_norm, v_b_w_out, v_c_w_in, v_c_conv_w, v_c_conv_b, v_c_dt_bias, v_c_a_log, v_c_d, v_c_norm, v_c_w_out, v_d_w_in, v_d_lower_bounds, v_d_o_norm, v_d_w_out):
    args = locals()
    w = {n: args[n] for n in WEIGHTS}
    m = {n: args['m_' + n] for n in WEIGHTS}
    v = {n: args['v_' + n] for n in WEIGHTS}
    cx, cy, cc = lax.axis_index("x"), lax.axis_index("y"), lax.axis_index("c")
    chip = 2 * cx + cy

    big_shard_shapes = [w[n].shape for n in BIG]
    gathered = allgather_chips(_pack([w[n] for n in BIG], MXU_DTYPE))
    per_chip = [_unpack(gathered[j], big_shard_shapes) for j in range(N_CHIPS)]
    full = {n: _merge_chips(jnp.stack([per_chip[j][k] for j in range(N_CHIPS)]), SHARD_AXIS[n])
            for k, n in enumerate(BIG)}
    small_sharded = [n for n in SMALL if n in SHARD_AXIS]
    sg = allgather_devices(_pack([w[n] for n in small_sharded], F32))
    per_chip_s = [_unpack(sg[2 * j], [w[n].shape for n in small_sharded]) for j in range(N_CHIPS)]
    for k, n in enumerate(small_sharded):
        full[n] = _merge_chips(jnp.stack([per_chip_s[j][k] for j in range(N_CHIPS)]), SHARD_AXIS[n])
    for n in SMALL:
        if n not in SHARD_AXIS:
            full[n] = w[n]

    loss_acc, dx, grads = local_step(x[0], mem[0], positions[0], loss_target[0], full)
    loss = lax.psum(jnp.sum(loss_acc), ("x", "y", "c"))

    gb = jnp.stack([_pack([_split_chips(grads[n], SHARD_AXIS[n])[j] for n in BIG], F32) for j in range(N_CHIPS)])
    rows = gb.shape[1]
    gb = gb.reshape(N_CHIPS, 2, rows // 2, PACK_COLS)
    got = swap_halves_sibling(gb)
    pair = add_own_half(gb, got, cc.reshape(1).astype(jnp.int32))
    half = sum_slabs(exchange_chips(pair), "sum_chips")
    red = join_halves_sibling(half).reshape(rows, PACK_COLS)
    g_big = dict(zip(BIG, _unpack(red, big_shard_shapes)))

    small_full_shapes = [grads[n].shape for n in SMALL]
    gs = sum_slabs(allgather_devices(_pack([grads[n] for n in SMALL], F32)), "sum_devices")
    g_small = {}
    for n, gfull in zip(SMALL, _unpack(gs, small_full_shapes)):
        if n in SHARD_AXIS:
            ax = SHARD_AXIS[n]
            size = gfull.shape[ax] // N_CHIPS
            gfull = lax.dynamic_slice_in_dim(gfull, chip * size, size, axis=ax)
        g_small[n] = gfull

    g_out, delta, new_m, new_v = {}, {}, {}, {}
    for n in BIG:
        sh = w[n].shape
        two_d = (-1, sh[-1])
        d_, m_, v_ = adamw(f"adamw_{n}", w[n].reshape(two_d), g_big[n].reshape(two_d), m[n].reshape(two_d),
                           v[n].reshape(two_d))
        g_out[n], delta[n], new_m[n], new_v[n] = g_big[n], d_.reshape(sh), m_.reshape(sh), v_.reshape(sh)
    small_shapes = [w[n].shape for n in SMALL]
    d_, m_, v_ = adamw("adamw_small", _pack([w[n] for n in SMALL], F32), _pack([g_small[n] for n in SMALL], F32),
                       _pack([m[n] for n in SMALL], F32), _pack([v[n] for n in SMALL], F32))
    for n, a_, b_, c_ in zip(SMALL, _unpack(d_, small_shapes), _unpack(m_, small_shapes), _unpack(v_, small_shapes)):
        g_out[n], delta[n], new_m[n], new_v[n] = g_small[n], a_, b_, c_

    return (loss, dx[None], *[g_out[n] for n in WEIGHTS], *[delta[n] for n in WEIGHTS],
            *[new_m[n] for n in WEIGHTS], *[new_v[n] for n in WEIGHTS])
```

```python
import functools
import math

import jax
import jax.numpy as jnp
import numpy as np
from jax import lax
from jax.experimental import pallas as pl
from jax.experimental.pallas import tpu as pltpu

F32 = jnp.float32
MXU_DTYPE = jnp.bfloat16
GRAD_WIRE_DTYPE = jnp.bfloat16
VMEM_LIMIT_V7X = 56 * 1024 * 1024
LANE = 128
SUBLANE = 8

D_MODEL = 1024
N_MEM = 256
EPS = 1e-6
ROPE_THETA = 10000.0
CHUNK = 64
XA_HEADS, XA_HD = 4, 64
GLA_HEADS, GLA_DK, GLA_DV, GLA_RANK, GLA_GATE_NORM = 4, 96, 192, 16, 16.0
DIL_GROUPS = ((128, 1), (512, 4), (2048, 16))
DIL_HEADS, DIL_HD, DIL_BLOCK = 4, 128, 128
SSM_HD, SSM_HEADS, SSM_GROUPS, SSM_STATE, SSM_CONV = 64, 12, 2, 128, 4
HGRN_HEADS, HGRN_DK = 6, 128
D_FF = 2816
FFN_CONV = 3
ADAM_LR, ADAM_B1, ADAM_B2, ADAM_EPS, ADAM_WD, ADAM_STEP = 0.001, 0.9, 0.999, 1e-08, 0.01, 10

MM_TILES = (1408, 1024, 768, 512, 384, 256, 128)
MM_VMEM_BUDGET = 40 * 1024 * 1024
ROW_BLOCK = 256
SCAN_CHUNKS = 2
PACK_COLS = 1024
PACK_ROWS = 32

WEIGHTS = ['mem_norm', 'mix_norm', 'xa_w_kv', 'xa_q_norm', 'xa_k_norm', 'ffn_norm', 'ffn_w_up', 'ffn_conv_w',
           'ffn_conv_b', 'ffn_w_down', 'a_w_in', 'a_w_gate2', 'a_b_gate', 'a_o_norm', 'a_w_out', 'b_w_in', 'b_q_norm',
           'b_k_norm', 'b_w_out', 'c_w_in', 'c_conv_w', 'c_conv_b', 'c_dt_bias', 'c_a_log', 'c_d', 'c_norm', 'c_w_out',
           'd_w_in', 'd_lower_bounds', 'd_o_norm', 'd_w_out']
SHARD_AXIS = {'xa_w_kv': 1, 'ffn_w_up': 2, 'ffn_conv_w': 2, 'ffn_w_down': 1, 'a_w_in': 1, 'a_w_gate2': 1, 'a_w_out': 0,
              'b_w_in': 1, 'b_w_out': 1, 'c_w_in': 1, 'c_conv_w': 1, 'c_w_out': 0, 'd_w_in': 1, 'd_w_out': 0}
BIG = ['xa_w_kv', 'ffn_w_up', 'ffn_w_down', 'a_w_in', 'a_w_gate2', 'a_w_out', 'b_w_in', 'b_w_out', 'c_w_in', 'c_w_out',
       'd_w_in', 'd_w_out']
SMALL = [n for n in WEIGHTS if n not in BIG]
N_CHIPS = 4
N_DEV = 8


class _MatmulSet:
    def __init__(self, cast, precision):
        def dot(a, b, dims):
            if cast:
                a = a.astype(MXU_DTYPE)
                b = b.astype(MXU_DTYPE)
            return lax.dot_general(a, b, (dims, ((), ())), precision=precision, preferred_element_type=F32)

        @jax.custom_vjp
        def nn(a, b):
            return dot(a, b, ((1,), (0,)))

        @jax.custom_vjp
        def nt(a, b):
            return dot(a, b, ((1,), (1,)))

        @jax.custom_vjp
        def tn(a, b):
            return dot(a, b, ((0,), (0,)))

        nn.defvjp(lambda a, b: (nn(a, b), (a, b)), lambda r, g: (nt(g, r[1]), tn(r[0], g)))
        nt.defvjp(lambda a, b: (nt(a, b), (a, b)), lambda r, g: (nn(g, r[1]), tn(g, r[0])))
        tn.defvjp(lambda a, b: (tn(a, b), (a, b)), lambda r, g: (nt(r[1], g), nn(r[0], g)))
        self.nn, self.nt, self.tn = nn, nt, tn


mm = _MatmulSet(True, None)
hi = _MatmulSet(False, lax.Precision.HIGHEST)


def _sigmoid(x):
    return jax.nn.sigmoid(x)


def _silu(x):
    return x * jax.nn.sigmoid(x)


def _softplus(x):
    return jnp.maximum(x, 0.0) + jnp.log1p(jnp.exp(-jnp.abs(x)))


def _rms(x, g, n_real=None):
    n = n_real or x.shape[-1]
    ms = jnp.sum(x * x, axis=-1, keepdims=True) * (1.0 / n)
    return x * lax.rsqrt(ms + EPS) * g


@jax.custom_vjp
def _swap_halves(x):
    return pltpu.roll(x, 64, 1)


_swap_halves.defvjp(lambda x: (_swap_halves(x), None), lambda _, g: (_swap_halves(g),))


def _tile(n, cands):
    for c in cands:
        if n % c == 0:
            return c
    raise ValueError(f"no tile for {n} among {cands}")


def _cparams(sem):
    return pltpu.CompilerParams(dimension_semantics=sem, vmem_limit_bytes=VMEM_LIMIT_V7X)


def _f32(v):
    return v.astype(F32) if jnp.issubdtype(v.dtype, jnp.floating) else v


def matmul(a, b, *, ta=False, tb=False, add=None, out_dtype=F32):
    m, k = (a.shape[1], a.shape[0]) if ta else a.shape
    n = b.shape[0] if tb else b.shape[1]
    assert k == (b.shape[1] if tb else b.shape[0]), (a.shape, b.shape, ta, tb)
    tk = _tile(k, MM_TILES)
    nk = k // tk
    sa, sb, so = a.dtype.itemsize, b.dtype.itemsize, jnp.dtype(out_dtype).itemsize

    def vmem(tm_, tn_):
        return (2 * tm_ * tk * sa + 2 * tk * tn_ * sb + 2 * tm_ * tn_ * so + (tm_ * tn_ * 4 if nk > 1 else 0)
                + (2 * tm_ * tn_ * add.dtype.itemsize if add is not None else 0))

    fits = [(tm_ * tn_, tm_, tn_) for tm_ in MM_TILES if m % tm_ == 0 for tn_ in MM_TILES if n % tn_ == 0
            if vmem(tm_, tn_) <= MM_VMEM_BUDGET]
    _, tm, tn = max(fits)
    dims = (((0,) if ta else (1,)), ((1,) if tb else (0,)))

    def body(*refs):
        a_ref, b_ref = refs[0], refs[1]
        add_ref = refs[2] if add is not None else None
        o_ref = refs[3] if add is not None else refs[2]
        part = lax.dot_general(a_ref[...].astype(MXU_DTYPE), b_ref[...].astype(MXU_DTYPE), (dims, ((), ())),
                               preferred_element_type=F32)

        def finish(r):
            if add_ref is not None:
                r = r + add_ref[...].astype(F32)
            o_ref[...] = r.astype(o_ref.dtype)

        if nk == 1:
            finish(part)
            return
        acc = refs[-1]
        kk = pl.program_id(2)

        @pl.when(kk == 0)
        def _():
            acc[...] = part

        @pl.when(kk > 0)
        def _():
            acc[...] += part

        @pl.when(kk == nk - 1)
        def _():
            finish(acc[...])

    a_spec = pl.BlockSpec((tk, tm), lambda i, j, q: (q, i)) if ta else pl.BlockSpec((tm, tk), lambda i, j, q: (i, q))
    b_spec = pl.BlockSpec((tn, tk), lambda i, j, q: (j, q)) if tb else pl.BlockSpec((tk, tn), lambda i, j, q: (q, j))
    o_spec = pl.BlockSpec((tm, tn), lambda i, j, q: (i, j))
    ins, specs = [a, b], [a_spec, b_spec]
    if add is not None:
        ins.append(add)
        specs.append(o_spec)
    return pl.pallas_call(
        body, name=f"mm_{m}x{k}x{n}_{int(ta)}{int(tb)}{int(add is not None)}",
        grid=(m // tm, n // tn, nk), in_specs=specs, out_specs=o_spec,
        out_shape=jax.ShapeDtypeStruct((m, n), out_dtype),
        scratch_shapes=[pltpu.VMEM((tm, tn), F32)] if nk > 1 else [],
        compiler_params=_cparams(("parallel", "parallel", "arbitrary")),
    )(*ins)


def _row_spec(a, block):
    return pl.BlockSpec((block, a.shape[1]), lambda i: (i, 0))


def _whole_spec(a):
    return pl.BlockSpec(a.shape, lambda i: (0,) * a.ndim)


def tmap(name, fn, rows, params, outs, block=ROW_BLOCK):
    s = rows[0].shape[0]
    block = min(block, s)
    nr, npar = len(rows), len(params)

    def body(*refs):
        res = fn(*[_f32(r[...]) for r in refs[:nr]], *[_f32(p[...]) for p in refs[nr:nr + npar]])
        for o_ref, v in zip(refs[nr + npar:], res, strict=True):
            o_ref[...] = v.astype(o_ref.dtype)

    return pl.pallas_call(
        body, name=name, grid=(s // block,),
        in_specs=[_row_spec(a, block) for a in rows] + [_whole_spec(p) for p in params],
        out_specs=[pl.BlockSpec((block, w), lambda i: (i, 0)) for w, _ in outs],
        out_shape=[jax.ShapeDtypeStruct((s, w), dt) for w, dt in outs],
        compiler_params=_cparams(("parallel",)),
    )(*rows, *params)


def tmap_bwd(name, fn, rows, params, douts, row_grad, row_add=None, grad_dtype=None, block=ROW_BLOCK):
    s = rows[0].shape[0]
    block = min(block, s)
    grad_dtype = grad_dtype or MXU_DTYPE
    nr, npar, nd = len(rows), len(params), len(douts)
    gr = [i for i in range(nr) if row_grad[i]]
    row_add = row_add or {}
    adds = [row_add[i] for i in gr if i in row_add]

    def body(*refs):
        rv = [_f32(r[...]) for r in refs[:nr]]
        pv = [_f32(p[...]) for p in refs[nr:nr + npar]]
        dv = tuple(_f32(d[...]) for d in refs[nr + npar:nr + npar + nd])
        add_refs = list(refs[nr + npar + nd:nr + npar + nd + len(adds)])
        out_refs = refs[nr + npar + nd + len(adds):]

        def f(*diff):
            rr = list(rv)
            for n_, i_ in enumerate(gr):
                rr[i_] = diff[n_]
            return tuple(fn(*rr, *diff[len(gr):]))

        _, vjp = jax.vjp(f, *[rv[i_] for i_ in gr], *pv)
        g = vjp(dv)
        for n_, i_ in enumerate(gr):
            v = g[n_]
            if i_ in row_add:
                v = v + add_refs.pop(0)[...].astype(F32)
            out_refs[n_][...] = v.astype(out_refs[n_].dtype)
        first = pl.program_id(0) == 0
        for n_ in range(npar):
            ref = out_refs[len(gr) + n_]

            @pl.when(first)
            def _(ref=ref):
                ref[...] = jnp.zeros_like(ref)

            ref[...] += g[len(gr) + n_]

    res = pl.pallas_call(
        body, name=name, grid=(s // block,),
        in_specs=[_row_spec(a, block) for a in rows] + [_whole_spec(p) for p in params]
        + [_row_spec(d, block) for d in douts] + [_row_spec(a, block) for a in adds],
        out_specs=[_row_spec(rows[i], block) for i in gr] + [_whole_spec(p) for p in params],
        out_shape=[jax.ShapeDtypeStruct(rows[i].shape, F32 if i in row_add else grad_dtype) for i in gr]
        + [jax.ShapeDtypeStruct(p.shape, F32) for p in params],
        compiler_params=_cparams(("arbitrary",)),
    )(*rows, *params, *douts, *adds)
    return list(res[:len(gr)]), list(res[len(gr):])


def rscan(name, fn, state_shapes, rows, params, outs, block):
    s = rows[0].shape[0]
    nsteps = s // block
    nr, npar, no, ns = len(rows), len(params), len(outs), len(state_shapes)

    def body(*refs):
        out_refs = refs[nr + npar:nr + npar + no]
        sav_refs = refs[nr + npar + no:nr + npar + no + ns]
        st_refs = refs[nr + npar + no + ns:]

        @pl.when(pl.program_id(0) == 0)
        def _():
            for st in st_refs:
                st[...] = jnp.zeros_like(st)

        sts = tuple(st[...] for st in st_refs)
        for sv, v in zip(sav_refs, sts):
            sv[...] = v
        new, res = fn(sts, *[_f32(r[...]) for r in refs[:nr]], *[_f32(p[...]) for p in refs[nr:nr + npar]])
        for st, v in zip(st_refs, new, strict=True):
            st[...] = v
        for o_ref, v in zip(out_refs, res, strict=True):
            o_ref[...] = v.astype(o_ref.dtype)

    res = pl.pallas_call(
        body, name=name, grid=(nsteps,),
        in_specs=[_row_spec(a, block) for a in rows] + [_whole_spec(p) for p in params],
        out_specs=[pl.BlockSpec((block, w), lambda i: (i, 0)) for w, _ in outs]
        + [pl.BlockSpec(sh, lambda i: (i, 0)) for sh in state_shapes],
        out_shape=[jax.ShapeDtypeStruct((s, w), dt) for w, dt in outs]
        + [jax.ShapeDtypeStruct((nsteps * sh[0], sh[1]), F32) for sh in state_shapes],
        scratch_shapes=[pltpu.VMEM(sh, F32) for sh in state_shapes],
        compiler_params=_cparams(("arbitrary",)),
    )(*rows, *params)
    return list(res[:no]), list(res[no:])


def rscan_bwd(name, fn, saved, rows, params, douts, block, grad_dtype=None):
    s = rows[0].shape[0]
    nsteps = s // block
    grad_dtype = grad_dtype or MXU_DTYPE
    nr, npar, nd, ns = len(rows), len(params), len(douts), len(saved)
    state_shapes = [(sv.shape[0] // nsteps, sv.shape[1]) for sv in saved]

    def body(*refs):
        rv = [_f32(r[...]) for r in refs[:nr]]
        pv = [_f32(p[...]) for p in refs[nr:nr + npar]]
        dv = tuple(_f32(d[...]) for d in refs[nr + npar:nr + npar + nd])
        sv = tuple(x[...] for x in refs[nr + npar + nd:nr + npar + nd + ns])
        out_refs = refs[nr + npar + nd + ns:nr + npar + nd + ns + nr + npar]
        dst_refs = refs[nr + npar + nd + ns + nr + npar:]
        first = pl.program_id(0) == 0

        @pl.when(first)
        def _():
            for d in dst_refs:
                d[...] = jnp.zeros_like(d)

        def f(sts, *args):
            return fn(sts, *args)

        _, vjp = jax.vjp(f, sv, *rv, *pv)
        g = vjp((tuple(d[...] for d in dst_refs), dv))
        for d, v in zip(dst_refs, g[0], strict=True):
            d[...] = v
        for n_ in range(nr):
            out_refs[n_][...] = g[1 + n_].astype(out_refs[n_].dtype)
        for n_ in range(npar):
            ref = out_refs[nr + n_]

            @pl.when(first)
            def _(ref=ref):
                ref[...] = jnp.zeros_like(ref)

            ref[...] += g[1 + nr + n_]

    rev = lambda i: (nsteps - 1 - i, 0)
    res = pl.pallas_call(
        body, name=name, grid=(nsteps,),
        in_specs=[pl.BlockSpec((block, a.shape[1]), rev) for a in rows] + [_whole_spec(p) for p in params]
        + [pl.BlockSpec((block, d.shape[1]), rev) for d in douts] + [pl.BlockSpec(sh, rev) for sh in state_shapes],
        out_specs=[pl.BlockSpec((block, a.shape[1]), rev) for a in rows] + [_whole_spec(p) for p in params],
        out_shape=[jax.ShapeDtypeStruct(a.shape, grad_dtype) for a in rows]
        + [jax.ShapeDtypeStruct(p.shape, F32) for p in params],
        scratch_shapes=[pltpu.VMEM(sh, F32) for sh in state_shapes],
        compiler_params=_cparams(("arbitrary",)),
    )(*rows, *params, *douts, *saved)
    return list(res[:nr]), list(res[nr:])


def _norm_stage(x, g):
    return (_rms(x, g),)


def _tril():
    r = lax.broadcasted_iota(jnp.int32, (CHUNK, CHUNK), 0)
    c = lax.broadcasted_iota(jnp.int32, (CHUNK, CHUNK), 1)
    return r >= c


def _gla_chunk(st, q, k, v, la):
    tril = _tril()
    b = hi.nn(tril.astype(F32), la)
    rowi = lax.broadcasted_iota(jnp.int32, (CHUNK, 1), 0)
    b_last = jnp.sum(la, axis=0, keepdims=True)
    b_ref = jnp.sum(jnp.where(rowi < CHUNK // 2, la, 0.0), axis=0, keepdims=True)
    att = mm.nt(q * jnp.exp(b - b_ref), k * jnp.exp(b_ref - b))
    att = jnp.where(tril, att, 0.0)
    o = mm.nn(att, v) + mm.nn(q * jnp.exp(b), st)
    decay = jnp.exp(hi.tn(la, jnp.ones((CHUNK, v.shape[1]), F32)))
    st2 = decay * st + mm.tn(k * jnp.exp(b_last - b), v)
    return st2, o


def _gla_step(heads, kp, vp, scale):
    def fn(states, q, k, v, la):
        sts = list(states)
        rows = []
        for c in range(q.shape[0] // CHUNK):
            r = slice(c * CHUNK, (c + 1) * CHUNK)
            oh = []
            for h in range(heads):
                ks, vs = slice(h * kp, (h + 1) * kp), slice(h * vp, (h + 1) * vp)
                qh = q[r, ks] * scale if scale != 1.0 else q[r, ks]
                sts[h], o = _gla_chunk(sts[h], qh, k[r, ks], v[r, vs], la[r, ks])
                oh.append(o)
            rows.append(jnp.concatenate(oh, axis=1))
        return tuple(sts), (jnp.concatenate(rows, axis=0),)

    return fn


def _ssd_step(states, xa, dtr, dtb, alog, dsk):
    sts = list(states)
    tril = _tril()
    trif = tril.astype(F32)
    avg = jnp.full((CHUNK, LANE), 1.0 / LANE, F32)
    hg = SSM_HEADS // SSM_GROUPS
    xw = SSM_HEADS * LANE
    rows = []
    for c in range(xa.shape[0] // CHUNK):
        r = slice(c * CHUNK, (c + 1) * CHUNK)
        yh = []
        for g in range(SSM_GROUPS):
            bm = xa[r, xw + g * LANE:xw + (g + 1) * LANE]
            cm = xa[r, xw + (SSM_GROUPS + g) * LANE:xw + (SSM_GROUPS + g + 1) * LANE]
            cb = mm.nt(cm, bm)
            for hh in range(hg):
                h = g * hg + hh
                ls = slice(h * LANE, (h + 1) * LANE)
                xs = xa[r, ls]
                dt = _softplus(dtr[r, ls] + dtb[:, ls])
                a = dt * (-jnp.exp(alog[:, ls]))
                xdt = xs * dt
                acs = hi.nn(trif, a)
                acs_last = jnp.sum(a, axis=0, keepdims=True)
                seg = hi.nt(acs, avg) - hi.nt(avg, acs)
                lmat = jnp.exp(jnp.where(tril, seg, -1e30))
                y = mm.nn(cb * lmat, xdt) + mm.nn(cm, sts[h]) * jnp.exp(acs) + dsk[:, ls] * xs
                sts[h] = jnp.exp(acs_last) * sts[h] + mm.tn(bm, xdt * jnp.exp(acs_last - acs))
                yh.append(y)
        rows.append(jnp.concatenate(yh, axis=1))
    return tuple(sts), (jnp.concatenate(rows, axis=0),)


def _gla_pre(glr, w2, bg):
    z = mm.nn(glr, w2) + bg
    return (-_softplus(-z) * (1.0 / GLA_GATE_NORM),)


def _gla_post(o, og, g):
    w = 2 * LANE
    return (jnp.concatenate([_rms(o[:, h * w:(h + 1) * w], g, GLA_DV) * _silu(og[:, h * w:(h + 1) * w])
                             for h in range(GLA_HEADS)], axis=1),)


def _hgrn_pre(q, f, lbnd):
    e = jnp.exp(lbnd - jnp.max(lbnd, axis=0, keepdims=True))
    rowi = lax.broadcasted_iota(jnp.int32, e.shape, 0)
    lb = jnp.sum(jnp.where(rowi >= 1, e, 0.0), axis=0, keepdims=True) / jnp.sum(e, axis=0, keepdims=True)
    fg = lb + (1.0 - lb) * _sigmoid(f)
    return _silu(q), 1.0 - fg, jnp.log(fg)


def _hgrn_post(o, og, g):
    return (jnp.concatenate([_rms(o[:, h * LANE:(h + 1) * LANE], g) for h in range(HGRN_HEADS)], axis=1)
            * _sigmoid(og),)


def _mamba_post(y, z, g):
    v = y * _silu(z)
    w = (SSM_HEADS // SSM_GROUPS) * LANE
    n_real = (SSM_HEADS // SSM_GROUPS) * SSM_HD
    return (jnp.concatenate([_rms(v[:, i * w:(i + 1) * w], g[:, i * w:(i + 1) * w], n_real)
                             for i in range(SSM_GROUPS)], axis=1),)


def _dil_pre(q, k, cosf, sinf, qg, kg):
    def one(x, g):
        hs = []
        for h in range(DIL_HEADS):
            n = _rms(x[:, h * LANE:(h + 1) * LANE], g)
            hs.append(n * cosf + _swap_halves(n) * sinf)
        return jnp.concatenate(hs, axis=1)

    return one(q, qg), one(k, kg)


def _dil_merge(o0, o1, o2, l0, l1, l2):
    m = jnp.maximum(jnp.maximum(l0, l1), l2)
    e0, e1, e2 = jnp.exp(l0 - m), jnp.exp(l1 - m), jnp.exp(l2 - m)
    return ((e0 * o0 + e1 * o1 + e2 * o2) / (e0 + e1 + e2),)


def _dil_block(q, kp, kc, vp, vc, lim):
    kk = jnp.concatenate([kp, kc], axis=0)
    vv = jnp.concatenate([vp, vc], axis=0)
    s = mm.nt(q, kk) * (DIL_HD ** -0.5)
    i = lax.broadcasted_iota(jnp.int32, s.shape, 0)
    j = lax.broadcasted_iota(jnp.int32, s.shape, 1)
    dist = DIL_BLOCK + i - j
    s = jnp.where((dist >= 0) & (dist <= DIL_BLOCK) & (j >= lim), s, -1e30)
    m = jnp.max(s, axis=-1, keepdims=True)
    p = jnp.exp(s - m)
    l = jnp.sum(p, axis=-1, keepdims=True)
    return mm.nn(p / l, vv), jnp.broadcast_to(m + jnp.log(l), (q.shape[0], LANE))


def _xattn(xq, kv, qg, kg):
    w = XA_HEADS * LANE
    os_ = []
    for h in range(XA_HEADS):
        ls = slice(h * LANE, (h + 1) * LANE)
        q = _rms(xq[:, ls], qg, XA_HD)
        k = _rms(kv[:, ls], kg, XA_HD)
        s = mm.nt(q, k) * (XA_HD ** -0.5)
        p = jnp.exp(s - jnp.max(s, axis=-1, keepdims=True))
        p = p / jnp.sum(p, axis=-1, keepdims=True)
        os_.append(mm.nn(p, kv[:, w + h * LANE:w + (h + 1) * LANE]))
    return (jnp.concatenate(os_, axis=1),)


def dil_attn(name, q, k, v, r):
    s, w = q.shape
    l = s // r
    nb = l // DIL_BLOCK
    q2, k2, v2 = (t.reshape(l, r * w) for t in (q, k, v))

    def body(q_r, kp_r, kc_r, vp_r, vc_r, o_r, l_r):
        lim = jnp.where(pl.program_id(1) == 0, DIL_BLOCK, 0)
        for h in range(DIL_HEADS):
            ls = slice(h * LANE, (h + 1) * LANE)
            o, lse = _dil_block(q_r[:, ls], kp_r[:, ls], kc_r[:, ls], vp_r[:, ls], vc_r[:, ls], lim)
            o_r[:, ls] = o
            l_r[:, ls] = lse

    cur = pl.BlockSpec((DIL_BLOCK, w), lambda res, n: (n, res))
    prev = pl.BlockSpec((DIL_BLOCK, w), lambda res, n: (jnp.maximum(n - 1, 0), res))
    o, lse = pl.pallas_call(
        body, name=name, grid=(r, nb), in_specs=[cur, prev, cur, prev, cur], out_specs=[cur, cur],
        out_shape=[jax.ShapeDtypeStruct((l, r * w), F32)] * 2,
        compiler_params=_cparams(("parallel", "parallel")),
    )(q2, k2, k2, v2, v2)
    return o.reshape(s, w), lse.reshape(s, w)


def dil_attn_bwd(name, q, k, v, do, dlse, r):
    s, w = q.shape
    l = s // r
    nb = l // DIL_BLOCK
    q2, k2, v2, do2, dl2 = (t.reshape(l, r * w) for t in (q, k, v, do, dlse))

    def body(q_r, kp_r, kc_r, vp_r, vc_r, do_r, dl_r, dq_r, dk_r, dv_r, ck, cv):
        i = pl.program_id(1)
        lim = jnp.where(i == nb - 1, DIL_BLOCK, 0)

        @pl.when(i == 0)
        def _():
            ck[...] = jnp.zeros_like(ck)
            cv[...] = jnp.zeros_like(cv)

        for h in range(DIL_HEADS):
            ls = slice(h * LANE, (h + 1) * LANE)
            _, vjp = jax.vjp(functools.partial(_dil_block, lim=lim),
                             q_r[:, ls], kp_r[:, ls], kc_r[:, ls], vp_r[:, ls], vc_r[:, ls])
            gq, gkp, gkc, gvp, gvc = vjp((do_r[:, ls], dl_r[:, ls]))
            dq_r[:, ls] = gq.astype(dq_r.dtype)
            dk_r[:, ls] = (gkc + ck[:, ls]).astype(dk_r.dtype)
            dv_r[:, ls] = (gvc + cv[:, ls]).astype(dv_r.dtype)
            ck[:, ls] = gkp
            cv[:, ls] = gvp

    cur = pl.BlockSpec((DIL_BLOCK, w), lambda res, i: (nb - 1 - i, res))
    prev = pl.BlockSpec((DIL_BLOCK, w), lambda res, i: (jnp.maximum(nb - 2 - i, 0), res))
    dq, dk, dv = pl.pallas_call(
        body, name=name, grid=(r, nb), in_specs=[cur, prev, cur, prev, cur, cur, cur], out_specs=[cur, cur, cur],
        out_shape=[jax.ShapeDtypeStruct((l, r * w), F32), jax.ShapeDtypeStruct((l, r * w), F32),
                   jax.ShapeDtypeStruct((l, r * w), MXU_DTYPE)],
        scratch_shapes=[pltpu.VMEM((DIL_BLOCK, w), F32)] * 2,
        compiler_params=_cparams(("parallel", "arbitrary")),
    )(q2, k2, k2, v2, v2, do2, dl2)
    return dq.reshape(s, w), dk.reshape(s, w), dv.reshape(s, w)


def _dsilu(u):
    sg = _sigmoid(u)
    return sg * (1.0 + u * (1.0 - sg))


def _taps(xp, w_ref, ntap, lo, hi_):
    acc = None
    for j in range(ntap):
        sh = ntap - 1 - j
        term = w_ref[j:j + 1, :] * (pltpu.roll(xp, sh, 0) if sh else xp)[lo:hi_]
        acc = term if acc is None else acc + term
    return acc


def conv_fwd(name, x, w, b, mode, out_dtype, tc, block=ROW_BLOCK):
    s, c = x.shape
    ntap = w.shape[0]
    block = min(block, s)
    f = c // 2 if mode == 'glu' else c
    nh = 2 if mode == 'glu' else 1
    off = f // tc

    def body(*refs):
        first = pl.program_id(1) == 0
        us = []
        for hlf in range(nh):
            prev_r, cur_r, w_r, b_r = refs[4 * hlf:4 * hlf + 4]
            xp = jnp.concatenate([jnp.where(first, 0.0, prev_r[...]), cur_r[...]], axis=0)
            us.append(b_r[...] + _taps(xp, w_r, ntap, SUBLANE, SUBLANE + block))
        o_ref = refs[-1]
        o_ref[...] = (_silu(us[0]) * us[1] if mode == 'glu' else _silu(us[0])).astype(o_ref.dtype)

    rb = block // SUBLANE
    ins, specs = [], []
    for hlf in range(nh):
        o = hlf * off
        ins += [x, x, w, b]
        specs += [pl.BlockSpec((SUBLANE, tc), lambda j, i, o=o: (jnp.maximum(i * rb - 1, 0), j + o)),
                  pl.BlockSpec((block, tc), lambda j, i, o=o: (i, j + o)),
                  pl.BlockSpec((ntap, tc), lambda j, i, o=o: (0, j + o)),
                  pl.BlockSpec((1, tc), lambda j, i, o=o: (0, j + o))]
    return pl.pallas_call(
        body, name=name, grid=(f // tc, s // block), in_specs=specs,
        out_specs=pl.BlockSpec((block, tc), lambda j, i: (i, j)),
        out_shape=jax.ShapeDtypeStruct((s, f), out_dtype),
        compiler_params=_cparams(("parallel", "parallel")),
    )(*ins)


def conv_bwd(name, x, w, b, dout, mode, tc, block=ROW_BLOCK):
    s, c = x.shape
    ntap = w.shape[0]
    block = min(block, s)
    nblk = s // block
    f = c // 2 if mode == 'glu' else c
    nh = 2 if mode == 'glu' else 1
    off = f // tc
    ext = block + SUBLANE

    def body(*refs):
        i = pl.program_id(1)
        first, last = i == 0, i == nblk - 1
        dcur_r, dnext_r = refs[5 * nh], refs[5 * nh + 1]
        outs = refs[5 * nh + 2:]
        d_e = jnp.concatenate([dcur_r[...], jnp.where(last, 0.0, dnext_r[...])], axis=0)
        xps, us = [], []
        for hlf in range(nh):
            prev_r, cur_r, next_r, w_r, b_r = refs[5 * hlf:5 * hlf + 5]
            xp = jnp.concatenate([jnp.where(first, 0.0, prev_r[...]), cur_r[...], jnp.where(last, 0.0, next_r[...])],
                                 axis=0)
            xps.append(xp)
            us.append(b_r[...] + _taps(xp, w_r, ntap, SUBLANE, SUBLANE + ext))
        if mode == 'glu':
            dus = [d_e * us[1] * _dsilu(us[0]), d_e * _silu(us[0])]
        else:
            dus = [d_e * _dsilu(us[0])]
        for hlf in range(nh):
            w_r = refs[5 * hlf + 3]
            dx_r, dw_r, db_r = outs[3 * hlf:3 * hlf + 3]
            du, xp = dus[hlf], xps[hlf]

            @pl.when(first)
            def _(dw_r=dw_r, db_r=db_r):
                dw_r[...] = jnp.zeros_like(dw_r)
                db_r[...] = jnp.zeros_like(db_r)

            db_r[...] += jnp.sum(du[:block], axis=0, keepdims=True)
            dx = None
            for j in range(ntap):
                sh = ntap - 1 - j
                xs = (pltpu.roll(xp, sh, 0) if sh else xp)[SUBLANE:SUBLANE + block]
                dw_r[j:j + 1, :] += jnp.sum(du[:block] * xs, axis=0, keepdims=True)
                term = w_r[j:j + 1, :] * (pltpu.roll(du, ext - sh, 0) if sh else du)[:block]
                dx = term if dx is None else dx + term
            dx_r[...] = dx.astype(dx_r.dtype)

    rb = block // SUBLANE
    nrow8 = s // SUBLANE
    ins, specs = [], []
    for hlf in range(nh):
        o = hlf * off
        ins += [x, x, x, w, b]
        specs += [pl.BlockSpec((SUBLANE, tc), lambda j, i, o=o: (jnp.maximum(i * rb - 1, 0), j + o)),
                  pl.BlockSpec((block, tc), lambda j, i, o=o: (i, j + o)),
                  pl.BlockSpec((SUBLANE, tc), lambda j, i, o=o: (jnp.minimum((i + 1) * rb, nrow8 - 1), j + o)),
                  pl.BlockSpec((ntap, tc), lambda j, i, o=o: (0, j + o)),
                  pl.BlockSpec((1, tc), lambda j, i, o=o: (0, j + o))]
    ins += [dout, dout]
    specs += [pl.BlockSpec((block, tc), lambda j, i: (i, j)),
              pl.BlockSpec((SUBLANE, tc), lambda j, i: (jnp.minimum((i + 1) * rb, nrow8 - 1), j))]
    out_specs, out_shape = [], []
    for hlf in range(nh):
        out_specs += [pl.BlockSpec((block, tc), lambda j, i: (i, j)), pl.BlockSpec((ntap, tc), lambda j, i: (0, j)),
                      pl.BlockSpec((1, tc), lambda j, i: (0, j))]
        out_shape += [jax.ShapeDtypeStruct((s, f), MXU_DTYPE), jax.ShapeDtypeStruct((ntap, f), F32),
                      jax.ShapeDtypeStruct((1, f), F32)]
    res = pl.pallas_call(
        body, name=name, grid=(f // tc, nblk), in_specs=specs, out_specs=out_specs, out_shape=out_shape,
        compiler_params=_cparams(("parallel", "arbitrary")),
    )(*ins)
    if nh == 1:
        return res[0], res[1], res[2]
    return (jnp.concatenate([res[0], res[3]], axis=1), jnp.concatenate([res[1], res[4]], axis=1),
            jnp.concatenate([res[2], res[5]], axis=1))


def loss_head(y, target, block=ROW_BLOCK):
    s, d = y.shape
    block = min(block, s)

    def body(y_r, t_r, acc_r, dy_r):
        e = y_r[...] - t_r[...]
        dy_r[...] = e * (1.0 / d)

        @pl.when(pl.program_id(0) == 0)
        def _():
            acc_r[...] = jnp.zeros_like(acc_r)

        acc_r[...] += jnp.sum((e * e).reshape(block // SUBLANE, SUBLANE, d), axis=0) * (0.5 / d)

    return pl.pallas_call(
        body, name="loss_head", grid=(s // block,),
        in_specs=[pl.BlockSpec((block, d), lambda i: (i, 0))] * 2,
        out_specs=[pl.BlockSpec((SUBLANE, d), lambda i: (0, 0)), pl.BlockSpec((block, d), lambda i: (i, 0))],
        out_shape=[jax.ShapeDtypeStruct((SUBLANE, d), F32), jax.ShapeDtypeStruct((s, d), F32)],
        compiler_params=_cparams(("arbitrary",)),
    )(y, target)


def adamw(name, w, g, m, v):
    r, c = w.shape
    tr = r if r <= 512 else _tile(r, (512, 256, 128, 64, 32, 16, 8))
    if c * tr * 4 > (1 << 21):
        tr = _tile(r, (256, 128, 64, 32, 16, 8))

    def body(w_r, g_r, m_r, v_r, d_r, nm_r, nv_r):
        gg = g_r[...]
        nm = ADAM_B1 * m_r[...] + (1.0 - ADAM_B1) * gg
        nv = ADAM_B2 * v_r[...] + (1.0 - ADAM_B2) * (gg * gg)
        m_hat = nm / (1.0 - ADAM_B1 ** ADAM_STEP)
        v_hat = nv / (1.0 - ADAM_B2 ** ADAM_STEP)
        d_r[...] = -ADAM_LR * (m_hat / (jnp.sqrt(v_hat) + ADAM_EPS) + ADAM_WD * w_r[...])
        nm_r[...] = nm
        nv_r[...] = nv

    spec = pl.BlockSpec((tr, c), lambda i: (i, 0))
    return pl.pallas_call(
        body, name=name, grid=(r // tr,), in_specs=[spec] * 4, out_specs=[spec] * 3,
        out_shape=[jax.ShapeDtypeStruct((r, c), F32)] * 3, compiler_params=_cparams(("parallel",)),
    )(w, g, m, v)


MESH = pl.DeviceIdType.MESH
_ANY = pl.BlockSpec(memory_space=pl.ANY)


def _place():
    return lax.axis_index("x"), lax.axis_index("y"), lax.axis_index("c")


def allgather_chips(shard):
    r, c = shard.shape
    hr = r // 2

    def body(w_ref, out_ref, send_sems, recv_sems, local_sem):
        x, y, cc = _place()
        sibling = (x, y, 1 - cc)
        chips = [(1 - x, y), (x, 1 - y), (1 - x, 1 - y)]

        def half(chip, core):
            return out_ref.at[2 * chip[0] + chip[1], pl.ds(core * hr, hr), :]

        def copy(k, chip, core, to, src=None):
            return pltpu.make_async_remote_copy(
                src_ref=half(chip, core) if src is None else src, dst_ref=half(chip, core),
                send_sem=send_sems.at[k], recv_sem=recv_sems.at[k], device_id=to, device_id_type=MESH)

        mine = pltpu.make_async_copy(w_ref, out_ref.at[2 * x + y], local_sem)
        mine.start()
        my_half = w_ref.at[pl.ds(cc * hr, hr), :]
        first = [copy(j, (x, y), cc, (*chip, cc), src=my_half) for j, chip in enumerate(chips)]
        for cp in first:
            cp.start()
        passed = [copy(3 + j, chip, cc, sibling) for j, chip in enumerate(chips)]
        for j, chip in enumerate(chips):
            copy(j, chip, cc, (x, y, cc)).wait_recv()
            passed[j].start()
        for j, chip in enumerate(chips):
            copy(3 + j, chip, 1 - cc, (x, y, cc)).wait_recv()
        for cp in first + passed:
            cp.wait_send()
        mine.wait()

    return pl.pallas_call(
        body, name="allgather_chips", in_specs=[_ANY], out_specs=_ANY,
        out_shape=jax.ShapeDtypeStruct((N_CHIPS, r, c), shard.dtype),
        scratch_shapes=[pltpu.SemaphoreType.DMA((6,)), pltpu.SemaphoreType.DMA((6,)), pltpu.SemaphoreType.DMA],
    )(shard)


def allgather_devices(buf):
    r, c = buf.shape

    def body(b_ref, out_ref, send_sems, recv_sems, local_sem):
        x, y, cc = _place()
        me = 4 * x + 2 * y + cc
        mine = pltpu.make_async_copy(b_ref, out_ref.at[me], local_sem)
        mine.start()
        copies = []
        for k in range(1, N_DEV):
            px, py, pc = x ^ (k >> 2), y ^ ((k >> 1) & 1), cc ^ (k & 1)
            cp = pltpu.make_async_remote_copy(src_ref=b_ref, dst_ref=out_ref.at[me], send_sem=send_sems.at[k - 1],
                                              recv_sem=recv_sems.at[k - 1], device_id=(px, py, pc), device_id_type=MESH)
            cp.start()
            copies.append((cp, 4 * px + 2 * py + pc))
        for k, (cp, peer) in enumerate(copies):
            pltpu.make_async_remote_copy(src_ref=b_ref, dst_ref=out_ref.at[peer], send_sem=send_sems.at[k],
                                         recv_sem=recv_sems.at[k], device_id=(x, y, cc), device_id_type=MESH).wait_recv()
        for cp, _ in copies:
            cp.wait_send()
        mine.wait()

    return pl.pallas_call(
        body, name="allgather_devices", in_specs=[_ANY], out_specs=_ANY,
        out_shape=jax.ShapeDtypeStruct((N_DEV, r, c), buf.dtype),
        scratch_shapes=[pltpu.SemaphoreType.DMA((N_DEV - 1,)), pltpu.SemaphoreType.DMA((N_DEV - 1,)),
                        pltpu.SemaphoreType.DMA],
    )(buf)


def swap_halves_sibling(g):
    n, _, r, c = g.shape

    def body(g_ref, out_ref, send_sem, recv_sem):
        x, y, cc = _place()
        cp = pltpu.make_async_remote_copy(src_ref=g_ref.at[:, 1 - cc], dst_ref=out_ref, send_sem=send_sem,
                                          recv_sem=recv_sem, device_id=(x, y, 1 - cc), device_id_type=MESH)
        cp.start()
        cp.wait()

    return pl.pallas_call(
        body, name="swap_halves_sibling", in_specs=[_ANY], out_specs=_ANY,
        out_shape=jax.ShapeDtypeStruct((n, r, c), g.dtype),
        scratch_shapes=[pltpu.SemaphoreType.DMA, pltpu.SemaphoreType.DMA],
    )(g)


def exchange_chips(p):
    n, r, c = p.shape

    def body(p_ref, out_ref, send_sems, recv_sems, local_sem):
        x, y, cc = _place()
        me = 2 * x + y
        mine = pltpu.make_async_copy(p_ref.at[me], out_ref.at[me], local_sem)
        mine.start()
        chips = [(1 - x, y), (x, 1 - y), (1 - x, 1 - y)]
        cps = []
        for j, chip in enumerate(chips):
            cp = pltpu.make_async_remote_copy(src_ref=p_ref.at[2 * chip[0] + chip[1]], dst_ref=out_ref.at[me],
                                              send_sem=send_sems.at[j], recv_sem=recv_sems.at[j],
                                              device_id=(*chip, cc), device_id_type=MESH)
            cp.start()
            cps.append(cp)
        for j, chip in enumerate(chips):
            pltpu.make_async_remote_copy(src_ref=p_ref.at[me], dst_ref=out_ref.at[2 * chip[0] + chip[1]],
                                         send_sem=send_sems.at[j], recv_sem=recv_sems.at[j],
                                         device_id=(x, y, cc), device_id_type=MESH).wait_recv()
        for cp in cps:
            cp.wait_send()
        mine.wait()

    return pl.pallas_call(
        body, name="exchange_chips", in_specs=[_ANY], out_specs=_ANY,
        out_shape=jax.ShapeDtypeStruct((n, r, c), p.dtype),
        scratch_shapes=[pltpu.SemaphoreType.DMA((3,)), pltpu.SemaphoreType.DMA((3,)), pltpu.SemaphoreType.DMA],
    )(p)


def join_halves_sibling(h):
    r, c = h.shape

    def body(h_ref, out_ref, send_sem, recv_sem, local_sem):
        x, y, cc = _place()
        mine = pltpu.make_async_copy(h_ref, out_ref.at[cc], local_sem)
        mine.start()
        cp = pltpu.make_async_remote_copy(src_ref=h_ref, dst_ref=out_ref.at[cc], send_sem=send_sem, recv_sem=recv_sem,
                                          device_id=(x, y, 1 - cc), device_id_type=MESH)
        cp.start()
        pltpu.make_async_remote_copy(src_ref=h_ref, dst_ref=out_ref.at[1 - cc], send_sem=send_sem, recv_sem=recv_sem,
                                     device_id=(x, y, cc), device_id_type=MESH).wait_recv()
        cp.wait_send()
        mine.wait()

    return pl.pallas_call(
        body, name="join_halves_sibling", in_specs=[_ANY], out_specs=_ANY,
        out_shape=jax.ShapeDtypeStruct((2, r, c), h.dtype),
        scratch_shapes=[pltpu.SemaphoreType.DMA, pltpu.SemaphoreType.DMA, pltpu.SemaphoreType.DMA],
    )(h)


def add_own_half(g, got, core, out_dtype):
    n, _, r, c = g.shape
    tr = _tile(r, (512, 256, 128, 64, 32, 16))

    def body(c_ref, g_r, o_r, out_r):
        out_r[...] = (g_r[...] + o_r[...]).astype(out_r.dtype)

    return pl.pallas_call(
        body, name="add_own_half",
        grid_spec=pltpu.PrefetchScalarGridSpec(
            num_scalar_prefetch=1, grid=(n, r // tr),
            in_specs=[pl.BlockSpec((None, None, tr, c), lambda i, j, c_ref: (i, c_ref[0], j, 0)),
                      pl.BlockSpec((None, tr, c), lambda i, j, c_ref: (i, j, 0))],
            out_specs=pl.BlockSpec((None, tr, c), lambda i, j, c_ref: (i, j, 0))),
        out_shape=jax.ShapeDtypeStruct((n, r, c), out_dtype),
        compiler_params=_cparams(("parallel", "parallel")),
    )(core, g, got)


def sum_slabs(p, name):
    n, r, c = p.shape
    tr = _tile(r, [t for t in (512, 256, 128, 64, 32, 16) if n * t * c * p.dtype.itemsize <= (1 << 23)])

    def body(p_r, out_r):
        acc = p_r[0].astype(F32)
        for k in range(1, n):
            acc = acc + p_r[k].astype(F32)
        out_r[...] = acc

    return pl.pallas_call(
        body, name=name, grid=(r // tr,), in_specs=[pl.BlockSpec((n, tr, c), lambda i: (0, i, 0))],
        out_specs=pl.BlockSpec((tr, c), lambda i: (i, 0)), out_shape=jax.ShapeDtypeStruct((r, c), F32),
        compiler_params=_cparams(("parallel",)),
    )(p)


def _lay(arr, axis, pieces, total, reps=()):
    items = [(d, n, lax.slice_in_dim(arr, s0, s0 + n, axis=axis)) for s0, n, d in pieces]
    items += [(d, n, jnp.repeat(lax.slice_in_dim(arr, s0, s0 + 1, axis=axis), n, axis=axis)) for s0, d, n in reps]
    items.sort(key=lambda t: t[0])
    parts, pos = [], 0

    def zeros(n):
        sh = list(arr.shape)
        sh[axis] = n
        return jnp.zeros(sh, arr.dtype)

    for d, n, v in items:
        if d > pos:
            parts.append(zeros(d - pos))
        parts.append(v)
        pos = d + n
    if total > pos:
        parts.append(zeros(total - pos))
    return jnp.concatenate(parts, axis=axis) if len(parts) > 1 else parts[0]


def _unlay_parts(g, axis, pieces, reps=()):
    out = [(s0, lax.slice_in_dim(g, d, d + n, axis=axis)) for s0, n, d in pieces]
    out += [(s0, jnp.sum(lax.slice_in_dim(g, d, d + n, axis=axis), axis=axis, keepdims=True)) for s0, d, n in reps]
    return out


def _join(parts, axis):
    parts = sorted(parts, key=lambda t: t[0])
    return jnp.concatenate([p for _, p in parts], axis=axis)


def _heads(src0, n_heads, width, padded, dst0=0):
    return [(src0 + h * width, width, dst0 + h * padded) for h in range(n_heads)]


_XQ = lambda src0: _heads(src0, XA_HEADS, XA_HD, LANE)
_XA_W = XA_HEADS * LANE

LAYOUT = {
    'a': dict(
        segs=dict(q=(_heads(0, 4, 96, LANE), 512, ()), k=(_heads(384, 4, 96, LANE), 512, ()),
                  v=(_heads(768, 4, 192, 256), 1024, ()), glr=([(1536, 16, 0)], LANE, ()),
                  og=(_heads(1552, 4, 192, 256), 1024, ()), xq=(_XQ(2320), _XA_W, ())),
        tok=(_heads(0, 4, 192, 256), 1024), xa=(_XQ(768), _XA_W)),
    'b': dict(
        segs=dict(**{f"q{g}": ([(512 * g, 512, 0)], 512, ()) for g in range(3)},
                  **{f"k{g}": ([(1536 + 512 * g, 512, 0)], 512, ()) for g in range(3)},
                  **{f"v{g}": ([(3072 + 512 * g, 512, 0)], 512, ()) for g in range(3)},
                  xq=(_XQ(4608), _XA_W, ())),
        tok=([(0, 512, 0)], 512), xa=(_XQ(512), _XA_W)),
    'c': dict(
        segs=dict(z=(_heads(0, 12, 64, LANE), 1536, ()),
                  xbc=(_heads(768, 12, 64, LANE) + [(1536, 256, 1536), (1792, 256, 1792)], 2048, ()),
                  dt=([], 1536, tuple((2048 + h, h * LANE, LANE) for h in range(12))),
                  xq=(_XQ(2060), _XA_W, ())),
        tok=(_heads(0, 12, 64, LANE), 1536), xa=(_XQ(768), _XA_W)),
    'd': dict(
        segs=dict(q=([(0, 768, 0)], 768, ()), f=([(768, 768, 0)], 768, ()), i=([(1536, 768, 0)], 768, ()),
                  og=([(2304, 768, 0)], 768, ()), xq=(_XQ(3072), _XA_W, ())),
        tok=([(0, 768, 0)], 768), xa=(_XQ(768), _XA_W)),
}
KINDS = 'abcd'
_XS_PIECES = _heads(0, 12, 64, LANE)
_XBC_PIECES = _XS_PIECES + [(768, 256, 1536), (1024, 256, 1792)]
_HEAD_REPS = tuple((h, h * LANE, LANE) for h in range(12))


def _row(v):
    return v.reshape(1, -1)


def local_step(x, mem, positions, target, W):
    s = x.shape[0]
    grads = {}
    scan_block = CHUNK * SCAN_CHUNKS

    inv_freq = ROPE_THETA ** (-jnp.arange(DIL_HD // 2, dtype=F32) / (DIL_HD // 2))
    ang = positions.astype(F32)[:, None] * inv_freq
    cosf = jnp.concatenate([jnp.cos(ang), jnp.cos(ang)], axis=-1)
    sinf = jnp.concatenate([-jnp.sin(ang), jnp.sin(ang)], axis=-1)

    mem_g = _row(W['mem_norm'])
    (mem_n,) = tmap("mem_norm", _norm_stage, [mem], [mem_g], [(D_MODEL, MXU_DTYPE)])
    kv_lay = _heads(0, 4, 64, LANE) + _heads(256, 4, 64, LANE, dst0=_XA_W)

    saved = []
    for i in range(4):
        kind = KINDS[i]
        lay = LAYOUT[kind]
        sv = dict(x0=x)
        w_in = W[f'{kind}_w_in']
        w_out = W[f'{kind}_w_out']
        sv['w_seg'] = {n: _lay(w_in, 1, p, t, r).astype(MXU_DTYPE) for n, (p, t, r) in lay['segs'].items()}
        sv['wo_tok'] = _lay(w_out, 0, *lay['tok']).astype(MXU_DTYPE)
        sv['wo_xa'] = _lay(w_out, 0, *lay['xa']).astype(MXU_DTYPE)
        sv['w_kv'] = _lay(W['xa_w_kv'][i], 1, kv_lay, 2 * _XA_W).astype(MXU_DTYPE)
        sv['g1'] = _row(W['mix_norm'][i])
        (h,) = tmap(f"mix_norm_{i}", _norm_stage, [x], [sv['g1']], [(D_MODEL, MXU_DTYPE)])
        sv['h'] = h
        seg = {n: matmul(h, w) for n, w in sv['w_seg'].items()}
        sv['seg'] = seg

        if kind == 'a':
            sv['w2'] = _lay(_lay(W['a_w_gate2'], 1, _heads(0, 4, 96, LANE), 512), 0, [(0, 16, 0)], LANE)
            sv['bg'] = _row(_lay(W['a_b_gate'], 0, _heads(0, 4, 96, LANE), 512))
            sv['on'] = _row(_lay(W['a_o_norm'], 0, [(0, 192, 0)], 256))
            (la,) = tmap("gla_pre", _gla_pre, [seg['glr']], [sv['w2'], sv['bg']], [(512, F32)])
            sv['la'] = la
            sv['scan_fn'] = _gla_step(GLA_HEADS, LANE, 2 * LANE, GLA_DK ** -0.5)
            sv['scan_rows'] = [seg['q'], seg['k'], seg['v'], la]
            (o,), sv['states'] = rscan("gla_scan", sv['scan_fn'], [(LANE, 2 * LANE)] * GLA_HEADS, sv['scan_rows'], [],
                                       [(1024, F32)], scan_block)
            sv['o'] = o
            (tok,) = tmap("gla_post", _gla_post, [o, seg['og']], [sv['on']], [(1024, MXU_DTYPE)])
        elif kind == 'b':
            sv['qg'], sv['kg'] = _row(W['b_q_norm']), _row(W['b_k_norm'])
            sv['qn'], sv['kn'], os_, ls_ = [], [], [], []
            for g, (window, r) in enumerate(DIL_GROUPS):
                assert window // r == DIL_BLOCK and (s // r) % DIL_BLOCK == 0
                qn, kn = tmap(f"dil_pre_{g}", _dil_pre, [seg[f'q{g}'], seg[f'k{g}'], cosf, sinf], [sv['qg'], sv['kg']],
                              [(512, F32), (512, F32)])
                o, lse = dil_attn(f"dil_attn_{g}", qn, kn, seg[f'v{g}'], r)
                sv['qn'].append(qn)
                sv['kn'].append(kn)
                os_.append(o)
                ls_.append(lse)
            sv['os'], sv['ls'] = os_, ls_
            (tok,) = tmap("dil_merge", _dil_merge, os_ + ls_, [], [(512, MXU_DTYPE)])
        elif kind == 'c':
            sv['cw'] = _lay(W['c_conv_w'], 1, _XBC_PIECES, 2048)
            sv['cb'] = _row(_lay(W['c_conv_b'], 0, _XBC_PIECES, 2048))
            sv['dtb'] = _row(_lay(W['c_dt_bias'], 0, [], 1536, _HEAD_REPS))
            sv['alog'] = _row(_lay(W['c_a_log'], 0, [], 1536, _HEAD_REPS))
            sv['dsk'] = _row(_lay(W['c_d'], 0, [], 1536, _HEAD_REPS))
            sv['cn'] = _row(_lay(W['c_norm'], 0, _XS_PIECES, 1536))
            xact = conv_fwd("ssm_conv", seg['xbc'], sv['cw'], sv['cb'], 'silu', F32, 512)
            sv['xact'] = xact
            sv['scan_rows'] = [xact, seg['dt']]
            sv['scan_params'] = [sv['dtb'], sv['alog'], sv['dsk']]
            (yv,), sv['states'] = rscan("ssd_scan", _ssd_step, [(LANE, LANE)] * SSM_HEADS, sv['scan_rows'],
                                        sv['scan_params'], [(1536, F32)], scan_block)
            sv['y'] = yv
            (tok,) = tmap("ssd_post", _mamba_post, [yv, seg['z']], [sv['cn']], [(1536, MXU_DTYPE)])
        else:
            sv['lbnd'] = W['d_lower_bounds']
            sv['on'] = _row(W['d_o_norm'])
            qq, kk, la = tmap("hgrn_pre", _hgrn_pre, [seg['q'], seg['f']], [sv['lbnd']], [(768, F32)] * 3)
            sv['scan_fn'] = _gla_step(HGRN_HEADS, LANE, LANE, 1.0)
            sv['scan_rows'] = [qq, kk, seg['i'], la]
            (o,), sv['states'] = rscan("hgrn_scan", sv['scan_fn'], [(LANE, LANE)] * HGRN_HEADS, sv['scan_rows'], [],
                                       [(768, F32)], scan_block)
            sv['o'] = o
            (tok,) = tmap("hgrn_post", _hgrn_post, [o, seg['og']], [sv['on']], [(768, MXU_DTYPE)])
        sv['tok'] = tok

        kv = matmul(mem_n, sv['w_kv'])
        sv['kv'] = kv
        sv['xqg'] = _row(_lay(W['xa_q_norm'][i], 0, [(0, 64, 0)], LANE))
        sv['xkg'] = _row(_lay(W['xa_k_norm'][i], 0, [(0, 64, 0)], LANE))
        (xa,) = tmap(f"xattn_{i}", _xattn, [seg['xq']], [kv, sv['xqg'], sv['xkg']], [(_XA_W, MXU_DTYPE)])
        sv['xa'] = xa
        x = matmul(tok, sv['wo_tok'], add=x)
        x = matmul(xa, sv['wo_xa'], add=x)
        sv['x1'] = x

        sv['g2'] = _row(W['ffn_norm'][i])
        sv['w_up'] = W['ffn_w_up'][i].astype(MXU_DTYPE)
        sv['w_down'] = W['ffn_w_down'][i].astype(MXU_DTYPE)
        sv['fcw'] = W['ffn_conv_w'][i]
        sv['fcb'] = _row(W['ffn_conv_b'][i])
        (h2,) = tmap(f"ffn_norm_{i}", _norm_stage, [x], [sv['g2']], [(D_MODEL, MXU_DTYPE)])
        sv['h2'] = h2
        u0 = matmul(h2, sv['w_up'])
        sv['u0'] = u0
        act = conv_fwd("ffn_conv", u0, sv['fcw'], sv['fcb'], 'glu', MXU_DTYPE, 1408)
        sv['act'] = act
        x = matmul(act, sv['w_down'], add=x)
        saved.append(sv)

    loss_acc, dx = loss_head(x, target)

    g_stack = {n: [None] * 4 for n in ('mix_norm', 'xa_w_kv', 'xa_q_norm', 'xa_k_norm', 'ffn_norm', 'ffn_w_up',
                                        'ffn_conv_w', 'ffn_conv_b', 'ffn_w_down')}
    d_memn = None
    for i in reversed(range(4)):
        kind = KINDS[i]
        lay = LAYOUT[kind]
        sv = saved[i]
        seg = sv['seg']
        dact = matmul(dx, sv['w_down'], tb=True)
        g_stack['ffn_w_down'][i] = matmul(sv['act'], dx, ta=True)
        du0, dcw, dcb = conv_bwd("ffn_conv_bwd", sv['u0'], sv['fcw'], sv['fcb'], dact, 'glu', 1408)
        g_stack['ffn_conv_w'][i], g_stack['ffn_conv_b'][i] = dcw, dcb[0]
        dh2 = matmul(du0, sv['w_up'], tb=True)
        g_stack['ffn_w_up'][i] = matmul(sv['h2'], du0, ta=True)
        (dx,), (dg2,) = tmap_bwd(f"ffn_norm_bwd_{i}", _norm_stage, [sv['x1']], [sv['g2']], [dh2], [True], {0: dx})
        g_stack['ffn_norm'][i] = dg2[0]
        dtok = matmul(dx, sv['wo_tok'], tb=True)
        dxa = matmul(dx, sv['wo_xa'], tb=True)
        g_wo = _unlay_parts(matmul(sv['tok'], dx, ta=True), 0, lay['tok'][0]) \
            + _unlay_parts(matmul(sv['xa'], dx, ta=True), 0, lay['xa'][0])
        grads[f'{kind}_w_out'] = _join(g_wo, 0)
        (dxq,), (dkv, dqg, dkg) = tmap_bwd(f"xattn_bwd_{i}", _xattn, [seg['xq']], [sv['kv'], sv['xqg'], sv['xkg']],
                                           [dxa], [True])
        g_stack['xa_q_norm'][i], g_stack['xa_k_norm'][i] = dqg[0, :XA_HD], dkg[0, :XA_HD]
        g_stack['xa_w_kv'][i] = _join(_unlay_parts(matmul(mem_n, dkv, ta=True), 1, kv_lay), 1)
        d_memn = matmul(dkv, sv['w_kv'], tb=True, add=d_memn)
        dseg = dict(xq=dxq)
        if kind == 'a':
            (do, dog), (don,) = tmap_bwd("gla_post_bwd", _gla_post, [sv['o'], seg['og']], [sv['on']], [dtok],
                                         [True, True], grad_dtype=F32)
            grads['a_o_norm'] = don[0, :GLA_DV]
            (dq, dk, dv, dla), _ = rscan_bwd("gla_scan_bwd", sv['scan_fn'], sv['states'], sv['scan_rows'], [], [do],
                                             scan_block, grad_dtype=F32)
            (dglr,), (dw2, dbg) = tmap_bwd("gla_pre_bwd", _gla_pre, [seg['glr']], [sv['w2'], sv['bg']], [dla], [True])
            grads['a_w_gate2'] = _join(_unlay_parts(dw2[:GLA_RANK], 1, _heads(0, 4, 96, LANE)), 1)
            grads['a_b_gate'] = _join(_unlay_parts(dbg[0], 0, _heads(0, 4, 96, LANE)), 0)
            dseg.update(q=dq, k=dk, v=dv, glr=dglr, og=dog)
        elif kind == 'b':
            res, _ = tmap_bwd("dil_merge_bwd", _dil_merge, sv['os'] + sv['ls'], [], [dtok], [True] * 6, grad_dtype=F32)
            dqg = dkg = 0.0
            for g, (_, r) in enumerate(DIL_GROUPS):
                dqn, dkn, dv = dil_attn_bwd(f"dil_attn_bwd_{g}", sv['qn'][g], sv['kn'][g], seg[f'v{g}'], res[g],
                                            res[3 + g], r)
                (dq, dk), (a_, b_) = tmap_bwd(f"dil_pre_bwd_{g}", _dil_pre, [seg[f'q{g}'], seg[f'k{g}'], cosf, sinf],
                                              [sv['qg'], sv['kg']], [dqn, dkn], [True, True, False, False])
                dqg, dkg = dqg + a_, dkg + b_
                dseg.update({f'q{g}': dq, f'k{g}': dk, f'v{g}': dv})
            grads['b_q_norm'], grads['b_k_norm'] = dqg[0], dkg[0]
        elif kind == 'c':
            (dy, dz), (dcn,) = tmap_bwd("ssd_post_bwd", _mamba_post, [sv['y'], seg['z']], [sv['cn']], [dtok],
                                        [True, True], grad_dtype=F32)
            grads['c_norm'] = _join(_unlay_parts(dcn[0], 0, _XS_PIECES), 0)
            (dxact, ddt), (ddtb, dalog, ddsk) = rscan_bwd("ssd_scan_bwd", _ssd_step, sv['states'], sv['scan_rows'],
                                                          sv['scan_params'], [dy], scan_block, grad_dtype=F32)
            for nm, gv in (('c_dt_bias', ddtb), ('c_a_log', dalog), ('c_d', ddsk)):
                grads[nm] = _join(_unlay_parts(gv[0], 0, [], _HEAD_REPS), 0)
            dxbc, dcw, dcb = conv_bwd("ssm_conv_bwd", seg['xbc'], sv['cw'], sv['cb'], dxact, 'silu', 512)
            grads['c_conv_w'] = _join(_unlay_parts(dcw, 1, _XBC_PIECES), 1)
            grads['c_conv_b'] = _join(_unlay_parts(dcb[0], 0, _XBC_PIECES), 0)
            dseg.update(z=dz, xbc=dxbc, dt=ddt)
        else:
            (do, dog), (don,) = tmap_bwd("hgrn_post_bwd", _hgrn_post, [sv['o'], seg['og']], [sv['on']], [dtok],
                                         [True, True], grad_dtype=F32)
            grads['d_o_norm'] = don[0]
            (dqq, dkk, di, dla), _ = rscan_bwd("hgrn_scan_bwd", sv['scan_fn'], sv['states'], sv['scan_rows'], [], [do],
                                               scan_block, grad_dtype=F32)
            (dq, df), (dlb,) = tmap_bwd("hgrn_pre_bwd", _hgrn_pre, [seg['q'], seg['f']], [sv['lbnd']], [dqq, dkk, dla],
                                        [True, True])
            grads['d_lower_bounds'] = dlb
            dseg.update(q=dq, f=df, i=di, og=dog)
        dh = None
        g_in = []
        for n, (p, t, rp) in lay['segs'].items():
            dh = matmul(dseg[n], sv['w_seg'][n], tb=True, add=dh)
            g_in += _unlay_parts(matmul(sv['h'], dseg[n], ta=True), 1, p, rp)
        grads[f'{kind}_w_in'] = _join(g_in, 1)
        (dx,), (dg1,) = tmap_bwd(f"mix_norm_bwd_{i}", _norm_stage, [sv['x0']], [sv['g1']], [dh], [True], {0: dx})
        g_stack['mix_norm'][i] = dg1[0]

    _, (dmg,) = tmap_bwd("mem_norm_bwd", _norm_stage, [mem], [mem_g], [d_memn], [False])
    grads['mem_norm'] = dmg[0]
    for n, parts in g_stack.items():
        grads[n] = jnp.stack(parts)
    return loss_acc, dx, grads


def _pack(arrs, dtype, row_multiple=PACK_ROWS):
    parts, rows = [], 0
    for a in arrs:
        f = a.reshape(-1).astype(dtype)
        unit = PACK_ROWS * PACK_COLS
        pad = (-f.shape[0]) % unit
        if pad:
            f = jnp.concatenate([f, jnp.zeros((pad,), dtype)])
        parts.append(f.reshape(-1, PACK_COLS))
        rows += parts[-1].shape[0]
    if rows % row_multiple:
        parts.append(jnp.zeros((row_multiple - rows % row_multiple, PACK_COLS), dtype))
    return jnp.concatenate(parts, axis=0)


def _unpack(buf, shapes):
    out, row = [], 0
    for sh in shapes:
        n = int(np.prod(sh))
        rows = -(-n // (PACK_ROWS * PACK_COLS)) * PACK_ROWS
        out.append(buf[row:row + rows].reshape(-1)[:n].reshape(sh))
        row += rows
    return out


def _shard_shape(full, axis):
    sh = list(full)
    sh[axis] //= N_CHIPS
    return tuple(sh)


def _split_chips(a, axis):
    sh = a.shape
    return jnp.moveaxis(a.reshape(sh[:axis] + (N_CHIPS, sh[axis] // N_CHIPS) + sh[axis + 1:]), axis, 0)


def _merge_chips(a, axis):
    a = jnp.moveaxis(a, 0, axis)
    sh = a.shape
    return a.reshape(sh[:axis] + (sh[axis] * sh[axis + 1],) + sh[axis + 2:])


def kernel(x, mem, positions, mem_norm, mix_norm, xa_w_kv, xa_q_norm, xa_k_norm, ffn_norm, ffn_w_up, ffn_conv_w, ffn_conv_b, ffn_w_down, a_w_in, a_w_gate2, a_b_gate, a_o_norm, a_w_out, b_w_in, b_q_norm, b_k_norm, b_w_out, c_w_in, c_conv_w, c_conv_b, c_dt_bias, c_a_log, c_d, c_norm, c_w_out, d_w_in, d_lower_bounds, d_o_norm, d_w_out, loss_target, m_mem_norm, m_mix_norm, m_xa_w_kv, m_xa_q_norm, m_xa_k_norm, m_ffn_norm, m_ffn_w_up, m_ffn_conv_w, m_ffn_conv_b, m_ffn_w_down, m_a_w_in, m_a_w_gate2, m_a_b_gate, m_a_o_norm, m_a_w_out, m_b_w_in, m_b_q_norm, m_b_k_norm, m_b_w_out, m_c_w_in, m_c_conv_w, m_c_conv_b, m_c_dt_bias, m_c_a_log, m_c_d, m_c_norm, m_c_w_out, m_d_w_in, m_d_lower_bounds, m_d_o_norm, m_d_w_out, v_mem_norm, v_mix_norm, v_xa_w_kv, v_xa_q_norm, v_xa_k_norm, v_ffn_norm, v_ffn_w_up, v_ffn_conv_w, v_ffn_conv_b, v_ffn_w_down, v_a_w_in, v_a_w_gate2, v_a_b_gate, v_a_o_norm, v_a_w_out, v_b_w_in, v_b_q_norm, v_b_k_norm, v_b_w_out, v_c_w_in, v_c_conv_w, v_c_conv_b, v_c_dt_bias, v_c_a_log, v_c_d, v_c_norm, v_c_w_out, v_d_w_in, v_d_lower_bounds, v_d_o_norm, v_d_w_out):
    args = locals()
    w = {n: args[n] for n in WEIGHTS}
    m = {n: args['m_' + n] for n in WEIGHTS}
    v = {n: args['v_' + n] for n in WEIGHTS}
    cx, cy, cc = lax.axis_index("x"), lax.axis_index("y"), lax.axis_index("c")
    chip = 2 * cx + cy

    big_shard_shapes = [w[n].shape for n in BIG]
    gathered = allgather_chips(_pack([w[n] for n in BIG], MXU_DTYPE))
    per_chip = [_unpack(gathered[j], big_shard_shapes) for j in range(N_CHIPS)]
    full = {n: _merge_chips(jnp.stack([per_chip[j][k] for j in range(N_CHIPS)]), SHARD_AXIS[n])
            for k, n in enumerate(BIG)}
    small_sharded = [n for n in SMALL if n in SHARD_AXIS]
    sg = allgather_devices(_pack([w[n] for n in small_sharded], F32))
    per_chip_s = [_unpack(sg[2 * j], [w[n].shape for n in small_sharded]) for j in range(N_CHIPS)]
    for k, n in enumerate(small_sharded):
        full[n] = _merge_chips(jnp.stack([per_chip_s[j][k] for j in range(N_CHIPS)]), SHARD_AXIS[n])
    for n in SMALL:
        if n not in SHARD_AXIS:
            full[n] = w[n]

    loss_acc, dx, grads = local_step(x[0], mem[0], positions[0], loss_target[0], full)
    loss = lax.psum(jnp.sum(loss_acc), ("x", "y", "c"))

    gb = jnp.stack([_pack([_split_chips(grads[n], SHARD_AXIS[n])[j] for n in BIG], F32, 1024)
                    for j in range(N_CHIPS)])
    rows = gb.shape[1]
    gb = gb.reshape(N_CHIPS, 2, rows // 2, PACK_COLS)
    got = swap_halves_sibling(gb)
    pair = add_own_half(gb, got, cc.reshape(1).astype(jnp.int32), GRAD_WIRE_DTYPE)
    half = sum_slabs(exchange_chips(pair), "sum_chips")
    red = join_halves_sibling(half).reshape(rows, PACK_COLS)
    g_big = dict(zip(BIG, _unpack(red, big_shard_shapes)))

    small_full_shapes = [grads[n].shape for n in SMALL]
    gs = sum_slabs(allgather_devices(_pack([grads[n] for n in SMALL], F32, 128)), "sum_devices")
    g_small = {}
    for n, gfull in zip(SMALL, _unpack(gs, small_full_shapes)):
        if n in SHARD_AXIS:
            ax = SHARD_AXIS[n]
            size = gfull.shape[ax] // N_CHIPS
            gfull = lax.dynamic_slice_in_dim(gfull, chip * size, size, axis=ax)
        g_small[n] = gfull

    g_out, delta, new_m, new_v = {}, {}, {}, {}
    for n in BIG:
        sh = w[n].shape
        two_d = (-1, sh[-1])
        d_, m_, v_ = adamw(f"adamw_{n}", w[n].reshape(two_d), g_big[n].reshape(two_d), m[n].reshape(two_d),
                           v[n].reshape(two_d))
        g_out[n], delta[n], new_m[n], new_v[n] = g_big[n], d_.reshape(sh), m_.reshape(sh), v_.reshape(sh)
    small_shapes = [w[n].shape for n in SMALL]
    d_, m_, v_ = adamw("adamw_small", _pack([w[n] for n in SMALL], F32), _pack([g_small[n] for n in SMALL], F32),
                       _pack([m[n] for n in SMALL], F32), _pack([v[n] for n in SMALL], F32))
    for n, a_, b_, c_ in zip(SMALL, _unpack(d_, small_shapes), _unpack(m_, small_shapes), _unpack(v_, small_shapes)):
        g_out[n], delta[n], new_m[n], new_v[n] = g_small[n], a_, b_, c_

    return (loss, dx[None], *[g_out[n] for n in WEIGHTS], *[delta[n] for n in WEIGHTS],
            *[new_m[n] for n in WEIGHTS], *[new_v[n] for n in WEIGHTS])
```

```python
import functools
import math

import jax
import jax.numpy as jnp
import numpy as np
from jax import lax
from jax.experimental import pallas as pl
from jax.experimental.pallas import tpu as pltpu

F32 = jnp.float32
MXU_DTYPE = jnp.bfloat16
GRAD_WIRE_DTYPE = jnp.bfloat16
VMEM_LIMIT_V7X = 56 * 1024 * 1024
LANE = 128
SUBLANE = 8

D_MODEL = 1024
N_MEM = 256
EPS = 1e-6
ROPE_THETA = 10000.0
CHUNK = 64
XA_HEADS, XA_HD = 4, 64
GLA_HEADS, GLA_DK, GLA_DV, GLA_RANK, GLA_GATE_NORM = 4, 96, 192, 16, 16.0
DIL_GROUPS = ((128, 1), (512, 4), (2048, 16))
DIL_HEADS, DIL_HD, DIL_BLOCK = 4, 128, 128
SSM_HD, SSM_HEADS, SSM_GROUPS, SSM_STATE, SSM_CONV = 64, 12, 2, 128, 4
HGRN_HEADS, HGRN_DK = 6, 128
D_FF = 2816
FFN_CONV = 3
ADAM_LR, ADAM_B1, ADAM_B2, ADAM_EPS, ADAM_WD, ADAM_STEP = 0.001, 0.9, 0.999, 1e-08, 0.01, 10

MM_TILES = (1408, 1024, 768, 512, 384, 256, 128)
MM_VMEM_BUDGET = 40 * 1024 * 1024
ROW_BLOCK = 256
SCAN_CHUNKS = 2
PACK_COLS = 1024
PACK_ROWS = 32

WEIGHTS = ['mem_norm', 'mix_norm', 'xa_w_kv', 'xa_q_norm', 'xa_k_norm', 'ffn_norm', 'ffn_w_up', 'ffn_conv_w',
           'ffn_conv_b', 'ffn_w_down', 'a_w_in', 'a_w_gate2', 'a_b_gate', 'a_o_norm', 'a_w_out', 'b_w_in', 'b_q_norm',
           'b_k_norm', 'b_w_out', 'c_w_in', 'c_conv_w', 'c_conv_b', 'c_dt_bias', 'c_a_log', 'c_d', 'c_norm', 'c_w_out',
           'd_w_in', 'd_lower_bounds', 'd_o_norm', 'd_w_out']
SHARD_AXIS = {'xa_w_kv': 1, 'ffn_w_up': 2, 'ffn_conv_w': 2, 'ffn_w_down': 1, 'a_w_in': 1, 'a_w_gate2': 1, 'a_w_out': 0,
              'b_w_in': 1, 'b_w_out': 1, 'c_w_in': 1, 'c_conv_w': 1, 'c_w_out': 0, 'd_w_in': 1, 'd_w_out': 0}
BIG = ['xa_w_kv', 'ffn_w_up', 'ffn_w_down', 'a_w_in', 'a_w_gate2', 'a_w_out', 'b_w_in', 'b_w_out', 'c_w_in', 'c_w_out',
       'd_w_in', 'd_w_out']
SMALL = [n for n in WEIGHTS if n not in BIG]
N_CHIPS = 4
N_DEV = 8


class _MatmulSet:
    def __init__(self, cast, precision):
        def dot(a, b, dims):
            if cast:
                a = a.astype(MXU_DTYPE)
                b = b.astype(MXU_DTYPE)
            return lax.dot_general(a, b, (dims, ((), ())), precision=precision, preferred_element_type=F32)

        @jax.custom_vjp
        def nn(a, b):
            return dot(a, b, ((1,), (0,)))

        @jax.custom_vjp
        def nt(a, b):
            return dot(a, b, ((1,), (1,)))

        @jax.custom_vjp
        def tn(a, b):
            return dot(a, b, ((0,), (0,)))

        nn.defvjp(lambda a, b: (nn(a, b), (a, b)), lambda r, g: (nt(g, r[1]), tn(r[0], g)))
        nt.defvjp(lambda a, b: (nt(a, b), (a, b)), lambda r, g: (nn(g, r[1]), tn(g, r[0])))
        tn.defvjp(lambda a, b: (tn(a, b), (a, b)), lambda r, g: (nt(r[1], g), nn(r[0], g)))
        self.nn, self.nt, self.tn = nn, nt, tn


mm = _MatmulSet(True, None)
hi = _MatmulSet(False, lax.Precision.HIGHEST)


def _sigmoid(x):
    return jax.nn.sigmoid(x)


def _silu(x):
    return x * jax.nn.sigmoid(x)


def _softplus(x):
    return jnp.maximum(x, 0.0) + jnp.log1p(jnp.exp(-jnp.abs(x)))


def _rms(x, g, n_real=None):
    n = n_real or x.shape[-1]
    ms = jnp.sum(x * x, axis=-1, keepdims=True) * (1.0 / n)
    return x * lax.rsqrt(ms + EPS) * g


@jax.custom_vjp
def _swap_halves(x):
    return pltpu.roll(x, 64, 1)


_swap_halves.defvjp(lambda x: (_swap_halves(x), None), lambda _, g: (_swap_halves(g),))


def _tile(n, cands):
    for c in cands:
        if n % c == 0:
            return c
    raise ValueError(f"no tile for {n} among {cands}")


def _cparams(sem):
    return pltpu.CompilerParams(dimension_semantics=sem, vmem_limit_bytes=VMEM_LIMIT_V7X)


def _f32(v):
    return v.astype(F32) if jnp.issubdtype(v.dtype, jnp.floating) else v


def matmul(a, b, *, ta=False, tb=False, add=None, out_dtype=F32):
    m, k = (a.shape[1], a.shape[0]) if ta else a.shape
    n = b.shape[0] if tb else b.shape[1]
    assert k == (b.shape[1] if tb else b.shape[0]), (a.shape, b.shape, ta, tb)
    tk = _tile(k, MM_TILES)
    nk = k // tk
    sa, sb, so = a.dtype.itemsize, b.dtype.itemsize, jnp.dtype(out_dtype).itemsize

    def vmem(tm_, tn_):
        return (2 * tm_ * tk * sa + 2 * tk * tn_ * sb + 2 * tm_ * tn_ * so + (tm_ * tn_ * 4 if nk > 1 else 0)
                + (2 * tm_ * tn_ * add.dtype.itemsize if add is not None else 0))

    fits = [(tm_ * tn_, tm_, tn_) for tm_ in MM_TILES if m % tm_ == 0 for tn_ in MM_TILES if n % tn_ == 0
            if vmem(tm_, tn_) <= MM_VMEM_BUDGET]
    _, tm, tn = max(fits)
    dims = (((0,) if ta else (1,)), ((1,) if tb else (0,)))

    def body(*refs):
        a_ref, b_ref = refs[0], refs[1]
        add_ref = refs[2] if add is not None else None
        o_ref = refs[3] if add is not None else refs[2]
        part = lax.dot_general(a_ref[...].astype(MXU_DTYPE), b_ref[...].astype(MXU_DTYPE), (dims, ((), ())),
                               preferred_element_type=F32)

        def finish(r):
            if add_ref is not None:
                r = r + add_ref[...].astype(F32)
            o_ref[...] = r.astype(o_ref.dtype)

        if nk == 1:
            finish(part)
            return
        acc = refs[-1]
        kk = pl.program_id(2)

        @pl.when(kk == 0)
        def _():
            acc[...] = part

        @pl.when(kk > 0)
        def _():
            acc[...] += part

        @pl.when(kk == nk - 1)
        def _():
            finish(acc[...])

    a_spec = pl.BlockSpec((tk, tm), lambda i, j, q: (q, i)) if ta else pl.BlockSpec((tm, tk), lambda i, j, q: (i, q))
    b_spec = pl.BlockSpec((tn, tk), lambda i, j, q: (j, q)) if tb else pl.BlockSpec((tk, tn), lambda i, j, q: (q, j))
    o_spec = pl.BlockSpec((tm, tn), lambda i, j, q: (i, j))
    ins, specs = [a, b], [a_spec, b_spec]
    if add is not None:
        ins.append(add)
        specs.append(o_spec)
    return pl.pallas_call(
        body, name=f"mm_{m}x{k}x{n}_{int(ta)}{int(tb)}{int(add is not None)}",
        grid=(m // tm, n // tn, nk), in_specs=specs, out_specs=o_spec,
        out_shape=jax.ShapeDtypeStruct((m, n), out_dtype),
        scratch_shapes=[pltpu.VMEM((tm, tn), F32)] if nk > 1 else [],
        compiler_params=_cparams(("parallel", "parallel", "arbitrary")),
    )(*ins)


def _row_spec(a, block):
    return pl.BlockSpec((block, a.shape[1]), lambda i: (i, 0))


def _whole_spec(a):
    return pl.BlockSpec(a.shape, lambda i: (0,) * a.ndim)


def tmap(name, fn, rows, params, outs, block=ROW_BLOCK):
    s = rows[0].shape[0]
    block = min(block, s)
    nr, npar = len(rows), len(params)

    def body(*refs):
        res = fn(*[_f32(r[...]) for r in refs[:nr]], *[_f32(p[...]) for p in refs[nr:nr + npar]])
        for o_ref, v in zip(refs[nr + npar:], res, strict=True):
            o_ref[...] = v.astype(o_ref.dtype)

    return pl.pallas_call(
        body, name=name, grid=(s // block,),
        in_specs=[_row_spec(a, block) for a in rows] + [_whole_spec(p) for p in params],
        out_specs=[pl.BlockSpec((block, w), lambda i: (i, 0)) for w, _ in outs],
        out_shape=[jax.ShapeDtypeStruct((s, w), dt) for w, dt in outs],
        compiler_params=_cparams(("parallel",)),
    )(*rows, *params)


def tmap_bwd(name, fn, rows, params, douts, row_grad, row_add=None, grad_dtype=None, block=ROW_BLOCK):
    s = rows[0].shape[0]
    block = min(block, s)
    grad_dtype = grad_dtype or MXU_DTYPE
    nr, npar, nd = len(rows), len(params), len(douts)
    gr = [i for i in range(nr) if row_grad[i]]
    row_add = row_add or {}
    adds = [row_add[i] for i in gr if i in row_add]

    def body(*refs):
        rv = [_f32(r[...]) for r in refs[:nr]]
        pv = [_f32(p[...]) for p in refs[nr:nr + npar]]
        dv = tuple(_f32(d[...]) for d in refs[nr + npar:nr + npar + nd])
        add_refs = list(refs[nr + npar + nd:nr + npar + nd + len(adds)])
        out_refs = refs[nr + npar + nd + len(adds):]

        def f(*diff):
            rr = list(rv)
            for n_, i_ in enumerate(gr):
                rr[i_] = diff[n_]
            return tuple(fn(*rr, *diff[len(gr):]))

        _, vjp = jax.vjp(f, *[rv[i_] for i_ in gr], *pv)
        g = vjp(dv)
        for n_, i_ in enumerate(gr):
            v = g[n_]
            if i_ in row_add:
                v = v + add_refs.pop(0)[...].astype(F32)
            out_refs[n_][...] = v.astype(out_refs[n_].dtype)
        first = pl.program_id(0) == 0
        for n_ in range(npar):
            ref = out_refs[len(gr) + n_]

            @pl.when(first)
            def _(ref=ref):
                ref[...] = jnp.zeros_like(ref)

            ref[...] += g[len(gr) + n_]

    res = pl.pallas_call(
        body, name=name, grid=(s // block,),
        in_specs=[_row_spec(a, block) for a in rows] + [_whole_spec(p) for p in params]
        + [_row_spec(d, block) for d in douts] + [_row_spec(a, block) for a in adds],
        out_specs=[_row_spec(rows[i], block) for i in gr] + [_whole_spec(p) for p in params],
        out_shape=[jax.ShapeDtypeStruct(rows[i].shape, F32 if i in row_add else grad_dtype) for i in gr]
        + [jax.ShapeDtypeStruct(p.shape, F32) for p in params],
        compiler_params=_cparams(("arbitrary",)),
    )(*rows, *params, *douts, *adds)
    return list(res[:len(gr)]), list(res[len(gr):])


def rscan(name, fn, state_shapes, rows, params, outs, block):
    s = rows[0].shape[0]
    nsteps = s // block
    nr, npar, no, ns = len(rows), len(params), len(outs), len(state_shapes)

    def body(*refs):
        out_refs = refs[nr + npar:nr + npar + no]
        sav_refs = refs[nr + npar + no:nr + npar + no + ns]
        st_refs = refs[nr + npar + no + ns:]

        @pl.when(pl.program_id(0) == 0)
        def _():
            for st in st_refs:
                st[...] = jnp.zeros_like(st)

        sts = tuple(st[...] for st in st_refs)
        for sv, v in zip(sav_refs, sts):
            sv[...] = v
        new, res = fn(sts, *[_f32(r[...]) for r in refs[:nr]], *[_f32(p[...]) for p in refs[nr:nr + npar]])
        for st, v in zip(st_refs, new, strict=True):
            st[...] = v
        for o_ref, v in zip(out_refs, res, strict=True):
            o_ref[...] = v.astype(o_ref.dtype)

    res = pl.pallas_call(
        body, name=name, grid=(nsteps,),
        in_specs=[_row_spec(a, block) for a in rows] + [_whole_spec(p) for p in params],
        out_specs=[pl.BlockSpec((block, w), lambda i: (i, 0)) for w, _ in outs]
        + [pl.BlockSpec(sh, lambda i: (i, 0)) for sh in state_shapes],
        out_shape=[jax.ShapeDtypeStruct((s, w), dt) for w, dt in outs]
        + [jax.ShapeDtypeStruct((nsteps * sh[0], sh[1]), F32) for sh in state_shapes],
        scratch_shapes=[pltpu.VMEM(sh, F32) for sh in state_shapes],
        compiler_params=_cparams(("arbitrary",)),
    )(*rows, *params)
    return list(res[:no]), list(res[no:])


def rscan_bwd(name, fn, saved, rows, params, douts, block, grad_dtype=None):
    s = rows[0].shape[0]
    nsteps = s // block
    grad_dtype = grad_dtype or MXU_DTYPE
    nr, npar, nd, ns = len(rows), len(params), len(douts), len(saved)
    state_shapes = [(sv.shape[0] // nsteps, sv.shape[1]) for sv in saved]

    def body(*refs):
        rv = [_f32(r[...]) for r in refs[:nr]]
        pv = [_f32(p[...]) for p in refs[nr:nr + npar]]
        dv = tuple(_f32(d[...]) for d in refs[nr + npar:nr + npar + nd])
        sv = tuple(x[...] for x in refs[nr + npar + nd:nr + npar + nd + ns])
        out_refs = refs[nr + npar + nd + ns:nr + npar + nd + ns + nr + npar]
        dst_refs = refs[nr + npar + nd + ns + nr + npar:]
        first = pl.program_id(0) == 0

        @pl.when(first)
        def _():
            for d in dst_refs:
                d[...] = jnp.zeros_like(d)

        def f(sts, *args):
            return fn(sts, *args)

        _, vjp = jax.vjp(f, sv, *rv, *pv)
        g = vjp((tuple(d[...] for d in dst_refs), dv))
        for d, v in zip(dst_refs, g[0], strict=True):
            d[...] = v
        for n_ in range(nr):
            out_refs[n_][...] = g[1 + n_].astype(out_refs[n_].dtype)
        for n_ in range(npar):
            ref = out_refs[nr + n_]

            @pl.when(first)
            def _(ref=ref):
                ref[...] = jnp.zeros_like(ref)

            ref[...] += g[1 + nr + n_]

    rev = lambda i: (nsteps - 1 - i, 0)
    res = pl.pallas_call(
        body, name=name, grid=(nsteps,),
        in_specs=[pl.BlockSpec((block, a.shape[1]), rev) for a in rows] + [_whole_spec(p) for p in params]
        + [pl.BlockSpec((block, d.shape[1]), rev) for d in douts] + [pl.BlockSpec(sh, rev) for sh in state_shapes],
        out_specs=[pl.BlockSpec((block, a.shape[1]), rev) for a in rows] + [_whole_spec(p) for p in params],
        out_shape=[jax.ShapeDtypeStruct(a.shape, grad_dtype) for a in rows]
        + [jax.ShapeDtypeStruct(p.shape, F32) for p in params],
        scratch_shapes=[pltpu.VMEM(sh, F32) for sh in state_shapes],
        compiler_params=_cparams(("arbitrary",)),
    )(*rows, *params, *douts, *saved)
    return list(res[:nr]), list(res[nr:])


def _norm_stage(x, g):
    return (_rms(x, g),)


def _tril():
    r = lax.broadcasted_iota(jnp.int32, (CHUNK, CHUNK), 0)
    c = lax.broadcasted_iota(jnp.int32, (CHUNK, CHUNK), 1)
    return r >= c


def _gla_chunk(st, q, k, v, la, b):
    tril = _tril()
    rowi = lax.broadcasted_iota(jnp.int32, (CHUNK, 1), 0)
    b_last = jnp.sum(la, axis=0, keepdims=True)
    b_ref = jnp.sum(jnp.where(rowi < CHUNK // 2, la, 0.0), axis=0, keepdims=True)
    att = mm.nt(q * jnp.exp(b - b_ref), k * jnp.exp(b_ref - b))
    att = jnp.where(tril, att, 0.0)
    o = mm.nn(att, v) + mm.nn(q * jnp.exp(b), st)
    decay = jnp.exp(jnp.broadcast_to(b_last, (LANE, LANE)).T)
    decay = jnp.concatenate([decay] * (v.shape[1] // LANE), axis=1)
    st2 = decay * st + mm.tn(k * jnp.exp(b_last - b), v)
    return st2, o


def _gla_step(heads, vp, scale):
    kp = LANE

    def fn(states, q, k, v, la):
        sts = list(states)
        trif = _tril().astype(F32)
        rows = []
        for c in range(q.shape[0] // CHUNK):
            r = slice(c * CHUNK, (c + 1) * CHUNK)
            b_all = hi.nn(trif, la[r])
            oh = []
            for h in range(heads):
                ks, vs = slice(h * kp, (h + 1) * kp), slice(h * vp, (h + 1) * vp)
                qh = q[r, ks] * scale if scale != 1.0 else q[r, ks]
                sts[h], o = _gla_chunk(sts[h], qh, k[r, ks], v[r, vs], la[r, ks], b_all[:, ks])
                oh.append(o)
            rows.append(jnp.concatenate(oh, axis=1))
        return tuple(sts), (jnp.concatenate(rows, axis=0),)

    return fn


def _ssd_step(states, xa, dtr, dtb, alog, dsk):
    sts = list(states)
    trif = _tril().astype(F32)
    wide = lax.broadcasted_iota(jnp.int32, (CHUNK, LANE), 0) >= lax.broadcasted_iota(jnp.int32, (CHUNK, LANE), 1)
    hg = SSM_HEADS // SSM_GROUPS
    xw = SSM_HEADS * LANE
    lane, head = lax.broadcasted_iota(jnp.int32, (LANE, xw), 1), lax.broadcasted_iota(jnp.int32, (LANE, xw), 0)
    spread = ((lane >= head * LANE) & (lane < (head + 1) * LANE)).astype(F32)
    neg_a = -jnp.exp(alog)
    pad = jnp.zeros((CHUNK, LANE), F32)
    rows = []
    for c in range(xa.shape[0] // CHUNK):
        r = slice(c * CHUNK, (c + 1) * CHUNK)
        dt_all = _softplus(hi.nn(dtr[r], spread) + dtb)
        a_all = dt_all * neg_a
        acs_all = hi.nn(trif, a_all)
        last_all = jnp.sum(a_all, axis=0, keepdims=True)
        yh = []
        for g in range(SSM_GROUPS):
            bm = xa[r, xw + g * LANE:xw + (g + 1) * LANE]
            cm = xa[r, xw + (SSM_GROUPS + g) * LANE:xw + (SSM_GROUPS + g + 1) * LANE]
            cb = mm.nt(cm, jnp.concatenate([bm, pad], axis=0))
            for hh in range(hg):
                h = g * hg + hh
                ls = slice(h * LANE, (h + 1) * LANE)
                xs, acs, acs_last = xa[r, ls], acs_all[:, ls], last_all[:, ls]
                xdt = xs * dt_all[:, ls]
                seg = acs - jnp.concatenate([acs, pad], axis=0).T[:CHUNK]
                lmat = jnp.exp(jnp.where(wide, seg, -1e30))
                y = (mm.nn(cb * lmat, jnp.concatenate([xdt, pad], axis=0)) + mm.nn(cm, sts[h]) * jnp.exp(acs)
                     + dsk[:, ls] * xs)
                sts[h] = jnp.exp(acs_last) * sts[h] + mm.tn(bm, xdt * jnp.exp(acs_last - acs))
                yh.append(y)
        rows.append(jnp.concatenate(yh, axis=1))
    return tuple(sts), (jnp.concatenate(rows, axis=0),)


def _gla_pre(glr, w2, bg):
    z = mm.nn(glr, w2) + bg
    return (-_softplus(-z) * (1.0 / GLA_GATE_NORM),)


def _gla_post(o, og, g):
    w = 2 * LANE
    return (jnp.concatenate([_rms(o[:, h * w:(h + 1) * w], g, GLA_DV) * _silu(og[:, h * w:(h + 1) * w])
                             for h in range(GLA_HEADS)], axis=1),)


def _hgrn_pre(q, f, lbnd):
    e = jnp.exp(lbnd - jnp.max(lbnd, axis=0, keepdims=True))
    rowi = lax.broadcasted_iota(jnp.int32, e.shape, 0)
    lb = jnp.sum(jnp.where(rowi >= 1, e, 0.0), axis=0, keepdims=True) / jnp.sum(e, axis=0, keepdims=True)
    fg = lb + (1.0 - lb) * _sigmoid(f)
    return _silu(q), 1.0 - fg, jnp.log(fg)


def _hgrn_post(o, og, g):
    return (jnp.concatenate([_rms(o[:, h * LANE:(h + 1) * LANE], g) for h in range(HGRN_HEADS)], axis=1)
            * _sigmoid(og),)


def _mamba_post(y, z, g):
    v = y * _silu(z)
    w = (SSM_HEADS // SSM_GROUPS) * LANE
    n_real = (SSM_HEADS // SSM_GROUPS) * SSM_HD
    return (jnp.concatenate([_rms(v[:, i * w:(i + 1) * w], g[:, i * w:(i + 1) * w], n_real)
                             for i in range(SSM_GROUPS)], axis=1),)


def _dil_pre(q, k, cosf, sinf, qg, kg):
    def one(x, g):
        hs = []
        for h in range(DIL_HEADS):
            n = _rms(x[:, h * LANE:(h + 1) * LANE], g)
            hs.append(n * cosf + _swap_halves(n) * sinf)
        return jnp.concatenate(hs, axis=1)

    return one(q, qg), one(k, kg)


def _dil_merge(o0, o1, o2, l0, l1, l2):
    m = jnp.maximum(jnp.maximum(l0, l1), l2)
    e0, e1, e2 = jnp.exp(l0 - m), jnp.exp(l1 - m), jnp.exp(l2 - m)
    return ((e0 * o0 + e1 * o1 + e2 * o2) / (e0 + e1 + e2),)


def _dil_block(q, kp, kc, vp, vc, lim):
    kk = jnp.concatenate([kp, kc], axis=0)
    vv = jnp.concatenate([vp, vc], axis=0)
    s = mm.nt(q, kk) * (DIL_HD ** -0.5)
    i = lax.broadcasted_iota(jnp.int32, s.shape, 0)
    j = lax.broadcasted_iota(jnp.int32, s.shape, 1)
    dist = DIL_BLOCK + i - j
    s = jnp.where((dist >= 0) & (dist <= DIL_BLOCK) & (j >= lim), s, -1e30)
    m = jnp.max(s, axis=-1, keepdims=True)
    p = jnp.exp(s - m)
    l = jnp.sum(p, axis=-1, keepdims=True)
    return mm.nn(p / l, vv), jnp.broadcast_to(m + jnp.log(l), (q.shape[0], LANE))


def _xattn(xq, kv, qg, kg):
    w = XA_HEADS * LANE
    os_ = []
    for h in range(XA_HEADS):
        ls = slice(h * LANE, (h + 1) * LANE)
        q = _rms(xq[:, ls], qg, XA_HD)
        k = _rms(kv[:, ls], kg, XA_HD)
        s = mm.nt(q, k) * (XA_HD ** -0.5)
        p = jnp.exp(s - jnp.max(s, axis=-1, keepdims=True))
        p = p / jnp.sum(p, axis=-1, keepdims=True)
        os_.append(mm.nn(p, kv[:, w + h * LANE:w + (h + 1) * LANE]))
    return (jnp.concatenate(os_, axis=1),)


def dil_attn(name, q, k, v, r):
    s, w = q.shape
    l = s // r
    nb = l // DIL_BLOCK
    q2, k2, v2 = (t.reshape(l, r * w) for t in (q, k, v))

    def body(q_r, kp_r, kc_r, vp_r, vc_r, o_r, l_r):
        lim = jnp.where(pl.program_id(1) == 0, DIL_BLOCK, 0)
        for h in range(DIL_HEADS):
            ls = slice(h * LANE, (h + 1) * LANE)
            o, lse = _dil_block(q_r[:, ls], kp_r[:, ls], kc_r[:, ls], vp_r[:, ls], vc_r[:, ls], lim)
            o_r[:, ls] = o
            l_r[:, ls] = lse

    cur = pl.BlockSpec((DIL_BLOCK, w), lambda res, n: (n, res))
    prev = pl.BlockSpec((DIL_BLOCK, w), lambda res, n: (jnp.maximum(n - 1, 0), res))
    o, lse = pl.pallas_call(
        body, name=name, grid=(r, nb), in_specs=[cur, prev, cur, prev, cur], out_specs=[cur, cur],
        out_shape=[jax.ShapeDtypeStruct((l, r * w), F32)] * 2,
        compiler_params=_cparams(("parallel", "parallel")),
    )(q2, k2, k2, v2, v2)
    return o.reshape(s, w), lse.reshape(s, w)


def dil_attn_bwd(name, q, k, v, do, dlse, r):
    s, w = q.shape
    l = s // r
    nb = l // DIL_BLOCK
    q2, k2, v2, do2, dl2 = (t.reshape(l, r * w) for t in (q, k, v, do, dlse))

    def body(q_r, kp_r, kc_r, vp_r, vc_r, do_r, dl_r, dq_r, dk_r, dv_r, ck, cv):
        i = pl.program_id(1)
        lim = jnp.where(i == nb - 1, DIL_BLOCK, 0)

        @pl.when(i == 0)
        def _():
            ck[...] = jnp.zeros_like(ck)
            cv[...] = jnp.zeros_like(cv)

        for h in range(DIL_HEADS):
            ls = slice(h * LANE, (h + 1) * LANE)
            _, vjp = jax.vjp(functools.partial(_dil_block, lim=lim),
                             q_r[:, ls], kp_r[:, ls], kc_r[:, ls], vp_r[:, ls], vc_r[:, ls])
            gq, gkp, gkc, gvp, gvc = vjp((do_r[:, ls], dl_r[:, ls]))
            dq_r[:, ls] = gq.astype(dq_r.dtype)
            dk_r[:, ls] = (gkc + ck[:, ls]).astype(dk_r.dtype)
            dv_r[:, ls] = (gvc + cv[:, ls]).astype(dv_r.dtype)
            ck[:, ls] = gkp
            cv[:, ls] = gvp

    cur = pl.BlockSpec((DIL_BLOCK, w), lambda res, i: (nb - 1 - i, res))
    prev = pl.BlockSpec((DIL_BLOCK, w), lambda res, i: (jnp.maximum(nb - 2 - i, 0), res))
    dq, dk, dv = pl.pallas_call(
        body, name=name, grid=(r, nb), in_specs=[cur, prev, cur, prev, cur, cur, cur], out_specs=[cur, cur, cur],
        out_shape=[jax.ShapeDtypeStruct((l, r * w), F32), jax.ShapeDtypeStruct((l, r * w), F32),
                   jax.ShapeDtypeStruct((l, r * w), MXU_DTYPE)],
        scratch_shapes=[pltpu.VMEM((DIL_BLOCK, w), F32)] * 2,
        compiler_params=_cparams(("parallel", "arbitrary")),
    )(q2, k2, k2, v2, v2, do2, dl2)
    return dq.reshape(s, w), dk.reshape(s, w), dv.reshape(s, w)


def _dsilu(u):
    sg = _sigmoid(u)
    return sg * (1.0 + u * (1.0 - sg))


def _taps(xp, w_ref, ntap, lo, hi_):
    acc = None
    for j in range(ntap):
        sh = ntap - 1 - j
        term = w_ref[j:j + 1, :] * (pltpu.roll(xp, sh, 0) if sh else xp)[lo:hi_]
        acc = term if acc is None else acc + term
    return acc


def conv_fwd(name, x, w, b, mode, out_dtype, tc, block=ROW_BLOCK):
    s, c = x.shape
    ntap = w.shape[0]
    block = min(block, s)
    f = c // 2 if mode == 'glu' else c
    nh = 2 if mode == 'glu' else 1
    off = f // tc

    def body(*refs):
        first = pl.program_id(1) == 0
        us = []
        for hlf in range(nh):
            prev_r, cur_r, w_r, b_r = refs[4 * hlf:4 * hlf + 4]
            xp = jnp.concatenate([jnp.where(first, 0.0, prev_r[...]), cur_r[...]], axis=0)
            us.append(b_r[...] + _taps(xp, w_r, ntap, SUBLANE, SUBLANE + block))
        o_ref = refs[-1]
        o_ref[...] = (_silu(us[0]) * us[1] if mode == 'glu' else _silu(us[0])).astype(o_ref.dtype)

    rb = block // SUBLANE
    ins, specs = [], []
    for hlf in range(nh):
        o = hlf * off
        ins += [x, x, w, b]
        specs += [pl.BlockSpec((SUBLANE, tc), lambda j, i, o=o: (jnp.maximum(i * rb - 1, 0), j + o)),
                  pl.BlockSpec((block, tc), lambda j, i, o=o: (i, j + o)),
                  pl.BlockSpec((ntap, tc), lambda j, i, o=o: (0, j + o)),
                  pl.BlockSpec((1, tc), lambda j, i, o=o: (0, j + o))]
    return pl.pallas_call(
        body, name=name, grid=(f // tc, s // block), in_specs=specs,
        out_specs=pl.BlockSpec((block, tc), lambda j, i: (i, j)),
        out_shape=jax.ShapeDtypeStruct((s, f), out_dtype),
        compiler_params=_cparams(("parallel", "parallel")),
    )(*ins)


def conv_bwd(name, x, w, b, dout, mode, tc, block=ROW_BLOCK):
    s, c = x.shape
    ntap = w.shape[0]
    block = min(block, s)
    nblk = s // block
    f = c // 2 if mode == 'glu' else c
    nh = 2 if mode == 'glu' else 1
    off = f // tc
    ext = block + SUBLANE

    def body(*refs):
        i = pl.program_id(1)
        first, last = i == 0, i == nblk - 1
        dcur_r, dnext_r = refs[5 * nh], refs[5 * nh + 1]
        outs = refs[5 * nh + 2:]
        d_e = jnp.concatenate([dcur_r[...], jnp.where(last, 0.0, dnext_r[...])], axis=0)
        xps, us = [], []
        for hlf in range(nh):
            prev_r, cur_r, next_r, w_r, b_r = refs[5 * hlf:5 * hlf + 5]
            xp = jnp.concatenate([jnp.where(first, 0.0, prev_r[...]), cur_r[...], jnp.where(last, 0.0, next_r[...])],
                                 axis=0)
            xps.append(xp)
            us.append(b_r[...] + _taps(xp, w_r, ntap, SUBLANE, SUBLANE + ext))
        if mode == 'glu':
            dus = [d_e * us[1] * _dsilu(us[0]), d_e * _silu(us[0])]
        else:
            dus = [d_e * _dsilu(us[0])]
        for hlf in range(nh):
            w_r = refs[5 * hlf + 3]
            dx_r, dw_r, db_r = outs[3 * hlf:3 * hlf + 3]
            du, xp = dus[hlf], xps[hlf]

            @pl.when(first)
            def _(dw_r=dw_r, db_r=db_r):
                dw_r[...] = jnp.zeros_like(dw_r)
                db_r[...] = jnp.zeros_like(db_r)

            db_r[...] += jnp.sum(du[:block], axis=0, keepdims=True)
            dx = None
            for j in range(ntap):
                sh = ntap - 1 - j
                xs = (pltpu.roll(xp, sh, 0) if sh else xp)[SUBLANE:SUBLANE + block]
                dw_r[j:j + 1, :] += jnp.sum(du[:block] * xs, axis=0, keepdims=True)
                term = w_r[j:j + 1, :] * (pltpu.roll(du, ext - sh, 0) if sh else du)[:block]
                dx = term if dx is None else dx + term
            dx_r[...] = dx.astype(dx_r.dtype)

    rb = block // SUBLANE
    nrow8 = s // SUBLANE
    ins, specs = [], []
    for hlf in range(nh):
        o = hlf * off
        ins += [x, x, x, w, b]
        specs += [pl.BlockSpec((SUBLANE, tc), lambda j, i, o=o: (jnp.maximum(i * rb - 1, 0), j + o)),
                  pl.BlockSpec((block, tc), lambda j, i, o=o: (i, j + o)),
                  pl.BlockSpec((SUBLANE, tc), lambda j, i, o=o: (jnp.minimum((i + 1) * rb, nrow8 - 1), j + o)),
                  pl.BlockSpec((ntap, tc), lambda j, i, o=o: (0, j + o)),
                  pl.BlockSpec((1, tc), lambda j, i, o=o: (0, j + o))]
    ins += [dout, dout]
    specs += [pl.BlockSpec((block, tc), lambda j, i: (i, j)),
              pl.BlockSpec((SUBLANE, tc), lambda j, i: (jnp.minimum((i + 1) * rb, nrow8 - 1), j))]
    out_specs, out_shape = [], []
    for hlf in range(nh):
        out_specs += [pl.BlockSpec((block, tc), lambda j, i: (i, j)), pl.BlockSpec((ntap, tc), lambda j, i: (0, j)),
                      pl.BlockSpec((1, tc), lambda j, i: (0, j))]
        out_shape += [jax.ShapeDtypeStruct((s, f), MXU_DTYPE), jax.ShapeDtypeStruct((ntap, f), F32),
                      jax.ShapeDtypeStruct((1, f), F32)]
    res = pl.pallas_call(
        body, name=name, grid=(f // tc, nblk), in_specs=specs, out_specs=out_specs, out_shape=out_shape,
        compiler_params=_cparams(("parallel", "arbitrary")),
    )(*ins)
    if nh == 1:
        return res[0], res[1], res[2]
    return (jnp.concatenate([res[0], res[3]], axis=1), jnp.concatenate([res[1], res[4]], axis=1),
            jnp.concatenate([res[2], res[5]], axis=1))


def loss_head(y, target, block=ROW_BLOCK):
    s, d = y.shape
    block = min(block, s)

    def body(y_r, t_r, acc_r, dy_r):
        e = y_r[...] - t_r[...]
        dy_r[...] = e * (1.0 / d)

        @pl.when(pl.program_id(0) == 0)
        def _():
            acc_r[...] = jnp.zeros_like(acc_r)

        acc_r[...] += jnp.sum((e * e).reshape(block // SUBLANE, SUBLANE, d), axis=0) * (0.5 / d)

    return pl.pallas_call(
        body, name="loss_head", grid=(s // block,),
        in_specs=[pl.BlockSpec((block, d), lambda i: (i, 0))] * 2,
        out_specs=[pl.BlockSpec((SUBLANE, d), lambda i: (0, 0)), pl.BlockSpec((block, d), lambda i: (i, 0))],
        out_shape=[jax.ShapeDtypeStruct((SUBLANE, d), F32), jax.ShapeDtypeStruct((s, d), F32)],
        compiler_params=_cparams(("arbitrary",)),
    )(y, target)


def adamw(name, w, g, m, v):
    r, c = w.shape
    tr = r if r <= 512 else _tile(r, (512, 256, 128, 64, 32, 16, 8))
    if c * tr * 4 > (1 << 21):
        tr = _tile(r, (256, 128, 64, 32, 16, 8))

    def body(w_r, g_r, m_r, v_r, d_r, nm_r, nv_r):
        gg = g_r[...]
        nm = ADAM_B1 * m_r[...] + (1.0 - ADAM_B1) * gg
        nv = ADAM_B2 * v_r[...] + (1.0 - ADAM_B2) * (gg * gg)
        m_hat = nm / (1.0 - ADAM_B1 ** ADAM_STEP)
        v_hat = nv / (1.0 - ADAM_B2 ** ADAM_STEP)
        d_r[...] = -ADAM_LR * (m_hat / (jnp.sqrt(v_hat) + ADAM_EPS) + ADAM_WD * w_r[...])
        nm_r[...] = nm
        nv_r[...] = nv

    spec = pl.BlockSpec((tr, c), lambda i: (i, 0))
    return pl.pallas_call(
        body, name=name, grid=(r // tr,), in_specs=[spec] * 4, out_specs=[spec] * 3,
        out_shape=[jax.ShapeDtypeStruct((r, c), F32)] * 3, compiler_params=_cparams(("parallel",)),
    )(w, g, m, v)


MESH = pl.DeviceIdType.MESH
_ANY = pl.BlockSpec(memory_space=pl.ANY)


def _place():
    return lax.axis_index("x"), lax.axis_index("y"), lax.axis_index("c")


def allgather_chips(shard):
    r, c = shard.shape
    hr = r // 2

    def body(w_ref, out_ref, send_sems, recv_sems):
        x, y, cc = _place()
        sibling = (x, y, 1 - cc)
        chips = [(1 - x, y), (x, 1 - y), (1 - x, 1 - y)]

        def half(chip, core):
            return out_ref.at[2 * chip[0] + chip[1], pl.ds(core * hr, hr), :]

        def copy(k, chip, core, to, src=None):
            return pltpu.make_async_remote_copy(
                src_ref=half(chip, core) if src is None else src, dst_ref=half(chip, core),
                send_sem=send_sems.at[k], recv_sem=recv_sems.at[k], device_id=to, device_id_type=MESH)

        my_half = w_ref.at[pl.ds(cc * hr, hr), :]
        first = [copy(j, (x, y), cc, (*chip, cc), src=my_half) for j, chip in enumerate(chips)]
        for cp in first:
            cp.start()
        passed = [copy(3 + j, chip, cc, sibling) for j, chip in enumerate(chips)]
        for j, chip in enumerate(chips):
            copy(j, chip, cc, (x, y, cc)).wait_recv()
            passed[j].start()
        for j, chip in enumerate(chips):
            copy(3 + j, chip, 1 - cc, (x, y, cc)).wait_recv()
        for cp in first + passed:
            cp.wait_send()

    out = pl.pallas_call(
        body, name="allgather_chips", in_specs=[_ANY], out_specs=_ANY,
        out_shape=jax.ShapeDtypeStruct((N_CHIPS, r, c), shard.dtype),
        scratch_shapes=[pltpu.SemaphoreType.DMA((6,)), pltpu.SemaphoreType.DMA((6,))],
    )(shard)
    chip = 2 * lax.axis_index("x") + lax.axis_index("y")
    return lax.dynamic_update_slice(out, shard[None], (chip, 0, 0))


def allgather_devices(buf):
    r, c = buf.shape

    def body(b_ref, out_ref, send_sems, recv_sems, local_sem):
        x, y, cc = _place()
        me = 4 * x + 2 * y + cc
        mine = pltpu.make_async_copy(b_ref, out_ref.at[me], local_sem)
        mine.start()
        copies = []
        for k in range(1, N_DEV):
            px, py, pc = x ^ (k >> 2), y ^ ((k >> 1) & 1), cc ^ (k & 1)
            cp = pltpu.make_async_remote_copy(src_ref=b_ref, dst_ref=out_ref.at[me], send_sem=send_sems.at[k - 1],
                                              recv_sem=recv_sems.at[k - 1], device_id=(px, py, pc), device_id_type=MESH)
            cp.start()
            copies.append((cp, 4 * px + 2 * py + pc))
        for k, (cp, peer) in enumerate(copies):
            pltpu.make_async_remote_copy(src_ref=b_ref, dst_ref=out_ref.at[peer], send_sem=send_sems.at[k],
                                         recv_sem=recv_sems.at[k], device_id=(x, y, cc), device_id_type=MESH).wait_recv()
        for cp, _ in copies:
            cp.wait_send()
        mine.wait()

    return pl.pallas_call(
        body, name="allgather_devices", in_specs=[_ANY], out_specs=_ANY,
        out_shape=jax.ShapeDtypeStruct((N_DEV, r, c), buf.dtype),
        scratch_shapes=[pltpu.SemaphoreType.DMA((N_DEV - 1,)), pltpu.SemaphoreType.DMA((N_DEV - 1,)),
                        pltpu.SemaphoreType.DMA],
    )(buf)


def swap_halves_sibling(g):
    n, _, r, c = g.shape

    def body(g_ref, out_ref, send_sem, recv_sem):
        x, y, cc = _place()
        cp = pltpu.make_async_remote_copy(src_ref=g_ref.at[:, 1 - cc], dst_ref=out_ref, send_sem=send_sem,
                                          recv_sem=recv_sem, device_id=(x, y, 1 - cc), device_id_type=MESH)
        cp.start()
        cp.wait()

    return pl.pallas_call(
        body, name="swap_halves_sibling", in_specs=[_ANY], out_specs=_ANY,
        out_shape=jax.ShapeDtypeStruct((n, r, c), g.dtype),
        scratch_shapes=[pltpu.SemaphoreType.DMA, pltpu.SemaphoreType.DMA],
    )(g)


def exchange_chips(p):
    n, r, c = p.shape

    def body(p_ref, out_ref, send_sems, recv_sems):
        x, y, cc = _place()
        me = 2 * x + y
        chips = [(1 - x, y), (x, 1 - y), (1 - x, 1 - y)]
        cps = []
        for j, chip in enumerate(chips):
            cp = pltpu.make_async_remote_copy(src_ref=p_ref.at[2 * chip[0] + chip[1]], dst_ref=out_ref.at[me],
                                              send_sem=send_sems.at[j], recv_sem=recv_sems.at[j],
                                              device_id=(*chip, cc), device_id_type=MESH)
            cp.start()
            cps.append(cp)
        for j, chip in enumerate(chips):
            pltpu.make_async_remote_copy(src_ref=p_ref.at[me], dst_ref=out_ref.at[2 * chip[0] + chip[1]],
                                         send_sem=send_sems.at[j], recv_sem=recv_sems.at[j],
                                         device_id=(x, y, cc), device_id_type=MESH).wait_recv()
        for cp in cps:
            cp.wait_send()

    return pl.pallas_call(
        body, name="exchange_chips", in_specs=[_ANY], out_specs=_ANY,
        out_shape=jax.ShapeDtypeStruct((n, r, c), p.dtype),
        scratch_shapes=[pltpu.SemaphoreType.DMA((3,)), pltpu.SemaphoreType.DMA((3,))],
    )(p)


def sum_chips(got, own, chip):
    n, r, c = got.shape
    tr = _tile(r, (512, 256, 128, 64, 32, 16))

    def body(chip_ref, got_r, own_r, out_r):
        mine = own_r[...].astype(F32)
        acc = None
        for k in range(n):
            term = jnp.where(chip_ref[0] == k, mine, got_r[k].astype(F32))
            acc = term if acc is None else acc + term
        out_r[...] = acc

    return pl.pallas_call(
        body, name="sum_chips",
        grid_spec=pltpu.PrefetchScalarGridSpec(
            num_scalar_prefetch=1, grid=(r // tr,),
            in_specs=[pl.BlockSpec((n, tr, c), lambda i, chip_ref: (0, i, 0)),
                      pl.BlockSpec((None, tr, c), lambda i, chip_ref: (chip_ref[0], i, 0))],
            out_specs=pl.BlockSpec((tr, c), lambda i, chip_ref: (i, 0))),
        out_shape=jax.ShapeDtypeStruct((r, c), F32),
        compiler_params=_cparams(("parallel",)),
    )(chip, got, own)


def join_halves_sibling(h):
    r, c = h.shape

    def body(h_ref, out_ref, send_sem, recv_sem):
        x, y, cc = _place()
        cp = pltpu.make_async_remote_copy(src_ref=h_ref, dst_ref=out_ref.at[cc], send_sem=send_sem, recv_sem=recv_sem,
                                          device_id=(x, y, 1 - cc), device_id_type=MESH)
        cp.start()
        pltpu.make_async_remote_copy(src_ref=h_ref, dst_ref=out_ref.at[1 - cc], send_sem=send_sem, recv_sem=recv_sem,
                                     device_id=(x, y, cc), device_id_type=MESH).wait_recv()
        cp.wait_send()

    out = pl.pallas_call(
        body, name="join_halves_sibling", in_specs=[_ANY], out_specs=_ANY,
        out_shape=jax.ShapeDtypeStruct((2, r, c), h.dtype),
        scratch_shapes=[pltpu.SemaphoreType.DMA, pltpu.SemaphoreType.DMA],
    )(h)
    return lax.dynamic_update_slice(out, h[None], (lax.axis_index("c"), 0, 0))


def add_own_half(g, got, core, out_dtype):
    n, _, r, c = g.shape
    tr = _tile(r, (512, 256, 128, 64, 32, 16))

    def body(c_ref, g_r, o_r, out_r):
        out_r[...] = (g_r[...] + o_r[...]).astype(out_r.dtype)

    return pl.pallas_call(
        body, name="add_own_half",
        grid_spec=pltpu.PrefetchScalarGridSpec(
            num_scalar_prefetch=1, grid=(n, r // tr),
            in_specs=[pl.BlockSpec((None, None, tr, c), lambda i, j, c_ref: (i, c_ref[0], j, 0)),
                      pl.BlockSpec((None, tr, c), lambda i, j, c_ref: (i, j, 0))],
            out_specs=pl.BlockSpec((None, tr, c), lambda i, j, c_ref: (i, j, 0))),
        out_shape=jax.ShapeDtypeStruct((n, r, c), out_dtype),
        compiler_params=_cparams(("parallel", "parallel")),
    )(core, g, got)


def sum_slabs(p, name):
    n, r, c = p.shape
    tr = _tile(r, [t for t in (512, 256, 128, 64, 32, 16) if n * t * c * p.dtype.itemsize <= (1 << 23)])

    def body(p_r, out_r):
        acc = p_r[0].astype(F32)
        for k in range(1, n):
            acc = acc + p_r[k].astype(F32)
        out_r[...] = acc

    return pl.pallas_call(
        body, name=name, grid=(r // tr,), in_specs=[pl.BlockSpec((n, tr, c), lambda i: (0, i, 0))],
        out_specs=pl.BlockSpec((tr, c), lambda i: (i, 0)), out_shape=jax.ShapeDtypeStruct((r, c), F32),
        compiler_params=_cparams(("parallel",)),
    )(p)


def _lay(arr, axis, pieces, total, reps=()):
    items = [(d, n, lax.slice_in_dim(arr, s0, s0 + n, axis=axis)) for s0, n, d in pieces]
    items += [(d, n, jnp.repeat(lax.slice_in_dim(arr, s0, s0 + 1, axis=axis), n, axis=axis)) for s0, d, n in reps]
    items.sort(key=lambda t: t[0])
    parts, pos = [], 0

    def zeros(n):
        sh = list(arr.shape)
        sh[axis] = n
        return jnp.zeros(sh, arr.dtype)

    for d, n, v in items:
        if d > pos:
            parts.append(zeros(d - pos))
        parts.append(v)
        pos = d + n
    if total > pos:
        parts.append(zeros(total - pos))
    return jnp.concatenate(parts, axis=axis) if len(parts) > 1 else parts[0]


def _unlay_parts(g, axis, pieces, reps=()):
    out = [(s0, lax.slice_in_dim(g, d, d + n, axis=axis)) for s0, n, d in pieces]
    out += [(s0, jnp.sum(lax.slice_in_dim(g, d, d + n, axis=axis), axis=axis, keepdims=True)) for s0, d, n in reps]
    return out


def _join(parts, axis):
    parts = sorted(parts, key=lambda t: t[0])
    return jnp.concatenate([p for _, p in parts], axis=axis)


def _heads(src0, n_heads, width, padded, dst0=0):
    return [(src0 + h * width, width, dst0 + h * padded) for h in range(n_heads)]


_XQ = lambda src0: _heads(src0, XA_HEADS, XA_HD, LANE)
_XA_W = XA_HEADS * LANE

LAYOUT = {
    'a': dict(
        segs=dict(q=(_heads(0, 4, 96, LANE), 512, ()), k=(_heads(384, 4, 96, LANE), 512, ()),
                  v=(_heads(768, 4, 192, 256), 1024, ()), glr=([(1536, 16, 0)], LANE, ()),
                  og=(_heads(1552, 4, 192, 256), 1024, ()), xq=(_XQ(2320), _XA_W, ())),
        tok=(_heads(0, 4, 192, 256), 1024), xa=(_XQ(768), _XA_W)),
    'b': dict(
        segs=dict(**{f"q{g}": ([(512 * g, 512, 0)], 512, ()) for g in range(3)},
                  **{f"k{g}": ([(1536 + 512 * g, 512, 0)], 512, ()) for g in range(3)},
                  **{f"v{g}": ([(3072 + 512 * g, 512, 0)], 512, ()) for g in range(3)},
                  xq=(_XQ(4608), _XA_W, ())),
        tok=([(0, 512, 0)], 512), xa=(_XQ(512), _XA_W)),
    'c': dict(
        segs=dict(z=(_heads(0, 12, 64, LANE), 1536, ()),
                  xbc=(_heads(768, 12, 64, LANE) + [(1536, 256, 1536), (1792, 256, 1792)], 2048, ()),
                  dt=([(2048, 12, 0)], LANE, ()),
                  xq=(_XQ(2060), _XA_W, ())),
        tok=(_heads(0, 12, 64, LANE), 1536), xa=(_XQ(768), _XA_W)),
    'd': dict(
        segs=dict(q=([(0, 768, 0)], 768, ()), f=([(768, 768, 0)], 768, ()), i=([(1536, 768, 0)], 768, ()),
                  og=([(2304, 768, 0)], 768, ()), xq=(_XQ(3072), _XA_W, ())),
        tok=([(0, 768, 0)], 768), xa=(_XQ(768), _XA_W)),
}
KINDS = 'abcd'
_XS_PIECES = _heads(0, 12, 64, LANE)
_XBC_PIECES = _XS_PIECES + [(768, 256, 1536), (1024, 256, 1792)]
_HEAD_REPS = tuple((h, h * LANE, LANE) for h in range(12))


def _row(v):
    return v.reshape(1, -1)


def local_step(x, mem, positions, target, W):
    s = x.shape[0]
    grads = {}
    scan_block = CHUNK * SCAN_CHUNKS

    inv_freq = ROPE_THETA ** (-jnp.arange(DIL_HD // 2, dtype=F32) / (DIL_HD // 2))
    ang = positions.astype(F32)[:, None] * inv_freq
    cosf = jnp.concatenate([jnp.cos(ang), jnp.cos(ang)], axis=-1)
    sinf = jnp.concatenate([-jnp.sin(ang), jnp.sin(ang)], axis=-1)

    mem_g = _row(W['mem_norm'])
    (mem_n,) = tmap("mem_norm", _norm_stage, [mem], [mem_g], [(D_MODEL, MXU_DTYPE)])
    kv_lay = _heads(0, 4, 64, LANE) + _heads(256, 4, 64, LANE, dst0=_XA_W)

    saved = []
    for i in range(4):
        kind = KINDS[i]
        lay = LAYOUT[kind]
        sv = dict(x0=x)
        w_in = W[f'{kind}_w_in']
        w_out = W[f'{kind}_w_out']
        sv['w_seg'] = {n: _lay(w_in, 1, p, t, r).astype(MXU_DTYPE) for n, (p, t, r) in lay['segs'].items()}
        sv['wo_tok'] = _lay(w_out, 0, *lay['tok']).astype(MXU_DTYPE)
        sv['wo_xa'] = _lay(w_out, 0, *lay['xa']).astype(MXU_DTYPE)
        sv['w_kv'] = _lay(W['xa_w_kv'][i], 1, kv_lay, 2 * _XA_W).astype(MXU_DTYPE)
        sv['g1'] = _row(W['mix_norm'][i])
        (h,) = tmap(f"mix_norm_{i}", _norm_stage, [x], [sv['g1']], [(D_MODEL, MXU_DTYPE)])
        sv['h'] = h
        seg = {n: matmul(h, w) for n, w in sv['w_seg'].items()}
        sv['seg'] = seg

        if kind == 'a':
            sv['w2'] = _lay(_lay(W['a_w_gate2'], 1, _heads(0, 4, 96, LANE), 512), 0, [(0, 16, 0)], LANE)
            sv['bg'] = _row(_lay(W['a_b_gate'], 0, _heads(0, 4, 96, LANE), 512))
            sv['on'] = _row(_lay(W['a_o_norm'], 0, [(0, 192, 0)], 256))
            (la,) = tmap("gla_pre", _gla_pre, [seg['glr']], [sv['w2'], sv['bg']], [(512, F32)])
            sv['la'] = la
            sv['scan_fn'] = _gla_step(GLA_HEADS, 2 * LANE, GLA_DK ** -0.5)
            sv['scan_rows'] = [seg['q'], seg['k'], seg['v'], la]
            (o,), sv['states'] = rscan("gla_scan", sv['scan_fn'], [(LANE, 2 * LANE)] * GLA_HEADS, sv['scan_rows'], [],
                                       [(1024, F32)], scan_block)
            sv['o'] = o
            (tok,) = tmap("gla_post", _gla_post, [o, seg['og']], [sv['on']], [(1024, MXU_DTYPE)])
        elif kind == 'b':
            sv['qg'], sv['kg'] = _row(W['b_q_norm']), _row(W['b_k_norm'])
            sv['qn'], sv['kn'], os_, ls_ = [], [], [], []
            for g, (window, r) in enumerate(DIL_GROUPS):
                assert window // r == DIL_BLOCK and (s // r) % DIL_BLOCK == 0
                qn, kn = tmap(f"dil_pre_{g}", _dil_pre, [seg[f'q{g}'], seg[f'k{g}'], cosf, sinf], [sv['qg'], sv['kg']],
                              [(512, F32), (512, F32)])
                o, lse = dil_attn(f"dil_attn_{g}", qn, kn, seg[f'v{g}'], r)
                sv['qn'].append(qn)
                sv['kn'].append(kn)
                os_.append(o)
                ls_.append(lse)
            sv['os'], sv['ls'] = os_, ls_
            (tok,) = tmap("dil_merge", _dil_merge, os_ + ls_, [], [(512, MXU_DTYPE)])
        elif kind == 'c':
            sv['cw'] = _lay(W['c_conv_w'], 1, _XBC_PIECES, 2048)
            sv['cb'] = _row(_lay(W['c_conv_b'], 0, _XBC_PIECES, 2048))
            sv['dtb'] = _row(_lay(W['c_dt_bias'], 0, [], 1536, _HEAD_REPS))
            sv['alog'] = _row(_lay(W['c_a_log'], 0, [], 1536, _HEAD_REPS))
            sv['dsk'] = _row(_lay(W['c_d'], 0, [], 1536, _HEAD_REPS))
            sv['cn'] = _row(_lay(W['c_norm'], 0, _XS_PIECES, 1536))
            xact = conv_fwd("ssm_conv", seg['xbc'], sv['cw'], sv['cb'], 'silu', F32, 512)
            sv['xact'] = xact
            sv['scan_rows'] = [xact, seg['dt']]
            sv['scan_params'] = [sv['dtb'], sv['alog'], sv['dsk']]
            (yv,), sv['states'] = rscan("ssd_scan", _ssd_step, [(LANE, LANE)] * SSM_HEADS, sv['scan_rows'],
                                        sv['scan_params'], [(1536, F32)], scan_block)
            sv['y'] = yv
            (tok,) = tmap("ssd_post", _mamba_post, [yv, seg['z']], [sv['cn']], [(1536, MXU_DTYPE)])
        else:
            sv['lbnd'] = W['d_lower_bounds']
            sv['on'] = _row(W['d_o_norm'])
            qq, kk, la = tmap("hgrn_pre", _hgrn_pre, [seg['q'], seg['f']], [sv['lbnd']], [(768, F32)] * 3)
            sv['scan_fn'] = _gla_step(HGRN_HEADS, LANE, 1.0)
            sv['scan_rows'] = [qq, kk, seg['i'], la]
            (o,), sv['states'] = rscan("hgrn_scan", sv['scan_fn'], [(LANE, LANE)] * HGRN_HEADS, sv['scan_rows'], [],
                                       [(768, F32)], scan_block)
            sv['o'] = o
            (tok,) = tmap("hgrn_post", _hgrn_post, [o, seg['og']], [sv['on']], [(768, MXU_DTYPE)])
        sv['tok'] = tok

        kv = matmul(mem_n, sv['w_kv'])
        sv['kv'] = kv
        sv['xqg'] = _row(_lay(W['xa_q_norm'][i], 0, [(0, 64, 0)], LANE))
        sv['xkg'] = _row(_lay(W['xa_k_norm'][i], 0, [(0, 64, 0)], LANE))
        (xa,) = tmap(f"xattn_{i}", _xattn, [seg['xq']], [kv, sv['xqg'], sv['xkg']], [(_XA_W, MXU_DTYPE)])
        sv['xa'] = xa
        x = matmul(tok, sv['wo_tok'], add=x)
        x = matmul(xa, sv['wo_xa'], add=x)
        sv['x1'] = x

        sv['g2'] = _row(W['ffn_norm'][i])
        sv['w_up'] = W['ffn_w_up'][i].astype(MXU_DTYPE)
        sv['w_down'] = W['ffn_w_down'][i].astype(MXU_DTYPE)
        sv['fcw'] = W['ffn_conv_w'][i]
        sv['fcb'] = _row(W['ffn_conv_b'][i])
        (h2,) = tmap(f"ffn_norm_{i}", _norm_stage, [x], [sv['g2']], [(D_MODEL, MXU_DTYPE)])
        sv['h2'] = h2
        u0 = matmul(h2, sv['w_up'])
        sv['u0'] = u0
        act = conv_fwd("ffn_conv", u0, sv['fcw'], sv['fcb'], 'glu', MXU_DTYPE, 1408)
        sv['act'] = act
        x = matmul(act, sv['w_down'], add=x)
        saved.append(sv)

    loss_acc, dx = loss_head(x, target)

    g_stack = {n: [None] * 4 for n in ('mix_norm', 'xa_w_kv', 'xa_q_norm', 'xa_k_norm', 'ffn_norm', 'ffn_w_up',
                                        'ffn_conv_w', 'ffn_conv_b', 'ffn_w_down')}
    d_memn = None
    for i in reversed(range(4)):
        kind = KINDS[i]
        lay = LAYOUT[kind]
        sv = saved[i]
        seg = sv['seg']
        dact = matmul(dx, sv['w_down'], tb=True)
        g_stack['ffn_w_down'][i] = matmul(sv['act'], dx, ta=True)
        du0, dcw, dcb = conv_bwd("ffn_conv_bwd", sv['u0'], sv['fcw'], sv['fcb'], dact, 'glu', 1408)
        g_stack['ffn_conv_w'][i], g_stack['ffn_conv_b'][i] = dcw, dcb[0]
        dh2 = matmul(du0, sv['w_up'], tb=True)
        g_stack['ffn_w_up'][i] = matmul(sv['h2'], du0, ta=True)
        (dx,), (dg2,) = tmap_bwd(f"ffn_norm_bwd_{i}", _norm_stage, [sv['x1']], [sv['g2']], [dh2], [True], {0: dx})
        g_stack['ffn_norm'][i] = dg2[0]
        dtok = matmul(dx, sv['wo_tok'], tb=True)
        dxa = matmul(dx, sv['wo_xa'], tb=True)
        g_wo = _unlay_parts(matmul(sv['tok'], dx, ta=True), 0, lay['tok'][0]) \
            + _unlay_parts(matmul(sv['xa'], dx, ta=True), 0, lay['xa'][0])
        grads[f'{kind}_w_out'] = _join(g_wo, 0)
        (dxq,), (dkv, dqg, dkg) = tmap_bwd(f"xattn_bwd_{i}", _xattn, [seg['xq']], [sv['kv'], sv['xqg'], sv['xkg']],
                                           [dxa], [True])
        g_stack['xa_q_norm'][i], g_stack['xa_k_norm'][i] = dqg[0, :XA_HD], dkg[0, :XA_HD]
        g_stack['xa_w_kv'][i] = _join(_unlay_parts(matmul(mem_n, dkv, ta=True), 1, kv_lay), 1)
        d_memn = matmul(dkv, sv['w_kv'], tb=True, add=d_memn)
        dseg = dict(xq=dxq)
        if kind == 'a':
            (do, dog), (don,) = tmap_bwd("gla_post_bwd", _gla_post, [sv['o'], seg['og']], [sv['on']], [dtok],
                                         [True, True], grad_dtype=F32)
            grads['a_o_norm'] = don[0, :GLA_DV]
            (dq, dk, dv, dla), _ = rscan_bwd("gla_scan_bwd", sv['scan_fn'], sv['states'], sv['scan_rows'], [], [do],
                                             scan_block, grad_dtype=F32)
            (dglr,), (dw2, dbg) = tmap_bwd("gla_pre_bwd", _gla_pre, [seg['glr']], [sv['w2'], sv['bg']], [dla], [True])
            grads['a_w_gate2'] = _join(_unlay_parts(dw2[:GLA_RANK], 1, _heads(0, 4, 96, LANE)), 1)
            grads['a_b_gate'] = _join(_unlay_parts(dbg[0], 0, _heads(0, 4, 96, LANE)), 0)
            dseg.update(q=dq, k=dk, v=dv, glr=dglr, og=dog)
        elif kind == 'b':
            res, _ = tmap_bwd("dil_merge_bwd", _dil_merge, sv['os'] + sv['ls'], [], [dtok], [True] * 6, grad_dtype=F32)
            dqg = dkg = 0.0
            for g, (_, r) in enumerate(DIL_GROUPS):
                dqn, dkn, dv = dil_attn_bwd(f"dil_attn_bwd_{g}", sv['qn'][g], sv['kn'][g], seg[f'v{g}'], res[g],
                                            res[3 + g], r)
                (dq, dk), (a_, b_) = tmap_bwd(f"dil_pre_bwd_{g}", _dil_pre, [seg[f'q{g}'], seg[f'k{g}'], cosf, sinf],
                                              [sv['qg'], sv['kg']], [dqn, dkn], [True, True, False, False])
                dqg, dkg = dqg + a_, dkg + b_
                dseg.update({f'q{g}': dq, f'k{g}': dk, f'v{g}': dv})
            grads['b_q_norm'], grads['b_k_norm'] = dqg[0], dkg[0]
        elif kind == 'c':
            (dy, dz), (dcn,) = tmap_bwd("ssd_post_bwd", _mamba_post, [sv['y'], seg['z']], [sv['cn']], [dtok],
                                        [True, True], grad_dtype=F32)
            grads['c_norm'] = _join(_unlay_parts(dcn[0], 0, _XS_PIECES), 0)
            (dxact, ddt), (ddtb, dalog, ddsk) = rscan_bwd("ssd_scan_bwd", _ssd_step, sv['states'], sv['scan_rows'],
                                                          sv['scan_params'], [dy], scan_block, grad_dtype=F32)
            for nm, gv in (('c_dt_bias', ddtb), ('c_a_log', dalog), ('c_d', ddsk)):
                grads[nm] = _join(_unlay_parts(gv[0], 0, [], _HEAD_REPS), 0)
            dxbc, dcw, dcb = conv_bwd("ssm_conv_bwd", seg['xbc'], sv['cw'], sv['cb'], dxact, 'silu', 512)
            grads['c_conv_w'] = _join(_unlay_parts(dcw, 1, _XBC_PIECES), 1)
            grads['c_conv_b'] = _join(_unlay_parts(dcb[0], 0, _XBC_PIECES), 0)
            dseg.update(z=dz, xbc=dxbc, dt=ddt)
        else:
            (do, dog), (don,) = tmap_bwd("hgrn_post_bwd", _hgrn_post, [sv['o'], seg['og']], [sv['on']], [dtok],
                                         [True, True], grad_dtype=F32)
            grads['d_o_norm'] = don[0]
            (dqq, dkk, di, dla), _ = rscan_bwd("hgrn_scan_bwd", sv['scan_fn'], sv['states'], sv['scan_rows'], [], [do],
                                               scan_block, grad_dtype=F32)
            (dq, df), (dlb,) = tmap_bwd("hgrn_pre_bwd", _hgrn_pre, [seg['q'], seg['f']], [sv['lbnd']], [dqq, dkk, dla],
                                        [True, True])
            grads['d_lower_bounds'] = dlb
            dseg.update(q=dq, f=df, i=di, og=dog)
        dh = None
        g_in = []
        for n, (p, t, rp) in lay['segs'].items():
            dh = matmul(dseg[n], sv['w_seg'][n], tb=True, add=dh)
            g_in += _unlay_parts(matmul(sv['h'], dseg[n], ta=True), 1, p, rp)
        grads[f'{kind}_w_in'] = _join(g_in, 1)
        (dx,), (dg1,) = tmap_bwd(f"mix_norm_bwd_{i}", _norm_stage, [sv['x0']], [sv['g1']], [dh], [True], {0: dx})
        g_stack['mix_norm'][i] = dg1[0]

    _, (dmg,) = tmap_bwd("mem_norm_bwd", _norm_stage, [mem], [mem_g], [d_memn], [False])
    grads['mem_norm'] = dmg[0]
    for n, parts in g_stack.items():
        grads[n] = jnp.stack(parts)
    return loss_acc, dx, grads


def _pack(arrs, dtype, row_multiple=PACK_ROWS):
    parts, rows = [], 0
    for a in arrs:
        f = a.reshape(-1).astype(dtype)
        unit = PACK_ROWS * PACK_COLS
        pad = (-f.shape[0]) % unit
        if pad:
            f = jnp.concatenate([f, jnp.zeros((pad,), dtype)])
        parts.append(f.reshape(-1, PACK_COLS))
        rows += parts[-1].shape[0]
    if rows % row_multiple:
        parts.append(jnp.zeros((row_multiple - rows % row_multiple, PACK_COLS), dtype))
    return jnp.concatenate(parts, axis=0)


def _unpack(buf, shapes):
    out, row = [], 0
    for sh in shapes:
        n = int(np.prod(sh))
        rows = -(-n // (PACK_ROWS * PACK_COLS)) * PACK_ROWS
        out.append(buf[row:row + rows].reshape(-1)[:n].reshape(sh))
        row += rows
    return out


def _shard_shape(full, axis):
    sh = list(full)
    sh[axis] //= N_CHIPS
    return tuple(sh)


def _split_chips(a, axis):
    sh = a.shape
    return jnp.moveaxis(a.reshape(sh[:axis] + (N_CHIPS, sh[axis] // N_CHIPS) + sh[axis + 1:]), axis, 0)


def _merge_chips(a, axis):
    a = jnp.moveaxis(a, 0, axis)
    sh = a.shape
    return a.reshape(sh[:axis] + (sh[axis] * sh[axis + 1],) + sh[axis + 2:])


def kernel(x, mem, positions, mem_norm, mix_norm, xa_w_kv, xa_q_norm, xa_k_norm, ffn_norm, ffn_w_up, ffn_conv_w, ffn_conv_b, ffn_w_down, a_w_in, a_w_gate2, a_b_gate, a_o_norm, a_w_out, b_w_in, b_q_norm, b_k_norm, b_w_out, c_w_in, c_conv_w, c_conv_b, c_dt_bias, c_a_log, c_d, c_norm, c_w_out, d_w_in, d_lower_bounds, d_o_norm, d_w_out, loss_target, m_mem_norm, m_mix_norm, m_xa_w_kv, m_xa_q_norm, m_xa_k_norm, m_ffn_norm, m_ffn_w_up, m_ffn_conv_w, m_ffn_conv_b, m_ffn_w_down, m_a_w_in, m_a_w_gate2, m_a_b_gate, m_a_o_norm, m_a_w_out, m_b_w_in, m_b_q_norm, m_b_k_norm, m_b_w_out, m_c_w_in, m_c_conv_w, m_c_conv_b, m_c_dt_bias, m_c_a_log, m_c_d, m_c_norm, m_c_w_out, m_d_w_in, m_d_lower_bounds, m_d_o_norm, m_d_w_out, v_mem_norm, v_mix_norm, v_xa_w_kv, v_xa_q_norm, v_xa_k_norm, v_ffn_norm, v_ffn_w_up, v_ffn_conv_w, v_ffn_conv_b, v_ffn_w_down, v_a_w_in, v_a_w_gate2, v_a_b_gate, v_a_o_norm, v_a_w_out, v_b_w_in, v_b_q_norm, v_b_k_norm, v_b_w_out, v_c_w_in, v_c_conv_w, v_c_conv_b, v_c_dt_bias, v_c_a_log, v_c_d, v_c_norm, v_c_w_out, v_d_w_in, v_d_lower_bounds, v_d_o_norm, v_d_w_out):
    args = locals()
    w = {n: args[n] for n in WEIGHTS}
    m = {n: args['m_' + n] for n in WEIGHTS}
    v = {n: args['v_' + n] for n in WEIGHTS}
    cx, cy, cc = lax.axis_index("x"), lax.axis_index("y"), lax.axis_index("c")
    chip = 2 * cx + cy

    big_shard_shapes = [w[n].shape for n in BIG]
    gathered = allgather_chips(_pack([w[n] for n in BIG], MXU_DTYPE))
    per_chip = [_unpack(gathered[j], big_shard_shapes) for j in range(N_CHIPS)]
    full = {n: _merge_chips(jnp.stack([per_chip[j][k] for j in range(N_CHIPS)]), SHARD_AXIS[n])
            for k, n in enumerate(BIG)}
    small_sharded = [n for n in SMALL if n in SHARD_AXIS]
    sg = allgather_devices(_pack([w[n] for n in small_sharded], F32))
    per_chip_s = [_unpack(sg[2 * j], [w[n].shape for n in small_sharded]) for j in range(N_CHIPS)]
    for k, n in enumerate(small_sharded):
        full[n] = _merge_chips(jnp.stack([per_chip_s[j][k] for j in range(N_CHIPS)]), SHARD_AXIS[n])
    for n in SMALL:
        if n not in SHARD_AXIS:
            full[n] = w[n]

    loss_acc, dx, grads = local_step(x[0], mem[0], positions[0], loss_target[0], full)
    loss = lax.psum(jnp.sum(loss_acc), ("x", "y", "c"))

    gb = jnp.stack([_pack([_split_chips(grads[n], SHARD_AXIS[n])[j] for n in BIG], F32, 1024)
                    for j in range(N_CHIPS)])
    rows = gb.shape[1]
    gb = gb.reshape(N_CHIPS, 2, rows // 2, PACK_COLS)
    got = swap_halves_sibling(gb)
    pair = add_own_half(gb, got, cc.reshape(1).astype(jnp.int32), GRAD_WIRE_DTYPE)
    half = sum_chips(exchange_chips(pair), pair, chip.reshape(1).astype(jnp.int32))
    red = join_halves_sibling(half).reshape(rows, PACK_COLS)
    g_big = dict(zip(BIG, _unpack(red, big_shard_shapes)))

    small_full_shapes = [grads[n].shape for n in SMALL]
    gs = sum_slabs(allgather_devices(_pack([grads[n] for n in SMALL], F32, 128)), "sum_devices")
    g_small = {}
    for n, gfull in zip(SMALL, _unpack(gs, small_full_shapes)):
        if n in SHARD_AXIS:
            ax = SHARD_AXIS[n]
            size = gfull.shape[ax] // N_CHIPS
            gfull = lax.dynamic_slice_in_dim(gfull, chip * size, size, axis=ax)
        g_small[n] = gfull

    g_out, delta, new_m, new_v = {}, {}, {}, {}
    for n in BIG:
        sh = w[n].shape
        two_d = (-1, sh[-1])
        d_, m_, v_ = adamw(f"adamw_{n}", w[n].reshape(two_d), g_big[n].reshape(two_d), m[n].reshape(two_d),
                           v[n].reshape(two_d))
        g_out[n], delta[n], new_m[n], new_v[n] = g_big[n], d_.reshape(sh), m_.reshape(sh), v_.reshape(sh)
    small_shapes = [w[n].shape for n in SMALL]
    d_, m_, v_ = adamw("adamw_small", _pack([w[n] for n in SMALL], F32), _pack([g_small[n] for n in SMALL], F32),
                       _pack([m[n] for n in SMALL], F32), _pack([v[n] for n in SMALL], F32))
    for n, a_, b_, c_ in zip(SMALL, _unpack(d_, small_shapes), _unpack(m_, small_shapes), _unpack(v_, small_shapes)):
        g_out[n], delta[n], new_m[n], new_v[n] = g_small[n], a_, b_, c_

    return (loss, dx[None], *[g_out[n] for n in WEIGHTS], *[delta[n] for n in WEIGHTS],
            *[new_m[n] for n in WEIGHTS], *[new_v[n] for n in WEIGHTS])
```

```python
import functools
import math

import jax
import jax.numpy as jnp
import numpy as np
from jax import lax
from jax.experimental import pallas as pl
from jax.experimental.pallas import tpu as pltpu

F32 = jnp.float32
MXU_DTYPE = jnp.bfloat16
GRAD_WIRE_DTYPE = jnp.bfloat16
VMEM_LIMIT_V7X = 56 * 1024 * 1024
LANE = 128
SUBLANE = 8

D_MODEL = 1024
N_MEM = 256
EPS = 1e-6
ROPE_THETA = 10000.0
CHUNK = 64
XA_HEADS, XA_HD = 4, 64
GLA_HEADS, GLA_DK, GLA_DV, GLA_RANK, GLA_GATE_NORM = 4, 96, 192, 16, 16.0
DIL_GROUPS = ((128, 1), (512, 4), (2048, 16))
DIL_HEADS, DIL_HD, DIL_BLOCK = 4, 128, 128
SSM_HD, SSM_HEADS, SSM_GROUPS, SSM_STATE, SSM_CONV = 64, 12, 2, 128, 4
HGRN_HEADS, HGRN_DK = 6, 128
D_FF = 2816
FFN_CONV = 3
ADAM_LR, ADAM_B1, ADAM_B2, ADAM_EPS, ADAM_WD, ADAM_STEP = 0.001, 0.9, 0.999, 1e-08, 0.01, 10

MM_TILES = (1408, 1024, 768, 512, 384, 256, 128)
MM_VMEM_BUDGET = 40 * 1024 * 1024
ROW_BLOCK = 256
SCAN_CHUNKS = 2
PACK_COLS = 1024
PACK_ROWS = 32

WEIGHTS = ['mem_norm', 'mix_norm', 'xa_w_kv', 'xa_q_norm', 'xa_k_norm', 'ffn_norm', 'ffn_w_up', 'ffn_conv_w',
           'ffn_conv_b', 'ffn_w_down', 'a_w_in', 'a_w_gate2', 'a_b_gate', 'a_o_norm', 'a_w_out', 'b_w_in', 'b_q_norm',
           'b_k_norm', 'b_w_out', 'c_w_in', 'c_conv_w', 'c_conv_b', 'c_dt_bias', 'c_a_log', 'c_d', 'c_norm', 'c_w_out',
           'd_w_in', 'd_lower_bounds', 'd_o_norm', 'd_w_out']
SHARD_AXIS = {'xa_w_kv': 1, 'ffn_w_up': 2, 'ffn_conv_w': 2, 'ffn_w_down': 1, 'a_w_in': 1, 'a_w_gate2': 1, 'a_w_out': 0,
              'b_w_in': 1, 'b_w_out': 1, 'c_w_in': 1, 'c_conv_w': 1, 'c_w_out': 0, 'd_w_in': 1, 'd_w_out': 0}
BIG = ['xa_w_kv', 'ffn_w_up', 'ffn_w_down', 'a_w_in', 'a_w_gate2', 'a_w_out', 'b_w_in', 'b_w_out', 'c_w_in', 'c_w_out',
       'd_w_in', 'd_w_out']
SMALL = [n for n in WEIGHTS if n not in BIG]
N_CHIPS = 4
N_DEV = 8


class _MatmulSet:
    def __init__(self, cast, precision):
        def dot(a, b, dims):
            if cast:
                a = a.astype(MXU_DTYPE)
                b = b.astype(MXU_DTYPE)
            return lax.dot_general(a, b, (dims, ((), ())), precision=precision, preferred_element_type=F32)

        @jax.custom_vjp
        def nn(a, b):
            return dot(a, b, ((1,), (0,)))

        @jax.custom_vjp
        def nt(a, b):
            return dot(a, b, ((1,), (1,)))

        @jax.custom_vjp
        def tn(a, b):
            return dot(a, b, ((0,), (0,)))

        nn.defvjp(lambda a, b: (nn(a, b), (a, b)), lambda r, g: (nt(g, r[1]), tn(r[0], g)))
        nt.defvjp(lambda a, b: (nt(a, b), (a, b)), lambda r, g: (nn(g, r[1]), tn(g, r[0])))
        tn.defvjp(lambda a, b: (tn(a, b), (a, b)), lambda r, g: (nt(r[1], g), nn(r[0], g)))
        self.nn, self.nt, self.tn = nn, nt, tn


mm = _MatmulSet(True, None)
hi = _MatmulSet(False, lax.Precision.HIGHEST)


def _sigmoid(x):
    return jax.nn.sigmoid(x)


def _silu(x):
    return x * jax.nn.sigmoid(x)


def _softplus(x):
    return jnp.maximum(x, 0.0) + jnp.log1p(jnp.exp(-jnp.abs(x)))


def _rms(x, g, n_real=None):
    n = n_real or x.shape[-1]
    ms = jnp.sum(x * x, axis=-1, keepdims=True) * (1.0 / n)
    return x * lax.rsqrt(ms + EPS) * g


@jax.custom_vjp
def _swap_halves(x):
    return pltpu.roll(x, 64, 1)


_swap_halves.defvjp(lambda x: (_swap_halves(x), None), lambda _, g: (_swap_halves(g),))


def _tile(n, cands):
    for c in cands:
        if n % c == 0:
            return c
    raise ValueError(f"no tile for {n} among {cands}")


def _cparams(sem):
    return pltpu.CompilerParams(dimension_semantics=sem, vmem_limit_bytes=VMEM_LIMIT_V7X)


def _f32(v):
    return v.astype(F32) if jnp.issubdtype(v.dtype, jnp.floating) else v


def matmul(a, b, *, ta=False, tb=False, add=None, out_dtype=F32):
    m, k = (a.shape[1], a.shape[0]) if ta else a.shape
    n = b.shape[0] if tb else b.shape[1]
    assert k == (b.shape[1] if tb else b.shape[0]), (a.shape, b.shape, ta, tb)
    tk = _tile(k, MM_TILES)
    nk = k // tk
    sa, sb, so = a.dtype.itemsize, b.dtype.itemsize, jnp.dtype(out_dtype).itemsize

    def vmem(tm_, tn_):
        return (2 * tm_ * tk * sa + 2 * tk * tn_ * sb + 2 * tm_ * tn_ * so + (tm_ * tn_ * 4 if nk > 1 else 0)
                + (2 * tm_ * tn_ * add.dtype.itemsize if add is not None else 0))

    fits = [(tm_ * tn_, tm_, tn_) for tm_ in MM_TILES if m % tm_ == 0 for tn_ in MM_TILES if n % tn_ == 0
            if vmem(tm_, tn_) <= MM_VMEM_BUDGET]
    _, tm, tn = max(fits)
    dims = (((0,) if ta else (1,)), ((1,) if tb else (0,)))

    def body(*refs):
        a_ref, b_ref = refs[0], refs[1]
        add_ref = refs[2] if add is not None else None
        o_ref = refs[3] if add is not None else refs[2]
        part = lax.dot_general(a_ref[...].astype(MXU_DTYPE), b_ref[...].astype(MXU_DTYPE), (dims, ((), ())),
                               preferred_element_type=F32)

        def finish(r):
            if add_ref is not None:
                r = r + add_ref[...].astype(F32)
            o_ref[...] = r.astype(o_ref.dtype)

        if nk == 1:
            finish(part)
            return
        acc = refs[-1]
        kk = pl.program_id(2)

        @pl.when(kk == 0)
        def _():
            acc[...] = part

        @pl.when(kk > 0)
        def _():
            acc[...] += part

        @pl.when(kk == nk - 1)
        def _():
            finish(acc[...])

    a_spec = pl.BlockSpec((tk, tm), lambda i, j, q: (q, i)) if ta else pl.BlockSpec((tm, tk), lambda i, j, q: (i, q))
    b_spec = pl.BlockSpec((tn, tk), lambda i, j, q: (j, q)) if tb else pl.BlockSpec((tk, tn), lambda i, j, q: (q, j))
    o_spec = pl.BlockSpec((tm, tn), lambda i, j, q: (i, j))
    ins, specs = [a, b], [a_spec, b_spec]
    if add is not None:
        ins.append(add)
        specs.append(o_spec)
    return pl.pallas_call(
        body, name=f"mm_{m}x{k}x{n}_{int(ta)}{int(tb)}{int(add is not None)}",
        grid=(m // tm, n // tn, nk), in_specs=specs, out_specs=o_spec,
        out_shape=jax.ShapeDtypeStruct((m, n), out_dtype),
        scratch_shapes=[pltpu.VMEM((tm, tn), F32)] if nk > 1 else [],
        compiler_params=_cparams(("parallel", "parallel", "arbitrary")),
    )(*ins)


def _row_spec(a, block):
    return pl.BlockSpec((block, a.shape[1]), lambda i: (i, 0))


def _whole_spec(a):
    return pl.BlockSpec(a.shape, lambda i: (0,) * a.ndim)


def tmap(name, fn, rows, params, outs, block=ROW_BLOCK):
    s = rows[0].shape[0]
    block = min(block, s)
    nr, npar = len(rows), len(params)

    def body(*refs):
        res = fn(*[_f32(r[...]) for r in refs[:nr]], *[_f32(p[...]) for p in refs[nr:nr + npar]])
        for o_ref, v in zip(refs[nr + npar:], res, strict=True):
            o_ref[...] = v.astype(o_ref.dtype)

    return pl.pallas_call(
        body, name=name, grid=(s // block,),
        in_specs=[_row_spec(a, block) for a in rows] + [_whole_spec(p) for p in params],
        out_specs=[pl.BlockSpec((block, w), lambda i: (i, 0)) for w, _ in outs],
        out_shape=[jax.ShapeDtypeStruct((s, w), dt) for w, dt in outs],
        compiler_params=_cparams(("parallel",)),
    )(*rows, *params)


def tmap_bwd(name, fn, rows, params, douts, row_grad, row_add=None, grad_dtype=None, block=ROW_BLOCK):
    s = rows[0].shape[0]
    block = min(block, s)
    grad_dtype = grad_dtype or MXU_DTYPE
    nr, npar, nd = len(rows), len(params), len(douts)
    gr = [i for i in range(nr) if row_grad[i]]
    row_add = row_add or {}
    adds = [row_add[i] for i in gr if i in row_add]

    def body(*refs):
        rv = [_f32(r[...]) for r in refs[:nr]]
        pv = [_f32(p[...]) for p in refs[nr:nr + npar]]
        dv = tuple(_f32(d[...]) for d in refs[nr + npar:nr + npar + nd])
        add_refs = list(refs[nr + npar + nd:nr + npar + nd + len(adds)])
        out_refs = refs[nr + npar + nd + len(adds):]

        def f(*diff):
            rr = list(rv)
            for n_, i_ in enumerate(gr):
                rr[i_] = diff[n_]
            return tuple(fn(*rr, *diff[len(gr):]))

        _, vjp = jax.vjp(f, *[rv[i_] for i_ in gr], *pv)
        g = vjp(dv)
        for n_, i_ in enumerate(gr):
            v = g[n_]
            if i_ in row_add:
                v = v + add_refs.pop(0)[...].astype(F32)
            out_refs[n_][...] = v.astype(out_refs[n_].dtype)
        first = pl.program_id(0) == 0
        for n_ in range(npar):
            ref = out_refs[len(gr) + n_]

            @pl.when(first)
            def _(ref=ref):
                ref[...] = jnp.zeros_like(ref)

            ref[...] += g[len(gr) + n_]

    res = pl.pallas_call(
        body, name=name, grid=(s // block,),
        in_specs=[_row_spec(a, block) for a in rows] + [_whole_spec(p) for p in params]
        + [_row_spec(d, block) for d in douts] + [_row_spec(a, block) for a in adds],
        out_specs=[_row_spec(rows[i], block) for i in gr] + [_whole_spec(p) for p in params],
        out_shape=[jax.ShapeDtypeStruct(rows[i].shape, F32 if i in row_add else grad_dtype) for i in gr]
        + [jax.ShapeDtypeStruct(p.shape, F32) for p in params],
        compiler_params=_cparams(("arbitrary",)),
    )(*rows, *params, *douts, *adds)
    return list(res[:len(gr)]), list(res[len(gr):])


def rscan(name, fn, state_shapes, rows, params, outs, block):
    s = rows[0].shape[0]
    nsteps = s // block
    nr, npar, no, ns = len(rows), len(params), len(outs), len(state_shapes)

    def body(*refs):
        out_refs = refs[nr + npar:nr + npar + no]
        sav_refs = refs[nr + npar + no:nr + npar + no + ns]
        st_refs = refs[nr + npar + no + ns:]

        @pl.when(pl.program_id(0) == 0)
        def _():
            for st in st_refs:
                st[...] = jnp.zeros_like(st)

        sts = tuple(st[...] for st in st_refs)
        for sv, v in zip(sav_refs, sts):
            sv[...] = v
        new, res = fn(sts, *[_f32(r[...]) for r in refs[:nr]], *[_f32(p[...]) for p in refs[nr:nr + npar]])
        for st, v in zip(st_refs, new, strict=True):
            st[...] = v
        for o_ref, v in zip(out_refs, res, strict=True):
            o_ref[...] = v.astype(o_ref.dtype)

    res = pl.pallas_call(
        body, name=name, grid=(nsteps,),
        in_specs=[_row_spec(a, block) for a in rows] + [_whole_spec(p) for p in params],
        out_specs=[pl.BlockSpec((block, w), lambda i: (i, 0)) for w, _ in outs]
        + [pl.BlockSpec(sh, lambda i: (i, 0)) for sh in state_shapes],
        out_shape=[jax.ShapeDtypeStruct((s, w), dt) for w, dt in outs]
        + [jax.ShapeDtypeStruct((nsteps * sh[0], sh[1]), F32) for sh in state_shapes],
        scratch_shapes=[pltpu.VMEM(sh, F32) for sh in state_shapes],
        compiler_params=_cparams(("arbitrary",)),
    )(*rows, *params)
    return list(res[:no]), list(res[no:])


def rscan_bwd(name, fn, saved, rows, params, douts, block, grad_dtype=None):
    s = rows[0].shape[0]
    nsteps = s // block
    grad_dtype = grad_dtype or MXU_DTYPE
    nr, npar, nd, ns = len(rows), len(params), len(douts), len(saved)
    state_shapes = [(sv.shape[0] // nsteps, sv.shape[1]) for sv in saved]

    def body(*refs):
        rv = [_f32(r[...]) for r in refs[:nr]]
        pv = [_f32(p[...]) for p in refs[nr:nr + npar]]
        dv = tuple(_f32(d[...]) for d in refs[nr + npar:nr + npar + nd])
        sv = tuple(x[...] for x in refs[nr + npar + nd:nr + npar + nd + ns])
        out_refs = refs[nr + npar + nd + ns:nr + npar + nd + ns + nr + npar]
        dst_refs = refs[nr + npar + nd + ns + nr + npar:]
        first = pl.program_id(0) == 0

        @pl.when(first)
        def _():
            for d in dst_refs:
                d[...] = jnp.zeros_like(d)

        def f(sts, *args):
            return fn(sts, *args)

        _, vjp = jax.vjp(f, sv, *rv, *pv)
        g = vjp((tuple(d[...] for d in dst_refs), dv))
        for d, v in zip(dst_refs, g[0], strict=True):
            d[...] = v
        for n_ in range(nr):
            out_refs[n_][...] = g[1 + n_].astype(out_refs[n_].dtype)
        for n_ in range(npar):
            ref = out_refs[nr + n_]

            @pl.when(first)
            def _(ref=ref):
                ref[...] = jnp.zeros_like(ref)

            ref[...] += g[1 + nr + n_]

    rev = lambda i: (nsteps - 1 - i, 0)
    res = pl.pallas_call(
        body, name=name, grid=(nsteps,),
        in_specs=[pl.BlockSpec((block, a.shape[1]), rev) for a in rows] + [_whole_spec(p) for p in params]
        + [pl.BlockSpec((block, d.shape[1]), rev) for d in douts] + [pl.BlockSpec(sh, rev) for sh in state_shapes],
        out_specs=[pl.BlockSpec((block, a.shape[1]), rev) for a in rows] + [_whole_spec(p) for p in params],
        out_shape=[jax.ShapeDtypeStruct(a.shape, grad_dtype) for a in rows]
        + [jax.ShapeDtypeStruct(p.shape, F32) for p in params],
        scratch_shapes=[pltpu.VMEM(sh, F32) for sh in state_shapes],
        compiler_params=_cparams(("arbitrary",)),
    )(*rows, *params, *douts, *saved)
    return list(res[:nr]), list(res[nr:])


def _norm_stage(x, g):
    return (_rms(x, g),)


def _tril():
    r = lax.broadcasted_iota(jnp.int32, (CHUNK, CHUNK), 0)
    c = lax.broadcasted_iota(jnp.int32, (CHUNK, CHUNK), 1)
    return r >= c


def _gla_chunk(st, q, k, v, la, b):
    tril = _tril()
    rowi = lax.broadcasted_iota(jnp.int32, (CHUNK, 1), 0)
    b_last = jnp.sum(la, axis=0, keepdims=True)
    b_ref = jnp.sum(jnp.where(rowi < CHUNK // 2, la, 0.0), axis=0, keepdims=True)
    att = mm.nt(q * jnp.exp(b - b_ref), k * jnp.exp(b_ref - b))
    att = jnp.where(tril, att, 0.0)
    o = mm.nn(att, v) + mm.nn(q * jnp.exp(b), st)
    decay = jnp.exp(jnp.broadcast_to(b_last, (LANE, LANE)).T)
    decay = jnp.concatenate([decay] * (v.shape[1] // LANE), axis=1)
    st2 = decay * st + mm.tn(k * jnp.exp(b_last - b), v)
    return st2, o


def _gla_step(heads, vp, scale):
    kp = LANE

    def fn(states, q, k, v, la):
        sts = list(states)
        trif = _tril().astype(F32)
        rows = []
        for c in range(q.shape[0] // CHUNK):
            r = slice(c * CHUNK, (c + 1) * CHUNK)
            b_all = hi.nn(trif, la[r])
            oh = []
            for h in range(heads):
                ks, vs = slice(h * kp, (h + 1) * kp), slice(h * vp, (h + 1) * vp)
                qh = q[r, ks] * scale if scale != 1.0 else q[r, ks]
                sts[h], o = _gla_chunk(sts[h], qh, k[r, ks], v[r, vs], la[r, ks], b_all[:, ks])
                oh.append(o)
            rows.append(jnp.concatenate(oh, axis=1))
        return tuple(sts), (jnp.concatenate(rows, axis=0),)

    return fn


def _ssd_step(states, xa, dtr, dtb, alog, dsk):
    sts = list(states)
    trif = _tril().astype(F32)
    wide = lax.broadcasted_iota(jnp.int32, (CHUNK, LANE), 0) >= lax.broadcasted_iota(jnp.int32, (CHUNK, LANE), 1)
    hg = SSM_HEADS // SSM_GROUPS
    xw = SSM_HEADS * LANE
    lane, head = lax.broadcasted_iota(jnp.int32, (LANE, xw), 1), lax.broadcasted_iota(jnp.int32, (LANE, xw), 0)
    spread = ((lane >= head * LANE) & (lane < (head + 1) * LANE)).astype(F32)
    neg_a = -jnp.exp(alog)
    pad = jnp.zeros((CHUNK, LANE), F32)
    rows = []
    for c in range(xa.shape[0] // CHUNK):
        r = slice(c * CHUNK, (c + 1) * CHUNK)
        dt_all = _softplus(hi.nn(dtr[r], spread) + dtb)
        a_all = dt_all * neg_a
        acs_all = hi.nn(trif, a_all)
        last_all = jnp.sum(a_all, axis=0, keepdims=True)
        yh = []
        for g in range(SSM_GROUPS):
            bm = xa[r, xw + g * LANE:xw + (g + 1) * LANE]
            cm = xa[r, xw + (SSM_GROUPS + g) * LANE:xw + (SSM_GROUPS + g + 1) * LANE]
            cb = mm.nt(cm, jnp.concatenate([bm, pad], axis=0))
            for hh in range(hg):
                h = g * hg + hh
                ls = slice(h * LANE, (h + 1) * LANE)
                xs, acs, acs_last = xa[r, ls], acs_all[:, ls], last_all[:, ls]
                xdt = xs * dt_all[:, ls]
                seg = acs - jnp.concatenate([acs, pad], axis=0).T[:CHUNK]
                lmat = jnp.exp(jnp.where(wide, seg, -1e30))
                y = (mm.nn(cb * lmat, jnp.concatenate([xdt, pad], axis=0)) + mm.nn(cm, sts[h]) * jnp.exp(acs)
                     + dsk[:, ls] * xs)
                sts[h] = jnp.exp(acs_last) * sts[h] + mm.tn(bm, xdt * jnp.exp(acs_last - acs))
                yh.append(y)
        rows.append(jnp.concatenate(yh, axis=1))
    return tuple(sts), (jnp.concatenate(rows, axis=0),)


def _gla_pre(glr, w2, bg):
    z = mm.nn(glr, w2) + bg
    return (-_softplus(-z) * (1.0 / GLA_GATE_NORM),)


def _gla_post(o, og, g):
    w = 2 * LANE
    return (jnp.concatenate([_rms(o[:, h * w:(h + 1) * w], g, GLA_DV) * _silu(og[:, h * w:(h + 1) * w])
                             for h in range(GLA_HEADS)], axis=1),)


def _hgrn_pre(q, f, lbnd):
    e = jnp.exp(lbnd - jnp.max(lbnd, axis=0, keepdims=True))
    rowi = lax.broadcasted_iota(jnp.int32, e.shape, 0)
    lb = jnp.sum(jnp.where(rowi >= 1, e, 0.0), axis=0, keepdims=True) / jnp.sum(e, axis=0, keepdims=True)
    fg = lb + (1.0 - lb) * _sigmoid(f)
    return _silu(q), 1.0 - fg, jnp.log(fg)


def _hgrn_post(o, og, g):
    return (jnp.concatenate([_rms(o[:, h * LANE:(h + 1) * LANE], g) for h in range(HGRN_HEADS)], axis=1)
            * _sigmoid(og),)


def _mamba_post(y, z, g):
    v = y * _silu(z)
    w = (SSM_HEADS // SSM_GROUPS) * LANE
    n_real = (SSM_HEADS // SSM_GROUPS) * SSM_HD
    return (jnp.concatenate([_rms(v[:, i * w:(i + 1) * w], g[:, i * w:(i + 1) * w], n_real)
                             for i in range(SSM_GROUPS)], axis=1),)


def _dil_pre(q, k, cosf, sinf, qg, kg):
    def groups(x, g):
        out = []
        for grp in range(len(DIL_GROUPS)):
            hs = []
            for h in range(grp * DIL_HEADS, (grp + 1) * DIL_HEADS):
                n = _rms(x[:, h * LANE:(h + 1) * LANE], g)
                hs.append(n * cosf + _swap_halves(n) * sinf)
            out.append(jnp.concatenate(hs, axis=1))
        return out

    return (*groups(q, qg), *groups(k, kg))


def _dil_merge(o0, o1, o2, l0, l1, l2):
    m = jnp.maximum(jnp.maximum(l0, l1), l2)
    e0, e1, e2 = jnp.exp(l0 - m), jnp.exp(l1 - m), jnp.exp(l2 - m)
    return ((e0 * o0 + e1 * o1 + e2 * o2) / (e0 + e1 + e2),)


def _dil_block(q, kp, kc, vp, vc, lim):
    kk = jnp.concatenate([kp, kc], axis=0)
    vv = jnp.concatenate([vp, vc], axis=0)
    s = mm.nt(q, kk) * (DIL_HD ** -0.5)
    i = lax.broadcasted_iota(jnp.int32, s.shape, 0)
    j = lax.broadcasted_iota(jnp.int32, s.shape, 1)
    dist = DIL_BLOCK + i - j
    s = jnp.where((dist >= 0) & (dist <= DIL_BLOCK) & (j >= lim), s, -1e30)
    m = jnp.max(s, axis=-1, keepdims=True)
    p = jnp.exp(s - m)
    l = jnp.sum(p, axis=-1, keepdims=True)
    return mm.nn(p / l, vv), jnp.broadcast_to(m + jnp.log(l), (q.shape[0], LANE))


def _xattn(xq, kv, qg, kg):
    w = XA_HEADS * LANE
    os_ = []
    for h in range(XA_HEADS):
        ls = slice(h * LANE, (h + 1) * LANE)
        q = _rms(xq[:, ls], qg, XA_HD)
        k = _rms(kv[:, ls], kg, XA_HD)
        s = mm.nt(q, k) * (XA_HD ** -0.5)
        p = jnp.exp(s - jnp.max(s, axis=-1, keepdims=True))
        p = p / jnp.sum(p, axis=-1, keepdims=True)
        os_.append(mm.nn(p, kv[:, w + h * LANE:w + (h + 1) * LANE]))
    return (jnp.concatenate(os_, axis=1),)


def dil_attn(name, q, k, v, r, g):
    s, w = q.shape
    l = s // r
    nb = l // DIL_BLOCK
    ng = v.shape[1] // w
    q2, k2, v2 = (t.reshape(l, r * t.shape[1]) for t in (q, k, v))

    def body(q_r, kp_r, kc_r, vp_r, vc_r, o_r, l_r):
        lim = jnp.where(pl.program_id(1) == 0, DIL_BLOCK, 0)
        for h in range(DIL_HEADS):
            ls = slice(h * LANE, (h + 1) * LANE)
            o, lse = _dil_block(q_r[:, ls], kp_r[:, ls], kc_r[:, ls], vp_r[:, ls], vc_r[:, ls], lim)
            o_r[:, ls] = o
            l_r[:, ls] = lse

    cur = pl.BlockSpec((DIL_BLOCK, w), lambda res, n: (n, res))
    prev = pl.BlockSpec((DIL_BLOCK, w), lambda res, n: (jnp.maximum(n - 1, 0), res))
    vcur = pl.BlockSpec((DIL_BLOCK, w), lambda res, n: (n, res * ng + g))
    vprev = pl.BlockSpec((DIL_BLOCK, w), lambda res, n: (jnp.maximum(n - 1, 0), res * ng + g))
    o, lse = pl.pallas_call(
        body, name=name, grid=(r, nb), in_specs=[cur, prev, cur, vprev, vcur], out_specs=[cur, cur],
        out_shape=[jax.ShapeDtypeStruct((l, r * w), F32)] * 2,
        compiler_params=_cparams(("parallel", "parallel")),
    )(q2, k2, k2, v2, v2)
    return o.reshape(s, w), lse.reshape(s, w)


def dil_attn_bwd(name, q, k, v, do, dlse, r, g):
    s, w = q.shape
    l = s // r
    nb = l // DIL_BLOCK
    ng = v.shape[1] // w
    q2, k2, v2, do2, dl2 = (t.reshape(l, r * t.shape[1]) for t in (q, k, v, do, dlse))

    def body(q_r, kp_r, kc_r, vp_r, vc_r, do_r, dl_r, dq_r, dk_r, dv_r, ck, cv):
        i = pl.program_id(1)
        lim = jnp.where(i == nb - 1, DIL_BLOCK, 0)

        @pl.when(i == 0)
        def _():
            ck[...] = jnp.zeros_like(ck)
            cv[...] = jnp.zeros_like(cv)

        for h in range(DIL_HEADS):
            ls = slice(h * LANE, (h + 1) * LANE)
            _, vjp = jax.vjp(functools.partial(_dil_block, lim=lim),
                             q_r[:, ls], kp_r[:, ls], kc_r[:, ls], vp_r[:, ls], vc_r[:, ls])
            gq, gkp, gkc, gvp, gvc = vjp((do_r[:, ls], dl_r[:, ls]))
            dq_r[:, ls] = gq.astype(dq_r.dtype)
            dk_r[:, ls] = (gkc + ck[:, ls]).astype(dk_r.dtype)
            dv_r[:, ls] = (gvc + cv[:, ls]).astype(dv_r.dtype)
            ck[:, ls] = gkp
            cv[:, ls] = gvp

    cur = pl.BlockSpec((DIL_BLOCK, w), lambda res, i: (nb - 1 - i, res))
    prev = pl.BlockSpec((DIL_BLOCK, w), lambda res, i: (jnp.maximum(nb - 2 - i, 0), res))
    vcur = pl.BlockSpec((DIL_BLOCK, w), lambda res, i: (nb - 1 - i, res * ng + g))
    vprev = pl.BlockSpec((DIL_BLOCK, w), lambda res, i: (jnp.maximum(nb - 2 - i, 0), res * ng + g))
    dq, dk, dv = pl.pallas_call(
        body, name=name, grid=(r, nb), in_specs=[cur, prev, cur, vprev, vcur, cur, cur], out_specs=[cur, cur, cur],
        out_shape=[jax.ShapeDtypeStruct((l, r * w), F32), jax.ShapeDtypeStruct((l, r * w), F32),
                   jax.ShapeDtypeStruct((l, r * w), MXU_DTYPE)],
        scratch_shapes=[pltpu.VMEM((DIL_BLOCK, w), F32)] * 2,
        compiler_params=_cparams(("parallel", "arbitrary")),
    )(q2, k2, k2, v2, v2, do2, dl2)
    return dq.reshape(s, w), dk.reshape(s, w), dv.reshape(s, w)


def _dsilu(u):
    sg = _sigmoid(u)
    return sg * (1.0 + u * (1.0 - sg))


CONV_STRIP = 16


def _shifted_rows(prev8, cur_r, next8, lanes, s0, n, sh, block):
    if s0 - sh < 0:
        assert s0 == 0
        xp = jnp.concatenate([prev8, cur_r[0:n, lanes]], axis=0)
        return pltpu.roll(xp, sh, 0)[SUBLANE:SUBLANE + n]
    if s0 - sh + n > block:
        assert s0 == block and n == SUBLANE
        xp = jnp.concatenate([cur_r[block - SUBLANE:block, lanes], next8], axis=0)
        return (pltpu.roll(xp, sh, 0) if sh else xp)[SUBLANE:]
    return cur_r[pl.ds(s0 - sh, n), lanes]


def conv_fwd(name, x, w, b, mode, out_dtype, tc, block=ROW_BLOCK):
    s, c = x.shape
    ntap = w.shape[0]
    block = min(block, s)
    f = c // 2 if mode == 'glu' else c
    nh = 2 if mode == 'glu' else 1
    off = f // tc

    def body(*refs):
        first = pl.program_id(1) == 0
        o_ref = refs[-1]

        def column(cidx, carry):
            lanes = pl.ds(pl.multiple_of(cidx * LANE, LANE), LANE)
            prevs = [jnp.where(first, 0.0, refs[4 * hlf][:, lanes]) for hlf in range(nh)]
            for s0 in range(0, block, CONV_STRIP):
                us = []
                for hlf in range(nh):
                    _, cur_r, w_r, b_r = refs[4 * hlf:4 * hlf + 4]
                    acc = b_r[:, lanes]
                    for j in range(ntap):
                        xs = _shifted_rows(prevs[hlf], cur_r, None, lanes, s0, CONV_STRIP, ntap - 1 - j, block)
                        acc = acc + w_r[j:j + 1, lanes] * xs
                    us.append(acc)
                res = _silu(us[0]) * us[1] if mode == 'glu' else _silu(us[0])
                o_ref[pl.ds(s0, CONV_STRIP), lanes] = res.astype(o_ref.dtype)
            return carry

        lax.fori_loop(0, tc // LANE, column, 0)

    rb = block // SUBLANE
    ins, specs = [], []
    for hlf in range(nh):
        o = hlf * off
        ins += [x, x, w, b]
        specs += [pl.BlockSpec((SUBLANE, tc), lambda j, i, o=o: (jnp.maximum(i * rb - 1, 0), j + o)),
                  pl.BlockSpec((block, tc), lambda j, i, o=o: (i, j + o)),
                  pl.BlockSpec((ntap, tc), lambda j, i, o=o: (0, j + o)),
                  pl.BlockSpec((1, tc), lambda j, i, o=o: (0, j + o))]
    return pl.pallas_call(
        body, name=name, grid=(f // tc, s // block), in_specs=specs,
        out_specs=pl.BlockSpec((block, tc), lambda j, i: (i, j)),
        out_shape=jax.ShapeDtypeStruct((s, f), out_dtype),
        compiler_params=_cparams(("parallel", "parallel")),
    )(*ins)


def conv_bwd(name, x, w, b, dout, mode, tc, block=ROW_BLOCK):
    s, c = x.shape
    ntap = w.shape[0]
    block = min(block, s)
    nblk = s // block
    f = c // 2 if mode == 'glu' else c
    nh = 2 if mode == 'glu' else 1
    off = f // tc
    ext = block + SUBLANE

    def body(*refs):
        i = pl.program_id(1)
        first, last = i == 0, i == nblk - 1
        dcur_r, dnext_r = refs[5 * nh], refs[5 * nh + 1]
        outs = refs[5 * nh + 2:5 * nh + 2 + 3 * nh]
        du_scr = refs[5 * nh + 2 + 3 * nh:]

        @pl.when(first)
        def _():
            for hlf in range(nh):
                outs[3 * hlf + 1][...] = jnp.zeros_like(outs[3 * hlf + 1])
                outs[3 * hlf + 2][...] = jnp.zeros_like(outs[3 * hlf + 2])

        def column(cidx, carry):
            lanes = pl.ds(pl.multiple_of(cidx * LANE, LANE), LANE)
            prevs = [jnp.where(first, 0.0, refs[5 * hlf][:, lanes]) for hlf in range(nh)]
            nexts = [jnp.where(last, 0.0, refs[5 * hlf + 2][:, lanes]) for hlf in range(nh)]
            db_acc = [jnp.zeros((CONV_STRIP, LANE), F32) for _ in range(nh)]
            dw_acc = [[jnp.zeros((CONV_STRIP, LANE), F32) for _ in range(ntap)] for _ in range(nh)]
            for s0 in range(0, ext, CONV_STRIP):
                n = min(CONV_STRIP, ext - s0)
                d_e = dcur_r[pl.ds(s0, n), lanes] if s0 < block else jnp.where(last, 0.0, dnext_r[:, lanes])
                xs, us = [], []
                for hlf in range(nh):
                    cur_r, w_r, b_r = refs[5 * hlf + 1], refs[5 * hlf + 3], refs[5 * hlf + 4]
                    sh_rows = [_shifted_rows(prevs[hlf], cur_r, nexts[hlf], lanes, s0, n, ntap - 1 - j, block)
                               for j in range(ntap)]
                    acc = b_r[:, lanes]
                    for j in range(ntap):
                        acc = acc + w_r[j:j + 1, lanes] * sh_rows[j]
                    xs.append(sh_rows)
                    us.append(acc)
                dus = [d_e * us[1] * _dsilu(us[0]), d_e * _silu(us[0])] if mode == 'glu' else [d_e * _dsilu(us[0])]
                for hlf in range(nh):
                    du_scr[hlf][pl.ds(s0, n), lanes] = dus[hlf]
                    if s0 < block:
                        db_acc[hlf] = db_acc[hlf] + dus[hlf]
                        for j in range(ntap):
                            dw_acc[hlf][j] = dw_acc[hlf][j] + dus[hlf] * xs[hlf][j]
            for hlf in range(nh):
                w_r = refs[5 * hlf + 3]
                dx_r, dw_r, db_r = outs[3 * hlf:3 * hlf + 3]
                db_r[:, lanes] += jnp.sum(db_acc[hlf], axis=0, keepdims=True)
                for j in range(ntap):
                    dw_r[j:j + 1, lanes] += jnp.sum(dw_acc[hlf][j], axis=0, keepdims=True)
                for s0 in range(0, block, CONV_STRIP):
                    dx = None
                    for j in range(ntap):
                        term = w_r[j:j + 1, lanes] * du_scr[hlf][pl.ds(s0 + ntap - 1 - j, CONV_STRIP), lanes]
                        dx = term if dx is None else dx + term
                    dx_r[pl.ds(s0, CONV_STRIP), lanes] = dx.astype(dx_r.dtype)
            return carry

        lax.fori_loop(0, tc // LANE, column, 0)

    rb = block // SUBLANE
    nrow8 = s // SUBLANE
    ins, specs = [], []
    for hlf in range(nh):
        o = hlf * off
        ins += [x, x, x, w, b]
        specs += [pl.BlockSpec((SUBLANE, tc), lambda j, i, o=o: (jnp.maximum(i * rb - 1, 0), j + o)),
                  pl.BlockSpec((block, tc), lambda j, i, o=o: (i, j + o)),
                  pl.BlockSpec((SUBLANE, tc), lambda j, i, o=o: (jnp.minimum((i + 1) * rb, nrow8 - 1), j + o)),
                  pl.BlockSpec((ntap, tc), lambda j, i, o=o: (0, j + o)),
                  pl.BlockSpec((1, tc), lambda j, i, o=o: (0, j + o))]
    ins += [dout, dout]
    specs += [pl.BlockSpec((block, tc), lambda j, i: (i, j)),
              pl.BlockSpec((SUBLANE, tc), lambda j, i: (jnp.minimum((i + 1) * rb, nrow8 - 1), j))]
    out_specs, out_shape = [], []
    for hlf in range(nh):
        out_specs += [pl.BlockSpec((block, tc), lambda j, i: (i, j)), pl.BlockSpec((ntap, tc), lambda j, i: (0, j)),
                      pl.BlockSpec((1, tc), lambda j, i: (0, j))]
        out_shape += [jax.ShapeDtypeStruct((s, f), MXU_DTYPE), jax.ShapeDtypeStruct((ntap, f), F32),
                      jax.ShapeDtypeStruct((1, f), F32)]
    res = pl.pallas_call(
        body, name=name, grid=(f // tc, nblk), in_specs=specs, out_specs=out_specs, out_shape=out_shape,
        scratch_shapes=[pltpu.VMEM((ext, tc), F32)] * nh,
        compiler_params=_cparams(("parallel", "arbitrary")),
    )(*ins)
    if nh == 1:
        return [res[0]], res[1], res[2]
    return [res[0], res[3]], jnp.concatenate([res[1], res[4]], axis=1), jnp.concatenate([res[2], res[5]], axis=1)


def loss_head(y, target, block=ROW_BLOCK):
    s, d = y.shape
    block = min(block, s)

    def body(y_r, t_r, acc_r, dy_r):
        e = y_r[...] - t_r[...]
        dy_r[...] = e * (1.0 / d)

        @pl.when(pl.program_id(0) == 0)
        def _():
            acc_r[...] = jnp.zeros_like(acc_r)

        acc_r[...] += jnp.sum((e * e).reshape(block // SUBLANE, SUBLANE, d), axis=0) * (0.5 / d)

    return pl.pallas_call(
        body, name="loss_head", grid=(s // block,),
        in_specs=[pl.BlockSpec((block, d), lambda i: (i, 0))] * 2,
        out_specs=[pl.BlockSpec((SUBLANE, d), lambda i: (0, 0)), pl.BlockSpec((block, d), lambda i: (i, 0))],
        out_shape=[jax.ShapeDtypeStruct((SUBLANE, d), F32), jax.ShapeDtypeStruct((s, d), F32)],
        compiler_params=_cparams(("arbitrary",)),
    )(y, target)


def adamw(name, w, g, m, v):
    r, c = w.shape
    tr = r if r <= 512 else _tile(r, (512, 256, 128, 64, 32, 16, 8))
    if c * tr * 4 > (1 << 21):
        tr = _tile(r, (256, 128, 64, 32, 16, 8))

    def body(w_r, g_r, m_r, v_r, d_r, nm_r, nv_r):
        gg = g_r[...]
        nm = ADAM_B1 * m_r[...] + (1.0 - ADAM_B1) * gg
        nv = ADAM_B2 * v_r[...] + (1.0 - ADAM_B2) * (gg * gg)
        m_hat = nm / (1.0 - ADAM_B1 ** ADAM_STEP)
        v_hat = nv / (1.0 - ADAM_B2 ** ADAM_STEP)
        d_r[...] = -ADAM_LR * (m_hat / (jnp.sqrt(v_hat) + ADAM_EPS) + ADAM_WD * w_r[...])
        nm_r[...] = nm
        nv_r[...] = nv

    spec = pl.BlockSpec((tr, c), lambda i: (i, 0))
    return pl.pallas_call(
        body, name=name, grid=(r // tr,), in_specs=[spec] * 4, out_specs=[spec] * 3,
        out_shape=[jax.ShapeDtypeStruct((r, c), F32)] * 3, compiler_params=_cparams(("parallel",)),
    )(w, g, m, v)


MESH = pl.DeviceIdType.MESH
_ANY = pl.BlockSpec(memory_space=pl.ANY)


def _place():
    return lax.axis_index("x"), lax.axis_index("y"), lax.axis_index("c")


def allgather_chips(shard):
    r, c = shard.shape
    hr = r // 2

    def body(w_ref, out_ref, send_sems, recv_sems):
        x, y, cc = _place()
        sibling = (x, y, 1 - cc)
        chips = [(1 - x, y), (x, 1 - y), (1 - x, 1 - y)]

        def half(chip, core):
            return out_ref.at[2 * chip[0] + chip[1], pl.ds(core * hr, hr), :]

        def copy(k, chip, core, to, src=None):
            return pltpu.make_async_remote_copy(
                src_ref=half(chip, core) if src is None else src, dst_ref=half(chip, core),
                send_sem=send_sems.at[k], recv_sem=recv_sems.at[k], device_id=to, device_id_type=MESH)

        my_half = w_ref.at[pl.ds(cc * hr, hr), :]
        first = [copy(j, (x, y), cc, (*chip, cc), src=my_half) for j, chip in enumerate(chips)]
        for cp in first:
            cp.start()
        passed = [copy(3 + j, chip, cc, sibling) for j, chip in enumerate(chips)]
        for j, chip in enumerate(chips):
            copy(j, chip, cc, (x, y, cc)).wait_recv()
            passed[j].start()
        for j, chip in enumerate(chips):
            copy(3 + j, chip, 1 - cc, (x, y, cc)).wait_recv()
        for cp in first + passed:
            cp.wait_send()

    out = pl.pallas_call(
        body, name="allgather_chips", in_specs=[_ANY], out_specs=_ANY,
        out_shape=jax.ShapeDtypeStruct((N_CHIPS, r, c), shard.dtype),
        scratch_shapes=[pltpu.SemaphoreType.DMA((6,)), pltpu.SemaphoreType.DMA((6,))],
    )(shard)
    chip = 2 * lax.axis_index("x") + lax.axis_index("y")
    return lax.dynamic_update_slice(out, shard[None], (chip, 0, 0))


def allgather_devices(buf):
    r, c = buf.shape

    def body(b_ref, out_ref, send_sems, recv_sems, local_sem):
        x, y, cc = _place()
        me = 4 * x + 2 * y + cc
        mine = pltpu.make_async_copy(b_ref, out_ref.at[me], local_sem)
        mine.start()
        copies = []
        for k in range(1, N_DEV):
            px, py, pc = x ^ (k >> 2), y ^ ((k >> 1) & 1), cc ^ (k & 1)
            cp = pltpu.make_async_remote_copy(src_ref=b_ref, dst_ref=out_ref.at[me], send_sem=send_sems.at[k - 1],
                                              recv_sem=recv_sems.at[k - 1], device_id=(px, py, pc), device_id_type=MESH)
            cp.start()
            copies.append((cp, 4 * px + 2 * py + pc))
        for k, (cp, peer) in enumerate(copies):
            pltpu.make_async_remote_copy(src_ref=b_ref, dst_ref=out_ref.at[peer], send_sem=send_sems.at[k],
                                         recv_sem=recv_sems.at[k], device_id=(x, y, cc), device_id_type=MESH).wait_recv()
        for cp, _ in copies:
            cp.wait_send()
        mine.wait()

    return pl.pallas_call(
        body, name="allgather_devices", in_specs=[_ANY], out_specs=_ANY,
        out_shape=jax.ShapeDtypeStruct((N_DEV, r, c), buf.dtype),
        scratch_shapes=[pltpu.SemaphoreType.DMA((N_DEV - 1,)), pltpu.SemaphoreType.DMA((N_DEV - 1,)),
                        pltpu.SemaphoreType.DMA],
    )(buf)


def swap_halves_sibling(g):
    n, _, r, c = g.shape

    def body(g_ref, out_ref, send_sem, recv_sem):
        x, y, cc = _place()
        cp = pltpu.make_async_remote_copy(src_ref=g_ref.at[:, 1 - cc], dst_ref=out_ref, send_sem=send_sem,
                                          recv_sem=recv_sem, device_id=(x, y, 1 - cc), device_id_type=MESH)
        cp.start()
        cp.wait()

    return pl.pallas_call(
        body, name="swap_halves_sibling", in_specs=[_ANY], out_specs=_ANY,
        out_shape=jax.ShapeDtypeStruct((n, r, c), g.dtype),
        scratch_shapes=[pltpu.SemaphoreType.DMA, pltpu.SemaphoreType.DMA],
    )(g)


def exchange_chips(p):
    n, r, c = p.shape

    def body(p_ref, out_ref, send_sems, recv_sems):
        x, y, cc = _place()
        me = 2 * x + y
        chips = [(1 - x, y), (x, 1 - y), (1 - x, 1 - y)]
        cps = []
        for j, chip in enumerate(chips):
            cp = pltpu.make_async_remote_copy(src_ref=p_ref.at[2 * chip[0] + chip[1]], dst_ref=out_ref.at[me],
                                              send_sem=send_sems.at[j], recv_sem=recv_sems.at[j],
                                              device_id=(*chip, cc), device_id_type=MESH)
            cp.start()
            cps.append(cp)
        for j, chip in enumerate(chips):
            pltpu.make_async_remote_copy(src_ref=p_ref.at[me], dst_ref=out_ref.at[2 * chip[0] + chip[1]],
                                         send_sem=send_sems.at[j], recv_sem=recv_sems.at[j],
                                         device_id=(x, y, cc), device_id_type=MESH).wait_recv()
        for cp in cps:
            cp.wait_send()

    return pl.pallas_call(
        body, name="exchange_chips", in_specs=[_ANY], out_specs=_ANY,
        out_shape=jax.ShapeDtypeStruct((n, r, c), p.dtype),
        scratch_shapes=[pltpu.SemaphoreType.DMA((3,)), pltpu.SemaphoreType.DMA((3,))],
    )(p)


def sum_chips(got, own, chip):
    n, r, c = got.shape
    tr = _tile(r, (512, 256, 128, 64, 32, 16))

    def body(chip_ref, got_r, own_r, out_r):
        mine = own_r[...].astype(F32)
        acc = None
        for k in range(n):
            term = jnp.where(chip_ref[0] == k, mine, got_r[k].astype(F32))
            acc = term if acc is None else acc + term
        out_r[...] = acc

    return pl.pallas_call(
        body, name="sum_chips",
        grid_spec=pltpu.PrefetchScalarGridSpec(
            num_scalar_prefetch=1, grid=(r // tr,),
            in_specs=[pl.BlockSpec((n, tr, c), lambda i, chip_ref: (0, i, 0)),
                      pl.BlockSpec((None, tr, c), lambda i, chip_ref: (chip_ref[0], i, 0))],
            out_specs=pl.BlockSpec((tr, c), lambda i, chip_ref: (i, 0))),
        out_shape=jax.ShapeDtypeStruct((r, c), F32),
        compiler_params=_cparams(("parallel",)),
    )(chip, got, own)


def join_halves_sibling(h):
    r, c = h.shape

    def body(h_ref, out_ref, send_sem, recv_sem):
        x, y, cc = _place()
        cp = pltpu.make_async_remote_copy(src_ref=h_ref, dst_ref=out_ref.at[cc], send_sem=send_sem, recv_sem=recv_sem,
                                          device_id=(x, y, 1 - cc), device_id_type=MESH)
        cp.start()
        pltpu.make_async_remote_copy(src_ref=h_ref, dst_ref=out_ref.at[1 - cc], send_sem=send_sem, recv_sem=recv_sem,
                                     device_id=(x, y, cc), device_id_type=MESH).wait_recv()
        cp.wait_send()

    out = pl.pallas_call(
        body, name="join_halves_sibling", in_specs=[_ANY], out_specs=_ANY,
        out_shape=jax.ShapeDtypeStruct((2, r, c), h.dtype),
        scratch_shapes=[pltpu.SemaphoreType.DMA, pltpu.SemaphoreType.DMA],
    )(h)
    return lax.dynamic_update_slice(out, h[None], (lax.axis_index("c"), 0, 0))


def add_own_half(g, got, core, out_dtype):
    n, _, r, c = g.shape
    tr = _tile(r, (512, 256, 128, 64, 32, 16))

    def body(c_ref, g_r, o_r, out_r):
        out_r[...] = (g_r[...] + o_r[...]).astype(out_r.dtype)

    return pl.pallas_call(
        body, name="add_own_half",
        grid_spec=pltpu.PrefetchScalarGridSpec(
            num_scalar_prefetch=1, grid=(n, r // tr),
            in_specs=[pl.BlockSpec((None, None, tr, c), lambda i, j, c_ref: (i, c_ref[0], j, 0)),
                      pl.BlockSpec((None, tr, c), lambda i, j, c_ref: (i, j, 0))],
            out_specs=pl.BlockSpec((None, tr, c), lambda i, j, c_ref: (i, j, 0))),
        out_shape=jax.ShapeDtypeStruct((n, r, c), out_dtype),
        compiler_params=_cparams(("parallel", "parallel")),
    )(core, g, got)


def sum_slabs(p, name):
    n, r, c = p.shape
    tr = _tile(r, [t for t in (512, 256, 128, 64, 32, 16) if n * t * c * p.dtype.itemsize <= (1 << 23)])

    def body(p_r, out_r):
        acc = p_r[0].astype(F32)
        for k in range(1, n):
            acc = acc + p_r[k].astype(F32)
        out_r[...] = acc

    return pl.pallas_call(
        body, name=name, grid=(r // tr,), in_specs=[pl.BlockSpec((n, tr, c), lambda i: (0, i, 0))],
        out_specs=pl.BlockSpec((tr, c), lambda i: (i, 0)), out_shape=jax.ShapeDtypeStruct((r, c), F32),
        compiler_params=_cparams(("parallel",)),
    )(p)


def _lay(arr, axis, pieces, total, reps=()):
    items = [(d, n, lax.slice_in_dim(arr, s0, s0 + n, axis=axis)) for s0, n, d in pieces]
    items += [(d, n, jnp.repeat(lax.slice_in_dim(arr, s0, s0 + 1, axis=axis), n, axis=axis)) for s0, d, n in reps]
    items.sort(key=lambda t: t[0])
    parts, pos = [], 0

    def zeros(n):
        sh = list(arr.shape)
        sh[axis] = n
        return jnp.zeros(sh, arr.dtype)

    for d, n, v in items:
        if d > pos:
            parts.append(zeros(d - pos))
        parts.append(v)
        pos = d + n
    if total > pos:
        parts.append(zeros(total - pos))
    return jnp.concatenate(parts, axis=axis) if len(parts) > 1 else parts[0]


def _unlay_parts(g, axis, pieces, reps=()):
    out = [(s0, lax.slice_in_dim(g, d, d + n, axis=axis)) for s0, n, d in pieces]
    out += [(s0, jnp.sum(lax.slice_in_dim(g, d, d + n, axis=axis), axis=axis, keepdims=True)) for s0, d, n in reps]
    return out


def _join(parts, axis):
    parts = sorted(parts, key=lambda t: t[0])
    return jnp.concatenate([p for _, p in parts], axis=axis)


def _heads(src0, n_heads, width, padded, dst0=0):
    return [(src0 + h * width, width, dst0 + h * padded) for h in range(n_heads)]


_XQ = lambda src0: _heads(src0, XA_HEADS, XA_HD, LANE)
_XA_W = XA_HEADS * LANE

LAYOUT = {
    'a': dict(
        segs=dict(q=(_heads(0, 4, 96, LANE), 512, ()), k=(_heads(384, 4, 96, LANE), 512, ()),
                  v=(_heads(768, 4, 192, 256), 1024, ()), glr=([(1536, 16, 0)], LANE, ()),
                  og=(_heads(1552, 4, 192, 256), 1024, ()), xq=(_XQ(2320), _XA_W, ())),
        tok=(_heads(0, 4, 192, 256), 1024), xa=(_XQ(768), _XA_W)),
    'b': dict(
        segs=dict(q=([(0, 1536, 0)], 1536, ()), k=([(1536, 1536, 0)], 1536, ()), v=([(3072, 1536, 0)], 1536, ()),
                  xq=(_XQ(4608), _XA_W, ())),
        tok=([(0, 512, 0)], 512), xa=(_XQ(512), _XA_W)),
    'c': dict(
        segs=dict(z=(_heads(0, 12, 64, LANE), 1536, ()),
                  xbc=(_heads(768, 12, 64, LANE) + [(1536, 256, 1536), (1792, 256, 1792)], 2048, ()),
                  dt=([(2048, 12, 0)], LANE, ()),
                  xq=(_XQ(2060), _XA_W, ())),
        tok=(_heads(0, 12, 64, LANE), 1536), xa=(_XQ(768), _XA_W)),
    'd': dict(
        segs=dict(q=([(0, 768, 0)], 768, ()), f=([(768, 768, 0)], 768, ()), i=([(1536, 768, 0)], 768, ()),
                  og=([(2304, 768, 0)], 768, ()), xq=(_XQ(3072), _XA_W, ())),
        tok=([(0, 768, 0)], 768), xa=(_XQ(768), _XA_W)),
}
KINDS = 'abcd'
_XS_PIECES = _heads(0, 12, 64, LANE)
_XBC_PIECES = _XS_PIECES + [(768, 256, 1536), (1024, 256, 1792)]
_HEAD_REPS = tuple((h, h * LANE, LANE) for h in range(12))


def _row(v):
    return v.reshape(1, -1)


def local_step(x, mem, positions, target, W):
    s = x.shape[0]
    grads = {}
    scan_block = CHUNK * SCAN_CHUNKS

    inv_freq = ROPE_THETA ** (-jnp.arange(DIL_HD // 2, dtype=F32) / (DIL_HD // 2))
    ang = positions.astype(F32)[:, None] * inv_freq
    cosf = jnp.concatenate([jnp.cos(ang), jnp.cos(ang)], axis=-1)
    sinf = jnp.concatenate([-jnp.sin(ang), jnp.sin(ang)], axis=-1)

    mem_g = _row(W['mem_norm'])
    (mem_n,) = tmap("mem_norm", _norm_stage, [mem], [mem_g], [(D_MODEL, MXU_DTYPE)])
    kv_lay = _heads(0, 4, 64, LANE) + _heads(256, 4, 64, LANE, dst0=_XA_W)

    saved = []
    for i in range(4):
        kind = KINDS[i]
        lay = LAYOUT[kind]
        sv = dict(x0=x)
        w_in = W[f'{kind}_w_in']
        w_out = W[f'{kind}_w_out']
        sv['w_seg'] = {n: _lay(w_in, 1, p, t, r).astype(MXU_DTYPE) for n, (p, t, r) in lay['segs'].items()}
        sv['wo_tok'] = _lay(w_out, 0, *lay['tok']).astype(MXU_DTYPE)
        sv['wo_xa'] = _lay(w_out, 0, *lay['xa']).astype(MXU_DTYPE)
        sv['w_kv'] = _lay(W['xa_w_kv'][i], 1, kv_lay, 2 * _XA_W).astype(MXU_DTYPE)
        sv['g1'] = _row(W['mix_norm'][i])
        (h,) = tmap(f"mix_norm_{i}", _norm_stage, [x], [sv['g1']], [(D_MODEL, MXU_DTYPE)])
        sv['h'] = h
        seg = {n: matmul(h, w) for n, w in sv['w_seg'].items()}
        sv['seg'] = seg

        if kind == 'a':
            sv['w2'] = _lay(_lay(W['a_w_gate2'], 1, _heads(0, 4, 96, LANE), 512), 0, [(0, 16, 0)], LANE)
            sv['bg'] = _row(_lay(W['a_b_gate'], 0, _heads(0, 4, 96, LANE), 512))
            sv['on'] = _row(_lay(W['a_o_norm'], 0, [(0, 192, 0)], 256))
            (la,) = tmap("gla_pre", _gla_pre, [seg['glr']], [sv['w2'], sv['bg']], [(512, F32)])
            sv['la'] = la
            sv['scan_fn'] = _gla_step(GLA_HEADS, 2 * LANE, GLA_DK ** -0.5)
            sv['scan_rows'] = [seg['q'], seg['k'], seg['v'], la]
            (o,), sv['states'] = rscan("gla_scan", sv['scan_fn'], [(LANE, 2 * LANE)] * GLA_HEADS, sv['scan_rows'], [],
                                       [(1024, F32)], scan_block)
            sv['o'] = o
            (tok,) = tmap("gla_post", _gla_post, [o, seg['og']], [sv['on']], [(1024, MXU_DTYPE)])
        elif kind == 'b':
            sv['qg'], sv['kg'] = _row(W['b_q_norm']), _row(W['b_k_norm'])
            os_, ls_ = [], []
            qkn = tmap("dil_pre", _dil_pre, [seg['q'], seg['k'], cosf, sinf], [sv['qg'], sv['kg']], [(512, F32)] * 6)
            sv['qn'], sv['kn'] = qkn[:3], qkn[3:]
            for g, (window, r) in enumerate(DIL_GROUPS):
                assert window // r == DIL_BLOCK and (s // r) % DIL_BLOCK == 0
                o, lse = dil_attn(f"dil_attn_{g}", sv['qn'][g], sv['kn'][g], seg['v'], r, g)
                os_.append(o)
                ls_.append(lse)
            sv['os'], sv['ls'] = os_, ls_
            (tok,) = tmap("dil_merge", _dil_merge, os_ + ls_, [], [(512, MXU_DTYPE)])
        elif kind == 'c':
            sv['cw'] = _lay(W['c_conv_w'], 1, _XBC_PIECES, 2048)
            sv['cb'] = _row(_lay(W['c_conv_b'], 0, _XBC_PIECES, 2048))
            sv['dtb'] = _row(_lay(W['c_dt_bias'], 0, [], 1536, _HEAD_REPS))
            sv['alog'] = _row(_lay(W['c_a_log'], 0, [], 1536, _HEAD_REPS))
            sv['dsk'] = _row(_lay(W['c_d'], 0, [], 1536, _HEAD_REPS))
            sv['cn'] = _row(_lay(W['c_norm'], 0, _XS_PIECES, 1536))
            xact = conv_fwd("ssm_conv", seg['xbc'], sv['cw'], sv['cb'], 'silu', F32, 512)
            sv['xact'] = xact
            sv['scan_rows'] = [xact, seg['dt']]
            sv['scan_params'] = [sv['dtb'], sv['alog'], sv['dsk']]
            (yv,), sv['states'] = rscan("ssd_scan", _ssd_step, [(LANE, LANE)] * SSM_HEADS, sv['scan_rows'],
                                        sv['scan_params'], [(1536, F32)], scan_block)
            sv['y'] = yv
            (tok,) = tmap("ssd_post", _mamba_post, [yv, seg['z']], [sv['cn']], [(1536, MXU_DTYPE)])
        else:
            sv['lbnd'] = W['d_lower_bounds']
            sv['on'] = _row(W['d_o_norm'])
            qq, kk, la = tmap("hgrn_pre", _hgrn_pre, [seg['q'], seg['f']], [sv['lbnd']], [(768, F32)] * 3)
            sv['scan_fn'] = _gla_step(HGRN_HEADS, LANE, 1.0)
            sv['scan_rows'] = [qq, kk, seg['i'], la]
            (o,), sv['states'] = rscan("hgrn_scan", sv['scan_fn'], [(LANE, LANE)] * HGRN_HEADS, sv['scan_rows'], [],
                                       [(768, F32)], scan_block)
            sv['o'] = o
            (tok,) = tmap("hgrn_post", _hgrn_post, [o, seg['og']], [sv['on']], [(768, MXU_DTYPE)])
        sv['tok'] = tok

        kv = matmul(mem_n, sv['w_kv'])
        sv['kv'] = kv
        sv['xqg'] = _row(_lay(W['xa_q_norm'][i], 0, [(0, 64, 0)], LANE))
        sv['xkg'] = _row(_lay(W['xa_k_norm'][i], 0, [(0, 64, 0)], LANE))
        (xa,) = tmap(f"xattn_{i}", _xattn, [seg['xq']], [kv, sv['xqg'], sv['xkg']], [(_XA_W, MXU_DTYPE)])
        sv['xa'] = xa
        x = matmul(tok, sv['wo_tok'], add=x)
        x = matmul(xa, sv['wo_xa'], add=x)
        sv['x1'] = x

        sv['g2'] = _row(W['ffn_norm'][i])
        sv['w_up'] = W['ffn_w_up'][i].astype(MXU_DTYPE)
        sv['w_down'] = W['ffn_w_down'][i].astype(MXU_DTYPE)
        sv['fcw'] = W['ffn_conv_w'][i]
        sv['fcb'] = _row(W['ffn_conv_b'][i])
        (h2,) = tmap(f"ffn_norm_{i}", _norm_stage, [x], [sv['g2']], [(D_MODEL, MXU_DTYPE)])
        sv['h2'] = h2
        u0 = matmul(h2, sv['w_up'])
        sv['u0'] = u0
        act = conv_fwd("ffn_conv", u0, sv['fcw'], sv['fcb'], 'glu', MXU_DTYPE, 1408)
        sv['act'] = act
        x = matmul(act, sv['w_down'], add=x)
        saved.append(sv)

    loss_acc, dx = loss_head(x, target)

    g_stack = {n: [None] * 4 for n in ('mix_norm', 'xa_w_kv', 'xa_q_norm', 'xa_k_norm', 'ffn_norm', 'ffn_w_up',
                                        'ffn_conv_w', 'ffn_conv_b', 'ffn_w_down')}
    d_memn = None
    for i in reversed(range(4)):
        kind = KINDS[i]
        lay = LAYOUT[kind]
        sv = saved[i]
        seg = sv['seg']
        dact = matmul(dx, sv['w_down'], tb=True)
        g_stack['ffn_w_down'][i] = matmul(sv['act'], dx, ta=True)
        (du_g, du_v), dcw, dcb = conv_bwd("ffn_conv_bwd", sv['u0'], sv['fcw'], sv['fcb'], dact, 'glu', 1408)
        g_stack['ffn_conv_w'][i], g_stack['ffn_conv_b'][i] = dcw, dcb[0]
        dh2 = matmul(du_v, sv['w_up'][:, D_FF:], tb=True, add=matmul(du_g, sv['w_up'][:, :D_FF], tb=True))
        g_stack['ffn_w_up'][i] = jnp.concatenate([matmul(sv['h2'], du_g, ta=True), matmul(sv['h2'], du_v, ta=True)],
                                                 axis=1)
        (dx,), (dg2,) = tmap_bwd(f"ffn_norm_bwd_{i}", _norm_stage, [sv['x1']], [sv['g2']], [dh2], [True], {0: dx})
        g_stack['ffn_norm'][i] = dg2[0]
        dtok = matmul(dx, sv['wo_tok'], tb=True)
        dxa = matmul(dx, sv['wo_xa'], tb=True)
        g_wo = _unlay_parts(matmul(sv['tok'], dx, ta=True), 0, lay['tok'][0]) \
            + _unlay_parts(matmul(sv['xa'], dx, ta=True), 0, lay['xa'][0])
        grads[f'{kind}_w_out'] = _join(g_wo, 0)
        (dxq,), (dkv, dqg, dkg) = tmap_bwd(f"xattn_bwd_{i}", _xattn, [seg['xq']], [sv['kv'], sv['xqg'], sv['xkg']],
                                           [dxa], [True])
        g_stack['xa_q_norm'][i], g_stack['xa_k_norm'][i] = dqg[0, :XA_HD], dkg[0, :XA_HD]
        g_stack['xa_w_kv'][i] = _join(_unlay_parts(matmul(mem_n, dkv, ta=True), 1, kv_lay), 1)
        d_memn = matmul(dkv, sv['w_kv'], tb=True, add=d_memn)
        dseg = dict(xq=dxq)
        if kind == 'a':
            (do, dog), (don,) = tmap_bwd("gla_post_bwd", _gla_post, [sv['o'], seg['og']], [sv['on']], [dtok],
                                         [True, True], grad_dtype=F32)
            grads['a_o_norm'] = don[0, :GLA_DV]
            (dq, dk, dv, dla), _ = rscan_bwd("gla_scan_bwd", sv['scan_fn'], sv['states'], sv['scan_rows'], [], [do],
                                             scan_block, grad_dtype=F32)
            (dglr,), (dw2, dbg) = tmap_bwd("gla_pre_bwd", _gla_pre, [seg['glr']], [sv['w2'], sv['bg']], [dla], [True])
            grads['a_w_gate2'] = _join(_unlay_parts(dw2[:GLA_RANK], 1, _heads(0, 4, 96, LANE)), 1)
            grads['a_b_gate'] = _join(_unlay_parts(dbg[0], 0, _heads(0, 4, 96, LANE)), 0)
            dseg.update(q=dq, k=dk, v=dv, glr=dglr, og=dog)
        elif kind == 'b':
            res, _ = tmap_bwd("dil_merge_bwd", _dil_merge, sv['os'] + sv['ls'], [], [dtok], [True] * 6, grad_dtype=F32)
            dqn, dkn, dvs = [], [], []
            for g, (_, r) in enumerate(DIL_GROUPS):
                a_, b_, c_ = dil_attn_bwd(f"dil_attn_bwd_{g}", sv['qn'][g], sv['kn'][g], seg['v'], res[g], res[3 + g],
                                          r, g)
                dqn.append(a_)
                dkn.append(b_)
                dvs.append(c_)
            (dq, dk), (dqg, dkg) = tmap_bwd("dil_pre_bwd", _dil_pre, [seg['q'], seg['k'], cosf, sinf],
                                            [sv['qg'], sv['kg']], dqn + dkn, [True, True, False, False])
            dseg.update(q=dq, k=dk, v=jnp.concatenate(dvs, axis=1))
            grads['b_q_norm'], grads['b_k_norm'] = dqg[0], dkg[0]
        elif kind == 'c':
            (dy, dz), (dcn,) = tmap_bwd("ssd_post_bwd", _mamba_post, [sv['y'], seg['z']], [sv['cn']], [dtok],
                                        [True, True], grad_dtype=F32)
            grads['c_norm'] = _join(_unlay_parts(dcn[0], 0, _XS_PIECES), 0)
            (dxact, ddt), (ddtb, dalog, ddsk) = rscan_bwd("ssd_scan_bwd", _ssd_step, sv['states'], sv['scan_rows'],
                                                          sv['scan_params'], [dy], scan_block, grad_dtype=F32)
            for nm, gv in (('c_dt_bias', ddtb), ('c_a_log', dalog), ('c_d', ddsk)):
                grads[nm] = _join(_unlay_parts(gv[0], 0, [], _HEAD_REPS), 0)
            (dxbc,), dcw, dcb = conv_bwd("ssm_conv_bwd", seg['xbc'], sv['cw'], sv['cb'], dxact, 'silu', 512)
            grads['c_conv_w'] = _join(_unlay_parts(dcw, 1, _XBC_PIECES), 1)
            grads['c_conv_b'] = _join(_unlay_parts(dcb[0], 0, _XBC_PIECES), 0)
            dseg.update(z=dz, xbc=dxbc, dt=ddt)
        else:
            (do, dog), (don,) = tmap_bwd("hgrn_post_bwd", _hgrn_post, [sv['o'], seg['og']], [sv['on']], [dtok],
                                         [True, True], grad_dtype=F32)
            grads['d_o_norm'] = don[0]
            (dqq, dkk, di, dla), _ = rscan_bwd("hgrn_scan_bwd", sv['scan_fn'], sv['states'], sv['scan_rows'], [], [do],
                                               scan_block, grad_dtype=F32)
            (dq, df), (dlb,) = tmap_bwd("hgrn_pre_bwd", _hgrn_pre, [seg['q'], seg['f']], [sv['lbnd']], [dqq, dkk, dla],
                                        [True, True])
            grads['d_lower_bounds'] = dlb
            dseg.update(q=dq, f=df, i=di, og=dog)
        dh = None
        g_in = []
        for n, (p, t, rp) in lay['segs'].items():
            dh = matmul(dseg[n], sv['w_seg'][n], tb=True, add=dh)
            g_in += _unlay_parts(matmul(sv['h'], dseg[n], ta=True), 1, p, rp)
        grads[f'{kind}_w_in'] = _join(g_in, 1)
        (dx,), (dg1,) = tmap_bwd(f"mix_norm_bwd_{i}", _norm_stage, [sv['x0']], [sv['g1']], [dh], [True], {0: dx})
        g_stack['mix_norm'][i] = dg1[0]

    _, (dmg,) = tmap_bwd("mem_norm_bwd", _norm_stage, [mem], [mem_g], [d_memn], [False])
    grads['mem_norm'] = dmg[0]
    for n, parts in g_stack.items():
        grads[n] = jnp.stack(parts)
    return loss_acc, dx, grads


def _pack(arrs, dtype, row_multiple=PACK_ROWS):
    parts, rows = [], 0
    for a in arrs:
        f = a.reshape(-1).astype(dtype)
        unit = PACK_ROWS * PACK_COLS
        pad = (-f.shape[0]) % unit
        if pad:
            f = jnp.concatenate([f, jnp.zeros((pad,), dtype)])
        parts.append(f.reshape(-1, PACK_COLS))
        rows += parts[-1].shape[0]
    if rows % row_multiple:
        parts.append(jnp.zeros((row_multiple - rows % row_multiple, PACK_COLS), dtype))
    return jnp.concatenate(parts, axis=0)


def _unpack(buf, shapes):
    out, row = [], 0
    for sh in shapes:
        n = int(np.prod(sh))
        rows = -(-n // (PACK_ROWS * PACK_COLS)) * PACK_ROWS
        out.append(buf[row:row + rows].reshape(-1)[:n].reshape(sh))
        row += rows
    return out


def _shard_shape(full, axis):
    sh = list(full)
    sh[axis] //= N_CHIPS
    return tuple(sh)


def _split_chips(a, axis):
    sh = a.shape
    return jnp.moveaxis(a.reshape(sh[:axis] + (N_CHIPS, sh[axis] // N_CHIPS) + sh[axis + 1:]), axis, 0)


def _merge_chips(a, axis):
    a = jnp.moveaxis(a, 0, axis)
    sh = a.shape
    return a.reshape(sh[:axis] + (sh[axis] * sh[axis + 1],) + sh[axis + 2:])


def kernel(x, mem, positions, mem_norm, mix_norm, xa_w_kv, xa_q_norm, xa_k_norm, ffn_norm, ffn_w_up, ffn_conv_w, ffn_conv_b, ffn_w_down, a_w_in, a_w_gate2, a_b_gate, a_o_norm, a_w_out, b_w_in, b_q_norm, b_k_norm, b_w_out, c_w_in, c_conv_w, c_conv_b, c_dt_bias, c_a_log, c_d, c_norm, c_w_out, d_w_in, d_lower_bounds, d_o_norm, d_w_out, loss_target, m_mem_norm, m_mix_norm, m_xa_w_kv, m_xa_q_norm, m_xa_k_norm, m_ffn_norm, m_ffn_w_up, m_ffn_conv_w, m_ffn_conv_b, m_ffn_w_down, m_a_w_in, m_a_w_gate2, m_a_b_gate, m_a_o_norm, m_a_w_out, m_b_w_in, m_b_q_norm, m_b_k_norm, m_b_w_out, m_c_w_in, m_c_conv_w, m_c_conv_b, m_c_dt_bias, m_c_a_log, m_c_d, m_c_norm, m_c_w_out, m_d_w_in, m_d_lower_bounds, m_d_o_norm, m_d_w_out, v_mem_norm, v_mix_norm, v_xa_w_kv, v_xa_q_norm, v_xa_k_norm, v_ffn_norm, v_ffn_w_up, v_ffn_conv_w, v_ffn_conv_b, v_ffn_w_down, v_a_w_in, v_a_w_gate2, v_a_b_gate, v_a_o_norm, v_a_w_out, v_b_w_in, v_b_q_norm, v_b_k_norm, v_b_w_out, v_c_w_in, v_c_conv_w, v_c_conv_b, v_c_dt_bias, v_c_a_log, v_c_d, v_c_norm, v_c_w_out, v_d_w_in, v_d_lower_bounds, v_d_o_norm, v_d_w_out):
    args = locals()
    w = {n: args[n] for n in WEIGHTS}
    m = {n: args['m_' + n] for n in WEIGHTS}
    v = {n: args['v_' + n] for n in WEIGHTS}
    cx, cy, cc = lax.axis_index("x"), lax.axis_index("y"), lax.axis_index("c")
    chip = 2 * cx + cy

    big_shard_shapes = [w[n].shape for n in BIG]
    gathered = allgather_chips(_pack([w[n] for n in BIG], MXU_DTYPE))
    per_chip = [_unpack(gathered[j], big_shard_shapes) for j in range(N_CHIPS)]
    full = {n: _merge_chips(jnp.stack([per_chip[j][k] for j in range(N_CHIPS)]), SHARD_AXIS[n])
            for k, n in enumerate(BIG)}
    small_sharded = [n for n in SMALL if n in SHARD_AXIS]
    sg = allgather_devices(_pack([w[n] for n in small_sharded], F32))
    per_chip_s = [_unpack(sg[2 * j], [w[n].shape for n in small_sharded]) for j in range(N_CHIPS)]
    for k, n in enumerate(small_sharded):
        full[n] = _merge_chips(jnp.stack([per_chip_s[j][k] for j in range(N_CHIPS)]), SHARD_AXIS[n])
    for n in SMALL:
        if n not in SHARD_AXIS:
            full[n] = w[n]

    loss_acc, dx, grads = local_step(x[0], mem[0], positions[0], loss_target[0], full)
    loss = lax.psum(jnp.sum(loss_acc), ("x", "y", "c"))

    gb = jnp.stack([_pack([_split_chips(grads[n], SHARD_AXIS[n])[j] for n in BIG], F32, 1024)
                    for j in range(N_CHIPS)])
    rows = gb.shape[1]
    gb = gb.reshape(N_CHIPS, 2, rows // 2, PACK_COLS)
    got = swap_halves_sibling(gb)
    pair = add_own_half(gb, got, cc.reshape(1).astype(jnp.int32), GRAD_WIRE_DTYPE)
    half = sum_chips(exchange_chips(pair), pair, chip.reshape(1).astype(jnp.int32))
    red = join_halves_sibling(half).reshape(rows, PACK_COLS)
    g_big = dict(zip(BIG, _unpack(red, big_shard_shapes)))

    small_full_shapes = [grads[n].shape for n in SMALL]
    gs = sum_slabs(allgather_devices(_pack([grads[n] for n in SMALL], F32, 128)), "sum_devices")
    g_small = {}
    for n, gfull in zip(SMALL, _unpack(gs, small_full_shapes)):
        if n in SHARD_AXIS:
            ax = SHARD_AXIS[n]
            size = gfull.shape[ax] // N_CHIPS
            gfull = lax.dynamic_slice_in_dim(gfull, chip * size, size, axis=ax)
        g_small[n] = gfull

    g_out, delta, new_m, new_v = {**g_big, **g_small}, {}, {}, {}
    for n in WEIGHTS:
        sh = w[n].shape
        two_d = (-1, sh[-1])
        d_, m_, v_ = adamw(f"adamw_{n}", w[n].reshape(two_d), g_out[n].reshape(two_d), m[n].reshape(two_d),
                           v[n].reshape(two_d))
        delta[n], new_m[n], new_v[n] = d_.reshape(sh), m_.reshape(sh), v_.reshape(sh)

    return (loss, dx[None], *[g_out[n] for n in WEIGHTS], *[delta[n] for n in WEIGHTS],
            *[new_m[n] for n in WEIGHTS], *[new_v[n] for n in WEIGHTS])
```

```python
import functools
import math

import jax
import jax.numpy as jnp
import numpy as np
from jax import lax
from jax.experimental import pallas as pl
from jax.experimental.pallas import tpu as pltpu

F32 = jnp.float32
MXU_DTYPE = jnp.bfloat16
GRAD_WIRE_DTYPE = jnp.bfloat16
VMEM_LIMIT_V7X = 56 * 1024 * 1024
LANE = 128
SUBLANE = 8

D_MODEL = 1024
N_MEM = 256
EPS = 1e-6
ROPE_THETA = 10000.0
CHUNK = 64
XA_HEADS, XA_HD = 4, 64
GLA_HEADS, GLA_DK, GLA_DV, GLA_RANK, GLA_GATE_NORM = 4, 96, 192, 16, 16.0
DIL_GROUPS = ((128, 1), (512, 4), (2048, 16))
DIL_HEADS, DIL_HD, DIL_BLOCK = 4, 128, 128
SSM_HD, SSM_HEADS, SSM_GROUPS, SSM_STATE, SSM_CONV = 64, 12, 2, 128, 4
HGRN_HEADS, HGRN_DK = 6, 128
D_FF = 2816
FFN_CONV = 3
ADAM_LR, ADAM_B1, ADAM_B2, ADAM_EPS, ADAM_WD, ADAM_STEP = 0.001, 0.9, 0.999, 1e-08, 0.01, 10

MM_TILES = (1408, 1024, 768, 512, 384, 256, 128)
MM_VMEM_BUDGET = 40 * 1024 * 1024
ROW_BLOCK = 256
SCAN_CHUNKS = 2
PACK_COLS = 1024
PACK_ROWS = 32

WEIGHTS = ['mem_norm', 'mix_norm', 'xa_w_kv', 'xa_q_norm', 'xa_k_norm', 'ffn_norm', 'ffn_w_up', 'ffn_conv_w',
           'ffn_conv_b', 'ffn_w_down', 'a_w_in', 'a_w_gate2', 'a_b_gate', 'a_o_norm', 'a_w_out', 'b_w_in', 'b_q_norm',
           'b_k_norm', 'b_w_out', 'c_w_in', 'c_conv_w', 'c_conv_b', 'c_dt_bias', 'c_a_log', 'c_d', 'c_norm', 'c_w_out',
           'd_w_in', 'd_lower_bounds', 'd_o_norm', 'd_w_out']
SHARD_AXIS = {'xa_w_kv': 1, 'ffn_w_up': 2, 'ffn_conv_w': 2, 'ffn_w_down': 1, 'a_w_in': 1, 'a_w_gate2': 1, 'a_w_out': 0,
              'b_w_in': 1, 'b_w_out': 1, 'c_w_in': 1, 'c_conv_w': 1, 'c_w_out': 0, 'd_w_in': 1, 'd_w_out': 0}
BIG = ['xa_w_kv', 'ffn_w_up', 'ffn_w_down', 'a_w_in', 'a_w_gate2', 'a_w_out', 'b_w_in', 'b_w_out', 'c_w_in', 'c_w_out',
       'd_w_in', 'd_w_out']
SMALL = [n for n in WEIGHTS if n not in BIG]
LAYERED = ['ffn_w_up', 'ffn_w_down']
N_CHIPS = 4
N_DEV = 8


class _MatmulSet:
    def __init__(self, cast, precision):
        def dot(a, b, dims):
            if cast:
                a = a.astype(MXU_DTYPE)
                b = b.astype(MXU_DTYPE)
            return lax.dot_general(a, b, (dims, ((), ())), precision=precision, preferred_element_type=F32)

        @jax.custom_vjp
        def nn(a, b):
            return dot(a, b, ((1,), (0,)))

        @jax.custom_vjp
        def nt(a, b):
            return dot(a, b, ((1,), (1,)))

        @jax.custom_vjp
        def tn(a, b):
            return dot(a, b, ((0,), (0,)))

        nn.defvjp(lambda a, b: (nn(a, b), (a, b)), lambda r, g: (nt(g, r[1]), tn(r[0], g)))
        nt.defvjp(lambda a, b: (nt(a, b), (a, b)), lambda r, g: (nn(g, r[1]), tn(g, r[0])))
        tn.defvjp(lambda a, b: (tn(a, b), (a, b)), lambda r, g: (nt(r[1], g), nn(r[0], g)))
        self.nn, self.nt, self.tn = nn, nt, tn


mm = _MatmulSet(True, None)
hi = _MatmulSet(False, lax.Precision.HIGHEST)


def _sigmoid(x):
    return jax.nn.sigmoid(x)


def _silu(x):
    return x * jax.nn.sigmoid(x)


def _softplus(x):
    return jnp.maximum(x, 0.0) + jnp.log1p(jnp.exp(-jnp.abs(x)))


def _rms(x, g, n_real=None):
    n = n_real or x.shape[-1]
    ms = jnp.sum(x * x, axis=-1, keepdims=True) * (1.0 / n)
    return x * lax.rsqrt(ms + EPS) * g


@jax.custom_vjp
def _swap_halves(x):
    return pltpu.roll(x, 64, 1)


_swap_halves.defvjp(lambda x: (_swap_halves(x), None), lambda _, g: (_swap_halves(g),))


def _tile(n, cands):
    for c in cands:
        if n % c == 0:
            return c
    raise ValueError(f"no tile for {n} among {cands}")


def _cparams(sem):
    return pltpu.CompilerParams(dimension_semantics=sem, vmem_limit_bytes=VMEM_LIMIT_V7X)


def _f32(v):
    return v.astype(F32) if jnp.issubdtype(v.dtype, jnp.floating) else v


def matmul(a, b, *, ta=False, tb=False, add=None, out_dtype=F32, b_layer=None, b_koff=0, into=None):
    m, k = (a.shape[1], a.shape[0]) if ta else a.shape
    b2 = b.shape[1:] if b_layer is not None else b.shape
    n = b2[0] if tb else b2[1]
    assert b_koff + k <= (b2[1] if tb else b2[0]), (a.shape, b.shape, ta, tb, b_koff)
    tk = _tile(math.gcd(k, b_koff) if b_koff else k, MM_TILES)
    nk = k // tk
    sa, sb, so = a.dtype.itemsize, b.dtype.itemsize, jnp.dtype(out_dtype).itemsize

    def vmem(tm_, tn_):
        return (2 * tm_ * tk * sa + 2 * tk * tn_ * sb + 2 * tm_ * tn_ * so + (tm_ * tn_ * 4 if nk > 1 else 0)
                + (2 * tm_ * tn_ * add.dtype.itemsize if add is not None else 0))

    n_align = math.gcd(n, into[2]) if into is not None and into[2] else n
    fits = [(tm_ * tn_, tm_, tn_) for tm_ in MM_TILES if m % tm_ == 0 for tn_ in MM_TILES
            if n % tn_ == 0 and n_align % tn_ == 0 and vmem(tm_, tn_) <= MM_VMEM_BUDGET]
    _, tm, tn = max(fits)
    dims = (((0,) if ta else (1,)), ((1,) if tb else (0,)))
    n_extra = (add is not None) + (into is not None)

    def body(*refs):
        a_ref, b_ref = refs[0], refs[1]
        add_ref = refs[2] if add is not None else None
        o_ref = refs[2 + n_extra]
        part = lax.dot_general(a_ref[...].astype(MXU_DTYPE), b_ref[...].astype(MXU_DTYPE), (dims, ((), ())),
                               preferred_element_type=F32)

        def finish(r):
            if add_ref is not None:
                r = r + add_ref[...].astype(F32)
            o_ref[...] = r.astype(o_ref.dtype)

        if nk == 1:
            finish(part)
            return
        acc = refs[-1]
        kk = pl.program_id(2)

        @pl.when(kk == 0)
        def _():
            acc[...] = part

        @pl.when(kk > 0)
        def _():
            acc[...] += part

        @pl.when(kk == nk - 1)
        def _():
            finish(acc[...])

    a_spec = pl.BlockSpec((tk, tm), lambda i, j, q: (q, i)) if ta else pl.BlockSpec((tm, tk), lambda i, j, q: (i, q))
    ko = b_koff // tk
    if b_layer is None:
        b_spec = (pl.BlockSpec((tn, tk), lambda i, j, q: (j, q + ko)) if tb
                  else pl.BlockSpec((tk, tn), lambda i, j, q: (q + ko, j)))
    else:
        b_spec = (pl.BlockSpec((None, tn, tk), lambda i, j, q: (b_layer, j, q + ko)) if tb
                  else pl.BlockSpec((None, tk, tn), lambda i, j, q: (b_layer, q + ko, j)))
    o_spec = pl.BlockSpec((tm, tn), lambda i, j, q: (i, j))
    ins, specs = [a, b], [a_spec, b_spec]
    if add is not None:
        ins.append(add)
        specs.append(o_spec)
    aliases = {}
    out_shape = jax.ShapeDtypeStruct((m, n), out_dtype)
    if into is not None:
        buf, layer, col0 = into
        assert buf.shape[1] == m and buf.dtype == out_dtype
        co = col0 // tn
        ins.append(buf)
        specs.append(_ANY)
        aliases = {len(ins) - 1: 0}
        o_spec = pl.BlockSpec((None, tm, tn), lambda i, j, q: (layer, i, j + co))
        out_shape = jax.ShapeDtypeStruct(buf.shape, buf.dtype)
    return pl.pallas_call(
        body, name=f"mm_{m}x{k}x{n}_{int(ta)}{int(tb)}{int(add is not None)}{int(b_layer is not None)}{int(into is not None)}",
        grid=(m // tm, n // tn, nk), in_specs=specs, out_specs=o_spec, out_shape=out_shape,
        input_output_aliases=aliases,
        scratch_shapes=[pltpu.VMEM((tm, tn), F32)] if nk > 1 else [],
        compiler_params=_cparams(("parallel", "parallel", "arbitrary")),
    )(*ins)


def _row_spec(a, block):
    return pl.BlockSpec((block, a.shape[1]), lambda i: (i, 0))


def _whole_spec(a):
    return pl.BlockSpec(a.shape, lambda i: (0,) * a.ndim)


def tmap(name, fn, rows, params, outs, block=ROW_BLOCK):
    s = rows[0].shape[0]
    block = min(block, s)
    nr, npar = len(rows), len(params)

    def body(*refs):
        res = fn(*[_f32(r[...]) for r in refs[:nr]], *[_f32(p[...]) for p in refs[nr:nr + npar]])
        for o_ref, v in zip(refs[nr + npar:], res, strict=True):
            o_ref[...] = v.astype(o_ref.dtype)

    return pl.pallas_call(
        body, name=name, grid=(s // block,),
        in_specs=[_row_spec(a, block) for a in rows] + [_whole_spec(p) for p in params],
        out_specs=[pl.BlockSpec((block, w), lambda i: (i, 0)) for w, _ in outs],
        out_shape=[jax.ShapeDtypeStruct((s, w), dt) for w, dt in outs],
        compiler_params=_cparams(("parallel",)),
    )(*rows, *params)


def tmap_bwd(name, fn, rows, params, douts, row_grad, row_add=None, grad_dtype=None, block=ROW_BLOCK):
    s = rows[0].shape[0]
    block = min(block, s)
    grad_dtype = grad_dtype or MXU_DTYPE
    nr, npar, nd = len(rows), len(params), len(douts)
    gr = [i for i in range(nr) if row_grad[i]]
    row_add = row_add or {}
    adds = [row_add[i] for i in gr if i in row_add]

    def body(*refs):
        rv = [_f32(r[...]) for r in refs[:nr]]
        pv = [_f32(p[...]) for p in refs[nr:nr + npar]]
        dv = tuple(_f32(d[...]) for d in refs[nr + npar:nr + npar + nd])
        add_refs = list(refs[nr + npar + nd:nr + npar + nd + len(adds)])
        out_refs = refs[nr + npar + nd + len(adds):]

        def f(*diff):
            rr = list(rv)
            for n_, i_ in enumerate(gr):
                rr[i_] = diff[n_]
            return tuple(fn(*rr, *diff[len(gr):]))

        _, vjp = jax.vjp(f, *[rv[i_] for i_ in gr], *pv)
        g = vjp(dv)
        for n_, i_ in enumerate(gr):
            v = g[n_]
            if i_ in row_add:
                v = v + add_refs.pop(0)[...].astype(F32)
            out_refs[n_][...] = v.astype(out_refs[n_].dtype)
        first = pl.program_id(0) == 0
        for n_ in range(npar):
            ref = out_refs[len(gr) + n_]

            @pl.when(first)
            def _(ref=ref):
                ref[...] = jnp.zeros_like(ref)

            ref[...] += g[len(gr) + n_]

    res = pl.pallas_call(
        body, name=name, grid=(s // block,),
        in_specs=[_row_spec(a, block) for a in rows] + [_whole_spec(p) for p in params]
        + [_row_spec(d, block) for d in douts] + [_row_spec(a, block) for a in adds],
        out_specs=[_row_spec(rows[i], block) for i in gr] + [_whole_spec(p) for p in params],
        out_shape=[jax.ShapeDtypeStruct(rows[i].shape, F32 if i in row_add else grad_dtype) for i in gr]
        + [jax.ShapeDtypeStruct(p.shape, F32) for p in params],
        compiler_params=_cparams(("arbitrary",)),
    )(*rows, *params, *douts, *adds)
    return list(res[:len(gr)]), list(res[len(gr):])


def rscan(name, fn, state_shapes, rows, params, outs, block):
    s = rows[0].shape[0]
    nsteps = s // block
    nr, npar, no, ns = len(rows), len(params), len(outs), len(state_shapes)

    def body(*refs):
        out_refs = refs[nr + npar:nr + npar + no]
        sav_refs = refs[nr + npar + no:nr + npar + no + ns]
        st_refs = refs[nr + npar + no + ns:]

        @pl.when(pl.program_id(0) == 0)
        def _():
            for st in st_refs:
                st[...] = jnp.zeros_like(st)

        sts = tuple(st[...] for st in st_refs)
        for sv, v in zip(sav_refs, sts):
            sv[...] = v
        new, res = fn(sts, *[_f32(r[...]) for r in refs[:nr]], *[_f32(p[...]) for p in refs[nr:nr + npar]])
        for st, v in zip(st_refs, new, strict=True):
            st[...] = v
        for o_ref, v in zip(out_refs, res, strict=True):
            o_ref[...] = v.astype(o_ref.dtype)

    res = pl.pallas_call(
        body, name=name, grid=(nsteps,),
        in_specs=[_row_spec(a, block) for a in rows] + [_whole_spec(p) for p in params],
        out_specs=[pl.BlockSpec((block, w), lambda i: (i, 0)) for w, _ in outs]
        + [pl.BlockSpec(sh, lambda i: (i, 0)) for sh in state_shapes],
        out_shape=[jax.ShapeDtypeStruct((s, w), dt) for w, dt in outs]
        + [jax.ShapeDtypeStruct((nsteps * sh[0], sh[1]), F32) for sh in state_shapes],
        scratch_shapes=[pltpu.VMEM(sh, F32) for sh in state_shapes],
        compiler_params=_cparams(("arbitrary",)),
    )(*rows, *params)
    return list(res[:no]), list(res[no:])


def rscan_bwd(name, fn, saved, rows, params, douts, block, grad_dtype=None):
    s = rows[0].shape[0]
    nsteps = s // block
    grad_dtype = grad_dtype or MXU_DTYPE
    nr, npar, nd, ns = len(rows), len(params), len(douts), len(saved)
    state_shapes = [(sv.shape[0] // nsteps, sv.shape[1]) for sv in saved]

    def body(*refs):
        rv = [_f32(r[...]) for r in refs[:nr]]
        pv = [_f32(p[...]) for p in refs[nr:nr + npar]]
        dv = tuple(_f32(d[...]) for d in refs[nr + npar:nr + npar + nd])
        sv = tuple(x[...] for x in refs[nr + npar + nd:nr + npar + nd + ns])
        out_refs = refs[nr + npar + nd + ns:nr + npar + nd + ns + nr + npar]
        dst_refs = refs[nr + npar + nd + ns + nr + npar:]
        first = pl.program_id(0) == 0

        @pl.when(first)
        def _():
            for d in dst_refs:
                d[...] = jnp.zeros_like(d)

        def f(sts, *args):
            return fn(sts, *args)

        _, vjp = jax.vjp(f, sv, *rv, *pv)
        g = vjp((tuple(d[...] for d in dst_refs), dv))
        for d, v in zip(dst_refs, g[0], strict=True):
            d[...] = v
        for n_ in range(nr):
            out_refs[n_][...] = g[1 + n_].astype(out_refs[n_].dtype)
        for n_ in range(npar):
            ref = out_refs[nr + n_]

            @pl.when(first)
            def _(ref=ref):
                ref[...] = jnp.zeros_like(ref)

            ref[...] += g[1 + nr + n_]

    rev = lambda i: (nsteps - 1 - i, 0)
    res = pl.pallas_call(
        body, name=name, grid=(nsteps,),
        in_specs=[pl.BlockSpec((block, a.shape[1]), rev) for a in rows] + [_whole_spec(p) for p in params]
        + [pl.BlockSpec((block, d.shape[1]), rev) for d in douts] + [pl.BlockSpec(sh, rev) for sh in state_shapes],
        out_specs=[pl.BlockSpec((block, a.shape[1]), rev) for a in rows] + [_whole_spec(p) for p in params],
        out_shape=[jax.ShapeDtypeStruct(a.shape, grad_dtype) for a in rows]
        + [jax.ShapeDtypeStruct(p.shape, F32) for p in params],
        scratch_shapes=[pltpu.VMEM(sh, F32) for sh in state_shapes],
        compiler_params=_cparams(("arbitrary",)),
    )(*rows, *params, *douts, *saved)
    return list(res[:nr]), list(res[nr:])


def _norm_stage(x, g):
    return (_rms(x, g),)


def _tril():
    r = lax.broadcasted_iota(jnp.int32, (CHUNK, CHUNK), 0)
    c = lax.broadcasted_iota(jnp.int32, (CHUNK, CHUNK), 1)
    return r >= c


def _gla_chunk(st, q, k, v, la, b):
    tril = _tril()
    rowi = lax.broadcasted_iota(jnp.int32, (CHUNK, 1), 0)
    b_last = jnp.sum(la, axis=0, keepdims=True)
    b_ref = jnp.sum(jnp.where(rowi < CHUNK // 2, la, 0.0), axis=0, keepdims=True)
    att = mm.nt(q * jnp.exp(b - b_ref), k * jnp.exp(b_ref - b))
    att = jnp.where(tril, att, 0.0)
    o = mm.nn(att, v) + mm.nn(q * jnp.exp(b), st)
    decay = jnp.exp(jnp.broadcast_to(b_last, (LANE, LANE)).T)
    decay = jnp.concatenate([decay] * (v.shape[1] // LANE), axis=1)
    st2 = decay * st + mm.tn(k * jnp.exp(b_last - b), v)
    return st2, o


def _gla_step(heads, vp, scale):
    kp = LANE

    def fn(states, q, k, v, la):
        sts = list(states)
        trif = _tril().astype(F32)
        rows = []
        for c in range(q.shape[0] // CHUNK):
            r = slice(c * CHUNK, (c + 1) * CHUNK)
            b_all = hi.nn(trif, la[r])
            oh = []
            for h in range(heads):
                ks, vs = slice(h * kp, (h + 1) * kp), slice(h * vp, (h + 1) * vp)
                qh = q[r, ks] * scale if scale != 1.0 else q[r, ks]
                sts[h], o = _gla_chunk(sts[h], qh, k[r, ks], v[r, vs], la[r, ks], b_all[:, ks])
                oh.append(o)
            rows.append(jnp.concatenate(oh, axis=1))
        return tuple(sts), (jnp.concatenate(rows, axis=0),)

    return fn


def _ssd_step(states, xa, dtr, dtb, alog, dsk):
    sts = list(states)
    trif = _tril().astype(F32)
    wide = lax.broadcasted_iota(jnp.int32, (CHUNK, LANE), 0) >= lax.broadcasted_iota(jnp.int32, (CHUNK, LANE), 1)
    hg = SSM_HEADS // SSM_GROUPS
    xw = SSM_HEADS * LANE
    lane, head = lax.broadcasted_iota(jnp.int32, (LANE, xw), 1), lax.broadcasted_iota(jnp.int32, (LANE, xw), 0)
    spread = ((lane >= head * LANE) & (lane < (head + 1) * LANE)).astype(F32)
    neg_a = -jnp.exp(alog)
    pad = jnp.zeros((CHUNK, LANE), F32)
    rows = []
    for c in range(xa.shape[0] // CHUNK):
        r = slice(c * CHUNK, (c + 1) * CHUNK)
        dt_all = _softplus(hi.nn(dtr[r], spread) + dtb)
        a_all = dt_all * neg_a
        acs_all = hi.nn(trif, a_all)
        last_all = jnp.sum(a_all, axis=0, keepdims=True)
        yh = []
        for g in range(SSM_GROUPS):
            bm = xa[r, xw + g * LANE:xw + (g + 1) * LANE]
            cm = xa[r, xw + (SSM_GROUPS + g) * LANE:xw + (SSM_GROUPS + g + 1) * LANE]
            cb = mm.nt(cm, jnp.concatenate([bm, pad], axis=0))
            for hh in range(hg):
                h = g * hg + hh
                ls = slice(h * LANE, (h + 1) * LANE)
                xs, acs, acs_last = xa[r, ls], acs_all[:, ls], last_all[:, ls]
                xdt = xs * dt_all[:, ls]
                seg = acs - jnp.concatenate([acs, pad], axis=0).T[:CHUNK]
                lmat = jnp.exp(jnp.where(wide, seg, -1e30))
                y = (mm.nn(cb * lmat, jnp.concatenate([xdt, pad], axis=0)) + mm.nn(cm, sts[h]) * jnp.exp(acs)
                     + dsk[:, ls] * xs)
                sts[h] = jnp.exp(acs_last) * sts[h] + mm.tn(bm, xdt * jnp.exp(acs_last - acs))
                yh.append(y)
        rows.append(jnp.concatenate(yh, axis=1))
    return tuple(sts), (jnp.concatenate(rows, axis=0),)


def _gla_pre(glr, w2, bg):
    z = mm.nn(glr, w2) + bg
    return (-_softplus(-z) * (1.0 / GLA_GATE_NORM),)


def _gla_post(o, og, g):
    w = 2 * LANE
    return (jnp.concatenate([_rms(o[:, h * w:(h + 1) * w], g, GLA_DV) * _silu(og[:, h * w:(h + 1) * w])
                             for h in range(GLA_HEADS)], axis=1),)


def _hgrn_pre(q, f, lbnd):
    e = jnp.exp(lbnd - jnp.max(lbnd, axis=0, keepdims=True))
    rowi = lax.broadcasted_iota(jnp.int32, e.shape, 0)
    lb = jnp.sum(jnp.where(rowi >= 1, e, 0.0), axis=0, keepdims=True) / jnp.sum(e, axis=0, keepdims=True)
    fg = lb + (1.0 - lb) * _sigmoid(f)
    return _silu(q), 1.0 - fg, jnp.log(fg)


def _hgrn_post(o, og, g):
    return (jnp.concatenate([_rms(o[:, h * LANE:(h + 1) * LANE], g) for h in range(HGRN_HEADS)], axis=1)
            * _sigmoid(og),)


def _mamba_post(y, z, g):
    v = y * _silu(z)
    w = (SSM_HEADS // SSM_GROUPS) * LANE
    n_real = (SSM_HEADS // SSM_GROUPS) * SSM_HD
    return (jnp.concatenate([_rms(v[:, i * w:(i + 1) * w], g[:, i * w:(i + 1) * w], n_real)
                             for i in range(SSM_GROUPS)], axis=1),)


def _dil_pre(q, k, cosf, sinf, qg, kg):
    def groups(x, g):
        out = []
        for grp in range(len(DIL_GROUPS)):
            hs = []
            for h in range(grp * DIL_HEADS, (grp + 1) * DIL_HEADS):
                n = _rms(x[:, h * LANE:(h + 1) * LANE], g)
                hs.append(n * cosf + _swap_halves(n) * sinf)
            out.append(jnp.concatenate(hs, axis=1))
        return out

    return (*groups(q, qg), *groups(k, kg))


def _dil_merge(o0, o1, o2, l0, l1, l2):
    m = jnp.maximum(jnp.maximum(l0, l1), l2)
    e0, e1, e2 = jnp.exp(l0 - m), jnp.exp(l1 - m), jnp.exp(l2 - m)
    return ((e0 * o0 + e1 * o1 + e2 * o2) / (e0 + e1 + e2),)


def _dil_block(q, kp, kc, vp, vc, lim):
    kk = jnp.concatenate([kp, kc], axis=0)
    vv = jnp.concatenate([vp, vc], axis=0)
    s = mm.nt(q, kk) * (DIL_HD ** -0.5)
    i = lax.broadcasted_iota(jnp.int32, s.shape, 0)
    j = lax.broadcasted_iota(jnp.int32, s.shape, 1)
    dist = DIL_BLOCK + i - j
    s = jnp.where((dist >= 0) & (dist <= DIL_BLOCK) & (j >= lim), s, -1e30)
    m = jnp.max(s, axis=-1, keepdims=True)
    p = jnp.exp(s - m)
    l = jnp.sum(p, axis=-1, keepdims=True)
    return mm.nn(p / l, vv), jnp.broadcast_to(m + jnp.log(l), (q.shape[0], LANE))


def _xattn(xq, kv, qg, kg):
    w = XA_HEADS * LANE
    os_ = []
    for h in range(XA_HEADS):
        ls = slice(h * LANE, (h + 1) * LANE)
        q = _rms(xq[:, ls], qg, XA_HD)
        k = _rms(kv[:, ls], kg, XA_HD)
        s = mm.nt(q, k) * (XA_HD ** -0.5)
        p = jnp.exp(s - jnp.max(s, axis=-1, keepdims=True))
        p = p / jnp.sum(p, axis=-1, keepdims=True)
        os_.append(mm.nn(p, kv[:, w + h * LANE:w + (h + 1) * LANE]))
    return (jnp.concatenate(os_, axis=1),)


def dil_attn(name, q, k, v, r, g):
    s, w = q.shape
    l = s // r
    nb = l // DIL_BLOCK
    ng = v.shape[1] // w
    q2, k2, v2 = (t.reshape(l, r * t.shape[1]) for t in (q, k, v))

    def body(q_r, kp_r, kc_r, vp_r, vc_r, o_r, l_r):
        lim = jnp.where(pl.program_id(1) == 0, DIL_BLOCK, 0)
        for h in range(DIL_HEADS):
            ls = slice(h * LANE, (h + 1) * LANE)
            o, lse = _dil_block(q_r[:, ls], kp_r[:, ls], kc_r[:, ls], vp_r[:, ls], vc_r[:, ls], lim)
            o_r[:, ls] = o
            l_r[:, ls] = lse

    cur = pl.BlockSpec((DIL_BLOCK, w), lambda res, n: (n, res))
    prev = pl.BlockSpec((DIL_BLOCK, w), lambda res, n: (jnp.maximum(n - 1, 0), res))
    vcur = pl.BlockSpec((DIL_BLOCK, w), lambda res, n: (n, res * ng + g))
    vprev = pl.BlockSpec((DIL_BLOCK, w), lambda res, n: (jnp.maximum(n - 1, 0), res * ng + g))
    o, lse = pl.pallas_call(
        body, name=name, grid=(r, nb), in_specs=[cur, prev, cur, vprev, vcur], out_specs=[cur, cur],
        out_shape=[jax.ShapeDtypeStruct((l, r * w), F32)] * 2,
        compiler_params=_cparams(("parallel", "parallel")),
    )(q2, k2, k2, v2, v2)
    return o.reshape(s, w), lse.reshape(s, w)


def dil_attn_bwd(name, q, k, v, do, dlse, r, g):
    s, w = q.shape
    l = s // r
    nb = l // DIL_BLOCK
    ng = v.shape[1] // w
    q2, k2, v2, do2, dl2 = (t.reshape(l, r * t.shape[1]) for t in (q, k, v, do, dlse))

    def body(q_r, kp_r, kc_r, vp_r, vc_r, do_r, dl_r, dq_r, dk_r, dv_r, ck, cv):
        i = pl.program_id(1)
        lim = jnp.where(i == nb - 1, DIL_BLOCK, 0)

        @pl.when(i == 0)
        def _():
            ck[...] = jnp.zeros_like(ck)
            cv[...] = jnp.zeros_like(cv)

        for h in range(DIL_HEADS):
            ls = slice(h * LANE, (h + 1) * LANE)
            _, vjp = jax.vjp(functools.partial(_dil_block, lim=lim),
                             q_r[:, ls], kp_r[:, ls], kc_r[:, ls], vp_r[:, ls], vc_r[:, ls])
            gq, gkp, gkc, gvp, gvc = vjp((do_r[:, ls], dl_r[:, ls]))
            dq_r[:, ls] = gq.astype(dq_r.dtype)
            dk_r[:, ls] = (gkc + ck[:, ls]).astype(dk_r.dtype)
            dv_r[:, ls] = (gvc + cv[:, ls]).astype(dv_r.dtype)
            ck[:, ls] = gkp
            cv[:, ls] = gvp

    cur = pl.BlockSpec((DIL_BLOCK, w), lambda res, i: (nb - 1 - i, res))
    prev = pl.BlockSpec((DIL_BLOCK, w), lambda res, i: (jnp.maximum(nb - 2 - i, 0), res))
    vcur = pl.BlockSpec((DIL_BLOCK, w), lambda res, i: (nb - 1 - i, res * ng + g))
    vprev = pl.BlockSpec((DIL_BLOCK, w), lambda res, i: (jnp.maximum(nb - 2 - i, 0), res * ng + g))
    dq, dk, dv = pl.pallas_call(
        body, name=name, grid=(r, nb), in_specs=[cur, prev, cur, vprev, vcur, cur, cur], out_specs=[cur, cur, cur],
        out_shape=[jax.ShapeDtypeStruct((l, r * w), F32), jax.ShapeDtypeStruct((l, r * w), F32),
                   jax.ShapeDtypeStruct((l, r * w), MXU_DTYPE)],
        scratch_shapes=[pltpu.VMEM((DIL_BLOCK, w), F32)] * 2,
        compiler_params=_cparams(("parallel", "arbitrary")),
    )(q2, k2, k2, v2, v2, do2, dl2)
    return dq.reshape(s, w), dk.reshape(s, w), dv.reshape(s, w)


def _dsilu(u):
    sg = _sigmoid(u)
    return sg * (1.0 + u * (1.0 - sg))


CONV_STRIP = 16


def _shifted_rows(prev8, cur_r, next8, lanes, s0, n, sh, block):
    if s0 - sh < 0:
        assert s0 == 0
        xp = jnp.concatenate([prev8, cur_r[0:n, lanes]], axis=0)
        return pltpu.roll(xp, sh, 0)[SUBLANE:SUBLANE + n]
    if s0 - sh + n > block:
        assert s0 == block and n == SUBLANE
        xp = jnp.concatenate([cur_r[block - SUBLANE:block, lanes], next8], axis=0)
        return (pltpu.roll(xp, sh, 0) if sh else xp)[SUBLANE:]
    return cur_r[pl.ds(s0 - sh, n), lanes]


def conv_fwd(name, x, w, b, mode, out_dtype, tc, block=ROW_BLOCK):
    s, c = x.shape
    ntap = w.shape[0]
    block = min(block, s)
    f = c // 2 if mode == 'glu' else c
    nh = 2 if mode == 'glu' else 1
    off = f // tc

    def body(*refs):
        first = pl.program_id(1) == 0
        o_ref = refs[-1]

        def column(cidx, carry):
            lanes = pl.ds(pl.multiple_of(cidx * LANE, LANE), LANE)
            prevs = [jnp.where(first, 0.0, refs[4 * hlf][:, lanes]) for hlf in range(nh)]
            for s0 in range(0, block, CONV_STRIP):
                us = []
                for hlf in range(nh):
                    _, cur_r, w_r, b_r = refs[4 * hlf:4 * hlf + 4]
                    acc = b_r[:, lanes]
                    for j in range(ntap):
                        xs = _shifted_rows(prevs[hlf], cur_r, None, lanes, s0, CONV_STRIP, ntap - 1 - j, block)
                        acc = acc + w_r[j:j + 1, lanes] * xs
                    us.append(acc)
                res = _silu(us[0]) * us[1] if mode == 'glu' else _silu(us[0])
                o_ref[pl.ds(s0, CONV_STRIP), lanes] = res.astype(o_ref.dtype)
            return carry

        lax.fori_loop(0, tc // LANE, column, 0)

    rb = block // SUBLANE
    ins, specs = [], []
    for hlf in range(nh):
        o = hlf * off
        ins += [x, x, w, b]
        specs += [pl.BlockSpec((SUBLANE, tc), lambda j, i, o=o: (jnp.maximum(i * rb - 1, 0), j + o)),
                  pl.BlockSpec((block, tc), lambda j, i, o=o: (i, j + o)),
                  pl.BlockSpec((ntap, tc), lambda j, i, o=o: (0, j + o)),
                  pl.BlockSpec((1, tc), lambda j, i, o=o: (0, j + o))]
    return pl.pallas_call(
        body, name=name, grid=(f // tc, s // block), in_specs=specs,
        out_specs=pl.BlockSpec((block, tc), lambda j, i: (i, j)),
        out_shape=jax.ShapeDtypeStruct((s, f), out_dtype),
        compiler_params=_cparams(("parallel", "parallel")),
    )(*ins)


def conv_bwd(name, x, w, b, dout, mode, tc, block=ROW_BLOCK):
    s, c = x.shape
    ntap = w.shape[0]
    block = min(block, s)
    nblk = s // block
    f = c // 2 if mode == 'glu' else c
    nh = 2 if mode == 'glu' else 1
    off = f // tc
    ext = block + SUBLANE

    def body(*refs):
        i = pl.program_id(1)
        first, last = i == 0, i == nblk - 1
        dcur_r, dnext_r = refs[5 * nh], refs[5 * nh + 1]
        outs = refs[5 * nh + 2:5 * nh + 2 + 3 * nh]
        du_scr = refs[5 * nh + 2 + 3 * nh:]

        @pl.when(first)
        def _():
            for hlf in range(nh):
                outs[3 * hlf + 1][...] = jnp.zeros_like(outs[3 * hlf + 1])
                outs[3 * hlf + 2][...] = jnp.zeros_like(outs[3 * hlf + 2])

        def column(cidx, carry):
            lanes = pl.ds(pl.multiple_of(cidx * LANE, LANE), LANE)
            prevs = [jnp.where(first, 0.0, refs[5 * hlf][:, lanes]) for hlf in range(nh)]
            nexts = [jnp.where(last, 0.0, refs[5 * hlf + 2][:, lanes]) for hlf in range(nh)]
            db_acc = [jnp.zeros((CONV_STRIP, LANE), F32) for _ in range(nh)]
            dw_acc = [[jnp.zeros((CONV_STRIP, LANE), F32) for _ in range(ntap)] for _ in range(nh)]
            for s0 in range(0, ext, CONV_STRIP):
                n = min(CONV_STRIP, ext - s0)
                d_e = dcur_r[pl.ds(s0, n), lanes] if s0 < block else jnp.where(last, 0.0, dnext_r[:, lanes])
                xs, us = [], []
                for hlf in range(nh):
                    cur_r, w_r, b_r = refs[5 * hlf + 1], refs[5 * hlf + 3], refs[5 * hlf + 4]
                    sh_rows = [_shifted_rows(prevs[hlf], cur_r, nexts[hlf], lanes, s0, n, ntap - 1 - j, block)
                               for j in range(ntap)]
                    acc = b_r[:, lanes]
                    for j in range(ntap):
                        acc = acc + w_r[j:j + 1, lanes] * sh_rows[j]
                    xs.append(sh_rows)
                    us.append(acc)
                dus = [d_e * us[1] * _dsilu(us[0]), d_e * _silu(us[0])] if mode == 'glu' else [d_e * _dsilu(us[0])]
                for hlf in range(nh):
                    du_scr[hlf][pl.ds(s0, n), lanes] = dus[hlf]
                    if s0 < block:
                        db_acc[hlf] = db_acc[hlf] + dus[hlf]
                        for j in range(ntap):
                            dw_acc[hlf][j] = dw_acc[hlf][j] + dus[hlf] * xs[hlf][j]
            for hlf in range(nh):
                w_r = refs[5 * hlf + 3]
                dx_r, dw_r, db_r = outs[3 * hlf:3 * hlf + 3]
                db_r[:, lanes] += jnp.sum(db_acc[hlf], axis=0, keepdims=True)
                for j in range(ntap):
                    dw_r[j:j + 1, lanes] += jnp.sum(dw_acc[hlf][j], axis=0, keepdims=True)
                for s0 in range(0, block, CONV_STRIP):
                    dx = None
                    for j in range(ntap):
                        term = w_r[j:j + 1, lanes] * du_scr[hlf][pl.ds(s0 + ntap - 1 - j, CONV_STRIP), lanes]
                        dx = term if dx is None else dx + term
                    dx_r[pl.ds(s0, CONV_STRIP), lanes] = dx.astype(dx_r.dtype)
            return carry

        lax.fori_loop(0, tc // LANE, column, 0)

    rb = block // SUBLANE
    nrow8 = s // SUBLANE
    ins, specs = [], []
    for hlf in range(nh):
        o = hlf * off
        ins += [x, x, x, w, b]
        specs += [pl.BlockSpec((SUBLANE, tc), lambda j, i, o=o: (jnp.maximum(i * rb - 1, 0), j + o)),
                  pl.BlockSpec((block, tc), lambda j, i, o=o: (i, j + o)),
                  pl.BlockSpec((SUBLANE, tc), lambda j, i, o=o: (jnp.minimum((i + 1) * rb, nrow8 - 1), j + o)),
                  pl.BlockSpec((ntap, tc), lambda j, i, o=o: (0, j + o)),
                  pl.BlockSpec((1, tc), lambda j, i, o=o: (0, j + o))]
    ins += [dout, dout]
    specs += [pl.BlockSpec((block, tc), lambda j, i: (i, j)),
              pl.BlockSpec((SUBLANE, tc), lambda j, i: (jnp.minimum((i + 1) * rb, nrow8 - 1), j))]
    out_specs, out_shape = [], []
    for hlf in range(nh):
        out_specs += [pl.BlockSpec((block, tc), lambda j, i: (i, j)), pl.BlockSpec((ntap, tc), lambda j, i: (0, j)),
                      pl.BlockSpec((1, tc), lambda j, i: (0, j))]
        out_shape += [jax.ShapeDtypeStruct((s, f), MXU_DTYPE), jax.ShapeDtypeStruct((ntap, f), F32),
                      jax.ShapeDtypeStruct((1, f), F32)]
    res = pl.pallas_call(
        body, name=name, grid=(f // tc, nblk), in_specs=specs, out_specs=out_specs, out_shape=out_shape,
        scratch_shapes=[pltpu.VMEM((ext, tc), F32)] * nh,
        compiler_params=_cparams(("parallel", "arbitrary")),
    )(*ins)
    if nh == 1:
        return [res[0]], res[1], res[2]
    return [res[0], res[3]], jnp.concatenate([res[1], res[4]], axis=1), jnp.concatenate([res[2], res[5]], axis=1)


def loss_head(y, target, block=ROW_BLOCK):
    s, d = y.shape
    block = min(block, s)

    def body(y_r, t_r, acc_r, dy_r):
        e = y_r[...] - t_r[...]
        dy_r[...] = e * (1.0 / d)

        @pl.when(pl.program_id(0) == 0)
        def _():
            acc_r[...] = jnp.zeros_like(acc_r)

        acc_r[...] += jnp.sum((e * e).reshape(block // SUBLANE, SUBLANE, d), axis=0) * (0.5 / d)

    return pl.pallas_call(
        body, name="loss_head", grid=(s // block,),
        in_specs=[pl.BlockSpec((block, d), lambda i: (i, 0))] * 2,
        out_specs=[pl.BlockSpec((SUBLANE, d), lambda i: (0, 0)), pl.BlockSpec((block, d), lambda i: (i, 0))],
        out_shape=[jax.ShapeDtypeStruct((SUBLANE, d), F32), jax.ShapeDtypeStruct((s, d), F32)],
        compiler_params=_cparams(("arbitrary",)),
    )(y, target)


def adamw(name, w, g, m, v):
    r, c = w.shape
    tr = r if r <= 512 else _tile(r, (512, 256, 128, 64, 32, 16, 8))
    if c * tr * 4 > (1 << 21):
        tr = _tile(r, (256, 128, 64, 32, 16, 8))

    def body(w_r, g_r, m_r, v_r, d_r, nm_r, nv_r):
        gg = g_r[...]
        nm = ADAM_B1 * m_r[...] + (1.0 - ADAM_B1) * gg
        nv = ADAM_B2 * v_r[...] + (1.0 - ADAM_B2) * (gg * gg)
        m_hat = nm / (1.0 - ADAM_B1 ** ADAM_STEP)
        v_hat = nv / (1.0 - ADAM_B2 ** ADAM_STEP)
        d_r[...] = -ADAM_LR * (m_hat / (jnp.sqrt(v_hat) + ADAM_EPS) + ADAM_WD * w_r[...])
        nm_r[...] = nm
        nv_r[...] = nv

    spec = pl.BlockSpec((tr, c), lambda i: (i, 0))
    return pl.pallas_call(
        body, name=name, grid=(r // tr,), in_specs=[spec] * 4, out_specs=[spec] * 3,
        out_shape=[jax.ShapeDtypeStruct((r, c), F32)] * 3, compiler_params=_cparams(("parallel",)),
    )(w, g, m, v)


MESH = pl.DeviceIdType.MESH
_ANY = pl.BlockSpec(memory_space=pl.ANY)


def _place():
    return lax.axis_index("x"), lax.axis_index("y"), lax.axis_index("c")


class Packed:
    def __init__(self, shard_shape):
        self.r, self.c = shard_shape
        self.h = self.r // 2
        self.whole = (N_CHIPS, self.r, self.c)
        self.got = (N_CHIPS, self.h, self.c)
        self.slab_half = (self.h, self.c)

    def shard_half(self, ref, core):
        return ref.at[pl.ds(core * self.h, self.h)]

    def whole_half(self, ref, chip, core):
        return ref.at[chip, pl.ds(core * self.h, self.h)]

    def place(self, whole, shard, chip):
        return lax.dynamic_update_slice(whole, shard[None], (chip, 0, 0))

    def grad_half(self, ref, core):
        return ref.at[:, core]

    def pair_slab(self, ref, chip):
        return ref.at[chip]


class Layered:
    def __init__(self, shard_shape, axis):
        self.axis, self.size, self.h = axis, shard_shape[axis], shard_shape[0] // 2
        self.whole = tuple(d * N_CHIPS if i == axis else d for i, d in enumerate(shard_shape))
        self.got = (self.h,) + self.whole[1:]
        self.slab_half = (self.h,) + tuple(shard_shape[1:])

    def _window(self, first, chip):
        idx = [first, slice(None), slice(None)]
        idx[self.axis] = pl.ds(pl.multiple_of(chip * self.size, SUBLANE if self.axis == 1 else LANE), self.size)
        return tuple(idx)

    def shard_half(self, ref, core):
        return ref.at[pl.ds(core * self.h, self.h)]

    def whole_half(self, ref, chip, core):
        return ref.at[self._window(pl.ds(core * self.h, self.h), chip)]

    def place(self, whole, shard, chip):
        return lax.dynamic_update_slice_in_dim(whole, shard, chip * self.size, self.axis)

    def grad_half(self, ref, core):
        return ref.at[pl.ds(core * self.h, self.h)]

    def pair_slab(self, ref, chip):
        return ref.at[self._window(slice(None), chip)]


def allgather_chips(shards, kinds):
    n = len(shards)

    def body(*refs):
        w_refs, out_refs, send_sems, recv_sems = refs[:n], refs[n:2 * n], refs[2 * n], refs[2 * n + 1]
        x, y, cc = _place()
        sibling = (x, y, 1 - cc)
        chips = [(1 - x, y), (x, 1 - y), (1 - x, 1 - y)]

        def copy(t, k, chip, core, to, src=None):
            dst = kinds[t].whole_half(out_refs[t], 2 * chip[0] + chip[1], core)
            return pltpu.make_async_remote_copy(
                src_ref=dst if src is None else src, dst_ref=dst, send_sem=send_sems.at[6 * t + k],
                recv_sem=recv_sems.at[6 * t + k], device_id=to, device_id_type=MESH)

        first, passed = [], []
        for t in range(n):
            mine = kinds[t].shard_half(w_refs[t], cc)
            first += [copy(t, j, (x, y), cc, (*chip, cc), src=mine) for j, chip in enumerate(chips)]
        for cp in first:
            cp.start()
        for t in range(n):
            for j, chip in enumerate(chips):
                copy(t, j, chip, cc, (x, y, cc)).wait_recv()
                passed.append(copy(t, 3 + j, chip, cc, sibling))
                passed[-1].start()
        for t in range(n):
            for j, chip in enumerate(chips):
                copy(t, 3 + j, chip, 1 - cc, (x, y, cc)).wait_recv()
        for cp in first + passed:
            cp.wait_send()

    outs = pl.pallas_call(
        body, name="allgather_chips", in_specs=[_ANY] * n, out_specs=[_ANY] * n,
        out_shape=[jax.ShapeDtypeStruct(k.whole, s.dtype) for k, s in zip(kinds, shards)],
        scratch_shapes=[pltpu.SemaphoreType.DMA((6 * n,)), pltpu.SemaphoreType.DMA((6 * n,))],
    )(*shards)
    chip = 2 * lax.axis_index("x") + lax.axis_index("y")
    return [k.place(o, s, chip) for k, o, s in zip(kinds, outs, shards)]


def allgather_devices(buf):
    r, c = buf.shape

    def body(b_ref, out_ref, send_sems, recv_sems, local_sem):
        x, y, cc = _place()
        me = 4 * x + 2 * y + cc
        mine = pltpu.make_async_copy(b_ref, out_ref.at[me], local_sem)
        mine.start()
        copies = []
        for k in range(1, N_DEV):
            px, py, pc = x ^ (k >> 2), y ^ ((k >> 1) & 1), cc ^ (k & 1)
            cp = pltpu.make_async_remote_copy(src_ref=b_ref, dst_ref=out_ref.at[me], send_sem=send_sems.at[k - 1],
                                              recv_sem=recv_sems.at[k - 1], device_id=(px, py, pc), device_id_type=MESH)
            cp.start()
            copies.append((cp, 4 * px + 2 * py + pc))
        for k, (cp, peer) in enumerate(copies):
            pltpu.make_async_remote_copy(src_ref=b_ref, dst_ref=out_ref.at[peer], send_sem=send_sems.at[k],
                                         recv_sem=recv_sems.at[k], device_id=(x, y, cc), device_id_type=MESH).wait_recv()
        for cp, _ in copies:
            cp.wait_send()
        mine.wait()

    return pl.pallas_call(
        body, name="allgather_devices", in_specs=[_ANY], out_specs=_ANY,
        out_shape=jax.ShapeDtypeStruct((N_DEV, r, c), buf.dtype),
        scratch_shapes=[pltpu.SemaphoreType.DMA((N_DEV - 1,)), pltpu.SemaphoreType.DMA((N_DEV - 1,)),
                        pltpu.SemaphoreType.DMA],
    )(buf)


def swap_halves_sibling(gs, kinds):
    n = len(gs)

    def body(*refs):
        g_refs, out_refs, send_sems, recv_sems = refs[:n], refs[n:2 * n], refs[2 * n], refs[2 * n + 1]
        x, y, cc = _place()
        cps = []
        for t in range(n):
            cps.append(pltpu.make_async_remote_copy(
                src_ref=kinds[t].grad_half(g_refs[t], 1 - cc), dst_ref=out_refs[t], send_sem=send_sems.at[t],
                recv_sem=recv_sems.at[t], device_id=(x, y, 1 - cc), device_id_type=MESH))
            cps[-1].start()
        for cp in cps:
            cp.wait()

    return pl.pallas_call(
        body, name="swap_halves_sibling", in_specs=[_ANY] * n, out_specs=[_ANY] * n,
        out_shape=[jax.ShapeDtypeStruct(k.got, g.dtype) for k, g in zip(kinds, gs)],
        scratch_shapes=[pltpu.SemaphoreType.DMA((n,)), pltpu.SemaphoreType.DMA((n,))],
    )(*gs)


def exchange_chips(pairs, kinds):
    n = len(pairs)

    def body(*refs):
        p_refs, out_refs, send_sems, recv_sems = refs[:n], refs[n:2 * n], refs[2 * n], refs[2 * n + 1]
        x, y, cc = _place()
        me = 2 * x + y
        chips = [(1 - x, y), (x, 1 - y), (1 - x, 1 - y)]
        cps = []
        for t in range(n):
            for j, chip in enumerate(chips):
                cp = pltpu.make_async_remote_copy(
                    src_ref=kinds[t].pair_slab(p_refs[t], 2 * chip[0] + chip[1]), dst_ref=out_refs[t].at[me],
                    send_sem=send_sems.at[3 * t + j], recv_sem=recv_sems.at[3 * t + j], device_id=(*chip, cc),
                    device_id_type=MESH)
                cp.start()
                cps.append(cp)
        for t in range(n):
            for j, chip in enumerate(chips):
                pltpu.make_async_remote_copy(
                    src_ref=kinds[t].pair_slab(p_refs[t], me), dst_ref=out_refs[t].at[2 * chip[0] + chip[1]],
                    send_sem=send_sems.at[3 * t + j], recv_sem=recv_sems.at[3 * t + j], device_id=(x, y, cc),
                    device_id_type=MESH).wait_recv()
        for cp in cps:
            cp.wait_send()

    return pl.pallas_call(
        body, name="exchange_chips", in_specs=[_ANY] * n, out_specs=[_ANY] * n,
        out_shape=[jax.ShapeDtypeStruct((N_CHIPS,) + k.slab_half, p.dtype) for k, p in zip(kinds, pairs)],
        scratch_shapes=[pltpu.SemaphoreType.DMA((3 * n,)), pltpu.SemaphoreType.DMA((3 * n,))],
    )(*pairs)


def _row_tile(n, limit=512):
    return max(t for t in range(16, limit + 1, 16) if n % t == 0)


def sum_chips(got, own, kind, chip, name):
    def body(chip_ref, got_r, own_r, out_r):
        mine = own_r[...].astype(F32)
        acc = None
        for k in range(N_CHIPS):
            term = jnp.where(chip_ref[0] == k, mine, got_r[k].astype(F32))
            acc = term if acc is None else acc + term
        out_r[...] = acc

    if isinstance(kind, Packed):
        r, c = kind.slab_half
        tr = _row_tile(r)
        grid = (r // tr,)
        specs = [pl.BlockSpec((N_CHIPS, tr, c), lambda i, chip_ref: (0, i, 0)),
                 pl.BlockSpec((None, tr, c), lambda i, chip_ref: (chip_ref[0], i, 0))]
        out_spec = pl.BlockSpec((tr, c), lambda i, chip_ref: (i, 0))
    else:
        hl, a, b = kind.slab_half
        ta = _row_tile(a, 352)
        grid = (hl, a // ta)
        own_map = ((lambda l, i, chip_ref: (l, i, chip_ref[0])) if kind.axis == 2
                   else (lambda l, i, chip_ref: (l, chip_ref[0] * (a // ta) + i, 0)))
        specs = [pl.BlockSpec((N_CHIPS, None, ta, b), lambda l, i, chip_ref: (0, l, i, 0)),
                 pl.BlockSpec((None, ta, b), own_map)]
        out_spec = pl.BlockSpec((None, ta, b), lambda l, i, chip_ref: (l, i, 0))
    return pl.pallas_call(
        body, name=name,
        grid_spec=pltpu.PrefetchScalarGridSpec(num_scalar_prefetch=1, grid=grid, in_specs=specs, out_specs=out_spec),
        out_shape=jax.ShapeDtypeStruct(kind.slab_half, F32),
        compiler_params=_cparams(("parallel",) * len(grid)),
    )(chip, got, own)


def join_halves_sibling(halves):
    n = len(halves)

    def body(*refs):
        h_refs, out_refs, send_sems, recv_sems = refs[:n], refs[n:2 * n], refs[2 * n], refs[2 * n + 1]
        x, y, cc = _place()
        cps = []
        for t in range(n):
            cps.append(pltpu.make_async_remote_copy(
                src_ref=h_refs[t], dst_ref=out_refs[t].at[cc], send_sem=send_sems.at[t], recv_sem=recv_sems.at[t],
                device_id=(x, y, 1 - cc), device_id_type=MESH))
            cps[-1].start()
        for t in range(n):
            pltpu.make_async_remote_copy(
                src_ref=h_refs[t], dst_ref=out_refs[t].at[1 - cc], send_sem=send_sems.at[t], recv_sem=recv_sems.at[t],
                device_id=(x, y, cc), device_id_type=MESH).wait_recv()
        for cp in cps:
            cp.wait_send()

    outs = pl.pallas_call(
        body, name="join_halves_sibling", in_specs=[_ANY] * n, out_specs=[_ANY] * n,
        out_shape=[jax.ShapeDtypeStruct((2,) + h.shape, h.dtype) for h in halves],
        scratch_shapes=[pltpu.SemaphoreType.DMA((n,)), pltpu.SemaphoreType.DMA((n,))],
    )(*halves)
    core = lax.axis_index("c")
    return [lax.dynamic_update_slice_in_dim(o, h[None], core, 0) for o, h in zip(outs, halves)]


def add_own_half(g, got, kind, core, out_dtype, name):
    def body(c_ref, g_r, o_r, out_r):
        out_r[...] = (g_r[...] + o_r[...]).astype(out_r.dtype)

    if isinstance(kind, Packed):
        r, c = kind.slab_half
        tr = _row_tile(r)
        grid = (N_CHIPS, r // tr)
        specs = [pl.BlockSpec((None, None, tr, c), lambda i, j, c_ref: (i, c_ref[0], j, 0)),
                 pl.BlockSpec((None, tr, c), lambda i, j, c_ref: (i, j, 0))]
        out_spec = pl.BlockSpec((None, tr, c), lambda i, j, c_ref: (i, j, 0))
    else:
        hl, a, b = kind.got
        ta = _row_tile(a, 128)
        grid = (hl, a // ta)
        specs = [pl.BlockSpec((None, ta, b), lambda l, i, c_ref: (c_ref[0] * hl + l, i, 0)),
                 pl.BlockSpec((None, ta, b), lambda l, i, c_ref: (l, i, 0))]
        out_spec = pl.BlockSpec((None, ta, b), lambda l, i, c_ref: (l, i, 0))
    return pl.pallas_call(
        body, name=name,
        grid_spec=pltpu.PrefetchScalarGridSpec(num_scalar_prefetch=1, grid=grid, in_specs=specs, out_specs=out_spec),
        out_shape=jax.ShapeDtypeStruct(kind.got, out_dtype),
        compiler_params=_cparams(("parallel", "parallel")),
    )(core, g, got)


def sum_slabs(p, name):
    n, r, c = p.shape
    tr = _tile(r, [t for t in (512, 256, 128, 64, 32, 16) if n * t * c * p.dtype.itemsize <= (1 << 23)])

    def body(p_r, out_r):
        acc = p_r[0].astype(F32)
        for k in range(1, n):
            acc = acc + p_r[k].astype(F32)
        out_r[...] = acc

    return pl.pallas_call(
        body, name=name, grid=(r // tr,), in_specs=[pl.BlockSpec((n, tr, c), lambda i: (0, i, 0))],
        out_specs=pl.BlockSpec((tr, c), lambda i: (i, 0)), out_shape=jax.ShapeDtypeStruct((r, c), F32),
        compiler_params=_cparams(("parallel",)),
    )(p)


def _lay(arr, axis, pieces, total, reps=()):
    items = [(d, n, lax.slice_in_dim(arr, s0, s0 + n, axis=axis)) for s0, n, d in pieces]
    items += [(d, n, jnp.repeat(lax.slice_in_dim(arr, s0, s0 + 1, axis=axis), n, axis=axis)) for s0, d, n in reps]
    items.sort(key=lambda t: t[0])
    parts, pos = [], 0

    def zeros(n):
        sh = list(arr.shape)
        sh[axis] = n
        return jnp.zeros(sh, arr.dtype)

    for d, n, v in items:
        if d > pos:
            parts.append(zeros(d - pos))
        parts.append(v)
        pos = d + n
    if total > pos:
        parts.append(zeros(total - pos))
    return jnp.concatenate(parts, axis=axis) if len(parts) > 1 else parts[0]


def _unlay_parts(g, axis, pieces, reps=()):
    out = [(s0, lax.slice_in_dim(g, d, d + n, axis=axis)) for s0, n, d in pieces]
    out += [(s0, jnp.sum(lax.slice_in_dim(g, d, d + n, axis=axis), axis=axis, keepdims=True)) for s0, d, n in reps]
    return out


def _join(parts, axis):
    parts = sorted(parts, key=lambda t: t[0])
    return jnp.concatenate([p for _, p in parts], axis=axis)


def _heads(src0, n_heads, width, padded, dst0=0):
    return [(src0 + h * width, width, dst0 + h * padded) for h in range(n_heads)]


_XQ = lambda src0: _heads(src0, XA_HEADS, XA_HD, LANE)
_XA_W = XA_HEADS * LANE

LAYOUT = {
    'a': dict(
        segs=dict(q=(_heads(0, 4, 96, LANE), 512, ()), k=(_heads(384, 4, 96, LANE), 512, ()),
                  v=(_heads(768, 4, 192, 256), 1024, ()), glr=([(1536, 16, 0)], LANE, ()),
                  og=(_heads(1552, 4, 192, 256), 1024, ()), xq=(_XQ(2320), _XA_W, ())),
        tok=(_heads(0, 4, 192, 256), 1024), xa=(_XQ(768), _XA_W)),
    'b': dict(
        segs=dict(q=([(0, 1536, 0)], 1536, ()), k=([(1536, 1536, 0)], 1536, ()), v=([(3072, 1536, 0)], 1536, ()),
                  xq=(_XQ(4608), _XA_W, ())),
        tok=([(0, 512, 0)], 512), xa=(_XQ(512), _XA_W)),
    'c': dict(
        segs=dict(z=(_heads(0, 12, 64, LANE), 1536, ()),
                  xbc=(_heads(768, 12, 64, LANE) + [(1536, 256, 1536), (1792, 256, 1792)], 2048, ()),
                  dt=([(2048, 12, 0)], LANE, ()),
                  xq=(_XQ(2060), _XA_W, ())),
        tok=(_heads(0, 12, 64, LANE), 1536), xa=(_XQ(768), _XA_W)),
    'd': dict(
        segs=dict(q=([(0, 768, 0)], 768, ()), f=([(768, 768, 0)], 768, ()), i=([(1536, 768, 0)], 768, ()),
                  og=([(2304, 768, 0)], 768, ()), xq=(_XQ(3072), _XA_W, ())),
        tok=([(0, 768, 0)], 768), xa=(_XQ(768), _XA_W)),
}
KINDS = 'abcd'
_XS_PIECES = _heads(0, 12, 64, LANE)
_XBC_PIECES = _XS_PIECES + [(768, 256, 1536), (1024, 256, 1792)]
_HEAD_REPS = tuple((h, h * LANE, LANE) for h in range(12))


def _row(v):
    return v.reshape(1, -1)


def local_step(x, mem, positions, target, W):
    s = x.shape[0]
    grads = {}
    scan_block = CHUNK * SCAN_CHUNKS

    inv_freq = ROPE_THETA ** (-jnp.arange(DIL_HD // 2, dtype=F32) / (DIL_HD // 2))
    ang = positions.astype(F32)[:, None] * inv_freq
    cosf = jnp.concatenate([jnp.cos(ang), jnp.cos(ang)], axis=-1)
    sinf = jnp.concatenate([-jnp.sin(ang), jnp.sin(ang)], axis=-1)

    w_up, w_down = W['ffn_w_up'].astype(MXU_DTYPE), W['ffn_w_down'].astype(MXU_DTYPE)
    mem_g = _row(W['mem_norm'])
    (mem_n,) = tmap("mem_norm", _norm_stage, [mem], [mem_g], [(D_MODEL, MXU_DTYPE)])
    kv_lay = _heads(0, 4, 64, LANE) + _heads(256, 4, 64, LANE, dst0=_XA_W)

    saved = []
    for i in range(4):
        kind = KINDS[i]
        lay = LAYOUT[kind]
        sv = dict(x0=x)
        w_in = W[f'{kind}_w_in']
        w_out = W[f'{kind}_w_out']
        sv['w_seg'] = {n: _lay(w_in, 1, p, t, r).astype(MXU_DTYPE) for n, (p, t, r) in lay['segs'].items()}
        sv['wo_tok'] = _lay(w_out, 0, *lay['tok']).astype(MXU_DTYPE)
        sv['wo_xa'] = _lay(w_out, 0, *lay['xa']).astype(MXU_DTYPE)
        sv['w_kv'] = _lay(W['xa_w_kv'][i], 1, kv_lay, 2 * _XA_W).astype(MXU_DTYPE)
        sv['g1'] = _row(W['mix_norm'][i])
        (h,) = tmap(f"mix_norm_{i}", _norm_stage, [x], [sv['g1']], [(D_MODEL, MXU_DTYPE)])
        sv['h'] = h
        seg = {n: matmul(h, w) for n, w in sv['w_seg'].items()}
        sv['seg'] = seg

        if kind == 'a':
            sv['w2'] = _lay(_lay(W['a_w_gate2'], 1, _heads(0, 4, 96, LANE), 512), 0, [(0, 16, 0)], LANE)
            sv['bg'] = _row(_lay(W['a_b_gate'], 0, _heads(0, 4, 96, LANE), 512))
            sv['on'] = _row(_lay(W['a_o_norm'], 0, [(0, 192, 0)], 256))
            (la,) = tmap("gla_pre", _gla_pre, [seg['glr']], [sv['w2'], sv['bg']], [(512, F32)])
            sv['la'] = la
            sv['scan_fn'] = _gla_step(GLA_HEADS, 2 * LANE, GLA_DK ** -0.5)
            sv['scan_rows'] = [seg['q'], seg['k'], seg['v'], la]
            (o,), sv['states'] = rscan("gla_scan", sv['scan_fn'], [(LANE, 2 * LANE)] * GLA_HEADS, sv['scan_rows'], [],
                                       [(1024, F32)], scan_block)
            sv['o'] = o
            (tok,) = tmap("gla_post", _gla_post, [o, seg['og']], [sv['on']], [(1024, MXU_DTYPE)])
        elif kind == 'b':
            sv['qg'], sv['kg'] = _row(W['b_q_norm']), _row(W['b_k_norm'])
            os_, ls_ = [], []
            qkn = tmap("dil_pre", _dil_pre, [seg['q'], seg['k'], cosf, sinf], [sv['qg'], sv['kg']], [(512, F32)] * 6)
            sv['qn'], sv['kn'] = qkn[:3], qkn[3:]
            for g, (window, r) in enumerate(DIL_GROUPS):
                assert window // r == DIL_BLOCK and (s // r) % DIL_BLOCK == 0
                o, lse = dil_attn(f"dil_attn_{g}", sv['qn'][g], sv['kn'][g], seg['v'], r, g)
                os_.append(o)
                ls_.append(lse)
            sv['os'], sv['ls'] = os_, ls_
            (tok,) = tmap("dil_merge", _dil_merge, os_ + ls_, [], [(512, MXU_DTYPE)])
        elif kind == 'c':
            sv['cw'] = _lay(W['c_conv_w'], 1, _XBC_PIECES, 2048)
            sv['cb'] = _row(_lay(W['c_conv_b'], 0, _XBC_PIECES, 2048))
            sv['dtb'] = _row(_lay(W['c_dt_bias'], 0, [], 1536, _HEAD_REPS))
            sv['alog'] = _row(_lay(W['c_a_log'], 0, [], 1536, _HEAD_REPS))
            sv['dsk'] = _row(_lay(W['c_d'], 0, [], 1536, _HEAD_REPS))
            sv['cn'] = _row(_lay(W['c_norm'], 0, _XS_PIECES, 1536))
            xact = conv_fwd("ssm_conv", seg['xbc'], sv['cw'], sv['cb'], 'silu', F32, 512)
            sv['xact'] = xact
            sv['scan_rows'] = [xact, seg['dt']]
            sv['scan_params'] = [sv['dtb'], sv['alog'], sv['dsk']]
            (yv,), sv['states'] = rscan("ssd_scan", _ssd_step, [(LANE, LANE)] * SSM_HEADS, sv['scan_rows'],
                                        sv['scan_params'], [(1536, F32)], scan_block)
            sv['y'] = yv
            (tok,) = tmap("ssd_post", _mamba_post, [yv, seg['z']], [sv['cn']], [(1536, MXU_DTYPE)])
        else:
            sv['lbnd'] = W['d_lower_bounds']
            sv['on'] = _row(W['d_o_norm'])
            qq, kk, la = tmap("hgrn_pre", _hgrn_pre, [seg['q'], seg['f']], [sv['lbnd']], [(768, F32)] * 3)
            sv['scan_fn'] = _gla_step(HGRN_HEADS, LANE, 1.0)
            sv['scan_rows'] = [qq, kk, seg['i'], la]
            (o,), sv['states'] = rscan("hgrn_scan", sv['scan_fn'], [(LANE, LANE)] * HGRN_HEADS, sv['scan_rows'], [],
                                       [(768, F32)], scan_block)
            sv['o'] = o
            (tok,) = tmap("hgrn_post", _hgrn_post, [o, seg['og']], [sv['on']], [(768, MXU_DTYPE)])
        sv['tok'] = tok

        kv = matmul(mem_n, sv['w_kv'])
        sv['kv'] = kv
        sv['xqg'] = _row(_lay(W['xa_q_norm'][i], 0, [(0, 64, 0)], LANE))
        sv['xkg'] = _row(_lay(W['xa_k_norm'][i], 0, [(0, 64, 0)], LANE))
        (xa,) = tmap(f"xattn_{i}", _xattn, [seg['xq']], [kv, sv['xqg'], sv['xkg']], [(_XA_W, MXU_DTYPE)])
        sv['xa'] = xa
        x = matmul(tok, sv['wo_tok'], add=x)
        x = matmul(xa, sv['wo_xa'], add=x)
        sv['x1'] = x

        sv['g2'] = _row(W['ffn_norm'][i])
        sv['fcw'] = W['ffn_conv_w'][i]
        sv['fcb'] = _row(W['ffn_conv_b'][i])
        (h2,) = tmap(f"ffn_norm_{i}", _norm_stage, [x], [sv['g2']], [(D_MODEL, MXU_DTYPE)])
        sv['h2'] = h2
        u0 = matmul(h2, w_up, b_layer=i)
        sv['u0'] = u0
        act = conv_fwd("ffn_conv", u0, sv['fcw'], sv['fcb'], 'glu', MXU_DTYPE, 1408)
        sv['act'] = act
        x = matmul(act, w_down, b_layer=i, add=x)
        saved.append(sv)

    loss_acc, dx = loss_head(x, target)

    g_stack = {n: [None] * 4 for n in ('mix_norm', 'xa_w_kv', 'xa_q_norm', 'xa_k_norm', 'ffn_norm', 'ffn_conv_w',
                                        'ffn_conv_b')}
    g_up = jnp.zeros(W['ffn_w_up'].shape, F32)
    g_down = jnp.zeros(W['ffn_w_down'].shape, F32)
    d_memn = None
    for i in reversed(range(4)):
        kind = KINDS[i]
        lay = LAYOUT[kind]
        sv = saved[i]
        seg = sv['seg']
        dact = matmul(dx, w_down, tb=True, b_layer=i)
        g_down = matmul(sv['act'], dx, ta=True, into=(g_down, i, 0))
        (du_g, du_v), dcw, dcb = conv_bwd("ffn_conv_bwd", sv['u0'], sv['fcw'], sv['fcb'], dact, 'glu', 1408)
        g_stack['ffn_conv_w'][i], g_stack['ffn_conv_b'][i] = dcw, dcb[0]
        dh2 = matmul(du_v, w_up, tb=True, b_layer=i, b_koff=D_FF, add=matmul(du_g, w_up, tb=True, b_layer=i))
        g_up = matmul(sv['h2'], du_g, ta=True, into=(g_up, i, 0))
        g_up = matmul(sv['h2'], du_v, ta=True, into=(g_up, i, D_FF))
        (dx,), (dg2,) = tmap_bwd(f"ffn_norm_bwd_{i}", _norm_stage, [sv['x1']], [sv['g2']], [dh2], [True], {0: dx})
        g_stack['ffn_norm'][i] = dg2[0]
        dtok = matmul(dx, sv['wo_tok'], tb=True)
        dxa = matmul(dx, sv['wo_xa'], tb=True)
        g_wo = _unlay_parts(matmul(sv['tok'], dx, ta=True), 0, lay['tok'][0]) \
            + _unlay_parts(matmul(sv['xa'], dx, ta=True), 0, lay['xa'][0])
        grads[f'{kind}_w_out'] = _join(g_wo, 0)
        (dxq,), (dkv, dqg, dkg) = tmap_bwd(f"xattn_bwd_{i}", _xattn, [seg['xq']], [sv['kv'], sv['xqg'], sv['xkg']],
                                           [dxa], [True])
        g_stack['xa_q_norm'][i], g_stack['xa_k_norm'][i] = dqg[0, :XA_HD], dkg[0, :XA_HD]
        g_stack['xa_w_kv'][i] = _join(_unlay_parts(matmul(mem_n, dkv, ta=True), 1, kv_lay), 1)
        d_memn = matmul(dkv, sv['w_kv'], tb=True, add=d_memn)
        dseg = dict(xq=dxq)
        if kind == 'a':
            (do, dog), (don,) = tmap_bwd("gla_post_bwd", _gla_post, [sv['o'], seg['og']], [sv['on']], [dtok],
                                         [True, True], grad_dtype=F32)
            grads['a_o_norm'] = don[0, :GLA_DV]
            (dq, dk, dv, dla), _ = rscan_bwd("gla_scan_bwd", sv['scan_fn'], sv['states'], sv['scan_rows'], [], [do],
                                             scan_block, grad_dtype=F32)
            (dglr,), (dw2, dbg) = tmap_bwd("gla_pre_bwd", _gla_pre, [seg['glr']], [sv['w2'], sv['bg']], [dla], [True])
            grads['a_w_gate2'] = _join(_unlay_parts(dw2[:GLA_RANK], 1, _heads(0, 4, 96, LANE)), 1)
            grads['a_b_gate'] = _join(_unlay_parts(dbg[0], 0, _heads(0, 4, 96, LANE)), 0)
            dseg.update(q=dq, k=dk, v=dv, glr=dglr, og=dog)
        elif kind == 'b':
            res, _ = tmap_bwd("dil_merge_bwd", _dil_merge, sv['os'] + sv['ls'], [], [dtok], [True] * 6, grad_dtype=F32)
            dqn, dkn, dvs = [], [], []
            for g, (_, r) in enumerate(DIL_GROUPS):
                a_, b_, c_ = dil_attn_bwd(f"dil_attn_bwd_{g}", sv['qn'][g], sv['kn'][g], seg['v'], res[g], res[3 + g],
                                          r, g)
                dqn.append(a_)
                dkn.append(b_)
                dvs.append(c_)
            (dq, dk), (dqg, dkg) = tmap_bwd("dil_pre_bwd", _dil_pre, [seg['q'], seg['k'], cosf, sinf],
                                            [sv['qg'], sv['kg']], dqn + dkn, [True, True, False, False])
            dseg.update(q=dq, k=dk, v=jnp.concatenate(dvs, axis=1))
            grads['b_q_norm'], grads['b_k_norm'] = dqg[0], dkg[0]
        elif kind == 'c':
            (dy, dz), (dcn,) = tmap_bwd("ssd_post_bwd", _mamba_post, [sv['y'], seg['z']], [sv['cn']], [dtok],
                                        [True, True], grad_dtype=F32)
            grads['c_norm'] = _join(_unlay_parts(dcn[0], 0, _XS_PIECES), 0)
            (dxact, ddt), (ddtb, dalog, ddsk) = rscan_bwd("ssd_scan_bwd", _ssd_step, sv['states'], sv['scan_rows'],
                                                          sv['scan_params'], [dy], scan_block, grad_dtype=F32)
            for nm, gv in (('c_dt_bias', ddtb), ('c_a_log', dalog), ('c_d', ddsk)):
                grads[nm] = _join(_unlay_parts(gv[0], 0, [], _HEAD_REPS), 0)
            (dxbc,), dcw, dcb = conv_bwd("ssm_conv_bwd", seg['xbc'], sv['cw'], sv['cb'], dxact, 'silu', 512)
            grads['c_conv_w'] = _join(_unlay_parts(dcw, 1, _XBC_PIECES), 1)
            grads['c_conv_b'] = _join(_unlay_parts(dcb[0], 0, _XBC_PIECES), 0)
            dseg.update(z=dz, xbc=dxbc, dt=ddt)
        else:
            (do, dog), (don,) = tmap_bwd("hgrn_post_bwd", _hgrn_post, [sv['o'], seg['og']], [sv['on']], [dtok],
                                         [True, True], grad_dtype=F32)
            grads['d_o_norm'] = don[0]
            (dqq, dkk, di, dla), _ = rscan_bwd("hgrn_scan_bwd", sv['scan_fn'], sv['states'], sv['scan_rows'], [], [do],
                                               scan_block, grad_dtype=F32)
            (dq, df), (dlb,) = tmap_bwd("hgrn_pre_bwd", _hgrn_pre, [seg['q'], seg['f']], [sv['lbnd']], [dqq, dkk, dla],
                                        [True, True])
            grads['d_lower_bounds'] = dlb
            dseg.update(q=dq, f=df, i=di, og=dog)
        dh = None
        g_in = []
        for n, (p, t, rp) in lay['segs'].items():
            dh = matmul(dseg[n], sv['w_seg'][n], tb=True, add=dh)
            g_in += _unlay_parts(matmul(sv['h'], dseg[n], ta=True), 1, p, rp)
        grads[f'{kind}_w_in'] = _join(g_in, 1)
        (dx,), (dg1,) = tmap_bwd(f"mix_norm_bwd_{i}", _norm_stage, [sv['x0']], [sv['g1']], [dh], [True], {0: dx})
        g_stack['mix_norm'][i] = dg1[0]

    _, (dmg,) = tmap_bwd("mem_norm_bwd", _norm_stage, [mem], [mem_g], [d_memn], [False])
    grads['mem_norm'] = dmg[0]
    for n, parts in g_stack.items():
        grads[n] = jnp.stack(parts)
    grads['ffn_w_up'], grads['ffn_w_down'] = g_up, g_down
    return loss_acc, dx, grads


def _pack(arrs, dtype, row_multiple=PACK_ROWS):
    parts, rows = [], 0
    for a in arrs:
        f = a.reshape(-1).astype(dtype)
        unit = PACK_ROWS * PACK_COLS
        pad = (-f.shape[0]) % unit
        if pad:
            f = jnp.concatenate([f, jnp.zeros((pad,), dtype)])
        parts.append(f.reshape(-1, PACK_COLS))
        rows += parts[-1].shape[0]
    if rows % row_multiple:
        parts.append(jnp.zeros((row_multiple - rows % row_multiple, PACK_COLS), dtype))
    return jnp.concatenate(parts, axis=0)


def _unpack(buf, shapes):
    out, row = [], 0
    for sh in shapes:
        n = int(np.prod(sh))
        rows = -(-n // (PACK_ROWS * PACK_COLS)) * PACK_ROWS
        out.append(buf[row:row + rows].reshape(-1)[:n].reshape(sh))
        row += rows
    return out


def _pack_rows(arrs):
    parts = []
    for a in arrs:
        f = a.reshape(-1).astype(F32)
        parts.append(jnp.pad(f, (0, (-f.shape[0]) % PACK_COLS)))
    flat = jnp.concatenate(parts)
    rows = flat.shape[0] // PACK_COLS
    return jnp.pad(flat, (0, (-rows % 16) * PACK_COLS)).reshape(-1, PACK_COLS)


def _unpack_rows(buf, shapes):
    flat, out, pos = buf.reshape(-1), [], 0
    for sh in shapes:
        n = int(np.prod(sh))
        out.append(flat[pos:pos + n].reshape(sh))
        pos += -(-n // PACK_COLS) * PACK_COLS
    return out


def _split_chips(a, axis):
    sh = a.shape
    return jnp.moveaxis(a.reshape(sh[:axis] + (N_CHIPS, sh[axis] // N_CHIPS) + sh[axis + 1:]), axis, 0)


def _merge_chips(a, axis):
    a = jnp.moveaxis(a, 0, axis)
    sh = a.shape
    return a.reshape(sh[:axis] + (sh[axis] * sh[axis + 1],) + sh[axis + 2:])


def kernel(x, mem, positions, mem_norm, mix_norm, xa_w_kv, xa_q_norm, xa_k_norm, ffn_norm, ffn_w_up, ffn_conv_w, ffn_conv_b, ffn_w_down, a_w_in, a_w_gate2, a_b_gate, a_o_norm, a_w_out, b_w_in, b_q_norm, b_k_norm, b_w_out, c_w_in, c_conv_w, c_conv_b, c_dt_bias, c_a_log, c_d, c_norm, c_w_out, d_w_in, d_lower_bounds, d_o_norm, d_w_out, loss_target, m_mem_norm, m_mix_norm, m_xa_w_kv, m_xa_q_norm, m_xa_k_norm, m_ffn_norm, m_ffn_w_up, m_ffn_conv_w, m_ffn_conv_b, m_ffn_w_down, m_a_w_in, m_a_w_gate2, m_a_b_gate, m_a_o_norm, m_a_w_out, m_b_w_in, m_b_q_norm, m_b_k_norm, m_b_w_out, m_c_w_in, m_c_conv_w, m_c_conv_b, m_c_dt_bias, m_c_a_log, m_c_d, m_c_norm, m_c_w_out, m_d_w_in, m_d_lower_bounds, m_d_o_norm, m_d_w_out, v_mem_norm, v_mix_norm, v_xa_w_kv, v_xa_q_norm, v_xa_k_norm, v_ffn_norm, v_ffn_w_up, v_ffn_conv_w, v_ffn_conv_b, v_ffn_w_down, v_a_w_in, v_a_w_gate2, v_a_b_gate, v_a_o_norm, v_a_w_out, v_b_w_in, v_b_q_norm, v_b_k_norm, v_b_w_out, v_c_w_in, v_c_conv_w, v_c_conv_b, v_c_dt_bias, v_c_a_log, v_c_d, v_c_norm, v_c_w_out, v_d_w_in, v_d_lower_bounds, v_d_o_norm, v_d_w_out):
    args = locals()
    w = {n: args[n] for n in WEIGHTS}
    m = {n: args['m_' + n] for n in WEIGHTS}
    v = {n: args['v_' + n] for n in WEIGHTS}
    cx, cy, cc = lax.axis_index("x"), lax.axis_index("y"), lax.axis_index("c")
    chip = 2 * cx + cy

    packed_names = [n for n in BIG if n not in LAYERED]
    packed_shapes = [w[n].shape for n in packed_names]
    packed_w = _pack([w[n] for n in packed_names], MXU_DTYPE, 1024)
    kinds = [Packed(packed_w.shape)] + [Layered(w[n].shape, SHARD_AXIS[n]) for n in LAYERED]
    gathered, *whole = allgather_chips([packed_w] + [w[n].astype(MXU_DTYPE) for n in LAYERED], kinds)
    per_chip = [_unpack(gathered[j], packed_shapes) for j in range(N_CHIPS)]
    full = {n: _merge_chips(jnp.stack([per_chip[j][k] for j in range(N_CHIPS)]), SHARD_AXIS[n])
            for k, n in enumerate(packed_names)}
    full.update(zip(LAYERED, whole))
    small_sharded = [n for n in SMALL if n in SHARD_AXIS]
    sg = allgather_devices(_pack([w[n] for n in small_sharded], F32))
    per_chip_s = [_unpack(sg[2 * j], [w[n].shape for n in small_sharded]) for j in range(N_CHIPS)]
    for k, n in enumerate(small_sharded):
        full[n] = _merge_chips(jnp.stack([per_chip_s[j][k] for j in range(N_CHIPS)]), SHARD_AXIS[n])
    for n in SMALL:
        if n not in SHARD_AXIS:
            full[n] = w[n]

    loss_acc, dx, grads = local_step(x[0], mem[0], positions[0], loss_target[0], full)
    loss = lax.psum(jnp.sum(loss_acc), ("x", "y", "c"))

    gb = jnp.stack([_pack([_split_chips(grads[n], SHARD_AXIS[n])[j] for n in packed_names], F32, 1024)
                    for j in range(N_CHIPS)])
    rows = gb.shape[1]
    gs_ = [gb.reshape(N_CHIPS, 2, rows // 2, PACK_COLS)] + [grads[n] for n in LAYERED]
    core_id, chip_id = cc.reshape(1).astype(jnp.int32), chip.reshape(1).astype(jnp.int32)
    gots = swap_halves_sibling(gs_, kinds)
    pairs = [add_own_half(g, o, k, core_id, GRAD_WIRE_DTYPE, f"add_own_half_{t}")
             for t, (g, o, k) in enumerate(zip(gs_, gots, kinds))]
    recvd = exchange_chips(pairs, kinds)
    halves = [sum_chips(r, p, k, chip_id, f"sum_chips_{t}") for t, (r, p, k) in enumerate(zip(recvd, pairs, kinds))]
    red, *red_layered = join_halves_sibling(halves)
    g_big = dict(zip(packed_names, _unpack(red.reshape(rows, PACK_COLS), packed_shapes)))
    for n, r in zip(LAYERED, red_layered):
        g_big[n] = r.reshape(w[n].shape)

    small_full_shapes = [grads[n].shape for n in SMALL]
    gs = sum_slabs(allgather_devices(_pack_rows([grads[n] for n in SMALL])), "sum_devices")
    g_small = {}
    for n, gfull in zip(SMALL, _unpack_rows(gs, small_full_shapes)):
        if n in SHARD_AXIS:
            ax = SHARD_AXIS[n]
            size = gfull.shape[ax] // N_CHIPS
            gfull = lax.dynamic_slice_in_dim(gfull, chip * size, size, axis=ax)
        g_small[n] = gfull

    g_out, delta, new_m, new_v = {**g_big, **g_small}, {}, {}, {}
    for n in WEIGHTS:
        sh = w[n].shape
        two_d = (-1, sh[-1])
        d_, m_, v_ = adamw(f"adamw_{n}", w[n].reshape(two_d), g_out[n].reshape(two_d), m[n].reshape(two_d),
                           v[n].reshape(two_d))
        delta[n], new_m[n], new_v[n] = d_.reshape(sh), m_.reshape(sh), v_.reshape(sh)

    return (loss, dx[None], *[g_out[n] for n in WEIGHTS], *[delta[n] for n in WEIGHTS],
            *[new_m[n] for n in WEIGHTS], *[new_v[n] for n in WEIGHTS])
```

```python
import functools
import math

import jax
import jax.numpy as jnp
import numpy as np
from jax import lax
from jax.experimental import pallas as pl
from jax.experimental.pallas import tpu as pltpu

F32 = jnp.float32
MXU_DTYPE = jnp.bfloat16
GRAD_WIRE_DTYPE = jnp.bfloat16
VMEM_LIMIT_V7X = 56 * 1024 * 1024
LANE = 128
SUBLANE = 8

D_MODEL = 1024
N_MEM = 256
EPS = 1e-6
ROPE_THETA = 10000.0
CHUNK = 64
XA_HEADS, XA_HD = 4, 64
GLA_HEADS, GLA_DK, GLA_DV, GLA_RANK, GLA_GATE_NORM = 4, 96, 192, 16, 16.0
DIL_GROUPS = ((128, 1), (512, 4), (2048, 16))
DIL_HEADS, DIL_HD, DIL_BLOCK = 4, 128, 128
SSM_HD, SSM_HEADS, SSM_GROUPS, SSM_STATE, SSM_CONV = 64, 12, 2, 128, 4
HGRN_HEADS, HGRN_DK = 6, 128
D_FF = 2816
FFN_CONV = 3
ADAM_LR, ADAM_B1, ADAM_B2, ADAM_EPS, ADAM_WD, ADAM_STEP = 0.001, 0.9, 0.999, 1e-08, 0.01, 10

MM_TILES = (2816, 1408, 1024, 768, 512, 384, 256, 128)
MM_K_TILES = (2816, 2048, 1536, 1408, 1024, 768, 512, 384, 256, 128)
MM_MIN_OUT_TILE = 512 * 1024
MM_VMEM_BUDGET = 40 * 1024 * 1024
ROW_BLOCK = 256
SCAN_CHUNKS = 2
PACK_COLS = 1024
PACK_ROWS = 32

WEIGHTS = ['mem_norm', 'mix_norm', 'xa_w_kv', 'xa_q_norm', 'xa_k_norm', 'ffn_norm', 'ffn_w_up', 'ffn_conv_w',
           'ffn_conv_b', 'ffn_w_down', 'a_w_in', 'a_w_gate2', 'a_b_gate', 'a_o_norm', 'a_w_out', 'b_w_in', 'b_q_norm',
           'b_k_norm', 'b_w_out', 'c_w_in', 'c_conv_w', 'c_conv_b', 'c_dt_bias', 'c_a_log', 'c_d', 'c_norm', 'c_w_out',
           'd_w_in', 'd_lower_bounds', 'd_o_norm', 'd_w_out']
SHARD_AXIS = {'xa_w_kv': 1, 'ffn_w_up': 2, 'ffn_conv_w': 2, 'ffn_w_down': 1, 'a_w_in': 1, 'a_w_gate2': 1, 'a_w_out': 0,
              'b_w_in': 1, 'b_w_out': 1, 'c_w_in': 1, 'c_conv_w': 1, 'c_w_out': 0, 'd_w_in': 1, 'd_w_out': 0}
BIG = ['xa_w_kv', 'ffn_w_up', 'ffn_w_down', 'a_w_in', 'a_w_gate2', 'a_w_out', 'b_w_in', 'b_w_out', 'c_w_in', 'c_w_out',
       'd_w_in', 'd_w_out']
SMALL = [n for n in WEIGHTS if n not in BIG]
LAYERED = ['ffn_w_up', 'ffn_w_down']
N_CHIPS = 4
N_DEV = 8


class _MatmulSet:
    def __init__(self, cast, precision):
        def dot(a, b, dims):
            if cast:
                a = a.astype(MXU_DTYPE)
                b = b.astype(MXU_DTYPE)
            return lax.dot_general(a, b, (dims, ((), ())), precision=precision, preferred_element_type=F32)

        @jax.custom_vjp
        def nn(a, b):
            return dot(a, b, ((1,), (0,)))

        @jax.custom_vjp
        def nt(a, b):
            return dot(a, b, ((1,), (1,)))

        @jax.custom_vjp
        def tn(a, b):
            return dot(a, b, ((0,), (0,)))

        nn.defvjp(lambda a, b: (nn(a, b), (a, b)), lambda r, g: (nt(g, r[1]), tn(r[0], g)))
        nt.defvjp(lambda a, b: (nt(a, b), (a, b)), lambda r, g: (nn(g, r[1]), tn(g, r[0])))
        tn.defvjp(lambda a, b: (tn(a, b), (a, b)), lambda r, g: (nt(r[1], g), nn(r[0], g)))
        self.nn, self.nt, self.tn = nn, nt, tn


mm = _MatmulSet(True, None)
hi = _MatmulSet(False, lax.Precision.HIGHEST)


def _sigmoid(x):
    return jax.nn.sigmoid(x)


def _silu(x):
    return x * jax.nn.sigmoid(x)


def _softplus(x):
    return jnp.maximum(x, 0.0) + jnp.log1p(jnp.exp(-jnp.abs(x)))


def _rms(x, g, n_real=None):
    n = n_real or x.shape[-1]
    ms = jnp.sum(x * x, axis=-1, keepdims=True) * (1.0 / n)
    return x * lax.rsqrt(ms + EPS) * g


@jax.custom_vjp
def _swap_halves(x):
    return pltpu.roll(x, 64, 1)


_swap_halves.defvjp(lambda x: (_swap_halves(x), None), lambda _, g: (_swap_halves(g),))


def _tile(n, cands):
    for c in cands:
        if n % c == 0:
            return c
    raise ValueError(f"no tile for {n} among {cands}")


def _cparams(sem):
    return pltpu.CompilerParams(dimension_semantics=sem, vmem_limit_bytes=VMEM_LIMIT_V7X)


def _f32(v):
    return v.astype(F32) if jnp.issubdtype(v.dtype, jnp.floating) else v


def matmul(a, b, *, ta=False, tb=False, add=None, out_dtype=F32, b_layer=None, b_koff=0, into=None):
    m, k = (a.shape[1], a.shape[0]) if ta else a.shape
    b2 = b.shape[1:] if b_layer is not None else b.shape
    n = b2[0] if tb else b2[1]
    assert b_koff + k <= (b2[1] if tb else b2[0]), (a.shape, b.shape, ta, tb, b_koff)
    sa, sb, so = a.dtype.itemsize, b.dtype.itemsize, jnp.dtype(out_dtype).itemsize
    n_align = math.gcd(n, into[2]) if into is not None and into[2] else n
    k_align = math.gcd(k, b_koff) if b_koff else k

    def vmem(tm_, tn_, tk_):
        return (2 * tm_ * tk_ * sa + 2 * tk_ * tn_ * sb + 2 * tm_ * tn_ * so + (tm_ * tn_ * 4 if tk_ < k else 0)
                + (2 * tm_ * tn_ * add.dtype.itemsize if add is not None else 0))

    for tk in [t for t in MM_K_TILES if k_align % t == 0]:
        fits = [(tm_ * tn_, tm_, tn_) for tm_ in MM_TILES if m % tm_ == 0 for tn_ in MM_TILES
                if n % tn_ == 0 and n_align % tn_ == 0 and vmem(tm_, tn_, tk) <= MM_VMEM_BUDGET]
        if fits and (max(fits)[0] >= min(MM_MIN_OUT_TILE, m * n) or tk == MM_K_TILES[-1]):
            break
    _, tm, tn = max(fits)
    nk = k // tk
    dims = (((0,) if ta else (1,)), ((1,) if tb else (0,)))
    n_extra = (add is not None) + (into is not None)

    def body(*refs):
        a_ref, b_ref = refs[0], refs[1]
        add_ref = refs[2] if add is not None else None
        o_ref = refs[2 + n_extra]
        part = lax.dot_general(a_ref[...].astype(MXU_DTYPE), b_ref[...].astype(MXU_DTYPE), (dims, ((), ())),
                               preferred_element_type=F32)

        def finish(r):
            if add_ref is not None:
                r = r + add_ref[...].astype(F32)
            o_ref[...] = r.astype(o_ref.dtype)

        if nk == 1:
            finish(part)
            return
        acc = refs[-1]
        kk = pl.program_id(2)

        @pl.when(kk == 0)
        def _():
            acc[...] = part

        @pl.when(kk > 0)
        def _():
            acc[...] += part

        @pl.when(kk == nk - 1)
        def _():
            finish(acc[...])

    a_spec = pl.BlockSpec((tk, tm), lambda i, j, q: (q, i)) if ta else pl.BlockSpec((tm, tk), lambda i, j, q: (i, q))
    ko = b_koff // tk
    if b_layer is None:
        b_spec = (pl.BlockSpec((tn, tk), lambda i, j, q: (j, q + ko)) if tb
                  else pl.BlockSpec((tk, tn), lambda i, j, q: (q + ko, j)))
    else:
        b_spec = (pl.BlockSpec((None, tn, tk), lambda i, j, q: (b_layer, j, q + ko)) if tb
                  else pl.BlockSpec((None, tk, tn), lambda i, j, q: (b_layer, q + ko, j)))
    o_spec = pl.BlockSpec((tm, tn), lambda i, j, q: (i, j))
    ins, specs = [a, b], [a_spec, b_spec]
    if add is not None:
        ins.append(add)
        specs.append(o_spec)
    aliases = {}
    out_shape = jax.ShapeDtypeStruct((m, n), out_dtype)
    if into is not None:
        buf, layer, col0 = into
        assert buf.shape[1] == m and buf.dtype == out_dtype
        co = col0 // tn
        ins.append(buf)
        specs.append(_ANY)
        aliases = {len(ins) - 1: 0}
        o_spec = pl.BlockSpec((None, tm, tn), lambda i, j, q: (layer, i, j + co))
        out_shape = jax.ShapeDtypeStruct(buf.shape, buf.dtype)
    return pl.pallas_call(
        body, name=f"mm_{m}x{k}x{n}_{int(ta)}{int(tb)}{int(add is not None)}{int(b_layer is not None)}{int(into is not None)}",
        grid=(m // tm, n // tn, nk), in_specs=specs, out_specs=o_spec, out_shape=out_shape,
        input_output_aliases=aliases,
        scratch_shapes=[pltpu.VMEM((tm, tn), F32)] if nk > 1 else [],
        compiler_params=_cparams(("parallel", "parallel", "arbitrary")),
    )(*ins)


def _row_spec(a, block):
    return pl.BlockSpec((block, a.shape[1]), lambda i: (i, 0))


def _whole_spec(a):
    return pl.BlockSpec(a.shape, lambda i: (0,) * a.ndim)


def tmap(name, fn, rows, params, outs, block=ROW_BLOCK):
    s = rows[0].shape[0]
    block = min(block, s)
    nr, npar = len(rows), len(params)

    def body(*refs):
        res = fn(*[_f32(r[...]) for r in refs[:nr]], *[_f32(p[...]) for p in refs[nr:nr + npar]])
        for o_ref, v in zip(refs[nr + npar:], res, strict=True):
            o_ref[...] = v.astype(o_ref.dtype)

    return pl.pallas_call(
        body, name=name, grid=(s // block,),
        in_specs=[_row_spec(a, block) for a in rows] + [_whole_spec(p) for p in params],
        out_specs=[pl.BlockSpec((block, w), lambda i: (i, 0)) for w, _ in outs],
        out_shape=[jax.ShapeDtypeStruct((s, w), dt) for w, dt in outs],
        compiler_params=_cparams(("parallel",)),
    )(*rows, *params)


def tmap_bwd(name, fn, rows, params, douts, row_grad, row_add=None, grad_dtype=None, block=ROW_BLOCK):
    s = rows[0].shape[0]
    block = min(block, s)
    grad_dtype = grad_dtype or MXU_DTYPE
    nr, npar, nd = len(rows), len(params), len(douts)
    gr = [i for i in range(nr) if row_grad[i]]
    row_add = row_add or {}
    adds = [row_add[i] for i in gr if i in row_add]

    def body(*refs):
        rv = [_f32(r[...]) for r in refs[:nr]]
        pv = [_f32(p[...]) for p in refs[nr:nr + npar]]
        dv = tuple(_f32(d[...]) for d in refs[nr + npar:nr + npar + nd])
        add_refs = list(refs[nr + npar + nd:nr + npar + nd + len(adds)])
        out_refs = refs[nr + npar + nd + len(adds):]

        def f(*diff):
            rr = list(rv)
            for n_, i_ in enumerate(gr):
                rr[i_] = diff[n_]
            return tuple(fn(*rr, *diff[len(gr):]))

        _, vjp = jax.vjp(f, *[rv[i_] for i_ in gr], *pv)
        g = vjp(dv)
        for n_, i_ in enumerate(gr):
            v = g[n_]
            if i_ in row_add:
                v = v + add_refs.pop(0)[...].astype(F32)
            out_refs[n_][...] = v.astype(out_refs[n_].dtype)
        first = pl.program_id(0) == 0
        for n_ in range(npar):
            ref = out_refs[len(gr) + n_]

            @pl.when(first)
            def _(ref=ref):
                ref[...] = jnp.zeros_like(ref)

            ref[...] += g[len(gr) + n_]

    res = pl.pallas_call(
        body, name=name, grid=(s // block,),
        in_specs=[_row_spec(a, block) for a in rows] + [_whole_spec(p) for p in params]
        + [_row_spec(d, block) for d in douts] + [_row_spec(a, block) for a in adds],
        out_specs=[_row_spec(rows[i], block) for i in gr] + [_whole_spec(p) for p in params],
        out_shape=[jax.ShapeDtypeStruct(rows[i].shape, F32 if i in row_add else grad_dtype) for i in gr]
        + [jax.ShapeDtypeStruct(p.shape, F32) for p in params],
        compiler_params=_cparams(("arbitrary",)),
    )(*rows, *params, *douts, *adds)
    return list(res[:len(gr)]), list(res[len(gr):])


def rscan(name, fn, state_shapes, rows, params, outs, block):
    s = rows[0].shape[0]
    nsteps = s // block
    nr, npar, no, ns = len(rows), len(params), len(outs), len(state_shapes)

    def body(*refs):
        out_refs = refs[nr + npar:nr + npar + no]
        sav_refs = refs[nr + npar + no:nr + npar + no + ns]
        st_refs = refs[nr + npar + no + ns:]

        @pl.when(pl.program_id(0) == 0)
        def _():
            for st in st_refs:
                st[...] = jnp.zeros_like(st)

        sts = tuple(st[...] for st in st_refs)
        for sv, v in zip(sav_refs, sts):
            sv[...] = v
        new, res = fn(sts, *[_f32(r[...]) for r in refs[:nr]], *[_f32(p[...]) for p in refs[nr:nr + npar]])
        for st, v in zip(st_refs, new, strict=True):
            st[...] = v
        for o_ref, v in zip(out_refs, res, strict=True):
            o_ref[...] = v.astype(o_ref.dtype)

    res = pl.pallas_call(
        body, name=name, grid=(nsteps,),
        in_specs=[_row_spec(a, block) for a in rows] + [_whole_spec(p) for p in params],
        out_specs=[pl.BlockSpec((block, w), lambda i: (i, 0)) for w, _ in outs]
        + [pl.BlockSpec(sh, lambda i: (i, 0)) for sh in state_shapes],
        out_shape=[jax.ShapeDtypeStruct((s, w), dt) for w, dt in outs]
        + [jax.ShapeDtypeStruct((nsteps * sh[0], sh[1]), F32) for sh in state_shapes],
        scratch_shapes=[pltpu.VMEM(sh, F32) for sh in state_shapes],
        compiler_params=_cparams(("arbitrary",)),
    )(*rows, *params)
    return list(res[:no]), list(res[no:])


def rscan_bwd(name, fn, saved, rows, params, douts, block, grad_dtype=None):
    s = rows[0].shape[0]
    nsteps = s // block
    grad_dtype = grad_dtype or MXU_DTYPE
    nr, npar, nd, ns = len(rows), len(params), len(douts), len(saved)
    state_shapes = [(sv.shape[0] // nsteps, sv.shape[1]) for sv in saved]

    def body(*refs):
        rv = [_f32(r[...]) for r in refs[:nr]]
        pv = [_f32(p[...]) for p in refs[nr:nr + npar]]
        dv = tuple(_f32(d[...]) for d in refs[nr + npar:nr + npar + nd])
        sv = tuple(x[...] for x in refs[nr + npar + nd:nr + npar + nd + ns])
        out_refs = refs[nr + npar + nd + ns:nr + npar + nd + ns + nr + npar]
        dst_refs = refs[nr + npar + nd + ns + nr + npar:]
        first = pl.program_id(0) == 0

        @pl.when(first)
        def _():
            for d in dst_refs:
                d[...] = jnp.zeros_like(d)

        def f(sts, *args):
            return fn(sts, *args)

        _, vjp = jax.vjp(f, sv, *rv, *pv)
        g = vjp((tuple(d[...] for d in dst_refs), dv))
        for d, v in zip(dst_refs, g[0], strict=True):
            d[...] = v
        for n_ in range(nr):
            out_refs[n_][...] = g[1 + n_].astype(out_refs[n_].dtype)
        for n_ in range(npar):
            ref = out_refs[nr + n_]

            @pl.when(first)
            def _(ref=ref):
                ref[...] = jnp.zeros_like(ref)

            ref[...] += g[1 + nr + n_]

    rev = lambda i: (nsteps - 1 - i, 0)
    res = pl.pallas_call(
        body, name=name, grid=(nsteps,),
        in_specs=[pl.BlockSpec((block, a.shape[1]), rev) for a in rows] + [_whole_spec(p) for p in params]
        + [pl.BlockSpec((block, d.shape[1]), rev) for d in douts] + [pl.BlockSpec(sh, rev) for sh in state_shapes],
        out_specs=[pl.BlockSpec((block, a.shape[1]), rev) for a in rows] + [_whole_spec(p) for p in params],
        out_shape=[jax.ShapeDtypeStruct(a.shape, grad_dtype) for a in rows]
        + [jax.ShapeDtypeStruct(p.shape, F32) for p in params],
        scratch_shapes=[pltpu.VMEM(sh, F32) for sh in state_shapes],
        compiler_params=_cparams(("arbitrary",)),
    )(*rows, *params, *douts, *saved)
    return list(res[:nr]), list(res[nr:])


def _norm_stage(x, g):
    return (_rms(x, g),)


def _tril():
    r = lax.broadcasted_iota(jnp.int32, (CHUNK, CHUNK), 0)
    c = lax.broadcasted_iota(jnp.int32, (CHUNK, CHUNK), 1)
    return r >= c


def _gla_chunk(st, q, k, v, la, b):
    tril = _tril()
    rowi = lax.broadcasted_iota(jnp.int32, (CHUNK, 1), 0)
    b_last = jnp.sum(la, axis=0, keepdims=True)
    b_ref = jnp.sum(jnp.where(rowi < CHUNK // 2, la, 0.0), axis=0, keepdims=True)
    att = mm.nt(q * jnp.exp(b - b_ref), k * jnp.exp(b_ref - b))
    att = jnp.where(tril, att, 0.0)
    o = mm.nn(att, v) + mm.nn(q * jnp.exp(b), st)
    decay = jnp.exp(jnp.broadcast_to(b_last, (LANE, LANE)).T)
    decay = jnp.concatenate([decay] * (v.shape[1] // LANE), axis=1)
    st2 = decay * st + mm.tn(k * jnp.exp(b_last - b), v)
    return st2, o


def _gla_step(heads, vp, scale):
    kp = LANE

    def fn(states, q, k, v, la):
        sts = list(states)
        trif = _tril().astype(F32)
        rows = []
        for c in range(q.shape[0] // CHUNK):
            r = slice(c * CHUNK, (c + 1) * CHUNK)
            b_all = hi.nn(trif, la[r])
            oh = []
            for h in range(heads):
                ks, vs = slice(h * kp, (h + 1) * kp), slice(h * vp, (h + 1) * vp)
                qh = q[r, ks] * scale if scale != 1.0 else q[r, ks]
                sts[h], o = _gla_chunk(sts[h], qh, k[r, ks], v[r, vs], la[r, ks], b_all[:, ks])
                oh.append(o)
            rows.append(jnp.concatenate(oh, axis=1))
        return tuple(sts), (jnp.concatenate(rows, axis=0),)

    return fn


def _ssd_step(states, xa, dtr, dtb, alog, dsk):
    sts = list(states)
    trif = _tril().astype(F32)
    wide = lax.broadcasted_iota(jnp.int32, (CHUNK, LANE), 0) >= lax.broadcasted_iota(jnp.int32, (CHUNK, LANE), 1)
    hg = SSM_HEADS // SSM_GROUPS
    xw = SSM_HEADS * LANE
    lane, head = lax.broadcasted_iota(jnp.int32, (LANE, xw), 1), lax.broadcasted_iota(jnp.int32, (LANE, xw), 0)
    spread = ((lane >= head * LANE) & (lane < (head + 1) * LANE)).astype(F32)
    neg_a = -jnp.exp(alog)
    pad = jnp.zeros((CHUNK, LANE), F32)
    rows = []
    for c in range(xa.shape[0] // CHUNK):
        r = slice(c * CHUNK, (c + 1) * CHUNK)
        dt_all = _softplus(hi.nn(dtr[r], spread) + dtb)
        a_all = dt_all * neg_a
        acs_all = hi.nn(trif, a_all)
        last_all = jnp.sum(a_all, axis=0, keepdims=True)
        yh = []
        for g in range(SSM_GROUPS):
            bm = xa[r, xw + g * LANE:xw + (g + 1) * LANE]
            cm = xa[r, xw + (SSM_GROUPS + g) * LANE:xw + (SSM_GROUPS + g + 1) * LANE]
            cb = mm.nt(cm, jnp.concatenate([bm, pad], axis=0))
            for hh in range(hg):
                h = g * hg + hh
                ls = slice(h * LANE, (h + 1) * LANE)
                xs, acs, acs_last = xa[r, ls], acs_all[:, ls], last_all[:, ls]
                xdt = xs * dt_all[:, ls]
                seg = acs - jnp.concatenate([acs, pad], axis=0).T[:CHUNK]
                lmat = jnp.exp(jnp.where(wide, seg, -1e30))
                y = (mm.nn(cb * lmat, jnp.concatenate([xdt, pad], axis=0)) + mm.nn(cm, sts[h]) * jnp.exp(acs)
                     + dsk[:, ls] * xs)
                sts[h] = jnp.exp(acs_last) * sts[h] + mm.tn(bm, xdt * jnp.exp(acs_last - acs))
                yh.append(y)
        rows.append(jnp.concatenate(yh, axis=1))
    return tuple(sts), (jnp.concatenate(rows, axis=0),)


def _gla_pre(glr, w2, bg):
    z = mm.nn(glr, w2) + bg
    return (-_softplus(-z) * (1.0 / GLA_GATE_NORM),)


def _gla_post(o, og, g):
    w = 2 * LANE
    return (jnp.concatenate([_rms(o[:, h * w:(h + 1) * w], g, GLA_DV) * _silu(og[:, h * w:(h + 1) * w])
                             for h in range(GLA_HEADS)], axis=1),)


def _hgrn_pre(q, f, lbnd):
    e = jnp.exp(lbnd - jnp.max(lbnd, axis=0, keepdims=True))
    rowi = lax.broadcasted_iota(jnp.int32, e.shape, 0)
    lb = jnp.sum(jnp.where(rowi >= 1, e, 0.0), axis=0, keepdims=True) / jnp.sum(e, axis=0, keepdims=True)
    fg = lb + (1.0 - lb) * _sigmoid(f)
    return _silu(q), 1.0 - fg, jnp.log(fg)


def _hgrn_post(o, og, g):
    return (jnp.concatenate([_rms(o[:, h * LANE:(h + 1) * LANE], g) for h in range(HGRN_HEADS)], axis=1)
            * _sigmoid(og),)


def _mamba_post(y, z, g):
    v = y * _silu(z)
    w = (SSM_HEADS // SSM_GROUPS) * LANE
    n_real = (SSM_HEADS // SSM_GROUPS) * SSM_HD
    return (jnp.concatenate([_rms(v[:, i * w:(i + 1) * w], g[:, i * w:(i + 1) * w], n_real)
                             for i in range(SSM_GROUPS)], axis=1),)


def _dil_pre(q, k, cosf, sinf, qg, kg):
    def groups(x, g):
        out = []
        for grp in range(len(DIL_GROUPS)):
            hs = []
            for h in range(grp * DIL_HEADS, (grp + 1) * DIL_HEADS):
                n = _rms(x[:, h * LANE:(h + 1) * LANE], g)
                hs.append(n * cosf + _swap_halves(n) * sinf)
            out.append(jnp.concatenate(hs, axis=1))
        return out

    return (*groups(q, qg), *groups(k, kg))


def _dil_merge(o0, o1, o2, l0, l1, l2):
    m = jnp.maximum(jnp.maximum(l0, l1), l2)
    e0, e1, e2 = jnp.exp(l0 - m), jnp.exp(l1 - m), jnp.exp(l2 - m)
    return ((e0 * o0 + e1 * o1 + e2 * o2) / (e0 + e1 + e2),)


def _dil_block(q, kp, kc, vp, vc, lim):
    kk = jnp.concatenate([kp, kc], axis=0)
    vv = jnp.concatenate([vp, vc], axis=0)
    s = mm.nt(q, kk) * (DIL_HD ** -0.5)
    i = lax.broadcasted_iota(jnp.int32, s.shape, 0)
    j = lax.broadcasted_iota(jnp.int32, s.shape, 1)
    dist = DIL_BLOCK + i - j
    s = jnp.where((dist >= 0) & (dist <= DIL_BLOCK) & (j >= lim), s, -1e30)
    m = jnp.max(s, axis=-1, keepdims=True)
    p = jnp.exp(s - m)
    l = jnp.sum(p, axis=-1, keepdims=True)
    return mm.nn(p / l, vv), jnp.broadcast_to(m + jnp.log(l), (q.shape[0], LANE))


def _xattn(xq, kv, qg, kg):
    w = XA_HEADS * LANE
    os_ = []
    for h in range(XA_HEADS):
        ls = slice(h * LANE, (h + 1) * LANE)
        q = _rms(xq[:, ls], qg, XA_HD)
        k = _rms(kv[:, ls], kg, XA_HD)
        s = mm.nt(q, k) * (XA_HD ** -0.5)
        p = jnp.exp(s - jnp.max(s, axis=-1, keepdims=True))
        p = p / jnp.sum(p, axis=-1, keepdims=True)
        os_.append(mm.nn(p, kv[:, w + h * LANE:w + (h + 1) * LANE]))
    return (jnp.concatenate(os_, axis=1),)


def _dil_geometry(s, w, r, g, v_cols):
    hb = DIL_HEADS if r == 1 else 1
    rb = DIL_BLOCK * r
    nb = s // rb
    bw = hb * LANE
    v_col0 = g * (w // bw)
    assert v_cols % bw == 0 and s % rb == 0
    return hb, rb, nb, bw, v_col0


def _sub(r, res):
    return pl.ds(res, DIL_BLOCK, stride=r) if r > 1 else slice(None)


def dil_attn(name, q, k, v, r, g):
    s, w = q.shape
    hb, rb, nb, bw, v_col0 = _dil_geometry(s, w, r, g, v.shape[1])

    def body(q_r, kp_r, kc_r, vp_r, vc_r, o_r, l_r):
        lim = jnp.where(pl.program_id(1) == 0, DIL_BLOCK, 0)
        for res in range(r):
            rows = _sub(r, res)
            for h in range(hb):
                ls = slice(h * LANE, (h + 1) * LANE)
                o, lse = _dil_block(q_r[rows, ls], kp_r[rows, ls], kc_r[rows, ls], vp_r[rows, ls], vc_r[rows, ls], lim)
                o_r[rows, ls] = o
                l_r[rows, ls] = lse

    cur = pl.BlockSpec((rb, bw), lambda hblk, n: (n, hblk))
    prev = pl.BlockSpec((rb, bw), lambda hblk, n: (jnp.maximum(n - 1, 0), hblk))
    vcur = pl.BlockSpec((rb, bw), lambda hblk, n: (n, v_col0 + hblk))
    vprev = pl.BlockSpec((rb, bw), lambda hblk, n: (jnp.maximum(n - 1, 0), v_col0 + hblk))
    return pl.pallas_call(
        body, name=name, grid=(w // bw, nb), in_specs=[cur, prev, cur, vprev, vcur], out_specs=[cur, cur],
        out_shape=[jax.ShapeDtypeStruct((s, w), F32)] * 2,
        compiler_params=_cparams(("parallel", "parallel")),
    )(q, k, k, v, v)


def dil_attn_bwd(name, q, k, v, do, dlse, r, g):
    s, w = q.shape
    hb, rb, nb, bw, v_col0 = _dil_geometry(s, w, r, g, v.shape[1])

    def body(q_r, kp_r, kc_r, vp_r, vc_r, do_r, dl_r, dq_r, dk_r, dv_r, ck, cv):
        i = pl.program_id(1)
        lim = jnp.where(i == nb - 1, DIL_BLOCK, 0)

        @pl.when(i == 0)
        def _():
            ck[...] = jnp.zeros_like(ck)
            cv[...] = jnp.zeros_like(cv)

        for res in range(r):
            rows = _sub(r, res)
            for h in range(hb):
                ls = slice(h * LANE, (h + 1) * LANE)
                _, vjp = jax.vjp(functools.partial(_dil_block, lim=lim),
                                 q_r[rows, ls], kp_r[rows, ls], kc_r[rows, ls], vp_r[rows, ls], vc_r[rows, ls])
                gq, gkp, gkc, gvp, gvc = vjp((do_r[rows, ls], dl_r[rows, ls]))
                dq_r[rows, ls] = gq
                dk_r[rows, ls] = gkc + ck[rows, ls]
                dv_r[rows, ls] = gvc + cv[rows, ls]
                ck[rows, ls] = gkp
                cv[rows, ls] = gvp

    cur = pl.BlockSpec((rb, bw), lambda hblk, i: (nb - 1 - i, hblk))
    prev = pl.BlockSpec((rb, bw), lambda hblk, i: (jnp.maximum(nb - 2 - i, 0), hblk))
    vcur = pl.BlockSpec((rb, bw), lambda hblk, i: (nb - 1 - i, v_col0 + hblk))
    vprev = pl.BlockSpec((rb, bw), lambda hblk, i: (jnp.maximum(nb - 2 - i, 0), v_col0 + hblk))
    return pl.pallas_call(
        body, name=name, grid=(w // bw, nb), in_specs=[cur, prev, cur, vprev, vcur, cur, cur],
        out_specs=[cur, cur, cur], out_shape=[jax.ShapeDtypeStruct((s, w), F32)] * 3,
        scratch_shapes=[pltpu.VMEM((rb, bw), F32)] * 2,
        compiler_params=_cparams(("parallel", "arbitrary")),
    )(q, k, k, v, v, do, dlse)


def _dsilu(u):
    sg = _sigmoid(u)
    return sg * (1.0 + u * (1.0 - sg))


CONV_STRIP = 16


def _shifted_rows(prev8, cur_r, next8, lanes, s0, n, sh, block):
    if s0 - sh < 0:
        assert s0 == 0
        xp = jnp.concatenate([prev8, cur_r[0:n, lanes]], axis=0)
        return pltpu.roll(xp, sh, 0)[SUBLANE:SUBLANE + n]
    if s0 - sh + n > block:
        assert s0 == block and n == SUBLANE
        xp = jnp.concatenate([cur_r[block - SUBLANE:block, lanes], next8], axis=0)
        return (pltpu.roll(xp, sh, 0) if sh else xp)[SUBLANE:]
    return cur_r[pl.ds(s0 - sh, n), lanes]


def conv_fwd(name, x, w, b, mode, out_dtype, tc, block=ROW_BLOCK):
    s, c = x.shape
    ntap = w.shape[0]
    block = min(block, s)
    f = c // 2 if mode == 'glu' else c
    nh = 2 if mode == 'glu' else 1
    off = f // tc

    def body(*refs):
        first = pl.program_id(1) == 0
        o_ref = refs[-1]

        def column(cidx, carry):
            lanes = pl.ds(pl.multiple_of(cidx * LANE, LANE), LANE)
            prevs = [jnp.where(first, 0.0, refs[4 * hlf][:, lanes]) for hlf in range(nh)]
            for s0 in range(0, block, CONV_STRIP):
                us = []
                for hlf in range(nh):
                    _, cur_r, w_r, b_r = refs[4 * hlf:4 * hlf + 4]
                    acc = b_r[:, lanes]
                    for j in range(ntap):
                        xs = _shifted_rows(prevs[hlf], cur_r, None, lanes, s0, CONV_STRIP, ntap - 1 - j, block)
                        acc = acc + w_r[j:j + 1, lanes] * xs
                    us.append(acc)
                res = _silu(us[0]) * us[1] if mode == 'glu' else _silu(us[0])
                o_ref[pl.ds(s0, CONV_STRIP), lanes] = res.astype(o_ref.dtype)
            return carry

        lax.fori_loop(0, tc // LANE, column, 0)

    rb = block // SUBLANE
    ins, specs = [], []
    for hlf in range(nh):
        o = hlf * off
        ins += [x, x, w, b]
        specs += [pl.BlockSpec((SUBLANE, tc), lambda j, i, o=o: (jnp.maximum(i * rb - 1, 0), j + o)),
                  pl.BlockSpec((block, tc), lambda j, i, o=o: (i, j + o)),
                  pl.BlockSpec((ntap, tc), lambda j, i, o=o: (0, j + o)),
                  pl.BlockSpec((1, tc), lambda j, i, o=o: (0, j + o))]
    return pl.pallas_call(
        body, name=name, grid=(f // tc, s // block), in_specs=specs,
        out_specs=pl.BlockSpec((block, tc), lambda j, i: (i, j)),
        out_shape=jax.ShapeDtypeStruct((s, f), out_dtype),
        compiler_params=_cparams(("parallel", "parallel")),
    )(*ins)


def conv_bwd(name, x, w, b, dout, mode, tc, block=ROW_BLOCK):
    s, c = x.shape
    ntap = w.shape[0]
    block = min(block, s)
    nblk = s // block
    f = c // 2 if mode == 'glu' else c
    nh = 2 if mode == 'glu' else 1
    off = f // tc
    ext = block + SUBLANE

    def body(*refs):
        i = pl.program_id(1)
        first, last = i == 0, i == nblk - 1
        dcur_r, dnext_r = refs[5 * nh], refs[5 * nh + 1]
        outs = refs[5 * nh + 2:5 * nh + 2 + 3 * nh]
        du_scr = refs[5 * nh + 2 + 3 * nh:]

        @pl.when(first)
        def _():
            for hlf in range(nh):
                outs[3 * hlf + 1][...] = jnp.zeros_like(outs[3 * hlf + 1])
                outs[3 * hlf + 2][...] = jnp.zeros_like(outs[3 * hlf + 2])

        def column(cidx, carry):
            lanes = pl.ds(pl.multiple_of(cidx * LANE, LANE), LANE)
            prevs = [jnp.where(first, 0.0, refs[5 * hlf][:, lanes]) for hlf in range(nh)]
            nexts = [jnp.where(last, 0.0, refs[5 * hlf + 2][:, lanes]) for hlf in range(nh)]
            db_acc = [jnp.zeros((CONV_STRIP, LANE), F32) for _ in range(nh)]
            dw_acc = [[jnp.zeros((CONV_STRIP, LANE), F32) for _ in range(ntap)] for _ in range(nh)]
            for s0 in range(0, ext, CONV_STRIP):
                n = min(CONV_STRIP, ext - s0)
                d_e = dcur_r[pl.ds(s0, n), lanes] if s0 < block else jnp.where(last, 0.0, dnext_r[:, lanes])
                xs, us = [], []
                for hlf in range(nh):
                    cur_r, w_r, b_r = refs[5 * hlf + 1], refs[5 * hlf + 3], refs[5 * hlf + 4]
                    sh_rows = [_shifted_rows(prevs[hlf], cur_r, nexts[hlf], lanes, s0, n, ntap - 1 - j, block)
                               for j in range(ntap)]
                    acc = b_r[:, lanes]
                    for j in range(ntap):
                        acc = acc + w_r[j:j + 1, lanes] * sh_rows[j]
                    xs.append(sh_rows)
                    us.append(acc)
                dus = [d_e * us[1] * _dsilu(us[0]), d_e * _silu(us[0])] if mode == 'glu' else [d_e * _dsilu(us[0])]
                for hlf in range(nh):
                    du_scr[hlf][pl.ds(s0, n), lanes] = dus[hlf]
                    if s0 < block:
                        db_acc[hlf] = db_acc[hlf] + dus[hlf]
                        for j in range(ntap):
                            dw_acc[hlf][j] = dw_acc[hlf][j] + dus[hlf] * xs[hlf][j]
            for hlf in range(nh):
                w_r = refs[5 * hlf + 3]
                dx_r, dw_r, db_r = outs[3 * hlf:3 * hlf + 3]
                db_r[:, lanes] += jnp.sum(db_acc[hlf], axis=0, keepdims=True)
                for j in range(ntap):
                    dw_r[j:j + 1, lanes] += jnp.sum(dw_acc[hlf][j], axis=0, keepdims=True)
                for s0 in range(0, block, CONV_STRIP):
                    dx = None
                    for j in range(ntap):
                        term = w_r[j:j + 1, lanes] * du_scr[hlf][pl.ds(s0 + ntap - 1 - j, CONV_STRIP), lanes]
                        dx = term if dx is None else dx + term
                    dx_r[pl.ds(s0, CONV_STRIP), lanes] = dx.astype(dx_r.dtype)
            return carry

        lax.fori_loop(0, tc // LANE, column, 0)

    rb = block // SUBLANE
    nrow8 = s // SUBLANE
    ins, specs = [], []
    for hlf in range(nh):
        o = hlf * off
        ins += [x, x, x, w, b]
        specs += [pl.BlockSpec((SUBLANE, tc), lambda j, i, o=o: (jnp.maximum(i * rb - 1, 0), j + o)),
                  pl.BlockSpec((block, tc), lambda j, i, o=o: (i, j + o)),
                  pl.BlockSpec((SUBLANE, tc), lambda j, i, o=o: (jnp.minimum((i + 1) * rb, nrow8 - 1), j + o)),
                  pl.BlockSpec((ntap, tc), lambda j, i, o=o: (0, j + o)),
                  pl.BlockSpec((1, tc), lambda j, i, o=o: (0, j + o))]
    ins += [dout, dout]
    specs += [pl.BlockSpec((block, tc), lambda j, i: (i, j)),
              pl.BlockSpec((SUBLANE, tc), lambda j, i: (jnp.minimum((i + 1) * rb, nrow8 - 1), j))]
    out_specs, out_shape = [], []
    for hlf in range(nh):
        out_specs += [pl.BlockSpec((block, tc), lambda j, i: (i, j)), pl.BlockSpec((ntap, tc), lambda j, i: (0, j)),
                      pl.BlockSpec((1, tc), lambda j, i: (0, j))]
        out_shape += [jax.ShapeDtypeStruct((s, f), MXU_DTYPE), jax.ShapeDtypeStruct((ntap, f), F32),
                      jax.ShapeDtypeStruct((1, f), F32)]
    res = pl.pallas_call(
        body, name=name, grid=(f // tc, nblk), in_specs=specs, out_specs=out_specs, out_shape=out_shape,
        scratch_shapes=[pltpu.VMEM((ext, tc), F32)] * nh,
        compiler_params=_cparams(("parallel", "arbitrary")),
    )(*ins)
    if nh == 1:
        return [res[0]], res[1], res[2]
    return [res[0], res[3]], jnp.concatenate([res[1], res[4]], axis=1), jnp.concatenate([res[2], res[5]], axis=1)


def loss_head(y, target, block=ROW_BLOCK):
    s, d = y.shape
    block = min(block, s)

    def body(y_r, t_r, acc_r, dy_r):
        e = y_r[...] - t_r[...]
        dy_r[...] = e * (1.0 / d)

        @pl.when(pl.program_id(0) == 0)
        def _():
            acc_r[...] = jnp.zeros_like(acc_r)

        acc_r[...] += jnp.sum((e * e).reshape(block // SUBLANE, SUBLANE, d), axis=0) * (0.5 / d)

    return pl.pallas_call(
        body, name="loss_head", grid=(s // block,),
        in_specs=[pl.BlockSpec((block, d), lambda i: (i, 0))] * 2,
        out_specs=[pl.BlockSpec((SUBLANE, d), lambda i: (0, 0)), pl.BlockSpec((block, d), lambda i: (i, 0))],
        out_shape=[jax.ShapeDtypeStruct((SUBLANE, d), F32), jax.ShapeDtypeStruct((s, d), F32)],
        compiler_params=_cparams(("arbitrary",)),
    )(y, target)


def adamw(name, w, g, m, v):
    r, c = w.shape
    tr = r if r <= 512 else _tile(r, (512, 256, 128, 64, 32, 16, 8))
    if c * tr * 4 > (1 << 21):
        tr = _tile(r, (256, 128, 64, 32, 16, 8))

    def body(w_r, g_r, m_r, v_r, d_r, nm_r, nv_r):
        gg = g_r[...]
        nm = ADAM_B1 * m_r[...] + (1.0 - ADAM_B1) * gg
        nv = ADAM_B2 * v_r[...] + (1.0 - ADAM_B2) * (gg * gg)
        m_hat = nm / (1.0 - ADAM_B1 ** ADAM_STEP)
        v_hat = nv / (1.0 - ADAM_B2 ** ADAM_STEP)
        d_r[...] = -ADAM_LR * (m_hat / (jnp.sqrt(v_hat) + ADAM_EPS) + ADAM_WD * w_r[...])
        nm_r[...] = nm
        nv_r[...] = nv

    spec = pl.BlockSpec((tr, c), lambda i: (i, 0))
    return pl.pallas_call(
        body, name=name, grid=(r // tr,), in_specs=[spec] * 4, out_specs=[spec] * 3,
        out_shape=[jax.ShapeDtypeStruct((r, c), F32)] * 3, compiler_params=_cparams(("parallel",)),
    )(w, g, m, v)


MESH = pl.DeviceIdType.MESH
_ANY = pl.BlockSpec(memory_space=pl.ANY)


def _place():
    return lax.axis_index("x"), lax.axis_index("y"), lax.axis_index("c")


class Packed:
    def __init__(self, shard_shape):
        self.r, self.c = shard_shape
        self.h = self.r // 2
        self.whole = (N_CHIPS, self.r, self.c)
        self.got = (N_CHIPS, self.h, self.c)
        self.slab_half = (self.h, self.c)

    def shard_half(self, ref, core):
        return ref.at[pl.ds(core * self.h, self.h)]

    def whole_half(self, ref, chip, core):
        return ref.at[chip, pl.ds(core * self.h, self.h)]

    def place(self, whole, shard, chip):
        return lax.dynamic_update_slice(whole, shard[None], (chip, 0, 0))

    def grad_half(self, ref, core):
        return ref.at[:, core]

    def pair_slab(self, ref, chip):
        return ref.at[chip]


class Layered:
    def __init__(self, shard_shape, axis):
        self.axis, self.size, self.h = axis, shard_shape[axis], shard_shape[0] // 2
        self.whole = tuple(d * N_CHIPS if i == axis else d for i, d in enumerate(shard_shape))
        self.got = (self.h,) + self.whole[1:]
        self.slab_half = (self.h,) + tuple(shard_shape[1:])

    def _window(self, first, chip):
        idx = [first, slice(None), slice(None)]
        idx[self.axis] = pl.ds(pl.multiple_of(chip * self.size, SUBLANE if self.axis == 1 else LANE), self.size)
        return tuple(idx)

    def shard_half(self, ref, core):
        return ref.at[pl.ds(core * self.h, self.h)]

    def whole_half(self, ref, chip, core):
        return ref.at[self._window(pl.ds(core * self.h, self.h), chip)]

    def place(self, whole, shard, chip):
        return lax.dynamic_update_slice_in_dim(whole, shard, chip * self.size, self.axis)

    def grad_half(self, ref, core):
        return ref.at[pl.ds(core * self.h, self.h)]

    def pair_slab(self, ref, chip):
        return ref.at[self._window(slice(None), chip)]


def allgather_chips(shards, kinds):
    n = len(shards)

    def body(*refs):
        w_refs, out_refs, send_sems, recv_sems = refs[:n], refs[n:2 * n], refs[2 * n], refs[2 * n + 1]
        x, y, cc = _place()
        sibling = (x, y, 1 - cc)
        chips = [(1 - x, y), (x, 1 - y), (1 - x, 1 - y)]

        def copy(t, k, chip, core, to, src=None):
            dst = kinds[t].whole_half(out_refs[t], 2 * chip[0] + chip[1], core)
            return pltpu.make_async_remote_copy(
                src_ref=dst if src is None else src, dst_ref=dst, send_sem=send_sems.at[6 * t + k],
                recv_sem=recv_sems.at[6 * t + k], device_id=to, device_id_type=MESH)

        first, passed = [], []
        for t in range(n):
            mine = kinds[t].shard_half(w_refs[t], cc)
            first += [copy(t, j, (x, y), cc, (*chip, cc), src=mine) for j, chip in enumerate(chips)]
        for cp in first:
            cp.start()
        for t in range(n):
            for j, chip in enumerate(chips):
                copy(t, j, chip, cc, (x, y, cc)).wait_recv()
                passed.append(copy(t, 3 + j, chip, cc, sibling))
                passed[-1].start()
        for t in range(n):
            for j, chip in enumerate(chips):
                copy(t, 3 + j, chip, 1 - cc, (x, y, cc)).wait_recv()
        for cp in first + passed:
            cp.wait_send()

    outs = pl.pallas_call(
        body, name="allgather_chips", in_specs=[_ANY] * n, out_specs=[_ANY] * n,
        out_shape=[jax.ShapeDtypeStruct(k.whole, s.dtype) for k, s in zip(kinds, shards)],
        scratch_shapes=[pltpu.SemaphoreType.DMA((6 * n,)), pltpu.SemaphoreType.DMA((6 * n,))],
    )(*shards)
    chip = 2 * lax.axis_index("x") + lax.axis_index("y")
    return [k.place(o, s, chip) for k, o, s in zip(kinds, outs, shards)]


def allgather_devices(buf):
    r, c = buf.shape

    def body(b_ref, out_ref, send_sems, recv_sems, local_sem):
        x, y, cc = _place()
        me = 4 * x + 2 * y + cc
        mine = pltpu.make_async_copy(b_ref, out_ref.at[me], local_sem)
        mine.start()
        copies = []
        for k in range(1, N_DEV):
            px, py, pc = x ^ (k >> 2), y ^ ((k >> 1) & 1), cc ^ (k & 1)
            cp = pltpu.make_async_remote_copy(src_ref=b_ref, dst_ref=out_ref.at[me], send_sem=send_sems.at[k - 1],
                                              recv_sem=recv_sems.at[k - 1], device_id=(px, py, pc), device_id_type=MESH)
            cp.start()
            copies.append((cp, 4 * px + 2 * py + pc))
        for k, (cp, peer) in enumerate(copies):
            pltpu.make_async_remote_copy(src_ref=b_ref, dst_ref=out_ref.at[peer], send_sem=send_sems.at[k],
                                         recv_sem=recv_sems.at[k], device_id=(x, y, cc), device_id_type=MESH).wait_recv()
        for cp, _ in copies:
            cp.wait_send()
        mine.wait()

    return pl.pallas_call(
        body, name="allgather_devices", in_specs=[_ANY], out_specs=_ANY,
        out_shape=jax.ShapeDtypeStruct((N_DEV, r, c), buf.dtype),
        scratch_shapes=[pltpu.SemaphoreType.DMA((N_DEV - 1,)), pltpu.SemaphoreType.DMA((N_DEV - 1,)),
                        pltpu.SemaphoreType.DMA],
    )(buf)


def swap_halves_sibling(gs, kinds):
    n = len(gs)

    def body(*refs):
        g_refs, out_refs, send_sems, recv_sems = refs[:n], refs[n:2 * n], refs[2 * n], refs[2 * n + 1]
        x, y, cc = _place()
        cps = []
        for t in range(n):
            cps.append(pltpu.make_async_remote_copy(
                src_ref=kinds[t].grad_half(g_refs[t], 1 - cc), dst_ref=out_refs[t], send_sem=send_sems.at[t],
                recv_sem=recv_sems.at[t], device_id=(x, y, 1 - cc), device_id_type=MESH))
            cps[-1].start()
        for cp in cps:
            cp.wait()

    return pl.pallas_call(
        body, name="swap_halves_sibling", in_specs=[_ANY] * n, out_specs=[_ANY] * n,
        out_shape=[jax.ShapeDtypeStruct(k.got, g.dtype) for k, g in zip(kinds, gs)],
        scratch_shapes=[pltpu.SemaphoreType.DMA((n,)), pltpu.SemaphoreType.DMA((n,))],
    )(*gs)


def exchange_chips(pairs, kinds):
    n = len(pairs)

    def body(*refs):
        p_refs, out_refs, send_sems, recv_sems = refs[:n], refs[n:2 * n], refs[2 * n], refs[2 * n + 1]
        x, y, cc = _place()
        me = 2 * x + y
        chips = [(1 - x, y), (x, 1 - y), (1 - x, 1 - y)]
        cps = []
        for t in range(n):
            for j, chip in enumerate(chips):
                cp = pltpu.make_async_remote_copy(
                    src_ref=kinds[t].pair_slab(p_refs[t], 2 * chip[0] + chip[1]), dst_ref=out_refs[t].at[me],
                    send_sem=send_sems.at[3 * t + j], recv_sem=recv_sems.at[3 * t + j], device_id=(*chip, cc),
                    device_id_type=MESH)
                cp.start()
                cps.append(cp)
        for t in range(n):
            for j, chip in enumerate(chips):
                pltpu.make_async_remote_copy(
                    src_ref=kinds[t].pair_slab(p_refs[t], me), dst_ref=out_refs[t].at[2 * chip[0] + chip[1]],
                    send_sem=send_sems.at[3 * t + j], recv_sem=recv_sems.at[3 * t + j], device_id=(x, y, cc),
                    device_id_type=MESH).wait_recv()
        for cp in cps:
            cp.wait_send()

    return pl.pallas_call(
        body, name="exchange_chips", in_specs=[_ANY] * n, out_specs=[_ANY] * n,
        out_shape=[jax.ShapeDtypeStruct((N_CHIPS,) + k.slab_half, p.dtype) for k, p in zip(kinds, pairs)],
        scratch_shapes=[pltpu.SemaphoreType.DMA((3 * n,)), pltpu.SemaphoreType.DMA((3 * n,))],
    )(*pairs)


def _row_tile(n, limit=512):
    return max(t for t in range(16, limit + 1, 16) if n % t == 0)


def sum_chips(got, own, kind, chip, name):
    def body(chip_ref, got_r, own_r, out_r):
        mine = own_r[...].astype(F32)
        acc = None
        for k in range(N_CHIPS):
            term = jnp.where(chip_ref[0] == k, mine, got_r[k].astype(F32))
            acc = term if acc is None else acc + term
        out_r[...] = acc

    if isinstance(kind, Packed):
        r, c = kind.slab_half
        tr = _row_tile(r)
        grid = (r // tr,)
        specs = [pl.BlockSpec((N_CHIPS, tr, c), lambda i, chip_ref: (0, i, 0)),
                 pl.BlockSpec((None, tr, c), lambda i, chip_ref: (chip_ref[0], i, 0))]
        out_spec = pl.BlockSpec((tr, c), lambda i, chip_ref: (i, 0))
    else:
        hl, a, b = kind.slab_half
        ta = _row_tile(a, 352)
        grid = (hl, a // ta)
        own_map = ((lambda l, i, chip_ref: (l, i, chip_ref[0])) if kind.axis == 2
                   else (lambda l, i, chip_ref: (l, chip_ref[0] * (a // ta) + i, 0)))
        specs = [pl.BlockSpec((N_CHIPS, None, ta, b), lambda l, i, chip_ref: (0, l, i, 0)),
                 pl.BlockSpec((None, ta, b), own_map)]
        out_spec = pl.BlockSpec((None, ta, b), lambda l, i, chip_ref: (l, i, 0))
    return pl.pallas_call(
        body, name=name,
        grid_spec=pltpu.PrefetchScalarGridSpec(num_scalar_prefetch=1, grid=grid, in_specs=specs, out_specs=out_spec),
        out_shape=jax.ShapeDtypeStruct(kind.slab_half, F32),
        compiler_params=_cparams(("parallel",) * len(grid)),
    )(chip, got, own)


def join_halves_sibling(halves):
    n = len(halves)

    def body(*refs):
        h_refs, out_refs, send_sems, recv_sems = refs[:n], refs[n:2 * n], refs[2 * n], refs[2 * n + 1]
        x, y, cc = _place()
        cps = []
        for t in range(n):
            cps.append(pltpu.make_async_remote_copy(
                src_ref=h_refs[t], dst_ref=out_refs[t].at[cc], send_sem=send_sems.at[t], recv_sem=recv_sems.at[t],
                device_id=(x, y, 1 - cc), device_id_type=MESH))
            cps[-1].start()
        for t in range(n):
            pltpu.make_async_remote_copy(
                src_ref=h_refs[t], dst_ref=out_refs[t].at[1 - cc], send_sem=send_sems.at[t], recv_sem=recv_sems.at[t],
                device_id=(x, y, cc), device_id_type=MESH).wait_recv()
        for cp in cps:
            cp.wait_send()

    outs = pl.pallas_call(
        body, name="join_halves_sibling", in_specs=[_ANY] * n, out_specs=[_ANY] * n,
        out_shape=[jax.ShapeDtypeStruct((2,) + h.shape, h.dtype) for h in halves],
        scratch_shapes=[pltpu.SemaphoreType.DMA((n,)), pltpu.SemaphoreType.DMA((n,))],
    )(*halves)
    core = lax.axis_index("c")
    return [lax.dynamic_update_slice_in_dim(o, h[None], core, 0) for o, h in zip(outs, halves)]


def add_own_half(g, got, kind, core, out_dtype, name):
    def body(c_ref, g_r, o_r, out_r):
        out_r[...] = (g_r[...] + o_r[...]).astype(out_r.dtype)

    if isinstance(kind, Packed):
        r, c = kind.slab_half
        tr = _row_tile(r)
        grid = (N_CHIPS, r // tr)
        specs = [pl.BlockSpec((None, None, tr, c), lambda i, j, c_ref: (i, c_ref[0], j, 0)),
                 pl.BlockSpec((None, tr, c), lambda i, j, c_ref: (i, j, 0))]
        out_spec = pl.BlockSpec((None, tr, c), lambda i, j, c_ref: (i, j, 0))
    else:
        hl, a, b = kind.got
        ta = _row_tile(a, 128)
        grid = (hl, a // ta)
        specs = [pl.BlockSpec((None, ta, b), lambda l, i, c_ref: (c_ref[0] * hl + l, i, 0)),
                 pl.BlockSpec((None, ta, b), lambda l, i, c_ref: (l, i, 0))]
        out_spec = pl.BlockSpec((None, ta, b), lambda l, i, c_ref: (l, i, 0))
    return pl.pallas_call(
        body, name=name,
        grid_spec=pltpu.PrefetchScalarGridSpec(num_scalar_prefetch=1, grid=grid, in_specs=specs, out_specs=out_spec),
        out_shape=jax.ShapeDtypeStruct(kind.got, out_dtype),
        compiler_params=_cparams(("parallel", "parallel")),
    )(core, g, got)


def sum_slabs(p, name):
    n, r, c = p.shape
    tr = _tile(r, [t for t in (512, 256, 128, 64, 32, 16) if n * t * c * p.dtype.itemsize <= (1 << 23)])

    def body(p_r, out_r):
        acc = p_r[0].astype(F32)
        for k in range(1, n):
            acc = acc + p_r[k].astype(F32)
        out_r[...] = acc

    return pl.pallas_call(
        body, name=name, grid=(r // tr,), in_specs=[pl.BlockSpec((n, tr, c), lambda i: (0, i, 0))],
        out_specs=pl.BlockSpec((tr, c), lambda i: (i, 0)), out_shape=jax.ShapeDtypeStruct((r, c), F32),
        compiler_params=_cparams(("parallel",)),
    )(p)


def _lay(arr, axis, pieces, total, reps=()):
    items = [(d, n, lax.slice_in_dim(arr, s0, s0 + n, axis=axis)) for s0, n, d in pieces]
    items += [(d, n, jnp.repeat(lax.slice_in_dim(arr, s0, s0 + 1, axis=axis), n, axis=axis)) for s0, d, n in reps]
    items.sort(key=lambda t: t[0])
    parts, pos = [], 0

    def zeros(n):
        sh = list(arr.shape)
        sh[axis] = n
        return jnp.zeros(sh, arr.dtype)

    for d, n, v in items:
        if d > pos:
            parts.append(zeros(d - pos))
        parts.append(v)
        pos = d + n
    if total > pos:
        parts.append(zeros(total - pos))
    return jnp.concatenate(parts, axis=axis) if len(parts) > 1 else parts[0]


def _unlay_parts(g, axis, pieces, reps=()):
    out = [(s0, lax.slice_in_dim(g, d, d + n, axis=axis)) for s0, n, d in pieces]
    out += [(s0, jnp.sum(lax.slice_in_dim(g, d, d + n, axis=axis), axis=axis, keepdims=True)) for s0, d, n in reps]
    return out


def _join(parts, axis):
    parts = sorted(parts, key=lambda t: t[0])
    return jnp.concatenate([p for _, p in parts], axis=axis)


def _heads(src0, n_heads, width, padded, dst0=0):
    return [(src0 + h * width, width, dst0 + h * padded) for h in range(n_heads)]


_XQ = lambda src0: _heads(src0, XA_HEADS, XA_HD, LANE)
_XA_W = XA_HEADS * LANE

LAYOUT = {
    'a': dict(
        segs=dict(q=(_heads(0, 4, 96, LANE), 512, ()), k=(_heads(384, 4, 96, LANE), 512, ()),
                  v=(_heads(768, 4, 192, 256), 1024, ()), glr=([(1536, 16, 0)], LANE, ()),
                  og=(_heads(1552, 4, 192, 256), 1024, ()), xq=(_XQ(2320), _XA_W, ())),
        tok=(_heads(0, 4, 192, 256), 1024), xa=(_XQ(768), _XA_W)),
    'b': dict(
        segs=dict(q=([(0, 1536, 0)], 1536, ()), k=([(1536, 1536, 0)], 1536, ()), v=([(3072, 1536, 0)], 1536, ()),
                  xq=(_XQ(4608), _XA_W, ())),
        tok=([(0, 512, 0)], 512), xa=(_XQ(512), _XA_W)),
    'c': dict(
        segs=dict(z=(_heads(0, 12, 64, LANE), 1536, ()),
                  xbc=(_heads(768, 12, 64, LANE) + [(1536, 256, 1536), (1792, 256, 1792)], 2048, ()),
                  dt=([(2048, 12, 0)], LANE, ()),
                  xq=(_XQ(2060), _XA_W, ())),
        tok=(_heads(0, 12, 64, LANE), 1536), xa=(_XQ(768), _XA_W)),
    'd': dict(
        segs=dict(q=([(0, 768, 0)], 768, ()), f=([(768, 768, 0)], 768, ()), i=([(1536, 768, 0)], 768, ()),
                  og=([(2304, 768, 0)], 768, ()), xq=(_XQ(3072), _XA_W, ())),
        tok=([(0, 768, 0)], 768), xa=(_XQ(768), _XA_W)),
}
KINDS = 'abcd'
_XS_PIECES = _heads(0, 12, 64, LANE)
_XBC_PIECES = _XS_PIECES + [(768, 256, 1536), (1024, 256, 1792)]
_HEAD_REPS = tuple((h, h * LANE, LANE) for h in range(12))


def _row(v):
    return v.reshape(1, -1)


def local_step(x, mem, positions, target, W):
    s = x.shape[0]
    grads = {}
    scan_block = CHUNK * SCAN_CHUNKS

    inv_freq = ROPE_THETA ** (-jnp.arange(DIL_HD // 2, dtype=F32) / (DIL_HD // 2))
    ang = positions.astype(F32)[:, None] * inv_freq
    cosf = jnp.concatenate([jnp.cos(ang), jnp.cos(ang)], axis=-1)
    sinf = jnp.concatenate([-jnp.sin(ang), jnp.sin(ang)], axis=-1)

    w_up, w_down = W['ffn_w_up'].astype(MXU_DTYPE), W['ffn_w_down'].astype(MXU_DTYPE)
    mem_g = _row(W['mem_norm'])
    (mem_n,) = tmap("mem_norm", _norm_stage, [mem], [mem_g], [(D_MODEL, MXU_DTYPE)])
    kv_lay = _heads(0, 4, 64, LANE) + _heads(256, 4, 64, LANE, dst0=_XA_W)

    saved = []
    for i in range(4):
        kind = KINDS[i]
        lay = LAYOUT[kind]
        sv = dict(x0=x)
        w_in = W[f'{kind}_w_in']
        w_out = W[f'{kind}_w_out']
        sv['w_seg'] = {n: _lay(w_in, 1, p, t, r).astype(MXU_DTYPE) for n, (p, t, r) in lay['segs'].items()}
        sv['wo_tok'] = _lay(w_out, 0, *lay['tok']).astype(MXU_DTYPE)
        sv['wo_xa'] = _lay(w_out, 0, *lay['xa']).astype(MXU_DTYPE)
        sv['w_kv'] = _lay(W['xa_w_kv'][i], 1, kv_lay, 2 * _XA_W).astype(MXU_DTYPE)
        sv['g1'] = _row(W['mix_norm'][i])
        (h,) = tmap(f"mix_norm_{i}", _norm_stage, [x], [sv['g1']], [(D_MODEL, MXU_DTYPE)])
        sv['h'] = h
        seg = {n: matmul(h, w) for n, w in sv['w_seg'].items()}
        sv['seg'] = seg

        if kind == 'a':
            sv['w2'] = _lay(_lay(W['a_w_gate2'], 1, _heads(0, 4, 96, LANE), 512), 0, [(0, 16, 0)], LANE)
            sv['bg'] = _row(_lay(W['a_b_gate'], 0, _heads(0, 4, 96, LANE), 512))
            sv['on'] = _row(_lay(W['a_o_norm'], 0, [(0, 192, 0)], 256))
            (la,) = tmap("gla_pre", _gla_pre, [seg['glr']], [sv['w2'], sv['bg']], [(512, F32)])
            sv['la'] = la
            sv['scan_fn'] = _gla_step(GLA_HEADS, 2 * LANE, GLA_DK ** -0.5)
            sv['scan_rows'] = [seg['q'], seg['k'], seg['v'], la]
            (o,), sv['states'] = rscan("gla_scan", sv['scan_fn'], [(LANE, 2 * LANE)] * GLA_HEADS, sv['scan_rows'], [],
                                       [(1024, F32)], scan_block)
            sv['o'] = o
            (tok,) = tmap("gla_post", _gla_post, [o, seg['og']], [sv['on']], [(1024, MXU_DTYPE)])
        elif kind == 'b':
            sv['qg'], sv['kg'] = _row(W['b_q_norm']), _row(W['b_k_norm'])
            os_, ls_ = [], []
            qkn = tmap("dil_pre", _dil_pre, [seg['q'], seg['k'], cosf, sinf], [sv['qg'], sv['kg']], [(512, F32)] * 6)
            sv['qn'], sv['kn'] = qkn[:3], qkn[3:]
            for g, (window, r) in enumerate(DIL_GROUPS):
                assert window // r == DIL_BLOCK and (s // r) % DIL_BLOCK == 0
                o, lse = dil_attn(f"dil_attn_{g}", sv['qn'][g], sv['kn'][g], seg['v'], r, g)
                os_.append(o)
                ls_.append(lse)
            sv['os'], sv['ls'] = os_, ls_
            (tok,) = tmap("dil_merge", _dil_merge, os_ + ls_, [], [(512, MXU_DTYPE)])
        elif kind == 'c':
            sv['cw'] = _lay(W['c_conv_w'], 1, _XBC_PIECES, 2048)
            sv['cb'] = _row(_lay(W['c_conv_b'], 0, _XBC_PIECES, 2048))
            sv['dtb'] = _row(_lay(W['c_dt_bias'], 0, [], 1536, _HEAD_REPS))
            sv['alog'] = _row(_lay(W['c_a_log'], 0, [], 1536, _HEAD_REPS))
            sv['dsk'] = _row(_lay(W['c_d'], 0, [], 1536, _HEAD_REPS))
            sv['cn'] = _row(_lay(W['c_norm'], 0, _XS_PIECES, 1536))
            xact = conv_fwd("ssm_conv", seg['xbc'], sv['cw'], sv['cb'], 'silu', F32, 512)
            sv['xact'] = xact
            sv['scan_rows'] = [xact, seg['dt']]
            sv['scan_params'] = [sv['dtb'], sv['alog'], sv['dsk']]
            (yv,), sv['states'] = rscan("ssd_scan", _ssd_step, [(LANE, LANE)] * SSM_HEADS, sv['scan_rows'],
                                        sv['scan_params'], [(1536, F32)], scan_block)
            sv['y'] = yv
            (tok,) = tmap("ssd_post", _mamba_post, [yv, seg['z']], [sv['cn']], [(1536, MXU_DTYPE)])
        else:
            sv['lbnd'] = W['d_lower_bounds']
            sv['on'] = _row(W['d_o_norm'])
            qq, kk, la = tmap("hgrn_pre", _hgrn_pre, [seg['q'], seg['f']], [sv['lbnd']], [(768, F32)] * 3)
            sv['scan_fn'] = _gla_step(HGRN_HEADS, LANE, 1.0)
            sv['scan_rows'] = [qq, kk, seg['i'], la]
            (o,), sv['states'] = rscan("hgrn_scan", sv['scan_fn'], [(LANE, LANE)] * HGRN_HEADS, sv['scan_rows'], [],
                                       [(768, F32)], scan_block)
            sv['o'] = o
            (tok,) = tmap("hgrn_post", _hgrn_post, [o, seg['og']], [sv['on']], [(768, MXU_DTYPE)])
        sv['tok'] = tok

        kv = matmul(mem_n, sv['w_kv'])
        sv['kv'] = kv
        sv['xqg'] = _row(_lay(W['xa_q_norm'][i], 0, [(0, 64, 0)], LANE))
        sv['xkg'] = _row(_lay(W['xa_k_norm'][i], 0, [(0, 64, 0)], LANE))
        (xa,) = tmap(f"xattn_{i}", _xattn, [seg['xq']], [kv, sv['xqg'], sv['xkg']], [(_XA_W, MXU_DTYPE)])
        sv['xa'] = xa
        x = matmul(tok, sv['wo_tok'], add=x)
        x = matmul(xa, sv['wo_xa'], add=x)
        sv['x1'] = x

        sv['g2'] = _row(W['ffn_norm'][i])
        sv['fcw'] = W['ffn_conv_w'][i]
        sv['fcb'] = _row(W['ffn_conv_b'][i])
        (h2,) = tmap(f"ffn_norm_{i}", _norm_stage, [x], [sv['g2']], [(D_MODEL, MXU_DTYPE)])
        sv['h2'] = h2
        u0 = matmul(h2, w_up, b_layer=i)
        sv['u0'] = u0
        act = conv_fwd("ffn_conv", u0, sv['fcw'], sv['fcb'], 'glu', MXU_DTYPE, 1408)
        sv['act'] = act
        x = matmul(act, w_down, b_layer=i, add=x)
        saved.append(sv)

    loss_acc, dx = loss_head(x, target)

    g_stack = {n: [None] * 4 for n in ('mix_norm', 'xa_w_kv', 'xa_q_norm', 'xa_k_norm', 'ffn_norm', 'ffn_conv_w',
                                        'ffn_conv_b')}
    g_up = jnp.zeros(W['ffn_w_up'].shape, F32)
    g_down = jnp.zeros(W['ffn_w_down'].shape, F32)
    d_memn = None
    for i in reversed(range(4)):
        kind = KINDS[i]
        lay = LAYOUT[kind]
        sv = saved[i]
        seg = sv['seg']
        dact = matmul(dx, w_down, tb=True, b_layer=i)
        g_down = matmul(sv['act'], dx, ta=True, into=(g_down, i, 0))
        (du_g, du_v), dcw, dcb = conv_bwd("ffn_conv_bwd", sv['u0'], sv['fcw'], sv['fcb'], dact, 'glu', 1408)
        g_stack['ffn_conv_w'][i], g_stack['ffn_conv_b'][i] = dcw, dcb[0]
        dh2 = matmul(du_v, w_up, tb=True, b_layer=i, b_koff=D_FF, add=matmul(du_g, w_up, tb=True, b_layer=i))
        g_up = matmul(sv['h2'], du_g, ta=True, into=(g_up, i, 0))
        g_up = matmul(sv['h2'], du_v, ta=True, into=(g_up, i, D_FF))
        (dx,), (dg2,) = tmap_bwd(f"ffn_norm_bwd_{i}", _norm_stage, [sv['x1']], [sv['g2']], [dh2], [True], {0: dx})
        g_stack['ffn_norm'][i] = dg2[0]
        dtok = matmul(dx, sv['wo_tok'], tb=True)
        dxa = matmul(dx, sv['wo_xa'], tb=True)
        g_wo = _unlay_parts(matmul(sv['tok'], dx, ta=True), 0, lay['tok'][0]) \
            + _unlay_parts(matmul(sv['xa'], dx, ta=True), 0, lay['xa'][0])
        grads[f'{kind}_w_out'] = _join(g_wo, 0)
        (dxq,), (dkv, dqg, dkg) = tmap_bwd(f"xattn_bwd_{i}", _xattn, [seg['xq']], [sv['kv'], sv['xqg'], sv['xkg']],
                                           [dxa], [True])
        g_stack['xa_q_norm'][i], g_stack['xa_k_norm'][i] = dqg[0, :XA_HD], dkg[0, :XA_HD]
        g_stack['xa_w_kv'][i] = _join(_unlay_parts(matmul(mem_n, dkv, ta=True), 1, kv_lay), 1)
        d_memn = matmul(dkv, sv['w_kv'], tb=True, add=d_memn)
        dseg = dict(xq=dxq)
        if kind == 'a':
            (do, dog), (don,) = tmap_bwd("gla_post_bwd", _gla_post, [sv['o'], seg['og']], [sv['on']], [dtok],
                                         [True, True], grad_dtype=F32)
            grads['a_o_norm'] = don[0, :GLA_DV]
            (dq, dk, dv, dla), _ = rscan_bwd("gla_scan_bwd", sv['scan_fn'], sv['states'], sv['scan_rows'], [], [do],
                                             scan_block, grad_dtype=F32)
            (dglr,), (dw2, dbg) = tmap_bwd("gla_pre_bwd", _gla_pre, [seg['glr']], [sv['w2'], sv['bg']], [dla], [True])
            grads['a_w_gate2'] = _join(_unlay_parts(dw2[:GLA_RANK], 1, _heads(0, 4, 96, LANE)), 1)
            grads['a_b_gate'] = _join(_unlay_parts(dbg[0], 0, _heads(0, 4, 96, LANE)), 0)
            dseg.update(q=dq, k=dk, v=dv, glr=dglr, og=dog)
        elif kind == 'b':
            res, _ = tmap_bwd("dil_merge_bwd", _dil_merge, sv['os'] + sv['ls'], [], [dtok], [True] * 6, grad_dtype=F32)
            dqn, dkn, dvs = [], [], []
            for g, (_, r) in enumerate(DIL_GROUPS):
                a_, b_, c_ = dil_attn_bwd(f"dil_attn_bwd_{g}", sv['qn'][g], sv['kn'][g], seg['v'], res[g], res[3 + g],
                                          r, g)
                dqn.append(a_)
                dkn.append(b_)
                dvs.append(c_)
            (dq, dk), (dqg, dkg) = tmap_bwd("dil_pre_bwd", _dil_pre, [seg['q'], seg['k'], cosf, sinf],
                                            [sv['qg'], sv['kg']], dqn + dkn, [True, True, False, False])
            dseg.update(q=dq, k=dk, v=jnp.concatenate(dvs, axis=1))
            grads['b_q_norm'], grads['b_k_norm'] = dqg[0], dkg[0]
        elif kind == 'c':
            (dy, dz), (dcn,) = tmap_bwd("ssd_post_bwd", _mamba_post, [sv['y'], seg['z']], [sv['cn']], [dtok],
                                        [True, True], grad_dtype=F32)
            grads['c_norm'] = _join(_unlay_parts(dcn[0], 0, _XS_PIECES), 0)
            (dxact, ddt), (ddtb, dalog, ddsk) = rscan_bwd("ssd_scan_bwd", _ssd_step, sv['states'], sv['scan_rows'],
                                                          sv['scan_params'], [dy], scan_block, grad_dtype=F32)
            for nm, gv in (('c_dt_bias', ddtb), ('c_a_log', dalog), ('c_d', ddsk)):
                grads[nm] = _join(_unlay_parts(gv[0], 0, [], _HEAD_REPS), 0)
            (dxbc,), dcw, dcb = conv_bwd("ssm_conv_bwd", seg['xbc'], sv['cw'], sv['cb'], dxact, 'silu', 512)
            grads['c_conv_w'] = _join(_unlay_parts(dcw, 1, _XBC_PIECES), 1)
            grads['c_conv_b'] = _join(_unlay_parts(dcb[0], 0, _XBC_PIECES), 0)
            dseg.update(z=dz, xbc=dxbc, dt=ddt)
        else:
            (do, dog), (don,) = tmap_bwd("hgrn_post_bwd", _hgrn_post, [sv['o'], seg['og']], [sv['on']], [dtok],
                                         [True, True], grad_dtype=F32)
            grads['d_o_norm'] = don[0]
            (dqq, dkk, di, dla), _ = rscan_bwd("hgrn_scan_bwd", sv['scan_fn'], sv['states'], sv['scan_rows'], [], [do],
                                               scan_block, grad_dtype=F32)
            (dq, df), (dlb,) = tmap_bwd("hgrn_pre_bwd", _hgrn_pre, [seg['q'], seg['f']], [sv['lbnd']], [dqq, dkk, dla],
                                        [True, True])
            grads['d_lower_bounds'] = dlb
            dseg.update(q=dq, f=df, i=di, og=dog)
        dh = None
        g_in = []
        for n, (p, t, rp) in lay['segs'].items():
            dh = matmul(dseg[n], sv['w_seg'][n], tb=True, add=dh)
            g_in += _unlay_parts(matmul(sv['h'], dseg[n], ta=True), 1, p, rp)
        grads[f'{kind}_w_in'] = _join(g_in, 1)
        (dx,), (dg1,) = tmap_bwd(f"mix_norm_bwd_{i}", _norm_stage, [sv['x0']], [sv['g1']], [dh], [True], {0: dx})
        g_stack['mix_norm'][i] = dg1[0]

    _, (dmg,) = tmap_bwd("mem_norm_bwd", _norm_stage, [mem], [mem_g], [d_memn], [False])
    grads['mem_norm'] = dmg[0]
    for n, parts in g_stack.items():
        grads[n] = jnp.stack(parts)
    grads['ffn_w_up'], grads['ffn_w_down'] = g_up, g_down
    return loss_acc, dx, grads


def _pack(arrs, dtype, row_multiple=PACK_ROWS):
    parts, rows = [], 0
    for a in arrs:
        f = a.reshape(-1).astype(dtype)
        unit = PACK_ROWS * PACK_COLS
        pad = (-f.shape[0]) % unit
        if pad:
            f = jnp.concatenate([f, jnp.zeros((pad,), dtype)])
        parts.append(f.reshape(-1, PACK_COLS))
        rows += parts[-1].shape[0]
    if rows % row_multiple:
        parts.append(jnp.zeros((row_multiple - rows % row_multiple, PACK_COLS), dtype))
    return jnp.concatenate(parts, axis=0)


def _unpack(buf, shapes):
    out, row = [], 0
    for sh in shapes:
        n = int(np.prod(sh))
        rows = -(-n // (PACK_ROWS * PACK_COLS)) * PACK_ROWS
        out.append(buf[row:row + rows].reshape(-1)[:n].reshape(sh))
        row += rows
    return out


def _pack_rows(arrs):
    parts = []
    for a in arrs:
        f = a.reshape(-1).astype(F32)
        parts.append(jnp.pad(f, (0, (-f.shape[0]) % PACK_COLS)))
    flat = jnp.concatenate(parts)
    rows = flat.shape[0] // PACK_COLS
    return jnp.pad(flat, (0, (-rows % 16) * PACK_COLS)).reshape(-1, PACK_COLS)


def _unpack_rows(buf, shapes):
    flat, out, pos = buf.reshape(-1), [], 0
    for sh in shapes:
        n = int(np.prod(sh))
        out.append(flat[pos:pos + n].reshape(sh))
        pos += -(-n // PACK_COLS) * PACK_COLS
    return out


def _split_chips(a, axis):
    sh = a.shape
    return jnp.moveaxis(a.reshape(sh[:axis] + (N_CHIPS, sh[axis] // N_CHIPS) + sh[axis + 1:]), axis, 0)


def _merge_chips(a, axis):
    a = jnp.moveaxis(a, 0, axis)
    sh = a.shape
    return a.reshape(sh[:axis] + (sh[axis] * sh[axis + 1],) + sh[axis + 2:])


def kernel(x, mem, positions, mem_norm, mix_norm, xa_w_kv, xa_q_norm, xa_k_norm, ffn_norm, ffn_w_up, ffn_conv_w, ffn_conv_b, ffn_w_down, a_w_in, a_w_gate2, a_b_gate, a_o_norm, a_w_out, b_w_in, b_q_norm, b_k_norm, b_w_out, c_w_in, c_conv_w, c_conv_b, c_dt_bias, c_a_log, c_d, c_norm, c_w_out, d_w_in, d_lower_bounds, d_o_norm, d_w_out, loss_target, m_mem_norm, m_mix_norm, m_xa_w_kv, m_xa_q_norm, m_xa_k_norm, m_ffn_norm, m_ffn_w_up, m_ffn_conv_w, m_ffn_conv_b, m_ffn_w_down, m_a_w_in, m_a_w_gate2, m_a_b_gate, m_a_o_norm, m_a_w_out, m_b_w_in, m_b_q_norm, m_b_k_norm, m_b_w_out, m_c_w_in, m_c_conv_w, m_c_conv_b, m_c_dt_bias, m_c_a_log, m_c_d, m_c_norm, m_c_w_out, m_d_w_in, m_d_lower_bounds, m_d_o_norm, m_d_w_out, v_mem_norm, v_mix_norm, v_xa_w_kv, v_xa_q_norm, v_xa_k_norm, v_ffn_norm, v_ffn_w_up, v_ffn_conv_w, v_ffn_conv_b, v_ffn_w_down, v_a_w_in, v_a_w_gate2, v_a_b_gate, v_a_o_norm, v_a_w_out, v_b_w_in, v_b_q_norm, v_b_k_norm, v_b_w_out, v_c_w_in, v_c_conv_w, v_c_conv_b, v_c_dt_bias, v_c_a_log, v_c_d, v_c_norm, v_c_w_out, v_d_w_in, v_d_lower_bounds, v_d_o_norm, v_d_w_out):
    args = locals()
    w = {n: args[n] for n in WEIGHTS}
    m = {n: args['m_' + n] for n in WEIGHTS}
    v = {n: args['v_' + n] for n in WEIGHTS}
    cx, cy, cc = lax.axis_index("x"), lax.axis_index("y"), lax.axis_index("c")
    chip = 2 * cx + cy

    packed_names = [n for n in BIG if n not in LAYERED]
    packed_shapes = [w[n].shape for n in packed_names]
    packed_w = _pack([w[n] for n in packed_names], MXU_DTYPE, 1024)
    kinds = [Packed(packed_w.shape)] + [Layered(w[n].shape, SHARD_AXIS[n]) for n in LAYERED]
    gathered, *whole = allgather_chips([packed_w] + [w[n].astype(MXU_DTYPE) for n in LAYERED], kinds)
    per_chip = [_unpack(gathered[j], packed_shapes) for j in range(N_CHIPS)]
    full = {n: _merge_chips(jnp.stack([per_chip[j][k] for j in range(N_CHIPS)]), SHARD_AXIS[n])
            for k, n in enumerate(packed_names)}
    full.update(zip(LAYERED, whole))
    small_sharded = [n for n in SMALL if n in SHARD_AXIS]
    sg = allgather_devices(_pack([w[n] for n in small_sharded], F32))
    per_chip_s = [_unpack(sg[2 * j], [w[n].shape for n in small_sharded]) for j in range(N_CHIPS)]
    for k, n in enumerate(small_sharded):
        full[n] = _merge_chips(jnp.stack([per_chip_s[j][k] for j in range(N_CHIPS)]), SHARD_AXIS[n])
    for n in SMALL:
        if n not in SHARD_AXIS:
            full[n] = w[n]

    loss_acc, dx, grads = local_step(x[0], mem[0], positions[0], loss_target[0], full)
    loss = lax.psum(jnp.sum(loss_acc), ("x", "y", "c"))

    gb = jnp.stack([_pack([_split_chips(grads[n], SHARD_AXIS[n])[j] for n in packed_names], F32, 1024)
                    for j in range(N_CHIPS)])
    rows = gb.shape[1]
    gs_ = [gb.reshape(N_CHIPS, 2, rows // 2, PACK_COLS)] + [grads[n] for n in LAYERED]
    core_id, chip_id = cc.reshape(1).astype(jnp.int32), chip.reshape(1).astype(jnp.int32)
    gots = swap_halves_sibling(gs_, kinds)
    pairs = [add_own_half(g, o, k, core_id, GRAD_WIRE_DTYPE, f"add_own_half_{t}")
             for t, (g, o, k) in enumerate(zip(gs_, gots, kinds))]
    recvd = exchange_chips(pairs, kinds)
    halves = [sum_chips(r, p, k, chip_id, f"sum_chips_{t}") for t, (r, p, k) in enumerate(zip(recvd, pairs, kinds))]
    red, *red_layered = join_halves_sibling(halves)
    g_big = dict(zip(packed_names, _unpack(red.reshape(rows, PACK_COLS), packed_shapes)))
    for n, r in zip(LAYERED, red_layered):
        g_big[n] = r.reshape(w[n].shape)

    small_full_shapes = [grads[n].shape for n in SMALL]
    gs = sum_slabs(allgather_devices(_pack_rows([grads[n] for n in SMALL])), "sum_devices")
    g_small = {}
    for n, gfull in zip(SMALL, _unpack_rows(gs, small_full_shapes)):
        if n in SHARD_AXIS:
            ax = SHARD_AXIS[n]
            size = gfull.shape[ax] // N_CHIPS
            gfull = lax.dynamic_slice_in_dim(gfull, chip * size, size, axis=ax)
        g_small[n] = gfull

    g_out, delta, new_m, new_v = {**g_big, **g_small}, {}, {}, {}
    for n in WEIGHTS:
        sh = w[n].shape
        two_d = (-1, sh[-1])
        d_, m_, v_ = adamw(f"adamw_{n}", w[n].reshape(two_d), g_out[n].reshape(two_d), m[n].reshape(two_d),
                           v[n].reshape(two_d))
        delta[n], new_m[n], new_v[n] = d_.reshape(sh), m_.reshape(sh), v_.reshape(sh)

    return (loss, dx[None], *[g_out[n] for n in WEIGHTS], *[delta[n] for n in WEIGHTS],
            *[new_m[n] for n in WEIGHTS], *[new_v[n] for n in WEIGHTS])
```

```python
import functools
import math

import jax
import jax.numpy as jnp
import numpy as np
from jax import lax
from jax.experimental import pallas as pl
from jax.experimental.pallas import tpu as pltpu

F32 = jnp.float32
MXU_DTYPE = jnp.bfloat16
GRAD_WIRE_DTYPE = jnp.bfloat16
VMEM_LIMIT_V7X = 56 * 1024 * 1024
LANE = 128
SUBLANE = 8

D_MODEL = 1024
N_MEM = 256
EPS = 1e-6
ROPE_THETA = 10000.0
CHUNK = 64
XA_HEADS, XA_HD = 4, 64
GLA_HEADS, GLA_DK, GLA_DV, GLA_RANK, GLA_GATE_NORM = 4, 96, 192, 16, 16.0
DIL_GROUPS = ((128, 1), (512, 4), (2048, 16))
DIL_HEADS, DIL_HD, DIL_BLOCK = 4, 128, 128
SSM_HD, SSM_HEADS, SSM_GROUPS, SSM_STATE, SSM_CONV = 64, 12, 2, 128, 4
HGRN_HEADS, HGRN_DK = 6, 128
D_FF = 2816
FFN_CONV = 3
ADAM_LR, ADAM_B1, ADAM_B2, ADAM_EPS, ADAM_WD, ADAM_STEP = 0.001, 0.9, 0.999, 1e-08, 0.01, 10

MM_TILES = (2816, 1408, 1024, 768, 512, 384, 256, 128)
MM_K_TILES = (2816, 2048, 1536, 1408, 1024, 768, 512, 384, 256, 128)
MM_MIN_OUT_TILE = 512 * 1024
MM_VMEM_BUDGET = 40 * 1024 * 1024
ROW_BLOCK = 256
SCAN_CHUNKS = 2
PACK_COLS = 1024
PACK_ROWS = 32

WEIGHTS = ['mem_norm', 'mix_norm', 'xa_w_kv', 'xa_q_norm', 'xa_k_norm', 'ffn_norm', 'ffn_w_up', 'ffn_conv_w',
           'ffn_conv_b', 'ffn_w_down', 'a_w_in', 'a_w_gate2', 'a_b_gate', 'a_o_norm', 'a_w_out', 'b_w_in', 'b_q_norm',
           'b_k_norm', 'b_w_out', 'c_w_in', 'c_conv_w', 'c_conv_b', 'c_dt_bias', 'c_a_log', 'c_d', 'c_norm', 'c_w_out',
           'd_w_in', 'd_lower_bounds', 'd_o_norm', 'd_w_out']
SHARD_AXIS = {'xa_w_kv': 1, 'ffn_w_up': 2, 'ffn_conv_w': 2, 'ffn_w_down': 1, 'a_w_in': 1, 'a_w_gate2': 1, 'a_w_out': 0,
              'b_w_in': 1, 'b_w_out': 1, 'c_w_in': 1, 'c_conv_w': 1, 'c_w_out': 0, 'd_w_in': 1, 'd_w_out': 0}
BIG = ['xa_w_kv', 'ffn_w_up', 'ffn_w_down', 'a_w_in', 'a_w_gate2', 'a_w_out', 'b_w_in', 'b_w_out', 'c_w_in', 'c_w_out',
       'd_w_in', 'd_w_out']
SMALL = [n for n in WEIGHTS if n not in BIG]
LAYERED = ['ffn_w_up', 'ffn_w_down']
N_CHIPS = 4
N_DEV = 8


class _MatmulSet:
    def __init__(self, cast, precision):
        def dot(a, b, dims):
            if cast:
                a = a.astype(MXU_DTYPE)
                b = b.astype(MXU_DTYPE)
            return lax.dot_general(a, b, (dims, ((), ())), precision=precision, preferred_element_type=F32)

        @jax.custom_vjp
        def nn(a, b):
            return dot(a, b, ((1,), (0,)))

        @jax.custom_vjp
        def nt(a, b):
            return dot(a, b, ((1,), (1,)))

        @jax.custom_vjp
        def tn(a, b):
            return dot(a, b, ((0,), (0,)))

        nn.defvjp(lambda a, b: (nn(a, b), (a, b)), lambda r, g: (nt(g, r[1]), tn(r[0], g)))
        nt.defvjp(lambda a, b: (nt(a, b), (a, b)), lambda r, g: (nn(g, r[1]), tn(g, r[0])))
        tn.defvjp(lambda a, b: (tn(a, b), (a, b)), lambda r, g: (nt(r[1], g), nn(r[0], g)))
        self.nn, self.nt, self.tn = nn, nt, tn


mm = _MatmulSet(True, None)
hi = _MatmulSet(False, lax.Precision.HIGHEST)


def _sigmoid(x):
    return jax.nn.sigmoid(x)


def _silu(x):
    return x * jax.nn.sigmoid(x)


def _softplus(x):
    return jnp.maximum(x, 0.0) + jnp.log1p(jnp.exp(-jnp.abs(x)))


def _rms(x, g, n_real=None):
    n = n_real or x.shape[-1]
    ms = jnp.sum(x * x, axis=-1, keepdims=True) * (1.0 / n)
    return x * lax.rsqrt(ms + EPS) * g


@jax.custom_vjp
def _swap_halves(x):
    return pltpu.roll(x, 64, 1)


_swap_halves.defvjp(lambda x: (_swap_halves(x), None), lambda _, g: (_swap_halves(g),))


def _tile(n, cands):
    for c in cands:
        if n % c == 0:
            return c
    raise ValueError(f"no tile for {n} among {cands}")


def _cparams(sem):
    return pltpu.CompilerParams(dimension_semantics=sem, vmem_limit_bytes=VMEM_LIMIT_V7X)


def _f32(v):
    return v.astype(F32) if jnp.issubdtype(v.dtype, jnp.floating) else v


def matmul(a, b, *, ta=False, tb=False, add=None, out_dtype=F32, b_layer=None, b_koff=0, into=None):
    m, k = (a.shape[1], a.shape[0]) if ta else a.shape
    b2 = b.shape[1:] if b_layer is not None else b.shape
    n = b2[0] if tb else b2[1]
    assert b_koff + k <= (b2[1] if tb else b2[0]), (a.shape, b.shape, ta, tb, b_koff)
    sa, sb, so = a.dtype.itemsize, b.dtype.itemsize, jnp.dtype(out_dtype).itemsize
    n_align = math.gcd(n, into[2]) if into is not None and into[2] else n
    k_align = math.gcd(k, b_koff) if b_koff else k

    def vmem(tm_, tn_, tk_):
        return (2 * tm_ * tk_ * sa + 2 * tk_ * tn_ * sb + 2 * tm_ * tn_ * so + (tm_ * tn_ * 4 if tk_ < k else 0)
                + (2 * tm_ * tn_ * add.dtype.itemsize if add is not None else 0))

    for tk in [t for t in MM_K_TILES if k_align % t == 0]:
        fits = [(tm_ * tn_, tm_, tn_) for tm_ in MM_TILES if m % tm_ == 0 for tn_ in MM_TILES
                if n % tn_ == 0 and n_align % tn_ == 0 and vmem(tm_, tn_, tk) <= MM_VMEM_BUDGET]
        if fits and (max(fits)[0] >= min(MM_MIN_OUT_TILE, m * n) or tk == MM_K_TILES[-1]):
            break
    _, tm, tn = max(fits)
    nk = k // tk
    dims = (((0,) if ta else (1,)), ((1,) if tb else (0,)))
    n_extra = (add is not None) + (into is not None)

    def body(*refs):
        a_ref, b_ref = refs[0], refs[1]
        add_ref = refs[2] if add is not None else None
        o_ref = refs[2 + n_extra]
        part = lax.dot_general(a_ref[...].astype(MXU_DTYPE), b_ref[...].astype(MXU_DTYPE), (dims, ((), ())),
                               preferred_element_type=F32)

        def finish(r):
            if add_ref is not None:
                r = r + add_ref[...].astype(F32)
            o_ref[...] = r.astype(o_ref.dtype)

        if nk == 1:
            finish(part)
            return
        acc = refs[-1]
        kk = pl.program_id(2)

        @pl.when(kk == 0)
        def _():
            acc[...] = part

        @pl.when(kk > 0)
        def _():
            acc[...] += part

        @pl.when(kk == nk - 1)
        def _():
            finish(acc[...])

    a_spec = pl.BlockSpec((tk, tm), lambda i, j, q: (q, i)) if ta else pl.BlockSpec((tm, tk), lambda i, j, q: (i, q))
    ko = b_koff // tk
    if b_layer is None:
        b_spec = (pl.BlockSpec((tn, tk), lambda i, j, q: (j, q + ko)) if tb
                  else pl.BlockSpec((tk, tn), lambda i, j, q: (q + ko, j)))
    else:
        b_spec = (pl.BlockSpec((None, tn, tk), lambda i, j, q: (b_layer, j, q + ko)) if tb
                  else pl.BlockSpec((None, tk, tn), lambda i, j, q: (b_layer, q + ko, j)))
    o_spec = pl.BlockSpec((tm, tn), lambda i, j, q: (i, j))
    ins, specs = [a, b], [a_spec, b_spec]
    if add is not None:
        ins.append(add)
        specs.append(o_spec)
    aliases = {}
    out_shape = jax.ShapeDtypeStruct((m, n), out_dtype)
    if into is not None:
        buf, layer, col0 = into
        assert buf.shape[1] == m and buf.dtype == out_dtype
        co = col0 // tn
        ins.append(buf)
        specs.append(_ANY)
        aliases = {len(ins) - 1: 0}
        o_spec = pl.BlockSpec((None, tm, tn), lambda i, j, q: (layer, i, j + co))
        out_shape = jax.ShapeDtypeStruct(buf.shape, buf.dtype)
    return pl.pallas_call(
        body, name=f"mm_{m}x{k}x{n}_{int(ta)}{int(tb)}{int(add is not None)}{int(b_layer is not None)}{int(into is not None)}",
        grid=(m // tm, n // tn, nk), in_specs=specs, out_specs=o_spec, out_shape=out_shape,
        input_output_aliases=aliases,
        scratch_shapes=[pltpu.VMEM((tm, tn), F32)] if nk > 1 else [],
        compiler_params=_cparams(("parallel", "parallel", "arbitrary")),
    )(*ins)


def _row_spec(a, block):
    return pl.BlockSpec((block, a.shape[1]), lambda i: (i, 0))


def _whole_spec(a):
    return pl.BlockSpec(a.shape, lambda i: (0,) * a.ndim)


def tmap(name, fn, rows, params, outs, block=ROW_BLOCK):
    s = rows[0].shape[0]
    block = min(block, s)
    nr, npar = len(rows), len(params)

    def body(*refs):
        res = fn(*[_f32(r[...]) for r in refs[:nr]], *[_f32(p[...]) for p in refs[nr:nr + npar]])
        for o_ref, v in zip(refs[nr + npar:], res, strict=True):
            o_ref[...] = v.astype(o_ref.dtype)

    return pl.pallas_call(
        body, name=name, grid=(s // block,),
        in_specs=[_row_spec(a, block) for a in rows] + [_whole_spec(p) for p in params],
        out_specs=[pl.BlockSpec((block, w), lambda i: (i, 0)) for w, _ in outs],
        out_shape=[jax.ShapeDtypeStruct((s, w), dt) for w, dt in outs],
        compiler_params=_cparams(("parallel",)),
    )(*rows, *params)


def tmap_bwd(name, fn, rows, params, douts, row_grad, row_add=None, grad_dtype=None, block=ROW_BLOCK):
    s = rows[0].shape[0]
    block = min(block, s)
    grad_dtype = grad_dtype or MXU_DTYPE
    nr, npar, nd = len(rows), len(params), len(douts)
    gr = [i for i in range(nr) if row_grad[i]]
    row_add = row_add or {}
    adds = [row_add[i] for i in gr if i in row_add]

    def body(*refs):
        rv = [_f32(r[...]) for r in refs[:nr]]
        pv = [_f32(p[...]) for p in refs[nr:nr + npar]]
        dv = tuple(_f32(d[...]) for d in refs[nr + npar:nr + npar + nd])
        add_refs = list(refs[nr + npar + nd:nr + npar + nd + len(adds)])
        out_refs = refs[nr + npar + nd + len(adds):]

        def f(*diff):
            rr = list(rv)
            for n_, i_ in enumerate(gr):
                rr[i_] = diff[n_]
            return tuple(fn(*rr, *diff[len(gr):]))

        _, vjp = jax.vjp(f, *[rv[i_] for i_ in gr], *pv)
        g = vjp(dv)
        for n_, i_ in enumerate(gr):
            v = g[n_]
            if i_ in row_add:
                v = v + add_refs.pop(0)[...].astype(F32)
            out_refs[n_][...] = v.astype(out_refs[n_].dtype)
        first = pl.program_id(0) == 0
        for n_ in range(npar):
            ref = out_refs[len(gr) + n_]

            @pl.when(first)
            def _(ref=ref):
                ref[...] = jnp.zeros_like(ref)

            ref[...] += g[len(gr) + n_]

    res = pl.pallas_call(
        body, name=name, grid=(s // block,),
        in_specs=[_row_spec(a, block) for a in rows] + [_whole_spec(p) for p in params]
        + [_row_spec(d, block) for d in douts] + [_row_spec(a, block) for a in adds],
        out_specs=[_row_spec(rows[i], block) for i in gr] + [_whole_spec(p) for p in params],
        out_shape=[jax.ShapeDtypeStruct(rows[i].shape, F32 if i in row_add else grad_dtype) for i in gr]
        + [jax.ShapeDtypeStruct(p.shape, F32) for p in params],
        compiler_params=_cparams(("arbitrary",)),
    )(*rows, *params, *douts, *adds)
    return list(res[:len(gr)]), list(res[len(gr):])


def rscan(name, fn, state_shapes, rows, params, outs, block):
    s = rows[0].shape[0]
    nsteps = s // block
    nr, npar, no, ns = len(rows), len(params), len(outs), len(state_shapes)

    def body(*refs):
        out_refs = refs[nr + npar:nr + npar + no]
        sav_refs = refs[nr + npar + no:nr + npar + no + ns]
        st_refs = refs[nr + npar + no + ns:]

        @pl.when(pl.program_id(0) == 0)
        def _():
            for st in st_refs:
                st[...] = jnp.zeros_like(st)

        sts = tuple(st[...] for st in st_refs)
        for sv, v in zip(sav_refs, sts):
            sv[...] = v
        new, res = fn(sts, *[_f32(r[...]) for r in refs[:nr]], *[_f32(p[...]) for p in refs[nr:nr + npar]])
        for st, v in zip(st_refs, new, strict=True):
            st[...] = v
        for o_ref, v in zip(out_refs, res, strict=True):
            o_ref[...] = v.astype(o_ref.dtype)

    res = pl.pallas_call(
        body, name=name, grid=(nsteps,),
        in_specs=[_row_spec(a, block) for a in rows] + [_whole_spec(p) for p in params],
        out_specs=[pl.BlockSpec((block, w), lambda i: (i, 0)) for w, _ in outs]
        + [pl.BlockSpec(sh, lambda i: (i, 0)) for sh in state_shapes],
        out_shape=[jax.ShapeDtypeStruct((s, w), dt) for w, dt in outs]
        + [jax.ShapeDtypeStruct((nsteps * sh[0], sh[1]), F32) for sh in state_shapes],
        scratch_shapes=[pltpu.VMEM(sh, F32) for sh in state_shapes],
        compiler_params=_cparams(("arbitrary",)),
    )(*rows, *params)
    return list(res[:no]), list(res[no:])


def rscan_bwd(name, fn, saved, rows, params, douts, block, grad_dtype=None):
    s = rows[0].shape[0]
    nsteps = s // block
    grad_dtype = grad_dtype or MXU_DTYPE
    nr, npar, nd, ns = len(rows), len(params), len(douts), len(saved)
    state_shapes = [(sv.shape[0] // nsteps, sv.shape[1]) for sv in saved]

    def body(*refs):
        rv = [_f32(r[...]) for r in refs[:nr]]
        pv = [_f32(p[...]) for p in refs[nr:nr + npar]]
        dv = tuple(_f32(d[...]) for d in refs[nr + npar:nr + npar + nd])
        sv = tuple(x[...] for x in refs[nr + npar + nd:nr + npar + nd + ns])
        out_refs = refs[nr + npar + nd + ns:nr + npar + nd + ns + nr + npar]
        dst_refs = refs[nr + npar + nd + ns + nr + npar:]
        first = pl.program_id(0) == 0

        @pl.when(first)
        def _():
            for d in dst_refs:
                d[...] = jnp.zeros_like(d)

        def f(sts, *args):
            return fn(sts, *args)

        _, vjp = jax.vjp(f, sv, *rv, *pv)
        g = vjp((tuple(d[...] for d in dst_refs), dv))
        for d, v in zip(dst_refs, g[0], strict=True):
            d[...] = v
        for n_ in range(nr):
            out_refs[n_][...] = g[1 + n_].astype(out_refs[n_].dtype)
        for n_ in range(npar):
            ref = out_refs[nr + n_]

            @pl.when(first)
            def _(ref=ref):
                ref[...] = jnp.zeros_like(ref)

            ref[...] += g[1 + nr + n_]

    rev = lambda i: (nsteps - 1 - i, 0)
    res = pl.pallas_call(
        body, name=name, grid=(nsteps,),
        in_specs=[pl.BlockSpec((block, a.shape[1]), rev) for a in rows] + [_whole_spec(p) for p in params]
        + [pl.BlockSpec((block, d.shape[1]), rev) for d in douts] + [pl.BlockSpec(sh, rev) for sh in state_shapes],
        out_specs=[pl.BlockSpec((block, a.shape[1]), rev) for a in rows] + [_whole_spec(p) for p in params],
        out_shape=[jax.ShapeDtypeStruct(a.shape, grad_dtype) for a in rows]
        + [jax.ShapeDtypeStruct(p.shape, F32) for p in params],
        scratch_shapes=[pltpu.VMEM(sh, F32) for sh in state_shapes],
        compiler_params=_cparams(("arbitrary",)),
    )(*rows, *params, *douts, *saved)
    return list(res[:nr]), list(res[nr:])


def _norm_stage(x, g):
    return (_rms(x, g),)


def _tril():
    r = lax.broadcasted_iota(jnp.int32, (CHUNK, CHUNK), 0)
    c = lax.broadcasted_iota(jnp.int32, (CHUNK, CHUNK), 1)
    return r >= c


def _gla_chunk(st, q, k, v, la, b):
    tril = _tril()
    rowi = lax.broadcasted_iota(jnp.int32, (CHUNK, 1), 0)
    b_last = jnp.sum(la, axis=0, keepdims=True)
    b_ref = jnp.sum(jnp.where(rowi < CHUNK // 2, la, 0.0), axis=0, keepdims=True)
    att = mm.nt(q * jnp.exp(b - b_ref), k * jnp.exp(b_ref - b))
    att = jnp.where(tril, att, 0.0)
    o = mm.nn(att, v) + mm.nn(q * jnp.exp(b), st)
    decay = jnp.exp(jnp.broadcast_to(b_last, (LANE, LANE)).T)
    decay = jnp.concatenate([decay] * (v.shape[1] // LANE), axis=1)
    st2 = decay * st + mm.tn(k * jnp.exp(b_last - b), v)
    return st2, o


def _gla_step(heads, vp, scale):
    kp = LANE

    def fn(states, q, k, v, la):
        sts = list(states)
        trif = _tril().astype(F32)
        rows = []
        for c in range(q.shape[0] // CHUNK):
            r = slice(c * CHUNK, (c + 1) * CHUNK)
            b_all = hi.nn(trif, la[r])
            oh = []
            for h in range(heads):
                ks, vs = slice(h * kp, (h + 1) * kp), slice(h * vp, (h + 1) * vp)
                qh = q[r, ks] * scale if scale != 1.0 else q[r, ks]
                sts[h], o = _gla_chunk(sts[h], qh, k[r, ks], v[r, vs], la[r, ks], b_all[:, ks])
                oh.append(o)
            rows.append(jnp.concatenate(oh, axis=1))
        return tuple(sts), (jnp.concatenate(rows, axis=0),)

    return fn


def _ssd_step(states, xa, dtr, dtb, alog, dsk):
    sts = list(states)
    trif = _tril().astype(F32)
    wide = lax.broadcasted_iota(jnp.int32, (CHUNK, LANE), 0) >= lax.broadcasted_iota(jnp.int32, (CHUNK, LANE), 1)
    hg = SSM_HEADS // SSM_GROUPS
    xw = SSM_HEADS * LANE
    lane, head = lax.broadcasted_iota(jnp.int32, (LANE, xw), 1), lax.broadcasted_iota(jnp.int32, (LANE, xw), 0)
    spread = ((lane >= head * LANE) & (lane < (head + 1) * LANE)).astype(F32)
    neg_a = -jnp.exp(alog)
    pad = jnp.zeros((CHUNK, LANE), F32)
    rows = []
    for c in range(xa.shape[0] // CHUNK):
        r = slice(c * CHUNK, (c + 1) * CHUNK)
        dt_all = _softplus(hi.nn(dtr[r], spread) + dtb)
        a_all = dt_all * neg_a
        acs_all = hi.nn(trif, a_all)
        last_all = jnp.sum(a_all, axis=0, keepdims=True)
        yh = []
        for g in range(SSM_GROUPS):
            bm = xa[r, xw + g * LANE:xw + (g + 1) * LANE]
            cm = xa[r, xw + (SSM_GROUPS + g) * LANE:xw + (SSM_GROUPS + g + 1) * LANE]
            cb = mm.nt(cm, jnp.concatenate([bm, pad], axis=0))
            for hh in range(hg):
                h = g * hg + hh
                ls = slice(h * LANE, (h + 1) * LANE)
                xs, acs, acs_last = xa[r, ls], acs_all[:, ls], last_all[:, ls]
                xdt = xs * dt_all[:, ls]
                seg = acs - jnp.concatenate([acs, pad], axis=0).T[:CHUNK]
                lmat = jnp.exp(jnp.where(wide, seg, -1e30))
                y = (mm.nn(cb * lmat, jnp.concatenate([xdt, pad], axis=0)) + mm.nn(cm, sts[h]) * jnp.exp(acs)
                     + dsk[:, ls] * xs)
                sts[h] = jnp.exp(acs_last) * sts[h] + mm.tn(bm, xdt * jnp.exp(acs_last - acs))
                yh.append(y)
        rows.append(jnp.concatenate(yh, axis=1))
    return tuple(sts), (jnp.concatenate(rows, axis=0),)


def _gla_pre(glr, w2, bg):
    z = mm.nn(glr, w2) + bg
    return (-_softplus(-z) * (1.0 / GLA_GATE_NORM),)


def _gla_post(o, og, g):
    w = 2 * LANE
    return (jnp.concatenate([_rms(o[:, h * w:(h + 1) * w], g, GLA_DV) * _silu(og[:, h * w:(h + 1) * w])
                             for h in range(GLA_HEADS)], axis=1),)


def _hgrn_pre(q, f, lbnd):
    e = jnp.exp(lbnd - jnp.max(lbnd, axis=0, keepdims=True))
    rowi = lax.broadcasted_iota(jnp.int32, e.shape, 0)
    lb = jnp.sum(jnp.where(rowi >= 1, e, 0.0), axis=0, keepdims=True) / jnp.sum(e, axis=0, keepdims=True)
    fg = lb + (1.0 - lb) * _sigmoid(f)
    return _silu(q), 1.0 - fg, jnp.log(fg)


def _hgrn_post(o, og, g):
    return (jnp.concatenate([_rms(o[:, h * LANE:(h + 1) * LANE], g) for h in range(HGRN_HEADS)], axis=1)
            * _sigmoid(og),)


def _mamba_post(y, z, g):
    v = y * _silu(z)
    w = (SSM_HEADS // SSM_GROUPS) * LANE
    n_real = (SSM_HEADS // SSM_GROUPS) * SSM_HD
    return (jnp.concatenate([_rms(v[:, i * w:(i + 1) * w], g[:, i * w:(i + 1) * w], n_real)
                             for i in range(SSM_GROUPS)], axis=1),)


def _dil_pre(q, k, cosf, sinf, qg, kg):
    def groups(x, g):
        out = []
        for grp in range(len(DIL_GROUPS)):
            hs = []
            for h in range(grp * DIL_HEADS, (grp + 1) * DIL_HEADS):
                n = _rms(x[:, h * LANE:(h + 1) * LANE], g)
                hs.append(n * cosf + _swap_halves(n) * sinf)
            out.append(jnp.concatenate(hs, axis=1))
        return out

    return (*groups(q, qg), *groups(k, kg))


def _dil_merge(o0, o1, o2, l0, l1, l2):
    m = jnp.maximum(jnp.maximum(l0, l1), l2)
    e0, e1, e2 = jnp.exp(l0 - m), jnp.exp(l1 - m), jnp.exp(l2 - m)
    return ((e0 * o0 + e1 * o1 + e2 * o2) / (e0 + e1 + e2),)


def _dil_block(q, kp, kc, vp, vc, lim):
    kk = jnp.concatenate([kp, kc], axis=0)
    vv = jnp.concatenate([vp, vc], axis=0)
    s = mm.nt(q, kk) * (DIL_HD ** -0.5)
    i = lax.broadcasted_iota(jnp.int32, s.shape, 0)
    j = lax.broadcasted_iota(jnp.int32, s.shape, 1)
    dist = DIL_BLOCK + i - j
    s = jnp.where((dist >= 0) & (dist <= DIL_BLOCK) & (j >= lim), s, -1e30)
    m = jnp.max(s, axis=-1, keepdims=True)
    p = jnp.exp(s - m)
    l = jnp.sum(p, axis=-1, keepdims=True)
    return mm.nn(p / l, vv), jnp.broadcast_to(m + jnp.log(l), (q.shape[0], LANE))


def _xattn(xq, kv, qg, kg):
    w = XA_HEADS * LANE
    os_ = []
    for h in range(XA_HEADS):
        ls = slice(h * LANE, (h + 1) * LANE)
        q = _rms(xq[:, ls], qg, XA_HD)
        k = _rms(kv[:, ls], kg, XA_HD)
        s = mm.nt(q, k) * (XA_HD ** -0.5)
        p = jnp.exp(s - jnp.max(s, axis=-1, keepdims=True))
        p = p / jnp.sum(p, axis=-1, keepdims=True)
        os_.append(mm.nn(p, kv[:, w + h * LANE:w + (h + 1) * LANE]))
    return (jnp.concatenate(os_, axis=1),)


def _dil_geometry(s, w, r, g, v_cols):
    hb = DIL_HEADS if r == 1 else 1
    rb = DIL_BLOCK * r
    nb = s // rb
    bw = hb * LANE
    v_col0 = g * (w // bw)
    assert v_cols % bw == 0 and s % rb == 0
    return hb, rb, nb, bw, v_col0


def _sub(r, res):
    return pl.ds(res, DIL_BLOCK, stride=r) if r > 1 else slice(None)


def dil_attn(name, q, k, v, r, g):
    s, w = q.shape
    hb, rb, nb, bw, v_col0 = _dil_geometry(s, w, r, g, v.shape[1])

    def body(q_r, kp_r, kc_r, vp_r, vc_r, o_r, l_r):
        lim = jnp.where(pl.program_id(1) == 0, DIL_BLOCK, 0)
        for res in range(r):
            rows = _sub(r, res)
            for h in range(hb):
                ls = slice(h * LANE, (h + 1) * LANE)
                o, lse = _dil_block(q_r[rows, ls], kp_r[rows, ls], kc_r[rows, ls], vp_r[rows, ls], vc_r[rows, ls], lim)
                o_r[rows, ls] = o
                l_r[rows, ls] = lse

    cur = pl.BlockSpec((rb, bw), lambda hblk, n: (n, hblk))
    prev = pl.BlockSpec((rb, bw), lambda hblk, n: (jnp.maximum(n - 1, 0), hblk))
    vcur = pl.BlockSpec((rb, bw), lambda hblk, n: (n, v_col0 + hblk))
    vprev = pl.BlockSpec((rb, bw), lambda hblk, n: (jnp.maximum(n - 1, 0), v_col0 + hblk))
    return pl.pallas_call(
        body, name=name, grid=(w // bw, nb), in_specs=[cur, prev, cur, vprev, vcur], out_specs=[cur, cur],
        out_shape=[jax.ShapeDtypeStruct((s, w), F32)] * 2,
        compiler_params=_cparams(("parallel", "parallel")),
    )(q, k, k, v, v)


def dil_attn_bwd(name, q, k, v, do, dlse, r, g):
    s, w = q.shape
    hb, rb, nb, bw, v_col0 = _dil_geometry(s, w, r, g, v.shape[1])

    def body(q_r, kp_r, kc_r, vp_r, vc_r, do_r, dl_r, dq_r, dk_r, dv_r, ck, cv):
        i = pl.program_id(1)
        lim = jnp.where(i == nb - 1, DIL_BLOCK, 0)

        @pl.when(i == 0)
        def _():
            ck[...] = jnp.zeros_like(ck)
            cv[...] = jnp.zeros_like(cv)

        for res in range(r):
            rows = _sub(r, res)
            for h in range(hb):
                ls = slice(h * LANE, (h + 1) * LANE)
                _, vjp = jax.vjp(functools.partial(_dil_block, lim=lim),
                                 q_r[rows, ls], kp_r[rows, ls], kc_r[rows, ls], vp_r[rows, ls], vc_r[rows, ls])
                gq, gkp, gkc, gvp, gvc = vjp((do_r[rows, ls], dl_r[rows, ls]))
                dq_r[rows, ls] = gq
                dk_r[rows, ls] = gkc + ck[rows, ls]
                dv_r[rows, ls] = gvc + cv[rows, ls]
                ck[rows, ls] = gkp
                cv[rows, ls] = gvp

    cur = pl.BlockSpec((rb, bw), lambda hblk, i: (nb - 1 - i, hblk))
    prev = pl.BlockSpec((rb, bw), lambda hblk, i: (jnp.maximum(nb - 2 - i, 0), hblk))
    vcur = pl.BlockSpec((rb, bw), lambda hblk, i: (nb - 1 - i, v_col0 + hblk))
    vprev = pl.BlockSpec((rb, bw), lambda hblk, i: (jnp.maximum(nb - 2 - i, 0), v_col0 + hblk))
    return pl.pallas_call(
        body, name=name, grid=(w // bw, nb), in_specs=[cur, prev, cur, vprev, vcur, cur, cur],
        out_specs=[cur, cur, cur], out_shape=[jax.ShapeDtypeStruct((s, w), F32)] * 3,
        scratch_shapes=[pltpu.VMEM((rb, bw), F32)] * 2,
        compiler_params=_cparams(("parallel", "arbitrary")),
    )(q, k, k, v, v, do, dlse)


def _dsilu(u):
    sg = _sigmoid(u)
    return sg * (1.0 + u * (1.0 - sg))


def _ride(body, rider, n_in, n_out, n_scratch, grid):
    if rider is None:
        return body, [], [], [], [], [], None
    ni, no = len(rider.ins), len(rider.out_shapes)

    def wrapped(*refs):
        k_in, r_in = refs[:n_in], refs[n_in:n_in + ni]
        k_out, r_out = refs[n_in + ni:n_in + ni + n_out], refs[n_in + ni + n_out:n_in + ni + n_out + no]
        k_scr = refs[n_in + ni + n_out + no:n_in + ni + n_out + no + n_scratch]
        send_sems, recv_sems = refs[-2], refs[-1]
        first = functools.reduce(jnp.logical_and, [pl.program_id(a) == 0 for a in range(len(grid))])
        last = functools.reduce(jnp.logical_and, [pl.program_id(a) == g - 1 for a, g in enumerate(grid)])

        @pl.when(first)
        def _():
            rider.start(r_in, r_out, send_sems, recv_sems)

        body(*k_in, *k_out, *k_scr)

        @pl.when(last)
        def _():
            rider.finish(r_in, r_out, send_sems, recv_sems)

    sems = [pltpu.SemaphoreType.DMA((rider.n_sems,)), pltpu.SemaphoreType.DMA((rider.n_sems,))]
    return wrapped, rider.ins, [_ANY] * ni, [_ANY] * no, rider.out_shapes, sems, ("arbitrary",) * len(grid)


CONV_STRIP = 16


def _shifted_rows(prev8, cur_r, next8, lanes, s0, n, sh, block):
    if s0 - sh < 0:
        assert s0 == 0
        xp = jnp.concatenate([prev8, cur_r[0:n, lanes]], axis=0)
        return pltpu.roll(xp, sh, 0)[SUBLANE:SUBLANE + n]
    if s0 - sh + n > block:
        assert s0 == block and n == SUBLANE
        xp = jnp.concatenate([cur_r[block - SUBLANE:block, lanes], next8], axis=0)
        return (pltpu.roll(xp, sh, 0) if sh else xp)[SUBLANE:]
    return cur_r[pl.ds(s0 - sh, n), lanes]


def conv_fwd(name, x, w, b, mode, out_dtype, tc, block=ROW_BLOCK, rider=None):
    s, c = x.shape
    ntap = w.shape[0]
    block = min(block, s)
    f = c // 2 if mode == 'glu' else c
    nh = 2 if mode == 'glu' else 1
    off = f // tc

    def body(*refs):
        first = pl.program_id(1) == 0
        o_ref = refs[-1]

        def column(cidx, carry):
            lanes = pl.ds(pl.multiple_of(cidx * LANE, LANE), LANE)
            prevs = [jnp.where(first, 0.0, refs[4 * hlf][:, lanes]) for hlf in range(nh)]
            for s0 in range(0, block, CONV_STRIP):
                us = []
                for hlf in range(nh):
                    _, cur_r, w_r, b_r = refs[4 * hlf:4 * hlf + 4]
                    acc = b_r[:, lanes]
                    for j in range(ntap):
                        xs = _shifted_rows(prevs[hlf], cur_r, None, lanes, s0, CONV_STRIP, ntap - 1 - j, block)
                        acc = acc + w_r[j:j + 1, lanes] * xs
                    us.append(acc)
                res = _silu(us[0]) * us[1] if mode == 'glu' else _silu(us[0])
                o_ref[pl.ds(s0, CONV_STRIP), lanes] = res.astype(o_ref.dtype)
            return carry

        lax.fori_loop(0, tc // LANE, column, 0)

    rb = block // SUBLANE
    ins, specs = [], []
    for hlf in range(nh):
        o = hlf * off
        ins += [x, x, w, b]
        specs += [pl.BlockSpec((SUBLANE, tc), lambda j, i, o=o: (jnp.maximum(i * rb - 1, 0), j + o)),
                  pl.BlockSpec((block, tc), lambda j, i, o=o: (i, j + o)),
                  pl.BlockSpec((ntap, tc), lambda j, i, o=o: (0, j + o)),
                  pl.BlockSpec((1, tc), lambda j, i, o=o: (0, j + o))]
    grid = (f // tc, s // block)
    body, r_ins, r_in_specs, r_out_specs, r_out_shapes, r_scratch, sem = _ride(body, rider, len(ins), 1, 0, grid)
    res = pl.pallas_call(
        body, name=name, grid=grid, in_specs=specs + r_in_specs,
        out_specs=[pl.BlockSpec((block, tc), lambda j, i: (i, j))] + r_out_specs,
        out_shape=[jax.ShapeDtypeStruct((s, f), out_dtype)] + r_out_shapes, scratch_shapes=r_scratch,
        compiler_params=_cparams(sem or ("parallel", "parallel")),
    )(*ins, *r_ins)
    return res[0] if rider is None else (res[0], rider.results(res[1:]))


def conv_bwd(name, x, w, b, dout, mode, tc, block=ROW_BLOCK, rider=None):
    s, c = x.shape
    ntap = w.shape[0]
    block = min(block, s)
    nblk = s // block
    f = c // 2 if mode == 'glu' else c
    nh = 2 if mode == 'glu' else 1
    off = f // tc
    ext = block + SUBLANE

    def body(*refs):
        i = pl.program_id(1)
        first, last = i == 0, i == nblk - 1
        dcur_r, dnext_r = refs[5 * nh], refs[5 * nh + 1]
        outs = refs[5 * nh + 2:5 * nh + 2 + 3 * nh]
        du_scr = refs[5 * nh + 2 + 3 * nh:]

        @pl.when(first)
        def _():
            for hlf in range(nh):
                outs[3 * hlf + 1][...] = jnp.zeros_like(outs[3 * hlf + 1])
                outs[3 * hlf + 2][...] = jnp.zeros_like(outs[3 * hlf + 2])

        def column(cidx, carry):
            lanes = pl.ds(pl.multiple_of(cidx * LANE, LANE), LANE)
            prevs = [jnp.where(first, 0.0, refs[5 * hlf][:, lanes]) for hlf in range(nh)]
            nexts = [jnp.where(last, 0.0, refs[5 * hlf + 2][:, lanes]) for hlf in range(nh)]
            db_acc = [jnp.zeros((CONV_STRIP, LANE), F32) for _ in range(nh)]
            dw_acc = [[jnp.zeros((CONV_STRIP, LANE), F32) for _ in range(ntap)] for _ in range(nh)]
            for s0 in range(0, ext, CONV_STRIP):
                n = min(CONV_STRIP, ext - s0)
                d_e = dcur_r[pl.ds(s0, n), lanes] if s0 < block else jnp.where(last, 0.0, dnext_r[:, lanes])
                xs, us = [], []
                for hlf in range(nh):
                    cur_r, w_r, b_r = refs[5 * hlf + 1], refs[5 * hlf + 3], refs[5 * hlf + 4]
                    sh_rows = [_shifted_rows(prevs[hlf], cur_r, nexts[hlf], lanes, s0, n, ntap - 1 - j, block)
                               for j in range(ntap)]
                    acc = b_r[:, lanes]
                    for j in range(ntap):
                        acc = acc + w_r[j:j + 1, lanes] * sh_rows[j]
                    xs.append(sh_rows)
                    us.append(acc)
                dus = [d_e * us[1] * _dsilu(us[0]), d_e * _silu(us[0])] if mode == 'glu' else [d_e * _dsilu(us[0])]
                for hlf in range(nh):
                    du_scr[hlf][pl.ds(s0, n), lanes] = dus[hlf]
                    if s0 < block:
                        db_acc[hlf] = db_acc[hlf] + dus[hlf]
                        for j in range(ntap):
                            dw_acc[hlf][j] = dw_acc[hlf][j] + dus[hlf] * xs[hlf][j]
            for hlf in range(nh):
                w_r = refs[5 * hlf + 3]
                dx_r, dw_r, db_r = outs[3 * hlf:3 * hlf + 3]
                db_r[:, lanes] += jnp.sum(db_acc[hlf], axis=0, keepdims=True)
                for j in range(ntap):
                    dw_r[j:j + 1, lanes] += jnp.sum(dw_acc[hlf][j], axis=0, keepdims=True)
                for s0 in range(0, block, CONV_STRIP):
                    dx = None
                    for j in range(ntap):
                        term = w_r[j:j + 1, lanes] * du_scr[hlf][pl.ds(s0 + ntap - 1 - j, CONV_STRIP), lanes]
                        dx = term if dx is None else dx + term
                    dx_r[pl.ds(s0, CONV_STRIP), lanes] = dx.astype(dx_r.dtype)
            return carry

        lax.fori_loop(0, tc // LANE, column, 0)

    rb = block // SUBLANE
    nrow8 = s // SUBLANE
    ins, specs = [], []
    for hlf in range(nh):
        o = hlf * off
        ins += [x, x, x, w, b]
        specs += [pl.BlockSpec((SUBLANE, tc), lambda j, i, o=o: (jnp.maximum(i * rb - 1, 0), j + o)),
                  pl.BlockSpec((block, tc), lambda j, i, o=o: (i, j + o)),
                  pl.BlockSpec((SUBLANE, tc), lambda j, i, o=o: (jnp.minimum((i + 1) * rb, nrow8 - 1), j + o)),
                  pl.BlockSpec((ntap, tc), lambda j, i, o=o: (0, j + o)),
                  pl.BlockSpec((1, tc), lambda j, i, o=o: (0, j + o))]
    ins += [dout, dout]
    specs += [pl.BlockSpec((block, tc), lambda j, i: (i, j)),
              pl.BlockSpec((SUBLANE, tc), lambda j, i: (jnp.minimum((i + 1) * rb, nrow8 - 1), j))]
    out_specs, out_shape = [], []
    for hlf in range(nh):
        out_specs += [pl.BlockSpec((block, tc), lambda j, i: (i, j)), pl.BlockSpec((ntap, tc), lambda j, i: (0, j)),
                      pl.BlockSpec((1, tc), lambda j, i: (0, j))]
        out_shape += [jax.ShapeDtypeStruct((s, f), MXU_DTYPE), jax.ShapeDtypeStruct((ntap, f), F32),
                      jax.ShapeDtypeStruct((1, f), F32)]
    grid = (f // tc, nblk)
    body, r_ins, r_in_specs, r_out_specs, r_out_shapes, r_scratch, sem = _ride(body, rider, len(ins), 3 * nh, nh, grid)
    res = pl.pallas_call(
        body, name=name, grid=grid, in_specs=specs + r_in_specs, out_specs=out_specs + r_out_specs,
        out_shape=out_shape + r_out_shapes, scratch_shapes=[pltpu.VMEM((ext, tc), F32)] * nh + r_scratch,
        compiler_params=_cparams(sem or ("parallel", "arbitrary")),
    )(*ins, *r_ins)
    rode = None if rider is None else rider.results(res[3 * nh:])
    if nh == 1:
        return [res[0]], res[1], res[2], rode
    return ([res[0], res[3]], jnp.concatenate([res[1], res[4]], axis=1), jnp.concatenate([res[2], res[5]], axis=1),
            rode)


def loss_head(y, target, block=ROW_BLOCK):
    s, d = y.shape
    block = min(block, s)

    def body(y_r, t_r, acc_r, dy_r):
        e = y_r[...] - t_r[...]
        dy_r[...] = e * (1.0 / d)

        @pl.when(pl.program_id(0) == 0)
        def _():
            acc_r[...] = jnp.zeros_like(acc_r)

        acc_r[...] += jnp.sum((e * e).reshape(block // SUBLANE, SUBLANE, d), axis=0) * (0.5 / d)

    return pl.pallas_call(
        body, name="loss_head", grid=(s // block,),
        in_specs=[pl.BlockSpec((block, d), lambda i: (i, 0))] * 2,
        out_specs=[pl.BlockSpec((SUBLANE, d), lambda i: (0, 0)), pl.BlockSpec((block, d), lambda i: (i, 0))],
        out_shape=[jax.ShapeDtypeStruct((SUBLANE, d), F32), jax.ShapeDtypeStruct((s, d), F32)],
        compiler_params=_cparams(("arbitrary",)),
    )(y, target)


def adamw(name, w, g, m, v):
    r, c = w.shape
    tr = r if r <= 512 else _tile(r, (512, 256, 128, 64, 32, 16, 8))
    if c * tr * 4 > (1 << 21):
        tr = _tile(r, (256, 128, 64, 32, 16, 8))

    def body(w_r, g_r, m_r, v_r, d_r, nm_r, nv_r):
        gg = g_r[...]
        nm = ADAM_B1 * m_r[...] + (1.0 - ADAM_B1) * gg
        nv = ADAM_B2 * v_r[...] + (1.0 - ADAM_B2) * (gg * gg)
        m_hat = nm / (1.0 - ADAM_B1 ** ADAM_STEP)
        v_hat = nv / (1.0 - ADAM_B2 ** ADAM_STEP)
        d_r[...] = -ADAM_LR * (m_hat / (jnp.sqrt(v_hat) + ADAM_EPS) + ADAM_WD * w_r[...])
        nm_r[...] = nm
        nv_r[...] = nv

    spec = pl.BlockSpec((tr, c), lambda i: (i, 0))
    return pl.pallas_call(
        body, name=name, grid=(r // tr,), in_specs=[spec] * 4, out_specs=[spec] * 3,
        out_shape=[jax.ShapeDtypeStruct((r, c), F32)] * 3, compiler_params=_cparams(("parallel",)),
    )(w, g, m, v)


MESH = pl.DeviceIdType.MESH
_ANY = pl.BlockSpec(memory_space=pl.ANY)


def _place():
    return lax.axis_index("x"), lax.axis_index("y"), lax.axis_index("c")


class Packed:
    def __init__(self, shard_shape):
        self.r, self.c = shard_shape
        self.h = self.r // 2
        self.whole = (N_CHIPS, self.r, self.c)
        self.got = (N_CHIPS, self.h, self.c)
        self.slab_half = (self.h, self.c)

    def shard_half(self, ref, core):
        return ref.at[pl.ds(core * self.h, self.h)]

    def whole_half(self, ref, chip, core):
        return ref.at[chip, pl.ds(core * self.h, self.h)]

    def place(self, whole, shard, chip):
        return lax.dynamic_update_slice(whole, shard[None], (chip, 0, 0))

    def grad_half(self, ref, core):
        return ref.at[:, core]

    def pair_slab(self, ref, chip):
        return ref.at[chip]


class SlabCols:
    def __init__(self, shard_shape):
        self.r, self.c = shard_shape
        self.h = self.r // 2
        self.whole = (self.r, N_CHIPS * self.c)
        self.got = (self.h, N_CHIPS * self.c)
        self.slab_half = (self.h, self.c)

    def _cols(self, chip):
        return pl.ds(pl.multiple_of(chip * self.c, LANE), self.c)

    def shard_half(self, ref, core):
        return ref.at[pl.ds(core * self.h, self.h)]

    def whole_half(self, ref, chip, core):
        return ref.at[pl.ds(core * self.h, self.h), self._cols(chip)]

    def place(self, whole, shard, chip):
        return lax.dynamic_update_slice_in_dim(whole, shard, chip * self.c, 1)

    def grad_half(self, ref, core):
        return ref.at[pl.ds(core * self.h, self.h)]

    def pair_slab(self, ref, chip):
        return ref.at[:, self._cols(chip)]


class GatherRider:
    def __init__(self, shards, kinds):
        self.ins, self.kinds, n = list(shards), kinds, len(shards)
        self.out_shapes = [jax.ShapeDtypeStruct(k.whole, s.dtype) for k, s in zip(kinds, shards)]
        self.n_sems = 6 * n

    def _copies(self, w_refs, out_refs, send_sems, recv_sems):
        x, y, cc = _place()
        chips = [(1 - x, y), (x, 1 - y), (1 - x, 1 - y)]

        def copy(t, k, chip, core, to, src=None):
            dst = self.kinds[t].whole_half(out_refs[t], 2 * chip[0] + chip[1], core)
            return pltpu.make_async_remote_copy(
                src_ref=dst if src is None else src, dst_ref=dst, send_sem=send_sems.at[6 * t + k],
                recv_sem=recv_sems.at[6 * t + k], device_id=to, device_id_type=MESH)

        first = [copy(t, j, (x, y), cc, (*chip, cc), src=self.kinds[t].shard_half(w_refs[t], cc))
                 for t in range(len(self.ins)) for j, chip in enumerate(chips)]
        return copy, first, chips, (x, y, cc)

    def start(self, w_refs, out_refs, send_sems, recv_sems):
        for cp in self._copies(w_refs, out_refs, send_sems, recv_sems)[1]:
            cp.start()

    def finish(self, w_refs, out_refs, send_sems, recv_sems):
        copy, first, chips, (x, y, cc) = self._copies(w_refs, out_refs, send_sems, recv_sems)
        passed = []
        for t in range(len(self.ins)):
            for j, chip in enumerate(chips):
                copy(t, j, chip, cc, (x, y, cc)).wait_recv()
                passed.append(copy(t, 3 + j, chip, cc, (x, y, 1 - cc)))
                passed[-1].start()
        for t in range(len(self.ins)):
            for j, chip in enumerate(chips):
                copy(t, 3 + j, chip, 1 - cc, (x, y, cc)).wait_recv()
        for cp in first + passed:
            cp.wait_send()

    def results(self, outs):
        chip = 2 * lax.axis_index("x") + lax.axis_index("y")
        return [k.place(o, s, chip) for k, o, s in zip(self.kinds, outs, self.ins)]


class ExchangeRider:
    def __init__(self, pairs, kinds):
        self.ins, self.kinds = list(pairs), kinds
        self.out_shapes = [jax.ShapeDtypeStruct((N_CHIPS,) + k.slab_half, p.dtype) for k, p in zip(kinds, pairs)]
        self.n_sems = 3 * len(pairs)

    def start(self, p_refs, out_refs, send_sems, recv_sems):
        x, y, cc = _place()
        for t in range(len(self.ins)):
            for j, chip in enumerate([(1 - x, y), (x, 1 - y), (1 - x, 1 - y)]):
                pltpu.make_async_remote_copy(
                    src_ref=self.kinds[t].pair_slab(p_refs[t], 2 * chip[0] + chip[1]), dst_ref=out_refs[t].at[2 * x + y],
                    send_sem=send_sems.at[3 * t + j], recv_sem=recv_sems.at[3 * t + j], device_id=(*chip, cc),
                    device_id_type=MESH).start()

    def finish(self, p_refs, out_refs, send_sems, recv_sems):
        x, y, cc = _place()
        me = 2 * x + y
        for t in range(len(self.ins)):
            for j, chip in enumerate([(1 - x, y), (x, 1 - y), (1 - x, 1 - y)]):
                them = 2 * chip[0] + chip[1]
                pltpu.make_async_remote_copy(
                    src_ref=self.kinds[t].pair_slab(p_refs[t], them), dst_ref=out_refs[t].at[them],
                    send_sem=send_sems.at[3 * t + j], recv_sem=recv_sems.at[3 * t + j], device_id=(x, y, cc),
                    device_id_type=MESH).wait()

    def results(self, outs):
        return list(outs)


def run_rider(rider, name):
    n, no = len(rider.ins), len(rider.out_shapes)

    def body(*refs):
        parts = (refs[:n], refs[n:n + no], refs[n + no], refs[n + no + 1])
        rider.start(*parts)
        rider.finish(*parts)

    outs = pl.pallas_call(
        body, name=name, in_specs=[_ANY] * n, out_specs=[_ANY] * no, out_shape=rider.out_shapes,
        scratch_shapes=[pltpu.SemaphoreType.DMA((rider.n_sems,)), pltpu.SemaphoreType.DMA((rider.n_sems,))],
    )(*rider.ins)
    return rider.results(outs)


def allgather_devices(buf):
    r, c = buf.shape

    def body(b_ref, out_ref, send_sems, recv_sems, local_sem):
        x, y, cc = _place()
        me = 4 * x + 2 * y + cc
        mine = pltpu.make_async_copy(b_ref, out_ref.at[me], local_sem)
        mine.start()
        copies = []
        for k in range(1, N_DEV):
            px, py, pc = x ^ (k >> 2), y ^ ((k >> 1) & 1), cc ^ (k & 1)
            cp = pltpu.make_async_remote_copy(src_ref=b_ref, dst_ref=out_ref.at[me], send_sem=send_sems.at[k - 1],
                                              recv_sem=recv_sems.at[k - 1], device_id=(px, py, pc), device_id_type=MESH)
            cp.start()
            copies.append((cp, 4 * px + 2 * py + pc))
        for k, (cp, peer) in enumerate(copies):
            pltpu.make_async_remote_copy(src_ref=b_ref, dst_ref=out_ref.at[peer], send_sem=send_sems.at[k],
                                         recv_sem=recv_sems.at[k], device_id=(x, y, cc), device_id_type=MESH).wait_recv()
        for cp, _ in copies:
            cp.wait_send()
        mine.wait()

    return pl.pallas_call(
        body, name="allgather_devices", in_specs=[_ANY], out_specs=_ANY,
        out_shape=jax.ShapeDtypeStruct((N_DEV, r, c), buf.dtype),
        scratch_shapes=[pltpu.SemaphoreType.DMA((N_DEV - 1,)), pltpu.SemaphoreType.DMA((N_DEV - 1,)),
                        pltpu.SemaphoreType.DMA],
    )(buf)


def swap_halves_sibling(gs, kinds, name):
    n = len(gs)

    def body(*refs):
        g_refs, out_refs, send_sems, recv_sems = refs[:n], refs[n:2 * n], refs[2 * n], refs[2 * n + 1]
        x, y, cc = _place()
        cps = []
        for t in range(n):
            cps.append(pltpu.make_async_remote_copy(
                src_ref=kinds[t].grad_half(g_refs[t], 1 - cc), dst_ref=out_refs[t], send_sem=send_sems.at[t],
                recv_sem=recv_sems.at[t], device_id=(x, y, 1 - cc), device_id_type=MESH))
            cps[-1].start()
        for cp in cps:
            cp.wait()

    return pl.pallas_call(
        body, name=name, in_specs=[_ANY] * n, out_specs=[_ANY] * n,
        out_shape=[jax.ShapeDtypeStruct(k.got, g.dtype) for k, g in zip(kinds, gs)],
        scratch_shapes=[pltpu.SemaphoreType.DMA((n,)), pltpu.SemaphoreType.DMA((n,))],
    )(*gs)


def _row_tile(n, limit=512):
    return max(t for t in range(16, limit + 1, 16) if n % t == 0)


def sum_chips(got, own, kind, chip, name):
    def body(chip_ref, got_r, own_r, out_r):
        mine = own_r[...].astype(F32)
        acc = None
        for k in range(N_CHIPS):
            term = jnp.where(chip_ref[0] == k, mine, got_r[k].astype(F32))
            acc = term if acc is None else acc + term
        out_r[...] = acc

    if isinstance(kind, Packed):
        r, c = kind.slab_half
        tr = _row_tile(r)
        grid = (r // tr,)
        specs = [pl.BlockSpec((N_CHIPS, tr, c), lambda i, chip_ref: (0, i, 0)),
                 pl.BlockSpec((None, tr, c), lambda i, chip_ref: (chip_ref[0], i, 0))]
        out_spec = pl.BlockSpec((tr, c), lambda i, chip_ref: (i, 0))
    else:
        r, c = kind.slab_half
        tr = _row_tile(r, 256)
        grid = (r // tr,)
        specs = [pl.BlockSpec((N_CHIPS, tr, c), lambda i, chip_ref: (0, i, 0)),
                 pl.BlockSpec((tr, c), lambda i, chip_ref: (i, chip_ref[0]))]
        out_spec = pl.BlockSpec((tr, c), lambda i, chip_ref: (i, 0))
    return pl.pallas_call(
        body, name=name,
        grid_spec=pltpu.PrefetchScalarGridSpec(num_scalar_prefetch=1, grid=grid, in_specs=specs, out_specs=out_spec),
        out_shape=jax.ShapeDtypeStruct(kind.slab_half, F32),
        compiler_params=_cparams(("parallel",) * len(grid)),
    )(chip, got, own)


def join_halves_sibling(halves):
    n = len(halves)

    def body(*refs):
        h_refs, out_refs, send_sems, recv_sems = refs[:n], refs[n:2 * n], refs[2 * n], refs[2 * n + 1]
        x, y, cc = _place()
        cps = []
        for t in range(n):
            cps.append(pltpu.make_async_remote_copy(
                src_ref=h_refs[t], dst_ref=out_refs[t].at[cc], send_sem=send_sems.at[t], recv_sem=recv_sems.at[t],
                device_id=(x, y, 1 - cc), device_id_type=MESH))
            cps[-1].start()
        for t in range(n):
            pltpu.make_async_remote_copy(
                src_ref=h_refs[t], dst_ref=out_refs[t].at[1 - cc], send_sem=send_sems.at[t], recv_sem=recv_sems.at[t],
                device_id=(x, y, cc), device_id_type=MESH).wait_recv()
        for cp in cps:
            cp.wait_send()

    outs = pl.pallas_call(
        body, name="join_halves_sibling", in_specs=[_ANY] * n, out_specs=[_ANY] * n,
        out_shape=[jax.ShapeDtypeStruct((2,) + h.shape, h.dtype) for h in halves],
        scratch_shapes=[pltpu.SemaphoreType.DMA((n,)), pltpu.SemaphoreType.DMA((n,))],
    )(*halves)
    core = lax.axis_index("c")
    return [lax.dynamic_update_slice_in_dim(o, h[None], core, 0) for o, h in zip(outs, halves)]


def add_own_half(g, got, kind, core, out_dtype, name):
    def body(c_ref, g_r, o_r, out_r):
        out_r[...] = (g_r[...] + o_r[...]).astype(out_r.dtype)

    if isinstance(kind, Packed):
        r, c = kind.slab_half
        tr = _row_tile(r)
        grid = (N_CHIPS, r // tr)
        specs = [pl.BlockSpec((None, None, tr, c), lambda i, j, c_ref: (i, c_ref[0], j, 0)),
                 pl.BlockSpec((None, tr, c), lambda i, j, c_ref: (i, j, 0))]
        out_spec = pl.BlockSpec((None, tr, c), lambda i, j, c_ref: (i, j, 0))
    else:
        h, c4 = kind.got
        tr = _row_tile(h, 128)
        grid = (1, h // tr)
        specs = [pl.BlockSpec((tr, c4), lambda i, j, c_ref: (c_ref[0] * (h // tr) + j, 0)),
                 pl.BlockSpec((tr, c4), lambda i, j, c_ref: (j, 0))]
        out_spec = pl.BlockSpec((tr, c4), lambda i, j, c_ref: (j, 0))
    return pl.pallas_call(
        body, name=name,
        grid_spec=pltpu.PrefetchScalarGridSpec(num_scalar_prefetch=1, grid=grid, in_specs=specs, out_specs=out_spec),
        out_shape=jax.ShapeDtypeStruct(kind.got, out_dtype),
        compiler_params=_cparams(("parallel", "parallel")),
    )(core, g, got)


def sum_slabs(p, name):
    n, r, c = p.shape
    tr = _tile(r, [t for t in (512, 256, 128, 64, 32, 16) if n * t * c * p.dtype.itemsize <= (1 << 23)])

    def body(p_r, out_r):
        acc = p_r[0].astype(F32)
        for k in range(1, n):
            acc = acc + p_r[k].astype(F32)
        out_r[...] = acc

    return pl.pallas_call(
        body, name=name, grid=(r // tr,), in_specs=[pl.BlockSpec((n, tr, c), lambda i: (0, i, 0))],
        out_specs=pl.BlockSpec((tr, c), lambda i: (i, 0)), out_shape=jax.ShapeDtypeStruct((r, c), F32),
        compiler_params=_cparams(("parallel",)),
    )(p)


def _lay(arr, axis, pieces, total, reps=()):
    items = [(d, n, lax.slice_in_dim(arr, s0, s0 + n, axis=axis)) for s0, n, d in pieces]
    items += [(d, n, jnp.repeat(lax.slice_in_dim(arr, s0, s0 + 1, axis=axis), n, axis=axis)) for s0, d, n in reps]
    items.sort(key=lambda t: t[0])
    parts, pos = [], 0

    def zeros(n):
        sh = list(arr.shape)
        sh[axis] = n
        return jnp.zeros(sh, arr.dtype)

    for d, n, v in items:
        if d > pos:
            parts.append(zeros(d - pos))
        parts.append(v)
        pos = d + n
    if total > pos:
        parts.append(zeros(total - pos))
    return jnp.concatenate(parts, axis=axis) if len(parts) > 1 else parts[0]


def _unlay_parts(g, axis, pieces, reps=()):
    out = [(s0, lax.slice_in_dim(g, d, d + n, axis=axis)) for s0, n, d in pieces]
    out += [(s0, jnp.sum(lax.slice_in_dim(g, d, d + n, axis=axis), axis=axis, keepdims=True)) for s0, d, n in reps]
    return out


def _join(parts, axis):
    parts = sorted(parts, key=lambda t: t[0])
    return jnp.concatenate([p for _, p in parts], axis=axis)


def _heads(src0, n_heads, width, padded, dst0=0):
    return [(src0 + h * width, width, dst0 + h * padded) for h in range(n_heads)]


_XQ = lambda src0: _heads(src0, XA_HEADS, XA_HD, LANE)
_XA_W = XA_HEADS * LANE

LAYOUT = {
    'a': dict(
        segs=dict(q=(_heads(0, 4, 96, LANE), 512, ()), k=(_heads(384, 4, 96, LANE), 512, ()),
                  v=(_heads(768, 4, 192, 256), 1024, ()), glr=([(1536, 16, 0)], LANE, ()),
                  og=(_heads(1552, 4, 192, 256), 1024, ()), xq=(_XQ(2320), _XA_W, ())),
        tok=(_heads(0, 4, 192, 256), 1024), xa=(_XQ(768), _XA_W)),
    'b': dict(
        segs=dict(q=([(0, 1536, 0)], 1536, ()), k=([(1536, 1536, 0)], 1536, ()), v=([(3072, 1536, 0)], 1536, ()),
                  xq=(_XQ(4608), _XA_W, ())),
        tok=([(0, 512, 0)], 512), xa=(_XQ(512), _XA_W)),
    'c': dict(
        segs=dict(z=(_heads(0, 12, 64, LANE), 1536, ()),
                  xbc=(_heads(768, 12, 64, LANE) + [(1536, 256, 1536), (1792, 256, 1792)], 2048, ()),
                  dt=([(2048, 12, 0)], LANE, ()),
                  xq=(_XQ(2060), _XA_W, ())),
        tok=(_heads(0, 12, 64, LANE), 1536), xa=(_XQ(768), _XA_W)),
    'd': dict(
        segs=dict(q=([(0, 768, 0)], 768, ()), f=([(768, 768, 0)], 768, ()), i=([(1536, 768, 0)], 768, ()),
                  og=([(2304, 768, 0)], 768, ()), xq=(_XQ(3072), _XA_W, ())),
        tok=([(0, 768, 0)], 768), xa=(_XQ(768), _XA_W)),
}
KINDS = 'abcd'
_XS_PIECES = _heads(0, 12, 64, LANE)
_XBC_PIECES = _XS_PIECES + [(768, 256, 1536), (1024, 256, 1792)]
_HEAD_REPS = tuple((h, h * LANE, LANE) for h in range(12))


def _row(v):
    return v.reshape(1, -1)


class LocalFfn:
    def __init__(self, w_up, w_down):
        self.w_up, self.w_down = w_up.astype(MXU_DTYPE), w_down.astype(MXU_DTYPE)
        self.g = {}

    def weights(self, i):
        return self.w_up[i], self.w_down[i]

    def fwd_rider(self, i):
        return None

    def bwd_rider(self, i):
        return None

    def grads(self, i, g_up, g_down):
        self.g[i] = (g_up, g_down)

    def whole_grads(self):
        return jnp.stack([self.g[i][0] for i in range(4)]), jnp.stack([self.g[i][1] for i in range(4)])


class ShardedFfn:
    def __init__(self, up_shards, down_shards, layer0, core_id):
        self.up, self.down = up_shards.astype(MXU_DTYPE), down_shards.astype(MXU_DTYPE)
        self.kinds = [SlabCols(self.up.shape[1:]), Packed(self.down.shape[1:])]
        self.whole = {0: layer0}
        self.core_id = core_id
        self.pending = None
        self.recvd = {}

    def whole_down(self, gathered):
        return gathered.reshape(-1, gathered.shape[-1])

    def weights(self, i):
        return self.whole[i]

    def fwd_rider(self, i):
        return GatherRider([self.up[i + 1], self.down[i + 1]], self.kinds) if i + 1 < 4 else None

    def fwd_rode(self, i, res):
        self.whole[i + 1] = (res[0], self.whole_down(res[1]))

    def bwd_rider(self, i):
        return ExchangeRider(self.pending[1], self.kinds) if self.pending is not None else None

    def bwd_rode(self, i, res):
        self.recvd[self.pending[0]] = (res, self.pending[1])
        self.pending = None

    def grads(self, i, g_up, g_down):
        k_up, k_down = self.kinds
        gs = [g_up, g_down.reshape(N_CHIPS, 2, k_down.h, k_down.c)]
        gots = swap_halves_sibling(gs, self.kinds, f"swap_halves_ffn_{i}")
        self.pending = (i, [add_own_half(g, o, k, self.core_id, GRAD_WIRE_DTYPE, f"add_own_half_ffn_{i}_{t}")
                            for t, (g, o, k) in enumerate(zip(gs, gots, self.kinds))])


def local_step(x, mem, positions, target, W, ffn=None):
    s = x.shape[0]
    grads = {}
    scan_block = CHUNK * SCAN_CHUNKS
    ffn = ffn or LocalFfn(W['ffn_w_up'], W['ffn_w_down'])

    inv_freq = ROPE_THETA ** (-jnp.arange(DIL_HD // 2, dtype=F32) / (DIL_HD // 2))
    ang = positions.astype(F32)[:, None] * inv_freq
    cosf = jnp.concatenate([jnp.cos(ang), jnp.cos(ang)], axis=-1)
    sinf = jnp.concatenate([-jnp.sin(ang), jnp.sin(ang)], axis=-1)

    mem_g = _row(W['mem_norm'])
    (mem_n,) = tmap("mem_norm", _norm_stage, [mem], [mem_g], [(D_MODEL, MXU_DTYPE)])
    kv_lay = _heads(0, 4, 64, LANE) + _heads(256, 4, 64, LANE, dst0=_XA_W)

    saved = []
    for i in range(4):
        kind = KINDS[i]
        lay = LAYOUT[kind]
        sv = dict(x0=x)
        w_in = W[f'{kind}_w_in']
        w_out = W[f'{kind}_w_out']
        sv['w_seg'] = {n: _lay(w_in, 1, p, t, r).astype(MXU_DTYPE) for n, (p, t, r) in lay['segs'].items()}
        sv['wo_tok'] = _lay(w_out, 0, *lay['tok']).astype(MXU_DTYPE)
        sv['wo_xa'] = _lay(w_out, 0, *lay['xa']).astype(MXU_DTYPE)
        sv['w_kv'] = _lay(W['xa_w_kv'][i], 1, kv_lay, 2 * _XA_W).astype(MXU_DTYPE)
        sv['g1'] = _row(W['mix_norm'][i])
        (h,) = tmap(f"mix_norm_{i}", _norm_stage, [x], [sv['g1']], [(D_MODEL, MXU_DTYPE)])
        sv['h'] = h
        seg = {n: matmul(h, w) for n, w in sv['w_seg'].items()}
        sv['seg'] = seg

        if kind == 'a':
            sv['w2'] = _lay(_lay(W['a_w_gate2'], 1, _heads(0, 4, 96, LANE), 512), 0, [(0, 16, 0)], LANE)
            sv['bg'] = _row(_lay(W['a_b_gate'], 0, _heads(0, 4, 96, LANE), 512))
            sv['on'] = _row(_lay(W['a_o_norm'], 0, [(0, 192, 0)], 256))
            (la,) = tmap("gla_pre", _gla_pre, [seg['glr']], [sv['w2'], sv['bg']], [(512, F32)])
            sv['la'] = la
            sv['scan_fn'] = _gla_step(GLA_HEADS, 2 * LANE, GLA_DK ** -0.5)
            sv['scan_rows'] = [seg['q'], seg['k'], seg['v'], la]
            (o,), sv['states'] = rscan("gla_scan", sv['scan_fn'], [(LANE, 2 * LANE)] * GLA_HEADS, sv['scan_rows'], [],
                                       [(1024, F32)], scan_block)
            sv['o'] = o
            (tok,) = tmap("gla_post", _gla_post, [o, seg['og']], [sv['on']], [(1024, MXU_DTYPE)])
        elif kind == 'b':
            sv['qg'], sv['kg'] = _row(W['b_q_norm']), _row(W['b_k_norm'])
            os_, ls_ = [], []
            qkn = tmap("dil_pre", _dil_pre, [seg['q'], seg['k'], cosf, sinf], [sv['qg'], sv['kg']], [(512, F32)] * 6)
            sv['qn'], sv['kn'] = qkn[:3], qkn[3:]
            for g, (window, r) in enumerate(DIL_GROUPS):
                assert window // r == DIL_BLOCK and (s // r) % DIL_BLOCK == 0
                o, lse = dil_attn(f"dil_attn_{g}", sv['qn'][g], sv['kn'][g], seg['v'], r, g)
                os_.append(o)
                ls_.append(lse)
            sv['os'], sv['ls'] = os_, ls_
            (tok,) = tmap("dil_merge", _dil_merge, os_ + ls_, [], [(512, MXU_DTYPE)])
        elif kind == 'c':
            sv['cw'] = _lay(W['c_conv_w'], 1, _XBC_PIECES, 2048)
            sv['cb'] = _row(_lay(W['c_conv_b'], 0, _XBC_PIECES, 2048))
            sv['dtb'] = _row(_lay(W['c_dt_bias'], 0, [], 1536, _HEAD_REPS))
            sv['alog'] = _row(_lay(W['c_a_log'], 0, [], 1536, _HEAD_REPS))
            sv['dsk'] = _row(_lay(W['c_d'], 0, [], 1536, _HEAD_REPS))
            sv['cn'] = _row(_lay(W['c_norm'], 0, _XS_PIECES, 1536))
            xact = conv_fwd("ssm_conv", seg['xbc'], sv['cw'], sv['cb'], 'silu', F32, 512)
            sv['xact'] = xact
            sv['scan_rows'] = [xact, seg['dt']]
            sv['scan_params'] = [sv['dtb'], sv['alog'], sv['dsk']]
            (yv,), sv['states'] = rscan("ssd_scan", _ssd_step, [(LANE, LANE)] * SSM_HEADS, sv['scan_rows'],
                                        sv['scan_params'], [(1536, F32)], scan_block)
            sv['y'] = yv
            (tok,) = tmap("ssd_post", _mamba_post, [yv, seg['z']], [sv['cn']], [(1536, MXU_DTYPE)])
        else:
            sv['lbnd'] = W['d_lower_bounds']
            sv['on'] = _row(W['d_o_norm'])
            qq, kk, la = tmap("hgrn_pre", _hgrn_pre, [seg['q'], seg['f']], [sv['lbnd']], [(768, F32)] * 3)
            sv['scan_fn'] = _gla_step(HGRN_HEADS, LANE, 1.0)
            sv['scan_rows'] = [qq, kk, seg['i'], la]
            (o,), sv['states'] = rscan("hgrn_scan", sv['scan_fn'], [(LANE, LANE)] * HGRN_HEADS, sv['scan_rows'], [],
                                       [(768, F32)], scan_block)
            sv['o'] = o
            (tok,) = tmap("hgrn_post", _hgrn_post, [o, seg['og']], [sv['on']], [(768, MXU_DTYPE)])
        sv['tok'] = tok

        kv = matmul(mem_n, sv['w_kv'])
        sv['kv'] = kv
        sv['xqg'] = _row(_lay(W['xa_q_norm'][i], 0, [(0, 64, 0)], LANE))
        sv['xkg'] = _row(_lay(W['xa_k_norm'][i], 0, [(0, 64, 0)], LANE))
        (xa,) = tmap(f"xattn_{i}", _xattn, [seg['xq']], [kv, sv['xqg'], sv['xkg']], [(_XA_W, MXU_DTYPE)])
        sv['xa'] = xa
        x = matmul(tok, sv['wo_tok'], add=x)
        x = matmul(xa, sv['wo_xa'], add=x)
        sv['x1'] = x

        sv['g2'] = _row(W['ffn_norm'][i])
        sv['fcw'] = W['ffn_conv_w'][i]
        sv['fcb'] = _row(W['ffn_conv_b'][i])
        (h2,) = tmap(f"ffn_norm_{i}", _norm_stage, [x], [sv['g2']], [(D_MODEL, MXU_DTYPE)])
        sv['h2'] = h2
        w_up, w_down = ffn.weights(i)
        u0 = matmul(h2, w_up)
        sv['u0'] = u0
        rider = ffn.fwd_rider(i)
        act = conv_fwd("ffn_conv", u0, sv['fcw'], sv['fcb'], 'glu', MXU_DTYPE, 1408, rider=rider)
        if rider is not None:
            act, rode = act
            ffn.fwd_rode(i, rode)
        sv['act'] = act
        x = matmul(act, w_down, add=x)
        saved.append(sv)

    loss_acc, dx = loss_head(x, target)

    g_stack = {n: [None] * 4 for n in ('mix_norm', 'xa_w_kv', 'xa_q_norm', 'xa_k_norm', 'ffn_norm', 'ffn_conv_w',
                                        'ffn_conv_b')}
    d_memn = None
    for i in reversed(range(4)):
        kind = KINDS[i]
        lay = LAYOUT[kind]
        sv = saved[i]
        seg = sv['seg']
        w_up, w_down = ffn.weights(i)
        dact = matmul(dx, w_down, tb=True)
        g_down = matmul(sv['act'], dx, ta=True)
        rider = ffn.bwd_rider(i)
        (du_g, du_v), dcw, dcb, rode = conv_bwd("ffn_conv_bwd", sv['u0'], sv['fcw'], sv['fcb'], dact, 'glu', 1408,
                                                rider=rider)
        if rider is not None:
            ffn.bwd_rode(i, rode)
        g_stack['ffn_conv_w'][i], g_stack['ffn_conv_b'][i] = dcw, dcb[0]
        dh2 = matmul(du_v, w_up, tb=True, b_koff=D_FF, add=matmul(du_g, w_up, tb=True))
        g_up = jnp.zeros((1,) + w_up.shape, F32)
        g_up = matmul(sv['h2'], du_g, ta=True, into=(g_up, 0, 0))
        g_up = matmul(sv['h2'], du_v, ta=True, into=(g_up, 0, D_FF))
        ffn.grads(i, g_up[0], g_down)
        (dx,), (dg2,) = tmap_bwd(f"ffn_norm_bwd_{i}", _norm_stage, [sv['x1']], [sv['g2']], [dh2], [True], {0: dx})
        g_stack['ffn_norm'][i] = dg2[0]
        dtok = matmul(dx, sv['wo_tok'], tb=True)
        dxa = matmul(dx, sv['wo_xa'], tb=True)
        g_wo = _unlay_parts(matmul(sv['tok'], dx, ta=True), 0, lay['tok'][0]) \
            + _unlay_parts(matmul(sv['xa'], dx, ta=True), 0, lay['xa'][0])
        grads[f'{kind}_w_out'] = _join(g_wo, 0)
        (dxq,), (dkv, dqg, dkg) = tmap_bwd(f"xattn_bwd_{i}", _xattn, [seg['xq']], [sv['kv'], sv['xqg'], sv['xkg']],
                                           [dxa], [True])
        g_stack['xa_q_norm'][i], g_stack['xa_k_norm'][i] = dqg[0, :XA_HD], dkg[0, :XA_HD]
        g_stack['xa_w_kv'][i] = _join(_unlay_parts(matmul(mem_n, dkv, ta=True), 1, kv_lay), 1)
        d_memn = matmul(dkv, sv['w_kv'], tb=True, add=d_memn)
        dseg = dict(xq=dxq)
        if kind == 'a':
            (do, dog), (don,) = tmap_bwd("gla_post_bwd", _gla_post, [sv['o'], seg['og']], [sv['on']], [dtok],
                                         [True, True], grad_dtype=F32)
            grads['a_o_norm'] = don[0, :GLA_DV]
            (dq, dk, dv, dla), _ = rscan_bwd("gla_scan_bwd", sv['scan_fn'], sv['states'], sv['scan_rows'], [], [do],
                                             scan_block, grad_dtype=F32)
            (dglr,), (dw2, dbg) = tmap_bwd("gla_pre_bwd", _gla_pre, [seg['glr']], [sv['w2'], sv['bg']], [dla], [True])
            grads['a_w_gate2'] = _join(_unlay_parts(dw2[:GLA_RANK], 1, _heads(0, 4, 96, LANE)), 1)
            grads['a_b_gate'] = _join(_unlay_parts(dbg[0], 0, _heads(0, 4, 96, LANE)), 0)
            dseg.update(q=dq, k=dk, v=dv, glr=dglr, og=dog)
        elif kind == 'b':
            res, _ = tmap_bwd("dil_merge_bwd", _dil_merge, sv['os'] + sv['ls'], [], [dtok], [True] * 6, grad_dtype=F32)
            dqn, dkn, dvs = [], [], []
            for g, (_, r) in enumerate(DIL_GROUPS):
                a_, b_, c_ = dil_attn_bwd(f"dil_attn_bwd_{g}", sv['qn'][g], sv['kn'][g], seg['v'], res[g], res[3 + g],
                                          r, g)
                dqn.append(a_)
                dkn.append(b_)
                dvs.append(c_)
            (dq, dk), (dqg, dkg) = tmap_bwd("dil_pre_bwd", _dil_pre, [seg['q'], seg['k'], cosf, sinf],
                                            [sv['qg'], sv['kg']], dqn + dkn, [True, True, False, False])
            dseg.update(q=dq, k=dk, v=jnp.concatenate(dvs, axis=1))
            grads['b_q_norm'], grads['b_k_norm'] = dqg[0], dkg[0]
        elif kind == 'c':
            (dy, dz), (dcn,) = tmap_bwd("ssd_post_bwd", _mamba_post, [sv['y'], seg['z']], [sv['cn']], [dtok],
                                        [True, True], grad_dtype=F32)
            grads['c_norm'] = _join(_unlay_parts(dcn[0], 0, _XS_PIECES), 0)
            (dxact, ddt), (ddtb, dalog, ddsk) = rscan_bwd("ssd_scan_bwd", _ssd_step, sv['states'], sv['scan_rows'],
                                                          sv['scan_params'], [dy], scan_block, grad_dtype=F32)
            for nm, gv in (('c_dt_bias', ddtb), ('c_a_log', dalog), ('c_d', ddsk)):
                grads[nm] = _join(_unlay_parts(gv[0], 0, [], _HEAD_REPS), 0)
            (dxbc,), dcw, dcb, _ = conv_bwd("ssm_conv_bwd", seg['xbc'], sv['cw'], sv['cb'], dxact, 'silu', 512)
            grads['c_conv_w'] = _join(_unlay_parts(dcw, 1, _XBC_PIECES), 1)
            grads['c_conv_b'] = _join(_unlay_parts(dcb[0], 0, _XBC_PIECES), 0)
            dseg.update(z=dz, xbc=dxbc, dt=ddt)
        else:
            (do, dog), (don,) = tmap_bwd("hgrn_post_bwd", _hgrn_post, [sv['o'], seg['og']], [sv['on']], [dtok],
                                         [True, True], grad_dtype=F32)
            grads['d_o_norm'] = don[0]
            (dqq, dkk, di, dla), _ = rscan_bwd("hgrn_scan_bwd", sv['scan_fn'], sv['states'], sv['scan_rows'], [], [do],
                                               scan_block, grad_dtype=F32)
            (dq, df), (dlb,) = tmap_bwd("hgrn_pre_bwd", _hgrn_pre, [seg['q'], seg['f']], [sv['lbnd']], [dqq, dkk, dla],
                                        [True, True])
            grads['d_lower_bounds'] = dlb
            dseg.update(q=dq, f=df, i=di, og=dog)
        dh = None
        g_in = []
        for n, (p, t, rp) in lay['segs'].items():
            dh = matmul(dseg[n], sv['w_seg'][n], tb=True, add=dh)
            g_in += _unlay_parts(matmul(sv['h'], dseg[n], ta=True), 1, p, rp)
        grads[f'{kind}_w_in'] = _join(g_in, 1)
        (dx,), (dg1,) = tmap_bwd(f"mix_norm_bwd_{i}", _norm_stage, [sv['x0']], [sv['g1']], [dh], [True], {0: dx})
        g_stack['mix_norm'][i] = dg1[0]

    _, (dmg,) = tmap_bwd("mem_norm_bwd", _norm_stage, [mem], [mem_g], [d_memn], [False])
    grads['mem_norm'] = dmg[0]
    for n, parts in g_stack.items():
        grads[n] = jnp.stack(parts)
    if isinstance(ffn, LocalFfn):
        grads['ffn_w_up'], grads['ffn_w_down'] = ffn.whole_grads()
    return loss_acc, dx, grads


def _pack(arrs, dtype, row_multiple=PACK_ROWS):
    parts, rows = [], 0
    for a in arrs:
        f = a.reshape(-1).astype(dtype)
        unit = PACK_ROWS * PACK_COLS
        pad = (-f.shape[0]) % unit
        if pad:
            f = jnp.concatenate([f, jnp.zeros((pad,), dtype)])
        parts.append(f.reshape(-1, PACK_COLS))
        rows += parts[-1].shape[0]
    if rows % row_multiple:
        parts.append(jnp.zeros((row_multiple - rows % row_multiple, PACK_COLS), dtype))
    return jnp.concatenate(parts, axis=0)


def _unpack(buf, shapes):
    out, row = [], 0
    for sh in shapes:
        n = int(np.prod(sh))
        rows = -(-n // (PACK_ROWS * PACK_COLS)) * PACK_ROWS
        out.append(buf[row:row + rows].reshape(-1)[:n].reshape(sh))
        row += rows
    return out


def _pack_rows(arrs):
    parts = []
    for a in arrs:
        f = a.reshape(-1).astype(F32)
        parts.append(jnp.pad(f, (0, (-f.shape[0]) % PACK_COLS)))
    flat = jnp.concatenate(parts)
    rows = flat.shape[0] // PACK_COLS
    return jnp.pad(flat, (0, (-rows % 16) * PACK_COLS)).reshape(-1, PACK_COLS)


def _unpack_rows(buf, shapes):
    flat, out, pos = buf.reshape(-1), [], 0
    for sh in shapes:
        n = int(np.prod(sh))
        out.append(flat[pos:pos + n].reshape(sh))
        pos += -(-n // PACK_COLS) * PACK_COLS
    return out


def _split_chips(a, axis):
    sh = a.shape
    return jnp.moveaxis(a.reshape(sh[:axis] + (N_CHIPS, sh[axis] // N_CHIPS) + sh[axis + 1:]), axis, 0)


def _merge_chips(a, axis):
    a = jnp.moveaxis(a, 0, axis)
    sh = a.shape
    return a.reshape(sh[:axis] + (sh[axis] * sh[axis + 1],) + sh[axis + 2:])


def kernel(x, mem, positions, mem_norm, mix_norm, xa_w_kv, xa_q_norm, xa_k_norm, ffn_norm, ffn_w_up, ffn_conv_w, ffn_conv_b, ffn_w_down, a_w_in, a_w_gate2, a_b_gate, a_o_norm, a_w_out, b_w_in, b_q_norm, b_k_norm, b_w_out, c_w_in, c_conv_w, c_conv_b, c_dt_bias, c_a_log, c_d, c_norm, c_w_out, d_w_in, d_lower_bounds, d_o_norm, d_w_out, loss_target, m_mem_norm, m_mix_norm, m_xa_w_kv, m_xa_q_norm, m_xa_k_norm, m_ffn_norm, m_ffn_w_up, m_ffn_conv_w, m_ffn_conv_b, m_ffn_w_down, m_a_w_in, m_a_w_gate2, m_a_b_gate, m_a_o_norm, m_a_w_out, m_b_w_in, m_b_q_norm, m_b_k_norm, m_b_w_out, m_c_w_in, m_c_conv_w, m_c_conv_b, m_c_dt_bias, m_c_a_log, m_c_d, m_c_norm, m_c_w_out, m_d_w_in, m_d_lower_bounds, m_d_o_norm, m_d_w_out, v_mem_norm, v_mix_norm, v_xa_w_kv, v_xa_q_norm, v_xa_k_norm, v_ffn_norm, v_ffn_w_up, v_ffn_conv_w, v_ffn_conv_b, v_ffn_w_down, v_a_w_in, v_a_w_gate2, v_a_b_gate, v_a_o_norm, v_a_w_out, v_b_w_in, v_b_q_norm, v_b_k_norm, v_b_w_out, v_c_w_in, v_c_conv_w, v_c_conv_b, v_c_dt_bias, v_c_a_log, v_c_d, v_c_norm, v_c_w_out, v_d_w_in, v_d_lower_bounds, v_d_o_norm, v_d_w_out):
    args = locals()
    w = {n: args[n] for n in WEIGHTS}
    m = {n: args['m_' + n] for n in WEIGHTS}
    v = {n: args['v_' + n] for n in WEIGHTS}
    cx, cy, cc = lax.axis_index("x"), lax.axis_index("y"), lax.axis_index("c")
    chip = 2 * cx + cy

    core_id, chip_id = cc.reshape(1).astype(jnp.int32), chip.reshape(1).astype(jnp.int32)
    packed_names = [n for n in BIG if n not in LAYERED]
    packed_shapes = [w[n].shape for n in packed_names]
    packed_w = _pack([w[n] for n in packed_names], MXU_DTYPE, 1024)
    ffn = ShardedFfn(w['ffn_w_up'], w['ffn_w_down'], None, core_id)
    kinds = [Packed(packed_w.shape)] + ffn.kinds
    gathered, up0, down0 = run_rider(GatherRider([packed_w, ffn.up[0], ffn.down[0]], kinds), "allgather_chips")
    ffn.whole[0] = (up0, ffn.whole_down(down0))
    per_chip = [_unpack(gathered[j], packed_shapes) for j in range(N_CHIPS)]
    full = {n: _merge_chips(jnp.stack([per_chip[j][k] for j in range(N_CHIPS)]), SHARD_AXIS[n])
            for k, n in enumerate(packed_names)}
    small_sharded = [n for n in SMALL if n in SHARD_AXIS]
    sg = allgather_devices(_pack([w[n] for n in small_sharded], F32))
    per_chip_s = [_unpack(sg[2 * j], [w[n].shape for n in small_sharded]) for j in range(N_CHIPS)]
    for k, n in enumerate(small_sharded):
        full[n] = _merge_chips(jnp.stack([per_chip_s[j][k] for j in range(N_CHIPS)]), SHARD_AXIS[n])
    for n in SMALL:
        if n not in SHARD_AXIS:
            full[n] = w[n]

    loss_acc, dx, grads = local_step(x[0], mem[0], positions[0], loss_target[0], full, ffn)
    loss = lax.psum(jnp.sum(loss_acc), ("x", "y", "c"))

    gb = jnp.stack([_pack([_split_chips(grads[n], SHARD_AXIS[n])[j] for n in packed_names], F32, 1024)
                    for j in range(N_CHIPS)])
    rows = gb.shape[1]
    gb = gb.reshape(N_CHIPS, 2, rows // 2, PACK_COLS)
    (got,) = swap_halves_sibling([gb], kinds[:1], "swap_halves_packed")
    pair = add_own_half(gb, got, kinds[0], core_id, GRAD_WIRE_DTYPE, "add_own_half_packed")
    last_layer, last_pairs = ffn.pending
    recvd = run_rider(ExchangeRider([pair] + last_pairs, kinds), "exchange_chips")
    ffn.recvd[last_layer] = (recvd[1:], last_pairs)
    halves = [sum_chips(recvd[0], pair, kinds[0], chip_id, "sum_chips_packed")]
    for i in range(4):
        got_i, pairs_i = ffn.recvd[i]
        halves += [sum_chips(r, p, k, chip_id, f"sum_chips_ffn_{i}_{t}")
                   for t, (r, p, k) in enumerate(zip(got_i, pairs_i, ffn.kinds))]
    red, *red_ffn = join_halves_sibling(halves)
    g_big = dict(zip(packed_names, _unpack(red.reshape(rows, PACK_COLS), packed_shapes)))
    g_big['ffn_w_up'] = jnp.stack(red_ffn[0::2]).reshape(w['ffn_w_up'].shape)
    g_big['ffn_w_down'] = jnp.stack(red_ffn[1::2]).reshape(w['ffn_w_down'].shape)

    small_full_shapes = [grads[n].shape for n in SMALL]
    gs = sum_slabs(allgather_devices(_pack_rows([grads[n] for n in SMALL])), "sum_devices")
    g_small = {}
    for n, gfull in zip(SMALL, _unpack_rows(gs, small_full_shapes)):
        if n in SHARD_AXIS:
            ax = SHARD_AXIS[n]
            size = gfull.shape[ax] // N_CHIPS
            gfull = lax.dynamic_slice_in_dim(gfull, chip * size, size, axis=ax)
        g_small[n] = gfull

    g_out, delta, new_m, new_v = {**g_big, **g_small}, {}, {}, {}
    for n in WEIGHTS:
        sh = w[n].shape
        two_d = (-1, sh[-1])
        d_, m_, v_ = adamw(f"adamw_{n}", w[n].reshape(two_d), g_out[n].reshape(two_d), m[n].reshape(two_d),
                           v[n].reshape(two_d))
        delta[n], new_m[n], new_v[n] = d_.reshape(sh), m_.reshape(sh), v_.reshape(sh)

    return (loss, dx[None], *[g_out[n] for n in WEIGHTS], *[delta[n] for n in WEIGHTS],
            *[new_m[n] for n in WEIGHTS], *[new_v[n] for n in WEIGHTS])
```

```python
import functools
import math

import jax
import jax.numpy as jnp
import numpy as np
from jax import lax
from jax.experimental import pallas as pl
from jax.experimental.pallas import tpu as pltpu

F32 = jnp.float32
MXU_DTYPE = jnp.bfloat16
GRAD_WIRE_DTYPE = jnp.bfloat16
VMEM_LIMIT_V7X = 56 * 1024 * 1024
LANE = 128
SUBLANE = 8

D_MODEL = 1024
N_MEM = 256
EPS = 1e-6
ROPE_THETA = 10000.0
CHUNK = 64
XA_HEADS, XA_HD = 4, 64
GLA_HEADS, GLA_DK, GLA_DV, GLA_RANK, GLA_GATE_NORM = 4, 96, 192, 16, 16.0
DIL_GROUPS = ((128, 1), (512, 4), (2048, 16))
DIL_HEADS, DIL_HD, DIL_BLOCK = 4, 128, 128
SSM_HD, SSM_HEADS, SSM_GROUPS, SSM_STATE, SSM_CONV = 64, 12, 2, 128, 4
HGRN_HEADS, HGRN_DK = 6, 128
D_FF = 2816
FFN_CONV = 3
ADAM_LR, ADAM_B1, ADAM_B2, ADAM_EPS, ADAM_WD, ADAM_STEP = 0.001, 0.9, 0.999, 1e-08, 0.01, 10

MM_TILES = (2816, 1408, 1024, 768, 512, 384, 256, 128)
MM_K_TILES = (2816, 2048, 1536, 1408, 1024, 768, 512, 384, 256, 128)
MM_MIN_OUT_TILE = 512 * 1024
MM_VMEM_BUDGET = 40 * 1024 * 1024
ROW_BLOCK = 256
SCAN_CHUNKS = 2
PACK_COLS = 1024
PACK_ROWS = 32

WEIGHTS = ['mem_norm', 'mix_norm', 'xa_w_kv', 'xa_q_norm', 'xa_k_norm', 'ffn_norm', 'ffn_w_up', 'ffn_conv_w',
           'ffn_conv_b', 'ffn_w_down', 'a_w_in', 'a_w_gate2', 'a_b_gate', 'a_o_norm', 'a_w_out', 'b_w_in', 'b_q_norm',
           'b_k_norm', 'b_w_out', 'c_w_in', 'c_conv_w', 'c_conv_b', 'c_dt_bias', 'c_a_log', 'c_d', 'c_norm', 'c_w_out',
           'd_w_in', 'd_lower_bounds', 'd_o_norm', 'd_w_out']
SHARD_AXIS = {'xa_w_kv': 1, 'ffn_w_up': 2, 'ffn_conv_w': 2, 'ffn_w_down': 1, 'a_w_in': 1, 'a_w_gate2': 1, 'a_w_out': 0,
              'b_w_in': 1, 'b_w_out': 1, 'c_w_in': 1, 'c_conv_w': 1, 'c_w_out': 0, 'd_w_in': 1, 'd_w_out': 0}
BIG = ['xa_w_kv', 'ffn_w_up', 'ffn_w_down', 'a_w_in', 'a_w_gate2', 'a_w_out', 'b_w_in', 'b_w_out', 'c_w_in', 'c_w_out',
       'd_w_in', 'd_w_out']
SMALL = [n for n in WEIGHTS if n not in BIG]
LAYERED = ['ffn_w_up', 'ffn_w_down']
N_CHIPS = 4
N_DEV = 8


class _MatmulSet:
    def __init__(self, cast, precision):
        def dot(a, b, dims):
            if cast:
                a = a.astype(MXU_DTYPE)
                b = b.astype(MXU_DTYPE)
            return lax.dot_general(a, b, (dims, ((), ())), precision=precision, preferred_element_type=F32)

        @jax.custom_vjp
        def nn(a, b):
            return dot(a, b, ((1,), (0,)))

        @jax.custom_vjp
        def nt(a, b):
            return dot(a, b, ((1,), (1,)))

        @jax.custom_vjp
        def tn(a, b):
            return dot(a, b, ((0,), (0,)))

        nn.defvjp(lambda a, b: (nn(a, b), (a, b)), lambda r, g: (nt(g, r[1]), tn(r[0], g)))
        nt.defvjp(lambda a, b: (nt(a, b), (a, b)), lambda r, g: (nn(g, r[1]), tn(g, r[0])))
        tn.defvjp(lambda a, b: (tn(a, b), (a, b)), lambda r, g: (nt(r[1], g), nn(r[0], g)))
        self.nn, self.nt, self.tn = nn, nt, tn


mm = _MatmulSet(True, None)
hi = _MatmulSet(False, lax.Precision.HIGHEST)


def _sigmoid(x):
    return jax.nn.sigmoid(x)


def _silu(x):
    return x * jax.nn.sigmoid(x)


def _softplus(x):
    return jnp.maximum(x, 0.0) + jnp.log1p(jnp.exp(-jnp.abs(x)))


def _rms(x, g, n_real=None):
    n = n_real or x.shape[-1]
    ms = jnp.sum(x * x, axis=-1, keepdims=True) * (1.0 / n)
    return x * lax.rsqrt(ms + EPS) * g


@jax.custom_vjp
def _swap_halves(x):
    return pltpu.roll(x, 64, 1)


_swap_halves.defvjp(lambda x: (_swap_halves(x), None), lambda _, g: (_swap_halves(g),))


def _tile(n, cands):
    for c in cands:
        if n % c == 0:
            return c
    raise ValueError(f"no tile for {n} among {cands}")


def _cparams(sem):
    return pltpu.CompilerParams(dimension_semantics=sem, vmem_limit_bytes=VMEM_LIMIT_V7X)


def _f32(v):
    return v.astype(F32) if jnp.issubdtype(v.dtype, jnp.floating) else v


def matmul(a, b, *, ta=False, tb=False, add=None, out_dtype=F32, b_layer=None, b_koff=0, into=None):
    m, k = (a.shape[1], a.shape[0]) if ta else a.shape
    b2 = b.shape[1:] if b_layer is not None else b.shape
    n = b2[0] if tb else b2[1]
    assert b_koff + k <= (b2[1] if tb else b2[0]), (a.shape, b.shape, ta, tb, b_koff)
    sa, sb, so = a.dtype.itemsize, b.dtype.itemsize, jnp.dtype(out_dtype).itemsize
    n_align = math.gcd(n, into[2]) if into is not None and into[2] else n
    k_align = math.gcd(k, b_koff) if b_koff else k

    def vmem(tm_, tn_, tk_):
        return (2 * tm_ * tk_ * sa + 2 * tk_ * tn_ * sb + 2 * tm_ * tn_ * so + (tm_ * tn_ * 4 if tk_ < k else 0)
                + (2 * tm_ * tn_ * add.dtype.itemsize if add is not None else 0))

    for tk in [t for t in MM_K_TILES if k_align % t == 0]:
        fits = [(tm_ * tn_, tm_, tn_) for tm_ in MM_TILES if m % tm_ == 0 for tn_ in MM_TILES
                if n % tn_ == 0 and n_align % tn_ == 0 and vmem(tm_, tn_, tk) <= MM_VMEM_BUDGET]
        if fits and (max(fits)[0] >= min(MM_MIN_OUT_TILE, m * n) or tk == MM_K_TILES[-1]):
            break
    _, tm, tn = max(fits)
    nk = k // tk
    dims = (((0,) if ta else (1,)), ((1,) if tb else (0,)))
    n_extra = (add is not None) + (into is not None)

    def body(*refs):
        a_ref, b_ref = refs[0], refs[1]
        add_ref = refs[2] if add is not None else None
        o_ref = refs[2 + n_extra]
        part = lax.dot_general(a_ref[...].astype(MXU_DTYPE), b_ref[...].astype(MXU_DTYPE), (dims, ((), ())),
                               preferred_element_type=F32)

        def finish(r):
            if add_ref is not None:
                r = r + add_ref[...].astype(F32)
            o_ref[...] = r.astype(o_ref.dtype)

        if nk == 1:
            finish(part)
            return
        acc = refs[-1]
        kk = pl.program_id(2)

        @pl.when(kk == 0)
        def _():
            acc[...] = part

        @pl.when(kk > 0)
        def _():
            acc[...] += part

        @pl.when(kk == nk - 1)
        def _():
            finish(acc[...])

    a_spec = pl.BlockSpec((tk, tm), lambda i, j, q: (q, i)) if ta else pl.BlockSpec((tm, tk), lambda i, j, q: (i, q))
    ko = b_koff // tk
    if b_layer is None:
        b_spec = (pl.BlockSpec((tn, tk), lambda i, j, q: (j, q + ko)) if tb
                  else pl.BlockSpec((tk, tn), lambda i, j, q: (q + ko, j)))
    else:
        b_spec = (pl.BlockSpec((None, tn, tk), lambda i, j, q: (b_layer, j, q + ko)) if tb
                  else pl.BlockSpec((None, tk, tn), lambda i, j, q: (b_layer, q + ko, j)))
    o_spec = pl.BlockSpec((tm, tn), lambda i, j, q: (i, j))
    ins, specs = [a, b], [a_spec, b_spec]
    if add is not None:
        ins.append(add)
        specs.append(o_spec)
    aliases = {}
    out_shape = jax.ShapeDtypeStruct((m, n), out_dtype)
    if into is not None:
        buf, layer, col0 = into
        assert buf.shape[1] == m and buf.dtype == out_dtype
        co = col0 // tn
        ins.append(buf)
        specs.append(_ANY)
        aliases = {len(ins) - 1: 0}
        o_spec = pl.BlockSpec((None, tm, tn), lambda i, j, q: (layer, i, j + co))
        out_shape = jax.ShapeDtypeStruct(buf.shape, buf.dtype)
    return pl.pallas_call(
        body, name=f"mm_{m}x{k}x{n}_{int(ta)}{int(tb)}{int(add is not None)}{int(b_layer is not None)}{int(into is not None)}",
        grid=(m // tm, n // tn, nk), in_specs=specs, out_specs=o_spec, out_shape=out_shape,
        input_output_aliases=aliases,
        scratch_shapes=[pltpu.VMEM((tm, tn), F32)] if nk > 1 else [],
        compiler_params=_cparams(("parallel", "parallel", "arbitrary")),
    )(*ins)


def _row_spec(a, block):
    return pl.BlockSpec((block, a.shape[1]), lambda i: (i, 0))


def _whole_spec(a):
    return pl.BlockSpec(a.shape, lambda i: (0,) * a.ndim)


def tmap(name, fn, rows, params, outs, block=ROW_BLOCK):
    s = rows[0].shape[0]
    block = min(block, s)
    nr, npar = len(rows), len(params)

    def body(*refs):
        res = fn(*[_f32(r[...]) for r in refs[:nr]], *[_f32(p[...]) for p in refs[nr:nr + npar]])
        for o_ref, v in zip(refs[nr + npar:], res, strict=True):
            o_ref[...] = v.astype(o_ref.dtype)

    return pl.pallas_call(
        body, name=name, grid=(s // block,),
        in_specs=[_row_spec(a, block) for a in rows] + [_whole_spec(p) for p in params],
        out_specs=[pl.BlockSpec((block, w), lambda i: (i, 0)) for w, _ in outs],
        out_shape=[jax.ShapeDtypeStruct((s, w), dt) for w, dt in outs],
        compiler_params=_cparams(("parallel",)),
    )(*rows, *params)


def tmap_bwd(name, fn, rows, params, douts, row_grad, row_add=None, grad_dtype=None, block=ROW_BLOCK):
    s = rows[0].shape[0]
    block = min(block, s)
    grad_dtype = grad_dtype or MXU_DTYPE
    nr, npar, nd = len(rows), len(params), len(douts)
    gr = [i for i in range(nr) if row_grad[i]]
    row_add = row_add or {}
    adds = [row_add[i] for i in gr if i in row_add]

    def body(*refs):
        rv = [_f32(r[...]) for r in refs[:nr]]
        pv = [_f32(p[...]) for p in refs[nr:nr + npar]]
        dv = tuple(_f32(d[...]) for d in refs[nr + npar:nr + npar + nd])
        add_refs = list(refs[nr + npar + nd:nr + npar + nd + len(adds)])
        out_refs = refs[nr + npar + nd + len(adds):]

        def f(*diff):
            rr = list(rv)
            for n_, i_ in enumerate(gr):
                rr[i_] = diff[n_]
            return tuple(fn(*rr, *diff[len(gr):]))

        _, vjp = jax.vjp(f, *[rv[i_] for i_ in gr], *pv)
        g = vjp(dv)
        for n_, i_ in enumerate(gr):
            v = g[n_]
            if i_ in row_add:
                v = v + add_refs.pop(0)[...].astype(F32)
            out_refs[n_][...] = v.astype(out_refs[n_].dtype)
        first = pl.program_id(0) == 0
        for n_ in range(npar):
            ref = out_refs[len(gr) + n_]

            @pl.when(first)
            def _(ref=ref):
                ref[...] = jnp.zeros_like(ref)

            ref[...] += g[len(gr) + n_]

    res = pl.pallas_call(
        body, name=name, grid=(s // block,),
        in_specs=[_row_spec(a, block) for a in rows] + [_whole_spec(p) for p in params]
        + [_row_spec(d, block) for d in douts] + [_row_spec(a, block) for a in adds],
        out_specs=[_row_spec(rows[i], block) for i in gr] + [_whole_spec(p) for p in params],
        out_shape=[jax.ShapeDtypeStruct(rows[i].shape, F32 if i in row_add else grad_dtype) for i in gr]
        + [jax.ShapeDtypeStruct(p.shape, F32) for p in params],
        compiler_params=_cparams(("arbitrary",)),
    )(*rows, *params, *douts, *adds)
    return list(res[:len(gr)]), list(res[len(gr):])


def rscan(name, fn, state_shapes, rows, params, outs, block):
    s = rows[0].shape[0]
    nsteps = s // block
    nr, npar, no, ns = len(rows), len(params), len(outs), len(state_shapes)

    def body(*refs):
        out_refs = refs[nr + npar:nr + npar + no]
        sav_refs = refs[nr + npar + no:nr + npar + no + ns]
        st_refs = refs[nr + npar + no + ns:]

        @pl.when(pl.program_id(0) == 0)
        def _():
            for st in st_refs:
                st[...] = jnp.zeros_like(st)

        sts = tuple(st[...] for st in st_refs)
        for sv, v in zip(sav_refs, sts):
            sv[...] = v
        new, res = fn(sts, *[_f32(r[...]) for r in refs[:nr]], *[_f32(p[...]) for p in refs[nr:nr + npar]])
        for st, v in zip(st_refs, new, strict=True):
            st[...] = v
        for o_ref, v in zip(out_refs, res, strict=True):
            o_ref[...] = v.astype(o_ref.dtype)

    res = pl.pallas_call(
        body, name=name, grid=(nsteps,),
        in_specs=[_row_spec(a, block) for a in rows] + [_whole_spec(p) for p in params],
        out_specs=[pl.BlockSpec((block, w), lambda i: (i, 0)) for w, _ in outs]
        + [pl.BlockSpec(sh, lambda i: (i, 0)) for sh in state_shapes],
        out_shape=[jax.ShapeDtypeStruct((s, w), dt) for w, dt in outs]
        + [jax.ShapeDtypeStruct((nsteps * sh[0], sh[1]), F32) for sh in state_shapes],
        scratch_shapes=[pltpu.VMEM(sh, F32) for sh in state_shapes],
        compiler_params=_cparams(("arbitrary",)),
    )(*rows, *params)
    return list(res[:no]), list(res[no:])


def rscan_bwd(name, fn, saved, rows, params, douts, block, grad_dtype=None):
    s = rows[0].shape[0]
    nsteps = s // block
    grad_dtype = grad_dtype or MXU_DTYPE
    nr, npar, nd, ns = len(rows), len(params), len(douts), len(saved)
    state_shapes = [(sv.shape[0] // nsteps, sv.shape[1]) for sv in saved]

    def body(*refs):
        rv = [_f32(r[...]) for r in refs[:nr]]
        pv = [_f32(p[...]) for p in refs[nr:nr + npar]]
        dv = tuple(_f32(d[...]) for d in refs[nr + npar:nr + npar + nd])
        sv = tuple(x[...] for x in refs[nr + npar + nd:nr + npar + nd + ns])
        out_refs = refs[nr + npar + nd + ns:nr + npar + nd + ns + nr + npar]
        dst_refs = refs[nr + npar + nd + ns + nr + npar:]
        first = pl.program_id(0) == 0

        @pl.when(first)
        def _():
            for d in dst_refs:
                d[...] = jnp.zeros_like(d)

        def f(sts, *args):
            return fn(sts, *args)

        _, vjp = jax.vjp(f, sv, *rv, *pv)
        g = vjp((tuple(d[...] for d in dst_refs), dv))
        for d, v in zip(dst_refs, g[0], strict=True):
            d[...] = v
        for n_ in range(nr):
            out_refs[n_][...] = g[1 + n_].astype(out_refs[n_].dtype)
        for n_ in range(npar):
            ref = out_refs[nr + n_]

            @pl.when(first)
            def _(ref=ref):
                ref[...] = jnp.zeros_like(ref)

            ref[...] += g[1 + nr + n_]

    rev = lambda i: (nsteps - 1 - i, 0)
    res = pl.pallas_call(
        body, name=name, grid=(nsteps,),
        in_specs=[pl.BlockSpec((block, a.shape[1]), rev) for a in rows] + [_whole_spec(p) for p in params]
        + [pl.BlockSpec((block, d.shape[1]), rev) for d in douts] + [pl.BlockSpec(sh, rev) for sh in state_shapes],
        out_specs=[pl.BlockSpec((block, a.shape[1]), rev) for a in rows] + [_whole_spec(p) for p in params],
        out_shape=[jax.ShapeDtypeStruct(a.shape, grad_dtype) for a in rows]
        + [jax.ShapeDtypeStruct(p.shape, F32) for p in params],
        scratch_shapes=[pltpu.VMEM(sh, F32) for sh in state_shapes],
        compiler_params=_cparams(("arbitrary",)),
    )(*rows, *params, *douts, *saved)
    return list(res[:nr]), list(res[nr:])


def _norm_stage(x, g):
    return (_rms(x, g),)


def _tril():
    r = lax.broadcasted_iota(jnp.int32, (CHUNK, CHUNK), 0)
    c = lax.broadcasted_iota(jnp.int32, (CHUNK, CHUNK), 1)
    return r >= c


def _gla_chunk(st, q, k, v, la, b):
    tril = _tril()
    rowi = lax.broadcasted_iota(jnp.int32, (CHUNK, 1), 0)
    b_last = jnp.sum(la, axis=0, keepdims=True)
    b_ref = jnp.sum(jnp.where(rowi < CHUNK // 2, la, 0.0), axis=0, keepdims=True)
    att = mm.nt(q * jnp.exp(b - b_ref), k * jnp.exp(b_ref - b))
    att = jnp.where(tril, att, 0.0)
    o = mm.nn(att, v) + mm.nn(q * jnp.exp(b), st)
    decay = jnp.exp(jnp.broadcast_to(b_last, (LANE, LANE)).T)
    decay = jnp.concatenate([decay] * (v.shape[1] // LANE), axis=1)
    st2 = decay * st + mm.tn(k * jnp.exp(b_last - b), v)
    return st2, o


def _gla_step(heads, vp, scale):
    kp = LANE

    def fn(states, q, k, v, la):
        sts = list(states)
        trif = _tril().astype(F32)
        rows = []
        for c in range(q.shape[0] // CHUNK):
            r = slice(c * CHUNK, (c + 1) * CHUNK)
            b_all = hi.nn(trif, la[r])
            oh = []
            for h in range(heads):
                ks, vs = slice(h * kp, (h + 1) * kp), slice(h * vp, (h + 1) * vp)
                qh = q[r, ks] * scale if scale != 1.0 else q[r, ks]
                sts[h], o = _gla_chunk(sts[h], qh, k[r, ks], v[r, vs], la[r, ks], b_all[:, ks])
                oh.append(o)
            rows.append(jnp.concatenate(oh, axis=1))
        return tuple(sts), (jnp.concatenate(rows, axis=0),)

    return fn


def _ssd_step(states, xa, dtr, dtb, alog, dsk):
    sts = list(states)
    trif = _tril().astype(F32)
    wide = lax.broadcasted_iota(jnp.int32, (CHUNK, LANE), 0) >= lax.broadcasted_iota(jnp.int32, (CHUNK, LANE), 1)
    hg = SSM_HEADS // SSM_GROUPS
    xw = SSM_HEADS * LANE
    lane, head = lax.broadcasted_iota(jnp.int32, (LANE, xw), 1), lax.broadcasted_iota(jnp.int32, (LANE, xw), 0)
    spread = ((lane >= head * LANE) & (lane < (head + 1) * LANE)).astype(F32)
    neg_a = -jnp.exp(alog)
    pad = jnp.zeros((CHUNK, LANE), F32)
    rows = []
    for c in range(xa.shape[0] // CHUNK):
        r = slice(c * CHUNK, (c + 1) * CHUNK)
        dt_all = _softplus(hi.nn(dtr[r], spread) + dtb)
        a_all = dt_all * neg_a
        acs_all = hi.nn(trif, a_all)
        last_all = jnp.sum(a_all, axis=0, keepdims=True)
        yh = []
        for g in range(SSM_GROUPS):
            bm = xa[r, xw + g * LANE:xw + (g + 1) * LANE]
            cm = xa[r, xw + (SSM_GROUPS + g) * LANE:xw + (SSM_GROUPS + g + 1) * LANE]
            cb = mm.nt(cm, jnp.concatenate([bm, pad], axis=0))
            for hh in range(hg):
                h = g * hg + hh
                ls = slice(h * LANE, (h + 1) * LANE)
                xs, acs, acs_last = xa[r, ls], acs_all[:, ls], last_all[:, ls]
                xdt = xs * dt_all[:, ls]
                seg = acs - jnp.concatenate([acs, pad], axis=0).T[:CHUNK]
                lmat = jnp.exp(jnp.where(wide, seg, -1e30))
                y = (mm.nn(cb * lmat, jnp.concatenate([xdt, pad], axis=0)) + mm.nn(cm, sts[h]) * jnp.exp(acs)
                     + dsk[:, ls] * xs)
                sts[h] = jnp.exp(acs_last) * sts[h] + mm.tn(bm, xdt * jnp.exp(acs_last - acs))
                yh.append(y)
        rows.append(jnp.concatenate(yh, axis=1))
    return tuple(sts), (jnp.concatenate(rows, axis=0),)


def _gla_pre(glr, w2, bg):
    z = mm.nn(glr, w2) + bg
    return (-_softplus(-z) * (1.0 / GLA_GATE_NORM),)


def _gla_post(o, og, g):
    w = 2 * LANE
    return (jnp.concatenate([_rms(o[:, h * w:(h + 1) * w], g, GLA_DV) * _silu(og[:, h * w:(h + 1) * w])
                             for h in range(GLA_HEADS)], axis=1),)


def _hgrn_pre(q, f, lbnd):
    e = jnp.exp(lbnd - jnp.max(lbnd, axis=0, keepdims=True))
    rowi = lax.broadcasted_iota(jnp.int32, e.shape, 0)
    lb = jnp.sum(jnp.where(rowi >= 1, e, 0.0), axis=0, keepdims=True) / jnp.sum(e, axis=0, keepdims=True)
    fg = lb + (1.0 - lb) * _sigmoid(f)
    return _silu(q), 1.0 - fg, jnp.log(fg)


def _hgrn_post(o, og, g):
    return (jnp.concatenate([_rms(o[:, h * LANE:(h + 1) * LANE], g) for h in range(HGRN_HEADS)], axis=1)
            * _sigmoid(og),)


def _mamba_post(y, z, g):
    v = y * _silu(z)
    w = (SSM_HEADS // SSM_GROUPS) * LANE
    n_real = (SSM_HEADS // SSM_GROUPS) * SSM_HD
    return (jnp.concatenate([_rms(v[:, i * w:(i + 1) * w], g[:, i * w:(i + 1) * w], n_real)
                             for i in range(SSM_GROUPS)], axis=1),)


def _dil_pre(q, k, cosf, sinf, qg, kg):
    def groups(x, g):
        out = []
        for grp in range(len(DIL_GROUPS)):
            hs = []
            for h in range(grp * DIL_HEADS, (grp + 1) * DIL_HEADS):
                n = _rms(x[:, h * LANE:(h + 1) * LANE], g)
                hs.append(n * cosf + _swap_halves(n) * sinf)
            out.append(jnp.concatenate(hs, axis=1))
        return out

    return (*groups(q, qg), *groups(k, kg))


def _dil_merge(o0, o1, o2, l0, l1, l2):
    m = jnp.maximum(jnp.maximum(l0, l1), l2)
    e0, e1, e2 = jnp.exp(l0 - m), jnp.exp(l1 - m), jnp.exp(l2 - m)
    return ((e0 * o0 + e1 * o1 + e2 * o2) / (e0 + e1 + e2),)


def _dil_block(q, kp, kc, vp, vc, lim):
    kk = jnp.concatenate([kp, kc], axis=0)
    vv = jnp.concatenate([vp, vc], axis=0)
    s = mm.nt(q, kk) * (DIL_HD ** -0.5)
    i = lax.broadcasted_iota(jnp.int32, s.shape, 0)
    j = lax.broadcasted_iota(jnp.int32, s.shape, 1)
    dist = DIL_BLOCK + i - j
    s = jnp.where((dist >= 0) & (dist <= DIL_BLOCK) & (j >= lim), s, -1e30)
    m = jnp.max(s, axis=-1, keepdims=True)
    p = jnp.exp(s - m)
    l = jnp.sum(p, axis=-1, keepdims=True)
    return mm.nn(p / l, vv), jnp.broadcast_to(m + jnp.log(l), (q.shape[0], LANE))


def _xattn(xq, kv, qg, kg):
    w = XA_HEADS * LANE
    os_ = []
    for h in range(XA_HEADS):
        ls = slice(h * LANE, (h + 1) * LANE)
        q = _rms(xq[:, ls], qg, XA_HD)
        k = _rms(kv[:, ls], kg, XA_HD)
        s = mm.nt(q, k) * (XA_HD ** -0.5)
        p = jnp.exp(s - jnp.max(s, axis=-1, keepdims=True))
        p = p / jnp.sum(p, axis=-1, keepdims=True)
        os_.append(mm.nn(p, kv[:, w + h * LANE:w + (h + 1) * LANE]))
    return (jnp.concatenate(os_, axis=1),)


def _dil_geometry(s, w, r, g, v_cols):
    hb = DIL_HEADS if r == 1 else 1
    rb = DIL_BLOCK * r
    nb = s // rb
    bw = hb * LANE
    v_col0 = g * (w // bw)
    assert v_cols % bw == 0 and s % rb == 0
    return hb, rb, nb, bw, v_col0


def _sub(r, res):
    return pl.ds(res, DIL_BLOCK, stride=r) if r > 1 else slice(None)


def dil_attn(name, q, k, v, r, g):
    s, w = q.shape
    hb, rb, nb, bw, v_col0 = _dil_geometry(s, w, r, g, v.shape[1])

    def body(q_r, kp_r, kc_r, vp_r, vc_r, o_r, l_r):
        lim = jnp.where(pl.program_id(1) == 0, DIL_BLOCK, 0)
        for res in range(r):
            rows = _sub(r, res)
            for h in range(hb):
                ls = slice(h * LANE, (h + 1) * LANE)
                o, lse = _dil_block(q_r[rows, ls], kp_r[rows, ls], kc_r[rows, ls], vp_r[rows, ls], vc_r[rows, ls], lim)
                o_r[rows, ls] = o
                l_r[rows, ls] = lse

    cur = pl.BlockSpec((rb, bw), lambda hblk, n: (n, hblk))
    prev = pl.BlockSpec((rb, bw), lambda hblk, n: (jnp.maximum(n - 1, 0), hblk))
    vcur = pl.BlockSpec((rb, bw), lambda hblk, n: (n, v_col0 + hblk))
    vprev = pl.BlockSpec((rb, bw), lambda hblk, n: (jnp.maximum(n - 1, 0), v_col0 + hblk))
    return pl.pallas_call(
        body, name=name, grid=(w // bw, nb), in_specs=[cur, prev, cur, vprev, vcur], out_specs=[cur, cur],
        out_shape=[jax.ShapeDtypeStruct((s, w), F32)] * 2,
        compiler_params=_cparams(("parallel", "parallel")),
    )(q, k, k, v, v)


def dil_attn_bwd(name, q, k, v, do, dlse, r, g):
    s, w = q.shape
    hb, rb, nb, bw, v_col0 = _dil_geometry(s, w, r, g, v.shape[1])

    def body(q_r, kp_r, kc_r, vp_r, vc_r, do_r, dl_r, dq_r, dk_r, dv_r, ck, cv):
        i = pl.program_id(1)
        lim = jnp.where(i == nb - 1, DIL_BLOCK, 0)

        @pl.when(i == 0)
        def _():
            ck[...] = jnp.zeros_like(ck)
            cv[...] = jnp.zeros_like(cv)

        for res in range(r):
            rows = _sub(r, res)
            for h in range(hb):
                ls = slice(h * LANE, (h + 1) * LANE)
                _, vjp = jax.vjp(functools.partial(_dil_block, lim=lim),
                                 q_r[rows, ls], kp_r[rows, ls], kc_r[rows, ls], vp_r[rows, ls], vc_r[rows, ls])
                gq, gkp, gkc, gvp, gvc = vjp((do_r[rows, ls], dl_r[rows, ls]))
                dq_r[rows, ls] = gq
                dk_r[rows, ls] = gkc + ck[rows, ls]
                dv_r[rows, ls] = gvc + cv[rows, ls]
                ck[rows, ls] = gkp
                cv[rows, ls] = gvp

    cur = pl.BlockSpec((rb, bw), lambda hblk, i: (nb - 1 - i, hblk))
    prev = pl.BlockSpec((rb, bw), lambda hblk, i: (jnp.maximum(nb - 2 - i, 0), hblk))
    vcur = pl.BlockSpec((rb, bw), lambda hblk, i: (nb - 1 - i, v_col0 + hblk))
    vprev = pl.BlockSpec((rb, bw), lambda hblk, i: (jnp.maximum(nb - 2 - i, 0), v_col0 + hblk))
    return pl.pallas_call(
        body, name=name, grid=(w // bw, nb), in_specs=[cur, prev, cur, vprev, vcur, cur, cur],
        out_specs=[cur, cur, cur], out_shape=[jax.ShapeDtypeStruct((s, w), F32)] * 3,
        scratch_shapes=[pltpu.VMEM((rb, bw), F32)] * 2,
        compiler_params=_cparams(("parallel", "arbitrary")),
    )(q, k, k, v, v, do, dlse)


def _dsilu(u):
    sg = _sigmoid(u)
    return sg * (1.0 + u * (1.0 - sg))


def _ride(body, rider, n_in, n_out, n_scratch, grid):
    if rider is None:
        return body, [], [], [], [], [], None
    ni, no = len(rider.ins), len(rider.out_shapes)

    def wrapped(*refs):
        k_in, r_in = refs[:n_in], refs[n_in:n_in + ni]
        k_out, r_out = refs[n_in + ni:n_in + ni + n_out], refs[n_in + ni + n_out:n_in + ni + n_out + no]
        k_scr = refs[n_in + ni + n_out + no:n_in + ni + n_out + no + n_scratch]
        send_sems, recv_sems = refs[-2], refs[-1]
        first = functools.reduce(jnp.logical_and, [pl.program_id(a) == 0 for a in range(len(grid))])
        last = functools.reduce(jnp.logical_and, [pl.program_id(a) == g - 1 for a, g in enumerate(grid)])

        @pl.when(first)
        def _():
            rider.start(r_in, r_out, send_sems, recv_sems)

        body(*k_in, *k_out, *k_scr)

        @pl.when(last)
        def _():
            rider.finish(r_in, r_out, send_sems, recv_sems)

    sems = [pltpu.SemaphoreType.DMA((rider.n_sems,)), pltpu.SemaphoreType.DMA((rider.n_sems,))]
    return wrapped, rider.ins, [_ANY] * ni, [_ANY] * no, rider.out_shapes, sems, ("arbitrary",) * len(grid)


CONV_STRIP = 16


def _shifted_rows(prev8, cur_r, next8, lanes, s0, n, sh, block):
    if s0 - sh < 0:
        assert s0 == 0
        xp = jnp.concatenate([prev8, cur_r[0:n, lanes]], axis=0)
        return pltpu.roll(xp, sh, 0)[SUBLANE:SUBLANE + n]
    if s0 - sh + n > block:
        assert s0 == block and n == SUBLANE
        xp = jnp.concatenate([cur_r[block - SUBLANE:block, lanes], next8], axis=0)
        return (pltpu.roll(xp, sh, 0) if sh else xp)[SUBLANE:]
    return cur_r[pl.ds(s0 - sh, n), lanes]


def conv_fwd(name, x, w, b, mode, out_dtype, tc, block=ROW_BLOCK, rider=None):
    s, c = x.shape
    ntap = w.shape[0]
    block = min(block, s)
    f = c // 2 if mode == 'glu' else c
    nh = 2 if mode == 'glu' else 1
    off = f // tc

    def body(*refs):
        first = pl.program_id(1) == 0
        o_ref = refs[-1]

        def column(cidx, carry):
            lanes = pl.ds(pl.multiple_of(cidx * LANE, LANE), LANE)
            prevs = [jnp.where(first, 0.0, refs[4 * hlf][:, lanes]) for hlf in range(nh)]
            for s0 in range(0, block, CONV_STRIP):
                us = []
                for hlf in range(nh):
                    _, cur_r, w_r, b_r = refs[4 * hlf:4 * hlf + 4]
                    acc = b_r[:, lanes]
                    for j in range(ntap):
                        xs = _shifted_rows(prevs[hlf], cur_r, None, lanes, s0, CONV_STRIP, ntap - 1 - j, block)
                        acc = acc + w_r[j:j + 1, lanes] * xs
                    us.append(acc)
                res = _silu(us[0]) * us[1] if mode == 'glu' else _silu(us[0])
                o_ref[pl.ds(s0, CONV_STRIP), lanes] = res.astype(o_ref.dtype)
            return carry

        lax.fori_loop(0, tc // LANE, column, 0)

    rb = block // SUBLANE
    ins, specs = [], []
    for hlf in range(nh):
        o = hlf * off
        ins += [x, x, w, b]
        specs += [pl.BlockSpec((SUBLANE, tc), lambda j, i, o=o: (jnp.maximum(i * rb - 1, 0), j + o)),
                  pl.BlockSpec((block, tc), lambda j, i, o=o: (i, j + o)),
                  pl.BlockSpec((ntap, tc), lambda j, i, o=o: (0, j + o)),
                  pl.BlockSpec((1, tc), lambda j, i, o=o: (0, j + o))]
    grid = (f // tc, s // block)
    body, r_ins, r_in_specs, r_out_specs, r_out_shapes, r_scratch, sem = _ride(body, rider, len(ins), 1, 0, grid)
    res = pl.pallas_call(
        body, name=name, grid=grid, in_specs=specs + r_in_specs,
        out_specs=[pl.BlockSpec((block, tc), lambda j, i: (i, j))] + r_out_specs,
        out_shape=[jax.ShapeDtypeStruct((s, f), out_dtype)] + r_out_shapes, scratch_shapes=r_scratch,
        compiler_params=_cparams(sem or ("parallel", "parallel")),
    )(*ins, *r_ins)
    return res[0] if rider is None else (res[0], rider.results(res[1:]))


def conv_bwd(name, x, w, b, dout, mode, tc, block=ROW_BLOCK, rider=None):
    s, c = x.shape
    ntap = w.shape[0]
    block = min(block, s)
    nblk = s // block
    f = c // 2 if mode == 'glu' else c
    nh = 2 if mode == 'glu' else 1
    off = f // tc
    ext = block + SUBLANE

    def body(*refs):
        i = pl.program_id(1)
        first, last = i == 0, i == nblk - 1
        dcur_r, dnext_r = refs[5 * nh], refs[5 * nh + 1]
        outs = refs[5 * nh + 2:5 * nh + 2 + 3 * nh]
        du_scr = refs[5 * nh + 2 + 3 * nh:]

        @pl.when(first)
        def _():
            for hlf in range(nh):
                outs[3 * hlf + 1][...] = jnp.zeros_like(outs[3 * hlf + 1])
                outs[3 * hlf + 2][...] = jnp.zeros_like(outs[3 * hlf + 2])

        def column(cidx, carry):
            lanes = pl.ds(pl.multiple_of(cidx * LANE, LANE), LANE)
            prevs = [jnp.where(first, 0.0, refs[5 * hlf][:, lanes]) for hlf in range(nh)]
            nexts = [jnp.where(last, 0.0, refs[5 * hlf + 2][:, lanes]) for hlf in range(nh)]
            db_acc = [jnp.zeros((CONV_STRIP, LANE), F32) for _ in range(nh)]
            dw_acc = [[jnp.zeros((CONV_STRIP, LANE), F32) for _ in range(ntap)] for _ in range(nh)]
            for s0 in range(0, ext, CONV_STRIP):
                n = min(CONV_STRIP, ext - s0)
                d_e = dcur_r[pl.ds(s0, n), lanes] if s0 < block else jnp.where(last, 0.0, dnext_r[:, lanes])
                xs, us = [], []
                for hlf in range(nh):
                    cur_r, w_r, b_r = refs[5 * hlf + 1], refs[5 * hlf + 3], refs[5 * hlf + 4]
                    sh_rows = [_shifted_rows(prevs[hlf], cur_r, nexts[hlf], lanes, s0, n, ntap - 1 - j, block)
                               for j in range(ntap)]
                    acc = b_r[:, lanes]
                    for j in range(ntap):
                        acc = acc + w_r[j:j + 1, lanes] * sh_rows[j]
                    xs.append(sh_rows)
                    us.append(acc)
                dus = [d_e * us[1] * _dsilu(us[0]), d_e * _silu(us[0])] if mode == 'glu' else [d_e * _dsilu(us[0])]
                for hlf in range(nh):
                    du_scr[hlf][pl.ds(s0, n), lanes] = dus[hlf]
                    if s0 < block:
                        db_acc[hlf] = db_acc[hlf] + dus[hlf]
                        for j in range(ntap):
                            dw_acc[hlf][j] = dw_acc[hlf][j] + dus[hlf] * xs[hlf][j]
            for hlf in range(nh):
                w_r = refs[5 * hlf + 3]
                dx_r, dw_r, db_r = outs[3 * hlf:3 * hlf + 3]
                db_r[:, lanes] += jnp.sum(db_acc[hlf], axis=0, keepdims=True)
                for j in range(ntap):
                    dw_r[j:j + 1, lanes] += jnp.sum(dw_acc[hlf][j], axis=0, keepdims=True)
                for s0 in range(0, block, CONV_STRIP):
                    dx = None
                    for j in range(ntap):
                        term = w_r[j:j + 1, lanes] * du_scr[hlf][pl.ds(s0 + ntap - 1 - j, CONV_STRIP), lanes]
                        dx = term if dx is None else dx + term
                    dx_r[pl.ds(s0, CONV_STRIP), lanes] = dx.astype(dx_r.dtype)
            return carry

        lax.fori_loop(0, tc // LANE, column, 0)

    rb = block // SUBLANE
    nrow8 = s // SUBLANE
    ins, specs = [], []
    for hlf in range(nh):
        o = hlf * off
        ins += [x, x, x, w, b]
        specs += [pl.BlockSpec((SUBLANE, tc), lambda j, i, o=o: (jnp.maximum(i * rb - 1, 0), j + o)),
                  pl.BlockSpec((block, tc), lambda j, i, o=o: (i, j + o)),
                  pl.BlockSpec((SUBLANE, tc), lambda j, i, o=o: (jnp.minimum((i + 1) * rb, nrow8 - 1), j + o)),
                  pl.BlockSpec((ntap, tc), lambda j, i, o=o: (0, j + o)),
                  pl.BlockSpec((1, tc), lambda j, i, o=o: (0, j + o))]
    ins += [dout, dout]
    specs += [pl.BlockSpec((block, tc), lambda j, i: (i, j)),
              pl.BlockSpec((SUBLANE, tc), lambda j, i: (jnp.minimum((i + 1) * rb, nrow8 - 1), j))]
    out_specs, out_shape = [], []
    for hlf in range(nh):
        out_specs += [pl.BlockSpec((block, tc), lambda j, i: (i, j)), pl.BlockSpec((ntap, tc), lambda j, i: (0, j)),
                      pl.BlockSpec((1, tc), lambda j, i: (0, j))]
        out_shape += [jax.ShapeDtypeStruct((s, f), MXU_DTYPE), jax.ShapeDtypeStruct((ntap, f), F32),
                      jax.ShapeDtypeStruct((1, f), F32)]
    grid = (f // tc, nblk)
    body, r_ins, r_in_specs, r_out_specs, r_out_shapes, r_scratch, sem = _ride(body, rider, len(ins), 3 * nh, nh, grid)
    res = pl.pallas_call(
        body, name=name, grid=grid, in_specs=specs + r_in_specs, out_specs=out_specs + r_out_specs,
        out_shape=out_shape + r_out_shapes, scratch_shapes=[pltpu.VMEM((ext, tc), F32)] * nh + r_scratch,
        compiler_params=_cparams(sem or ("parallel", "arbitrary")),
    )(*ins, *r_ins)
    rode = None if rider is None else rider.results(res[3 * nh:])
    if nh == 1:
        return [res[0]], res[1], res[2], rode
    return ([res[0], res[3]], jnp.concatenate([res[1], res[4]], axis=1), jnp.concatenate([res[2], res[5]], axis=1),
            rode)


def loss_head(y, target, block=ROW_BLOCK):
    s, d = y.shape
    block = min(block, s)

    def body(y_r, t_r, acc_r, dy_r):
        e = y_r[...] - t_r[...]
        dy_r[...] = e * (1.0 / d)

        @pl.when(pl.program_id(0) == 0)
        def _():
            acc_r[...] = jnp.zeros_like(acc_r)

        acc_r[...] += jnp.sum((e * e).reshape(block // SUBLANE, SUBLANE, d), axis=0) * (0.5 / d)

    return pl.pallas_call(
        body, name="loss_head", grid=(s // block,),
        in_specs=[pl.BlockSpec((block, d), lambda i: (i, 0))] * 2,
        out_specs=[pl.BlockSpec((SUBLANE, d), lambda i: (0, 0)), pl.BlockSpec((block, d), lambda i: (i, 0))],
        out_shape=[jax.ShapeDtypeStruct((SUBLANE, d), F32), jax.ShapeDtypeStruct((s, d), F32)],
        compiler_params=_cparams(("arbitrary",)),
    )(y, target)


def adamw(name, w, g, m, v):
    r, c = w.shape
    tr = r if r <= 512 else _tile(r, (512, 256, 128, 64, 32, 16, 8))
    if c * tr * 4 > (1 << 21):
        tr = _tile(r, (256, 128, 64, 32, 16, 8))

    def body(w_r, g_r, m_r, v_r, d_r, nm_r, nv_r):
        gg = g_r[...]
        nm = ADAM_B1 * m_r[...] + (1.0 - ADAM_B1) * gg
        nv = ADAM_B2 * v_r[...] + (1.0 - ADAM_B2) * (gg * gg)
        m_hat = nm / (1.0 - ADAM_B1 ** ADAM_STEP)
        v_hat = nv / (1.0 - ADAM_B2 ** ADAM_STEP)
        d_r[...] = -ADAM_LR * (m_hat / (jnp.sqrt(v_hat) + ADAM_EPS) + ADAM_WD * w_r[...])
        nm_r[...] = nm
        nv_r[...] = nv

    spec = pl.BlockSpec((tr, c), lambda i: (i, 0))
    return pl.pallas_call(
        body, name=name, grid=(r // tr,), in_specs=[spec] * 4, out_specs=[spec] * 3,
        out_shape=[jax.ShapeDtypeStruct((r, c), F32)] * 3, compiler_params=_cparams(("parallel",)),
    )(w, g, m, v)


MESH = pl.DeviceIdType.MESH
_ANY = pl.BlockSpec(memory_space=pl.ANY)


def _place():
    return lax.axis_index("x"), lax.axis_index("y"), lax.axis_index("c")


class Packed:
    def __init__(self, shard_shape):
        self.r, self.c = shard_shape
        self.h = self.r // 2
        self.whole = (N_CHIPS, self.r, self.c)
        self.got = (N_CHIPS, self.h, self.c)
        self.slab_half = (self.h, self.c)

    def shard_half(self, ref, core):
        return ref.at[pl.ds(core * self.h, self.h)]

    def whole_half(self, ref, chip, core):
        return ref.at[chip, pl.ds(core * self.h, self.h)]

    def place(self, whole, shard, chip):
        return lax.dynamic_update_slice(whole, shard[None], (chip, 0, 0))

    def grad_half(self, ref, core):
        return ref.at[:, core]

    def pair_slab(self, ref, chip):
        return ref.at[chip]


class SlabCols:
    def __init__(self, shard_shape):
        self.r, self.c = shard_shape
        self.h = self.r // 2
        self.whole = (self.r, N_CHIPS * self.c)
        self.got = (self.h, N_CHIPS * self.c)
        self.slab_half = (self.h, self.c)

    def _cols(self, chip):
        return pl.ds(pl.multiple_of(chip * self.c, LANE), self.c)

    def shard_half(self, ref, core):
        return ref.at[pl.ds(core * self.h, self.h)]

    def whole_half(self, ref, chip, core):
        return ref.at[pl.ds(core * self.h, self.h), self._cols(chip)]

    def place(self, whole, shard, chip):
        return lax.dynamic_update_slice_in_dim(whole, shard, chip * self.c, 1)

    def grad_half(self, ref, core):
        return ref.at[pl.ds(core * self.h, self.h)]

    def pair_slab(self, ref, chip):
        return ref.at[:, self._cols(chip)]


class GatherRider:
    def __init__(self, shards, kinds):
        self.ins, self.kinds, n = list(shards), kinds, len(shards)
        self.out_shapes = [jax.ShapeDtypeStruct(k.whole, s.dtype) for k, s in zip(kinds, shards)]
        self.n_sems = 6 * n

    def _copies(self, w_refs, out_refs, send_sems, recv_sems):
        x, y, cc = _place()
        chips = [(1 - x, y), (x, 1 - y), (1 - x, 1 - y)]

        def copy(t, k, chip, core, to, src=None):
            dst = self.kinds[t].whole_half(out_refs[t], 2 * chip[0] + chip[1], core)
            return pltpu.make_async_remote_copy(
                src_ref=dst if src is None else src, dst_ref=dst, send_sem=send_sems.at[6 * t + k],
                recv_sem=recv_sems.at[6 * t + k], device_id=to, device_id_type=MESH)

        first = [copy(t, j, (x, y), cc, (*chip, cc), src=self.kinds[t].shard_half(w_refs[t], cc))
                 for t in range(len(self.ins)) for j, chip in enumerate(chips)]
        return copy, first, chips, (x, y, cc)

    def start(self, w_refs, out_refs, send_sems, recv_sems):
        for cp in self._copies(w_refs, out_refs, send_sems, recv_sems)[1]:
            cp.start()

    def finish(self, w_refs, out_refs, send_sems, recv_sems):
        copy, first, chips, (x, y, cc) = self._copies(w_refs, out_refs, send_sems, recv_sems)
        passed = []
        for t in range(len(self.ins)):
            for j, chip in enumerate(chips):
                copy(t, j, chip, cc, (x, y, cc)).wait_recv()
                passed.append(copy(t, 3 + j, chip, cc, (x, y, 1 - cc)))
                passed[-1].start()
        for t in range(len(self.ins)):
            for j, chip in enumerate(chips):
                copy(t, 3 + j, chip, 1 - cc, (x, y, cc)).wait_recv()
        for cp in first + passed:
            cp.wait_send()

    def results(self, outs):
        chip = 2 * lax.axis_index("x") + lax.axis_index("y")
        return [k.place(o, s, chip) for k, o, s in zip(self.kinds, outs, self.ins)]


class ExchangeRider:
    def __init__(self, pairs, kinds):
        self.ins, self.kinds = list(pairs), kinds
        self.out_shapes = [jax.ShapeDtypeStruct((N_CHIPS,) + k.slab_half, p.dtype) for k, p in zip(kinds, pairs)]
        self.n_sems = 3 * len(pairs)

    def start(self, p_refs, out_refs, send_sems, recv_sems):
        x, y, cc = _place()
        for t in range(len(self.ins)):
            for j, chip in enumerate([(1 - x, y), (x, 1 - y), (1 - x, 1 - y)]):
                pltpu.make_async_remote_copy(
                    src_ref=self.kinds[t].pair_slab(p_refs[t], 2 * chip[0] + chip[1]), dst_ref=out_refs[t].at[2 * x + y],
                    send_sem=send_sems.at[3 * t + j], recv_sem=recv_sems.at[3 * t + j], device_id=(*chip, cc),
                    device_id_type=MESH).start()

    def finish(self, p_refs, out_refs, send_sems, recv_sems):
        x, y, cc = _place()
        me = 2 * x + y
        for t in range(len(self.ins)):
            for j, chip in enumerate([(1 - x, y), (x, 1 - y), (1 - x, 1 - y)]):
                them = 2 * chip[0] + chip[1]
                pltpu.make_async_remote_copy(
                    src_ref=self.kinds[t].pair_slab(p_refs[t], them), dst_ref=out_refs[t].at[them],
                    send_sem=send_sems.at[3 * t + j], recv_sem=recv_sems.at[3 * t + j], device_id=(x, y, cc),
                    device_id_type=MESH).wait()

    def results(self, outs):
        return list(outs)


def run_rider(rider, name):
    n, no = len(rider.ins), len(rider.out_shapes)

    def body(*refs):
        parts = (refs[:n], refs[n:n + no], refs[n + no], refs[n + no + 1])
        rider.start(*parts)
        rider.finish(*parts)

    outs = pl.pallas_call(
        body, name=name, in_specs=[_ANY] * n, out_specs=[_ANY] * no, out_shape=rider.out_shapes,
        scratch_shapes=[pltpu.SemaphoreType.DMA((rider.n_sems,)), pltpu.SemaphoreType.DMA((rider.n_sems,))],
    )(*rider.ins)
    return rider.results(outs)


def allgather_devices(buf):
    r, c = buf.shape

    def body(b_ref, out_ref, send_sems, recv_sems, local_sem):
        x, y, cc = _place()
        me = 4 * x + 2 * y + cc
        mine = pltpu.make_async_copy(b_ref, out_ref.at[me], local_sem)
        mine.start()
        copies = []
        for k in range(1, N_DEV):
            px, py, pc = x ^ (k >> 2), y ^ ((k >> 1) & 1), cc ^ (k & 1)
            cp = pltpu.make_async_remote_copy(src_ref=b_ref, dst_ref=out_ref.at[me], send_sem=send_sems.at[k - 1],
                                              recv_sem=recv_sems.at[k - 1], device_id=(px, py, pc), device_id_type=MESH)
            cp.start()
            copies.append((cp, 4 * px + 2 * py + pc))
        for k, (cp, peer) in enumerate(copies):
            pltpu.make_async_remote_copy(src_ref=b_ref, dst_ref=out_ref.at[peer], send_sem=send_sems.at[k],
                                         recv_sem=recv_sems.at[k], device_id=(x, y, cc), device_id_type=MESH).wait_recv()
        for cp, _ in copies:
            cp.wait_send()
        mine.wait()

    return pl.pallas_call(
        body, name="allgather_devices", in_specs=[_ANY], out_specs=_ANY,
        out_shape=jax.ShapeDtypeStruct((N_DEV, r, c), buf.dtype),
        scratch_shapes=[pltpu.SemaphoreType.DMA((N_DEV - 1,)), pltpu.SemaphoreType.DMA((N_DEV - 1,)),
                        pltpu.SemaphoreType.DMA],
    )(buf)


def swap_halves_sibling(gs, kinds, name):
    n = len(gs)

    def body(*refs):
        g_refs, out_refs, send_sems, recv_sems = refs[:n], refs[n:2 * n], refs[2 * n], refs[2 * n + 1]
        x, y, cc = _place()
        cps = []
        for t in range(n):
            cps.append(pltpu.make_async_remote_copy(
                src_ref=kinds[t].grad_half(g_refs[t], 1 - cc), dst_ref=out_refs[t], send_sem=send_sems.at[t],
                recv_sem=recv_sems.at[t], device_id=(x, y, 1 - cc), device_id_type=MESH))
            cps[-1].start()
        for cp in cps:
            cp.wait()

    return pl.pallas_call(
        body, name=name, in_specs=[_ANY] * n, out_specs=[_ANY] * n,
        out_shape=[jax.ShapeDtypeStruct(k.got, g.dtype) for k, g in zip(kinds, gs)],
        scratch_shapes=[pltpu.SemaphoreType.DMA((n,)), pltpu.SemaphoreType.DMA((n,))],
    )(*gs)


def _row_tile(n, limit=512):
    return max(t for t in range(16, limit + 1, 16) if n % t == 0)


def sum_chips(got, own, kind, chip, name):
    def body(chip_ref, got_r, own_r, out_r):
        mine = own_r[...].astype(F32)
        acc = None
        for k in range(N_CHIPS):
            term = jnp.where(chip_ref[0] == k, mine, got_r[k].astype(F32))
            acc = term if acc is None else acc + term
        out_r[...] = acc

    if isinstance(kind, Packed):
        r, c = kind.slab_half
        tr = _row_tile(r)
        grid = (r // tr,)
        specs = [pl.BlockSpec((N_CHIPS, tr, c), lambda i, chip_ref: (0, i, 0)),
                 pl.BlockSpec((None, tr, c), lambda i, chip_ref: (chip_ref[0], i, 0))]
        out_spec = pl.BlockSpec((tr, c), lambda i, chip_ref: (i, 0))
    else:
        r, c = kind.slab_half
        tr = _row_tile(r, 256)
        grid = (r // tr,)
        specs = [pl.BlockSpec((N_CHIPS, tr, c), lambda i, chip_ref: (0, i, 0)),
                 pl.BlockSpec((tr, c), lambda i, chip_ref: (i, chip_ref[0]))]
        out_spec = pl.BlockSpec((tr, c), lambda i, chip_ref: (i, 0))
    return pl.pallas_call(
        body, name=name,
        grid_spec=pltpu.PrefetchScalarGridSpec(num_scalar_prefetch=1, grid=grid, in_specs=specs, out_specs=out_spec),
        out_shape=jax.ShapeDtypeStruct(kind.slab_half, F32),
        compiler_params=_cparams(("parallel",) * len(grid)),
    )(chip, got, own)


def join_halves_sibling(halves):
    n = len(halves)

    def body(*refs):
        h_refs, out_refs, send_sems, recv_sems = refs[:n], refs[n:2 * n], refs[2 * n], refs[2 * n + 1]
        x, y, cc = _place()
        cps = []
        for t in range(n):
            cps.append(pltpu.make_async_remote_copy(
                src_ref=h_refs[t], dst_ref=out_refs[t].at[cc], send_sem=send_sems.at[t], recv_sem=recv_sems.at[t],
                device_id=(x, y, 1 - cc), device_id_type=MESH))
            cps[-1].start()
        for t in range(n):
            pltpu.make_async_remote_copy(
                src_ref=h_refs[t], dst_ref=out_refs[t].at[1 - cc], send_sem=send_sems.at[t], recv_sem=recv_sems.at[t],
                device_id=(x, y, cc), device_id_type=MESH).wait_recv()
        for cp in cps:
            cp.wait_send()

    outs = pl.pallas_call(
        body, name="join_halves_sibling", in_specs=[_ANY] * n, out_specs=[_ANY] * n,
        out_shape=[jax.ShapeDtypeStruct((2,) + h.shape, h.dtype) for h in halves],
        scratch_shapes=[pltpu.SemaphoreType.DMA((n,)), pltpu.SemaphoreType.DMA((n,))],
    )(*halves)
    core = lax.axis_index("c")
    return [lax.dynamic_update_slice_in_dim(o, h[None], core, 0) for o, h in zip(outs, halves)]


def add_own_half(g, got, kind, core, out_dtype, name):
    def body(c_ref, g_r, o_r, out_r):
        out_r[...] = (g_r[...] + o_r[...]).astype(out_r.dtype)

    if isinstance(kind, Packed):
        r, c = kind.slab_half
        tr = _row_tile(r)
        grid = (N_CHIPS, r // tr)
        specs = [pl.BlockSpec((None, None, tr, c), lambda i, j, c_ref: (i, c_ref[0], j, 0)),
                 pl.BlockSpec((None, tr, c), lambda i, j, c_ref: (i, j, 0))]
        out_spec = pl.BlockSpec((None, tr, c), lambda i, j, c_ref: (i, j, 0))
    else:
        h, c4 = kind.got
        tr = _row_tile(h, 128)
        grid = (1, h // tr)
        specs = [pl.BlockSpec((tr, c4), lambda i, j, c_ref: (c_ref[0] * (h // tr) + j, 0)),
                 pl.BlockSpec((tr, c4), lambda i, j, c_ref: (j, 0))]
        out_spec = pl.BlockSpec((tr, c4), lambda i, j, c_ref: (j, 0))
    return pl.pallas_call(
        body, name=name,
        grid_spec=pltpu.PrefetchScalarGridSpec(num_scalar_prefetch=1, grid=grid, in_specs=specs, out_specs=out_spec),
        out_shape=jax.ShapeDtypeStruct(kind.got, out_dtype),
        compiler_params=_cparams(("parallel", "parallel")),
    )(core, g, got)


def sum_slabs(p, name):
    n, r, c = p.shape
    tr = _tile(r, [t for t in (512, 256, 128, 64, 32, 16) if n * t * c * p.dtype.itemsize <= (1 << 23)])

    def body(p_r, out_r):
        acc = p_r[0].astype(F32)
        for k in range(1, n):
            acc = acc + p_r[k].astype(F32)
        out_r[...] = acc

    return pl.pallas_call(
        body, name=name, grid=(r // tr,), in_specs=[pl.BlockSpec((n, tr, c), lambda i: (0, i, 0))],
        out_specs=pl.BlockSpec((tr, c), lambda i: (i, 0)), out_shape=jax.ShapeDtypeStruct((r, c), F32),
        compiler_params=_cparams(("parallel",)),
    )(p)


def _lay(arr, axis, pieces, total, reps=()):
    items = [(d, n, lax.slice_in_dim(arr, s0, s0 + n, axis=axis)) for s0, n, d in pieces]
    items += [(d, n, jnp.repeat(lax.slice_in_dim(arr, s0, s0 + 1, axis=axis), n, axis=axis)) for s0, d, n in reps]
    items.sort(key=lambda t: t[0])
    parts, pos = [], 0

    def zeros(n):
        sh = list(arr.shape)
        sh[axis] = n
        return jnp.zeros(sh, arr.dtype)

    for d, n, v in items:
        if d > pos:
            parts.append(zeros(d - pos))
        parts.append(v)
        pos = d + n
    if total > pos:
        parts.append(zeros(total - pos))
    return jnp.concatenate(parts, axis=axis) if len(parts) > 1 else parts[0]


def _unlay_parts(g, axis, pieces, reps=()):
    out = [(s0, lax.slice_in_dim(g, d, d + n, axis=axis)) for s0, n, d in pieces]
    out += [(s0, jnp.sum(lax.slice_in_dim(g, d, d + n, axis=axis), axis=axis, keepdims=True)) for s0, d, n in reps]
    return out


def _join(parts, axis):
    parts = sorted(parts, key=lambda t: t[0])
    return jnp.concatenate([p for _, p in parts], axis=axis)


def _heads(src0, n_heads, width, padded, dst0=0):
    return [(src0 + h * width, width, dst0 + h * padded) for h in range(n_heads)]


_XQ = lambda src0: _heads(src0, XA_HEADS, XA_HD, LANE)
_XA_W = XA_HEADS * LANE

LAYOUT = {
    'a': dict(
        segs=dict(q=(_heads(0, 4, 96, LANE), 512, ()), k=(_heads(384, 4, 96, LANE), 512, ()),
                  v=(_heads(768, 4, 192, 256), 1024, ()), glr=([(1536, 16, 0)], LANE, ()),
                  og=(_heads(1552, 4, 192, 256), 1024, ()), xq=(_XQ(2320), _XA_W, ())),
        tok=(_heads(0, 4, 192, 256), 1024), xa=(_XQ(768), _XA_W)),
    'b': dict(
        segs=dict(q=([(0, 1536, 0)], 1536, ()), k=([(1536, 1536, 0)], 1536, ()), v=([(3072, 1536, 0)], 1536, ()),
                  xq=(_XQ(4608), _XA_W, ())),
        tok=([(0, 512, 0)], 512), xa=(_XQ(512), _XA_W)),
    'c': dict(
        segs=dict(z=(_heads(0, 12, 64, LANE), 1536, ()),
                  xbc=(_heads(768, 12, 64, LANE) + [(1536, 256, 1536), (1792, 256, 1792)], 2048, ()),
                  dt=([(2048, 12, 0)], LANE, ()),
                  xq=(_XQ(2060), _XA_W, ())),
        tok=(_heads(0, 12, 64, LANE), 1536), xa=(_XQ(768), _XA_W)),
    'd': dict(
        segs=dict(q=([(0, 768, 0)], 768, ()), f=([(768, 768, 0)], 768, ()), i=([(1536, 768, 0)], 768, ()),
                  og=([(2304, 768, 0)], 768, ()), xq=(_XQ(3072), _XA_W, ())),
        tok=([(0, 768, 0)], 768), xa=(_XQ(768), _XA_W)),
}
KINDS = 'abcd'
_XS_PIECES = _heads(0, 12, 64, LANE)
_XBC_PIECES = _XS_PIECES + [(768, 256, 1536), (1024, 256, 1792)]
_HEAD_REPS = tuple((h, h * LANE, LANE) for h in range(12))


def _row(v):
    return v.reshape(1, -1)


LAYER_WEIGHTS = [
    {'w_in': (f'{k}_w_in', None), 'w_out': (f'{k}_w_out', None), 'w_kv': ('xa_w_kv', i), 'w_up': ('ffn_w_up', i),
     'w_down': ('ffn_w_down', i), **({'w_gate2': ('a_w_gate2', None)} if k == 'a' else {})}
    for i, k in enumerate('abcd')]


class LocalLayers:
    def __init__(self, W):
        self.W, self.g = W, {}

    def weights(self, i):
        return {key: (self.W[n] if l is None else self.W[n][l]).astype(MXU_DTYPE)
                for key, (n, l) in LAYER_WEIGHTS[i].items()}

    def fwd_rider(self, i):
        return None

    def bwd_rider(self, i):
        return None

    def grads(self, i, g):
        self.g[i] = g

    def whole_grads(self):
        out = {}
        for i in range(4):
            for key, (n, l) in LAYER_WEIGHTS[i].items():
                if l is None:
                    out[n] = self.g[i][key]
        for n in ('xa_w_kv', 'ffn_w_up', 'ffn_w_down'):
            key = [k for k, (m, _) in LAYER_WEIGHTS[0].items() if m == n][0]
            out[n] = jnp.stack([self.g[i][key] for i in range(4)])
        return out


class ShardedLayers:
    def __init__(self, w, core_id):
        self.core_id = core_id
        self.names, self.axes, self.shards, self.packed, self.kinds = [], [], [], [], []
        for lw in LAYER_WEIGHTS:
            keys = [k for k in lw if k not in ('w_up', 'w_down')]
            sh = {k: (w[lw[k][0]] if lw[k][1] is None else w[lw[k][0]][lw[k][1]]).astype(MXU_DTYPE) for k in lw}
            ax = {k: SHARD_AXIS[lw[k][0]] - (lw[k][1] is not None) for k in lw}
            pk = _pack([sh[k] for k in keys], MXU_DTYPE, 256)
            self.names.append(keys)
            self.axes.append(ax)
            self.shards.append(sh)
            self.packed.append(pk)
            self.kinds.append([Packed(pk.shape), SlabCols(sh['w_up'].shape), Packed(sh['w_down'].shape)])
        self.whole = {}
        self.pending = None
        self.recvd = {}

    def _operands(self, i):
        return [self.packed[i], self.shards[i]['w_up'], self.shards[i]['w_down']]

    def _gathered(self, i, res):
        per_chip = [_unpack(res[0][j], [self.shards[i][k].shape for k in self.names[i]]) for j in range(N_CHIPS)]
        out = {k: _merge_chips(jnp.stack([per_chip[j][n] for j in range(N_CHIPS)]), self.axes[i][k])
               for n, k in enumerate(self.names[i])}
        out['w_up'], out['w_down'] = res[1], res[2].reshape(-1, res[2].shape[-1])
        self.whole[i] = out

    def first_gather(self):
        self._gathered(0, run_rider(GatherRider(self._operands(0), self.kinds[0]), "allgather_chips"))

    def weights(self, i):
        return self.whole[i]

    def fwd_rider(self, i):
        return GatherRider(self._operands(i + 1), self.kinds[i + 1]) if i + 1 < 4 else None

    def fwd_rode(self, i, res):
        self._gathered(i + 1, res)

    def bwd_rider(self, i):
        return ExchangeRider(self.pending[1], self.kinds[self.pending[0]]) if self.pending is not None else None

    def bwd_rode(self, i, res):
        self.recvd[self.pending[0]] = (res, self.pending[1])
        self.pending = None

    def grads(self, i, g):
        kinds = self.kinds[i]
        gb = jnp.stack([_pack([_split_chips(g[k], self.axes[i][k])[j] for k in self.names[i]], F32, 256)
                        for j in range(N_CHIPS)])
        gs = [gb.reshape(N_CHIPS, 2, kinds[0].h, kinds[0].c), g['w_up'],
              g['w_down'].reshape(N_CHIPS, 2, kinds[2].h, kinds[2].c)]
        gots = swap_halves_sibling(gs, kinds, f"swap_halves_{i}")
        self.pending = (i, [add_own_half(a, o, k, self.core_id, GRAD_WIRE_DTYPE, f"add_own_half_{i}_{t}")
                            for t, (a, o, k) in enumerate(zip(gs, gots, kinds))])

    def finish(self, chip_id):
        last, pairs = self.pending
        self.recvd[last] = (run_rider(ExchangeRider(pairs, self.kinds[last]), "exchange_chips"), pairs)
        halves = []
        for i in range(4):
            got, pairs = self.recvd[i]
            halves += [sum_chips(r, p, k, chip_id, f"sum_chips_{i}_{t}")
                       for t, (r, p, k) in enumerate(zip(got, pairs, self.kinds[i]))]
        joined = join_halves_sibling(halves)
        out, stacked = {}, {'xa_w_kv': [], 'ffn_w_up': [], 'ffn_w_down': []}
        for i, lw in enumerate(LAYER_WEIGHTS):
            red, up, down = joined[3 * i:3 * i + 3]
            parts = _unpack(red.reshape(-1, PACK_COLS), [self.shards[i][k].shape for k in self.names[i]])
            parts = dict(zip(self.names[i], parts), w_up=up.reshape(self.shards[i]['w_up'].shape),
                         w_down=down.reshape(self.shards[i]['w_down'].shape))
            for k, (n, l) in lw.items():
                if l is None:
                    out[n] = parts[k]
                else:
                    stacked[n].append(parts[k])
        out.update({n: jnp.stack(v) for n, v in stacked.items()})
        return out


def local_step(x, mem, positions, target, W, layers=None):
    s = x.shape[0]
    grads = {}
    scan_block = CHUNK * SCAN_CHUNKS
    ffn = layers or LocalLayers(W)

    inv_freq = ROPE_THETA ** (-jnp.arange(DIL_HD // 2, dtype=F32) / (DIL_HD // 2))
    ang = positions.astype(F32)[:, None] * inv_freq
    cosf = jnp.concatenate([jnp.cos(ang), jnp.cos(ang)], axis=-1)
    sinf = jnp.concatenate([-jnp.sin(ang), jnp.sin(ang)], axis=-1)

    mem_g = _row(W['mem_norm'])
    (mem_n,) = tmap("mem_norm", _norm_stage, [mem], [mem_g], [(D_MODEL, MXU_DTYPE)])
    kv_lay = _heads(0, 4, 64, LANE) + _heads(256, 4, 64, LANE, dst0=_XA_W)

    saved = []
    for i in range(4):
        kind = KINDS[i]
        lay = LAYOUT[kind]
        sv = dict(x0=x)
        wl = ffn.weights(i)
        w_in, w_out = wl['w_in'], wl['w_out']
        sv['w_seg'] = {n: _lay(w_in, 1, p, t, r).astype(MXU_DTYPE) for n, (p, t, r) in lay['segs'].items()}
        sv['wo_tok'] = _lay(w_out, 0, *lay['tok']).astype(MXU_DTYPE)
        sv['wo_xa'] = _lay(w_out, 0, *lay['xa']).astype(MXU_DTYPE)
        sv['w_kv'] = _lay(wl['w_kv'], 1, kv_lay, 2 * _XA_W).astype(MXU_DTYPE)
        sv['g1'] = _row(W['mix_norm'][i])
        (h,) = tmap(f"mix_norm_{i}", _norm_stage, [x], [sv['g1']], [(D_MODEL, MXU_DTYPE)])
        sv['h'] = h
        seg = {n: matmul(h, w) for n, w in sv['w_seg'].items()}
        sv['seg'] = seg

        if kind == 'a':
            sv['w2'] = _lay(_lay(wl['w_gate2'], 1, _heads(0, 4, 96, LANE), 512), 0, [(0, 16, 0)], LANE)
            sv['bg'] = _row(_lay(W['a_b_gate'], 0, _heads(0, 4, 96, LANE), 512))
            sv['on'] = _row(_lay(W['a_o_norm'], 0, [(0, 192, 0)], 256))
            (la,) = tmap("gla_pre", _gla_pre, [seg['glr']], [sv['w2'], sv['bg']], [(512, F32)])
            sv['la'] = la
            sv['scan_fn'] = _gla_step(GLA_HEADS, 2 * LANE, GLA_DK ** -0.5)
            sv['scan_rows'] = [seg['q'], seg['k'], seg['v'], la]
            (o,), sv['states'] = rscan("gla_scan", sv['scan_fn'], [(LANE, 2 * LANE)] * GLA_HEADS, sv['scan_rows'], [],
                                       [(1024, F32)], scan_block)
            sv['o'] = o
            (tok,) = tmap("gla_post", _gla_post, [o, seg['og']], [sv['on']], [(1024, MXU_DTYPE)])
        elif kind == 'b':
            sv['qg'], sv['kg'] = _row(W['b_q_norm']), _row(W['b_k_norm'])
            os_, ls_ = [], []
            qkn = tmap("dil_pre", _dil_pre, [seg['q'], seg['k'], cosf, sinf], [sv['qg'], sv['kg']], [(512, F32)] * 6)
            sv['qn'], sv['kn'] = qkn[:3], qkn[3:]
            for g, (window, r) in enumerate(DIL_GROUPS):
                assert window // r == DIL_BLOCK and (s // r) % DIL_BLOCK == 0
                o, lse = dil_attn(f"dil_attn_{g}", sv['qn'][g], sv['kn'][g], seg['v'], r, g)
                os_.append(o)
                ls_.append(lse)
            sv['os'], sv['ls'] = os_, ls_
            (tok,) = tmap("dil_merge", _dil_merge, os_ + ls_, [], [(512, MXU_DTYPE)])
        elif kind == 'c':
            sv['cw'] = _lay(W['c_conv_w'], 1, _XBC_PIECES, 2048)
            sv['cb'] = _row(_lay(W['c_conv_b'], 0, _XBC_PIECES, 2048))
            sv['dtb'] = _row(_lay(W['c_dt_bias'], 0, [], 1536, _HEAD_REPS))
            sv['alog'] = _row(_lay(W['c_a_log'], 0, [], 1536, _HEAD_REPS))
            sv['dsk'] = _row(_lay(W['c_d'], 0, [], 1536, _HEAD_REPS))
            sv['cn'] = _row(_lay(W['c_norm'], 0, _XS_PIECES, 1536))
            xact = conv_fwd("ssm_conv", seg['xbc'], sv['cw'], sv['cb'], 'silu', F32, 512)
            sv['xact'] = xact
            sv['scan_rows'] = [xact, seg['dt']]
            sv['scan_params'] = [sv['dtb'], sv['alog'], sv['dsk']]
            (yv,), sv['states'] = rscan("ssd_scan", _ssd_step, [(LANE, LANE)] * SSM_HEADS, sv['scan_rows'],
                                        sv['scan_params'], [(1536, F32)], scan_block)
            sv['y'] = yv
            (tok,) = tmap("ssd_post", _mamba_post, [yv, seg['z']], [sv['cn']], [(1536, MXU_DTYPE)])
        else:
            sv['lbnd'] = W['d_lower_bounds']
            sv['on'] = _row(W['d_o_norm'])
            qq, kk, la = tmap("hgrn_pre", _hgrn_pre, [seg['q'], seg['f']], [sv['lbnd']], [(768, F32)] * 3)
            sv['scan_fn'] = _gla_step(HGRN_HEADS, LANE, 1.0)
            sv['scan_rows'] = [qq, kk, seg['i'], la]
            (o,), sv['states'] = rscan("hgrn_scan", sv['scan_fn'], [(LANE, LANE)] * HGRN_HEADS, sv['scan_rows'], [],
                                       [(768, F32)], scan_block)
            sv['o'] = o
            (tok,) = tmap("hgrn_post", _hgrn_post, [o, seg['og']], [sv['on']], [(768, MXU_DTYPE)])
        sv['tok'] = tok

        kv = matmul(mem_n, sv['w_kv'])
        sv['kv'] = kv
        sv['xqg'] = _row(_lay(W['xa_q_norm'][i], 0, [(0, 64, 0)], LANE))
        sv['xkg'] = _row(_lay(W['xa_k_norm'][i], 0, [(0, 64, 0)], LANE))
        (xa,) = tmap(f"xattn_{i}", _xattn, [seg['xq']], [kv, sv['xqg'], sv['xkg']], [(_XA_W, MXU_DTYPE)])
        sv['xa'] = xa
        x = matmul(tok, sv['wo_tok'], add=x)
        x = matmul(xa, sv['wo_xa'], add=x)
        sv['x1'] = x

        sv['g2'] = _row(W['ffn_norm'][i])
        sv['fcw'] = W['ffn_conv_w'][i]
        sv['fcb'] = _row(W['ffn_conv_b'][i])
        (h2,) = tmap(f"ffn_norm_{i}", _norm_stage, [x], [sv['g2']], [(D_MODEL, MXU_DTYPE)])
        sv['h2'] = h2
        w_up, w_down = wl['w_up'], wl['w_down']
        sv['w_up'], sv['w_down'] = w_up, w_down
        u0 = matmul(h2, w_up)
        sv['u0'] = u0
        rider = ffn.fwd_rider(i)
        act = conv_fwd("ffn_conv", u0, sv['fcw'], sv['fcb'], 'glu', MXU_DTYPE, 1408, rider=rider)
        if rider is not None:
            act, rode = act
            ffn.fwd_rode(i, rode)
        sv['act'] = act
        x = matmul(act, w_down, add=x)
        saved.append(sv)

    loss_acc, dx = loss_head(x, target)

    g_stack = {n: [None] * 4 for n in ('mix_norm', 'xa_q_norm', 'xa_k_norm', 'ffn_norm', 'ffn_conv_w', 'ffn_conv_b')}
    d_memn = None
    for i in reversed(range(4)):
        kind = KINDS[i]
        lay = LAYOUT[kind]
        sv = saved[i]
        seg = sv['seg']
        w_up, w_down = sv['w_up'], sv['w_down']
        gl = {}
        dact = matmul(dx, w_down, tb=True)
        gl['w_down'] = matmul(sv['act'], dx, ta=True)
        rider = ffn.bwd_rider(i)
        (du_g, du_v), dcw, dcb, rode = conv_bwd("ffn_conv_bwd", sv['u0'], sv['fcw'], sv['fcb'], dact, 'glu', 1408,
                                                rider=rider)
        if rider is not None:
            ffn.bwd_rode(i, rode)
        g_stack['ffn_conv_w'][i], g_stack['ffn_conv_b'][i] = dcw, dcb[0]
        dh2 = matmul(du_v, w_up, tb=True, b_koff=D_FF, add=matmul(du_g, w_up, tb=True))
        g_up = jnp.zeros((1,) + w_up.shape, F32)
        g_up = matmul(sv['h2'], du_g, ta=True, into=(g_up, 0, 0))
        g_up = matmul(sv['h2'], du_v, ta=True, into=(g_up, 0, D_FF))
        gl['w_up'] = g_up[0]
        (dx,), (dg2,) = tmap_bwd(f"ffn_norm_bwd_{i}", _norm_stage, [sv['x1']], [sv['g2']], [dh2], [True], {0: dx})
        g_stack['ffn_norm'][i] = dg2[0]
        dtok = matmul(dx, sv['wo_tok'], tb=True)
        dxa = matmul(dx, sv['wo_xa'], tb=True)
        g_wo = _unlay_parts(matmul(sv['tok'], dx, ta=True), 0, lay['tok'][0]) \
            + _unlay_parts(matmul(sv['xa'], dx, ta=True), 0, lay['xa'][0])
        gl['w_out'] = _join(g_wo, 0)
        (dxq,), (dkv, dqg, dkg) = tmap_bwd(f"xattn_bwd_{i}", _xattn, [seg['xq']], [sv['kv'], sv['xqg'], sv['xkg']],
                                           [dxa], [True])
        g_stack['xa_q_norm'][i], g_stack['xa_k_norm'][i] = dqg[0, :XA_HD], dkg[0, :XA_HD]
        gl['w_kv'] = _join(_unlay_parts(matmul(mem_n, dkv, ta=True), 1, kv_lay), 1)
        d_memn = matmul(dkv, sv['w_kv'], tb=True, add=d_memn)
        dseg = dict(xq=dxq)
        if kind == 'a':
            (do, dog), (don,) = tmap_bwd("gla_post_bwd", _gla_post, [sv['o'], seg['og']], [sv['on']], [dtok],
                                         [True, True], grad_dtype=F32)
            grads['a_o_norm'] = don[0, :GLA_DV]
            (dq, dk, dv, dla), _ = rscan_bwd("gla_scan_bwd", sv['scan_fn'], sv['states'], sv['scan_rows'], [], [do],
                                             scan_block, grad_dtype=F32)
            (dglr,), (dw2, dbg) = tmap_bwd("gla_pre_bwd", _gla_pre, [seg['glr']], [sv['w2'], sv['bg']], [dla], [True])
            gl['w_gate2'] = _join(_unlay_parts(dw2[:GLA_RANK], 1, _heads(0, 4, 96, LANE)), 1)
            grads['a_b_gate'] = _join(_unlay_parts(dbg[0], 0, _heads(0, 4, 96, LANE)), 0)
            dseg.update(q=dq, k=dk, v=dv, glr=dglr, og=dog)
        elif kind == 'b':
            res, _ = tmap_bwd("dil_merge_bwd", _dil_merge, sv['os'] + sv['ls'], [], [dtok], [True] * 6, grad_dtype=F32)
            dqn, dkn, dvs = [], [], []
            for g, (_, r) in enumerate(DIL_GROUPS):
                a_, b_, c_ = dil_attn_bwd(f"dil_attn_bwd_{g}", sv['qn'][g], sv['kn'][g], seg['v'], res[g], res[3 + g],
                                          r, g)
                dqn.append(a_)
                dkn.append(b_)
                dvs.append(c_)
            (dq, dk), (dqg, dkg) = tmap_bwd("dil_pre_bwd", _dil_pre, [seg['q'], seg['k'], cosf, sinf],
                                            [sv['qg'], sv['kg']], dqn + dkn, [True, True, False, False])
            dseg.update(q=dq, k=dk, v=jnp.concatenate(dvs, axis=1))
            grads['b_q_norm'], grads['b_k_norm'] = dqg[0], dkg[0]
        elif kind == 'c':
            (dy, dz), (dcn,) = tmap_bwd("ssd_post_bwd", _mamba_post, [sv['y'], seg['z']], [sv['cn']], [dtok],
                                        [True, True], grad_dtype=F32)
            grads['c_norm'] = _join(_unlay_parts(dcn[0], 0, _XS_PIECES), 0)
            (dxact, ddt), (ddtb, dalog, ddsk) = rscan_bwd("ssd_scan_bwd", _ssd_step, sv['states'], sv['scan_rows'],
                                                          sv['scan_params'], [dy], scan_block, grad_dtype=F32)
            for nm, gv in (('c_dt_bias', ddtb), ('c_a_log', dalog), ('c_d', ddsk)):
                grads[nm] = _join(_unlay_parts(gv[0], 0, [], _HEAD_REPS), 0)
            (dxbc,), dcw, dcb, _ = conv_bwd("ssm_conv_bwd", seg['xbc'], sv['cw'], sv['cb'], dxact, 'silu', 512)
            grads['c_conv_w'] = _join(_unlay_parts(dcw, 1, _XBC_PIECES), 1)
            grads['c_conv_b'] = _join(_unlay_parts(dcb[0], 0, _XBC_PIECES), 0)
            dseg.update(z=dz, xbc=dxbc, dt=ddt)
        else:
            (do, dog), (don,) = tmap_bwd("hgrn_post_bwd", _hgrn_post, [sv['o'], seg['og']], [sv['on']], [dtok],
                                         [True, True], grad_dtype=F32)
            grads['d_o_norm'] = don[0]
            (dqq, dkk, di, dla), _ = rscan_bwd("hgrn_scan_bwd", sv['scan_fn'], sv['states'], sv['scan_rows'], [], [do],
                                               scan_block, grad_dtype=F32)
            (dq, df), (dlb,) = tmap_bwd("hgrn_pre_bwd", _hgrn_pre, [seg['q'], seg['f']], [sv['lbnd']], [dqq, dkk, dla],
                                        [True, True])
            grads['d_lower_bounds'] = dlb
            dseg.update(q=dq, f=df, i=di, og=dog)
        dh = None
        g_in = []
        for n, (p, t, rp) in lay['segs'].items():
            dh = matmul(dseg[n], sv['w_seg'][n], tb=True, add=dh)
            g_in += _unlay_parts(matmul(sv['h'], dseg[n], ta=True), 1, p, rp)
        gl['w_in'] = _join(g_in, 1)
        ffn.grads(i, gl)
        (dx,), (dg1,) = tmap_bwd(f"mix_norm_bwd_{i}", _norm_stage, [sv['x0']], [sv['g1']], [dh], [True], {0: dx})
        g_stack['mix_norm'][i] = dg1[0]

    _, (dmg,) = tmap_bwd("mem_norm_bwd", _norm_stage, [mem], [mem_g], [d_memn], [False])
    grads['mem_norm'] = dmg[0]
    for n, parts in g_stack.items():
        grads[n] = jnp.stack(parts)
    if isinstance(ffn, LocalLayers):
        grads.update(ffn.whole_grads())
    return loss_acc, dx, grads


def _pack(arrs, dtype, row_multiple=PACK_ROWS):
    parts, rows = [], 0
    for a in arrs:
        f = a.reshape(-1).astype(dtype)
        unit = PACK_ROWS * PACK_COLS
        pad = (-f.shape[0]) % unit
        if pad:
            f = jnp.concatenate([f, jnp.zeros((pad,), dtype)])
        parts.append(f.reshape(-1, PACK_COLS))
        rows += parts[-1].shape[0]
    if rows % row_multiple:
        parts.append(jnp.zeros((row_multiple - rows % row_multiple, PACK_COLS), dtype))
    return jnp.concatenate(parts, axis=0)


def _unpack(buf, shapes):
    out, row = [], 0
    for sh in shapes:
        n = int(np.prod(sh))
        rows = -(-n // (PACK_ROWS * PACK_COLS)) * PACK_ROWS
        out.append(buf[row:row + rows].reshape(-1)[:n].reshape(sh))
        row += rows
    return out


def _pack_rows(arrs):
    parts = []
    for a in arrs:
        f = a.reshape(-1).astype(F32)
        parts.append(jnp.pad(f, (0, (-f.shape[0]) % PACK_COLS)))
    flat = jnp.concatenate(parts)
    rows = flat.shape[0] // PACK_COLS
    return jnp.pad(flat, (0, (-rows % 16) * PACK_COLS)).reshape(-1, PACK_COLS)


def _unpack_rows(buf, shapes):
    flat, out, pos = buf.reshape(-1), [], 0
    for sh in shapes:
        n = int(np.prod(sh))
        out.append(flat[pos:pos + n].reshape(sh))
        pos += -(-n // PACK_COLS) * PACK_COLS
    return out


def _split_chips(a, axis):
    sh = a.shape
    return jnp.moveaxis(a.reshape(sh[:axis] + (N_CHIPS, sh[axis] // N_CHIPS) + sh[axis + 1:]), axis, 0)


def _merge_chips(a, axis):
    a = jnp.moveaxis(a, 0, axis)
    sh = a.shape
    return a.reshape(sh[:axis] + (sh[axis] * sh[axis + 1],) + sh[axis + 2:])


def kernel(x, mem, positions, mem_norm, mix_norm, xa_w_kv, xa_q_norm, xa_k_norm, ffn_norm, ffn_w_up, ffn_conv_w, ffn_conv_b, ffn_w_down, a_w_in, a_w_gate2, a_b_gate, a_o_norm, a_w_out, b_w_in, b_q_norm, b_k_norm, b_w_out, c_w_in, c_conv_w, c_conv_b, c_dt_bias, c_a_log, c_d, c_norm, c_w_out, d_w_in, d_lower_bounds, d_o_norm, d_w_out, loss_target, m_mem_norm, m_mix_norm, m_xa_w_kv, m_xa_q_norm, m_xa_k_norm, m_ffn_norm, m_ffn_w_up, m_ffn_conv_w, m_ffn_conv_b, m_ffn_w_down, m_a_w_in, m_a_w_gate2, m_a_b_gate, m_a_o_norm, m_a_w_out, m_b_w_in, m_b_q_norm, m_b_k_norm, m_b_w_out, m_c_w_in, m_c_conv_w, m_c_conv_b, m_c_dt_bias, m_c_a_log, m_c_d, m_c_norm, m_c_w_out, m_d_w_in, m_d_lower_bounds, m_d_o_norm, m_d_w_out, v_mem_norm, v_mix_norm, v_xa_w_kv, v_xa_q_norm, v_xa_k_norm, v_ffn_norm, v_ffn_w_up, v_ffn_conv_w, v_ffn_conv_b, v_ffn_w_down, v_a_w_in, v_a_w_gate2, v_a_b_gate, v_a_o_norm, v_a_w_out, v_b_w_in, v_b_q_norm, v_b_k_norm, v_b_w_out, v_c_w_in, v_c_conv_w, v_c_conv_b, v_c_dt_bias, v_c_a_log, v_c_d, v_c_norm, v_c_w_out, v_d_w_in, v_d_lower_bounds, v_d_o_norm, v_d_w_out):
    args = locals()
    w = {n: args[n] for n in WEIGHTS}
    m = {n: args['m_' + n] for n in WEIGHTS}
    v = {n: args['v_' + n] for n in WEIGHTS}
    cx, cy, cc = lax.axis_index("x"), lax.axis_index("y"), lax.axis_index("c")
    chip = 2 * cx + cy

    core_id, chip_id = cc.reshape(1).astype(jnp.int32), chip.reshape(1).astype(jnp.int32)
    layers = ShardedLayers(w, core_id)
    layers.first_gather()
    full = {}
    small_sharded = [n for n in SMALL if n in SHARD_AXIS]
    sg = allgather_devices(_pack([w[n] for n in small_sharded], F32))
    per_chip_s = [_unpack(sg[2 * j], [w[n].shape for n in small_sharded]) for j in range(N_CHIPS)]
    for k, n in enumerate(small_sharded):
        full[n] = _merge_chips(jnp.stack([per_chip_s[j][k] for j in range(N_CHIPS)]), SHARD_AXIS[n])
    for n in SMALL:
        if n not in SHARD_AXIS:
            full[n] = w[n]

    loss_acc, dx, grads = local_step(x[0], mem[0], positions[0], loss_target[0], full, layers)
    loss = lax.psum(jnp.sum(loss_acc), ("x", "y", "c"))

    g_big = layers.finish(chip_id)

    small_full_shapes = [grads[n].shape for n in SMALL]
    gs = sum_slabs(allgather_devices(_pack_rows([grads[n] for n in SMALL])), "sum_devices")
    g_small = {}
    for n, gfull in zip(SMALL, _unpack_rows(gs, small_full_shapes)):
        if n in SHARD_AXIS:
            ax = SHARD_AXIS[n]
            size = gfull.shape[ax] // N_CHIPS
            gfull = lax.dynamic_slice_in_dim(gfull, chip * size, size, axis=ax)
        g_small[n] = gfull

    g_out, delta, new_m, new_v = {**g_big, **g_small}, {}, {}, {}
    for n in WEIGHTS:
        sh = w[n].shape
        two_d = (-1, sh[-1])
        d_, m_, v_ = adamw(f"adamw_{n}", w[n].reshape(two_d), g_out[n].reshape(two_d), m[n].reshape(two_d),
                           v[n].reshape(two_d))
        delta[n], new_m[n], new_v[n] = d_.reshape(sh), m_.reshape(sh), v_.reshape(sh)

    return (loss, dx[None], *[g_out[n] for n in WEIGHTS], *[delta[n] for n in WEIGHTS],
            *[new_m[n] for n in WEIGHTS], *[new_v[n] for n in WEIGHTS])
```

```python
import functools
import math

import jax
import jax.numpy as jnp
import numpy as np
from jax import lax
from jax.experimental import pallas as pl
from jax.experimental.pallas import tpu as pltpu

F32 = jnp.float32
MXU_DTYPE = jnp.bfloat16
GRAD_WIRE_DTYPE = jnp.bfloat16
VMEM_LIMIT_V7X = 56 * 1024 * 1024
LANE = 128
SUBLANE = 8

D_MODEL = 1024
N_MEM = 256
EPS = 1e-6
ROPE_THETA = 10000.0
CHUNK = 64
XA_HEADS, XA_HD = 4, 64
GLA_HEADS, GLA_DK, GLA_DV, GLA_RANK, GLA_GATE_NORM = 4, 96, 192, 16, 16.0
DIL_GROUPS = ((128, 1), (512, 4), (2048, 16))
DIL_HEADS, DIL_HD, DIL_BLOCK = 4, 128, 128
SSM_HD, SSM_HEADS, SSM_GROUPS, SSM_STATE, SSM_CONV = 64, 12, 2, 128, 4
HGRN_HEADS, HGRN_DK = 6, 128
D_FF = 2816
FFN_CONV = 3
ADAM_LR, ADAM_B1, ADAM_B2, ADAM_EPS, ADAM_WD, ADAM_STEP = 0.001, 0.9, 0.999, 1e-08, 0.01, 10

MM_TILES = (2816, 1408, 1024, 768, 512, 384, 256, 128)
MM_K_TILES = (2816, 2560, 2048, 1792, 1536, 1408, 1024, 768, 512, 384, 256, 128)
MM_MIN_OUT_TILE = 512 * 1024
MM_VMEM_BUDGET = 40 * 1024 * 1024
ROW_BLOCK = 256
SCAN_CHUNKS = 2
PACK_COLS = 1024
PACK_ROWS = 32

WEIGHTS = ['mem_norm', 'mix_norm', 'xa_w_kv', 'xa_q_norm', 'xa_k_norm', 'ffn_norm', 'ffn_w_up', 'ffn_conv_w',
           'ffn_conv_b', 'ffn_w_down', 'a_w_in', 'a_w_gate2', 'a_b_gate', 'a_o_norm', 'a_w_out', 'b_w_in', 'b_q_norm',
           'b_k_norm', 'b_w_out', 'c_w_in', 'c_conv_w', 'c_conv_b', 'c_dt_bias', 'c_a_log', 'c_d', 'c_norm', 'c_w_out',
           'd_w_in', 'd_lower_bounds', 'd_o_norm', 'd_w_out']
SHARD_AXIS = {'xa_w_kv': 1, 'ffn_w_up': 2, 'ffn_conv_w': 2, 'ffn_w_down': 1, 'a_w_in': 1, 'a_w_gate2': 1, 'a_w_out': 0,
              'b_w_in': 1, 'b_w_out': 1, 'c_w_in': 1, 'c_conv_w': 1, 'c_w_out': 0, 'd_w_in': 1, 'd_w_out': 0}
BIG = ['xa_w_kv', 'ffn_w_up', 'ffn_w_down', 'a_w_in', 'a_w_gate2', 'a_w_out', 'b_w_in', 'b_w_out', 'c_w_in', 'c_w_out',
       'd_w_in', 'd_w_out']
SMALL = [n for n in WEIGHTS if n not in BIG]
LAYERED = ['ffn_w_up', 'ffn_w_down']
N_CHIPS = 4
N_DEV = 8


class _MatmulSet:
    def __init__(self, cast, precision):
        def dot(a, b, dims):
            if cast:
                a = a.astype(MXU_DTYPE)
                b = b.astype(MXU_DTYPE)
            return lax.dot_general(a, b, (dims, ((), ())), precision=precision, preferred_element_type=F32)

        @jax.custom_vjp
        def nn(a, b):
            return dot(a, b, ((1,), (0,)))

        @jax.custom_vjp
        def nt(a, b):
            return dot(a, b, ((1,), (1,)))

        @jax.custom_vjp
        def tn(a, b):
            return dot(a, b, ((0,), (0,)))

        nn.defvjp(lambda a, b: (nn(a, b), (a, b)), lambda r, g: (nt(g, r[1]), tn(r[0], g)))
        nt.defvjp(lambda a, b: (nt(a, b), (a, b)), lambda r, g: (nn(g, r[1]), tn(g, r[0])))
        tn.defvjp(lambda a, b: (tn(a, b), (a, b)), lambda r, g: (nt(r[1], g), nn(r[0], g)))
        self.nn, self.nt, self.tn = nn, nt, tn


mm = _MatmulSet(True, None)
hi = _MatmulSet(False, lax.Precision.HIGHEST)


def _sigmoid(x):
    return jax.nn.sigmoid(x)


def _silu(x):
    return x * jax.nn.sigmoid(x)


def _softplus(x):
    return jnp.maximum(x, 0.0) + jnp.log1p(jnp.exp(-jnp.abs(x)))


def _rms(x, g, n_real=None):
    n = n_real or x.shape[-1]
    ms = jnp.sum(x * x, axis=-1, keepdims=True) * (1.0 / n)
    return x * lax.rsqrt(ms + EPS) * g


@jax.custom_vjp
def _swap_halves(x):
    return pltpu.roll(x, 64, 1)


_swap_halves.defvjp(lambda x: (_swap_halves(x), None), lambda _, g: (_swap_halves(g),))


def _tile(n, cands):
    for c in cands:
        if n % c == 0:
            return c
    raise ValueError(f"no tile for {n} among {cands}")


def _cparams(sem):
    return pltpu.CompilerParams(dimension_semantics=sem, vmem_limit_bytes=VMEM_LIMIT_V7X)


def _f32(v):
    return v.astype(F32) if jnp.issubdtype(v.dtype, jnp.floating) else v


def matmul(a, b, *, ta=False, tb=False, add=None, out_dtype=F32, b_layer=None, b_koff=0, into=None):
    m, k = (a.shape[1], a.shape[0]) if ta else a.shape
    b2 = b.shape[1:] if b_layer is not None else b.shape
    n = b2[0] if tb else b2[1]
    assert b_koff + k <= (b2[1] if tb else b2[0]), (a.shape, b.shape, ta, tb, b_koff)
    sa, sb, so = a.dtype.itemsize, b.dtype.itemsize, jnp.dtype(out_dtype).itemsize
    n_align = math.gcd(n, into[2]) if into is not None and into[2] else n
    k_align = math.gcd(k, b_koff) if b_koff else k

    def vmem(tm_, tn_, tk_):
        return (2 * tm_ * tk_ * sa + 2 * tk_ * tn_ * sb + 2 * tm_ * tn_ * so + (tm_ * tn_ * 4 if tk_ < k else 0)
                + (2 * tm_ * tn_ * add.dtype.itemsize if add is not None else 0))

    for tk in [t for t in MM_K_TILES if k_align % t == 0]:
        fits = [(tm_ * tn_, tm_, tn_) for tm_ in MM_TILES if m % tm_ == 0 for tn_ in MM_TILES
                if n % tn_ == 0 and n_align % tn_ == 0 and vmem(tm_, tn_, tk) <= MM_VMEM_BUDGET]
        if fits and (max(fits)[0] >= min(MM_MIN_OUT_TILE, m * n) or tk == MM_K_TILES[-1]):
            break
    _, tm, tn = max(fits)
    nk = k // tk
    dims = (((0,) if ta else (1,)), ((1,) if tb else (0,)))
    n_extra = (add is not None) + (into is not None)

    def body(*refs):
        a_ref, b_ref = refs[0], refs[1]
        add_ref = refs[2] if add is not None else None
        o_ref = refs[2 + n_extra]
        part = lax.dot_general(a_ref[...].astype(MXU_DTYPE), b_ref[...].astype(MXU_DTYPE), (dims, ((), ())),
                               preferred_element_type=F32)

        def finish(r):
            if add_ref is not None:
                r = r + add_ref[...].astype(F32)
            o_ref[...] = r.astype(o_ref.dtype)

        if nk == 1:
            finish(part)
            return
        acc = refs[-1]
        kk = pl.program_id(2)

        @pl.when(kk == 0)
        def _():
            acc[...] = part

        @pl.when(kk > 0)
        def _():
            acc[...] += part

        @pl.when(kk == nk - 1)
        def _():
            finish(acc[...])

    a_spec = pl.BlockSpec((tk, tm), lambda i, j, q: (q, i)) if ta else pl.BlockSpec((tm, tk), lambda i, j, q: (i, q))
    ko = b_koff // tk
    if b_layer is None:
        b_spec = (pl.BlockSpec((tn, tk), lambda i, j, q: (j, q + ko)) if tb
                  else pl.BlockSpec((tk, tn), lambda i, j, q: (q + ko, j)))
    else:
        b_spec = (pl.BlockSpec((None, tn, tk), lambda i, j, q: (b_layer, j, q + ko)) if tb
                  else pl.BlockSpec((None, tk, tn), lambda i, j, q: (b_layer, q + ko, j)))
    o_spec = pl.BlockSpec((tm, tn), lambda i, j, q: (i, j))
    ins, specs = [a, b], [a_spec, b_spec]
    if add is not None:
        ins.append(add)
        specs.append(o_spec)
    aliases = {}
    out_shape = jax.ShapeDtypeStruct((m, n), out_dtype)
    if into is not None:
        buf, layer, col0 = into
        assert buf.shape[1] == m and buf.dtype == out_dtype
        co = col0 // tn
        ins.append(buf)
        specs.append(_ANY)
        aliases = {len(ins) - 1: 0}
        o_spec = pl.BlockSpec((None, tm, tn), lambda i, j, q: (layer, i, j + co))
        out_shape = jax.ShapeDtypeStruct(buf.shape, buf.dtype)
    return pl.pallas_call(
        body, name=f"mm_{m}x{k}x{n}_{int(ta)}{int(tb)}{int(add is not None)}{int(b_layer is not None)}{int(into is not None)}",
        grid=(m // tm, n // tn, nk), in_specs=specs, out_specs=o_spec, out_shape=out_shape,
        input_output_aliases=aliases,
        scratch_shapes=[pltpu.VMEM((tm, tn), F32)] if nk > 1 else [],
        compiler_params=_cparams(("parallel", "parallel", "arbitrary")),
    )(*ins)


def _row_spec(a, block):
    return pl.BlockSpec((block, a.shape[1]), lambda i: (i, 0))


def _whole_spec(a):
    return pl.BlockSpec(a.shape, lambda i: (0,) * a.ndim)


def tmap(name, fn, rows, params, outs, block=ROW_BLOCK):
    s = rows[0].shape[0]
    block = min(block, s)
    nr, npar = len(rows), len(params)

    def body(*refs):
        res = fn(*[_f32(r[...]) for r in refs[:nr]], *[_f32(p[...]) for p in refs[nr:nr + npar]])
        for o_ref, v in zip(refs[nr + npar:], res, strict=True):
            o_ref[...] = v.astype(o_ref.dtype)

    return pl.pallas_call(
        body, name=name, grid=(s // block,),
        in_specs=[_row_spec(a, block) for a in rows] + [_whole_spec(p) for p in params],
        out_specs=[pl.BlockSpec((block, w), lambda i: (i, 0)) for w, _ in outs],
        out_shape=[jax.ShapeDtypeStruct((s, w), dt) for w, dt in outs],
        compiler_params=_cparams(("parallel",)),
    )(*rows, *params)


def tmap_bwd(name, fn, rows, params, douts, row_grad, row_add=None, grad_dtype=None, block=ROW_BLOCK):
    s = rows[0].shape[0]
    block = min(block, s)
    grad_dtype = grad_dtype or MXU_DTYPE
    nr, npar, nd = len(rows), len(params), len(douts)
    gr = [i for i in range(nr) if row_grad[i]]
    row_add = row_add or {}
    adds = [row_add[i] for i in gr if i in row_add]

    def body(*refs):
        rv = [_f32(r[...]) for r in refs[:nr]]
        pv = [_f32(p[...]) for p in refs[nr:nr + npar]]
        dv = tuple(_f32(d[...]) for d in refs[nr + npar:nr + npar + nd])
        add_refs = list(refs[nr + npar + nd:nr + npar + nd + len(adds)])
        out_refs = refs[nr + npar + nd + len(adds):]

        def f(*diff):
            rr = list(rv)
            for n_, i_ in enumerate(gr):
                rr[i_] = diff[n_]
            return tuple(fn(*rr, *diff[len(gr):]))

        _, vjp = jax.vjp(f, *[rv[i_] for i_ in gr], *pv)
        g = vjp(dv)
        for n_, i_ in enumerate(gr):
            v = g[n_]
            if i_ in row_add:
                v = v + add_refs.pop(0)[...].astype(F32)
            out_refs[n_][...] = v.astype(out_refs[n_].dtype)
        first = pl.program_id(0) == 0
        for n_ in range(npar):
            ref = out_refs[len(gr) + n_]

            @pl.when(first)
            def _(ref=ref):
                ref[...] = jnp.zeros_like(ref)

            ref[...] += g[len(gr) + n_]

    res = pl.pallas_call(
        body, name=name, grid=(s // block,),
        in_specs=[_row_spec(a, block) for a in rows] + [_whole_spec(p) for p in params]
        + [_row_spec(d, block) for d in douts] + [_row_spec(a, block) for a in adds],
        out_specs=[_row_spec(rows[i], block) for i in gr] + [_whole_spec(p) for p in params],
        out_shape=[jax.ShapeDtypeStruct(rows[i].shape, F32 if i in row_add else grad_dtype) for i in gr]
        + [jax.ShapeDtypeStruct(p.shape, F32) for p in params],
        compiler_params=_cparams(("arbitrary",)),
    )(*rows, *params, *douts, *adds)
    return list(res[:len(gr)]), list(res[len(gr):])


def rscan(name, fn, state_shapes, rows, params, outs, block):
    s = rows[0].shape[0]
    nsteps = s // block
    nr, npar, no, ns = len(rows), len(params), len(outs), len(state_shapes)

    def body(*refs):
        out_refs = refs[nr + npar:nr + npar + no]
        sav_refs = refs[nr + npar + no:nr + npar + no + ns]
        st_refs = refs[nr + npar + no + ns:]

        @pl.when(pl.program_id(0) == 0)
        def _():
            for st in st_refs:
                st[...] = jnp.zeros_like(st)

        sts = tuple(st[...] for st in st_refs)
        for sv, v in zip(sav_refs, sts):
            sv[...] = v
        new, res = fn(sts, *[_f32(r[...]) for r in refs[:nr]], *[_f32(p[...]) for p in refs[nr:nr + npar]])
        for st, v in zip(st_refs, new, strict=True):
            st[...] = v
        for o_ref, v in zip(out_refs, res, strict=True):
            o_ref[...] = v.astype(o_ref.dtype)

    res = pl.pallas_call(
        body, name=name, grid=(nsteps,),
        in_specs=[_row_spec(a, block) for a in rows] + [_whole_spec(p) for p in params],
        out_specs=[pl.BlockSpec((block, w), lambda i: (i, 0)) for w, _ in outs]
        + [pl.BlockSpec(sh, lambda i: (i, 0)) for sh in state_shapes],
        out_shape=[jax.ShapeDtypeStruct((s, w), dt) for w, dt in outs]
        + [jax.ShapeDtypeStruct((nsteps * sh[0], sh[1]), F32) for sh in state_shapes],
        scratch_shapes=[pltpu.VMEM(sh, F32) for sh in state_shapes],
        compiler_params=_cparams(("arbitrary",)),
    )(*rows, *params)
    return list(res[:no]), list(res[no:])


def rscan_bwd(name, fn, saved, rows, params, douts, block, grad_dtype=None):
    s = rows[0].shape[0]
    nsteps = s // block
    grad_dtype = grad_dtype or MXU_DTYPE
    nr, npar, nd, ns = len(rows), len(params), len(douts), len(saved)
    state_shapes = [(sv.shape[0] // nsteps, sv.shape[1]) for sv in saved]

    def body(*refs):
        rv = [_f32(r[...]) for r in refs[:nr]]
        pv = [_f32(p[...]) for p in refs[nr:nr + npar]]
        dv = tuple(_f32(d[...]) for d in refs[nr + npar:nr + npar + nd])
        sv = tuple(x[...] for x in refs[nr + npar + nd:nr + npar + nd + ns])
        out_refs = refs[nr + npar + nd + ns:nr + npar + nd + ns + nr + npar]
        dst_refs = refs[nr + npar + nd + ns + nr + npar:]
        first = pl.program_id(0) == 0

        @pl.when(first)
        def _():
            for d in dst_refs:
                d[...] = jnp.zeros_like(d)

        def f(sts, *args):
            return fn(sts, *args)

        _, vjp = jax.vjp(f, sv, *rv, *pv)
        g = vjp((tuple(d[...] for d in dst_refs), dv))
        for d, v in zip(dst_refs, g[0], strict=True):
            d[...] = v
        for n_ in range(nr):
            out_refs[n_][...] = g[1 + n_].astype(out_refs[n_].dtype)
        for n_ in range(npar):
            ref = out_refs[nr + n_]

            @pl.when(first)
            def _(ref=ref):
                ref[...] = jnp.zeros_like(ref)

            ref[...] += g[1 + nr + n_]

    rev = lambda i: (nsteps - 1 - i, 0)
    res = pl.pallas_call(
        body, name=name, grid=(nsteps,),
        in_specs=[pl.BlockSpec((block, a.shape[1]), rev) for a in rows] + [_whole_spec(p) for p in params]
        + [pl.BlockSpec((block, d.shape[1]), rev) for d in douts] + [pl.BlockSpec(sh, rev) for sh in state_shapes],
        out_specs=[pl.BlockSpec((block, a.shape[1]), rev) for a in rows] + [_whole_spec(p) for p in params],
        out_shape=[jax.ShapeDtypeStruct(a.shape, grad_dtype) for a in rows]
        + [jax.ShapeDtypeStruct(p.shape, F32) for p in params],
        scratch_shapes=[pltpu.VMEM(sh, F32) for sh in state_shapes],
        compiler_params=_cparams(("arbitrary",)),
    )(*rows, *params, *douts, *saved)
    return list(res[:nr]), list(res[nr:])


def _norm_stage(x, g):
    return (_rms(x, g),)


def _tril():
    r = lax.broadcasted_iota(jnp.int32, (CHUNK, CHUNK), 0)
    c = lax.broadcasted_iota(jnp.int32, (CHUNK, CHUNK), 1)
    return r >= c


def _gla_chunk(st, q, k, v, la, b):
    tril = _tril()
    rowi = lax.broadcasted_iota(jnp.int32, (CHUNK, 1), 0)
    b_last = jnp.sum(la, axis=0, keepdims=True)
    b_ref = jnp.sum(jnp.where(rowi < CHUNK // 2, la, 0.0), axis=0, keepdims=True)
    att = mm.nt(q * jnp.exp(b - b_ref), k * jnp.exp(b_ref - b))
    att = jnp.where(tril, att, 0.0)
    o = mm.nn(att, v) + mm.nn(q * jnp.exp(b), st)
    decay = jnp.exp(jnp.broadcast_to(b_last, (LANE, LANE)).T)
    decay = jnp.concatenate([decay] * (v.shape[1] // LANE), axis=1)
    st2 = decay * st + mm.tn(k * jnp.exp(b_last - b), v)
    return st2, o


def _gla_step(heads, vp, scale):
    kp = LANE

    def fn(states, q, k, v, la):
        sts = list(states)
        trif = _tril().astype(F32)
        rows = []
        for c in range(q.shape[0] // CHUNK):
            r = slice(c * CHUNK, (c + 1) * CHUNK)
            b_all = hi.nn(trif, la[r])
            oh = []
            for h in range(heads):
                ks, vs = slice(h * kp, (h + 1) * kp), slice(h * vp, (h + 1) * vp)
                qh = q[r, ks] * scale if scale != 1.0 else q[r, ks]
                sts[h], o = _gla_chunk(sts[h], qh, k[r, ks], v[r, vs], la[r, ks], b_all[:, ks])
                oh.append(o)
            rows.append(jnp.concatenate(oh, axis=1))
        return tuple(sts), (jnp.concatenate(rows, axis=0),)

    return fn


def _ssd_step(states, xa, dtr, dtb, alog, dsk):
    sts = list(states)
    trif = _tril().astype(F32)
    wide = lax.broadcasted_iota(jnp.int32, (CHUNK, LANE), 0) >= lax.broadcasted_iota(jnp.int32, (CHUNK, LANE), 1)
    hg = SSM_HEADS // SSM_GROUPS
    xw = SSM_HEADS * LANE
    lane, head = lax.broadcasted_iota(jnp.int32, (LANE, xw), 1), lax.broadcasted_iota(jnp.int32, (LANE, xw), 0)
    spread = ((lane >= head * LANE) & (lane < (head + 1) * LANE)).astype(F32)
    neg_a = -jnp.exp(alog)
    pad = jnp.zeros((CHUNK, LANE), F32)
    rows = []
    for c in range(xa.shape[0] // CHUNK):
        r = slice(c * CHUNK, (c + 1) * CHUNK)
        dt_all = _softplus(hi.nn(dtr[r], spread) + dtb)
        a_all = dt_all * neg_a
        acs_all = hi.nn(trif, a_all)
        last_all = jnp.sum(a_all, axis=0, keepdims=True)
        yh = []
        for g in range(SSM_GROUPS):
            bm = xa[r, xw + g * LANE:xw + (g + 1) * LANE]
            cm = xa[r, xw + (SSM_GROUPS + g) * LANE:xw + (SSM_GROUPS + g + 1) * LANE]
            cb = mm.nt(cm, jnp.concatenate([bm, pad], axis=0))
            for hh in range(hg):
                h = g * hg + hh
                ls = slice(h * LANE, (h + 1) * LANE)
                xs, acs, acs_last = xa[r, ls], acs_all[:, ls], last_all[:, ls]
                xdt = xs * dt_all[:, ls]
                seg = acs - jnp.concatenate([acs, pad], axis=0).T[:CHUNK]
                lmat = jnp.exp(jnp.where(wide, seg, -1e30))
                y = (mm.nn(cb * lmat, jnp.concatenate([xdt, pad], axis=0)) + mm.nn(cm, sts[h]) * jnp.exp(acs)
                     + dsk[:, ls] * xs)
                sts[h] = jnp.exp(acs_last) * sts[h] + mm.tn(bm, xdt * jnp.exp(acs_last - acs))
                yh.append(y)
        rows.append(jnp.concatenate(yh, axis=1))
    return tuple(sts), (jnp.concatenate(rows, axis=0),)


def _gla_pre(glr, w2, bg):
    z = mm.nn(glr, w2) + bg
    return (-_softplus(-z) * (1.0 / GLA_GATE_NORM),)


def _gla_post(o, og, g):
    w = 2 * LANE
    return (jnp.concatenate([_rms(o[:, h * w:(h + 1) * w], g, GLA_DV) * _silu(og[:, h * w:(h + 1) * w])
                             for h in range(GLA_HEADS)], axis=1),)


def _hgrn_pre(q, f, lbnd):
    e = jnp.exp(lbnd - jnp.max(lbnd, axis=0, keepdims=True))
    rowi = lax.broadcasted_iota(jnp.int32, e.shape, 0)
    lb = jnp.sum(jnp.where(rowi >= 1, e, 0.0), axis=0, keepdims=True) / jnp.sum(e, axis=0, keepdims=True)
    fg = lb + (1.0 - lb) * _sigmoid(f)
    return _silu(q), 1.0 - fg, jnp.log(fg)


def _hgrn_post(o, og, g):
    return (jnp.concatenate([_rms(o[:, h * LANE:(h + 1) * LANE], g) for h in range(HGRN_HEADS)], axis=1)
            * _sigmoid(og),)


def _mamba_post(y, z, g):
    v = y * _silu(z)
    w = (SSM_HEADS // SSM_GROUPS) * LANE
    n_real = (SSM_HEADS // SSM_GROUPS) * SSM_HD
    return (jnp.concatenate([_rms(v[:, i * w:(i + 1) * w], g[:, i * w:(i + 1) * w], n_real)
                             for i in range(SSM_GROUPS)], axis=1),)


def _dil_pre(q, k, cosf, sinf, qg, kg):
    def groups(x, g):
        out = []
        for grp in range(len(DIL_GROUPS)):
            hs = []
            for h in range(grp * DIL_HEADS, (grp + 1) * DIL_HEADS):
                n = _rms(x[:, h * LANE:(h + 1) * LANE], g)
                hs.append(n * cosf + _swap_halves(n) * sinf)
            out.append(jnp.concatenate(hs, axis=1))
        return out

    return (*groups(q, qg), *groups(k, kg))


def _dil_merge(o0, o1, o2, l0, l1, l2):
    m = jnp.maximum(jnp.maximum(l0, l1), l2)
    e0, e1, e2 = jnp.exp(l0 - m), jnp.exp(l1 - m), jnp.exp(l2 - m)
    return ((e0 * o0 + e1 * o1 + e2 * o2) / (e0 + e1 + e2),)


def _dil_block(q, kp, kc, vp, vc, lim):
    kk = jnp.concatenate([kp, kc], axis=0)
    vv = jnp.concatenate([vp, vc], axis=0)
    s = mm.nt(q, kk) * (DIL_HD ** -0.5)
    i = lax.broadcasted_iota(jnp.int32, s.shape, 0)
    j = lax.broadcasted_iota(jnp.int32, s.shape, 1)
    dist = DIL_BLOCK + i - j
    s = jnp.where((dist >= 0) & (dist <= DIL_BLOCK) & (j >= lim), s, -1e30)
    m = jnp.max(s, axis=-1, keepdims=True)
    p = jnp.exp(s - m)
    l = jnp.sum(p, axis=-1, keepdims=True)
    return mm.nn(p / l, vv), jnp.broadcast_to(m + jnp.log(l), (q.shape[0], LANE))


def _xattn(xq, kv, qg, kg):
    w = XA_HEADS * LANE
    os_ = []
    for h in range(XA_HEADS):
        ls = slice(h * LANE, (h + 1) * LANE)
        q = _rms(xq[:, ls], qg, XA_HD)
        k = _rms(kv[:, ls], kg, XA_HD)
        s = mm.nt(q, k) * (XA_HD ** -0.5)
        p = jnp.exp(s - jnp.max(s, axis=-1, keepdims=True))
        p = p / jnp.sum(p, axis=-1, keepdims=True)
        os_.append(mm.nn(p, kv[:, w + h * LANE:w + (h + 1) * LANE]))
    return (jnp.concatenate(os_, axis=1),)


def _dil_geometry(s, w, r, g, v_cols):
    hb = DIL_HEADS if r == 1 else 1
    rb = DIL_BLOCK * r
    nb = s // rb
    bw = hb * LANE
    v_col0 = g * (w // bw)
    assert v_cols % bw == 0 and s % rb == 0
    return hb, rb, nb, bw, v_col0


def _sub(r, res):
    return pl.ds(res, DIL_BLOCK, stride=r) if r > 1 else slice(None)


def dil_attn(name, q, k, v, r, g):
    s, w = q.shape
    hb, rb, nb, bw, v_col0 = _dil_geometry(s, w, r, g, v.shape[1])

    def body(q_r, kp_r, kc_r, vp_r, vc_r, o_r, l_r):
        lim = jnp.where(pl.program_id(1) == 0, DIL_BLOCK, 0)
        for res in range(r):
            rows = _sub(r, res)
            for h in range(hb):
                ls = slice(h * LANE, (h + 1) * LANE)
                o, lse = _dil_block(q_r[rows, ls], kp_r[rows, ls], kc_r[rows, ls], vp_r[rows, ls], vc_r[rows, ls], lim)
                o_r[rows, ls] = o
                l_r[rows, ls] = lse

    cur = pl.BlockSpec((rb, bw), lambda hblk, n: (n, hblk))
    prev = pl.BlockSpec((rb, bw), lambda hblk, n: (jnp.maximum(n - 1, 0), hblk))
    vcur = pl.BlockSpec((rb, bw), lambda hblk, n: (n, v_col0 + hblk))
    vprev = pl.BlockSpec((rb, bw), lambda hblk, n: (jnp.maximum(n - 1, 0), v_col0 + hblk))
    return pl.pallas_call(
        body, name=name, grid=(w // bw, nb), in_specs=[cur, prev, cur, vprev, vcur], out_specs=[cur, cur],
        out_shape=[jax.ShapeDtypeStruct((s, w), F32)] * 2,
        compiler_params=_cparams(("parallel", "parallel")),
    )(q, k, k, v, v)


def dil_attn_bwd(name, q, k, v, do, dlse, r, g):
    s, w = q.shape
    hb, rb, nb, bw, v_col0 = _dil_geometry(s, w, r, g, v.shape[1])

    def body(q_r, kp_r, kc_r, vp_r, vc_r, do_r, dl_r, dq_r, dk_r, dv_r, ck, cv):
        i = pl.program_id(1)
        lim = jnp.where(i == nb - 1, DIL_BLOCK, 0)

        @pl.when(i == 0)
        def _():
            ck[...] = jnp.zeros_like(ck)
            cv[...] = jnp.zeros_like(cv)

        for res in range(r):
            rows = _sub(r, res)
            for h in range(hb):
                ls = slice(h * LANE, (h + 1) * LANE)
                _, vjp = jax.vjp(functools.partial(_dil_block, lim=lim),
                                 q_r[rows, ls], kp_r[rows, ls], kc_r[rows, ls], vp_r[rows, ls], vc_r[rows, ls])
                gq, gkp, gkc, gvp, gvc = vjp((do_r[rows, ls], dl_r[rows, ls]))
                dq_r[rows, ls] = gq
                dk_r[rows, ls] = gkc + ck[rows, ls]
                dv_r[rows, ls] = gvc + cv[rows, ls]
                ck[rows, ls] = gkp
                cv[rows, ls] = gvp

    cur = pl.BlockSpec((rb, bw), lambda hblk, i: (nb - 1 - i, hblk))
    prev = pl.BlockSpec((rb, bw), lambda hblk, i: (jnp.maximum(nb - 2 - i, 0), hblk))
    vcur = pl.BlockSpec((rb, bw), lambda hblk, i: (nb - 1 - i, v_col0 + hblk))
    vprev = pl.BlockSpec((rb, bw), lambda hblk, i: (jnp.maximum(nb - 2 - i, 0), v_col0 + hblk))
    return pl.pallas_call(
        body, name=name, grid=(w // bw, nb), in_specs=[cur, prev, cur, vprev, vcur, cur, cur],
        out_specs=[cur, cur, cur], out_shape=[jax.ShapeDtypeStruct((s, w), F32)] * 3,
        scratch_shapes=[pltpu.VMEM((rb, bw), F32)] * 2,
        compiler_params=_cparams(("parallel", "arbitrary")),
    )(q, k, k, v, v, do, dlse)


def _dsilu(u):
    sg = _sigmoid(u)
    return sg * (1.0 + u * (1.0 - sg))


def _ride(body, rider, n_in, n_out, n_scratch, grid):
    if rider is None:
        return body, [], [], [], [], [], None
    ni, no = len(rider.ins), len(rider.out_shapes)

    def wrapped(*refs):
        k_in, r_in = refs[:n_in], refs[n_in:n_in + ni]
        k_out, r_out = refs[n_in + ni:n_in + ni + n_out], refs[n_in + ni + n_out:n_in + ni + n_out + no]
        k_scr = refs[n_in + ni + n_out + no:n_in + ni + n_out + no + n_scratch]
        send_sems, recv_sems = refs[-2], refs[-1]
        first = functools.reduce(jnp.logical_and, [pl.program_id(a) == 0 for a in range(len(grid))])
        last = functools.reduce(jnp.logical_and, [pl.program_id(a) == g - 1 for a, g in enumerate(grid)])

        @pl.when(first)
        def _():
            rider.start(r_in, r_out, send_sems, recv_sems)

        body(*k_in, *k_out, *k_scr)

        @pl.when(last)
        def _():
            rider.finish(r_in, r_out, send_sems, recv_sems)

    sems = [pltpu.SemaphoreType.DMA((rider.n_sems,)), pltpu.SemaphoreType.DMA((rider.n_sems,))]
    return wrapped, rider.ins, [_ANY] * ni, [_ANY] * no, rider.out_shapes, sems, ("arbitrary",) * len(grid)


CONV_STRIP = 16


def _shifted_rows(prev8, cur_r, next8, lanes, s0, n, sh, block):
    if s0 - sh < 0:
        assert s0 == 0
        xp = jnp.concatenate([prev8, cur_r[0:n, lanes]], axis=0)
        return pltpu.roll(xp, sh, 0)[SUBLANE:SUBLANE + n]
    if s0 - sh + n > block:
        assert s0 == block and n == SUBLANE
        xp = jnp.concatenate([cur_r[block - SUBLANE:block, lanes], next8], axis=0)
        return (pltpu.roll(xp, sh, 0) if sh else xp)[SUBLANE:]
    return cur_r[pl.ds(s0 - sh, n), lanes]


def conv_fwd(name, x, w, b, mode, out_dtype, tc, block=ROW_BLOCK, rider=None):
    s, c = x.shape
    ntap = w.shape[0]
    block = min(block, s)
    f = c // 2 if mode == 'glu' else c
    nh = 2 if mode == 'glu' else 1
    off = f // tc

    def body(*refs):
        first = pl.program_id(1) == 0
        o_ref = refs[-1]

        def column(cidx, carry):
            lanes = pl.ds(pl.multiple_of(cidx * LANE, LANE), LANE)
            prevs = [jnp.where(first, 0.0, refs[4 * hlf][:, lanes]) for hlf in range(nh)]
            for s0 in range(0, block, CONV_STRIP):
                us = []
                for hlf in range(nh):
                    _, cur_r, w_r, b_r = refs[4 * hlf:4 * hlf + 4]
                    acc = b_r[:, lanes]
                    for j in range(ntap):
                        xs = _shifted_rows(prevs[hlf], cur_r, None, lanes, s0, CONV_STRIP, ntap - 1 - j, block)
                        acc = acc + w_r[j:j + 1, lanes] * xs
                    us.append(acc)
                res = _silu(us[0]) * us[1] if mode == 'glu' else _silu(us[0])
                o_ref[pl.ds(s0, CONV_STRIP), lanes] = res.astype(o_ref.dtype)
            return carry

        lax.fori_loop(0, tc // LANE, column, 0)

    rb = block // SUBLANE
    ins, specs = [], []
    for hlf in range(nh):
        o = hlf * off
        ins += [x, x, w, b]
        specs += [pl.BlockSpec((SUBLANE, tc), lambda j, i, o=o: (jnp.maximum(i * rb - 1, 0), j + o)),
                  pl.BlockSpec((block, tc), lambda j, i, o=o: (i, j + o)),
                  pl.BlockSpec((ntap, tc), lambda j, i, o=o: (0, j + o)),
                  pl.BlockSpec((1, tc), lambda j, i, o=o: (0, j + o))]
    grid = (f // tc, s // block)
    body, r_ins, r_in_specs, r_out_specs, r_out_shapes, r_scratch, sem = _ride(body, rider, len(ins), 1, 0, grid)
    res = pl.pallas_call(
        body, name=name, grid=grid, in_specs=specs + r_in_specs,
        out_specs=[pl.BlockSpec((block, tc), lambda j, i: (i, j))] + r_out_specs,
        out_shape=[jax.ShapeDtypeStruct((s, f), out_dtype)] + r_out_shapes, scratch_shapes=r_scratch,
        compiler_params=_cparams(sem or ("parallel", "parallel")),
    )(*ins, *r_ins)
    return res[0] if rider is None else (res[0], rider.results(res[1:]))


def conv_bwd(name, x, w, b, dout, mode, tc, block=ROW_BLOCK, rider=None):
    s, c = x.shape
    ntap = w.shape[0]
    block = min(block, s)
    nblk = s // block
    f = c // 2 if mode == 'glu' else c
    nh = 2 if mode == 'glu' else 1
    off = f // tc
    ext = block + SUBLANE

    def body(*refs):
        i = pl.program_id(1)
        first, last = i == 0, i == nblk - 1
        dcur_r, dnext_r = refs[5 * nh], refs[5 * nh + 1]
        outs = refs[5 * nh + 2:5 * nh + 2 + 3 * nh]
        du_scr = refs[5 * nh + 2 + 3 * nh:]

        @pl.when(first)
        def _():
            for hlf in range(nh):
                outs[3 * hlf + 1][...] = jnp.zeros_like(outs[3 * hlf + 1])
                outs[3 * hlf + 2][...] = jnp.zeros_like(outs[3 * hlf + 2])

        def column(cidx, carry):
            lanes = pl.ds(pl.multiple_of(cidx * LANE, LANE), LANE)
            prevs = [jnp.where(first, 0.0, refs[5 * hlf][:, lanes]) for hlf in range(nh)]
            nexts = [jnp.where(last, 0.0, refs[5 * hlf + 2][:, lanes]) for hlf in range(nh)]
            db_acc = [jnp.zeros((CONV_STRIP, LANE), F32) for _ in range(nh)]
            dw_acc = [[jnp.zeros((CONV_STRIP, LANE), F32) for _ in range(ntap)] for _ in range(nh)]
            for s0 in range(0, ext, CONV_STRIP):
                n = min(CONV_STRIP, ext - s0)
                d_e = dcur_r[pl.ds(s0, n), lanes] if s0 < block else jnp.where(last, 0.0, dnext_r[:, lanes])
                xs, us = [], []
                for hlf in range(nh):
                    cur_r, w_r, b_r = refs[5 * hlf + 1], refs[5 * hlf + 3], refs[5 * hlf + 4]
                    sh_rows = [_shifted_rows(prevs[hlf], cur_r, nexts[hlf], lanes, s0, n, ntap - 1 - j, block)
                               for j in range(ntap)]
                    acc = b_r[:, lanes]
                    for j in range(ntap):
                        acc = acc + w_r[j:j + 1, lanes] * sh_rows[j]
                    xs.append(sh_rows)
                    us.append(acc)
                dus = [d_e * us[1] * _dsilu(us[0]), d_e * _silu(us[0])] if mode == 'glu' else [d_e * _dsilu(us[0])]
                for hlf in range(nh):
                    du_scr[hlf][pl.ds(s0, n), lanes] = dus[hlf]
                    if s0 < block:
                        db_acc[hlf] = db_acc[hlf] + dus[hlf]
                        for j in range(ntap):
                            dw_acc[hlf][j] = dw_acc[hlf][j] + dus[hlf] * xs[hlf][j]
            for hlf in range(nh):
                w_r = refs[5 * hlf + 3]
                dx_r, dw_r, db_r = outs[3 * hlf:3 * hlf + 3]
                db_r[:, lanes] += jnp.sum(db_acc[hlf], axis=0, keepdims=True)
                for j in range(ntap):
                    dw_r[j:j + 1, lanes] += jnp.sum(dw_acc[hlf][j], axis=0, keepdims=True)
                for s0 in range(0, block, CONV_STRIP):
                    dx = None
                    for j in range(ntap):
                        term = w_r[j:j + 1, lanes] * du_scr[hlf][pl.ds(s0 + ntap - 1 - j, CONV_STRIP), lanes]
                        dx = term if dx is None else dx + term
                    dx_r[pl.ds(s0, CONV_STRIP), lanes] = dx.astype(dx_r.dtype)
            return carry

        lax.fori_loop(0, tc // LANE, column, 0)

    rb = block // SUBLANE
    nrow8 = s // SUBLANE
    ins, specs = [], []
    for hlf in range(nh):
        o = hlf * off
        ins += [x, x, x, w, b]
        specs += [pl.BlockSpec((SUBLANE, tc), lambda j, i, o=o: (jnp.maximum(i * rb - 1, 0), j + o)),
                  pl.BlockSpec((block, tc), lambda j, i, o=o: (i, j + o)),
                  pl.BlockSpec((SUBLANE, tc), lambda j, i, o=o: (jnp.minimum((i + 1) * rb, nrow8 - 1), j + o)),
                  pl.BlockSpec((ntap, tc), lambda j, i, o=o: (0, j + o)),
                  pl.BlockSpec((1, tc), lambda j, i, o=o: (0, j + o))]
    ins += [dout, dout]
    specs += [pl.BlockSpec((block, tc), lambda j, i: (i, j)),
              pl.BlockSpec((SUBLANE, tc), lambda j, i: (jnp.minimum((i + 1) * rb, nrow8 - 1), j))]
    out_specs, out_shape = [], []
    for hlf in range(nh):
        out_specs += [pl.BlockSpec((block, tc), lambda j, i: (i, j)), pl.BlockSpec((ntap, tc), lambda j, i: (0, j)),
                      pl.BlockSpec((1, tc), lambda j, i: (0, j))]
        out_shape += [jax.ShapeDtypeStruct((s, f), MXU_DTYPE), jax.ShapeDtypeStruct((ntap, f), F32),
                      jax.ShapeDtypeStruct((1, f), F32)]
    grid = (f // tc, nblk)
    body, r_ins, r_in_specs, r_out_specs, r_out_shapes, r_scratch, sem = _ride(body, rider, len(ins), 3 * nh, nh, grid)
    res = pl.pallas_call(
        body, name=name, grid=grid, in_specs=specs + r_in_specs, out_specs=out_specs + r_out_specs,
        out_shape=out_shape + r_out_shapes, scratch_shapes=[pltpu.VMEM((ext, tc), F32)] * nh + r_scratch,
        compiler_params=_cparams(sem or ("parallel", "arbitrary")),
    )(*ins, *r_ins)
    rode = None if rider is None else rider.results(res[3 * nh:])
    if nh == 1:
        return [res[0]], res[1], res[2], rode
    return ([res[0], res[3]], jnp.concatenate([res[1], res[4]], axis=1), jnp.concatenate([res[2], res[5]], axis=1),
            rode)


def loss_head(y, target, block=ROW_BLOCK):
    s, d = y.shape
    block = min(block, s)

    def body(y_r, t_r, acc_r, dy_r):
        e = y_r[...] - t_r[...]
        dy_r[...] = e * (1.0 / d)

        @pl.when(pl.program_id(0) == 0)
        def _():
            acc_r[...] = jnp.zeros_like(acc_r)

        acc_r[...] += jnp.sum((e * e).reshape(block // SUBLANE, SUBLANE, d), axis=0) * (0.5 / d)

    return pl.pallas_call(
        body, name="loss_head", grid=(s // block,),
        in_specs=[pl.BlockSpec((block, d), lambda i: (i, 0))] * 2,
        out_specs=[pl.BlockSpec((SUBLANE, d), lambda i: (0, 0)), pl.BlockSpec((block, d), lambda i: (i, 0))],
        out_shape=[jax.ShapeDtypeStruct((SUBLANE, d), F32), jax.ShapeDtypeStruct((s, d), F32)],
        compiler_params=_cparams(("arbitrary",)),
    )(y, target)


def adamw(name, w, g, m, v):
    r, c = w.shape
    tr = r if r <= 512 else _tile(r, (512, 256, 128, 64, 32, 16, 8))
    if c * tr * 4 > (1 << 21):
        tr = _tile(r, (256, 128, 64, 32, 16, 8))

    def body(w_r, g_r, m_r, v_r, d_r, nm_r, nv_r):
        gg = g_r[...]
        nm = ADAM_B1 * m_r[...] + (1.0 - ADAM_B1) * gg
        nv = ADAM_B2 * v_r[...] + (1.0 - ADAM_B2) * (gg * gg)
        m_hat = nm / (1.0 - ADAM_B1 ** ADAM_STEP)
        v_hat = nv / (1.0 - ADAM_B2 ** ADAM_STEP)
        d_r[...] = -ADAM_LR * (m_hat / (jnp.sqrt(v_hat) + ADAM_EPS) + ADAM_WD * w_r[...])
        nm_r[...] = nm
        nv_r[...] = nv

    spec = pl.BlockSpec((tr, c), lambda i: (i, 0))
    return pl.pallas_call(
        body, name=name, grid=(r // tr,), in_specs=[spec] * 4, out_specs=[spec] * 3,
        out_shape=[jax.ShapeDtypeStruct((r, c), F32)] * 3, compiler_params=_cparams(("parallel",)),
    )(w, g, m, v)


MESH = pl.DeviceIdType.MESH
_ANY = pl.BlockSpec(memory_space=pl.ANY)


def _place():
    return lax.axis_index("x"), lax.axis_index("y"), lax.axis_index("c")


class Packed:
    def __init__(self, shard_shape):
        self.r, self.c = shard_shape
        self.h = self.r // 2
        self.whole = (N_CHIPS, self.r, self.c)
        self.got = (N_CHIPS, self.h, self.c)
        self.slab_half = (self.h, self.c)

    def shard_half(self, ref, core):
        return ref.at[pl.ds(core * self.h, self.h)]

    def whole_half(self, ref, chip, core):
        return ref.at[chip, pl.ds(core * self.h, self.h)]

    def place(self, whole, shard, chip):
        return lax.dynamic_update_slice(whole, shard[None], (chip, 0, 0))

    def grad_half(self, ref, core):
        return ref.at[:, core]

    def pair_slab(self, ref, chip):
        return ref.at[chip]


class SlabCols:
    def __init__(self, shard_shape):
        self.r, self.c = shard_shape
        self.h = self.r // 2
        self.whole = (self.r, N_CHIPS * self.c)
        self.got = (self.h, N_CHIPS * self.c)
        self.slab_half = (self.h, self.c)

    def _cols(self, chip):
        return pl.ds(pl.multiple_of(chip * self.c, LANE), self.c)

    def shard_half(self, ref, core):
        return ref.at[pl.ds(core * self.h, self.h)]

    def whole_half(self, ref, chip, core):
        return ref.at[pl.ds(core * self.h, self.h), self._cols(chip)]

    def place(self, whole, shard, chip):
        return lax.dynamic_update_slice_in_dim(whole, shard, chip * self.c, 1)

    def grad_half(self, ref, core):
        return ref.at[pl.ds(core * self.h, self.h)]

    def pair_slab(self, ref, chip):
        return ref.at[:, self._cols(chip)]


class GatherRider:
    def __init__(self, shards, kinds):
        self.ins, self.kinds, n = list(shards), kinds, len(shards)
        self.out_shapes = [jax.ShapeDtypeStruct(k.whole, s.dtype) for k, s in zip(kinds, shards)]
        self.n_sems = 6 * n

    def _copies(self, w_refs, out_refs, send_sems, recv_sems):
        x, y, cc = _place()
        chips = [(1 - x, y), (x, 1 - y), (1 - x, 1 - y)]

        def copy(t, k, chip, core, to, src=None):
            dst = self.kinds[t].whole_half(out_refs[t], 2 * chip[0] + chip[1], core)
            return pltpu.make_async_remote_copy(
                src_ref=dst if src is None else src, dst_ref=dst, send_sem=send_sems.at[6 * t + k],
                recv_sem=recv_sems.at[6 * t + k], device_id=to, device_id_type=MESH)

        first = [copy(t, j, (x, y), cc, (*chip, cc), src=self.kinds[t].shard_half(w_refs[t], cc))
                 for t in range(len(self.ins)) for j, chip in enumerate(chips)]
        return copy, first, chips, (x, y, cc)

    def start(self, w_refs, out_refs, send_sems, recv_sems):
        for cp in self._copies(w_refs, out_refs, send_sems, recv_sems)[1]:
            cp.start()

    def finish(self, w_refs, out_refs, send_sems, recv_sems):
        copy, first, chips, (x, y, cc) = self._copies(w_refs, out_refs, send_sems, recv_sems)
        passed = []
        for t in range(len(self.ins)):
            for j, chip in enumerate(chips):
                copy(t, j, chip, cc, (x, y, cc)).wait_recv()
                passed.append(copy(t, 3 + j, chip, cc, (x, y, 1 - cc)))
                passed[-1].start()
        for t in range(len(self.ins)):
            for j, chip in enumerate(chips):
                copy(t, 3 + j, chip, 1 - cc, (x, y, cc)).wait_recv()
        for cp in first + passed:
            cp.wait_send()

    def results(self, outs):
        chip = 2 * lax.axis_index("x") + lax.axis_index("y")
        return [k.place(o, s, chip) for k, o, s in zip(self.kinds, outs, self.ins)]


class ExchangeRider:
    def __init__(self, pairs, kinds):
        self.ins, self.kinds = list(pairs), kinds
        self.out_shapes = [jax.ShapeDtypeStruct((N_CHIPS,) + k.slab_half, p.dtype) for k, p in zip(kinds, pairs)]
        self.n_sems = 3 * len(pairs)

    def start(self, p_refs, out_refs, send_sems, recv_sems):
        x, y, cc = _place()
        for t in range(len(self.ins)):
            for j, chip in enumerate([(1 - x, y), (x, 1 - y), (1 - x, 1 - y)]):
                pltpu.make_async_remote_copy(
                    src_ref=self.kinds[t].pair_slab(p_refs[t], 2 * chip[0] + chip[1]), dst_ref=out_refs[t].at[2 * x + y],
                    send_sem=send_sems.at[3 * t + j], recv_sem=recv_sems.at[3 * t + j], device_id=(*chip, cc),
                    device_id_type=MESH).start()

    def finish(self, p_refs, out_refs, send_sems, recv_sems):
        x, y, cc = _place()
        me = 2 * x + y
        for t in range(len(self.ins)):
            for j, chip in enumerate([(1 - x, y), (x, 1 - y), (1 - x, 1 - y)]):
                them = 2 * chip[0] + chip[1]
                pltpu.make_async_remote_copy(
                    src_ref=self.kinds[t].pair_slab(p_refs[t], them), dst_ref=out_refs[t].at[them],
                    send_sem=send_sems.at[3 * t + j], recv_sem=recv_sems.at[3 * t + j], device_id=(x, y, cc),
                    device_id_type=MESH).wait()

    def results(self, outs):
        return list(outs)


def run_rider(rider, name):
    n, no = len(rider.ins), len(rider.out_shapes)

    def body(*refs):
        parts = (refs[:n], refs[n:n + no], refs[n + no], refs[n + no + 1])
        rider.start(*parts)
        rider.finish(*parts)

    outs = pl.pallas_call(
        body, name=name, in_specs=[_ANY] * n, out_specs=[_ANY] * no, out_shape=rider.out_shapes,
        scratch_shapes=[pltpu.SemaphoreType.DMA((rider.n_sems,)), pltpu.SemaphoreType.DMA((rider.n_sems,))],
    )(*rider.ins)
    return rider.results(outs)


def allgather_devices(buf):
    r, c = buf.shape

    def body(b_ref, out_ref, send_sems, recv_sems, local_sem):
        x, y, cc = _place()
        me = 4 * x + 2 * y + cc
        mine = pltpu.make_async_copy(b_ref, out_ref.at[me], local_sem)
        mine.start()
        copies = []
        for k in range(1, N_DEV):
            px, py, pc = x ^ (k >> 2), y ^ ((k >> 1) & 1), cc ^ (k & 1)
            cp = pltpu.make_async_remote_copy(src_ref=b_ref, dst_ref=out_ref.at[me], send_sem=send_sems.at[k - 1],
                                              recv_sem=recv_sems.at[k - 1], device_id=(px, py, pc), device_id_type=MESH)
            cp.start()
            copies.append((cp, 4 * px + 2 * py + pc))
        for k, (cp, peer) in enumerate(copies):
            pltpu.make_async_remote_copy(src_ref=b_ref, dst_ref=out_ref.at[peer], send_sem=send_sems.at[k],
                                         recv_sem=recv_sems.at[k], device_id=(x, y, cc), device_id_type=MESH).wait_recv()
        for cp, _ in copies:
            cp.wait_send()
        mine.wait()

    return pl.pallas_call(
        body, name="allgather_devices", in_specs=[_ANY], out_specs=_ANY,
        out_shape=jax.ShapeDtypeStruct((N_DEV, r, c), buf.dtype),
        scratch_shapes=[pltpu.SemaphoreType.DMA((N_DEV - 1,)), pltpu.SemaphoreType.DMA((N_DEV - 1,)),
                        pltpu.SemaphoreType.DMA],
    )(buf)


def swap_halves_sibling(gs, kinds, name):
    n = len(gs)

    def body(*refs):
        g_refs, out_refs, send_sems, recv_sems = refs[:n], refs[n:2 * n], refs[2 * n], refs[2 * n + 1]
        x, y, cc = _place()
        cps = []
        for t in range(n):
            cps.append(pltpu.make_async_remote_copy(
                src_ref=kinds[t].grad_half(g_refs[t], 1 - cc), dst_ref=out_refs[t], send_sem=send_sems.at[t],
                recv_sem=recv_sems.at[t], device_id=(x, y, 1 - cc), device_id_type=MESH))
            cps[-1].start()
        for cp in cps:
            cp.wait()

    return pl.pallas_call(
        body, name=name, in_specs=[_ANY] * n, out_specs=[_ANY] * n,
        out_shape=[jax.ShapeDtypeStruct(k.got, g.dtype) for k, g in zip(kinds, gs)],
        scratch_shapes=[pltpu.SemaphoreType.DMA((n,)), pltpu.SemaphoreType.DMA((n,))],
    )(*gs)


def _row_tile(n, limit=512):
    return max(t for t in range(16, limit + 1, 16) if n % t == 0)


def sum_chips(got, own, kind, chip, name):
    def body(chip_ref, got_r, own_r, out_r):
        mine = own_r[...].astype(F32)
        acc = None
        for k in range(N_CHIPS):
            term = jnp.where(chip_ref[0] == k, mine, got_r[k].astype(F32))
            acc = term if acc is None else acc + term
        out_r[...] = acc

    if isinstance(kind, Packed):
        r, c = kind.slab_half
        tr = _row_tile(r)
        grid = (r // tr,)
        specs = [pl.BlockSpec((N_CHIPS, tr, c), lambda i, chip_ref: (0, i, 0)),
                 pl.BlockSpec((None, tr, c), lambda i, chip_ref: (chip_ref[0], i, 0))]
        out_spec = pl.BlockSpec((tr, c), lambda i, chip_ref: (i, 0))
    else:
        r, c = kind.slab_half
        tr = _row_tile(r, 256)
        grid = (r // tr,)
        specs = [pl.BlockSpec((N_CHIPS, tr, c), lambda i, chip_ref: (0, i, 0)),
                 pl.BlockSpec((tr, c), lambda i, chip_ref: (i, chip_ref[0]))]
        out_spec = pl.BlockSpec((tr, c), lambda i, chip_ref: (i, 0))
    return pl.pallas_call(
        body, name=name,
        grid_spec=pltpu.PrefetchScalarGridSpec(num_scalar_prefetch=1, grid=grid, in_specs=specs, out_specs=out_spec),
        out_shape=jax.ShapeDtypeStruct(kind.slab_half, F32),
        compiler_params=_cparams(("parallel",) * len(grid)),
    )(chip, got, own)


def join_halves_sibling(halves):
    n = len(halves)

    def body(*refs):
        h_refs, out_refs, send_sems, recv_sems = refs[:n], refs[n:2 * n], refs[2 * n], refs[2 * n + 1]
        x, y, cc = _place()
        cps = []
        for t in range(n):
            cps.append(pltpu.make_async_remote_copy(
                src_ref=h_refs[t], dst_ref=out_refs[t].at[cc], send_sem=send_sems.at[t], recv_sem=recv_sems.at[t],
                device_id=(x, y, 1 - cc), device_id_type=MESH))
            cps[-1].start()
        for t in range(n):
            pltpu.make_async_remote_copy(
                src_ref=h_refs[t], dst_ref=out_refs[t].at[1 - cc], send_sem=send_sems.at[t], recv_sem=recv_sems.at[t],
                device_id=(x, y, cc), device_id_type=MESH).wait_recv()
        for cp in cps:
            cp.wait_send()

    outs = pl.pallas_call(
        body, name="join_halves_sibling", in_specs=[_ANY] * n, out_specs=[_ANY] * n,
        out_shape=[jax.ShapeDtypeStruct((2,) + h.shape, h.dtype) for h in halves],
        scratch_shapes=[pltpu.SemaphoreType.DMA((n,)), pltpu.SemaphoreType.DMA((n,))],
    )(*halves)
    core = lax.axis_index("c")
    return [lax.dynamic_update_slice_in_dim(o, h[None], core, 0) for o, h in zip(outs, halves)]


def add_own_half(g, got, kind, core, out_dtype, name):
    def body(c_ref, g_r, o_r, out_r):
        out_r[...] = (g_r[...] + o_r[...]).astype(out_r.dtype)

    if isinstance(kind, Packed):
        r, c = kind.slab_half
        tr = _row_tile(r)
        grid = (N_CHIPS, r // tr)
        specs = [pl.BlockSpec((None, None, tr, c), lambda i, j, c_ref: (i, c_ref[0], j, 0)),
                 pl.BlockSpec((None, tr, c), lambda i, j, c_ref: (i, j, 0))]
        out_spec = pl.BlockSpec((None, tr, c), lambda i, j, c_ref: (i, j, 0))
    else:
        h, c4 = kind.got
        tr = _row_tile(h, 128)
        grid = (1, h // tr)
        specs = [pl.BlockSpec((tr, c4), lambda i, j, c_ref: (c_ref[0] * (h // tr) + j, 0)),
                 pl.BlockSpec((tr, c4), lambda i, j, c_ref: (j, 0))]
        out_spec = pl.BlockSpec((tr, c4), lambda i, j, c_ref: (j, 0))
    return pl.pallas_call(
        body, name=name,
        grid_spec=pltpu.PrefetchScalarGridSpec(num_scalar_prefetch=1, grid=grid, in_specs=specs, out_specs=out_spec),
        out_shape=jax.ShapeDtypeStruct(kind.got, out_dtype),
        compiler_params=_cparams(("parallel", "parallel")),
    )(core, g, got)


def sum_slabs(p, name):
    n, r, c = p.shape
    tr = _tile(r, [t for t in (512, 256, 128, 64, 32, 16) if n * t * c * p.dtype.itemsize <= (1 << 23)])

    def body(p_r, out_r):
        acc = p_r[0].astype(F32)
        for k in range(1, n):
            acc = acc + p_r[k].astype(F32)
        out_r[...] = acc

    return pl.pallas_call(
        body, name=name, grid=(r // tr,), in_specs=[pl.BlockSpec((n, tr, c), lambda i: (0, i, 0))],
        out_specs=pl.BlockSpec((tr, c), lambda i: (i, 0)), out_shape=jax.ShapeDtypeStruct((r, c), F32),
        compiler_params=_cparams(("parallel",)),
    )(p)


def _lay(arr, axis, pieces, total, reps=()):
    items = [(d, n, lax.slice_in_dim(arr, s0, s0 + n, axis=axis)) for s0, n, d in pieces]
    items += [(d, n, jnp.repeat(lax.slice_in_dim(arr, s0, s0 + 1, axis=axis), n, axis=axis)) for s0, d, n in reps]
    items.sort(key=lambda t: t[0])
    parts, pos = [], 0

    def zeros(n):
        sh = list(arr.shape)
        sh[axis] = n
        return jnp.zeros(sh, arr.dtype)

    for d, n, v in items:
        if d > pos:
            parts.append(zeros(d - pos))
        parts.append(v)
        pos = d + n
    if total > pos:
        parts.append(zeros(total - pos))
    return jnp.concatenate(parts, axis=axis) if len(parts) > 1 else parts[0]


def _unlay_parts(g, axis, pieces, reps=()):
    out = [(s0, lax.slice_in_dim(g, d, d + n, axis=axis)) for s0, n, d in pieces]
    out += [(s0, jnp.sum(lax.slice_in_dim(g, d, d + n, axis=axis), axis=axis, keepdims=True)) for s0, d, n in reps]
    return out


def _join(parts, axis):
    parts = sorted(parts, key=lambda t: t[0])
    return jnp.concatenate([p for _, p in parts], axis=axis)


def _heads(src0, n_heads, width, padded, dst0=0):
    return [(src0 + h * width, width, dst0 + h * padded) for h in range(n_heads)]


_XQ = lambda src0: _heads(src0, XA_HEADS, XA_HD, LANE)
_XA_W = XA_HEADS * LANE

LAYOUT = {
    'a': dict(
        segs=dict(q=(_heads(0, 4, 96, LANE), 512, ()), k=(_heads(384, 4, 96, LANE), 512, ()),
                  v=(_heads(768, 4, 192, 256), 1024, ()), glr=([(1536, 16, 0)], LANE, ()),
                  og=(_heads(1552, 4, 192, 256), 1024, ()), xq=(_XQ(2320), _XA_W, ())),
        tok=(_heads(0, 4, 192, 256), 1024), xa=(_XQ(768), _XA_W)),
    'b': dict(
        segs=dict(q=([(0, 1536, 0)], 1536, ()), k=([(1536, 1536, 0)], 1536, ()), v=([(3072, 1536, 0)], 1536, ()),
                  xq=(_XQ(4608), _XA_W, ())),
        tok=([(0, 512, 0)], 512), xa=(_XQ(512), _XA_W)),
    'c': dict(
        segs=dict(z=(_heads(0, 12, 64, LANE), 1536, ()),
                  xbc=(_heads(768, 12, 64, LANE) + [(1536, 256, 1536), (1792, 256, 1792)], 2048, ()),
                  dt=([(2048, 12, 0)], LANE, ()),
                  xq=(_XQ(2060), _XA_W, ())),
        tok=(_heads(0, 12, 64, LANE), 1536), xa=(_XQ(768), _XA_W)),
    'd': dict(
        segs=dict(q=([(0, 768, 0)], 768, ()), f=([(768, 768, 0)], 768, ()), i=([(1536, 768, 0)], 768, ()),
                  og=([(2304, 768, 0)], 768, ()), xq=(_XQ(3072), _XA_W, ())),
        tok=([(0, 768, 0)], 768), xa=(_XQ(768), _XA_W)),
}
KINDS = 'abcd'
_XS_PIECES = _heads(0, 12, 64, LANE)
_XBC_PIECES = _XS_PIECES + [(768, 256, 1536), (1024, 256, 1792)]
_HEAD_REPS = tuple((h, h * LANE, LANE) for h in range(12))


def _row(v):
    return v.reshape(1, -1)


LAYER_WEIGHTS = [
    {'w_in': (f'{k}_w_in', None), 'w_out': (f'{k}_w_out', None), 'w_kv': ('xa_w_kv', i), 'w_up': ('ffn_w_up', i),
     'w_down': ('ffn_w_down', i), **({'w_gate2': ('a_w_gate2', None)} if k == 'a' else {})}
    for i, k in enumerate('abcd')]


class LocalLayers:
    def __init__(self, W):
        self.W, self.g = W, {}

    def weights(self, i):
        return {key: (self.W[n] if l is None else self.W[n][l]).astype(MXU_DTYPE)
                for key, (n, l) in LAYER_WEIGHTS[i].items()}

    def fwd_rider(self, i):
        return None

    def bwd_rider(self, i):
        return None

    def grads(self, i, g):
        self.g[i] = g

    def whole_grads(self):
        out = {}
        for i in range(4):
            for key, (n, l) in LAYER_WEIGHTS[i].items():
                if l is None:
                    out[n] = self.g[i][key]
        for n in ('xa_w_kv', 'ffn_w_up', 'ffn_w_down'):
            key = [k for k, (m, _) in LAYER_WEIGHTS[0].items() if m == n][0]
            out[n] = jnp.stack([self.g[i][key] for i in range(4)])
        return out


class ShardedLayers:
    def __init__(self, w, core_id):
        self.core_id = core_id
        self.names, self.axes, self.shards, self.packed, self.kinds = [], [], [], [], []
        for lw in LAYER_WEIGHTS:
            keys = [k for k in lw if k not in ('w_up', 'w_down')]
            sh = {k: (w[lw[k][0]] if lw[k][1] is None else w[lw[k][0]][lw[k][1]]).astype(MXU_DTYPE) for k in lw}
            ax = {k: SHARD_AXIS[lw[k][0]] - (lw[k][1] is not None) for k in lw}
            pk = _pack([sh[k] for k in keys], MXU_DTYPE, 256)
            self.names.append(keys)
            self.axes.append(ax)
            self.shards.append(sh)
            self.packed.append(pk)
            self.kinds.append([Packed(pk.shape), SlabCols(sh['w_up'].shape), Packed(sh['w_down'].shape)])
        self.whole = {}
        self.pending = None
        self.recvd = {}

    def _operands(self, i):
        return [self.packed[i], self.shards[i]['w_up'], self.shards[i]['w_down']]

    def _gathered(self, i, res):
        per_chip = [_unpack(res[0][j], [self.shards[i][k].shape for k in self.names[i]]) for j in range(N_CHIPS)]
        out = {k: _merge_chips(jnp.stack([per_chip[j][n] for j in range(N_CHIPS)]), self.axes[i][k])
               for n, k in enumerate(self.names[i])}
        out['w_up'], out['w_down'] = res[1], res[2].reshape(-1, res[2].shape[-1])
        self.whole[i] = out

    def first_gather(self):
        self._gathered(0, run_rider(GatherRider(self._operands(0), self.kinds[0]), "allgather_chips"))

    def weights(self, i):
        return self.whole[i]

    def fwd_rider(self, i):
        return GatherRider(self._operands(i + 1), self.kinds[i + 1]) if i + 1 < 4 else None

    def fwd_rode(self, i, res):
        self._gathered(i + 1, res)

    def bwd_rider(self, i):
        return ExchangeRider(self.pending[1], self.kinds[self.pending[0]]) if self.pending is not None else None

    def bwd_rode(self, i, res):
        self.recvd[self.pending[0]] = (res, self.pending[1])
        self.pending = None

    def grads(self, i, g):
        kinds = self.kinds[i]
        gb = jnp.stack([_pack([_split_chips(g[k], self.axes[i][k])[j] for k in self.names[i]], F32, 256)
                        for j in range(N_CHIPS)])
        gs = [gb.reshape(N_CHIPS, 2, kinds[0].h, kinds[0].c), g['w_up'],
              g['w_down'].reshape(N_CHIPS, 2, kinds[2].h, kinds[2].c)]
        gots = swap_halves_sibling(gs, kinds, f"swap_halves_{i}")
        self.pending = (i, [add_own_half(a, o, k, self.core_id, GRAD_WIRE_DTYPE, f"add_own_half_{i}_{t}")
                            for t, (a, o, k) in enumerate(zip(gs, gots, kinds))])

    def finish(self, chip_id):
        last, pairs = self.pending
        self.recvd[last] = (run_rider(ExchangeRider(pairs, self.kinds[last]), "exchange_chips"), pairs)
        halves = []
        for i in range(4):
            got, pairs = self.recvd[i]
            halves += [sum_chips(r, p, k, chip_id, f"sum_chips_{i}_{t}")
                       for t, (r, p, k) in enumerate(zip(got, pairs, self.kinds[i]))]
        joined = join_halves_sibling(halves)
        out, stacked = {}, {'xa_w_kv': [], 'ffn_w_up': [], 'ffn_w_down': []}
        for i, lw in enumerate(LAYER_WEIGHTS):
            red, up, down = joined[3 * i:3 * i + 3]
            parts = _unpack(red.reshape(-1, PACK_COLS), [self.shards[i][k].shape for k in self.names[i]])
            parts = dict(zip(self.names[i], parts), w_up=up.reshape(self.shards[i]['w_up'].shape),
                         w_down=down.reshape(self.shards[i]['w_down'].shape))
            for k, (n, l) in lw.items():
                if l is None:
                    out[n] = parts[k]
                else:
                    stacked[n].append(parts[k])
        out.update({n: jnp.stack(v) for n, v in stacked.items()})
        return out


def local_step(x, mem, positions, target, W, layers=None):
    s = x.shape[0]
    grads = {}
    scan_block = CHUNK * SCAN_CHUNKS
    ffn = layers or LocalLayers(W)

    inv_freq = ROPE_THETA ** (-jnp.arange(DIL_HD // 2, dtype=F32) / (DIL_HD // 2))
    ang = positions.astype(F32)[:, None] * inv_freq
    cosf = jnp.concatenate([jnp.cos(ang), jnp.cos(ang)], axis=-1)
    sinf = jnp.concatenate([-jnp.sin(ang), jnp.sin(ang)], axis=-1)

    mem_g = _row(W['mem_norm'])
    (mem_n,) = tmap("mem_norm", _norm_stage, [mem], [mem_g], [(D_MODEL, MXU_DTYPE)])
    kv_lay = _heads(0, 4, 64, LANE) + _heads(256, 4, 64, LANE, dst0=_XA_W)

    saved = []
    for i in range(4):
        kind = KINDS[i]
        lay = LAYOUT[kind]
        sv = dict(x0=x)
        wl = ffn.weights(i)
        w_in, w_out = wl['w_in'], wl['w_out']
        sv['w_seg'] = {n: _lay(w_in, 1, p, t, r).astype(MXU_DTYPE) for n, (p, t, r) in lay['segs'].items()}
        sv['wo_tok'] = _lay(w_out, 0, *lay['tok']).astype(MXU_DTYPE)
        sv['wo_xa'] = _lay(w_out, 0, *lay['xa']).astype(MXU_DTYPE)
        sv['w_kv'] = _lay(wl['w_kv'], 1, kv_lay, 2 * _XA_W).astype(MXU_DTYPE)
        sv['g1'] = _row(W['mix_norm'][i])
        (h,) = tmap(f"mix_norm_{i}", _norm_stage, [x], [sv['g1']], [(D_MODEL, MXU_DTYPE)])
        sv['h'] = h
        seg = {n: matmul(h, w) for n, w in sv['w_seg'].items()}
        sv['seg'] = seg

        if kind == 'a':
            sv['w2'] = _lay(_lay(wl['w_gate2'], 1, _heads(0, 4, 96, LANE), 512), 0, [(0, 16, 0)], LANE)
            sv['bg'] = _row(_lay(W['a_b_gate'], 0, _heads(0, 4, 96, LANE), 512))
            sv['on'] = _row(_lay(W['a_o_norm'], 0, [(0, 192, 0)], 256))
            (la,) = tmap("gla_pre", _gla_pre, [seg['glr']], [sv['w2'], sv['bg']], [(512, F32)])
            sv['la'] = la
            sv['scan_fn'] = _gla_step(GLA_HEADS, 2 * LANE, GLA_DK ** -0.5)
            sv['scan_rows'] = [seg['q'], seg['k'], seg['v'], la]
            (o,), sv['states'] = rscan("gla_scan", sv['scan_fn'], [(LANE, 2 * LANE)] * GLA_HEADS, sv['scan_rows'], [],
                                       [(1024, F32)], scan_block)
            sv['o'] = o
            (tok,) = tmap("gla_post", _gla_post, [o, seg['og']], [sv['on']], [(1024, MXU_DTYPE)])
        elif kind == 'b':
            sv['qg'], sv['kg'] = _row(W['b_q_norm']), _row(W['b_k_norm'])
            os_, ls_ = [], []
            qkn = tmap("dil_pre", _dil_pre, [seg['q'], seg['k'], cosf, sinf], [sv['qg'], sv['kg']], [(512, F32)] * 6)
            sv['qn'], sv['kn'] = qkn[:3], qkn[3:]
            for g, (window, r) in enumerate(DIL_GROUPS):
                assert window // r == DIL_BLOCK and (s // r) % DIL_BLOCK == 0
                o, lse = dil_attn(f"dil_attn_{g}", sv['qn'][g], sv['kn'][g], seg['v'], r, g)
                os_.append(o)
                ls_.append(lse)
            sv['os'], sv['ls'] = os_, ls_
            (tok,) = tmap("dil_merge", _dil_merge, os_ + ls_, [], [(512, MXU_DTYPE)])
        elif kind == 'c':
            sv['cw'] = _lay(W['c_conv_w'], 1, _XBC_PIECES, 2048)
            sv['cb'] = _row(_lay(W['c_conv_b'], 0, _XBC_PIECES, 2048))
            sv['dtb'] = _row(_lay(W['c_dt_bias'], 0, [], 1536, _HEAD_REPS))
            sv['alog'] = _row(_lay(W['c_a_log'], 0, [], 1536, _HEAD_REPS))
            sv['dsk'] = _row(_lay(W['c_d'], 0, [], 1536, _HEAD_REPS))
            sv['cn'] = _row(_lay(W['c_norm'], 0, _XS_PIECES, 1536))
            xact = conv_fwd("ssm_conv", seg['xbc'], sv['cw'], sv['cb'], 'silu', F32, 512)
            sv['xact'] = xact
            sv['scan_rows'] = [xact, seg['dt']]
            sv['scan_params'] = [sv['dtb'], sv['alog'], sv['dsk']]
            (yv,), sv['states'] = rscan("ssd_scan", _ssd_step, [(LANE, LANE)] * SSM_HEADS, sv['scan_rows'],
                                        sv['scan_params'], [(1536, F32)], scan_block)
            sv['y'] = yv
            (tok,) = tmap("ssd_post", _mamba_post, [yv, seg['z']], [sv['cn']], [(1536, MXU_DTYPE)])
        else:
            sv['lbnd'] = W['d_lower_bounds']
            sv['on'] = _row(W['d_o_norm'])
            qq, kk, la = tmap("hgrn_pre", _hgrn_pre, [seg['q'], seg['f']], [sv['lbnd']], [(768, F32)] * 3)
            sv['scan_fn'] = _gla_step(HGRN_HEADS, LANE, 1.0)
            sv['scan_rows'] = [qq, kk, seg['i'], la]
            (o,), sv['states'] = rscan("hgrn_scan", sv['scan_fn'], [(LANE, LANE)] * HGRN_HEADS, sv['scan_rows'], [],
                                       [(768, F32)], scan_block)
            sv['o'] = o
            (tok,) = tmap("hgrn_post", _hgrn_post, [o, seg['og']], [sv['on']], [(768, MXU_DTYPE)])
        sv['tok'] = tok

        kv = matmul(mem_n, sv['w_kv'])
        sv['kv'] = kv
        sv['xqg'] = _row(_lay(W['xa_q_norm'][i], 0, [(0, 64, 0)], LANE))
        sv['xkg'] = _row(_lay(W['xa_k_norm'][i], 0, [(0, 64, 0)], LANE))
        (xa,) = tmap(f"xattn_{i}", _xattn, [seg['xq']], [kv, sv['xqg'], sv['xkg']], [(_XA_W, MXU_DTYPE)])
        sv['xa'] = xa
        x = matmul(tok, sv['wo_tok'], add=x)
        x = matmul(xa, sv['wo_xa'], add=x)
        sv['x1'] = x

        sv['g2'] = _row(W['ffn_norm'][i])
        sv['fcw'] = W['ffn_conv_w'][i]
        sv['fcb'] = _row(W['ffn_conv_b'][i])
        (h2,) = tmap(f"ffn_norm_{i}", _norm_stage, [x], [sv['g2']], [(D_MODEL, MXU_DTYPE)])
        sv['h2'] = h2
        w_up, w_down = wl['w_up'], wl['w_down']
        sv['w_up'], sv['w_down'] = w_up, w_down
        u0 = matmul(h2, w_up)
        sv['u0'] = u0
        rider = ffn.fwd_rider(i)
        act = conv_fwd("ffn_conv", u0, sv['fcw'], sv['fcb'], 'glu', MXU_DTYPE, 1408, rider=rider)
        if rider is not None:
            act, rode = act
            ffn.fwd_rode(i, rode)
        sv['act'] = act
        x = matmul(act, w_down, add=x)
        saved.append(sv)

    loss_acc, dx = loss_head(x, target)

    g_stack = {n: [None] * 4 for n in ('mix_norm', 'xa_q_norm', 'xa_k_norm', 'ffn_norm', 'ffn_conv_w', 'ffn_conv_b')}
    d_memn = None
    for i in reversed(range(4)):
        kind = KINDS[i]
        lay = LAYOUT[kind]
        sv = saved[i]
        seg = sv['seg']
        w_up, w_down = sv['w_up'], sv['w_down']
        gl = {}
        dact = matmul(dx, w_down, tb=True)
        gl['w_down'] = matmul(sv['act'], dx, ta=True)
        rider = ffn.bwd_rider(i)
        (du_g, du_v), dcw, dcb, rode = conv_bwd("ffn_conv_bwd", sv['u0'], sv['fcw'], sv['fcb'], dact, 'glu', 1408,
                                                rider=rider)
        if rider is not None:
            ffn.bwd_rode(i, rode)
        g_stack['ffn_conv_w'][i], g_stack['ffn_conv_b'][i] = dcw, dcb[0]
        dh2 = matmul(du_v, w_up, tb=True, b_koff=D_FF, add=matmul(du_g, w_up, tb=True))
        g_up = jnp.zeros((1,) + w_up.shape, F32)
        g_up = matmul(sv['h2'], du_g, ta=True, into=(g_up, 0, 0))
        g_up = matmul(sv['h2'], du_v, ta=True, into=(g_up, 0, D_FF))
        gl['w_up'] = g_up[0]
        (dx,), (dg2,) = tmap_bwd(f"ffn_norm_bwd_{i}", _norm_stage, [sv['x1']], [sv['g2']], [dh2], [True], {0: dx})
        g_stack['ffn_norm'][i] = dg2[0]
        dtok = matmul(dx, sv['wo_tok'], tb=True)
        dxa = matmul(dx, sv['wo_xa'], tb=True)
        g_wo = _unlay_parts(matmul(sv['tok'], dx, ta=True), 0, lay['tok'][0]) \
            + _unlay_parts(matmul(sv['xa'], dx, ta=True), 0, lay['xa'][0])
        gl['w_out'] = _join(g_wo, 0)
        (dxq,), (dkv, dqg, dkg) = tmap_bwd(f"xattn_bwd_{i}", _xattn, [seg['xq']], [sv['kv'], sv['xqg'], sv['xkg']],
                                           [dxa], [True])
        g_stack['xa_q_norm'][i], g_stack['xa_k_norm'][i] = dqg[0, :XA_HD], dkg[0, :XA_HD]
        gl['w_kv'] = _join(_unlay_parts(matmul(mem_n, dkv, ta=True), 1, kv_lay), 1)
        d_memn = matmul(dkv, sv['w_kv'], tb=True, add=d_memn)
        dseg = dict(xq=dxq)
        if kind == 'a':
            (do, dog), (don,) = tmap_bwd("gla_post_bwd", _gla_post, [sv['o'], seg['og']], [sv['on']], [dtok],
                                         [True, True], grad_dtype=F32)
            grads['a_o_norm'] = don[0, :GLA_DV]
            (dq, dk, dv, dla), _ = rscan_bwd("gla_scan_bwd", sv['scan_fn'], sv['states'], sv['scan_rows'], [], [do],
                                             scan_block, grad_dtype=F32)
            (dglr,), (dw2, dbg) = tmap_bwd("gla_pre_bwd", _gla_pre, [seg['glr']], [sv['w2'], sv['bg']], [dla], [True])
            gl['w_gate2'] = _join(_unlay_parts(dw2[:GLA_RANK], 1, _heads(0, 4, 96, LANE)), 1)
            grads['a_b_gate'] = _join(_unlay_parts(dbg[0], 0, _heads(0, 4, 96, LANE)), 0)
            dseg.update(q=dq, k=dk, v=dv, glr=dglr, og=dog)
        elif kind == 'b':
            res, _ = tmap_bwd("dil_merge_bwd", _dil_merge, sv['os'] + sv['ls'], [], [dtok], [True] * 6, grad_dtype=F32)
            dqn, dkn, dvs = [], [], []
            for g, (_, r) in enumerate(DIL_GROUPS):
                a_, b_, c_ = dil_attn_bwd(f"dil_attn_bwd_{g}", sv['qn'][g], sv['kn'][g], seg['v'], res[g], res[3 + g],
                                          r, g)
                dqn.append(a_)
                dkn.append(b_)
                dvs.append(c_)
            (dq, dk), (dqg, dkg) = tmap_bwd("dil_pre_bwd", _dil_pre, [seg['q'], seg['k'], cosf, sinf],
                                            [sv['qg'], sv['kg']], dqn + dkn, [True, True, False, False])
            dseg.update(q=dq, k=dk, v=jnp.concatenate(dvs, axis=1))
            grads['b_q_norm'], grads['b_k_norm'] = dqg[0], dkg[0]
        elif kind == 'c':
            (dy, dz), (dcn,) = tmap_bwd("ssd_post_bwd", _mamba_post, [sv['y'], seg['z']], [sv['cn']], [dtok],
                                        [True, True], grad_dtype=F32)
            grads['c_norm'] = _join(_unlay_parts(dcn[0], 0, _XS_PIECES), 0)
            (dxact, ddt), (ddtb, dalog, ddsk) = rscan_bwd("ssd_scan_bwd", _ssd_step, sv['states'], sv['scan_rows'],
                                                          sv['scan_params'], [dy], scan_block, grad_dtype=F32)
            for nm, gv in (('c_dt_bias', ddtb), ('c_a_log', dalog), ('c_d', ddsk)):
                grads[nm] = _join(_unlay_parts(gv[0], 0, [], _HEAD_REPS), 0)
            (dxbc,), dcw, dcb, _ = conv_bwd("ssm_conv_bwd", seg['xbc'], sv['cw'], sv['cb'], dxact, 'silu', 512)
            grads['c_conv_w'] = _join(_unlay_parts(dcw, 1, _XBC_PIECES), 1)
            grads['c_conv_b'] = _join(_unlay_parts(dcb[0], 0, _XBC_PIECES), 0)
            dseg.update(z=dz, xbc=dxbc, dt=ddt)
        else:
            (do, dog), (don,) = tmap_bwd("hgrn_post_bwd", _hgrn_post, [sv['o'], seg['og']], [sv['on']], [dtok],
                                         [True, True], grad_dtype=F32)
            grads['d_o_norm'] = don[0]
            (dqq, dkk, di, dla), _ = rscan_bwd("hgrn_scan_bwd", sv['scan_fn'], sv['states'], sv['scan_rows'], [], [do],
                                               scan_block, grad_dtype=F32)
            (dq, df), (dlb,) = tmap_bwd("hgrn_pre_bwd", _hgrn_pre, [seg['q'], seg['f']], [sv['lbnd']], [dqq, dkk, dla],
                                        [True, True])
            grads['d_lower_bounds'] = dlb
            dseg.update(q=dq, f=df, i=di, og=dog)
        names = list(lay['segs'])
        total = sum(lay['segs'][n][1] for n in names)
        pad = (-total) % 512
        zeros = [jnp.zeros((s, pad), MXU_DTYPE)] if pad else []
        dcat = jnp.concatenate([dseg[n].astype(MXU_DTYPE) for n in names] + zeros, axis=1)
        wcat = jnp.concatenate([sv['w_seg'][n] for n in names] + [z[:D_MODEL] for z in zeros], axis=1)
        dh = matmul(dcat, wcat, tb=True)
        gcat = matmul(sv['h'], dcat, ta=True)
        g_in, off = [], 0
        for n in names:
            p, t, rp = lay['segs'][n]
            g_in += _unlay_parts(gcat[:, off:off + t], 1, p, rp)
            off += t
        gl['w_in'] = _join(g_in, 1)
        ffn.grads(i, gl)
        (dx,), (dg1,) = tmap_bwd(f"mix_norm_bwd_{i}", _norm_stage, [sv['x0']], [sv['g1']], [dh], [True], {0: dx})
        g_stack['mix_norm'][i] = dg1[0]

    _, (dmg,) = tmap_bwd("mem_norm_bwd", _norm_stage, [mem], [mem_g], [d_memn], [False])
    grads['mem_norm'] = dmg[0]
    for n, parts in g_stack.items():
        grads[n] = jnp.stack(parts)
    if isinstance(ffn, LocalLayers):
        grads.update(ffn.whole_grads())
    return loss_acc, dx, grads


def _pack(arrs, dtype, row_multiple=PACK_ROWS):
    parts, rows = [], 0
    for a in arrs:
        f = a.reshape(-1).astype(dtype)
        unit = PACK_ROWS * PACK_COLS
        pad = (-f.shape[0]) % unit
        if pad:
            f = jnp.concatenate([f, jnp.zeros((pad,), dtype)])
        parts.append(f.reshape(-1, PACK_COLS))
        rows += parts[-1].shape[0]
    if rows % row_multiple:
        parts.append(jnp.zeros((row_multiple - rows % row_multiple, PACK_COLS), dtype))
    return jnp.concatenate(parts, axis=0)


def _unpack(buf, shapes):
    out, row = [], 0
    for sh in shapes:
        n = int(np.prod(sh))
        rows = -(-n // (PACK_ROWS * PACK_COLS)) * PACK_ROWS
        out.append(buf[row:row + rows].reshape(-1)[:n].reshape(sh))
        row += rows
    return out


def _pack_rows(arrs):
    parts = []
    for a in arrs:
        f = a.reshape(-1).astype(F32)
        parts.append(jnp.pad(f, (0, (-f.shape[0]) % PACK_COLS)))
    flat = jnp.concatenate(parts)
    rows = flat.shape[0] // PACK_COLS
    return jnp.pad(flat, (0, (-rows % 16) * PACK_COLS)).reshape(-1, PACK_COLS)


def _unpack_rows(buf, shapes):
    flat, out, pos = buf.reshape(-1), [], 0
    for sh in shapes:
        n = int(np.prod(sh))
        out.append(flat[pos:pos + n].reshape(sh))
        pos += -(-n // PACK_COLS) * PACK_COLS
    return out


def _split_chips(a, axis):
    sh = a.shape
    return jnp.moveaxis(a.reshape(sh[:axis] + (N_CHIPS, sh[axis] // N_CHIPS) + sh[axis + 1:]), axis, 0)


def _merge_chips(a, axis):
    a = jnp.moveaxis(a, 0, axis)
    sh = a.shape
    return a.reshape(sh[:axis] + (sh[axis] * sh[axis + 1],) + sh[axis + 2:])


def kernel(x, mem, positions, mem_norm, mix_norm, xa_w_kv, xa_q_norm, xa_k_norm, ffn_norm, ffn_w_up, ffn_conv_w, ffn_conv_b, ffn_w_down, a_w_in, a_w_gate2, a_b_gate, a_o_norm, a_w_out, b_w_in, b_q_norm, b_k_norm, b_w_out, c_w_in, c_conv_w, c_conv_b, c_dt_bias, c_a_log, c_d, c_norm, c_w_out, d_w_in, d_lower_bounds, d_o_norm, d_w_out, loss_target, m_mem_norm, m_mix_norm, m_xa_w_kv, m_xa_q_norm, m_xa_k_norm, m_ffn_norm, m_ffn_w_up, m_ffn_conv_w, m_ffn_conv_b, m_ffn_w_down, m_a_w_in, m_a_w_gate2, m_a_b_gate, m_a_o_norm, m_a_w_out, m_b_w_in, m_b_q_norm, m_b_k_norm, m_b_w_out, m_c_w_in, m_c_conv_w, m_c_conv_b, m_c_dt_bias, m_c_a_log, m_c_d, m_c_norm, m_c_w_out, m_d_w_in, m_d_lower_bounds, m_d_o_norm, m_d_w_out, v_mem_norm, v_mix_norm, v_xa_w_kv, v_xa_q_norm, v_xa_k_norm, v_ffn_norm, v_ffn_w_up, v_ffn_conv_w, v_ffn_conv_b, v_ffn_w_down, v_a_w_in, v_a_w_gate2, v_a_b_gate, v_a_o_norm, v_a_w_out, v_b_w_in, v_b_q_norm, v_b_k_norm, v_b_w_out, v_c_w_in, v_c_conv_w, v_c_conv_b, v_c_dt_bias, v_c_a_log, v_c_d, v_c_norm, v_c_w_out, v_d_w_in, v_d_lower_bounds, v_d_o_norm, v_d_w_out):
    args = locals()
    w = {n: args[n] for n in WEIGHTS}
    m = {n: args['m_' + n] for n in WEIGHTS}
    v = {n: args['v_' + n] for n in WEIGHTS}
    cx, cy, cc = lax.axis_index("x"), lax.axis_index("y"), lax.axis_index("c")
    chip = 2 * cx + cy

    core_id, chip_id = cc.reshape(1).astype(jnp.int32), chip.reshape(1).astype(jnp.int32)
    layers = ShardedLayers(w, core_id)
    layers.first_gather()
    full = {}
    small_sharded = [n for n in SMALL if n in SHARD_AXIS]
    sg = allgather_devices(_pack([w[n] for n in small_sharded], F32))
    per_chip_s = [_unpack(sg[2 * j], [w[n].shape for n in small_sharded]) for j in range(N_CHIPS)]
    for k, n in enumerate(small_sharded):
        full[n] = _merge_chips(jnp.stack([per_chip_s[j][k] for j in range(N_CHIPS)]), SHARD_AXIS[n])
    for n in SMALL:
        if n not in SHARD_AXIS:
            full[n] = w[n]

    loss_acc, dx, grads = local_step(x[0], mem[0], positions[0], loss_target[0], full, layers)
    loss = lax.psum(jnp.sum(loss_acc), ("x", "y", "c"))

    g_big = layers.finish(chip_id)

    small_full_shapes = [grads[n].shape for n in SMALL]
    gs = sum_slabs(allgather_devices(_pack_rows([grads[n] for n in SMALL])), "sum_devices")
    g_small = {}
    for n, gfull in zip(SMALL, _unpack_rows(gs, small_full_shapes)):
        if n in SHARD_AXIS:
            ax = SHARD_AXIS[n]
            size = gfull.shape[ax] // N_CHIPS
            gfull = lax.dynamic_slice_in_dim(gfull, chip * size, size, axis=ax)
        g_small[n] = gfull

    g_out, delta, new_m, new_v = {**g_big, **g_small}, {}, {}, {}
    for n in WEIGHTS:
        sh = w[n].shape
        two_d = (-1, sh[-1])
        d_, m_, v_ = adamw(f"adamw_{n}", w[n].reshape(two_d), g_out[n].reshape(two_d), m[n].reshape(two_d),
                           v[n].reshape(two_d))
        delta[n], new_m[n], new_v[n] = d_.reshape(sh), m_.reshape(sh), v_.reshape(sh)

    return (loss, dx[None], *[g_out[n] for n in WEIGHTS], *[delta[n] for n in WEIGHTS],
            *[new_m[n] for n in WEIGHTS], *[new_v[n] for n in WEIGHTS])
```

```python
import functools
import math

import jax
import jax.numpy as jnp
import numpy as np
from jax import lax
from jax.experimental import pallas as pl
from jax.experimental.pallas import tpu as pltpu

F32 = jnp.float32
MXU_DTYPE = jnp.bfloat16
GRAD_WIRE_DTYPE = jnp.bfloat16
VMEM_LIMIT_V7X = 56 * 1024 * 1024
LANE = 128
SUBLANE = 8

D_MODEL = 1024
N_MEM = 256
EPS = 1e-6
ROPE_THETA = 10000.0
CHUNK = 64
XA_HEADS, XA_HD = 4, 64
GLA_HEADS, GLA_DK, GLA_DV, GLA_RANK, GLA_GATE_NORM = 4, 96, 192, 16, 16.0
DIL_GROUPS = ((128, 1), (512, 4), (2048, 16))
DIL_HEADS, DIL_HD, DIL_BLOCK = 4, 128, 128
SSM_HD, SSM_HEADS, SSM_GROUPS, SSM_STATE, SSM_CONV = 64, 12, 2, 128, 4
HGRN_HEADS, HGRN_DK = 6, 128
D_FF = 2816
FFN_CONV = 3
ADAM_LR, ADAM_B1, ADAM_B2, ADAM_EPS, ADAM_WD, ADAM_STEP = 0.001, 0.9, 0.999, 1e-08, 0.01, 10

MM_TILES = (2816, 1408, 1024, 768, 512, 384, 256, 128)
MM_K_TILES = (2816, 2048, 1536, 1408, 1024, 768, 512, 384, 256, 128)
MM_MIN_OUT_TILE = 512 * 1024
MM_VMEM_BUDGET = 40 * 1024 * 1024
ROW_BLOCK = 256
SCAN_CHUNKS = 2
PACK_COLS = 1024
PACK_ROWS = 32

WEIGHTS = ['mem_norm', 'mix_norm', 'xa_w_kv', 'xa_q_norm', 'xa_k_norm', 'ffn_norm', 'ffn_w_up', 'ffn_conv_w',
           'ffn_conv_b', 'ffn_w_down', 'a_w_in', 'a_w_gate2', 'a_b_gate', 'a_o_norm', 'a_w_out', 'b_w_in', 'b_q_norm',
           'b_k_norm', 'b_w_out', 'c_w_in', 'c_conv_w', 'c_conv_b', 'c_dt_bias', 'c_a_log', 'c_d', 'c_norm', 'c_w_out',
           'd_w_in', 'd_lower_bounds', 'd_o_norm', 'd_w_out']
SHARD_AXIS = {'xa_w_kv': 1, 'ffn_w_up': 2, 'ffn_conv_w': 2, 'ffn_w_down': 1, 'a_w_in': 1, 'a_w_gate2': 1, 'a_w_out': 0,
              'b_w_in': 1, 'b_w_out': 1, 'c_w_in': 1, 'c_conv_w': 1, 'c_w_out': 0, 'd_w_in': 1, 'd_w_out': 0}
BIG = ['xa_w_kv', 'ffn_w_up', 'ffn_w_down', 'a_w_in', 'a_w_gate2', 'a_w_out', 'b_w_in', 'b_w_out', 'c_w_in', 'c_w_out',
       'd_w_in', 'd_w_out']
SMALL = [n for n in WEIGHTS if n not in BIG]
LAYERED = ['ffn_w_up', 'ffn_w_down']
N_CHIPS = 4
N_DEV = 8


class _MatmulSet:
    def __init__(self, cast, precision):
        def dot(a, b, dims):
            if cast:
                a = a.astype(MXU_DTYPE)
                b = b.astype(MXU_DTYPE)
            return lax.dot_general(a, b, (dims, ((), ())), precision=precision, preferred_element_type=F32)

        @jax.custom_vjp
        def nn(a, b):
            return dot(a, b, ((1,), (0,)))

        @jax.custom_vjp
        def nt(a, b):
            return dot(a, b, ((1,), (1,)))

        @jax.custom_vjp
        def tn(a, b):
            return dot(a, b, ((0,), (0,)))

        nn.defvjp(lambda a, b: (nn(a, b), (a, b)), lambda r, g: (nt(g, r[1]), tn(r[0], g)))
        nt.defvjp(lambda a, b: (nt(a, b), (a, b)), lambda r, g: (nn(g, r[1]), tn(g, r[0])))
        tn.defvjp(lambda a, b: (tn(a, b), (a, b)), lambda r, g: (nt(r[1], g), nn(r[0], g)))
        self.nn, self.nt, self.tn = nn, nt, tn


mm = _MatmulSet(True, None)
hi = _MatmulSet(False, lax.Precision.HIGHEST)


def _sigmoid(x):
    return jax.nn.sigmoid(x)


def _silu(x):
    return x * jax.nn.sigmoid(x)


def _softplus(x):
    return jnp.maximum(x, 0.0) + jnp.log1p(jnp.exp(-jnp.abs(x)))


def _rms(x, g, n_real=None):
    n = n_real or x.shape[-1]
    ms = jnp.sum(x * x, axis=-1, keepdims=True) * (1.0 / n)
    return x * lax.rsqrt(ms + EPS) * g


@jax.custom_vjp
def _swap_halves(x):
    return pltpu.roll(x, 64, 1)


_swap_halves.defvjp(lambda x: (_swap_halves(x), None), lambda _, g: (_swap_halves(g),))


def _tile(n, cands):
    for c in cands:
        if n % c == 0:
            return c
    raise ValueError(f"no tile for {n} among {cands}")


def _cparams(sem):
    return pltpu.CompilerParams(dimension_semantics=sem, vmem_limit_bytes=VMEM_LIMIT_V7X)


def _f32(v):
    return v.astype(F32) if jnp.issubdtype(v.dtype, jnp.floating) else v


def matmul(a, b, *, ta=False, tb=False, add=None, out_dtype=F32, b_layer=None, b_koff=0, into=None):
    m, k = (a.shape[1], a.shape[0]) if ta else a.shape
    b2 = b.shape[1:] if b_layer is not None else b.shape
    n = b2[0] if tb else b2[1]
    assert b_koff + k <= (b2[1] if tb else b2[0]), (a.shape, b.shape, ta, tb, b_koff)
    sa, sb, so = a.dtype.itemsize, b.dtype.itemsize, jnp.dtype(out_dtype).itemsize
    n_align = math.gcd(n, into[2]) if into is not None and into[2] else n
    k_align = math.gcd(k, b_koff) if b_koff else k

    def vmem(tm_, tn_, tk_):
        return (2 * tm_ * tk_ * sa + 2 * tk_ * tn_ * sb + 2 * tm_ * tn_ * so + (tm_ * tn_ * 4 if tk_ < k else 0)
                + (2 * tm_ * tn_ * add.dtype.itemsize if add is not None else 0))

    for tk in [t for t in MM_K_TILES if k_align % t == 0]:
        fits = [(tm_ * tn_, tm_, tn_) for tm_ in MM_TILES if m % tm_ == 0 for tn_ in MM_TILES
                if n % tn_ == 0 and n_align % tn_ == 0 and vmem(tm_, tn_, tk) <= MM_VMEM_BUDGET]
        if fits and (max(fits)[0] >= min(MM_MIN_OUT_TILE, m * n) or tk == MM_K_TILES[-1]):
            break
    _, tm, tn = max(fits)
    nk = k // tk
    dims = (((0,) if ta else (1,)), ((1,) if tb else (0,)))
    n_extra = (add is not None) + (into is not None)

    def body(*refs):
        a_ref, b_ref = refs[0], refs[1]
        add_ref = refs[2] if add is not None else None
        o_ref = refs[2 + n_extra]
        part = lax.dot_general(a_ref[...].astype(MXU_DTYPE), b_ref[...].astype(MXU_DTYPE), (dims, ((), ())),
                               preferred_element_type=F32)

        def finish(r):
            if add_ref is not None:
                r = r + add_ref[...].astype(F32)
            o_ref[...] = r.astype(o_ref.dtype)

        if nk == 1:
            finish(part)
            return
        acc = refs[-1]
        kk = pl.program_id(2)

        @pl.when(kk == 0)
        def _():
            acc[...] = part

        @pl.when(kk > 0)
        def _():
            acc[...] += part

        @pl.when(kk == nk - 1)
        def _():
            finish(acc[...])

    a_spec = pl.BlockSpec((tk, tm), lambda i, j, q: (q, i)) if ta else pl.BlockSpec((tm, tk), lambda i, j, q: (i, q))
    ko = b_koff // tk
    if b_layer is None:
        b_spec = (pl.BlockSpec((tn, tk), lambda i, j, q: (j, q + ko)) if tb
                  else pl.BlockSpec((tk, tn), lambda i, j, q: (q + ko, j)))
    else:
        b_spec = (pl.BlockSpec((None, tn, tk), lambda i, j, q: (b_layer, j, q + ko)) if tb
                  else pl.BlockSpec((None, tk, tn), lambda i, j, q: (b_layer, q + ko, j)))
    o_spec = pl.BlockSpec((tm, tn), lambda i, j, q: (i, j))
    ins, specs = [a, b], [a_spec, b_spec]
    if add is not None:
        ins.append(add)
        specs.append(o_spec)
    aliases = {}
    out_shape = jax.ShapeDtypeStruct((m, n), out_dtype)
    if into is not None:
        buf, layer, col0 = into
        assert buf.shape[1] == m and buf.dtype == out_dtype
        co = col0 // tn
        ins.append(buf)
        specs.append(_ANY)
        aliases = {len(ins) - 1: 0}
        o_spec = pl.BlockSpec((None, tm, tn), lambda i, j, q: (layer, i, j + co))
        out_shape = jax.ShapeDtypeStruct(buf.shape, buf.dtype)
    return pl.pallas_call(
        body, name=f"mm_{m}x{k}x{n}_{int(ta)}{int(tb)}{int(add is not None)}{int(b_layer is not None)}{int(into is not None)}",
        grid=(m // tm, n // tn, nk), in_specs=specs, out_specs=o_spec, out_shape=out_shape,
        input_output_aliases=aliases,
        scratch_shapes=[pltpu.VMEM((tm, tn), F32)] if nk > 1 else [],
        compiler_params=_cparams(("parallel", "parallel", "arbitrary")),
    )(*ins)


def matmul_nt_sum(a_list, b_list, tm=256):
    n_ops = len(a_list)
    m, n = a_list[0].shape[0], b_list[0].shape[0]

    def body(*refs):
        acc = None
        for t in range(n_ops):
            part = lax.dot_general(refs[t][...].astype(MXU_DTYPE), refs[n_ops + t][...].astype(MXU_DTYPE),
                                   (((1,), (1,)), ((), ())), preferred_element_type=F32)
            acc = part if acc is None else acc + part
        refs[2 * n_ops][...] = acc

    return pl.pallas_call(
        body, name=f"mm_nt_sum_{n_ops}x{sum(a.shape[1] for a in a_list)}", grid=(m // tm,),
        in_specs=[pl.BlockSpec((tm, a.shape[1]), lambda i: (i, 0)) for a in a_list] + [_whole_spec(b) for b in b_list],
        out_specs=pl.BlockSpec((tm, n), lambda i: (i, 0)), out_shape=jax.ShapeDtypeStruct((m, n), F32),
        compiler_params=_cparams(("parallel",)),
    )(*a_list, *b_list)


def _row_spec(a, block):
    return pl.BlockSpec((block, a.shape[1]), lambda i: (i, 0))


def _whole_spec(a):
    return pl.BlockSpec(a.shape, lambda i: (0,) * a.ndim)


def tmap(name, fn, rows, params, outs, block=ROW_BLOCK):
    s = rows[0].shape[0]
    block = min(block, s)
    nr, npar = len(rows), len(params)

    def body(*refs):
        res = fn(*[_f32(r[...]) for r in refs[:nr]], *[_f32(p[...]) for p in refs[nr:nr + npar]])
        for o_ref, v in zip(refs[nr + npar:], res, strict=True):
            o_ref[...] = v.astype(o_ref.dtype)

    return pl.pallas_call(
        body, name=name, grid=(s // block,),
        in_specs=[_row_spec(a, block) for a in rows] + [_whole_spec(p) for p in params],
        out_specs=[pl.BlockSpec((block, w), lambda i: (i, 0)) for w, _ in outs],
        out_shape=[jax.ShapeDtypeStruct((s, w), dt) for w, dt in outs],
        compiler_params=_cparams(("parallel",)),
    )(*rows, *params)


def tmap_bwd(name, fn, rows, params, douts, row_grad, row_add=None, grad_dtype=None, block=ROW_BLOCK):
    s = rows[0].shape[0]
    block = min(block, s)
    grad_dtype = grad_dtype or MXU_DTYPE
    nr, npar, nd = len(rows), len(params), len(douts)
    gr = [i for i in range(nr) if row_grad[i]]
    row_add = row_add or {}
    adds = [row_add[i] for i in gr if i in row_add]

    def body(*refs):
        rv = [_f32(r[...]) for r in refs[:nr]]
        pv = [_f32(p[...]) for p in refs[nr:nr + npar]]
        dv = tuple(_f32(d[...]) for d in refs[nr + npar:nr + npar + nd])
        add_refs = list(refs[nr + npar + nd:nr + npar + nd + len(adds)])
        out_refs = refs[nr + npar + nd + len(adds):]

        def f(*diff):
            rr = list(rv)
            for n_, i_ in enumerate(gr):
                rr[i_] = diff[n_]
            return tuple(fn(*rr, *diff[len(gr):]))

        _, vjp = jax.vjp(f, *[rv[i_] for i_ in gr], *pv)
        g = vjp(dv)
        for n_, i_ in enumerate(gr):
            v = g[n_]
            if i_ in row_add:
                v = v + add_refs.pop(0)[...].astype(F32)
            out_refs[n_][...] = v.astype(out_refs[n_].dtype)
        first = pl.program_id(0) == 0
        for n_ in range(npar):
            ref = out_refs[len(gr) + n_]

            @pl.when(first)
            def _(ref=ref):
                ref[...] = jnp.zeros_like(ref)

            ref[...] += g[len(gr) + n_]

    res = pl.pallas_call(
        body, name=name, grid=(s // block,),
        in_specs=[_row_spec(a, block) for a in rows] + [_whole_spec(p) for p in params]
        + [_row_spec(d, block) for d in douts] + [_row_spec(a, block) for a in adds],
        out_specs=[_row_spec(rows[i], block) for i in gr] + [_whole_spec(p) for p in params],
        out_shape=[jax.ShapeDtypeStruct(rows[i].shape, F32 if i in row_add else grad_dtype) for i in gr]
        + [jax.ShapeDtypeStruct(p.shape, F32) for p in params],
        compiler_params=_cparams(("arbitrary",)),
    )(*rows, *params, *douts, *adds)
    return list(res[:len(gr)]), list(res[len(gr):])


def rscan(name, fn, state_shapes, rows, params, outs, block):
    s = rows[0].shape[0]
    nsteps = s // block
    nr, npar, no, ns = len(rows), len(params), len(outs), len(state_shapes)

    def body(*refs):
        out_refs = refs[nr + npar:nr + npar + no]
        sav_refs = refs[nr + npar + no:nr + npar + no + ns]
        st_refs = refs[nr + npar + no + ns:]

        @pl.when(pl.program_id(0) == 0)
        def _():
            for st in st_refs:
                st[...] = jnp.zeros_like(st)

        sts = tuple(st[...] for st in st_refs)
        for sv, v in zip(sav_refs, sts):
            sv[...] = v
        new, res = fn(sts, *[_f32(r[...]) for r in refs[:nr]], *[_f32(p[...]) for p in refs[nr:nr + npar]])
        for st, v in zip(st_refs, new, strict=True):
            st[...] = v
        for o_ref, v in zip(out_refs, res, strict=True):
            o_ref[...] = v.astype(o_ref.dtype)

    res = pl.pallas_call(
        body, name=name, grid=(nsteps,),
        in_specs=[_row_spec(a, block) for a in rows] + [_whole_spec(p) for p in params],
        out_specs=[pl.BlockSpec((block, w), lambda i: (i, 0)) for w, _ in outs]
        + [pl.BlockSpec(sh, lambda i: (i, 0)) for sh in state_shapes],
        out_shape=[jax.ShapeDtypeStruct((s, w), dt) for w, dt in outs]
        + [jax.ShapeDtypeStruct((nsteps * sh[0], sh[1]), F32) for sh in state_shapes],
        scratch_shapes=[pltpu.VMEM(sh, F32) for sh in state_shapes],
        compiler_params=_cparams(("arbitrary",)),
    )(*rows, *params)
    return list(res[:no]), list(res[no:])


def rscan_bwd(name, fn, saved, rows, params, douts, block, grad_dtype=None):
    s = rows[0].shape[0]
    nsteps = s // block
    grad_dtype = grad_dtype or MXU_DTYPE
    nr, npar, nd, ns = len(rows), len(params), len(douts), len(saved)
    state_shapes = [(sv.shape[0] // nsteps, sv.shape[1]) for sv in saved]

    def body(*refs):
        rv = [_f32(r[...]) for r in refs[:nr]]
        pv = [_f32(p[...]) for p in refs[nr:nr + npar]]
        dv = tuple(_f32(d[...]) for d in refs[nr + npar:nr + npar + nd])
        sv = tuple(x[...] for x in refs[nr + npar + nd:nr + npar + nd + ns])
        out_refs = refs[nr + npar + nd + ns:nr + npar + nd + ns + nr + npar]
        dst_refs = refs[nr + npar + nd + ns + nr + npar:]
        first = pl.program_id(0) == 0

        @pl.when(first)
        def _():
            for d in dst_refs:
                d[...] = jnp.zeros_like(d)

        def f(sts, *args):
            return fn(sts, *args)

        _, vjp = jax.vjp(f, sv, *rv, *pv)
        g = vjp((tuple(d[...] for d in dst_refs), dv))
        for d, v in zip(dst_refs, g[0], strict=True):
            d[...] = v
        for n_ in range(nr):
            out_refs[n_][...] = g[1 + n_].astype(out_refs[n_].dtype)
        for n_ in range(npar):
            ref = out_refs[nr + n_]

            @pl.when(first)
            def _(ref=ref):
                ref[...] = jnp.zeros_like(ref)

            ref[...] += g[1 + nr + n_]

    rev = lambda i: (nsteps - 1 - i, 0)
    res = pl.pallas_call(
        body, name=name, grid=(nsteps,),
        in_specs=[pl.BlockSpec((block, a.shape[1]), rev) for a in rows] + [_whole_spec(p) for p in params]
        + [pl.BlockSpec((block, d.shape[1]), rev) for d in douts] + [pl.BlockSpec(sh, rev) for sh in state_shapes],
        out_specs=[pl.BlockSpec((block, a.shape[1]), rev) for a in rows] + [_whole_spec(p) for p in params],
        out_shape=[jax.ShapeDtypeStruct(a.shape, grad_dtype) for a in rows]
        + [jax.ShapeDtypeStruct(p.shape, F32) for p in params],
        scratch_shapes=[pltpu.VMEM(sh, F32) for sh in state_shapes],
        compiler_params=_cparams(("arbitrary",)),
    )(*rows, *params, *douts, *saved)
    return list(res[:nr]), list(res[nr:])


def _norm_stage(x, g):
    return (_rms(x, g),)


def _tril():
    r = lax.broadcasted_iota(jnp.int32, (CHUNK, CHUNK), 0)
    c = lax.broadcasted_iota(jnp.int32, (CHUNK, CHUNK), 1)
    return r >= c


def _gla_chunk(st, q, k, v, la, b):
    tril = _tril()
    rowi = lax.broadcasted_iota(jnp.int32, (CHUNK, 1), 0)
    b_last = jnp.sum(la, axis=0, keepdims=True)
    b_ref = jnp.sum(jnp.where(rowi < CHUNK // 2, la, 0.0), axis=0, keepdims=True)
    att = mm.nt(q * jnp.exp(b - b_ref), k * jnp.exp(b_ref - b))
    att = jnp.where(tril, att, 0.0)
    o = mm.nn(att, v) + mm.nn(q * jnp.exp(b), st)
    decay = jnp.exp(jnp.broadcast_to(b_last, (LANE, LANE)).T)
    decay = jnp.concatenate([decay] * (v.shape[1] // LANE), axis=1)
    st2 = decay * st + mm.tn(k * jnp.exp(b_last - b), v)
    return st2, o


def _gla_step(heads, vp, scale):
    kp = LANE

    def fn(states, q, k, v, la):
        sts = list(states)
        trif = _tril().astype(F32)
        rows = []
        for c in range(q.shape[0] // CHUNK):
            r = slice(c * CHUNK, (c + 1) * CHUNK)
            b_all = hi.nn(trif, la[r])
            oh = []
            for h in range(heads):
                ks, vs = slice(h * kp, (h + 1) * kp), slice(h * vp, (h + 1) * vp)
                qh = q[r, ks] * scale if scale != 1.0 else q[r, ks]
                sts[h], o = _gla_chunk(sts[h], qh, k[r, ks], v[r, vs], la[r, ks], b_all[:, ks])
                oh.append(o)
            rows.append(jnp.concatenate(oh, axis=1))
        return tuple(sts), (jnp.concatenate(rows, axis=0),)

    return fn


def _ssd_step(states, xa, dtr, dtb, alog, dsk):
    sts = list(states)
    trif = _tril().astype(F32)
    wide = lax.broadcasted_iota(jnp.int32, (CHUNK, LANE), 0) >= lax.broadcasted_iota(jnp.int32, (CHUNK, LANE), 1)
    hg = SSM_HEADS // SSM_GROUPS
    xw = SSM_HEADS * LANE
    lane, head = lax.broadcasted_iota(jnp.int32, (LANE, xw), 1), lax.broadcasted_iota(jnp.int32, (LANE, xw), 0)
    spread = ((lane >= head * LANE) & (lane < (head + 1) * LANE)).astype(F32)
    neg_a = -jnp.exp(alog)
    pad = jnp.zeros((CHUNK, LANE), F32)
    rows = []
    for c in range(xa.shape[0] // CHUNK):
        r = slice(c * CHUNK, (c + 1) * CHUNK)
        dt_all = _softplus(hi.nn(dtr[r], spread) + dtb)
        a_all = dt_all * neg_a
        acs_all = hi.nn(trif, a_all)
        last_all = jnp.sum(a_all, axis=0, keepdims=True)
        yh = []
        for g in range(SSM_GROUPS):
            bm = xa[r, xw + g * LANE:xw + (g + 1) * LANE]
            cm = xa[r, xw + (SSM_GROUPS + g) * LANE:xw + (SSM_GROUPS + g + 1) * LANE]
            cb = mm.nt(cm, jnp.concatenate([bm, pad], axis=0))
            for hh in range(hg):
                h = g * hg + hh
                ls = slice(h * LANE, (h + 1) * LANE)
                xs, acs, acs_last = xa[r, ls], acs_all[:, ls], last_all[:, ls]
                xdt = xs * dt_all[:, ls]
                seg = acs - jnp.concatenate([acs, pad], axis=0).T[:CHUNK]
                lmat = jnp.exp(jnp.where(wide, seg, -1e30))
                y = (mm.nn(cb * lmat, jnp.concatenate([xdt, pad], axis=0)) + mm.nn(cm, sts[h]) * jnp.exp(acs)
                     + dsk[:, ls] * xs)
                sts[h] = jnp.exp(acs_last) * sts[h] + mm.tn(bm, xdt * jnp.exp(acs_last - acs))
                yh.append(y)
        rows.append(jnp.concatenate(yh, axis=1))
    return tuple(sts), (jnp.concatenate(rows, axis=0),)


def _gla_pre(glr, w2, bg):
    z = mm.nn(glr, w2) + bg
    return (-_softplus(-z) * (1.0 / GLA_GATE_NORM),)


def _gla_post(o, og, g):
    w = 2 * LANE
    return (jnp.concatenate([_rms(o[:, h * w:(h + 1) * w], g, GLA_DV) * _silu(og[:, h * w:(h + 1) * w])
                             for h in range(GLA_HEADS)], axis=1),)


def _hgrn_pre(q, f, lbnd):
    e = jnp.exp(lbnd - jnp.max(lbnd, axis=0, keepdims=True))
    rowi = lax.broadcasted_iota(jnp.int32, e.shape, 0)
    lb = jnp.sum(jnp.where(rowi >= 1, e, 0.0), axis=0, keepdims=True) / jnp.sum(e, axis=0, keepdims=True)
    fg = lb + (1.0 - lb) * _sigmoid(f)
    return _silu(q), 1.0 - fg, jnp.log(fg)


def _hgrn_post(o, og, g):
    return (jnp.concatenate([_rms(o[:, h * LANE:(h + 1) * LANE], g) for h in range(HGRN_HEADS)], axis=1)
            * _sigmoid(og),)


def _mamba_post(y, z, g):
    v = y * _silu(z)
    w = (SSM_HEADS // SSM_GROUPS) * LANE
    n_real = (SSM_HEADS // SSM_GROUPS) * SSM_HD
    return (jnp.concatenate([_rms(v[:, i * w:(i + 1) * w], g[:, i * w:(i + 1) * w], n_real)
                             for i in range(SSM_GROUPS)], axis=1),)


def _dil_pre(q, k, cosf, sinf, qg, kg):
    def groups(x, g):
        out = []
        for grp in range(len(DIL_GROUPS)):
            hs = []
            for h in range(grp * DIL_HEADS, (grp + 1) * DIL_HEADS):
                n = _rms(x[:, h * LANE:(h + 1) * LANE], g)
                hs.append(n * cosf + _swap_halves(n) * sinf)
            out.append(jnp.concatenate(hs, axis=1))
        return out

    return (*groups(q, qg), *groups(k, kg))


def _dil_merge(o0, o1, o2, l0, l1, l2):
    m = jnp.maximum(jnp.maximum(l0, l1), l2)
    e0, e1, e2 = jnp.exp(l0 - m), jnp.exp(l1 - m), jnp.exp(l2 - m)
    return ((e0 * o0 + e1 * o1 + e2 * o2) / (e0 + e1 + e2),)


def _dil_block(q, kp, kc, vp, vc, lim):
    kk = jnp.concatenate([kp, kc], axis=0)
    vv = jnp.concatenate([vp, vc], axis=0)
    s = mm.nt(q, kk) * (DIL_HD ** -0.5)
    i = lax.broadcasted_iota(jnp.int32, s.shape, 0)
    j = lax.broadcasted_iota(jnp.int32, s.shape, 1)
    dist = DIL_BLOCK + i - j
    s = jnp.where((dist >= 0) & (dist <= DIL_BLOCK) & (j >= lim), s, -1e30)
    m = jnp.max(s, axis=-1, keepdims=True)
    p = jnp.exp(s - m)
    l = jnp.sum(p, axis=-1, keepdims=True)
    return mm.nn(p / l, vv), jnp.broadcast_to(m + jnp.log(l), (q.shape[0], LANE))


def _xattn(xq, kv, qg, kg):
    w = XA_HEADS * LANE
    os_ = []
    for h in range(XA_HEADS):
        ls = slice(h * LANE, (h + 1) * LANE)
        q = _rms(xq[:, ls], qg, XA_HD)
        k = _rms(kv[:, ls], kg, XA_HD)
        s = mm.nt(q, k) * (XA_HD ** -0.5)
        p = jnp.exp(s - jnp.max(s, axis=-1, keepdims=True))
        p = p / jnp.sum(p, axis=-1, keepdims=True)
        os_.append(mm.nn(p, kv[:, w + h * LANE:w + (h + 1) * LANE]))
    return (jnp.concatenate(os_, axis=1),)


def _dil_geometry(s, w, r, g, v_cols):
    hb = DIL_HEADS if r == 1 else 1
    rb = DIL_BLOCK * r
    nb = s // rb
    bw = hb * LANE
    v_col0 = g * (w // bw)
    assert v_cols % bw == 0 and s % rb == 0
    return hb, rb, nb, bw, v_col0


def _sub(r, res):
    return pl.ds(res, DIL_BLOCK, stride=r) if r > 1 else slice(None)


def dil_attn(name, q, k, v, r, g):
    s, w = q.shape
    hb, rb, nb, bw, v_col0 = _dil_geometry(s, w, r, g, v.shape[1])

    def body(q_r, kp_r, kc_r, vp_r, vc_r, o_r, l_r):
        lim = jnp.where(pl.program_id(1) == 0, DIL_BLOCK, 0)
        for res in range(r):
            rows = _sub(r, res)
            for h in range(hb):
                ls = slice(h * LANE, (h + 1) * LANE)
                o, lse = _dil_block(q_r[rows, ls], kp_r[rows, ls], kc_r[rows, ls], vp_r[rows, ls], vc_r[rows, ls], lim)
                o_r[rows, ls] = o
                l_r[rows, ls] = lse

    cur = pl.BlockSpec((rb, bw), lambda hblk, n: (n, hblk))
    prev = pl.BlockSpec((rb, bw), lambda hblk, n: (jnp.maximum(n - 1, 0), hblk))
    vcur = pl.BlockSpec((rb, bw), lambda hblk, n: (n, v_col0 + hblk))
    vprev = pl.BlockSpec((rb, bw), lambda hblk, n: (jnp.maximum(n - 1, 0), v_col0 + hblk))
    return pl.pallas_call(
        body, name=name, grid=(w // bw, nb), in_specs=[cur, prev, cur, vprev, vcur], out_specs=[cur, cur],
        out_shape=[jax.ShapeDtypeStruct((s, w), F32)] * 2,
        compiler_params=_cparams(("parallel", "parallel")),
    )(q, k, k, v, v)


def dil_attn_bwd(name, q, k, v, do, dlse, r, g):
    s, w = q.shape
    hb, rb, nb, bw, v_col0 = _dil_geometry(s, w, r, g, v.shape[1])

    def body(q_r, kp_r, kc_r, vp_r, vc_r, do_r, dl_r, dq_r, dk_r, dv_r, ck, cv):
        i = pl.program_id(1)
        lim = jnp.where(i == nb - 1, DIL_BLOCK, 0)

        @pl.when(i == 0)
        def _():
            ck[...] = jnp.zeros_like(ck)
            cv[...] = jnp.zeros_like(cv)

        for res in range(r):
            rows = _sub(r, res)
            for h in range(hb):
                ls = slice(h * LANE, (h + 1) * LANE)
                _, vjp = jax.vjp(functools.partial(_dil_block, lim=lim),
                                 q_r[rows, ls], kp_r[rows, ls], kc_r[rows, ls], vp_r[rows, ls], vc_r[rows, ls])
                gq, gkp, gkc, gvp, gvc = vjp((do_r[rows, ls], dl_r[rows, ls]))
                dq_r[rows, ls] = gq
                dk_r[rows, ls] = gkc + ck[rows, ls]
                dv_r[rows, ls] = gvc + cv[rows, ls]
                ck[rows, ls] = gkp
                cv[rows, ls] = gvp

    cur = pl.BlockSpec((rb, bw), lambda hblk, i: (nb - 1 - i, hblk))
    prev = pl.BlockSpec((rb, bw), lambda hblk, i: (jnp.maximum(nb - 2 - i, 0), hblk))
    vcur = pl.BlockSpec((rb, bw), lambda hblk, i: (nb - 1 - i, v_col0 + hblk))
    vprev = pl.BlockSpec((rb, bw), lambda hblk, i: (jnp.maximum(nb - 2 - i, 0), v_col0 + hblk))
    return pl.pallas_call(
        body, name=name, grid=(w // bw, nb), in_specs=[cur, prev, cur, vprev, vcur, cur, cur],
        out_specs=[cur, cur, cur], out_shape=[jax.ShapeDtypeStruct((s, w), F32)] * 3,
        scratch_shapes=[pltpu.VMEM((rb, bw), F32)] * 2,
        compiler_params=_cparams(("parallel", "arbitrary")),
    )(q, k, k, v, v, do, dlse)


def _dsilu(u):
    sg = _sigmoid(u)
    return sg * (1.0 + u * (1.0 - sg))


def _ride(body, rider, n_in, n_out, n_scratch, grid):
    if rider is None:
        return body, [], [], [], [], [], None
    ni, no = len(rider.ins), len(rider.out_shapes)

    def wrapped(*refs):
        k_in, r_in = refs[:n_in], refs[n_in:n_in + ni]
        k_out, r_out = refs[n_in + ni:n_in + ni + n_out], refs[n_in + ni + n_out:n_in + ni + n_out + no]
        k_scr = refs[n_in + ni + n_out + no:n_in + ni + n_out + no + n_scratch]
        send_sems, recv_sems = refs[-2], refs[-1]
        first = functools.reduce(jnp.logical_and, [pl.program_id(a) == 0 for a in range(len(grid))])
        last = functools.reduce(jnp.logical_and, [pl.program_id(a) == g - 1 for a, g in enumerate(grid)])

        @pl.when(first)
        def _():
            rider.start(r_in, r_out, send_sems, recv_sems)

        body(*k_in, *k_out, *k_scr)

        @pl.when(last)
        def _():
            rider.finish(r_in, r_out, send_sems, recv_sems)

    sems = [pltpu.SemaphoreType.DMA((rider.n_sems,)), pltpu.SemaphoreType.DMA((rider.n_sems,))]
    return wrapped, rider.ins, [_ANY] * ni, [_ANY] * no, rider.out_shapes, sems, ("arbitrary",) * len(grid)


CONV_STRIP = 16


def _shifted_rows(prev8, cur_r, next8, lanes, s0, n, sh, block):
    if s0 - sh < 0:
        assert s0 == 0
        xp = jnp.concatenate([prev8, cur_r[0:n, lanes]], axis=0)
        return pltpu.roll(xp, sh, 0)[SUBLANE:SUBLANE + n]
    if s0 - sh + n > block:
        assert s0 == block and n == SUBLANE
        xp = jnp.concatenate([cur_r[block - SUBLANE:block, lanes], next8], axis=0)
        return (pltpu.roll(xp, sh, 0) if sh else xp)[SUBLANE:]
    return cur_r[pl.ds(s0 - sh, n), lanes]


def conv_fwd(name, x, w, b, mode, out_dtype, tc, block=ROW_BLOCK, rider=None):
    s, c = x.shape
    ntap = w.shape[0]
    block = min(block, s)
    f = c // 2 if mode == 'glu' else c
    nh = 2 if mode == 'glu' else 1
    off = f // tc

    def body(*refs):
        first = pl.program_id(1) == 0
        o_ref = refs[-1]

        def column(cidx, carry):
            lanes = pl.ds(pl.multiple_of(cidx * LANE, LANE), LANE)
            prevs = [jnp.where(first, 0.0, refs[4 * hlf][:, lanes]) for hlf in range(nh)]
            for s0 in range(0, block, CONV_STRIP):
                us = []
                for hlf in range(nh):
                    _, cur_r, w_r, b_r = refs[4 * hlf:4 * hlf + 4]
                    acc = b_r[:, lanes]
                    for j in range(ntap):
                        xs = _shifted_rows(prevs[hlf], cur_r, None, lanes, s0, CONV_STRIP, ntap - 1 - j, block)
                        acc = acc + w_r[j:j + 1, lanes] * xs
                    us.append(acc)
                res = _silu(us[0]) * us[1] if mode == 'glu' else _silu(us[0])
                o_ref[pl.ds(s0, CONV_STRIP), lanes] = res.astype(o_ref.dtype)
            return carry

        lax.fori_loop(0, tc // LANE, column, 0)

    rb = block // SUBLANE
    ins, specs = [], []
    for hlf in range(nh):
        o = hlf * off
        ins += [x, x, w, b]
        specs += [pl.BlockSpec((SUBLANE, tc), lambda j, i, o=o: (jnp.maximum(i * rb - 1, 0), j + o)),
                  pl.BlockSpec((block, tc), lambda j, i, o=o: (i, j + o)),
                  pl.BlockSpec((ntap, tc), lambda j, i, o=o: (0, j + o)),
                  pl.BlockSpec((1, tc), lambda j, i, o=o: (0, j + o))]
    grid = (f // tc, s // block)
    body, r_ins, r_in_specs, r_out_specs, r_out_shapes, r_scratch, sem = _ride(body, rider, len(ins), 1, 0, grid)
    res = pl.pallas_call(
        body, name=name, grid=grid, in_specs=specs + r_in_specs,
        out_specs=[pl.BlockSpec((block, tc), lambda j, i: (i, j))] + r_out_specs,
        out_shape=[jax.ShapeDtypeStruct((s, f), out_dtype)] + r_out_shapes, scratch_shapes=r_scratch,
        compiler_params=_cparams(sem or ("parallel", "parallel")),
    )(*ins, *r_ins)
    return res[0] if rider is None else (res[0], rider.results(res[1:]))


def conv_bwd(name, x, w, b, dout, mode, tc, block=ROW_BLOCK, rider=None):
    s, c = x.shape
    ntap = w.shape[0]
    block = min(block, s)
    nblk = s // block
    f = c // 2 if mode == 'glu' else c
    nh = 2 if mode == 'glu' else 1
    off = f // tc
    ext = block + SUBLANE

    def body(*refs):
        i = pl.program_id(1)
        first, last = i == 0, i == nblk - 1
        dcur_r, dnext_r = refs[5 * nh], refs[5 * nh + 1]
        outs = refs[5 * nh + 2:5 * nh + 2 + 3 * nh]
        du_scr = refs[5 * nh + 2 + 3 * nh:]

        @pl.when(first)
        def _():
            for hlf in range(nh):
                outs[3 * hlf + 1][...] = jnp.zeros_like(outs[3 * hlf + 1])
                outs[3 * hlf + 2][...] = jnp.zeros_like(outs[3 * hlf + 2])

        def column(cidx, carry):
            lanes = pl.ds(pl.multiple_of(cidx * LANE, LANE), LANE)
            prevs = [jnp.where(first, 0.0, refs[5 * hlf][:, lanes]) for hlf in range(nh)]
            nexts = [jnp.where(last, 0.0, refs[5 * hlf + 2][:, lanes]) for hlf in range(nh)]
            db_acc = [jnp.zeros((CONV_STRIP, LANE), F32) for _ in range(nh)]
            dw_acc = [[jnp.zeros((CONV_STRIP, LANE), F32) for _ in range(ntap)] for _ in range(nh)]
            for s0 in range(0, ext, CONV_STRIP):
                n = min(CONV_STRIP, ext - s0)
                d_e = dcur_r[pl.ds(s0, n), lanes] if s0 < block else jnp.where(last, 0.0, dnext_r[:, lanes])
                xs, us = [], []
                for hlf in range(nh):
                    cur_r, w_r, b_r = refs[5 * hlf + 1], refs[5 * hlf + 3], refs[5 * hlf + 4]
                    sh_rows = [_shifted_rows(prevs[hlf], cur_r, nexts[hlf], lanes, s0, n, ntap - 1 - j, block)
                               for j in range(ntap)]
                    acc = b_r[:, lanes]
                    for j in range(ntap):
                        acc = acc + w_r[j:j + 1, lanes] * sh_rows[j]
                    xs.append(sh_rows)
                    us.append(acc)
                dus = [d_e * us[1] * _dsilu(us[0]), d_e * _silu(us[0])] if mode == 'glu' else [d_e * _dsilu(us[0])]
                for hlf in range(nh):
                    du_scr[hlf][pl.ds(s0, n), lanes] = dus[hlf]
                    if s0 < block:
                        db_acc[hlf] = db_acc[hlf] + dus[hlf]
                        for j in range(ntap):
                            dw_acc[hlf][j] = dw_acc[hlf][j] + dus[hlf] * xs[hlf][j]
            for hlf in range(nh):
                w_r = refs[5 * hlf + 3]
                dx_r, dw_r, db_r = outs[3 * hlf:3 * hlf + 3]
                db_r[:, lanes] += jnp.sum(db_acc[hlf], axis=0, keepdims=True)
                for j in range(ntap):
                    dw_r[j:j + 1, lanes] += jnp.sum(dw_acc[hlf][j], axis=0, keepdims=True)
                for s0 in range(0, block, CONV_STRIP):
                    dx = None
                    for j in range(ntap):
                        term = w_r[j:j + 1, lanes] * du_scr[hlf][pl.ds(s0 + ntap - 1 - j, CONV_STRIP), lanes]
                        dx = term if dx is None else dx + term
                    dx_r[pl.ds(s0, CONV_STRIP), lanes] = dx.astype(dx_r.dtype)
            return carry

        lax.fori_loop(0, tc // LANE, column, 0)

    rb = block // SUBLANE
    nrow8 = s // SUBLANE
    ins, specs = [], []
    for hlf in range(nh):
        o = hlf * off
        ins += [x, x, x, w, b]
        specs += [pl.BlockSpec((SUBLANE, tc), lambda j, i, o=o: (jnp.maximum(i * rb - 1, 0), j + o)),
                  pl.BlockSpec((block, tc), lambda j, i, o=o: (i, j + o)),
                  pl.BlockSpec((SUBLANE, tc), lambda j, i, o=o: (jnp.minimum((i + 1) * rb, nrow8 - 1), j + o)),
                  pl.BlockSpec((ntap, tc), lambda j, i, o=o: (0, j + o)),
                  pl.BlockSpec((1, tc), lambda j, i, o=o: (0, j + o))]
    ins += [dout, dout]
    specs += [pl.BlockSpec((block, tc), lambda j, i: (i, j)),
              pl.BlockSpec((SUBLANE, tc), lambda j, i: (jnp.minimum((i + 1) * rb, nrow8 - 1), j))]
    out_specs, out_shape = [], []
    for hlf in range(nh):
        out_specs += [pl.BlockSpec((block, tc), lambda j, i: (i, j)), pl.BlockSpec((ntap, tc), lambda j, i: (0, j)),
                      pl.BlockSpec((1, tc), lambda j, i: (0, j))]
        out_shape += [jax.ShapeDtypeStruct((s, f), MXU_DTYPE), jax.ShapeDtypeStruct((ntap, f), F32),
                      jax.ShapeDtypeStruct((1, f), F32)]
    grid = (f // tc, nblk)
    body, r_ins, r_in_specs, r_out_specs, r_out_shapes, r_scratch, sem = _ride(body, rider, len(ins), 3 * nh, nh, grid)
    res = pl.pallas_call(
        body, name=name, grid=grid, in_specs=specs + r_in_specs, out_specs=out_specs + r_out_specs,
        out_shape=out_shape + r_out_shapes, scratch_shapes=[pltpu.VMEM((ext, tc), F32)] * nh + r_scratch,
        compiler_params=_cparams(sem or ("parallel", "arbitrary")),
    )(*ins, *r_ins)
    rode = None if rider is None else rider.results(res[3 * nh:])
    if nh == 1:
        return [res[0]], res[1], res[2], rode
    return ([res[0], res[3]], jnp.concatenate([res[1], res[4]], axis=1), jnp.concatenate([res[2], res[5]], axis=1),
            rode)


def loss_head(y, target, block=ROW_BLOCK):
    s, d = y.shape
    block = min(block, s)

    def body(y_r, t_r, acc_r, dy_r):
        e = y_r[...] - t_r[...]
        dy_r[...] = e * (1.0 / d)

        @pl.when(pl.program_id(0) == 0)
        def _():
            acc_r[...] = jnp.zeros_like(acc_r)

        acc_r[...] += jnp.sum((e * e).reshape(block // SUBLANE, SUBLANE, d), axis=0) * (0.5 / d)

    return pl.pallas_call(
        body, name="loss_head", grid=(s // block,),
        in_specs=[pl.BlockSpec((block, d), lambda i: (i, 0))] * 2,
        out_specs=[pl.BlockSpec((SUBLANE, d), lambda i: (0, 0)), pl.BlockSpec((block, d), lambda i: (i, 0))],
        out_shape=[jax.ShapeDtypeStruct((SUBLANE, d), F32), jax.ShapeDtypeStruct((s, d), F32)],
        compiler_params=_cparams(("arbitrary",)),
    )(y, target)


def adamw(name, w, g, m, v):
    r, c = w.shape
    tr = r if r <= 512 else _tile(r, (512, 256, 128, 64, 32, 16, 8))
    if c * tr * 4 > (1 << 21):
        tr = _tile(r, (256, 128, 64, 32, 16, 8))

    def body(w_r, g_r, m_r, v_r, d_r, nm_r, nv_r):
        gg = g_r[...]
        nm = ADAM_B1 * m_r[...] + (1.0 - ADAM_B1) * gg
        nv = ADAM_B2 * v_r[...] + (1.0 - ADAM_B2) * (gg * gg)
        m_hat = nm / (1.0 - ADAM_B1 ** ADAM_STEP)
        v_hat = nv / (1.0 - ADAM_B2 ** ADAM_STEP)
        d_r[...] = -ADAM_LR * (m_hat / (jnp.sqrt(v_hat) + ADAM_EPS) + ADAM_WD * w_r[...])
        nm_r[...] = nm
        nv_r[...] = nv

    spec = pl.BlockSpec((tr, c), lambda i: (i, 0))
    return pl.pallas_call(
        body, name=name, grid=(r // tr,), in_specs=[spec] * 4, out_specs=[spec] * 3,
        out_shape=[jax.ShapeDtypeStruct((r, c), F32)] * 3, compiler_params=_cparams(("parallel",)),
    )(w, g, m, v)


MESH = pl.DeviceIdType.MESH
_ANY = pl.BlockSpec(memory_space=pl.ANY)


def _place():
    return lax.axis_index("x"), lax.axis_index("y"), lax.axis_index("c")


class Packed:
    def __init__(self, shard_shape):
        self.r, self.c = shard_shape
        self.h = self.r // 2
        self.whole = (N_CHIPS, self.r, self.c)
        self.got = (N_CHIPS, self.h, self.c)
        self.slab_half = (self.h, self.c)

    def shard_half(self, ref, core):
        return ref.at[pl.ds(core * self.h, self.h)]

    def whole_half(self, ref, chip, core):
        return ref.at[chip, pl.ds(core * self.h, self.h)]

    def place(self, whole, shard, chip):
        return lax.dynamic_update_slice(whole, shard[None], (chip, 0, 0))

    def grad_half(self, ref, core):
        return ref.at[:, core]

    def pair_slab(self, ref, chip):
        return ref.at[chip]


class SlabCols:
    def __init__(self, shard_shape):
        self.r, self.c = shard_shape
        self.h = self.r // 2
        self.whole = (self.r, N_CHIPS * self.c)
        self.got = (self.h, N_CHIPS * self.c)
        self.slab_half = (self.h, self.c)

    def _cols(self, chip):
        return pl.ds(pl.multiple_of(chip * self.c, LANE), self.c)

    def shard_half(self, ref, core):
        return ref.at[pl.ds(core * self.h, self.h)]

    def whole_half(self, ref, chip, core):
        return ref.at[pl.ds(core * self.h, self.h), self._cols(chip)]

    def place(self, whole, shard, chip):
        return lax.dynamic_update_slice_in_dim(whole, shard, chip * self.c, 1)

    def grad_half(self, ref, core):
        return ref.at[pl.ds(core * self.h, self.h)]

    def pair_slab(self, ref, chip):
        return ref.at[:, self._cols(chip)]


class GatherRider:
    def __init__(self, shards, kinds):
        self.ins, self.kinds, n = list(shards), kinds, len(shards)
        self.out_shapes = [jax.ShapeDtypeStruct(k.whole, s.dtype) for k, s in zip(kinds, shards)]
        self.n_sems = 6 * n

    def _copies(self, w_refs, out_refs, send_sems, recv_sems):
        x, y, cc = _place()
        chips = [(1 - x, y), (x, 1 - y), (1 - x, 1 - y)]

        def copy(t, k, chip, core, to, src=None):
            dst = self.kinds[t].whole_half(out_refs[t], 2 * chip[0] + chip[1], core)
            return pltpu.make_async_remote_copy(
                src_ref=dst if src is None else src, dst_ref=dst, send_sem=send_sems.at[6 * t + k],
                recv_sem=recv_sems.at[6 * t + k], device_id=to, device_id_type=MESH)

        first = [copy(t, j, (x, y), cc, (*chip, cc), src=self.kinds[t].shard_half(w_refs[t], cc))
                 for t in range(len(self.ins)) for j, chip in enumerate(chips)]
        return copy, first, chips, (x, y, cc)

    def start(self, w_refs, out_refs, send_sems, recv_sems):
        for cp in self._copies(w_refs, out_refs, send_sems, recv_sems)[1]:
            cp.start()

    def finish(self, w_refs, out_refs, send_sems, recv_sems):
        copy, first, chips, (x, y, cc) = self._copies(w_refs, out_refs, send_sems, recv_sems)
        passed = []
        for t in range(len(self.ins)):
            for j, chip in enumerate(chips):
                copy(t, j, chip, cc, (x, y, cc)).wait_recv()
                passed.append(copy(t, 3 + j, chip, cc, (x, y, 1 - cc)))
                passed[-1].start()
        for t in range(len(self.ins)):
            for j, chip in enumerate(chips):
                copy(t, 3 + j, chip, 1 - cc, (x, y, cc)).wait_recv()
        for cp in first + passed:
            cp.wait_send()

    def results(self, outs):
        chip = 2 * lax.axis_index("x") + lax.axis_index("y")
        return [k.place(o, s, chip) for k, o, s in zip(self.kinds, outs, self.ins)]


class ExchangeRider:
    def __init__(self, pairs, kinds):
        self.ins, self.kinds = list(pairs), kinds
        self.out_shapes = [jax.ShapeDtypeStruct((N_CHIPS,) + k.slab_half, p.dtype) for k, p in zip(kinds, pairs)]
        self.n_sems = 3 * len(pairs)

    def start(self, p_refs, out_refs, send_sems, recv_sems):
        x, y, cc = _place()
        for t in range(len(self.ins)):
            for j, chip in enumerate([(1 - x, y), (x, 1 - y), (1 - x, 1 - y)]):
                pltpu.make_async_remote_copy(
                    src_ref=self.kinds[t].pair_slab(p_refs[t], 2 * chip[0] + chip[1]), dst_ref=out_refs[t].at[2 * x + y],
                    send_sem=send_sems.at[3 * t + j], recv_sem=recv_sems.at[3 * t + j], device_id=(*chip, cc),
                    device_id_type=MESH).start()

    def finish(self, p_refs, out_refs, send_sems, recv_sems):
        x, y, cc = _place()
        me = 2 * x + y
        for t in range(len(self.ins)):
            for j, chip in enumerate([(1 - x, y), (x, 1 - y), (1 - x, 1 - y)]):
                them = 2 * chip[0] + chip[1]
                pltpu.make_async_remote_copy(
                    src_ref=self.kinds[t].pair_slab(p_refs[t], them), dst_ref=out_refs[t].at[them],
                    send_sem=send_sems.at[3 * t + j], recv_sem=recv_sems.at[3 * t + j], device_id=(x, y, cc),
                    device_id_type=MESH).wait()

    def results(self, outs):
        return list(outs)


def run_rider(rider, name):
    n, no = len(rider.ins), len(rider.out_shapes)

    def body(*refs):
        parts = (refs[:n], refs[n:n + no], refs[n + no], refs[n + no + 1])
        rider.start(*parts)
        rider.finish(*parts)

    outs = pl.pallas_call(
        body, name=name, in_specs=[_ANY] * n, out_specs=[_ANY] * no, out_shape=rider.out_shapes,
        scratch_shapes=[pltpu.SemaphoreType.DMA((rider.n_sems,)), pltpu.SemaphoreType.DMA((rider.n_sems,))],
    )(*rider.ins)
    return rider.results(outs)


def allgather_devices(buf):
    r, c = buf.shape

    def body(b_ref, out_ref, send_sems, recv_sems, local_sem):
        x, y, cc = _place()
        me = 4 * x + 2 * y + cc
        mine = pltpu.make_async_copy(b_ref, out_ref.at[me], local_sem)
        mine.start()
        copies = []
        for k in range(1, N_DEV):
            px, py, pc = x ^ (k >> 2), y ^ ((k >> 1) & 1), cc ^ (k & 1)
            cp = pltpu.make_async_remote_copy(src_ref=b_ref, dst_ref=out_ref.at[me], send_sem=send_sems.at[k - 1],
                                              recv_sem=recv_sems.at[k - 1], device_id=(px, py, pc), device_id_type=MESH)
            cp.start()
            copies.append((cp, 4 * px + 2 * py + pc))
        for k, (cp, peer) in enumerate(copies):
            pltpu.make_async_remote_copy(src_ref=b_ref, dst_ref=out_ref.at[peer], send_sem=send_sems.at[k],
                                         recv_sem=recv_sems.at[k], device_id=(x, y, cc), device_id_type=MESH).wait_recv()
        for cp, _ in copies:
            cp.wait_send()
        mine.wait()

    return pl.pallas_call(
        body, name="allgather_devices", in_specs=[_ANY], out_specs=_ANY,
        out_shape=jax.ShapeDtypeStruct((N_DEV, r, c), buf.dtype),
        scratch_shapes=[pltpu.SemaphoreType.DMA((N_DEV - 1,)), pltpu.SemaphoreType.DMA((N_DEV - 1,)),
                        pltpu.SemaphoreType.DMA],
    )(buf)


def swap_halves_sibling(gs, kinds, name):
    n = len(gs)

    def body(*refs):
        g_refs, out_refs, send_sems, recv_sems = refs[:n], refs[n:2 * n], refs[2 * n], refs[2 * n + 1]
        x, y, cc = _place()
        cps = []
        for t in range(n):
            cps.append(pltpu.make_async_remote_copy(
                src_ref=kinds[t].grad_half(g_refs[t], 1 - cc), dst_ref=out_refs[t], send_sem=send_sems.at[t],
                recv_sem=recv_sems.at[t], device_id=(x, y, 1 - cc), device_id_type=MESH))
            cps[-1].start()
        for cp in cps:
            cp.wait()

    return pl.pallas_call(
        body, name=name, in_specs=[_ANY] * n, out_specs=[_ANY] * n,
        out_shape=[jax.ShapeDtypeStruct(k.got, g.dtype) for k, g in zip(kinds, gs)],
        scratch_shapes=[pltpu.SemaphoreType.DMA((n,)), pltpu.SemaphoreType.DMA((n,))],
    )(*gs)


def _row_tile(n, limit=512):
    return max(t for t in range(16, limit + 1, 16) if n % t == 0)


def sum_chips(got, own, kind, chip, name):
    def body(chip_ref, got_r, own_r, out_r):
        mine = own_r[...].astype(F32)
        acc = None
        for k in range(N_CHIPS):
            term = jnp.where(chip_ref[0] == k, mine, got_r[k].astype(F32))
            acc = term if acc is None else acc + term
        out_r[...] = acc

    if isinstance(kind, Packed):
        r, c = kind.slab_half
        tr = _row_tile(r)
        grid = (r // tr,)
        specs = [pl.BlockSpec((N_CHIPS, tr, c), lambda i, chip_ref: (0, i, 0)),
                 pl.BlockSpec((None, tr, c), lambda i, chip_ref: (chip_ref[0], i, 0))]
        out_spec = pl.BlockSpec((tr, c), lambda i, chip_ref: (i, 0))
    else:
        r, c = kind.slab_half
        tr = _row_tile(r, 256)
        grid = (r // tr,)
        specs = [pl.BlockSpec((N_CHIPS, tr, c), lambda i, chip_ref: (0, i, 0)),
                 pl.BlockSpec((tr, c), lambda i, chip_ref: (i, chip_ref[0]))]
        out_spec = pl.BlockSpec((tr, c), lambda i, chip_ref: (i, 0))
    return pl.pallas_call(
        body, name=name,
        grid_spec=pltpu.PrefetchScalarGridSpec(num_scalar_prefetch=1, grid=grid, in_specs=specs, out_specs=out_spec),
        out_shape=jax.ShapeDtypeStruct(kind.slab_half, F32),
        compiler_params=_cparams(("parallel",) * len(grid)),
    )(chip, got, own)


def join_halves_sibling(halves):
    n = len(halves)

    def body(*refs):
        h_refs, out_refs, send_sems, recv_sems = refs[:n], refs[n:2 * n], refs[2 * n], refs[2 * n + 1]
        x, y, cc = _place()
        cps = []
        for t in range(n):
            cps.append(pltpu.make_async_remote_copy(
                src_ref=h_refs[t], dst_ref=out_refs[t].at[cc], send_sem=send_sems.at[t], recv_sem=recv_sems.at[t],
                device_id=(x, y, 1 - cc), device_id_type=MESH))
            cps[-1].start()
        for t in range(n):
            pltpu.make_async_remote_copy(
                src_ref=h_refs[t], dst_ref=out_refs[t].at[1 - cc], send_sem=send_sems.at[t], recv_sem=recv_sems.at[t],
                device_id=(x, y, cc), device_id_type=MESH).wait_recv()
        for cp in cps:
            cp.wait_send()

    outs = pl.pallas_call(
        body, name="join_halves_sibling", in_specs=[_ANY] * n, out_specs=[_ANY] * n,
        out_shape=[jax.ShapeDtypeStruct((2,) + h.shape, h.dtype) for h in halves],
        scratch_shapes=[pltpu.SemaphoreType.DMA((n,)), pltpu.SemaphoreType.DMA((n,))],
    )(*halves)
    core = lax.axis_index("c")
    return [lax.dynamic_update_slice_in_dim(o, h[None], core, 0) for o, h in zip(outs, halves)]


def add_own_half(g, got, kind, core, out_dtype, name):
    def body(c_ref, g_r, o_r, out_r):
        out_r[...] = (g_r[...] + o_r[...]).astype(out_r.dtype)

    if isinstance(kind, Packed):
        r, c = kind.slab_half
        tr = _row_tile(r)
        grid = (N_CHIPS, r // tr)
        specs = [pl.BlockSpec((None, None, tr, c), lambda i, j, c_ref: (i, c_ref[0], j, 0)),
                 pl.BlockSpec((None, tr, c), lambda i, j, c_ref: (i, j, 0))]
        out_spec = pl.BlockSpec((None, tr, c), lambda i, j, c_ref: (i, j, 0))
    else:
        h, c4 = kind.got
        tr = _row_tile(h, 128)
        grid = (1, h // tr)
        specs = [pl.BlockSpec((tr, c4), lambda i, j, c_ref: (c_ref[0] * (h // tr) + j, 0)),
                 pl.BlockSpec((tr, c4), lambda i, j, c_ref: (j, 0))]
        out_spec = pl.BlockSpec((tr, c4), lambda i, j, c_ref: (j, 0))
    return pl.pallas_call(
        body, name=name,
        grid_spec=pltpu.PrefetchScalarGridSpec(num_scalar_prefetch=1, grid=grid, in_specs=specs, out_specs=out_spec),
        out_shape=jax.ShapeDtypeStruct(kind.got, out_dtype),
        compiler_params=_cparams(("parallel", "parallel")),
    )(core, g, got)


def sum_slabs(p, name):
    n, r, c = p.shape
    tr = _tile(r, [t for t in (512, 256, 128, 64, 32, 16) if n * t * c * p.dtype.itemsize <= (1 << 23)])

    def body(p_r, out_r):
        acc = p_r[0].astype(F32)
        for k in range(1, n):
            acc = acc + p_r[k].astype(F32)
        out_r[...] = acc

    return pl.pallas_call(
        body, name=name, grid=(r // tr,), in_specs=[pl.BlockSpec((n, tr, c), lambda i: (0, i, 0))],
        out_specs=pl.BlockSpec((tr, c), lambda i: (i, 0)), out_shape=jax.ShapeDtypeStruct((r, c), F32),
        compiler_params=_cparams(("parallel",)),
    )(p)


def _lay(arr, axis, pieces, total, reps=()):
    items = [(d, n, lax.slice_in_dim(arr, s0, s0 + n, axis=axis)) for s0, n, d in pieces]
    items += [(d, n, jnp.repeat(lax.slice_in_dim(arr, s0, s0 + 1, axis=axis), n, axis=axis)) for s0, d, n in reps]
    items.sort(key=lambda t: t[0])
    parts, pos = [], 0

    def zeros(n):
        sh = list(arr.shape)
        sh[axis] = n
        return jnp.zeros(sh, arr.dtype)

    for d, n, v in items:
        if d > pos:
            parts.append(zeros(d - pos))
        parts.append(v)
        pos = d + n
    if total > pos:
        parts.append(zeros(total - pos))
    return jnp.concatenate(parts, axis=axis) if len(parts) > 1 else parts[0]


def _unlay_parts(g, axis, pieces, reps=()):
    out = [(s0, lax.slice_in_dim(g, d, d + n, axis=axis)) for s0, n, d in pieces]
    out += [(s0, jnp.sum(lax.slice_in_dim(g, d, d + n, axis=axis), axis=axis, keepdims=True)) for s0, d, n in reps]
    return out


def _join(parts, axis):
    parts = sorted(parts, key=lambda t: t[0])
    return jnp.concatenate([p for _, p in parts], axis=axis)


def _heads(src0, n_heads, width, padded, dst0=0):
    return [(src0 + h * width, width, dst0 + h * padded) for h in range(n_heads)]


_XQ = lambda src0: _heads(src0, XA_HEADS, XA_HD, LANE)
_XA_W = XA_HEADS * LANE

LAYOUT = {
    'a': dict(
        segs=dict(q=(_heads(0, 4, 96, LANE), 512, ()), k=(_heads(384, 4, 96, LANE), 512, ()),
                  v=(_heads(768, 4, 192, 256), 1024, ()), glr=([(1536, 16, 0)], LANE, ()),
                  og=(_heads(1552, 4, 192, 256), 1024, ()), xq=(_XQ(2320), _XA_W, ())),
        tok=(_heads(0, 4, 192, 256), 1024), xa=(_XQ(768), _XA_W)),
    'b': dict(
        segs=dict(q=([(0, 1536, 0)], 1536, ()), k=([(1536, 1536, 0)], 1536, ()), v=([(3072, 1536, 0)], 1536, ()),
                  xq=(_XQ(4608), _XA_W, ())),
        tok=([(0, 512, 0)], 512), xa=(_XQ(512), _XA_W)),
    'c': dict(
        segs=dict(z=(_heads(0, 12, 64, LANE), 1536, ()),
                  xbc=(_heads(768, 12, 64, LANE) + [(1536, 256, 1536), (1792, 256, 1792)], 2048, ()),
                  dt=([(2048, 12, 0)], LANE, ()),
                  xq=(_XQ(2060), _XA_W, ())),
        tok=(_heads(0, 12, 64, LANE), 1536), xa=(_XQ(768), _XA_W)),
    'd': dict(
        segs=dict(q=([(0, 768, 0)], 768, ()), f=([(768, 768, 0)], 768, ()), i=([(1536, 768, 0)], 768, ()),
                  og=([(2304, 768, 0)], 768, ()), xq=(_XQ(3072), _XA_W, ())),
        tok=([(0, 768, 0)], 768), xa=(_XQ(768), _XA_W)),
}
KINDS = 'abcd'
_XS_PIECES = _heads(0, 12, 64, LANE)
_XBC_PIECES = _XS_PIECES + [(768, 256, 1536), (1024, 256, 1792)]
_HEAD_REPS = tuple((h, h * LANE, LANE) for h in range(12))


def _row(v):
    return v.reshape(1, -1)


LAYER_WEIGHTS = [
    {'w_in': (f'{k}_w_in', None), 'w_out': (f'{k}_w_out', None), 'w_kv': ('xa_w_kv', i), 'w_up': ('ffn_w_up', i),
     'w_down': ('ffn_w_down', i), **({'w_gate2': ('a_w_gate2', None)} if k == 'a' else {})}
    for i, k in enumerate('abcd')]


class LocalLayers:
    def __init__(self, W):
        self.W, self.g = W, {}

    def weights(self, i):
        return {key: (self.W[n] if l is None else self.W[n][l]).astype(MXU_DTYPE)
                for key, (n, l) in LAYER_WEIGHTS[i].items()}

    def fwd_rider(self, i):
        return None

    def bwd_rider(self, i):
        return None

    def grads(self, i, g):
        self.g[i] = g

    def whole_grads(self):
        out = {}
        for i in range(4):
            for key, (n, l) in LAYER_WEIGHTS[i].items():
                if l is None:
                    out[n] = self.g[i][key]
        for n in ('xa_w_kv', 'ffn_w_up', 'ffn_w_down'):
            key = [k for k, (m, _) in LAYER_WEIGHTS[0].items() if m == n][0]
            out[n] = jnp.stack([self.g[i][key] for i in range(4)])
        return out


class ShardedLayers:
    def __init__(self, w, core_id):
        self.core_id = core_id
        self.names, self.axes, self.shards, self.packed, self.kinds = [], [], [], [], []
        for lw in LAYER_WEIGHTS:
            keys = [k for k in lw if k not in ('w_up', 'w_down')]
            sh = {k: (w[lw[k][0]] if lw[k][1] is None else w[lw[k][0]][lw[k][1]]).astype(MXU_DTYPE) for k in lw}
            ax = {k: SHARD_AXIS[lw[k][0]] - (lw[k][1] is not None) for k in lw}
            pk = _pack([sh[k] for k in keys], MXU_DTYPE, 256)
            self.names.append(keys)
            self.axes.append(ax)
            self.shards.append(sh)
            self.packed.append(pk)
            self.kinds.append([Packed(pk.shape), SlabCols(sh['w_up'].shape), Packed(sh['w_down'].shape)])
        self.whole = {}
        self.pending = None
        self.recvd = {}

    def _operands(self, i):
        return [self.packed[i], self.shards[i]['w_up'], self.shards[i]['w_down']]

    def _gathered(self, i, res):
        per_chip = [_unpack(res[0][j], [self.shards[i][k].shape for k in self.names[i]]) for j in range(N_CHIPS)]
        out = {k: _merge_chips(jnp.stack([per_chip[j][n] for j in range(N_CHIPS)]), self.axes[i][k])
               for n, k in enumerate(self.names[i])}
        out['w_up'], out['w_down'] = res[1], res[2].reshape(-1, res[2].shape[-1])
        self.whole[i] = out

    def first_gather(self):
        self._gathered(0, run_rider(GatherRider(self._operands(0), self.kinds[0]), "allgather_chips"))

    def weights(self, i):
        return self.whole[i]

    def fwd_rider(self, i):
        return GatherRider(self._operands(i + 1), self.kinds[i + 1]) if i + 1 < 4 else None

    def fwd_rode(self, i, res):
        self._gathered(i + 1, res)

    def bwd_rider(self, i):
        return ExchangeRider(self.pending[1], self.kinds[self.pending[0]]) if self.pending is not None else None

    def bwd_rode(self, i, res):
        self.recvd[self.pending[0]] = (res, self.pending[1])
        self.pending = None

    def grads(self, i, g):
        kinds = self.kinds[i]
        gb = jnp.stack([_pack([_split_chips(g[k], self.axes[i][k])[j] for k in self.names[i]], F32, 256)
                        for j in range(N_CHIPS)])
        gs = [gb.reshape(N_CHIPS, 2, kinds[0].h, kinds[0].c), g['w_up'],
              g['w_down'].reshape(N_CHIPS, 2, kinds[2].h, kinds[2].c)]
        gots = swap_halves_sibling(gs, kinds, f"swap_halves_{i}")
        self.pending = (i, [add_own_half(a, o, k, self.core_id, GRAD_WIRE_DTYPE, f"add_own_half_{i}_{t}")
                            for t, (a, o, k) in enumerate(zip(gs, gots, kinds))])

    def finish(self, chip_id):
        last, pairs = self.pending
        self.recvd[last] = (run_rider(ExchangeRider(pairs, self.kinds[last]), "exchange_chips"), pairs)
        halves = []
        for i in range(4):
            got, pairs = self.recvd[i]
            halves += [sum_chips(r, p, k, chip_id, f"sum_chips_{i}_{t}")
                       for t, (r, p, k) in enumerate(zip(got, pairs, self.kinds[i]))]
        joined = join_halves_sibling(halves)
        out, stacked = {}, {'xa_w_kv': [], 'ffn_w_up': [], 'ffn_w_down': []}
        for i, lw in enumerate(LAYER_WEIGHTS):
            red, up, down = joined[3 * i:3 * i + 3]
            parts = _unpack(red.reshape(-1, PACK_COLS), [self.shards[i][k].shape for k in self.names[i]])
            parts = dict(zip(self.names[i], parts), w_up=up.reshape(self.shards[i]['w_up'].shape),
                         w_down=down.reshape(self.shards[i]['w_down'].shape))
            for k, (n, l) in lw.items():
                if l is None:
                    out[n] = parts[k]
                else:
                    stacked[n].append(parts[k])
        out.update({n: jnp.stack(v) for n, v in stacked.items()})
        return out


def local_step(x, mem, positions, target, W, layers=None):
    s = x.shape[0]
    grads = {}
    scan_block = CHUNK * SCAN_CHUNKS
    ffn = layers or LocalLayers(W)

    inv_freq = ROPE_THETA ** (-jnp.arange(DIL_HD // 2, dtype=F32) / (DIL_HD // 2))
    ang = positions.astype(F32)[:, None] * inv_freq
    cosf = jnp.concatenate([jnp.cos(ang), jnp.cos(ang)], axis=-1)
    sinf = jnp.concatenate([-jnp.sin(ang), jnp.sin(ang)], axis=-1)

    mem_g = _row(W['mem_norm'])
    (mem_n,) = tmap("mem_norm", _norm_stage, [mem], [mem_g], [(D_MODEL, MXU_DTYPE)])
    kv_lay = _heads(0, 4, 64, LANE) + _heads(256, 4, 64, LANE, dst0=_XA_W)

    saved = []
    for i in range(4):
        kind = KINDS[i]
        lay = LAYOUT[kind]
        sv = dict(x0=x)
        wl = ffn.weights(i)
        w_in, w_out = wl['w_in'], wl['w_out']
        sv['w_seg'] = {n: _lay(w_in, 1, p, t, r).astype(MXU_DTYPE) for n, (p, t, r) in lay['segs'].items()}
        sv['wo_tok'] = _lay(w_out, 0, *lay['tok']).astype(MXU_DTYPE)
        sv['wo_xa'] = _lay(w_out, 0, *lay['xa']).astype(MXU_DTYPE)
        sv['w_kv'] = _lay(wl['w_kv'], 1, kv_lay, 2 * _XA_W).astype(MXU_DTYPE)
        sv['g1'] = _row(W['mix_norm'][i])
        (h,) = tmap(f"mix_norm_{i}", _norm_stage, [x], [sv['g1']], [(D_MODEL, MXU_DTYPE)])
        sv['h'] = h
        seg = {n: matmul(h, w) for n, w in sv['w_seg'].items()}
        sv['seg'] = seg

        if kind == 'a':
            sv['w2'] = _lay(_lay(wl['w_gate2'], 1, _heads(0, 4, 96, LANE), 512), 0, [(0, 16, 0)], LANE)
            sv['bg'] = _row(_lay(W['a_b_gate'], 0, _heads(0, 4, 96, LANE), 512))
            sv['on'] = _row(_lay(W['a_o_norm'], 0, [(0, 192, 0)], 256))
            (la,) = tmap("gla_pre", _gla_pre, [seg['glr']], [sv['w2'], sv['bg']], [(512, F32)])
            sv['la'] = la
            sv['scan_fn'] = _gla_step(GLA_HEADS, 2 * LANE, GLA_DK ** -0.5)
            sv['scan_rows'] = [seg['q'], seg['k'], seg['v'], la]
            (o,), sv['states'] = rscan("gla_scan", sv['scan_fn'], [(LANE, 2 * LANE)] * GLA_HEADS, sv['scan_rows'], [],
                                       [(1024, F32)], scan_block)
            sv['o'] = o
            (tok,) = tmap("gla_post", _gla_post, [o, seg['og']], [sv['on']], [(1024, MXU_DTYPE)])
        elif kind == 'b':
            sv['qg'], sv['kg'] = _row(W['b_q_norm']), _row(W['b_k_norm'])
            os_, ls_ = [], []
            qkn = tmap("dil_pre", _dil_pre, [seg['q'], seg['k'], cosf, sinf], [sv['qg'], sv['kg']], [(512, F32)] * 6)
            sv['qn'], sv['kn'] = qkn[:3], qkn[3:]
            for g, (window, r) in enumerate(DIL_GROUPS):
                assert window // r == DIL_BLOCK and (s // r) % DIL_BLOCK == 0
                o, lse = dil_attn(f"dil_attn_{g}", sv['qn'][g], sv['kn'][g], seg['v'], r, g)
                os_.append(o)
                ls_.append(lse)
            sv['os'], sv['ls'] = os_, ls_
            (tok,) = tmap("dil_merge", _dil_merge, os_ + ls_, [], [(512, MXU_DTYPE)])
        elif kind == 'c':
            sv['cw'] = _lay(W['c_conv_w'], 1, _XBC_PIECES, 2048)
            sv['cb'] = _row(_lay(W['c_conv_b'], 0, _XBC_PIECES, 2048))
            sv['dtb'] = _row(_lay(W['c_dt_bias'], 0, [], 1536, _HEAD_REPS))
            sv['alog'] = _row(_lay(W['c_a_log'], 0, [], 1536, _HEAD_REPS))
            sv['dsk'] = _row(_lay(W['c_d'], 0, [], 1536, _HEAD_REPS))
            sv['cn'] = _row(_lay(W['c_norm'], 0, _XS_PIECES, 1536))
            xact = conv_fwd("ssm_conv", seg['xbc'], sv['cw'], sv['cb'], 'silu', F32, 512)
            sv['xact'] = xact
            sv['scan_rows'] = [xact, seg['dt']]
            sv['scan_params'] = [sv['dtb'], sv['alog'], sv['dsk']]
            (yv,), sv['states'] = rscan("ssd_scan", _ssd_step, [(LANE, LANE)] * SSM_HEADS, sv['scan_rows'],
                                        sv['scan_params'], [(1536, F32)], scan_block)
            sv['y'] = yv
            (tok,) = tmap("ssd_post", _mamba_post, [yv, seg['z']], [sv['cn']], [(1536, MXU_DTYPE)])
        else:
            sv['lbnd'] = W['d_lower_bounds']
            sv['on'] = _row(W['d_o_norm'])
            qq, kk, la = tmap("hgrn_pre", _hgrn_pre, [seg['q'], seg['f']], [sv['lbnd']], [(768, F32)] * 3)
            sv['scan_fn'] = _gla_step(HGRN_HEADS, LANE, 1.0)
            sv['scan_rows'] = [qq, kk, seg['i'], la]
            (o,), sv['states'] = rscan("hgrn_scan", sv['scan_fn'], [(LANE, LANE)] * HGRN_HEADS, sv['scan_rows'], [],
                                       [(768, F32)], scan_block)
            sv['o'] = o
            (tok,) = tmap("hgrn_post", _hgrn_post, [o, seg['og']], [sv['on']], [(768, MXU_DTYPE)])
        sv['tok'] = tok

        kv = matmul(mem_n, sv['w_kv'])
        sv['kv'] = kv
        sv['xqg'] = _row(_lay(W['xa_q_norm'][i], 0, [(0, 64, 0)], LANE))
        sv['xkg'] = _row(_lay(W['xa_k_norm'][i], 0, [(0, 64, 0)], LANE))
        (xa,) = tmap(f"xattn_{i}", _xattn, [seg['xq']], [kv, sv['xqg'], sv['xkg']], [(_XA_W, MXU_DTYPE)])
        sv['xa'] = xa
        x = matmul(tok, sv['wo_tok'], add=x)
        x = matmul(xa, sv['wo_xa'], add=x)
        sv['x1'] = x

        sv['g2'] = _row(W['ffn_norm'][i])
        sv['fcw'] = W['ffn_conv_w'][i]
        sv['fcb'] = _row(W['ffn_conv_b'][i])
        (h2,) = tmap(f"ffn_norm_{i}", _norm_stage, [x], [sv['g2']], [(D_MODEL, MXU_DTYPE)])
        sv['h2'] = h2
        w_up, w_down = wl['w_up'], wl['w_down']
        sv['w_up'], sv['w_down'] = w_up, w_down
        u0 = matmul(h2, w_up)
        sv['u0'] = u0
        rider = ffn.fwd_rider(i)
        act = conv_fwd("ffn_conv", u0, sv['fcw'], sv['fcb'], 'glu', MXU_DTYPE, 1408, rider=rider)
        if rider is not None:
            act, rode = act
            ffn.fwd_rode(i, rode)
        sv['act'] = act
        x = matmul(act, w_down, add=x)
        saved.append(sv)

    loss_acc, dx = loss_head(x, target)

    g_stack = {n: [None] * 4 for n in ('mix_norm', 'xa_q_norm', 'xa_k_norm', 'ffn_norm', 'ffn_conv_w', 'ffn_conv_b')}
    d_memn = None
    for i in reversed(range(4)):
        kind = KINDS[i]
        lay = LAYOUT[kind]
        sv = saved[i]
        seg = sv['seg']
        w_up, w_down = sv['w_up'], sv['w_down']
        gl = {}
        dact = matmul(dx, w_down, tb=True)
        gl['w_down'] = matmul(sv['act'], dx, ta=True)
        rider = ffn.bwd_rider(i)
        (du_g, du_v), dcw, dcb, rode = conv_bwd("ffn_conv_bwd", sv['u0'], sv['fcw'], sv['fcb'], dact, 'glu', 1408,
                                                rider=rider)
        if rider is not None:
            ffn.bwd_rode(i, rode)
        g_stack['ffn_conv_w'][i], g_stack['ffn_conv_b'][i] = dcw, dcb[0]
        dh2 = matmul(du_v, w_up, tb=True, b_koff=D_FF, add=matmul(du_g, w_up, tb=True))
        g_up = jnp.zeros((1,) + w_up.shape, F32)
        g_up = matmul(sv['h2'], du_g, ta=True, into=(g_up, 0, 0))
        g_up = matmul(sv['h2'], du_v, ta=True, into=(g_up, 0, D_FF))
        gl['w_up'] = g_up[0]
        (dx,), (dg2,) = tmap_bwd(f"ffn_norm_bwd_{i}", _norm_stage, [sv['x1']], [sv['g2']], [dh2], [True], {0: dx})
        g_stack['ffn_norm'][i] = dg2[0]
        dtok = matmul(dx, sv['wo_tok'], tb=True)
        dxa = matmul(dx, sv['wo_xa'], tb=True)
        g_wo = _unlay_parts(matmul(sv['tok'], dx, ta=True), 0, lay['tok'][0]) \
            + _unlay_parts(matmul(sv['xa'], dx, ta=True), 0, lay['xa'][0])
        gl['w_out'] = _join(g_wo, 0)
        (dxq,), (dkv, dqg, dkg) = tmap_bwd(f"xattn_bwd_{i}", _xattn, [seg['xq']], [sv['kv'], sv['xqg'], sv['xkg']],
                                           [dxa], [True])
        g_stack['xa_q_norm'][i], g_stack['xa_k_norm'][i] = dqg[0, :XA_HD], dkg[0, :XA_HD]
        gl['w_kv'] = _join(_unlay_parts(matmul(mem_n, dkv, ta=True), 1, kv_lay), 1)
        d_memn = matmul(dkv, sv['w_kv'], tb=True, add=d_memn)
        dseg = dict(xq=dxq)
        if kind == 'a':
            (do, dog), (don,) = tmap_bwd("gla_post_bwd", _gla_post, [sv['o'], seg['og']], [sv['on']], [dtok],
                                         [True, True], grad_dtype=F32)
            grads['a_o_norm'] = don[0, :GLA_DV]
            (dq, dk, dv, dla), _ = rscan_bwd("gla_scan_bwd", sv['scan_fn'], sv['states'], sv['scan_rows'], [], [do],
                                             scan_block, grad_dtype=F32)
            (dglr,), (dw2, dbg) = tmap_bwd("gla_pre_bwd", _gla_pre, [seg['glr']], [sv['w2'], sv['bg']], [dla], [True])
            gl['w_gate2'] = _join(_unlay_parts(dw2[:GLA_RANK], 1, _heads(0, 4, 96, LANE)), 1)
            grads['a_b_gate'] = _join(_unlay_parts(dbg[0], 0, _heads(0, 4, 96, LANE)), 0)
            dseg.update(q=dq, k=dk, v=dv, glr=dglr, og=dog)
        elif kind == 'b':
            res, _ = tmap_bwd("dil_merge_bwd", _dil_merge, sv['os'] + sv['ls'], [], [dtok], [True] * 6, grad_dtype=F32)
            dqn, dkn, dvs = [], [], []
            for g, (_, r) in enumerate(DIL_GROUPS):
                a_, b_, c_ = dil_attn_bwd(f"dil_attn_bwd_{g}", sv['qn'][g], sv['kn'][g], seg['v'], res[g], res[3 + g],
                                          r, g)
                dqn.append(a_)
                dkn.append(b_)
                dvs.append(c_)
            (dq, dk), (dqg, dkg) = tmap_bwd("dil_pre_bwd", _dil_pre, [seg['q'], seg['k'], cosf, sinf],
                                            [sv['qg'], sv['kg']], dqn + dkn, [True, True, False, False])
            dseg.update(q=dq, k=dk, v=jnp.concatenate(dvs, axis=1))
            grads['b_q_norm'], grads['b_k_norm'] = dqg[0], dkg[0]
        elif kind == 'c':
            (dy, dz), (dcn,) = tmap_bwd("ssd_post_bwd", _mamba_post, [sv['y'], seg['z']], [sv['cn']], [dtok],
                                        [True, True], grad_dtype=F32)
            grads['c_norm'] = _join(_unlay_parts(dcn[0], 0, _XS_PIECES), 0)
            (dxact, ddt), (ddtb, dalog, ddsk) = rscan_bwd("ssd_scan_bwd", _ssd_step, sv['states'], sv['scan_rows'],
                                                          sv['scan_params'], [dy], scan_block, grad_dtype=F32)
            for nm, gv in (('c_dt_bias', ddtb), ('c_a_log', dalog), ('c_d', ddsk)):
                grads[nm] = _join(_unlay_parts(gv[0], 0, [], _HEAD_REPS), 0)
            (dxbc,), dcw, dcb, _ = conv_bwd("ssm_conv_bwd", seg['xbc'], sv['cw'], sv['cb'], dxact, 'silu', 512)
            grads['c_conv_w'] = _join(_unlay_parts(dcw, 1, _XBC_PIECES), 1)
            grads['c_conv_b'] = _join(_unlay_parts(dcb[0], 0, _XBC_PIECES), 0)
            dseg.update(z=dz, xbc=dxbc, dt=ddt)
        else:
            (do, dog), (don,) = tmap_bwd("hgrn_post_bwd", _hgrn_post, [sv['o'], seg['og']], [sv['on']], [dtok],
                                         [True, True], grad_dtype=F32)
            grads['d_o_norm'] = don[0]
            (dqq, dkk, di, dla), _ = rscan_bwd("hgrn_scan_bwd", sv['scan_fn'], sv['states'], sv['scan_rows'], [], [do],
                                               scan_block, grad_dtype=F32)
            (dq, df), (dlb,) = tmap_bwd("hgrn_pre_bwd", _hgrn_pre, [seg['q'], seg['f']], [sv['lbnd']], [dqq, dkk, dla],
                                        [True, True])
            grads['d_lower_bounds'] = dlb
            dseg.update(q=dq, f=df, i=di, og=dog)
        names = list(lay['segs'])
        dh = matmul_nt_sum([dseg[n] for n in names], [sv['w_seg'][n] for n in names])
        g_in = []
        for n, (p, t, rp) in lay['segs'].items():
            g_in += _unlay_parts(matmul(sv['h'], dseg[n], ta=True), 1, p, rp)
        gl['w_in'] = _join(g_in, 1)
        ffn.grads(i, gl)
        (dx,), (dg1,) = tmap_bwd(f"mix_norm_bwd_{i}", _norm_stage, [sv['x0']], [sv['g1']], [dh], [True], {0: dx})
        g_stack['mix_norm'][i] = dg1[0]

    _, (dmg,) = tmap_bwd("mem_norm_bwd", _norm_stage, [mem], [mem_g], [d_memn], [False])
    grads['mem_norm'] = dmg[0]
    for n, parts in g_stack.items():
        grads[n] = jnp.stack(parts)
    if isinstance(ffn, LocalLayers):
        grads.update(ffn.whole_grads())
    return loss_acc, dx, grads


def _pack(arrs, dtype, row_multiple=PACK_ROWS):
    parts, rows = [], 0
    for a in arrs:
        f = a.reshape(-1).astype(dtype)
        unit = PACK_ROWS * PACK_COLS
        pad = (-f.shape[0]) % unit
        if pad:
            f = jnp.concatenate([f, jnp.zeros((pad,), dtype)])
        parts.append(f.reshape(-1, PACK_COLS))
        rows += parts[-1].shape[0]
    if rows % row_multiple:
        parts.append(jnp.zeros((row_multiple - rows % row_multiple, PACK_COLS), dtype))
    return jnp.concatenate(parts, axis=0)


def _unpack(buf, shapes):
    out, row = [], 0
    for sh in shapes:
        n = int(np.prod(sh))
        rows = -(-n // (PACK_ROWS * PACK_COLS)) * PACK_ROWS
        out.append(buf[row:row + rows].reshape(-1)[:n].reshape(sh))
        row += rows
    return out


def _pack_rows(arrs):
    parts = []
    for a in arrs:
        f = a.reshape(-1).astype(F32)
        parts.append(jnp.pad(f, (0, (-f.shape[0]) % PACK_COLS)))
    flat = jnp.concatenate(parts)
    rows = flat.shape[0] // PACK_COLS
    return jnp.pad(flat, (0, (-rows % 16) * PACK_COLS)).reshape(-1, PACK_COLS)


def _unpack_rows(buf, shapes):
    flat, out, pos = buf.reshape(-1), [], 0
    for sh in shapes:
        n = int(np.prod(sh))
        out.append(flat[pos:pos + n].reshape(sh))
        pos += -(-n // PACK_COLS) * PACK_COLS
    return out


def _split_chips(a, axis):
    sh = a.shape
    return jnp.moveaxis(a.reshape(sh[:axis] + (N_CHIPS, sh[axis] // N_CHIPS) + sh[axis + 1:]), axis, 0)


def _merge_chips(a, axis):
    a = jnp.moveaxis(a, 0, axis)
    sh = a.shape
    return a.reshape(sh[:axis] + (sh[axis] * sh[axis + 1],) + sh[axis + 2:])


def kernel(x, mem, positions, mem_norm, mix_norm, xa_w_kv, xa_q_norm, xa_k_norm, ffn_norm, ffn_w_up, ffn_conv_w, ffn_conv_b, ffn_w_down, a_w_in, a_w_gate2, a_b_gate, a_o_norm, a_w_out, b_w_in, b_q_norm, b_k_norm, b_w_out, c_w_in, c_conv_w, c_conv_b, c_dt_bias, c_a_log, c_d, c_norm, c_w_out, d_w_in, d_lower_bounds, d_o_norm, d_w_out, loss_target, m_mem_norm, m_mix_norm, m_xa_w_kv, m_xa_q_norm, m_xa_k_norm, m_ffn_norm, m_ffn_w_up, m_ffn_conv_w, m_ffn_conv_b, m_ffn_w_down, m_a_w_in, m_a_w_gate2, m_a_b_gate, m_a_o_norm, m_a_w_out, m_b_w_in, m_b_q_norm, m_b_k_norm, m_b_w_out, m_c_w_in, m_c_conv_w, m_c_conv_b, m_c_dt_bias, m_c_a_log, m_c_d, m_c_norm, m_c_w_out, m_d_w_in, m_d_lower_bounds, m_d_o_norm, m_d_w_out, v_mem_norm, v_mix_norm, v_xa_w_kv, v_xa_q_norm, v_xa_k_norm, v_ffn_norm, v_ffn_w_up, v_ffn_conv_w, v_ffn_conv_b, v_ffn_w_down, v_a_w_in, v_a_w_gate2, v_a_b_gate, v_a_o_norm, v_a_w_out, v_b_w_in, v_b_q_norm, v_b_k_norm, v_b_w_out, v_c_w_in, v_c_conv_w, v_c_conv_b, v_c_dt_bias, v_c_a_log, v_c_d, v_c_norm, v_c_w_out, v_d_w_in, v_d_lower_bounds, v_d_o_norm, v_d_w_out):
    args = locals()
    w = {n: args[n] for n in WEIGHTS}
    m = {n: args['m_' + n] for n in WEIGHTS}
    v = {n: args['v_' + n] for n in WEIGHTS}
    cx, cy, cc = lax.axis_index("x"), lax.axis_index("y"), lax.axis_index("c")
    chip = 2 * cx + cy

    core_id, chip_id = cc.reshape(1).astype(jnp.int32), chip.reshape(1).astype(jnp.int32)
    layers = ShardedLayers(w, core_id)
    layers.first_gather()
    full = {}
    small_sharded = [n for n in SMALL if n in SHARD_AXIS]
    sg = allgather_devices(_pack([w[n] for n in small_sharded], F32))
    per_chip_s = [_unpack(sg[2 * j], [w[n].shape for n in small_sharded]) for j in range(N_CHIPS)]
    for k, n in enumerate(small_sharded):
        full[n] = _merge_chips(jnp.stack([per_chip_s[j][k] for j in range(N_CHIPS)]), SHARD_AXIS[n])
    for n in SMALL:
        if n not in SHARD_AXIS:
            full[n] = w[n]

    loss_acc, dx, grads = local_step(x[0], mem[0], positions[0], loss_target[0], full, layers)
    loss = lax.psum(jnp.sum(loss_acc), ("x", "y", "c"))

    g_big = layers.finish(chip_id)

    small_full_shapes = [grads[n].shape for n in SMALL]
    gs = sum_slabs(allgather_devices(_pack_rows([grads[n] for n in SMALL])), "sum_devices")
    g_small = {}
    for n, gfull in zip(SMALL, _unpack_rows(gs, small_full_shapes)):
        if n in SHARD_AXIS:
            ax = SHARD_AXIS[n]
            size = gfull.shape[ax] // N_CHIPS
            gfull = lax.dynamic_slice_in_dim(gfull, chip * size, size, axis=ax)
        g_small[n] = gfull

    g_out, delta, new_m, new_v = {**g_big, **g_small}, {}, {}, {}
    for n in WEIGHTS:
        sh = w[n].shape
        two_d = (-1, sh[-1])
        d_, m_, v_ = adamw(f"adamw_{n}", w[n].reshape(two_d), g_out[n].reshape(two_d), m[n].reshape(two_d),
                           v[n].reshape(two_d))
        delta[n], new_m[n], new_v[n] = d_.reshape(sh), m_.reshape(sh), v_.reshape(sh)

    return (loss, dx[None], *[g_out[n] for n in WEIGHTS], *[delta[n] for n in WEIGHTS],
            *[new_m[n] for n in WEIGHTS], *[new_v[n] for n in WEIGHTS])
```

```python
import functools
import math

import jax
import jax.numpy as jnp
import numpy as np
from jax import lax
from jax.experimental import pallas as pl
from jax.experimental.pallas import tpu as pltpu

F32 = jnp.float32
MXU_DTYPE = jnp.bfloat16
GRAD_WIRE_DTYPE = jnp.bfloat16
VMEM_LIMIT_V7X = 56 * 1024 * 1024
LANE = 128
SUBLANE = 8

D_MODEL = 1024
N_MEM = 256
EPS = 1e-6
ROPE_THETA = 10000.0
CHUNK = 64
XA_HEADS, XA_HD = 4, 64
GLA_HEADS, GLA_DK, GLA_DV, GLA_RANK, GLA_GATE_NORM = 4, 96, 192, 16, 16.0
DIL_GROUPS = ((128, 1), (512, 4), (2048, 16))
DIL_HEADS, DIL_HD, DIL_BLOCK = 4, 128, 128
SSM_HD, SSM_HEADS, SSM_GROUPS, SSM_STATE, SSM_CONV = 64, 12, 2, 128, 4
HGRN_HEADS, HGRN_DK = 6, 128
D_FF = 2816
FFN_CONV = 3
ADAM_LR, ADAM_B1, ADAM_B2, ADAM_EPS, ADAM_WD, ADAM_STEP = 0.001, 0.9, 0.999, 1e-08, 0.01, 10

MM_TILES = (2816, 1408, 1024, 768, 512, 384, 256, 128)
MM_K_TILES = (2816, 2048, 1536, 1408, 1024, 768, 512, 384, 256, 128)
MM_MIN_OUT_TILE = 512 * 1024
MM_VMEM_BUDGET = 40 * 1024 * 1024
ROW_BLOCK = 256
SCAN_CHUNKS = 2
PACK_COLS = 1024
PACK_ROWS = 32

WEIGHTS = ['mem_norm', 'mix_norm', 'xa_w_kv', 'xa_q_norm', 'xa_k_norm', 'ffn_norm', 'ffn_w_up', 'ffn_conv_w',
           'ffn_conv_b', 'ffn_w_down', 'a_w_in', 'a_w_gate2', 'a_b_gate', 'a_o_norm', 'a_w_out', 'b_w_in', 'b_q_norm',
           'b_k_norm', 'b_w_out', 'c_w_in', 'c_conv_w', 'c_conv_b', 'c_dt_bias', 'c_a_log', 'c_d', 'c_norm', 'c_w_out',
           'd_w_in', 'd_lower_bounds', 'd_o_norm', 'd_w_out']
SHARD_AXIS = {'xa_w_kv': 1, 'ffn_w_up': 2, 'ffn_conv_w': 2, 'ffn_w_down': 1, 'a_w_in': 1, 'a_w_gate2': 1, 'a_w_out': 0,
              'b_w_in': 1, 'b_w_out': 1, 'c_w_in': 1, 'c_conv_w': 1, 'c_w_out': 0, 'd_w_in': 1, 'd_w_out': 0}
BIG = ['xa_w_kv', 'ffn_w_up', 'ffn_w_down', 'a_w_in', 'a_w_gate2', 'a_w_out', 'b_w_in', 'b_w_out', 'c_w_in', 'c_w_out',
       'd_w_in', 'd_w_out']
SMALL = [n for n in WEIGHTS if n not in BIG]
LAYERED = ['ffn_w_up', 'ffn_w_down']
N_CHIPS = 4
N_DEV = 8


class _MatmulSet:
    def __init__(self, cast, precision):
        def dot(a, b, dims):
            if cast:
                a = a.astype(MXU_DTYPE)
                b = b.astype(MXU_DTYPE)
            return lax.dot_general(a, b, (dims, ((), ())), precision=precision, preferred_element_type=F32)

        @jax.custom_vjp
        def nn(a, b):
            return dot(a, b, ((1,), (0,)))

        @jax.custom_vjp
        def nt(a, b):
            return dot(a, b, ((1,), (1,)))

        @jax.custom_vjp
        def tn(a, b):
            return dot(a, b, ((0,), (0,)))

        nn.defvjp(lambda a, b: (nn(a, b), (a, b)), lambda r, g: (nt(g, r[1]), tn(r[0], g)))
        nt.defvjp(lambda a, b: (nt(a, b), (a, b)), lambda r, g: (nn(g, r[1]), tn(g, r[0])))
        tn.defvjp(lambda a, b: (tn(a, b), (a, b)), lambda r, g: (nt(r[1], g), nn(r[0], g)))
        self.nn, self.nt, self.tn = nn, nt, tn


mm = _MatmulSet(True, None)
hi = _MatmulSet(False, lax.Precision.HIGHEST)


def _sigmoid(x):
    return jax.nn.sigmoid(x)


def _silu(x):
    return x * jax.nn.sigmoid(x)


def _softplus(x):
    return jnp.maximum(x, 0.0) + jnp.log1p(jnp.exp(-jnp.abs(x)))


def _rms(x, g, n_real=None):
    n = n_real or x.shape[-1]
    ms = jnp.sum(x * x, axis=-1, keepdims=True) * (1.0 / n)
    return x * lax.rsqrt(ms + EPS) * g


@jax.custom_vjp
def _swap_halves(x):
    return pltpu.roll(x, 64, 1)


_swap_halves.defvjp(lambda x: (_swap_halves(x), None), lambda _, g: (_swap_halves(g),))


def _tile(n, cands):
    for c in cands:
        if n % c == 0:
            return c
    raise ValueError(f"no tile for {n} among {cands}")


def _cparams(sem):
    return pltpu.CompilerParams(dimension_semantics=sem, vmem_limit_bytes=VMEM_LIMIT_V7X)


def _f32(v):
    return v.astype(F32) if jnp.issubdtype(v.dtype, jnp.floating) else v


def matmul(a, b, *, ta=False, tb=False, add=None, out_dtype=F32, b_layer=None, b_koff=0, into=None):
    m, k = (a.shape[1], a.shape[0]) if ta else a.shape
    b2 = b.shape[1:] if b_layer is not None else b.shape
    n = b2[0] if tb else b2[1]
    assert b_koff + k <= (b2[1] if tb else b2[0]), (a.shape, b.shape, ta, tb, b_koff)
    sa, sb, so = a.dtype.itemsize, b.dtype.itemsize, jnp.dtype(out_dtype).itemsize
    n_align = math.gcd(n, into[2]) if into is not None and into[2] else n
    k_align = math.gcd(k, b_koff) if b_koff else k

    def vmem(tm_, tn_, tk_):
        return (2 * tm_ * tk_ * sa + 2 * tk_ * tn_ * sb + 2 * tm_ * tn_ * so + (tm_ * tn_ * 4 if tk_ < k else 0)
                + (2 * tm_ * tn_ * add.dtype.itemsize if add is not None else 0))

    for tk in [t for t in MM_K_TILES if k_align % t == 0]:
        fits = [(tm_ * tn_, tm_, tn_) for tm_ in MM_TILES if m % tm_ == 0 for tn_ in MM_TILES
                if n % tn_ == 0 and n_align % tn_ == 0 and vmem(tm_, tn_, tk) <= MM_VMEM_BUDGET]
        if fits and (max(fits)[0] >= min(MM_MIN_OUT_TILE, m * n) or tk == MM_K_TILES[-1]):
            break
    _, tm, tn = max(fits)
    nk = k // tk
    dims = (((0,) if ta else (1,)), ((1,) if tb else (0,)))
    n_extra = (add is not None) + (into is not None)

    def body(*refs):
        a_ref, b_ref = refs[0], refs[1]
        add_ref = refs[2] if add is not None else None
        o_ref = refs[2 + n_extra]
        part = lax.dot_general(a_ref[...].astype(MXU_DTYPE), b_ref[...].astype(MXU_DTYPE), (dims, ((), ())),
                               preferred_element_type=F32)

        def finish(r):
            if add_ref is not None:
                r = r + add_ref[...].astype(F32)
            o_ref[...] = r.astype(o_ref.dtype)

        if nk == 1:
            finish(part)
            return
        acc = refs[-1]
        kk = pl.program_id(2)

        @pl.when(kk == 0)
        def _():
            acc[...] = part

        @pl.when(kk > 0)
        def _():
            acc[...] += part

        @pl.when(kk == nk - 1)
        def _():
            finish(acc[...])

    a_spec = pl.BlockSpec((tk, tm), lambda i, j, q: (q, i)) if ta else pl.BlockSpec((tm, tk), lambda i, j, q: (i, q))
    ko = b_koff // tk
    if b_layer is None:
        b_spec = (pl.BlockSpec((tn, tk), lambda i, j, q: (j, q + ko)) if tb
                  else pl.BlockSpec((tk, tn), lambda i, j, q: (q + ko, j)))
    else:
        b_spec = (pl.BlockSpec((None, tn, tk), lambda i, j, q: (b_layer, j, q + ko)) if tb
                  else pl.BlockSpec((None, tk, tn), lambda i, j, q: (b_layer, q + ko, j)))
    o_spec = pl.BlockSpec((tm, tn), lambda i, j, q: (i, j))
    ins, specs = [a, b], [a_spec, b_spec]
    if add is not None:
        ins.append(add)
        specs.append(o_spec)
    aliases = {}
    out_shape = jax.ShapeDtypeStruct((m, n), out_dtype)
    if into is not None:
        buf, layer, col0 = into
        assert buf.shape[1] == m and buf.dtype == out_dtype
        co = col0 // tn
        ins.append(buf)
        specs.append(_ANY)
        aliases = {len(ins) - 1: 0}
        o_spec = pl.BlockSpec((None, tm, tn), lambda i, j, q: (layer, i, j + co))
        out_shape = jax.ShapeDtypeStruct(buf.shape, buf.dtype)
    return pl.pallas_call(
        body, name=f"mm_{m}x{k}x{n}_{int(ta)}{int(tb)}{int(add is not None)}{int(b_layer is not None)}{int(into is not None)}",
        grid=(m // tm, n // tn, nk), in_specs=specs, out_specs=o_spec, out_shape=out_shape,
        input_output_aliases=aliases,
        scratch_shapes=[pltpu.VMEM((tm, tn), F32)] if nk > 1 else [],
        compiler_params=_cparams(("parallel", "parallel", "arbitrary")),
    )(*ins)


def _resident_tm(m, row_bytes, resident_bytes):
    for tm in (512, 256, 128):
        if m % tm == 0 and 2 * (resident_bytes + tm * row_bytes) <= MM_VMEM_BUDGET:
            return tm
    return 128


def matmul_sum(a_list, b_list, tb, add=None):
    n_ops = len(a_list)
    m, n = a_list[0].shape[0], b_list[0].shape[0 if tb else 1]
    dims = ((1,), (1,) if tb else (0,))
    tm = _resident_tm(m, sum(a.shape[1] * a.dtype.itemsize for a in a_list) + n * 4 * (1 + (add is not None)),
                      sum(b.size * b.dtype.itemsize for b in b_list))

    def body(*refs):
        acc = refs[2 * n_ops][...] if add is not None else None
        for t in range(n_ops):
            part = lax.dot_general(refs[t][...].astype(MXU_DTYPE), refs[n_ops + t][...].astype(MXU_DTYPE),
                                   (dims, ((), ())), preferred_element_type=F32)
            acc = part if acc is None else acc + part
        refs[-1][...] = acc

    o_spec = pl.BlockSpec((tm, n), lambda i: (i, 0))
    return pl.pallas_call(
        body, name=f"mm_sum_{n_ops}x{sum(a.shape[1] for a in a_list)}_{int(tb)}{int(add is not None)}", grid=(m // tm,),
        in_specs=[pl.BlockSpec((tm, a.shape[1]), lambda i: (i, 0)) for a in a_list] + [_whole_spec(b) for b in b_list]
        + ([o_spec] if add is not None else []),
        out_specs=o_spec, out_shape=jax.ShapeDtypeStruct((m, n), F32),
        compiler_params=_cparams(("parallel",)),
    )(*a_list, *b_list, *([add] if add is not None else []))


def matmul_multi(a, b_list):
    n_ops = len(b_list)
    m = a.shape[0]
    tm = _resident_tm(m, a.shape[1] * a.dtype.itemsize + 4 * sum(b.shape[1] for b in b_list),
                      sum(b.size * b.dtype.itemsize for b in b_list))

    def body(*refs):
        av = refs[0][...].astype(MXU_DTYPE)
        for t in range(n_ops):
            refs[1 + n_ops + t][...] = jnp.dot(av, refs[1 + t][...].astype(MXU_DTYPE), preferred_element_type=F32)

    return pl.pallas_call(
        body, name=f"mm_multi_{n_ops}x{sum(b.shape[1] for b in b_list)}", grid=(m // tm,),
        in_specs=[pl.BlockSpec((tm, a.shape[1]), lambda i: (i, 0))] + [_whole_spec(b) for b in b_list],
        out_specs=[pl.BlockSpec((tm, b.shape[1]), lambda i: (i, 0)) for b in b_list],
        out_shape=[jax.ShapeDtypeStruct((m, b.shape[1]), F32) for b in b_list],
        compiler_params=_cparams(("parallel",)),
    )(a, *b_list)


def _row_spec(a, block):
    return pl.BlockSpec((block, a.shape[1]), lambda i: (i, 0))


def _whole_spec(a):
    return pl.BlockSpec(a.shape, lambda i: (0,) * a.ndim)


def tmap(name, fn, rows, params, outs, block=ROW_BLOCK):
    s = rows[0].shape[0]
    block = min(block, s)
    nr, npar = len(rows), len(params)

    def body(*refs):
        res = fn(*[_f32(r[...]) for r in refs[:nr]], *[_f32(p[...]) for p in refs[nr:nr + npar]])
        for o_ref, v in zip(refs[nr + npar:], res, strict=True):
            o_ref[...] = v.astype(o_ref.dtype)

    return pl.pallas_call(
        body, name=name, grid=(s // block,),
        in_specs=[_row_spec(a, block) for a in rows] + [_whole_spec(p) for p in params],
        out_specs=[pl.BlockSpec((block, w), lambda i: (i, 0)) for w, _ in outs],
        out_shape=[jax.ShapeDtypeStruct((s, w), dt) for w, dt in outs],
        compiler_params=_cparams(("parallel",)),
    )(*rows, *params)


def tmap_bwd(name, fn, rows, params, douts, row_grad, row_add=None, grad_dtype=None, block=ROW_BLOCK):
    s = rows[0].shape[0]
    block = min(block, s)
    grad_dtype = grad_dtype or MXU_DTYPE
    nr, npar, nd = len(rows), len(params), len(douts)
    gr = [i for i in range(nr) if row_grad[i]]
    row_add = row_add or {}
    adds = [row_add[i] for i in gr if i in row_add]

    def body(*refs):
        rv = [_f32(r[...]) for r in refs[:nr]]
        pv = [_f32(p[...]) for p in refs[nr:nr + npar]]
        dv = tuple(_f32(d[...]) for d in refs[nr + npar:nr + npar + nd])
        add_refs = list(refs[nr + npar + nd:nr + npar + nd + len(adds)])
        out_refs = refs[nr + npar + nd + len(adds):]

        def f(*diff):
            rr = list(rv)
            for n_, i_ in enumerate(gr):
                rr[i_] = diff[n_]
            return tuple(fn(*rr, *diff[len(gr):]))

        _, vjp = jax.vjp(f, *[rv[i_] for i_ in gr], *pv)
        g = vjp(dv)
        for n_, i_ in enumerate(gr):
            v = g[n_]
            if i_ in row_add:
                v = v + add_refs.pop(0)[...].astype(F32)
            out_refs[n_][...] = v.astype(out_refs[n_].dtype)
        first = pl.program_id(0) == 0
        for n_ in range(npar):
            ref = out_refs[len(gr) + n_]

            @pl.when(first)
            def _(ref=ref):
                ref[...] = jnp.zeros_like(ref)

            ref[...] += g[len(gr) + n_]

    res = pl.pallas_call(
        body, name=name, grid=(s // block,),
        in_specs=[_row_spec(a, block) for a in rows] + [_whole_spec(p) for p in params]
        + [_row_spec(d, block) for d in douts] + [_row_spec(a, block) for a in adds],
        out_specs=[_row_spec(rows[i], block) for i in gr] + [_whole_spec(p) for p in params],
        out_shape=[jax.ShapeDtypeStruct(rows[i].shape, F32 if i in row_add else grad_dtype) for i in gr]
        + [jax.ShapeDtypeStruct(p.shape, F32) for p in params],
        compiler_params=_cparams(("arbitrary",)),
    )(*rows, *params, *douts, *adds)
    return list(res[:len(gr)]), list(res[len(gr):])


def rscan(name, fn, state_shapes, rows, params, outs, block):
    s = rows[0].shape[0]
    nsteps = s // block
    nr, npar, no, ns = len(rows), len(params), len(outs), len(state_shapes)

    def body(*refs):
        out_refs = refs[nr + npar:nr + npar + no]
        sav_refs = refs[nr + npar + no:nr + npar + no + ns]
        st_refs = refs[nr + npar + no + ns:]

        @pl.when(pl.program_id(0) == 0)
        def _():
            for st in st_refs:
                st[...] = jnp.zeros_like(st)

        sts = tuple(st[...] for st in st_refs)
        for sv, v in zip(sav_refs, sts):
            sv[...] = v
        new, res = fn(sts, *[_f32(r[...]) for r in refs[:nr]], *[_f32(p[...]) for p in refs[nr:nr + npar]])
        for st, v in zip(st_refs, new, strict=True):
            st[...] = v
        for o_ref, v in zip(out_refs, res, strict=True):
            o_ref[...] = v.astype(o_ref.dtype)

    res = pl.pallas_call(
        body, name=name, grid=(nsteps,),
        in_specs=[_row_spec(a, block) for a in rows] + [_whole_spec(p) for p in params],
        out_specs=[pl.BlockSpec((block, w), lambda i: (i, 0)) for w, _ in outs]
        + [pl.BlockSpec(sh, lambda i: (i, 0)) for sh in state_shapes],
        out_shape=[jax.ShapeDtypeStruct((s, w), dt) for w, dt in outs]
        + [jax.ShapeDtypeStruct((nsteps * sh[0], sh[1]), F32) for sh in state_shapes],
        scratch_shapes=[pltpu.VMEM(sh, F32) for sh in state_shapes],
        compiler_params=_cparams(("arbitrary",)),
    )(*rows, *params)
    return list(res[:no]), list(res[no:])


def rscan_bwd(name, fn, saved, rows, params, douts, block, grad_dtype=None):
    s = rows[0].shape[0]
    nsteps = s // block
    grad_dtype = grad_dtype or MXU_DTYPE
    nr, npar, nd, ns = len(rows), len(params), len(douts), len(saved)
    state_shapes = [(sv.shape[0] // nsteps, sv.shape[1]) for sv in saved]

    def body(*refs):
        rv = [_f32(r[...]) for r in refs[:nr]]
        pv = [_f32(p[...]) for p in refs[nr:nr + npar]]
        dv = tuple(_f32(d[...]) for d in refs[nr + npar:nr + npar + nd])
        sv = tuple(x[...] for x in refs[nr + npar + nd:nr + npar + nd + ns])
        out_refs = refs[nr + npar + nd + ns:nr + npar + nd + ns + nr + npar]
        dst_refs = refs[nr + npar + nd + ns + nr + npar:]
        first = pl.program_id(0) == 0

        @pl.when(first)
        def _():
            for d in dst_refs:
                d[...] = jnp.zeros_like(d)

        def f(sts, *args):
            return fn(sts, *args)

        _, vjp = jax.vjp(f, sv, *rv, *pv)
        g = vjp((tuple(d[...] for d in dst_refs), dv))
        for d, v in zip(dst_refs, g[0], strict=True):
            d[...] = v
        for n_ in range(nr):
            out_refs[n_][...] = g[1 + n_].astype(out_refs[n_].dtype)
        for n_ in range(npar):
            ref = out_refs[nr + n_]

            @pl.when(first)
            def _(ref=ref):
                ref[...] = jnp.zeros_like(ref)

            ref[...] += g[1 + nr + n_]

    rev = lambda i: (nsteps - 1 - i, 0)
    res = pl.pallas_call(
        body, name=name, grid=(nsteps,),
        in_specs=[pl.BlockSpec((block, a.shape[1]), rev) for a in rows] + [_whole_spec(p) for p in params]
        + [pl.BlockSpec((block, d.shape[1]), rev) for d in douts] + [pl.BlockSpec(sh, rev) for sh in state_shapes],
        out_specs=[pl.BlockSpec((block, a.shape[1]), rev) for a in rows] + [_whole_spec(p) for p in params],
        out_shape=[jax.ShapeDtypeStruct(a.shape, grad_dtype) for a in rows]
        + [jax.ShapeDtypeStruct(p.shape, F32) for p in params],
        scratch_shapes=[pltpu.VMEM(sh, F32) for sh in state_shapes],
        compiler_params=_cparams(("arbitrary",)),
    )(*rows, *params, *douts, *saved)
    return list(res[:nr]), list(res[nr:])


def _norm_stage(x, g):
    return (_rms(x, g),)


def _tril():
    r = lax.broadcasted_iota(jnp.int32, (CHUNK, CHUNK), 0)
    c = lax.broadcasted_iota(jnp.int32, (CHUNK, CHUNK), 1)
    return r >= c


def _gla_chunk(st, q, k, v, la, b):
    tril = _tril()
    rowi = lax.broadcasted_iota(jnp.int32, (CHUNK, 1), 0)
    b_last = jnp.sum(la, axis=0, keepdims=True)
    b_ref = jnp.sum(jnp.where(rowi < CHUNK // 2, la, 0.0), axis=0, keepdims=True)
    att = mm.nt(q * jnp.exp(b - b_ref), k * jnp.exp(b_ref - b))
    att = jnp.where(tril, att, 0.0)
    o = mm.nn(att, v) + mm.nn(q * jnp.exp(b), st)
    decay = jnp.exp(jnp.broadcast_to(b_last, (LANE, LANE)).T)
    decay = jnp.concatenate([decay] * (v.shape[1] // LANE), axis=1)
    st2 = decay * st + mm.tn(k * jnp.exp(b_last - b), v)
    return st2, o


def _gla_step(heads, vp, scale):
    kp = LANE

    def fn(states, q, k, v, la):
        sts = list(states)
        trif = _tril().astype(F32)
        rows = []
        for c in range(q.shape[0] // CHUNK):
            r = slice(c * CHUNK, (c + 1) * CHUNK)
            b_all = hi.nn(trif, la[r])
            oh = []
            for h in range(heads):
                ks, vs = slice(h * kp, (h + 1) * kp), slice(h * vp, (h + 1) * vp)
                qh = q[r, ks] * scale if scale != 1.0 else q[r, ks]
                sts[h], o = _gla_chunk(sts[h], qh, k[r, ks], v[r, vs], la[r, ks], b_all[:, ks])
                oh.append(o)
            rows.append(jnp.concatenate(oh, axis=1))
        return tuple(sts), (jnp.concatenate(rows, axis=0),)

    return fn


def _ssd_step(states, xa, dtr, dtb, alog, dsk):
    sts = list(states)
    trif = _tril().astype(F32)
    wide = lax.broadcasted_iota(jnp.int32, (CHUNK, LANE), 0) >= lax.broadcasted_iota(jnp.int32, (CHUNK, LANE), 1)
    hg = SSM_HEADS // SSM_GROUPS
    xw = SSM_HEADS * LANE
    lane, head = lax.broadcasted_iota(jnp.int32, (LANE, xw), 1), lax.broadcasted_iota(jnp.int32, (LANE, xw), 0)
    spread = ((lane >= head * LANE) & (lane < (head + 1) * LANE)).astype(F32)
    neg_a = -jnp.exp(alog)
    pad = jnp.zeros((CHUNK, LANE), F32)
    rows = []
    for c in range(xa.shape[0] // CHUNK):
        r = slice(c * CHUNK, (c + 1) * CHUNK)
        dt_all = _softplus(hi.nn(dtr[r], spread) + dtb)
        a_all = dt_all * neg_a
        acs_all = hi.nn(trif, a_all)
        last_all = jnp.sum(a_all, axis=0, keepdims=True)
        yh = []
        for g in range(SSM_GROUPS):
            bm = xa[r, xw + g * LANE:xw + (g + 1) * LANE]
            cm = xa[r, xw + (SSM_GROUPS + g) * LANE:xw + (SSM_GROUPS + g + 1) * LANE]
            cb = mm.nt(cm, jnp.concatenate([bm, pad], axis=0))
            for hh in range(hg):
                h = g * hg + hh
                ls = slice(h * LANE, (h + 1) * LANE)
                xs, acs, acs_last = xa[r, ls], acs_all[:, ls], last_all[:, ls]
                xdt = xs * dt_all[:, ls]
                seg = acs - jnp.concatenate([acs, pad], axis=0).T[:CHUNK]
                lmat = jnp.exp(jnp.where(wide, seg, -1e30))
                y = (mm.nn(cb * lmat, jnp.concatenate([xdt, pad], axis=0)) + mm.nn(cm, sts[h]) * jnp.exp(acs)
                     + dsk[:, ls] * xs)
                sts[h] = jnp.exp(acs_last) * sts[h] + mm.tn(bm, xdt * jnp.exp(acs_last - acs))
                yh.append(y)
        rows.append(jnp.concatenate(yh, axis=1))
    return tuple(sts), (jnp.concatenate(rows, axis=0),)


def _gla_pre(glr, w2, bg):
    z = mm.nn(glr, w2) + bg
    return (-_softplus(-z) * (1.0 / GLA_GATE_NORM),)


def _gla_post(o, og, g):
    w = 2 * LANE
    return (jnp.concatenate([_rms(o[:, h * w:(h + 1) * w], g, GLA_DV) * _silu(og[:, h * w:(h + 1) * w])
                             for h in range(GLA_HEADS)], axis=1),)


def _hgrn_pre(q, f, lbnd):
    e = jnp.exp(lbnd - jnp.max(lbnd, axis=0, keepdims=True))
    rowi = lax.broadcasted_iota(jnp.int32, e.shape, 0)
    lb = jnp.sum(jnp.where(rowi >= 1, e, 0.0), axis=0, keepdims=True) / jnp.sum(e, axis=0, keepdims=True)
    fg = lb + (1.0 - lb) * _sigmoid(f)
    return _silu(q), 1.0 - fg, jnp.log(fg)


def _hgrn_post(o, og, g):
    return (jnp.concatenate([_rms(o[:, h * LANE:(h + 1) * LANE], g) for h in range(HGRN_HEADS)], axis=1)
            * _sigmoid(og),)


def _mamba_post(y, z, g):
    v = y * _silu(z)
    w = (SSM_HEADS // SSM_GROUPS) * LANE
    n_real = (SSM_HEADS // SSM_GROUPS) * SSM_HD
    return (jnp.concatenate([_rms(v[:, i * w:(i + 1) * w], g[:, i * w:(i + 1) * w], n_real)
                             for i in range(SSM_GROUPS)], axis=1),)


def _dil_pre(q, k, cosf, sinf, qg, kg):
    def groups(x, g):
        out = []
        for grp in range(len(DIL_GROUPS)):
            hs = []
            for h in range(grp * DIL_HEADS, (grp + 1) * DIL_HEADS):
                n = _rms(x[:, h * LANE:(h + 1) * LANE], g)
                hs.append(n * cosf + _swap_halves(n) * sinf)
            out.append(jnp.concatenate(hs, axis=1))
        return out

    return (*groups(q, qg), *groups(k, kg))


def _dil_merge(o0, o1, o2, l0, l1, l2):
    m = jnp.maximum(jnp.maximum(l0, l1), l2)
    e0, e1, e2 = jnp.exp(l0 - m), jnp.exp(l1 - m), jnp.exp(l2 - m)
    return ((e0 * o0 + e1 * o1 + e2 * o2) / (e0 + e1 + e2),)


def _dil_block(q, kp, kc, vp, vc, lim):
    kk = jnp.concatenate([kp, kc], axis=0)
    vv = jnp.concatenate([vp, vc], axis=0)
    s = mm.nt(q, kk) * (DIL_HD ** -0.5)
    i = lax.broadcasted_iota(jnp.int32, s.shape, 0)
    j = lax.broadcasted_iota(jnp.int32, s.shape, 1)
    dist = DIL_BLOCK + i - j
    s = jnp.where((dist >= 0) & (dist <= DIL_BLOCK) & (j >= lim), s, -1e30)
    m = jnp.max(s, axis=-1, keepdims=True)
    p = jnp.exp(s - m)
    l = jnp.sum(p, axis=-1, keepdims=True)
    return mm.nn(p / l, vv), jnp.broadcast_to(m + jnp.log(l), (q.shape[0], LANE))


def _xattn(xq, kv, qg, kg):
    w = XA_HEADS * LANE
    os_ = []
    for h in range(XA_HEADS):
        ls = slice(h * LANE, (h + 1) * LANE)
        q = _rms(xq[:, ls], qg, XA_HD)
        k = _rms(kv[:, ls], kg, XA_HD)
        s = mm.nt(q, k) * (XA_HD ** -0.5)
        p = jnp.exp(s - jnp.max(s, axis=-1, keepdims=True))
        p = p / jnp.sum(p, axis=-1, keepdims=True)
        os_.append(mm.nn(p, kv[:, w + h * LANE:w + (h + 1) * LANE]))
    return (jnp.concatenate(os_, axis=1),)


def _dil_geometry(s, w, r, g, v_cols):
    hb = DIL_HEADS if r == 1 else 1
    rb = DIL_BLOCK * r
    nb = s // rb
    bw = hb * LANE
    v_col0 = g * (w // bw)
    assert v_cols % bw == 0 and s % rb == 0
    return hb, rb, nb, bw, v_col0


def _sub(r, res):
    return pl.ds(res, DIL_BLOCK, stride=r) if r > 1 else slice(None)


def dil_attn(name, q, k, v, r, g):
    s, w = q.shape
    hb, rb, nb, bw, v_col0 = _dil_geometry(s, w, r, g, v.shape[1])

    def body(q_r, kp_r, kc_r, vp_r, vc_r, o_r, l_r):
        lim = jnp.where(pl.program_id(1) == 0, DIL_BLOCK, 0)
        for res in range(r):
            rows = _sub(r, res)
            for h in range(hb):
                ls = slice(h * LANE, (h + 1) * LANE)
                o, lse = _dil_block(q_r[rows, ls], kp_r[rows, ls], kc_r[rows, ls], vp_r[rows, ls], vc_r[rows, ls], lim)
                o_r[rows, ls] = o
                l_r[rows, ls] = lse

    cur = pl.BlockSpec((rb, bw), lambda hblk, n: (n, hblk))
    prev = pl.BlockSpec((rb, bw), lambda hblk, n: (jnp.maximum(n - 1, 0), hblk))
    vcur = pl.BlockSpec((rb, bw), lambda hblk, n: (n, v_col0 + hblk))
    vprev = pl.BlockSpec((rb, bw), lambda hblk, n: (jnp.maximum(n - 1, 0), v_col0 + hblk))
    return pl.pallas_call(
        body, name=name, grid=(w // bw, nb), in_specs=[cur, prev, cur, vprev, vcur], out_specs=[cur, cur],
        out_shape=[jax.ShapeDtypeStruct((s, w), F32)] * 2,
        compiler_params=_cparams(("parallel", "parallel")),
    )(q, k, k, v, v)


def dil_attn_bwd(name, q, k, v, do, dlse, r, g):
    s, w = q.shape
    hb, rb, nb, bw, v_col0 = _dil_geometry(s, w, r, g, v.shape[1])

    def body(q_r, kp_r, kc_r, vp_r, vc_r, do_r, dl_r, dq_r, dk_r, dv_r, ck, cv):
        i = pl.program_id(1)
        lim = jnp.where(i == nb - 1, DIL_BLOCK, 0)

        @pl.when(i == 0)
        def _():
            ck[...] = jnp.zeros_like(ck)
            cv[...] = jnp.zeros_like(cv)

        for res in range(r):
            rows = _sub(r, res)
            for h in range(hb):
                ls = slice(h * LANE, (h + 1) * LANE)
                _, vjp = jax.vjp(functools.partial(_dil_block, lim=lim),
                                 q_r[rows, ls], kp_r[rows, ls], kc_r[rows, ls], vp_r[rows, ls], vc_r[rows, ls])
                gq, gkp, gkc, gvp, gvc = vjp((do_r[rows, ls], dl_r[rows, ls]))
                dq_r[rows, ls] = gq
                dk_r[rows, ls] = gkc + ck[rows, ls]
                dv_r[rows, ls] = gvc + cv[rows, ls]
                ck[rows, ls] = gkp
                cv[rows, ls] = gvp

    cur = pl.BlockSpec((rb, bw), lambda hblk, i: (nb - 1 - i, hblk))
    prev = pl.BlockSpec((rb, bw), lambda hblk, i: (jnp.maximum(nb - 2 - i, 0), hblk))
    vcur = pl.BlockSpec((rb, bw), lambda hblk, i: (nb - 1 - i, v_col0 + hblk))
    vprev = pl.BlockSpec((rb, bw), lambda hblk, i: (jnp.maximum(nb - 2 - i, 0), v_col0 + hblk))
    return pl.pallas_call(
        body, name=name, grid=(w // bw, nb), in_specs=[cur, prev, cur, vprev, vcur, cur, cur],
        out_specs=[cur, cur, cur], out_shape=[jax.ShapeDtypeStruct((s, w), F32)] * 3,
        scratch_shapes=[pltpu.VMEM((rb, bw), F32)] * 2,
        compiler_params=_cparams(("parallel", "arbitrary")),
    )(q, k, k, v, v, do, dlse)


def _dsilu(u):
    sg = _sigmoid(u)
    return sg * (1.0 + u * (1.0 - sg))


def _ride(body, rider, n_in, n_out, n_scratch, grid):
    if rider is None:
        return body, [], [], [], [], [], None
    ni, no = len(rider.ins), len(rider.out_shapes)

    def wrapped(*refs):
        k_in, r_in = refs[:n_in], refs[n_in:n_in + ni]
        k_out, r_out = refs[n_in + ni:n_in + ni + n_out], refs[n_in + ni + n_out:n_in + ni + n_out + no]
        k_scr = refs[n_in + ni + n_out + no:n_in + ni + n_out + no + n_scratch]
        send_sems, recv_sems = refs[-2], refs[-1]
        first = functools.reduce(jnp.logical_and, [pl.program_id(a) == 0 for a in range(len(grid))])
        last = functools.reduce(jnp.logical_and, [pl.program_id(a) == g - 1 for a, g in enumerate(grid)])

        @pl.when(first)
        def _():
            rider.start(r_in, r_out, send_sems, recv_sems)

        body(*k_in, *k_out, *k_scr)

        @pl.when(last)
        def _():
            rider.finish(r_in, r_out, send_sems, recv_sems)

    sems = [pltpu.SemaphoreType.DMA((rider.n_sems,)), pltpu.SemaphoreType.DMA((rider.n_sems,))]
    return wrapped, rider.ins, [_ANY] * ni, [_ANY] * no, rider.out_shapes, sems, ("arbitrary",) * len(grid)


CONV_STRIP = 16


def _shifted_rows(prev8, cur_r, next8, lanes, s0, n, sh, block):
    if s0 - sh < 0:
        assert s0 == 0
        xp = jnp.concatenate([prev8, cur_r[0:n, lanes]], axis=0)
        return pltpu.roll(xp, sh, 0)[SUBLANE:SUBLANE + n]
    if s0 - sh + n > block:
        assert s0 == block and n == SUBLANE
        xp = jnp.concatenate([cur_r[block - SUBLANE:block, lanes], next8], axis=0)
        return (pltpu.roll(xp, sh, 0) if sh else xp)[SUBLANE:]
    return cur_r[pl.ds(s0 - sh, n), lanes]


def conv_fwd(name, x, w, b, mode, out_dtype, tc, block=ROW_BLOCK, rider=None):
    s, c = x.shape
    ntap = w.shape[0]
    block = min(block, s)
    f = c // 2 if mode == 'glu' else c
    nh = 2 if mode == 'glu' else 1
    off = f // tc

    def body(*refs):
        first = pl.program_id(1) == 0
        o_ref = refs[-1]

        def column(cidx, carry):
            lanes = pl.ds(pl.multiple_of(cidx * LANE, LANE), LANE)
            prevs = [jnp.where(first, 0.0, refs[4 * hlf][:, lanes]) for hlf in range(nh)]
            for s0 in range(0, block, CONV_STRIP):
                us = []
                for hlf in range(nh):
                    _, cur_r, w_r, b_r = refs[4 * hlf:4 * hlf + 4]
                    acc = b_r[:, lanes]
                    for j in range(ntap):
                        xs = _shifted_rows(prevs[hlf], cur_r, None, lanes, s0, CONV_STRIP, ntap - 1 - j, block)
                        acc = acc + w_r[j:j + 1, lanes] * xs
                    us.append(acc)
                res = _silu(us[0]) * us[1] if mode == 'glu' else _silu(us[0])
                o_ref[pl.ds(s0, CONV_STRIP), lanes] = res.astype(o_ref.dtype)
            return carry

        lax.fori_loop(0, tc // LANE, column, 0)

    rb = block // SUBLANE
    ins, specs = [], []
    for hlf in range(nh):
        o = hlf * off
        ins += [x, x, w, b]
        specs += [pl.BlockSpec((SUBLANE, tc), lambda j, i, o=o: (jnp.maximum(i * rb - 1, 0), j + o)),
                  pl.BlockSpec((block, tc), lambda j, i, o=o: (i, j + o)),
                  pl.BlockSpec((ntap, tc), lambda j, i, o=o: (0, j + o)),
                  pl.BlockSpec((1, tc), lambda j, i, o=o: (0, j + o))]
    grid = (f // tc, s // block)
    body, r_ins, r_in_specs, r_out_specs, r_out_shapes, r_scratch, sem = _ride(body, rider, len(ins), 1, 0, grid)
    res = pl.pallas_call(
        body, name=name, grid=grid, in_specs=specs + r_in_specs,
        out_specs=[pl.BlockSpec((block, tc), lambda j, i: (i, j))] + r_out_specs,
        out_shape=[jax.ShapeDtypeStruct((s, f), out_dtype)] + r_out_shapes, scratch_shapes=r_scratch,
        compiler_params=_cparams(sem or ("parallel", "parallel")),
    )(*ins, *r_ins)
    return res[0] if rider is None else (res[0], rider.results(res[1:]))


def conv_bwd(name, x, w, b, dout, mode, tc, block=ROW_BLOCK, rider=None):
    s, c = x.shape
    ntap = w.shape[0]
    block = min(block, s)
    nblk = s // block
    f = c // 2 if mode == 'glu' else c
    nh = 2 if mode == 'glu' else 1
    off = f // tc
    ext = block + SUBLANE

    def body(*refs):
        i = pl.program_id(1)
        first, last = i == 0, i == nblk - 1
        dcur_r, dnext_r = refs[5 * nh], refs[5 * nh + 1]
        outs = refs[5 * nh + 2:5 * nh + 2 + 3 * nh]
        du_scr = refs[5 * nh + 2 + 3 * nh:]

        @pl.when(first)
        def _():
            for hlf in range(nh):
                outs[3 * hlf + 1][...] = jnp.zeros_like(outs[3 * hlf + 1])
                outs[3 * hlf + 2][...] = jnp.zeros_like(outs[3 * hlf + 2])

        def column(cidx, carry):
            lanes = pl.ds(pl.multiple_of(cidx * LANE, LANE), LANE)
            prevs = [jnp.where(first, 0.0, refs[5 * hlf][:, lanes]) for hlf in range(nh)]
            nexts = [jnp.where(last, 0.0, refs[5 * hlf + 2][:, lanes]) for hlf in range(nh)]
            db_acc = [jnp.zeros((CONV_STRIP, LANE), F32) for _ in range(nh)]
            dw_acc = [[jnp.zeros((CONV_STRIP, LANE), F32) for _ in range(ntap)] for _ in range(nh)]
            for s0 in range(0, ext, CONV_STRIP):
                n = min(CONV_STRIP, ext - s0)
                d_e = dcur_r[pl.ds(s0, n), lanes] if s0 < block else jnp.where(last, 0.0, dnext_r[:, lanes])
                xs, us = [], []
                for hlf in range(nh):
                    cur_r, w_r, b_r = refs[5 * hlf + 1], refs[5 * hlf + 3], refs[5 * hlf + 4]
                    sh_rows = [_shifted_rows(prevs[hlf], cur_r, nexts[hlf], lanes, s0, n, ntap - 1 - j, block)
                               for j in range(ntap)]
                    acc = b_r[:, lanes]
                    for j in range(ntap):
                        acc = acc + w_r[j:j + 1, lanes] * sh_rows[j]
                    xs.append(sh_rows)
                    us.append(acc)
                dus = [d_e * us[1] * _dsilu(us[0]), d_e * _silu(us[0])] if mode == 'glu' else [d_e * _dsilu(us[0])]
                for hlf in range(nh):
                    du_scr[hlf][pl.ds(s0, n), lanes] = dus[hlf]
                    if s0 < block:
                        db_acc[hlf] = db_acc[hlf] + dus[hlf]
                        for j in range(ntap):
                            dw_acc[hlf][j] = dw_acc[hlf][j] + dus[hlf] * xs[hlf][j]
            for hlf in range(nh):
                w_r = refs[5 * hlf + 3]
                dx_r, dw_r, db_r = outs[3 * hlf:3 * hlf + 3]
                db_r[:, lanes] += jnp.sum(db_acc[hlf], axis=0, keepdims=True)
                for j in range(ntap):
                    dw_r[j:j + 1, lanes] += jnp.sum(dw_acc[hlf][j], axis=0, keepdims=True)
                for s0 in range(0, block, CONV_STRIP):
                    dx = None
                    for j in range(ntap):
                        term = w_r[j:j + 1, lanes] * du_scr[hlf][pl.ds(s0 + ntap - 1 - j, CONV_STRIP), lanes]
                        dx = term if dx is None else dx + term
                    dx_r[pl.ds(s0, CONV_STRIP), lanes] = dx.astype(dx_r.dtype)
            return carry

        lax.fori_loop(0, tc // LANE, column, 0)

    rb = block // SUBLANE
    nrow8 = s // SUBLANE
    ins, specs = [], []
    for hlf in range(nh):
        o = hlf * off
        ins += [x, x, x, w, b]
        specs += [pl.BlockSpec((SUBLANE, tc), lambda j, i, o=o: (jnp.maximum(i * rb - 1, 0), j + o)),
                  pl.BlockSpec((block, tc), lambda j, i, o=o: (i, j + o)),
                  pl.BlockSpec((SUBLANE, tc), lambda j, i, o=o: (jnp.minimum((i + 1) * rb, nrow8 - 1), j + o)),
                  pl.BlockSpec((ntap, tc), lambda j, i, o=o: (0, j + o)),
                  pl.BlockSpec((1, tc), lambda j, i, o=o: (0, j + o))]
    ins += [dout, dout]
    specs += [pl.BlockSpec((block, tc), lambda j, i: (i, j)),
              pl.BlockSpec((SUBLANE, tc), lambda j, i: (jnp.minimum((i + 1) * rb, nrow8 - 1), j))]
    out_specs, out_shape = [], []
    for hlf in range(nh):
        out_specs += [pl.BlockSpec((block, tc), lambda j, i: (i, j)), pl.BlockSpec((ntap, tc), lambda j, i: (0, j)),
                      pl.BlockSpec((1, tc), lambda j, i: (0, j))]
        out_shape += [jax.ShapeDtypeStruct((s, f), MXU_DTYPE), jax.ShapeDtypeStruct((ntap, f), F32),
                      jax.ShapeDtypeStruct((1, f), F32)]
    grid = (f // tc, nblk)
    body, r_ins, r_in_specs, r_out_specs, r_out_shapes, r_scratch, sem = _ride(body, rider, len(ins), 3 * nh, nh, grid)
    res = pl.pallas_call(
        body, name=name, grid=grid, in_specs=specs + r_in_specs, out_specs=out_specs + r_out_specs,
        out_shape=out_shape + r_out_shapes, scratch_shapes=[pltpu.VMEM((ext, tc), F32)] * nh + r_scratch,
        compiler_params=_cparams(sem or ("parallel", "arbitrary")),
    )(*ins, *r_ins)
    rode = None if rider is None else rider.results(res[3 * nh:])
    if nh == 1:
        return [res[0]], res[1], res[2], rode
    return ([res[0], res[3]], jnp.concatenate([res[1], res[4]], axis=1), jnp.concatenate([res[2], res[5]], axis=1),
            rode)


def loss_head(y, target, block=ROW_BLOCK):
    s, d = y.shape
    block = min(block, s)

    def body(y_r, t_r, acc_r, dy_r):
        e = y_r[...] - t_r[...]
        dy_r[...] = e * (1.0 / d)

        @pl.when(pl.program_id(0) == 0)
        def _():
            acc_r[...] = jnp.zeros_like(acc_r)

        acc_r[...] += jnp.sum((e * e).reshape(block // SUBLANE, SUBLANE, d), axis=0) * (0.5 / d)

    return pl.pallas_call(
        body, name="loss_head", grid=(s // block,),
        in_specs=[pl.BlockSpec((block, d), lambda i: (i, 0))] * 2,
        out_specs=[pl.BlockSpec((SUBLANE, d), lambda i: (0, 0)), pl.BlockSpec((block, d), lambda i: (i, 0))],
        out_shape=[jax.ShapeDtypeStruct((SUBLANE, d), F32), jax.ShapeDtypeStruct((s, d), F32)],
        compiler_params=_cparams(("arbitrary",)),
    )(y, target)


def adamw(name, w, g, m, v):
    r, c = w.shape
    tr = r if r <= 512 else _tile(r, (512, 256, 128, 64, 32, 16, 8))
    if c * tr * 4 > (1 << 21):
        tr = _tile(r, (256, 128, 64, 32, 16, 8))

    def body(w_r, g_r, m_r, v_r, d_r, nm_r, nv_r):
        gg = g_r[...]
        nm = ADAM_B1 * m_r[...] + (1.0 - ADAM_B1) * gg
        nv = ADAM_B2 * v_r[...] + (1.0 - ADAM_B2) * (gg * gg)
        m_hat = nm / (1.0 - ADAM_B1 ** ADAM_STEP)
        v_hat = nv / (1.0 - ADAM_B2 ** ADAM_STEP)
        d_r[...] = -ADAM_LR * (m_hat / (jnp.sqrt(v_hat) + ADAM_EPS) + ADAM_WD * w_r[...])
        nm_r[...] = nm
        nv_r[...] = nv

    spec = pl.BlockSpec((tr, c), lambda i: (i, 0))
    return pl.pallas_call(
        body, name=name, grid=(r // tr,), in_specs=[spec] * 4, out_specs=[spec] * 3,
        out_shape=[jax.ShapeDtypeStruct((r, c), F32)] * 3, compiler_params=_cparams(("parallel",)),
    )(w, g, m, v)


MESH = pl.DeviceIdType.MESH
_ANY = pl.BlockSpec(memory_space=pl.ANY)


def _place():
    return lax.axis_index("x"), lax.axis_index("y"), lax.axis_index("c")


class Packed:
    def __init__(self, shard_shape):
        self.r, self.c = shard_shape
        self.h = self.r // 2
        self.whole = (N_CHIPS, self.r, self.c)
        self.got = (N_CHIPS, self.h, self.c)
        self.slab_half = (self.h, self.c)

    def shard_half(self, ref, core):
        return ref.at[pl.ds(core * self.h, self.h)]

    def whole_half(self, ref, chip, core):
        return ref.at[chip, pl.ds(core * self.h, self.h)]

    def place(self, whole, shard, chip):
        return lax.dynamic_update_slice(whole, shard[None], (chip, 0, 0))

    def grad_half(self, ref, core):
        return ref.at[:, core]

    def pair_slab(self, ref, chip):
        return ref.at[chip]


class SlabCols:
    def __init__(self, shard_shape):
        self.r, self.c = shard_shape
        self.h = self.r // 2
        self.whole = (self.r, N_CHIPS * self.c)
        self.got = (self.h, N_CHIPS * self.c)
        self.slab_half = (self.h, self.c)

    def _cols(self, chip):
        return pl.ds(pl.multiple_of(chip * self.c, LANE), self.c)

    def shard_half(self, ref, core):
        return ref.at[pl.ds(core * self.h, self.h)]

    def whole_half(self, ref, chip, core):
        return ref.at[pl.ds(core * self.h, self.h), self._cols(chip)]

    def place(self, whole, shard, chip):
        return lax.dynamic_update_slice_in_dim(whole, shard, chip * self.c, 1)

    def grad_half(self, ref, core):
        return ref.at[pl.ds(core * self.h, self.h)]

    def pair_slab(self, ref, chip):
        return ref.at[:, self._cols(chip)]


class GatherRider:
    def __init__(self, shards, kinds):
        self.ins, self.kinds, n = list(shards), kinds, len(shards)
        self.out_shapes = [jax.ShapeDtypeStruct(k.whole, s.dtype) for k, s in zip(kinds, shards)]
        self.n_sems = 6 * n

    def _copies(self, w_refs, out_refs, send_sems, recv_sems):
        x, y, cc = _place()
        chips = [(1 - x, y), (x, 1 - y), (1 - x, 1 - y)]

        def copy(t, k, chip, core, to, src=None):
            dst = self.kinds[t].whole_half(out_refs[t], 2 * chip[0] + chip[1], core)
            return pltpu.make_async_remote_copy(
                src_ref=dst if src is None else src, dst_ref=dst, send_sem=send_sems.at[6 * t + k],
                recv_sem=recv_sems.at[6 * t + k], device_id=to, device_id_type=MESH)

        first = [copy(t, j, (x, y), cc, (*chip, cc), src=self.kinds[t].shard_half(w_refs[t], cc))
                 for t in range(len(self.ins)) for j, chip in enumerate(chips)]
        return copy, first, chips, (x, y, cc)

    def start(self, w_refs, out_refs, send_sems, recv_sems):
        for cp in self._copies(w_refs, out_refs, send_sems, recv_sems)[1]:
            cp.start()

    def finish(self, w_refs, out_refs, send_sems, recv_sems):
        copy, first, chips, (x, y, cc) = self._copies(w_refs, out_refs, send_sems, recv_sems)
        passed = []
        for t in range(len(self.ins)):
            for j, chip in enumerate(chips):
                copy(t, j, chip, cc, (x, y, cc)).wait_recv()
                passed.append(copy(t, 3 + j, chip, cc, (x, y, 1 - cc)))
                passed[-1].start()
        for t in range(len(self.ins)):
            for j, chip in enumerate(chips):
                copy(t, 3 + j, chip, 1 - cc, (x, y, cc)).wait_recv()
        for cp in first + passed:
            cp.wait_send()

    def results(self, outs):
        chip = 2 * lax.axis_index("x") + lax.axis_index("y")
        return [k.place(o, s, chip) for k, o, s in zip(self.kinds, outs, self.ins)]


class ExchangeRider:
    def __init__(self, pairs, kinds):
        self.ins, self.kinds = list(pairs), kinds
        self.out_shapes = [jax.ShapeDtypeStruct((N_CHIPS,) + k.slab_half, p.dtype) for k, p in zip(kinds, pairs)]
        self.n_sems = 3 * len(pairs)

    def start(self, p_refs, out_refs, send_sems, recv_sems):
        x, y, cc = _place()
        for t in range(len(self.ins)):
            for j, chip in enumerate([(1 - x, y), (x, 1 - y), (1 - x, 1 - y)]):
                pltpu.make_async_remote_copy(
                    src_ref=self.kinds[t].pair_slab(p_refs[t], 2 * chip[0] + chip[1]), dst_ref=out_refs[t].at[2 * x + y],
                    send_sem=send_sems.at[3 * t + j], recv_sem=recv_sems.at[3 * t + j], device_id=(*chip, cc),
                    device_id_type=MESH).start()

    def finish(self, p_refs, out_refs, send_sems, recv_sems):
        x, y, cc = _place()
        me = 2 * x + y
        for t in range(len(self.ins)):
            for j, chip in enumerate([(1 - x, y), (x, 1 - y), (1 - x, 1 - y)]):
                them = 2 * chip[0] + chip[1]
                pltpu.make_async_remote_copy(
                    src_ref=self.kinds[t].pair_slab(p_refs[t], them), dst_ref=out_refs[t].at[them],
                    send_sem=send_sems.at[3 * t + j], recv_sem=recv_sems.at[3 * t + j], device_id=(x, y, cc),
                    device_id_type=MESH).wait()

    def results(self, outs):
        return list(outs)


def run_rider(rider, name):
    n, no = len(rider.ins), len(rider.out_shapes)

    def body(*refs):
        parts = (refs[:n], refs[n:n + no], refs[n + no], refs[n + no + 1])
        rider.start(*parts)
        rider.finish(*parts)

    outs = pl.pallas_call(
        body, name=name, in_specs=[_ANY] * n, out_specs=[_ANY] * no, out_shape=rider.out_shapes,
        scratch_shapes=[pltpu.SemaphoreType.DMA((rider.n_sems,)), pltpu.SemaphoreType.DMA((rider.n_sems,))],
    )(*rider.ins)
    return rider.results(outs)


def allgather_devices(buf):
    r, c = buf.shape

    def body(b_ref, out_ref, send_sems, recv_sems, local_sem):
        x, y, cc = _place()
        me = 4 * x + 2 * y + cc
        mine = pltpu.make_async_copy(b_ref, out_ref.at[me], local_sem)
        mine.start()
        copies = []
        for k in range(1, N_DEV):
            px, py, pc = x ^ (k >> 2), y ^ ((k >> 1) & 1), cc ^ (k & 1)
            cp = pltpu.make_async_remote_copy(src_ref=b_ref, dst_ref=out_ref.at[me], send_sem=send_sems.at[k - 1],
                                              recv_sem=recv_sems.at[k - 1], device_id=(px, py, pc), device_id_type=MESH)
            cp.start()
            copies.append((cp, 4 * px + 2 * py + pc))
        for k, (cp, peer) in enumerate(copies):
            pltpu.make_async_remote_copy(src_ref=b_ref, dst_ref=out_ref.at[peer], send_sem=send_sems.at[k],
                                         recv_sem=recv_sems.at[k], device_id=(x, y, cc), device_id_type=MESH).wait_recv()
        for cp, _ in copies:
            cp.wait_send()
        mine.wait()

    return pl.pallas_call(
        body, name="allgather_devices", in_specs=[_ANY], out_specs=_ANY,
        out_shape=jax.ShapeDtypeStruct((N_DEV, r, c), buf.dtype),
        scratch_shapes=[pltpu.SemaphoreType.DMA((N_DEV - 1,)), pltpu.SemaphoreType.DMA((N_DEV - 1,)),
                        pltpu.SemaphoreType.DMA],
    )(buf)


def swap_halves_sibling(gs, kinds, name):
    n = len(gs)

    def body(*refs):
        g_refs, out_refs, send_sems, recv_sems = refs[:n], refs[n:2 * n], refs[2 * n], refs[2 * n + 1]
        x, y, cc = _place()
        cps = []
        for t in range(n):
            cps.append(pltpu.make_async_remote_copy(
                src_ref=kinds[t].grad_half(g_refs[t], 1 - cc), dst_ref=out_refs[t], send_sem=send_sems.at[t],
                recv_sem=recv_sems.at[t], device_id=(x, y, 1 - cc), device_id_type=MESH))
            cps[-1].start()
        for cp in cps:
            cp.wait()

    return pl.pallas_call(
        body, name=name, in_specs=[_ANY] * n, out_specs=[_ANY] * n,
        out_shape=[jax.ShapeDtypeStruct(k.got, g.dtype) for k, g in zip(kinds, gs)],
        scratch_shapes=[pltpu.SemaphoreType.DMA((n,)), pltpu.SemaphoreType.DMA((n,))],
    )(*gs)


def _row_tile(n, limit=512):
    return max(t for t in range(16, limit + 1, 16) if n % t == 0)


def sum_chips(got, own, kind, chip, name):
    def body(chip_ref, got_r, own_r, out_r):
        mine = own_r[...].astype(F32)
        acc = None
        for k in range(N_CHIPS):
            term = jnp.where(chip_ref[0] == k, mine, got_r[k].astype(F32))
            acc = term if acc is None else acc + term
        out_r[...] = acc

    if isinstance(kind, Packed):
        r, c = kind.slab_half
        tr = _row_tile(r)
        grid = (r // tr,)
        specs = [pl.BlockSpec((N_CHIPS, tr, c), lambda i, chip_ref: (0, i, 0)),
                 pl.BlockSpec((None, tr, c), lambda i, chip_ref: (chip_ref[0], i, 0))]
        out_spec = pl.BlockSpec((tr, c), lambda i, chip_ref: (i, 0))
    else:
        r, c = kind.slab_half
        tr = _row_tile(r, 256)
        grid = (r // tr,)
        specs = [pl.BlockSpec((N_CHIPS, tr, c), lambda i, chip_ref: (0, i, 0)),
                 pl.BlockSpec((tr, c), lambda i, chip_ref: (i, chip_ref[0]))]
        out_spec = pl.BlockSpec((tr, c), lambda i, chip_ref: (i, 0))
    return pl.pallas_call(
        body, name=name,
        grid_spec=pltpu.PrefetchScalarGridSpec(num_scalar_prefetch=1, grid=grid, in_specs=specs, out_specs=out_spec),
        out_shape=jax.ShapeDtypeStruct(kind.slab_half, F32),
        compiler_params=_cparams(("parallel",) * len(grid)),
    )(chip, got, own)


def join_halves_sibling(halves):
    n = len(halves)

    def body(*refs):
        h_refs, out_refs, send_sems, recv_sems = refs[:n], refs[n:2 * n], refs[2 * n], refs[2 * n + 1]
        x, y, cc = _place()
        cps = []
        for t in range(n):
            cps.append(pltpu.make_async_remote_copy(
                src_ref=h_refs[t], dst_ref=out_refs[t].at[cc], send_sem=send_sems.at[t], recv_sem=recv_sems.at[t],
                device_id=(x, y, 1 - cc), device_id_type=MESH))
            cps[-1].start()
        for t in range(n):
            pltpu.make_async_remote_copy(
                src_ref=h_refs[t], dst_ref=out_refs[t].at[1 - cc], send_sem=send_sems.at[t], recv_sem=recv_sems.at[t],
                device_id=(x, y, cc), device_id_type=MESH).wait_recv()
        for cp in cps:
            cp.wait_send()

    outs = pl.pallas_call(
        body, name="join_halves_sibling", in_specs=[_ANY] * n, out_specs=[_ANY] * n,
        out_shape=[jax.ShapeDtypeStruct((2,) + h.shape, h.dtype) for h in halves],
        scratch_shapes=[pltpu.SemaphoreType.DMA((n,)), pltpu.SemaphoreType.DMA((n,))],
    )(*halves)
    core = lax.axis_index("c")
    return [lax.dynamic_update_slice_in_dim(o, h[None], core, 0) for o, h in zip(outs, halves)]


def add_own_half(g, got, kind, core, out_dtype, name):
    def body(c_ref, g_r, o_r, out_r):
        out_r[...] = (g_r[...] + o_r[...]).astype(out_r.dtype)

    if isinstance(kind, Packed):
        r, c = kind.slab_half
        tr = _row_tile(r)
        grid = (N_CHIPS, r // tr)
        specs = [pl.BlockSpec((None, None, tr, c), lambda i, j, c_ref: (i, c_ref[0], j, 0)),
                 pl.BlockSpec((None, tr, c), lambda i, j, c_ref: (i, j, 0))]
        out_spec = pl.BlockSpec((None, tr, c), lambda i, j, c_ref: (i, j, 0))
    else:
        h, c4 = kind.got
        tr = _row_tile(h, 128)
        grid = (1, h // tr)
        specs = [pl.BlockSpec((tr, c4), lambda i, j, c_ref: (c_ref[0] * (h // tr) + j, 0)),
                 pl.BlockSpec((tr, c4), lambda i, j, c_ref: (j, 0))]
        out_spec = pl.BlockSpec((tr, c4), lambda i, j, c_ref: (j, 0))
    return pl.pallas_call(
        body, name=name,
        grid_spec=pltpu.PrefetchScalarGridSpec(num_scalar_prefetch=1, grid=grid, in_specs=specs, out_specs=out_spec),
        out_shape=jax.ShapeDtypeStruct(kind.got, out_dtype),
        compiler_params=_cparams(("parallel", "parallel")),
    )(core, g, got)


def sum_slabs(p, name):
    n, r, c = p.shape
    tr = _tile(r, [t for t in (512, 256, 128, 64, 32, 16) if n * t * c * p.dtype.itemsize <= (1 << 23)])

    def body(p_r, out_r):
        acc = p_r[0].astype(F32)
        for k in range(1, n):
            acc = acc + p_r[k].astype(F32)
        out_r[...] = acc

    return pl.pallas_call(
        body, name=name, grid=(r // tr,), in_specs=[pl.BlockSpec((n, tr, c), lambda i: (0, i, 0))],
        out_specs=pl.BlockSpec((tr, c), lambda i: (i, 0)), out_shape=jax.ShapeDtypeStruct((r, c), F32),
        compiler_params=_cparams(("parallel",)),
    )(p)


def _lay(arr, axis, pieces, total, reps=()):
    items = [(d, n, lax.slice_in_dim(arr, s0, s0 + n, axis=axis)) for s0, n, d in pieces]
    items += [(d, n, jnp.repeat(lax.slice_in_dim(arr, s0, s0 + 1, axis=axis), n, axis=axis)) for s0, d, n in reps]
    items.sort(key=lambda t: t[0])
    parts, pos = [], 0

    def zeros(n):
        sh = list(arr.shape)
        sh[axis] = n
        return jnp.zeros(sh, arr.dtype)

    for d, n, v in items:
        if d > pos:
            parts.append(zeros(d - pos))
        parts.append(v)
        pos = d + n
    if total > pos:
        parts.append(zeros(total - pos))
    return jnp.concatenate(parts, axis=axis) if len(parts) > 1 else parts[0]


def _unlay_parts(g, axis, pieces, reps=()):
    out = [(s0, lax.slice_in_dim(g, d, d + n, axis=axis)) for s0, n, d in pieces]
    out += [(s0, jnp.sum(lax.slice_in_dim(g, d, d + n, axis=axis), axis=axis, keepdims=True)) for s0, d, n in reps]
    return out


def _join(parts, axis):
    parts = sorted(parts, key=lambda t: t[0])
    return jnp.concatenate([p for _, p in parts], axis=axis)


def _heads(src0, n_heads, width, padded, dst0=0):
    return [(src0 + h * width, width, dst0 + h * padded) for h in range(n_heads)]


_XQ = lambda src0: _heads(src0, XA_HEADS, XA_HD, LANE)
_XA_W = XA_HEADS * LANE

LAYOUT = {
    'a': dict(
        segs=dict(q=(_heads(0, 4, 96, LANE), 512, ()), k=(_heads(384, 4, 96, LANE), 512, ()),
                  v=(_heads(768, 4, 192, 256), 1024, ()), glr=([(1536, 16, 0)], LANE, ()),
                  og=(_heads(1552, 4, 192, 256), 1024, ()), xq=(_XQ(2320), _XA_W, ())),
        tok=(_heads(0, 4, 192, 256), 1024), xa=(_XQ(768), _XA_W)),
    'b': dict(
        segs=dict(q=([(0, 1536, 0)], 1536, ()), k=([(1536, 1536, 0)], 1536, ()), v=([(3072, 1536, 0)], 1536, ()),
                  xq=(_XQ(4608), _XA_W, ())),
        tok=([(0, 512, 0)], 512), xa=(_XQ(512), _XA_W)),
    'c': dict(
        segs=dict(z=(_heads(0, 12, 64, LANE), 1536, ()),
                  xbc=(_heads(768, 12, 64, LANE) + [(1536, 256, 1536), (1792, 256, 1792)], 2048, ()),
                  dt=([(2048, 12, 0)], LANE, ()),
                  xq=(_XQ(2060), _XA_W, ())),
        tok=(_heads(0, 12, 64, LANE), 1536), xa=(_XQ(768), _XA_W)),
    'd': dict(
        segs=dict(q=([(0, 768, 0)], 768, ()), f=([(768, 768, 0)], 768, ()), i=([(1536, 768, 0)], 768, ()),
                  og=([(2304, 768, 0)], 768, ()), xq=(_XQ(3072), _XA_W, ())),
        tok=([(0, 768, 0)], 768), xa=(_XQ(768), _XA_W)),
}
KINDS = 'abcd'
_XS_PIECES = _heads(0, 12, 64, LANE)
_XBC_PIECES = _XS_PIECES + [(768, 256, 1536), (1024, 256, 1792)]
_HEAD_REPS = tuple((h, h * LANE, LANE) for h in range(12))


def _row(v):
    return v.reshape(1, -1)


LAYER_WEIGHTS = [
    {'w_in': (f'{k}_w_in', None), 'w_out': (f'{k}_w_out', None), 'w_kv': ('xa_w_kv', i), 'w_up': ('ffn_w_up', i),
     'w_down': ('ffn_w_down', i), **({'w_gate2': ('a_w_gate2', None)} if k == 'a' else {})}
    for i, k in enumerate('abcd')]


class LocalLayers:
    def __init__(self, W):
        self.W, self.g = W, {}

    def weights(self, i):
        return {key: (self.W[n] if l is None else self.W[n][l]).astype(MXU_DTYPE)
                for key, (n, l) in LAYER_WEIGHTS[i].items()}

    def fwd_rider(self, i):
        return None

    def bwd_rider(self, i):
        return None

    def grads(self, i, g):
        self.g[i] = g

    def whole_grads(self):
        out = {}
        for i in range(4):
            for key, (n, l) in LAYER_WEIGHTS[i].items():
                if l is None:
                    out[n] = self.g[i][key]
        for n in ('xa_w_kv', 'ffn_w_up', 'ffn_w_down'):
            key = [k for k, (m, _) in LAYER_WEIGHTS[0].items() if m == n][0]
            out[n] = jnp.stack([self.g[i][key] for i in range(4)])
        return out


class ShardedLayers:
    def __init__(self, w, core_id):
        self.core_id = core_id
        self.names, self.axes, self.shards, self.packed, self.kinds = [], [], [], [], []
        for lw in LAYER_WEIGHTS:
            keys = [k for k in lw if k not in ('w_up', 'w_down')]
            sh = {k: (w[lw[k][0]] if lw[k][1] is None else w[lw[k][0]][lw[k][1]]).astype(MXU_DTYPE) for k in lw}
            ax = {k: SHARD_AXIS[lw[k][0]] - (lw[k][1] is not None) for k in lw}
            pk = _pack([sh[k] for k in keys], MXU_DTYPE, 256)
            self.names.append(keys)
            self.axes.append(ax)
            self.shards.append(sh)
            self.packed.append(pk)
            self.kinds.append([Packed(pk.shape), SlabCols(sh['w_up'].shape), Packed(sh['w_down'].shape)])
        self.whole = {}
        self.pending = None
        self.recvd = {}

    def _operands(self, i):
        return [self.packed[i], self.shards[i]['w_up'], self.shards[i]['w_down']]

    def _gathered(self, i, res):
        per_chip = [_unpack(res[0][j], [self.shards[i][k].shape for k in self.names[i]]) for j in range(N_CHIPS)]
        out = {k: _merge_chips(jnp.stack([per_chip[j][n] for j in range(N_CHIPS)]), self.axes[i][k])
               for n, k in enumerate(self.names[i])}
        out['w_up'], out['w_down'] = res[1], res[2].reshape(-1, res[2].shape[-1])
        self.whole[i] = out

    def first_gather(self):
        self._gathered(0, run_rider(GatherRider(self._operands(0), self.kinds[0]), "allgather_chips"))

    def weights(self, i):
        return self.whole[i]

    def fwd_rider(self, i):
        return GatherRider(self._operands(i + 1), self.kinds[i + 1]) if i + 1 < 4 else None

    def fwd_rode(self, i, res):
        self._gathered(i + 1, res)

    def bwd_rider(self, i):
        return ExchangeRider(self.pending[1], self.kinds[self.pending[0]]) if self.pending is not None else None

    def bwd_rode(self, i, res):
        self.recvd[self.pending[0]] = (res, self.pending[1])
        self.pending = None

    def grads(self, i, g):
        kinds = self.kinds[i]
        gb = jnp.stack([_pack([_split_chips(g[k], self.axes[i][k])[j] for k in self.names[i]], F32, 256)
                        for j in range(N_CHIPS)])
        gs = [gb.reshape(N_CHIPS, 2, kinds[0].h, kinds[0].c), g['w_up'],
              g['w_down'].reshape(N_CHIPS, 2, kinds[2].h, kinds[2].c)]
        gots = swap_halves_sibling(gs, kinds, f"swap_halves_{i}")
        self.pending = (i, [add_own_half(a, o, k, self.core_id, GRAD_WIRE_DTYPE, f"add_own_half_{i}_{t}")
                            for t, (a, o, k) in enumerate(zip(gs, gots, kinds))])

    def finish(self, chip_id):
        last, pairs = self.pending
        self.recvd[last] = (run_rider(ExchangeRider(pairs, self.kinds[last]), "exchange_chips"), pairs)
        halves = []
        for i in range(4):
            got, pairs = self.recvd[i]
            halves += [sum_chips(r, p, k, chip_id, f"sum_chips_{i}_{t}")
                       for t, (r, p, k) in enumerate(zip(got, pairs, self.kinds[i]))]
        joined = join_halves_sibling(halves)
        out, stacked = {}, {'xa_w_kv': [], 'ffn_w_up': [], 'ffn_w_down': []}
        for i, lw in enumerate(LAYER_WEIGHTS):
            red, up, down = joined[3 * i:3 * i + 3]
            parts = _unpack(red.reshape(-1, PACK_COLS), [self.shards[i][k].shape for k in self.names[i]])
            parts = dict(zip(self.names[i], parts), w_up=up.reshape(self.shards[i]['w_up'].shape),
                         w_down=down.reshape(self.shards[i]['w_down'].shape))
            for k, (n, l) in lw.items():
                if l is None:
                    out[n] = parts[k]
                else:
                    stacked[n].append(parts[k])
        out.update({n: jnp.stack(v) for n, v in stacked.items()})
        return out


def local_step(x, mem, positions, target, W, layers=None):
    s = x.shape[0]
    grads = {}
    scan_block = CHUNK * SCAN_CHUNKS
    ffn = layers or LocalLayers(W)

    inv_freq = ROPE_THETA ** (-jnp.arange(DIL_HD // 2, dtype=F32) / (DIL_HD // 2))
    ang = positions.astype(F32)[:, None] * inv_freq
    cosf = jnp.concatenate([jnp.cos(ang), jnp.cos(ang)], axis=-1)
    sinf = jnp.concatenate([-jnp.sin(ang), jnp.sin(ang)], axis=-1)

    mem_g = _row(W['mem_norm'])
    (mem_n,) = tmap("mem_norm", _norm_stage, [mem], [mem_g], [(D_MODEL, MXU_DTYPE)])
    kv_lay = _heads(0, 4, 64, LANE) + _heads(256, 4, 64, LANE, dst0=_XA_W)

    saved = []
    for i in range(4):
        kind = KINDS[i]
        lay = LAYOUT[kind]
        sv = dict(x0=x)
        wl = ffn.weights(i)
        w_in, w_out = wl['w_in'], wl['w_out']
        sv['w_seg'] = {n: _lay(w_in, 1, p, t, r).astype(MXU_DTYPE) for n, (p, t, r) in lay['segs'].items()}
        sv['wo_tok'] = _lay(w_out, 0, *lay['tok']).astype(MXU_DTYPE)
        sv['wo_xa'] = _lay(w_out, 0, *lay['xa']).astype(MXU_DTYPE)
        sv['w_kv'] = _lay(wl['w_kv'], 1, kv_lay, 2 * _XA_W).astype(MXU_DTYPE)
        sv['g1'] = _row(W['mix_norm'][i])
        (h,) = tmap(f"mix_norm_{i}", _norm_stage, [x], [sv['g1']], [(D_MODEL, MXU_DTYPE)])
        sv['h'] = h
        seg = dict(zip(sv['w_seg'], matmul_multi(h, list(sv['w_seg'].values()))))
        sv['seg'] = seg

        if kind == 'a':
            sv['w2'] = _lay(_lay(wl['w_gate2'], 1, _heads(0, 4, 96, LANE), 512), 0, [(0, 16, 0)], LANE)
            sv['bg'] = _row(_lay(W['a_b_gate'], 0, _heads(0, 4, 96, LANE), 512))
            sv['on'] = _row(_lay(W['a_o_norm'], 0, [(0, 192, 0)], 256))
            (la,) = tmap("gla_pre", _gla_pre, [seg['glr']], [sv['w2'], sv['bg']], [(512, F32)])
            sv['la'] = la
            sv['scan_fn'] = _gla_step(GLA_HEADS, 2 * LANE, GLA_DK ** -0.5)
            sv['scan_rows'] = [seg['q'], seg['k'], seg['v'], la]
            (o,), sv['states'] = rscan("gla_scan", sv['scan_fn'], [(LANE, 2 * LANE)] * GLA_HEADS, sv['scan_rows'], [],
                                       [(1024, F32)], scan_block)
            sv['o'] = o
            (tok,) = tmap("gla_post", _gla_post, [o, seg['og']], [sv['on']], [(1024, MXU_DTYPE)])
        elif kind == 'b':
            sv['qg'], sv['kg'] = _row(W['b_q_norm']), _row(W['b_k_norm'])
            os_, ls_ = [], []
            qkn = tmap("dil_pre", _dil_pre, [seg['q'], seg['k'], cosf, sinf], [sv['qg'], sv['kg']], [(512, F32)] * 6)
            sv['qn'], sv['kn'] = qkn[:3], qkn[3:]
            for g, (window, r) in enumerate(DIL_GROUPS):
                assert window // r == DIL_BLOCK and (s // r) % DIL_BLOCK == 0
                o, lse = dil_attn(f"dil_attn_{g}", sv['qn'][g], sv['kn'][g], seg['v'], r, g)
                os_.append(o)
                ls_.append(lse)
            sv['os'], sv['ls'] = os_, ls_
            (tok,) = tmap("dil_merge", _dil_merge, os_ + ls_, [], [(512, MXU_DTYPE)])
        elif kind == 'c':
            sv['cw'] = _lay(W['c_conv_w'], 1, _XBC_PIECES, 2048)
            sv['cb'] = _row(_lay(W['c_conv_b'], 0, _XBC_PIECES, 2048))
            sv['dtb'] = _row(_lay(W['c_dt_bias'], 0, [], 1536, _HEAD_REPS))
            sv['alog'] = _row(_lay(W['c_a_log'], 0, [], 1536, _HEAD_REPS))
            sv['dsk'] = _row(_lay(W['c_d'], 0, [], 1536, _HEAD_REPS))
            sv['cn'] = _row(_lay(W['c_norm'], 0, _XS_PIECES, 1536))
            xact = conv_fwd("ssm_conv", seg['xbc'], sv['cw'], sv['cb'], 'silu', F32, 512)
            sv['xact'] = xact
            sv['scan_rows'] = [xact, seg['dt']]
            sv['scan_params'] = [sv['dtb'], sv['alog'], sv['dsk']]
            (yv,), sv['states'] = rscan("ssd_scan", _ssd_step, [(LANE, LANE)] * SSM_HEADS, sv['scan_rows'],
                                        sv['scan_params'], [(1536, F32)], scan_block)
            sv['y'] = yv
            (tok,) = tmap("ssd_post", _mamba_post, [yv, seg['z']], [sv['cn']], [(1536, MXU_DTYPE)])
        else:
            sv['lbnd'] = W['d_lower_bounds']
            sv['on'] = _row(W['d_o_norm'])
            qq, kk, la = tmap("hgrn_pre", _hgrn_pre, [seg['q'], seg['f']], [sv['lbnd']], [(768, F32)] * 3)
            sv['scan_fn'] = _gla_step(HGRN_HEADS, LANE, 1.0)
            sv['scan_rows'] = [qq, kk, seg['i'], la]
            (o,), sv['states'] = rscan("hgrn_scan", sv['scan_fn'], [(LANE, LANE)] * HGRN_HEADS, sv['scan_rows'], [],
                                       [(768, F32)], scan_block)
            sv['o'] = o
            (tok,) = tmap("hgrn_post", _hgrn_post, [o, seg['og']], [sv['on']], [(768, MXU_DTYPE)])
        sv['tok'] = tok

        kv = matmul(mem_n, sv['w_kv'])
        sv['kv'] = kv
        sv['xqg'] = _row(_lay(W['xa_q_norm'][i], 0, [(0, 64, 0)], LANE))
        sv['xkg'] = _row(_lay(W['xa_k_norm'][i], 0, [(0, 64, 0)], LANE))
        (xa,) = tmap(f"xattn_{i}", _xattn, [seg['xq']], [kv, sv['xqg'], sv['xkg']], [(_XA_W, MXU_DTYPE)])
        sv['xa'] = xa
        x = matmul_sum([tok, xa], [sv['wo_tok'], sv['wo_xa']], False, add=x)
        sv['x1'] = x

        sv['g2'] = _row(W['ffn_norm'][i])
        sv['fcw'] = W['ffn_conv_w'][i]
        sv['fcb'] = _row(W['ffn_conv_b'][i])
        (h2,) = tmap(f"ffn_norm_{i}", _norm_stage, [x], [sv['g2']], [(D_MODEL, MXU_DTYPE)])
        sv['h2'] = h2
        w_up, w_down = wl['w_up'], wl['w_down']
        sv['w_up'], sv['w_down'] = w_up, w_down
        u0 = matmul(h2, w_up)
        sv['u0'] = u0
        rider = ffn.fwd_rider(i)
        act = conv_fwd("ffn_conv", u0, sv['fcw'], sv['fcb'], 'glu', MXU_DTYPE, 1408, rider=rider)
        if rider is not None:
            act, rode = act
            ffn.fwd_rode(i, rode)
        sv['act'] = act
        x = matmul(act, w_down, add=x)
        saved.append(sv)

    loss_acc, dx = loss_head(x, target)

    g_stack = {n: [None] * 4 for n in ('mix_norm', 'xa_q_norm', 'xa_k_norm', 'ffn_norm', 'ffn_conv_w', 'ffn_conv_b')}
    d_memn = None
    for i in reversed(range(4)):
        kind = KINDS[i]
        lay = LAYOUT[kind]
        sv = saved[i]
        seg = sv['seg']
        w_up, w_down = sv['w_up'], sv['w_down']
        gl = {}
        dact = matmul(dx, w_down, tb=True)
        gl['w_down'] = matmul(sv['act'], dx, ta=True)
        rider = ffn.bwd_rider(i)
        (du_g, du_v), dcw, dcb, rode = conv_bwd("ffn_conv_bwd", sv['u0'], sv['fcw'], sv['fcb'], dact, 'glu', 1408,
                                                rider=rider)
        if rider is not None:
            ffn.bwd_rode(i, rode)
        g_stack['ffn_conv_w'][i], g_stack['ffn_conv_b'][i] = dcw, dcb[0]
        dh2 = matmul(du_v, w_up, tb=True, b_koff=D_FF, add=matmul(du_g, w_up, tb=True))
        g_up = jnp.zeros((1,) + w_up.shape, F32)
        g_up = matmul(sv['h2'], du_g, ta=True, into=(g_up, 0, 0))
        g_up = matmul(sv['h2'], du_v, ta=True, into=(g_up, 0, D_FF))
        gl['w_up'] = g_up[0]
        (dx,), (dg2,) = tmap_bwd(f"ffn_norm_bwd_{i}", _norm_stage, [sv['x1']], [sv['g2']], [dh2], [True], {0: dx})
        g_stack['ffn_norm'][i] = dg2[0]
        dtok = matmul(dx, sv['wo_tok'], tb=True)
        dxa = matmul(dx, sv['wo_xa'], tb=True)
        g_wo = _unlay_parts(matmul(sv['tok'], dx, ta=True), 0, lay['tok'][0]) \
            + _unlay_parts(matmul(sv['xa'], dx, ta=True), 0, lay['xa'][0])
        gl['w_out'] = _join(g_wo, 0)
        (dxq,), (dkv, dqg, dkg) = tmap_bwd(f"xattn_bwd_{i}", _xattn, [seg['xq']], [sv['kv'], sv['xqg'], sv['xkg']],
                                           [dxa], [True])
        g_stack['xa_q_norm'][i], g_stack['xa_k_norm'][i] = dqg[0, :XA_HD], dkg[0, :XA_HD]
        gl['w_kv'] = _join(_unlay_parts(matmul(mem_n, dkv, ta=True), 1, kv_lay), 1)
        d_memn = matmul(dkv, sv['w_kv'], tb=True, add=d_memn)
        dseg = dict(xq=dxq)
        if kind == 'a':
            (do, dog), (don,) = tmap_bwd("gla_post_bwd", _gla_post, [sv['o'], seg['og']], [sv['on']], [dtok],
                                         [True, True], grad_dtype=F32)
            grads['a_o_norm'] = don[0, :GLA_DV]
            (dq, dk, dv, dla), _ = rscan_bwd("gla_scan_bwd", sv['scan_fn'], sv['states'], sv['scan_rows'], [], [do],
                                             scan_block, grad_dtype=F32)
            (dglr,), (dw2, dbg) = tmap_bwd("gla_pre_bwd", _gla_pre, [seg['glr']], [sv['w2'], sv['bg']], [dla], [True])
            gl['w_gate2'] = _join(_unlay_parts(dw2[:GLA_RANK], 1, _heads(0, 4, 96, LANE)), 1)
            grads['a_b_gate'] = _join(_unlay_parts(dbg[0], 0, _heads(0, 4, 96, LANE)), 0)
            dseg.update(q=dq, k=dk, v=dv, glr=dglr, og=dog)
        elif kind == 'b':
            res, _ = tmap_bwd("dil_merge_bwd", _dil_merge, sv['os'] + sv['ls'], [], [dtok], [True] * 6, grad_dtype=F32)
            dqn, dkn, dvs = [], [], []
            for g, (_, r) in enumerate(DIL_GROUPS):
                a_, b_, c_ = dil_attn_bwd(f"dil_attn_bwd_{g}", sv['qn'][g], sv['kn'][g], seg['v'], res[g], res[3 + g],
                                          r, g)
                dqn.append(a_)
                dkn.append(b_)
                dvs.append(c_)
            (dq, dk), (dqg, dkg) = tmap_bwd("dil_pre_bwd", _dil_pre, [seg['q'], seg['k'], cosf, sinf],
                                            [sv['qg'], sv['kg']], dqn + dkn, [True, True, False, False])
            dseg.update(q=dq, k=dk, v=jnp.concatenate(dvs, axis=1))
            grads['b_q_norm'], grads['b_k_norm'] = dqg[0], dkg[0]
        elif kind == 'c':
            (dy, dz), (dcn,) = tmap_bwd("ssd_post_bwd", _mamba_post, [sv['y'], seg['z']], [sv['cn']], [dtok],
                                        [True, True], grad_dtype=F32)
            grads['c_norm'] = _join(_unlay_parts(dcn[0], 0, _XS_PIECES), 0)
            (dxact, ddt), (ddtb, dalog, ddsk) = rscan_bwd("ssd_scan_bwd", _ssd_step, sv['states'], sv['scan_rows'],
                                                          sv['scan_params'], [dy], scan_block, grad_dtype=F32)
            for nm, gv in (('c_dt_bias', ddtb), ('c_a_log', dalog), ('c_d', ddsk)):
                grads[nm] = _join(_unlay_parts(gv[0], 0, [], _HEAD_REPS), 0)
            (dxbc,), dcw, dcb, _ = conv_bwd("ssm_conv_bwd", seg['xbc'], sv['cw'], sv['cb'], dxact, 'silu', 512)
            grads['c_conv_w'] = _join(_unlay_parts(dcw, 1, _XBC_PIECES), 1)
            grads['c_conv_b'] = _join(_unlay_parts(dcb[0], 0, _XBC_PIECES), 0)
            dseg.update(z=dz, xbc=dxbc, dt=ddt)
        else:
            (do, dog), (don,) = tmap_bwd("hgrn_post_bwd", _hgrn_post, [sv['o'], seg['og']], [sv['on']], [dtok],
                                         [True, True], grad_dtype=F32)
            grads['d_o_norm'] = don[0]
            (dqq, dkk, di, dla), _ = rscan_bwd("hgrn_scan_bwd", sv['scan_fn'], sv['states'], sv['scan_rows'], [], [do],
                                               scan_block, grad_dtype=F32)
            (dq, df), (dlb,) = tmap_bwd("hgrn_pre_bwd", _hgrn_pre, [seg['q'], seg['f']], [sv['lbnd']], [dqq, dkk, dla],
                                        [True, True])
            grads['d_lower_bounds'] = dlb
            dseg.update(q=dq, f=df, i=di, og=dog)
        names = list(lay['segs'])
        dh = matmul_sum([dseg[n] for n in names], [sv['w_seg'][n] for n in names], True)
        g_in = []
        for n, (p, t, rp) in lay['segs'].items():
            g_in += _unlay_parts(matmul(sv['h'], dseg[n], ta=True), 1, p, rp)
        gl['w_in'] = _join(g_in, 1)
        ffn.grads(i, gl)
        (dx,), (dg1,) = tmap_bwd(f"mix_norm_bwd_{i}", _norm_stage, [sv['x0']], [sv['g1']], [dh], [True], {0: dx})
        g_stack['mix_norm'][i] = dg1[0]

    _, (dmg,) = tmap_bwd("mem_norm_bwd", _norm_stage, [mem], [mem_g], [d_memn], [False])
    grads['mem_norm'] = dmg[0]
    for n, parts in g_stack.items():
        grads[n] = jnp.stack(parts)
    if isinstance(ffn, LocalLayers):
        grads.update(ffn.whole_grads())
    return loss_acc, dx, grads


def _pack(arrs, dtype, row_multiple=PACK_ROWS):
    parts, rows = [], 0
    for a in arrs:
        f = a.reshape(-1).astype(dtype)
        unit = PACK_ROWS * PACK_COLS
        pad = (-f.shape[0]) % unit
        if pad:
            f = jnp.concatenate([f, jnp.zeros((pad,), dtype)])
        parts.append(f.reshape(-1, PACK_COLS))
        rows += parts[-1].shape[0]
    if rows % row_multiple:
        parts.append(jnp.zeros((row_multiple - rows % row_multiple, PACK_COLS), dtype))
    return jnp.concatenate(parts, axis=0)


def _unpack(buf, shapes):
    out, row = [], 0
    for sh in shapes:
        n = int(np.prod(sh))
        rows = -(-n // (PACK_ROWS * PACK_COLS)) * PACK_ROWS
        out.append(buf[row:row + rows].reshape(-1)[:n].reshape(sh))
        row += rows
    return out


def _pack_rows(arrs):
    parts = []
    for a in arrs:
        f = a.reshape(-1).astype(F32)
        parts.append(jnp.pad(f, (0, (-f.shape[0]) % PACK_COLS)))
    flat = jnp.concatenate(parts)
    rows = flat.shape[0] // PACK_COLS
    return jnp.pad(flat, (0, (-rows % 16) * PACK_COLS)).reshape(-1, PACK_COLS)


def _unpack_rows(buf, shapes):
    flat, out, pos = buf.reshape(-1), [], 0
    for sh in shapes:
        n = int(np.prod(sh))
        out.append(flat[pos:pos + n].reshape(sh))
        pos += -(-n // PACK_COLS) * PACK_COLS
    return out


def _split_chips(a, axis):
    sh = a.shape
    return jnp.moveaxis(a.reshape(sh[:axis] + (N_CHIPS, sh[axis] // N_CHIPS) + sh[axis + 1:]), axis, 0)


def _merge_chips(a, axis):
    a = jnp.moveaxis(a, 0, axis)
    sh = a.shape
    return a.reshape(sh[:axis] + (sh[axis] * sh[axis + 1],) + sh[axis + 2:])


def kernel(x, mem, positions, mem_norm, mix_norm, xa_w_kv, xa_q_norm, xa_k_norm, ffn_norm, ffn_w_up, ffn_conv_w, ffn_conv_b, ffn_w_down, a_w_in, a_w_gate2, a_b_gate, a_o_norm, a_w_out, b_w_in, b_q_norm, b_k_norm, b_w_out, c_w_in, c_conv_w, c_conv_b, c_dt_bias, c_a_log, c_d, c_norm, c_w_out, d_w_in, d_lower_bounds, d_o_norm, d_w_out, loss_target, m_mem_norm, m_mix_norm, m_xa_w_kv, m_xa_q_norm, m_xa_k_norm, m_ffn_norm, m_ffn_w_up, m_ffn_conv_w, m_ffn_conv_b, m_ffn_w_down, m_a_w_in, m_a_w_gate2, m_a_b_gate, m_a_o_norm, m_a_w_out, m_b_w_in, m_b_q_norm, m_b_k_norm, m_b_w_out, m_c_w_in, m_c_conv_w, m_c_conv_b, m_c_dt_bias, m_c_a_log, m_c_d, m_c_norm, m_c_w_out, m_d_w_in, m_d_lower_bounds, m_d_o_norm, m_d_w_out, v_mem_norm, v_mix_norm, v_xa_w_kv, v_xa_q_norm, v_xa_k_norm, v_ffn_norm, v_ffn_w_up, v_ffn_conv_w, v_ffn_conv_b, v_ffn_w_down, v_a_w_in, v_a_w_gate2, v_a_b_gate, v_a_o_norm, v_a_w_out, v_b_w_in, v_b_q_norm, v_b_k_norm, v_b_w_out, v_c_w_in, v_c_conv_w, v_c_conv_b, v_c_dt_bias, v_c_a_log, v_c_d, v_c_norm, v_c_w_out, v_d_w_in, v_d_lower_bounds, v_d_o_norm, v_d_w_out):
    args = locals()
    w = {n: args[n] for n in WEIGHTS}
    m = {n: args['m_' + n] for n in WEIGHTS}
    v = {n: args['v_' + n] for n in WEIGHTS}
    cx, cy, cc = lax.axis_index("x"), lax.axis_index("y"), lax.axis_index("c")
    chip = 2 * cx + cy

    core_id, chip_id = cc.reshape(1).astype(jnp.int32), chip.reshape(1).astype(jnp.int32)
    layers = ShardedLayers(w, core_id)
    layers.first_gather()
    full = {}
    small_sharded = [n for n in SMALL if n in SHARD_AXIS]
    sg = allgather_devices(_pack([w[n] for n in small_sharded], F32))
    per_chip_s = [_unpack(sg[2 * j], [w[n].shape for n in small_sharded]) for j in range(N_CHIPS)]
    for k, n in enumerate(small_sharded):
        full[n] = _merge_chips(jnp.stack([per_chip_s[j][k] for j in range(N_CHIPS)]), SHARD_AXIS[n])
    for n in SMALL:
        if n not in SHARD_AXIS:
            full[n] = w[n]

    loss_acc, dx, grads = local_step(x[0], mem[0], positions[0], loss_target[0], full, layers)
    loss = lax.psum(jnp.sum(loss_acc), ("x", "y", "c"))

    g_big = layers.finish(chip_id)

    small_full_shapes = [grads[n].shape for n in SMALL]
    gs = sum_slabs(allgather_devices(_pack_rows([grads[n] for n in SMALL])), "sum_devices")
    g_small = {}
    for n, gfull in zip(SMALL, _unpack_rows(gs, small_full_shapes)):
        if n in SHARD_AXIS:
            ax = SHARD_AXIS[n]
            size = gfull.shape[ax] // N_CHIPS
            gfull = lax.dynamic_slice_in_dim(gfull, chip * size, size, axis=ax)
        g_small[n] = gfull

    g_out, delta, new_m, new_v = {**g_big, **g_small}, {}, {}, {}
    for n in WEIGHTS:
        sh = w[n].shape
        two_d = (-1, sh[-1])
        d_, m_, v_ = adamw(f"adamw_{n}", w[n].reshape(two_d), g_out[n].reshape(two_d), m[n].reshape(two_d),
                           v[n].reshape(two_d))
        delta[n], new_m[n], new_v[n] = d_.reshape(sh), m_.reshape(sh), v_.reshape(sh)

    return (loss, dx[None], *[g_out[n] for n in WEIGHTS], *[delta[n] for n in WEIGHTS],
            *[new_m[n] for n in WEIGHTS], *[new_v[n] for n in WEIGHTS])
```

```python
import functools
import math

import jax
import jax.numpy as jnp
import numpy as np
from jax import lax
from jax.experimental import pallas as pl
from jax.experimental.pallas import tpu as pltpu

F32 = jnp.float32
MXU_DTYPE = jnp.bfloat16
GRAD_WIRE_DTYPE = jnp.bfloat16
VMEM_LIMIT_V7X = 56 * 1024 * 1024
LANE = 128
SUBLANE = 8

D_MODEL = 1024
N_MEM = 256
EPS = 1e-6
ROPE_THETA = 10000.0
CHUNK = 64
XA_HEADS, XA_HD = 4, 64
GLA_HEADS, GLA_DK, GLA_DV, GLA_RANK, GLA_GATE_NORM = 4, 96, 192, 16, 16.0
DIL_GROUPS = ((128, 1), (512, 4), (2048, 16))
DIL_HEADS, DIL_HD, DIL_BLOCK = 4, 128, 128
SSM_HD, SSM_HEADS, SSM_GROUPS, SSM_STATE, SSM_CONV = 64, 12, 2, 128, 4
HGRN_HEADS, HGRN_DK = 6, 128
D_FF = 2816
FFN_CONV = 3
ADAM_LR, ADAM_B1, ADAM_B2, ADAM_EPS, ADAM_WD, ADAM_STEP = 0.001, 0.9, 0.999, 1e-08, 0.01, 10

MM_TILES = (2816, 1408, 1024, 768, 512, 384, 256, 128)
MM_K_TILES = (2816, 2048, 1536, 1408, 1024, 768, 512, 384, 256, 128)
MM_MIN_OUT_TILE = 512 * 1024
MM_VMEM_BUDGET = 40 * 1024 * 1024
ROW_BLOCK = 256
SCAN_CHUNKS = 2
PACK_COLS = 1024
PACK_ROWS = 32

WEIGHTS = ['mem_norm', 'mix_norm', 'xa_w_kv', 'xa_q_norm', 'xa_k_norm', 'ffn_norm', 'ffn_w_up', 'ffn_conv_w',
           'ffn_conv_b', 'ffn_w_down', 'a_w_in', 'a_w_gate2', 'a_b_gate', 'a_o_norm', 'a_w_out', 'b_w_in', 'b_q_norm',
           'b_k_norm', 'b_w_out', 'c_w_in', 'c_conv_w', 'c_conv_b', 'c_dt_bias', 'c_a_log', 'c_d', 'c_norm', 'c_w_out',
           'd_w_in', 'd_lower_bounds', 'd_o_norm', 'd_w_out']
SHARD_AXIS = {'xa_w_kv': 1, 'ffn_w_up': 2, 'ffn_conv_w': 2, 'ffn_w_down': 1, 'a_w_in': 1, 'a_w_gate2': 1, 'a_w_out': 0,
              'b_w_in': 1, 'b_w_out': 1, 'c_w_in': 1, 'c_conv_w': 1, 'c_w_out': 0, 'd_w_in': 1, 'd_w_out': 0}
BIG = ['xa_w_kv', 'ffn_w_up', 'ffn_w_down', 'a_w_in', 'a_w_gate2', 'a_w_out', 'b_w_in', 'b_w_out', 'c_w_in', 'c_w_out',
       'd_w_in', 'd_w_out']
SMALL = [n for n in WEIGHTS if n not in BIG]
LAYERED = ['ffn_w_up', 'ffn_w_down']
N_CHIPS = 4
N_DEV = 8


class _MatmulSet:
    def __init__(self, cast, precision):
        def dot(a, b, dims):
            if cast:
                a = a.astype(MXU_DTYPE)
                b = b.astype(MXU_DTYPE)
            return lax.dot_general(a, b, (dims, ((), ())), precision=precision, preferred_element_type=F32)

        @jax.custom_vjp
        def nn(a, b):
            return dot(a, b, ((1,), (0,)))

        @jax.custom_vjp
        def nt(a, b):
            return dot(a, b, ((1,), (1,)))

        @jax.custom_vjp
        def tn(a, b):
            return dot(a, b, ((0,), (0,)))

        nn.defvjp(lambda a, b: (nn(a, b), (a, b)), lambda r, g: (nt(g, r[1]), tn(r[0], g)))
        nt.defvjp(lambda a, b: (nt(a, b), (a, b)), lambda r, g: (nn(g, r[1]), tn(g, r[0])))
        tn.defvjp(lambda a, b: (tn(a, b), (a, b)), lambda r, g: (nt(r[1], g), nn(r[0], g)))
        self.nn, self.nt, self.tn = nn, nt, tn


mm = _MatmulSet(True, None)
hi = _MatmulSet(False, lax.Precision.HIGHEST)


def _sigmoid(x):
    return jax.nn.sigmoid(x)


def _silu(x):
    return x * jax.nn.sigmoid(x)


def _softplus(x):
    return jnp.maximum(x, 0.0) + jnp.log1p(jnp.exp(-jnp.abs(x)))


def _rms(x, g, n_real=None):
    n = n_real or x.shape[-1]
    ms = jnp.sum(x * x, axis=-1, keepdims=True) * (1.0 / n)
    return x * lax.rsqrt(ms + EPS) * g


@jax.custom_vjp
def _swap_halves(x):
    return pltpu.roll(x, 64, 1)


_swap_halves.defvjp(lambda x: (_swap_halves(x), None), lambda _, g: (_swap_halves(g),))


def _tile(n, cands):
    for c in cands:
        if n % c == 0:
            return c
    raise ValueError(f"no tile for {n} among {cands}")


def _cparams(sem):
    return pltpu.CompilerParams(dimension_semantics=sem, vmem_limit_bytes=VMEM_LIMIT_V7X)


def _f32(v):
    return v.astype(F32) if jnp.issubdtype(v.dtype, jnp.floating) else v


def matmul(a, b, *, ta=False, tb=False, add=None, out_dtype=F32, b_layer=None, b_koff=0, into=None):
    m, k = (a.shape[1], a.shape[0]) if ta else a.shape
    b2 = b.shape[1:] if b_layer is not None else b.shape
    n = b2[0] if tb else b2[1]
    assert b_koff + k <= (b2[1] if tb else b2[0]), (a.shape, b.shape, ta, tb, b_koff)
    sa, sb, so = a.dtype.itemsize, b.dtype.itemsize, jnp.dtype(out_dtype).itemsize
    n_align = math.gcd(n, into[2]) if into is not None and into[2] else n
    k_align = math.gcd(k, b_koff) if b_koff else k

    def vmem(tm_, tn_, tk_):
        return (2 * tm_ * tk_ * sa + 2 * tk_ * tn_ * sb + 2 * tm_ * tn_ * so + (tm_ * tn_ * 4 if tk_ < k else 0)
                + (2 * tm_ * tn_ * add.dtype.itemsize if add is not None else 0))

    for tk in [t for t in MM_K_TILES if k_align % t == 0]:
        fits = [(tm_ * tn_, tm_, tn_) for tm_ in MM_TILES if m % tm_ == 0 for tn_ in MM_TILES
                if n % tn_ == 0 and n_align % tn_ == 0 and vmem(tm_, tn_, tk) <= MM_VMEM_BUDGET]
        if fits and (max(fits)[0] >= min(MM_MIN_OUT_TILE, m * n) or tk == MM_K_TILES[-1]):
            break
    _, tm, tn = max(fits)
    nk = k // tk
    dims = (((0,) if ta else (1,)), ((1,) if tb else (0,)))
    n_extra = (add is not None) + (into is not None)

    def body(*refs):
        a_ref, b_ref = refs[0], refs[1]
        add_ref = refs[2] if add is not None else None
        o_ref = refs[2 + n_extra]
        part = lax.dot_general(a_ref[...].astype(MXU_DTYPE), b_ref[...].astype(MXU_DTYPE), (dims, ((), ())),
                               preferred_element_type=F32)

        def finish(r):
            if add_ref is not None:
                r = r + add_ref[...].astype(F32)
            o_ref[...] = r.astype(o_ref.dtype)

        if nk == 1:
            finish(part)
            return
        acc = refs[-1]
        kk = pl.program_id(2)

        @pl.when(kk == 0)
        def _():
            acc[...] = part

        @pl.when(kk > 0)
        def _():
            acc[...] += part

        @pl.when(kk == nk - 1)
        def _():
            finish(acc[...])

    a_spec = pl.BlockSpec((tk, tm), lambda i, j, q: (q, i)) if ta else pl.BlockSpec((tm, tk), lambda i, j, q: (i, q))
    ko = b_koff // tk
    if b_layer is None:
        b_spec = (pl.BlockSpec((tn, tk), lambda i, j, q: (j, q + ko)) if tb
                  else pl.BlockSpec((tk, tn), lambda i, j, q: (q + ko, j)))
    else:
        b_spec = (pl.BlockSpec((None, tn, tk), lambda i, j, q: (b_layer, j, q + ko)) if tb
                  else pl.BlockSpec((None, tk, tn), lambda i, j, q: (b_layer, q + ko, j)))
    o_spec = pl.BlockSpec((tm, tn), lambda i, j, q: (i, j))
    ins, specs = [a, b], [a_spec, b_spec]
    if add is not None:
        ins.append(add)
        specs.append(o_spec)
    aliases = {}
    out_shape = jax.ShapeDtypeStruct((m, n), out_dtype)
    if into is not None:
        buf, layer, col0 = into
        assert buf.shape[1] == m and buf.dtype == out_dtype
        co = col0 // tn
        ins.append(buf)
        specs.append(_ANY)
        aliases = {len(ins) - 1: 0}
        o_spec = pl.BlockSpec((None, tm, tn), lambda i, j, q: (layer, i, j + co))
        out_shape = jax.ShapeDtypeStruct(buf.shape, buf.dtype)
    return pl.pallas_call(
        body, name=f"mm_{m}x{k}x{n}_{int(ta)}{int(tb)}{int(add is not None)}{int(b_layer is not None)}{int(into is not None)}",
        grid=(m // tm, n // tn, nk), in_specs=specs, out_specs=o_spec, out_shape=out_shape,
        input_output_aliases=aliases,
        scratch_shapes=[pltpu.VMEM((tm, tn), F32)] if nk > 1 else [],
        compiler_params=_cparams(("parallel", "parallel", "arbitrary")),
    )(*ins)


def _resident_tm(m, row_bytes, resident_bytes):
    for tm in (512, 256, 128):
        if m % tm == 0 and 2 * (resident_bytes + tm * row_bytes) <= MM_VMEM_BUDGET:
            return tm
    return 128


def matmul_sum(a_list, b_list, tb, add=None):
    n_ops = len(a_list)
    m, n = a_list[0].shape[0], b_list[0].shape[0 if tb else 1]
    dims = ((1,), (1,) if tb else (0,))
    tm = _resident_tm(m, sum(a.shape[1] * a.dtype.itemsize for a in a_list) + n * 4 * (1 + (add is not None)),
                      sum(b.size * b.dtype.itemsize for b in b_list))

    def body(*refs):
        acc = refs[2 * n_ops][...] if add is not None else None
        for t in range(n_ops):
            part = lax.dot_general(refs[t][...].astype(MXU_DTYPE), refs[n_ops + t][...].astype(MXU_DTYPE),
                                   (dims, ((), ())), preferred_element_type=F32)
            acc = part if acc is None else acc + part
        refs[-1][...] = acc

    o_spec = pl.BlockSpec((tm, n), lambda i: (i, 0))
    return pl.pallas_call(
        body, name=f"mm_sum_{n_ops}x{sum(a.shape[1] for a in a_list)}_{int(tb)}{int(add is not None)}", grid=(m // tm,),
        in_specs=[pl.BlockSpec((tm, a.shape[1]), lambda i: (i, 0)) for a in a_list] + [_whole_spec(b) for b in b_list]
        + ([o_spec] if add is not None else []),
        out_specs=o_spec, out_shape=jax.ShapeDtypeStruct((m, n), F32),
        compiler_params=_cparams(("parallel",)),
    )(*a_list, *b_list, *([add] if add is not None else []))


def matmul_multi(a, b_list):
    n_ops = len(b_list)
    m = a.shape[0]
    tm = _resident_tm(m, a.shape[1] * a.dtype.itemsize + 4 * sum(b.shape[1] for b in b_list),
                      sum(b.size * b.dtype.itemsize for b in b_list))

    def body(*refs):
        av = refs[0][...].astype(MXU_DTYPE)
        for t in range(n_ops):
            refs[1 + n_ops + t][...] = jnp.dot(av, refs[1 + t][...].astype(MXU_DTYPE), preferred_element_type=F32)

    return pl.pallas_call(
        body, name=f"mm_multi_{n_ops}x{sum(b.shape[1] for b in b_list)}", grid=(m // tm,),
        in_specs=[pl.BlockSpec((tm, a.shape[1]), lambda i: (i, 0))] + [_whole_spec(b) for b in b_list],
        out_specs=[pl.BlockSpec((tm, b.shape[1]), lambda i: (i, 0)) for b in b_list],
        out_shape=[jax.ShapeDtypeStruct((m, b.shape[1]), F32) for b in b_list],
        compiler_params=_cparams(("parallel",)),
    )(a, *b_list)


def _row_spec(a, block):
    return pl.BlockSpec((block, a.shape[1]), lambda i: (i, 0))


def _whole_spec(a):
    return pl.BlockSpec(a.shape, lambda i: (0,) * a.ndim)


def tmap(name, fn, rows, params, outs, block=ROW_BLOCK):
    s = rows[0].shape[0]
    block = min(block, s)
    nr, npar = len(rows), len(params)

    def body(*refs):
        res = fn(*[_f32(r[...]) for r in refs[:nr]], *[_f32(p[...]) for p in refs[nr:nr + npar]])
        for o_ref, v in zip(refs[nr + npar:], res, strict=True):
            o_ref[...] = v.astype(o_ref.dtype)

    return pl.pallas_call(
        body, name=name, grid=(s // block,),
        in_specs=[_row_spec(a, block) for a in rows] + [_whole_spec(p) for p in params],
        out_specs=[pl.BlockSpec((block, w), lambda i: (i, 0)) for w, _ in outs],
        out_shape=[jax.ShapeDtypeStruct((s, w), dt) for w, dt in outs],
        compiler_params=_cparams(("parallel",)),
    )(*rows, *params)


def tmap_bwd(name, fn, rows, params, douts, row_grad, row_add=None, grad_dtype=None, block=ROW_BLOCK, rider=None):
    s = rows[0].shape[0]
    block = min(block, s)
    grad_dtype = grad_dtype or MXU_DTYPE
    nr, npar, nd = len(rows), len(params), len(douts)
    gr = [i for i in range(nr) if row_grad[i]]
    row_add = row_add or {}
    adds = [row_add[i] for i in gr if i in row_add]

    def body(*refs):
        rv = [_f32(r[...]) for r in refs[:nr]]
        pv = [_f32(p[...]) for p in refs[nr:nr + npar]]
        dv = tuple(_f32(d[...]) for d in refs[nr + npar:nr + npar + nd])
        add_refs = list(refs[nr + npar + nd:nr + npar + nd + len(adds)])
        out_refs = refs[nr + npar + nd + len(adds):]

        def f(*diff):
            rr = list(rv)
            for n_, i_ in enumerate(gr):
                rr[i_] = diff[n_]
            return tuple(fn(*rr, *diff[len(gr):]))

        _, vjp = jax.vjp(f, *[rv[i_] for i_ in gr], *pv)
        g = vjp(dv)
        for n_, i_ in enumerate(gr):
            v = g[n_]
            if i_ in row_add:
                v = v + add_refs.pop(0)[...].astype(F32)
            out_refs[n_][...] = v.astype(out_refs[n_].dtype)
        first = pl.program_id(0) == 0
        for n_ in range(npar):
            ref = out_refs[len(gr) + n_]

            @pl.when(first)
            def _(ref=ref):
                ref[...] = jnp.zeros_like(ref)

            ref[...] += g[len(gr) + n_]

    grid = (s // block,)
    n_out = len(gr) + npar
    body, r_ins, r_in_specs, r_out_specs, r_out_shapes, r_scratch, _ = _ride(
        body, rider, nr + npar + nd + len(adds), n_out, 0, grid)
    res = pl.pallas_call(
        body, name=name, grid=grid,
        in_specs=[_row_spec(a, block) for a in rows] + [_whole_spec(p) for p in params]
        + [_row_spec(d, block) for d in douts] + [_row_spec(a, block) for a in adds] + r_in_specs,
        out_specs=[_row_spec(rows[i], block) for i in gr] + [_whole_spec(p) for p in params] + r_out_specs,
        out_shape=[jax.ShapeDtypeStruct(rows[i].shape, F32 if i in row_add else grad_dtype) for i in gr]
        + [jax.ShapeDtypeStruct(p.shape, F32) for p in params] + r_out_shapes,
        scratch_shapes=r_scratch, compiler_params=_cparams(("arbitrary",)),
    )(*rows, *params, *douts, *adds, *r_ins)
    if rider is not None:
        return list(res[:len(gr)]), list(res[len(gr):n_out]), rider.results(res[n_out:])
    return list(res[:len(gr)]), list(res[len(gr):])


def rscan(name, fn, state_shapes, rows, params, outs, block):
    s = rows[0].shape[0]
    nsteps = s // block
    nr, npar, no, ns = len(rows), len(params), len(outs), len(state_shapes)

    def body(*refs):
        out_refs = refs[nr + npar:nr + npar + no]
        sav_refs = refs[nr + npar + no:nr + npar + no + ns]
        st_refs = refs[nr + npar + no + ns:]

        @pl.when(pl.program_id(0) == 0)
        def _():
            for st in st_refs:
                st[...] = jnp.zeros_like(st)

        sts = tuple(st[...] for st in st_refs)
        for sv, v in zip(sav_refs, sts):
            sv[...] = v
        new, res = fn(sts, *[_f32(r[...]) for r in refs[:nr]], *[_f32(p[...]) for p in refs[nr:nr + npar]])
        for st, v in zip(st_refs, new, strict=True):
            st[...] = v
        for o_ref, v in zip(out_refs, res, strict=True):
            o_ref[...] = v.astype(o_ref.dtype)

    res = pl.pallas_call(
        body, name=name, grid=(nsteps,),
        in_specs=[_row_spec(a, block) for a in rows] + [_whole_spec(p) for p in params],
        out_specs=[pl.BlockSpec((block, w), lambda i: (i, 0)) for w, _ in outs]
        + [pl.BlockSpec(sh, lambda i: (i, 0)) for sh in state_shapes],
        out_shape=[jax.ShapeDtypeStruct((s, w), dt) for w, dt in outs]
        + [jax.ShapeDtypeStruct((nsteps * sh[0], sh[1]), F32) for sh in state_shapes],
        scratch_shapes=[pltpu.VMEM(sh, F32) for sh in state_shapes],
        compiler_params=_cparams(("arbitrary",)),
    )(*rows, *params)
    return list(res[:no]), list(res[no:])


def rscan_bwd(name, fn, saved, rows, params, douts, block, grad_dtype=None):
    s = rows[0].shape[0]
    nsteps = s // block
    grad_dtype = grad_dtype or MXU_DTYPE
    nr, npar, nd, ns = len(rows), len(params), len(douts), len(saved)
    state_shapes = [(sv.shape[0] // nsteps, sv.shape[1]) for sv in saved]

    def body(*refs):
        rv = [_f32(r[...]) for r in refs[:nr]]
        pv = [_f32(p[...]) for p in refs[nr:nr + npar]]
        dv = tuple(_f32(d[...]) for d in refs[nr + npar:nr + npar + nd])
        sv = tuple(x[...] for x in refs[nr + npar + nd:nr + npar + nd + ns])
        out_refs = refs[nr + npar + nd + ns:nr + npar + nd + ns + nr + npar]
        dst_refs = refs[nr + npar + nd + ns + nr + npar:]
        first = pl.program_id(0) == 0

        @pl.when(first)
        def _():
            for d in dst_refs:
                d[...] = jnp.zeros_like(d)

        def f(sts, *args):
            return fn(sts, *args)

        _, vjp = jax.vjp(f, sv, *rv, *pv)
        g = vjp((tuple(d[...] for d in dst_refs), dv))
        for d, v in zip(dst_refs, g[0], strict=True):
            d[...] = v
        for n_ in range(nr):
            out_refs[n_][...] = g[1 + n_].astype(out_refs[n_].dtype)
        for n_ in range(npar):
            ref = out_refs[nr + n_]

            @pl.when(first)
            def _(ref=ref):
                ref[...] = jnp.zeros_like(ref)

            ref[...] += g[1 + nr + n_]

    rev = lambda i: (nsteps - 1 - i, 0)
    res = pl.pallas_call(
        body, name=name, grid=(nsteps,),
        in_specs=[pl.BlockSpec((block, a.shape[1]), rev) for a in rows] + [_whole_spec(p) for p in params]
        + [pl.BlockSpec((block, d.shape[1]), rev) for d in douts] + [pl.BlockSpec(sh, rev) for sh in state_shapes],
        out_specs=[pl.BlockSpec((block, a.shape[1]), rev) for a in rows] + [_whole_spec(p) for p in params],
        out_shape=[jax.ShapeDtypeStruct(a.shape, grad_dtype) for a in rows]
        + [jax.ShapeDtypeStruct(p.shape, F32) for p in params],
        scratch_shapes=[pltpu.VMEM(sh, F32) for sh in state_shapes],
        compiler_params=_cparams(("arbitrary",)),
    )(*rows, *params, *douts, *saved)
    return list(res[:nr]), list(res[nr:])


def _norm_stage(x, g):
    return (_rms(x, g),)


def _tril():
    r = lax.broadcasted_iota(jnp.int32, (CHUNK, CHUNK), 0)
    c = lax.broadcasted_iota(jnp.int32, (CHUNK, CHUNK), 1)
    return r >= c


def _gla_chunk(st, q, k, v, la, b):
    tril = _tril()
    rowi = lax.broadcasted_iota(jnp.int32, (CHUNK, 1), 0)
    b_last = jnp.sum(la, axis=0, keepdims=True)
    b_ref = jnp.sum(jnp.where(rowi < CHUNK // 2, la, 0.0), axis=0, keepdims=True)
    att = mm.nt(q * jnp.exp(b - b_ref), k * jnp.exp(b_ref - b))
    att = jnp.where(tril, att, 0.0)
    o = mm.nn(att, v) + mm.nn(q * jnp.exp(b), st)
    decay = jnp.exp(jnp.broadcast_to(b_last, (LANE, LANE)).T)
    decay = jnp.concatenate([decay] * (v.shape[1] // LANE), axis=1)
    st2 = decay * st + mm.tn(k * jnp.exp(b_last - b), v)
    return st2, o


def _gla_step(heads, vp, scale):
    kp = LANE

    def fn(states, q, k, v, la):
        sts = list(states)
        trif = _tril().astype(F32)
        rows = []
        for c in range(q.shape[0] // CHUNK):
            r = slice(c * CHUNK, (c + 1) * CHUNK)
            b_all = hi.nn(trif, la[r])
            oh = []
            for h in range(heads):
                ks, vs = slice(h * kp, (h + 1) * kp), slice(h * vp, (h + 1) * vp)
                qh = q[r, ks] * scale if scale != 1.0 else q[r, ks]
                sts[h], o = _gla_chunk(sts[h], qh, k[r, ks], v[r, vs], la[r, ks], b_all[:, ks])
                oh.append(o)
            rows.append(jnp.concatenate(oh, axis=1))
        return tuple(sts), (jnp.concatenate(rows, axis=0),)

    return fn


def _ssd_step(states, xa, dtr, dtb, alog, dsk):
    sts = list(states)
    trif = _tril().astype(F32)
    wide = lax.broadcasted_iota(jnp.int32, (CHUNK, LANE), 0) >= lax.broadcasted_iota(jnp.int32, (CHUNK, LANE), 1)
    hg = SSM_HEADS // SSM_GROUPS
    xw = SSM_HEADS * LANE
    lane, head = lax.broadcasted_iota(jnp.int32, (LANE, xw), 1), lax.broadcasted_iota(jnp.int32, (LANE, xw), 0)
    spread = ((lane >= head * LANE) & (lane < (head + 1) * LANE)).astype(F32)
    neg_a = -jnp.exp(alog)
    pad = jnp.zeros((CHUNK, LANE), F32)
    rows = []
    for c in range(xa.shape[0] // CHUNK):
        r = slice(c * CHUNK, (c + 1) * CHUNK)
        dt_all = _softplus(hi.nn(dtr[r], spread) + dtb)
        a_all = dt_all * neg_a
        acs_all = hi.nn(trif, a_all)
        last_all = jnp.sum(a_all, axis=0, keepdims=True)
        yh = []
        for g in range(SSM_GROUPS):
            bm = xa[r, xw + g * LANE:xw + (g + 1) * LANE]
            cm = xa[r, xw + (SSM_GROUPS + g) * LANE:xw + (SSM_GROUPS + g + 1) * LANE]
            cb = mm.nt(cm, jnp.concatenate([bm, pad], axis=0))
            for hh in range(hg):
                h = g * hg + hh
                ls = slice(h * LANE, (h + 1) * LANE)
                xs, acs, acs_last = xa[r, ls], acs_all[:, ls], last_all[:, ls]
                xdt = xs * dt_all[:, ls]
                seg = acs - jnp.concatenate([acs, pad], axis=0).T[:CHUNK]
                lmat = jnp.exp(jnp.where(wide, seg, -1e30))
                y = (mm.nn(cb * lmat, jnp.concatenate([xdt, pad], axis=0)) + mm.nn(cm, sts[h]) * jnp.exp(acs)
                     + dsk[:, ls] * xs)
                sts[h] = jnp.exp(acs_last) * sts[h] + mm.tn(bm, xdt * jnp.exp(acs_last - acs))
                yh.append(y)
        rows.append(jnp.concatenate(yh, axis=1))
    return tuple(sts), (jnp.concatenate(rows, axis=0),)


def _gla_pre(glr, w2, bg):
    z = mm.nn(glr, w2) + bg
    return (-_softplus(-z) * (1.0 / GLA_GATE_NORM),)


def _gla_post(o, og, g):
    w = 2 * LANE
    return (jnp.concatenate([_rms(o[:, h * w:(h + 1) * w], g, GLA_DV) * _silu(og[:, h * w:(h + 1) * w])
                             for h in range(GLA_HEADS)], axis=1),)


def _hgrn_pre(q, f, lbnd):
    e = jnp.exp(lbnd - jnp.max(lbnd, axis=0, keepdims=True))
    rowi = lax.broadcasted_iota(jnp.int32, e.shape, 0)
    lb = jnp.sum(jnp.where(rowi >= 1, e, 0.0), axis=0, keepdims=True) / jnp.sum(e, axis=0, keepdims=True)
    fg = lb + (1.0 - lb) * _sigmoid(f)
    return _silu(q), 1.0 - fg, jnp.log(fg)


def _hgrn_post(o, og, g):
    return (jnp.concatenate([_rms(o[:, h * LANE:(h + 1) * LANE], g) for h in range(HGRN_HEADS)], axis=1)
            * _sigmoid(og),)


def _mamba_post(y, z, g):
    v = y * _silu(z)
    w = (SSM_HEADS // SSM_GROUPS) * LANE
    n_real = (SSM_HEADS // SSM_GROUPS) * SSM_HD
    return (jnp.concatenate([_rms(v[:, i * w:(i + 1) * w], g[:, i * w:(i + 1) * w], n_real)
                             for i in range(SSM_GROUPS)], axis=1),)


def _dil_pre(q, k, cosf, sinf, qg, kg):
    def groups(x, g):
        out = []
        for grp in range(len(DIL_GROUPS)):
            hs = []
            for h in range(grp * DIL_HEADS, (grp + 1) * DIL_HEADS):
                n = _rms(x[:, h * LANE:(h + 1) * LANE], g)
                hs.append(n * cosf + _swap_halves(n) * sinf)
            out.append(jnp.concatenate(hs, axis=1))
        return out

    return (*groups(q, qg), *groups(k, kg))


def _dil_merge(o0, o1, o2, l0, l1, l2):
    m = jnp.maximum(jnp.maximum(l0, l1), l2)
    e0, e1, e2 = jnp.exp(l0 - m), jnp.exp(l1 - m), jnp.exp(l2 - m)
    return ((e0 * o0 + e1 * o1 + e2 * o2) / (e0 + e1 + e2),)


def _dil_block(q, kp, kc, vp, vc, lim):
    kk = jnp.concatenate([kp, kc], axis=0)
    vv = jnp.concatenate([vp, vc], axis=0)
    s = mm.nt(q, kk) * (DIL_HD ** -0.5)
    i = lax.broadcasted_iota(jnp.int32, s.shape, 0)
    j = lax.broadcasted_iota(jnp.int32, s.shape, 1)
    dist = DIL_BLOCK + i - j
    s = jnp.where((dist >= 0) & (dist <= DIL_BLOCK) & (j >= lim), s, -1e30)
    m = jnp.max(s, axis=-1, keepdims=True)
    p = jnp.exp(s - m)
    l = jnp.sum(p, axis=-1, keepdims=True)
    return mm.nn(p / l, vv), jnp.broadcast_to(m + jnp.log(l), (q.shape[0], LANE))


def _xattn(xq, kv, qg, kg):
    w = XA_HEADS * LANE
    os_ = []
    for h in range(XA_HEADS):
        ls = slice(h * LANE, (h + 1) * LANE)
        q = _rms(xq[:, ls], qg, XA_HD)
        k = _rms(kv[:, ls], kg, XA_HD)
        s = mm.nt(q, k) * (XA_HD ** -0.5)
        p = jnp.exp(s - jnp.max(s, axis=-1, keepdims=True))
        p = p / jnp.sum(p, axis=-1, keepdims=True)
        os_.append(mm.nn(p, kv[:, w + h * LANE:w + (h + 1) * LANE]))
    return (jnp.concatenate(os_, axis=1),)


def _dil_geometry(s, w, r, g, v_cols):
    hb = DIL_HEADS if r == 1 else 1
    rb = DIL_BLOCK * r
    nb = s // rb
    bw = hb * LANE
    v_col0 = g * (w // bw)
    assert v_cols % bw == 0 and s % rb == 0
    return hb, rb, nb, bw, v_col0


def _sub(r, res):
    return pl.ds(res, DIL_BLOCK, stride=r) if r > 1 else slice(None)


def dil_attn(name, q, k, v, r, g):
    s, w = q.shape
    hb, rb, nb, bw, v_col0 = _dil_geometry(s, w, r, g, v.shape[1])

    def body(q_r, kp_r, kc_r, vp_r, vc_r, o_r, l_r):
        lim = jnp.where(pl.program_id(1) == 0, DIL_BLOCK, 0)
        for res in range(r):
            rows = _sub(r, res)
            for h in range(hb):
                ls = slice(h * LANE, (h + 1) * LANE)
                o, lse = _dil_block(q_r[rows, ls], kp_r[rows, ls], kc_r[rows, ls], vp_r[rows, ls], vc_r[rows, ls], lim)
                o_r[rows, ls] = o
                l_r[rows, ls] = lse

    cur = pl.BlockSpec((rb, bw), lambda hblk, n: (n, hblk))
    prev = pl.BlockSpec((rb, bw), lambda hblk, n: (jnp.maximum(n - 1, 0), hblk))
    vcur = pl.BlockSpec((rb, bw), lambda hblk, n: (n, v_col0 + hblk))
    vprev = pl.BlockSpec((rb, bw), lambda hblk, n: (jnp.maximum(n - 1, 0), v_col0 + hblk))
    return pl.pallas_call(
        body, name=name, grid=(w // bw, nb), in_specs=[cur, prev, cur, vprev, vcur], out_specs=[cur, cur],
        out_shape=[jax.ShapeDtypeStruct((s, w), F32)] * 2,
        compiler_params=_cparams(("parallel", "parallel")),
    )(q, k, k, v, v)


def dil_attn_bwd(name, q, k, v, do, dlse, r, g):
    s, w = q.shape
    hb, rb, nb, bw, v_col0 = _dil_geometry(s, w, r, g, v.shape[1])

    def body(q_r, kp_r, kc_r, vp_r, vc_r, do_r, dl_r, dq_r, dk_r, dv_r, ck, cv):
        i = pl.program_id(1)
        lim = jnp.where(i == nb - 1, DIL_BLOCK, 0)

        @pl.when(i == 0)
        def _():
            ck[...] = jnp.zeros_like(ck)
            cv[...] = jnp.zeros_like(cv)

        for res in range(r):
            rows = _sub(r, res)
            for h in range(hb):
                ls = slice(h * LANE, (h + 1) * LANE)
                _, vjp = jax.vjp(functools.partial(_dil_block, lim=lim),
                                 q_r[rows, ls], kp_r[rows, ls], kc_r[rows, ls], vp_r[rows, ls], vc_r[rows, ls])
                gq, gkp, gkc, gvp, gvc = vjp((do_r[rows, ls], dl_r[rows, ls]))
                dq_r[rows, ls] = gq
                dk_r[rows, ls] = gkc + ck[rows, ls]
                dv_r[rows, ls] = gvc + cv[rows, ls]
                ck[rows, ls] = gkp
                cv[rows, ls] = gvp

    cur = pl.BlockSpec((rb, bw), lambda hblk, i: (nb - 1 - i, hblk))
    prev = pl.BlockSpec((rb, bw), lambda hblk, i: (jnp.maximum(nb - 2 - i, 0), hblk))
    vcur = pl.BlockSpec((rb, bw), lambda hblk, i: (nb - 1 - i, v_col0 + hblk))
    vprev = pl.BlockSpec((rb, bw), lambda hblk, i: (jnp.maximum(nb - 2 - i, 0), v_col0 + hblk))
    return pl.pallas_call(
        body, name=name, grid=(w // bw, nb), in_specs=[cur, prev, cur, vprev, vcur, cur, cur],
        out_specs=[cur, cur, cur], out_shape=[jax.ShapeDtypeStruct((s, w), F32)] * 3,
        scratch_shapes=[pltpu.VMEM((rb, bw), F32)] * 2,
        compiler_params=_cparams(("parallel", "arbitrary")),
    )(q, k, k, v, v, do, dlse)


def _dsilu(u):
    sg = _sigmoid(u)
    return sg * (1.0 + u * (1.0 - sg))


def _ride(body, rider, n_in, n_out, n_scratch, grid):
    if rider is None:
        return body, [], [], [], [], [], None
    ni, no = len(rider.ins), len(rider.out_shapes)

    def wrapped(*refs):
        k_in, r_in = refs[:n_in], refs[n_in:n_in + ni]
        k_out, r_out = refs[n_in + ni:n_in + ni + n_out], refs[n_in + ni + n_out:n_in + ni + n_out + no]
        k_scr = refs[n_in + ni + n_out + no:n_in + ni + n_out + no + n_scratch]
        send_sems, recv_sems = refs[-2], refs[-1]
        first = functools.reduce(jnp.logical_and, [pl.program_id(a) == 0 for a in range(len(grid))])
        last = functools.reduce(jnp.logical_and, [pl.program_id(a) == g - 1 for a, g in enumerate(grid)])

        @pl.when(first)
        def _():
            rider.start(r_in, r_out, send_sems, recv_sems)

        body(*k_in, *k_out, *k_scr)

        @pl.when(last)
        def _():
            rider.finish(r_in, r_out, send_sems, recv_sems)

    sems = [pltpu.SemaphoreType.DMA((rider.n_sems,)), pltpu.SemaphoreType.DMA((rider.n_sems,))]
    return wrapped, rider.ins, [_ANY] * ni, [_ANY] * no, rider.out_shapes, sems, ("arbitrary",) * len(grid)


CONV_STRIP = 16


def _shifted_rows(prev8, cur_r, next8, lanes, s0, n, sh, block):
    if s0 - sh < 0:
        assert s0 == 0
        xp = jnp.concatenate([prev8, cur_r[0:n, lanes]], axis=0)
        return pltpu.roll(xp, sh, 0)[SUBLANE:SUBLANE + n]
    if s0 - sh + n > block:
        assert s0 == block and n == SUBLANE
        xp = jnp.concatenate([cur_r[block - SUBLANE:block, lanes], next8], axis=0)
        return (pltpu.roll(xp, sh, 0) if sh else xp)[SUBLANE:]
    return cur_r[pl.ds(s0 - sh, n), lanes]


def conv_fwd(name, x, w, b, mode, out_dtype, tc, block=ROW_BLOCK, rider=None):
    s, c = x.shape
    ntap = w.shape[0]
    block = min(block, s)
    f = c // 2 if mode == 'glu' else c
    nh = 2 if mode == 'glu' else 1
    off = f // tc

    def body(*refs):
        first = pl.program_id(1) == 0
        o_ref = refs[-1]

        def column(cidx, carry):
            lanes = pl.ds(pl.multiple_of(cidx * LANE, LANE), LANE)
            prevs = [jnp.where(first, 0.0, refs[4 * hlf][:, lanes]) for hlf in range(nh)]
            for s0 in range(0, block, CONV_STRIP):
                us = []
                for hlf in range(nh):
                    _, cur_r, w_r, b_r = refs[4 * hlf:4 * hlf + 4]
                    acc = b_r[:, lanes]
                    for j in range(ntap):
                        xs = _shifted_rows(prevs[hlf], cur_r, None, lanes, s0, CONV_STRIP, ntap - 1 - j, block)
                        acc = acc + w_r[j:j + 1, lanes] * xs
                    us.append(acc)
                res = _silu(us[0]) * us[1] if mode == 'glu' else _silu(us[0])
                o_ref[pl.ds(s0, CONV_STRIP), lanes] = res.astype(o_ref.dtype)
            return carry

        lax.fori_loop(0, tc // LANE, column, 0)

    rb = block // SUBLANE
    ins, specs = [], []
    for hlf in range(nh):
        o = hlf * off
        ins += [x, x, w, b]
        specs += [pl.BlockSpec((SUBLANE, tc), lambda j, i, o=o: (jnp.maximum(i * rb - 1, 0), j + o)),
                  pl.BlockSpec((block, tc), lambda j, i, o=o: (i, j + o)),
                  pl.BlockSpec((ntap, tc), lambda j, i, o=o: (0, j + o)),
                  pl.BlockSpec((1, tc), lambda j, i, o=o: (0, j + o))]
    grid = (f // tc, s // block)
    body, r_ins, r_in_specs, r_out_specs, r_out_shapes, r_scratch, sem = _ride(body, rider, len(ins), 1, 0, grid)
    res = pl.pallas_call(
        body, name=name, grid=grid, in_specs=specs + r_in_specs,
        out_specs=[pl.BlockSpec((block, tc), lambda j, i: (i, j))] + r_out_specs,
        out_shape=[jax.ShapeDtypeStruct((s, f), out_dtype)] + r_out_shapes, scratch_shapes=r_scratch,
        compiler_params=_cparams(sem or ("parallel", "parallel")),
    )(*ins, *r_ins)
    return res[0] if rider is None else (res[0], rider.results(res[1:]))


def conv_bwd(name, x, w, b, dout, mode, tc, block=ROW_BLOCK, rider=None):
    s, c = x.shape
    ntap = w.shape[0]
    block = min(block, s)
    nblk = s // block
    f = c // 2 if mode == 'glu' else c
    nh = 2 if mode == 'glu' else 1
    off = f // tc
    ext = block + SUBLANE

    def body(*refs):
        i = pl.program_id(1)
        first, last = i == 0, i == nblk - 1
        dcur_r, dnext_r = refs[5 * nh], refs[5 * nh + 1]
        outs = refs[5 * nh + 2:5 * nh + 2 + 3 * nh]
        du_scr = refs[5 * nh + 2 + 3 * nh:]

        @pl.when(first)
        def _():
            for hlf in range(nh):
                outs[3 * hlf + 1][...] = jnp.zeros_like(outs[3 * hlf + 1])
                outs[3 * hlf + 2][...] = jnp.zeros_like(outs[3 * hlf + 2])

        def column(cidx, carry):
            lanes = pl.ds(pl.multiple_of(cidx * LANE, LANE), LANE)
            prevs = [jnp.where(first, 0.0, refs[5 * hlf][:, lanes]) for hlf in range(nh)]
            nexts = [jnp.where(last, 0.0, refs[5 * hlf + 2][:, lanes]) for hlf in range(nh)]
            db_acc = [jnp.zeros((CONV_STRIP, LANE), F32) for _ in range(nh)]
            dw_acc = [[jnp.zeros((CONV_STRIP, LANE), F32) for _ in range(ntap)] for _ in range(nh)]
            for s0 in range(0, ext, CONV_STRIP):
                n = min(CONV_STRIP, ext - s0)
                d_e = dcur_r[pl.ds(s0, n), lanes] if s0 < block else jnp.where(last, 0.0, dnext_r[:, lanes])
                xs, us = [], []
                for hlf in range(nh):
                    cur_r, w_r, b_r = refs[5 * hlf + 1], refs[5 * hlf + 3], refs[5 * hlf + 4]
                    sh_rows = [_shifted_rows(prevs[hlf], cur_r, nexts[hlf], lanes, s0, n, ntap - 1 - j, block)
                               for j in range(ntap)]
                    acc = b_r[:, lanes]
                    for j in range(ntap):
                        acc = acc + w_r[j:j + 1, lanes] * sh_rows[j]
                    xs.append(sh_rows)
                    us.append(acc)
                dus = [d_e * us[1] * _dsilu(us[0]), d_e * _silu(us[0])] if mode == 'glu' else [d_e * _dsilu(us[0])]
                for hlf in range(nh):
                    du_scr[hlf][pl.ds(s0, n), lanes] = dus[hlf]
                    if s0 < block:
                        db_acc[hlf] = db_acc[hlf] + dus[hlf]
                        for j in range(ntap):
                            dw_acc[hlf][j] = dw_acc[hlf][j] + dus[hlf] * xs[hlf][j]
            for hlf in range(nh):
                w_r = refs[5 * hlf + 3]
                dx_r, dw_r, db_r = outs[3 * hlf:3 * hlf + 3]
                db_r[:, lanes] += jnp.sum(db_acc[hlf], axis=0, keepdims=True)
                for j in range(ntap):
                    dw_r[j:j + 1, lanes] += jnp.sum(dw_acc[hlf][j], axis=0, keepdims=True)
                for s0 in range(0, block, CONV_STRIP):
                    dx = None
                    for j in range(ntap):
                        term = w_r[j:j + 1, lanes] * du_scr[hlf][pl.ds(s0 + ntap - 1 - j, CONV_STRIP), lanes]
                        dx = term if dx is None else dx + term
                    dx_r[pl.ds(s0, CONV_STRIP), lanes] = dx.astype(dx_r.dtype)
            return carry

        lax.fori_loop(0, tc // LANE, column, 0)

    rb = block // SUBLANE
    nrow8 = s // SUBLANE
    ins, specs = [], []
    for hlf in range(nh):
        o = hlf * off
        ins += [x, x, x, w, b]
        specs += [pl.BlockSpec((SUBLANE, tc), lambda j, i, o=o: (jnp.maximum(i * rb - 1, 0), j + o)),
                  pl.BlockSpec((block, tc), lambda j, i, o=o: (i, j + o)),
                  pl.BlockSpec((SUBLANE, tc), lambda j, i, o=o: (jnp.minimum((i + 1) * rb, nrow8 - 1), j + o)),
                  pl.BlockSpec((ntap, tc), lambda j, i, o=o: (0, j + o)),
                  pl.BlockSpec((1, tc), lambda j, i, o=o: (0, j + o))]
    ins += [dout, dout]
    specs += [pl.BlockSpec((block, tc), lambda j, i: (i, j)),
              pl.BlockSpec((SUBLANE, tc), lambda j, i: (jnp.minimum((i + 1) * rb, nrow8 - 1), j))]
    out_specs, out_shape = [], []
    for hlf in range(nh):
        out_specs += [pl.BlockSpec((block, tc), lambda j, i: (i, j)), pl.BlockSpec((ntap, tc), lambda j, i: (0, j)),
                      pl.BlockSpec((1, tc), lambda j, i: (0, j))]
        out_shape += [jax.ShapeDtypeStruct((s, f), MXU_DTYPE), jax.ShapeDtypeStruct((ntap, f), F32),
                      jax.ShapeDtypeStruct((1, f), F32)]
    grid = (f // tc, nblk)
    body, r_ins, r_in_specs, r_out_specs, r_out_shapes, r_scratch, sem = _ride(body, rider, len(ins), 3 * nh, nh, grid)
    res = pl.pallas_call(
        body, name=name, grid=grid, in_specs=specs + r_in_specs, out_specs=out_specs + r_out_specs,
        out_shape=out_shape + r_out_shapes, scratch_shapes=[pltpu.VMEM((ext, tc), F32)] * nh + r_scratch,
        compiler_params=_cparams(sem or ("parallel", "arbitrary")),
    )(*ins, *r_ins)
    rode = None if rider is None else rider.results(res[3 * nh:])
    if nh == 1:
        return [res[0]], res[1], res[2], rode
    return ([res[0], res[3]], jnp.concatenate([res[1], res[4]], axis=1), jnp.concatenate([res[2], res[5]], axis=1),
            rode)


def loss_head(y, target, block=ROW_BLOCK):
    s, d = y.shape
    block = min(block, s)

    def body(y_r, t_r, acc_r, dy_r):
        e = y_r[...] - t_r[...]
        dy_r[...] = e * (1.0 / d)

        @pl.when(pl.program_id(0) == 0)
        def _():
            acc_r[...] = jnp.zeros_like(acc_r)

        acc_r[...] += jnp.sum((e * e).reshape(block // SUBLANE, SUBLANE, d), axis=0) * (0.5 / d)

    return pl.pallas_call(
        body, name="loss_head", grid=(s // block,),
        in_specs=[pl.BlockSpec((block, d), lambda i: (i, 0))] * 2,
        out_specs=[pl.BlockSpec((SUBLANE, d), lambda i: (0, 0)), pl.BlockSpec((block, d), lambda i: (i, 0))],
        out_shape=[jax.ShapeDtypeStruct((SUBLANE, d), F32), jax.ShapeDtypeStruct((s, d), F32)],
        compiler_params=_cparams(("arbitrary",)),
    )(y, target)


def adamw(name, w, g, m, v):
    r, c = w.shape
    tr = r if r <= 512 else _tile(r, (512, 256, 128, 64, 32, 16, 8))
    if c * tr * 4 > (1 << 21):
        tr = _tile(r, (256, 128, 64, 32, 16, 8))

    def body(w_r, g_r, m_r, v_r, d_r, nm_r, nv_r):
        gg = g_r[...]
        nm = ADAM_B1 * m_r[...] + (1.0 - ADAM_B1) * gg
        nv = ADAM_B2 * v_r[...] + (1.0 - ADAM_B2) * (gg * gg)
        m_hat = nm / (1.0 - ADAM_B1 ** ADAM_STEP)
        v_hat = nv / (1.0 - ADAM_B2 ** ADAM_STEP)
        d_r[...] = -ADAM_LR * (m_hat / (jnp.sqrt(v_hat) + ADAM_EPS) + ADAM_WD * w_r[...])
        nm_r[...] = nm
        nv_r[...] = nv

    spec = pl.BlockSpec((tr, c), lambda i: (i, 0))
    return pl.pallas_call(
        body, name=name, grid=(r // tr,), in_specs=[spec] * 4, out_specs=[spec] * 3,
        out_shape=[jax.ShapeDtypeStruct((r, c), F32)] * 3, compiler_params=_cparams(("parallel",)),
    )(w, g, m, v)


MESH = pl.DeviceIdType.MESH
_ANY = pl.BlockSpec(memory_space=pl.ANY)


def _place():
    return lax.axis_index("x"), lax.axis_index("y"), lax.axis_index("c")


class Packed:
    def __init__(self, shard_shape):
        self.r, self.c = shard_shape
        self.h = self.r // 2
        self.whole = (N_CHIPS, self.r, self.c)
        self.got = (N_CHIPS, self.h, self.c)
        self.slab_half = (self.h, self.c)

    def shard_half(self, ref, core):
        return ref.at[pl.ds(core * self.h, self.h)]

    def whole_half(self, ref, chip, core):
        return ref.at[chip, pl.ds(core * self.h, self.h)]

    def place(self, whole, shard, chip):
        return lax.dynamic_update_slice(whole, shard[None], (chip, 0, 0))

    def grad_half(self, ref, core):
        return ref.at[:, core]

    def pair_slab(self, ref, chip):
        return ref.at[chip]


class SlabCols:
    def __init__(self, shard_shape):
        self.r, self.c = shard_shape
        self.h = self.r // 2
        self.whole = (self.r, N_CHIPS * self.c)
        self.got = (self.h, N_CHIPS * self.c)
        self.slab_half = (self.h, self.c)

    def _cols(self, chip):
        return pl.ds(pl.multiple_of(chip * self.c, LANE), self.c)

    def shard_half(self, ref, core):
        return ref.at[pl.ds(core * self.h, self.h)]

    def whole_half(self, ref, chip, core):
        return ref.at[pl.ds(core * self.h, self.h), self._cols(chip)]

    def place(self, whole, shard, chip):
        return lax.dynamic_update_slice_in_dim(whole, shard, chip * self.c, 1)

    def grad_half(self, ref, core):
        return ref.at[pl.ds(core * self.h, self.h)]

    def pair_slab(self, ref, chip):
        return ref.at[:, self._cols(chip)]


class GatherRider:
    def __init__(self, shards, kinds):
        self.ins, self.kinds, n = list(shards), kinds, len(shards)
        self.out_shapes = [jax.ShapeDtypeStruct(k.whole, s.dtype) for k, s in zip(kinds, shards)]
        self.n_sems = 6 * n

    def _copies(self, w_refs, out_refs, send_sems, recv_sems):
        x, y, cc = _place()
        chips = [(1 - x, y), (x, 1 - y), (1 - x, 1 - y)]

        def copy(t, k, chip, core, to, src=None):
            dst = self.kinds[t].whole_half(out_refs[t], 2 * chip[0] + chip[1], core)
            return pltpu.make_async_remote_copy(
                src_ref=dst if src is None else src, dst_ref=dst, send_sem=send_sems.at[6 * t + k],
                recv_sem=recv_sems.at[6 * t + k], device_id=to, device_id_type=MESH)

        first = [copy(t, j, (x, y), cc, (*chip, cc), src=self.kinds[t].shard_half(w_refs[t], cc))
                 for t in range(len(self.ins)) for j, chip in enumerate(chips)]
        return copy, first, chips, (x, y, cc)

    def start(self, w_refs, out_refs, send_sems, recv_sems):
        for cp in self._copies(w_refs, out_refs, send_sems, recv_sems)[1]:
            cp.start()

    def finish(self, w_refs, out_refs, send_sems, recv_sems):
        copy, first, chips, (x, y, cc) = self._copies(w_refs, out_refs, send_sems, recv_sems)
        passed = []
        for t in range(len(self.ins)):
            for j, chip in enumerate(chips):
                copy(t, j, chip, cc, (x, y, cc)).wait_recv()
                passed.append(copy(t, 3 + j, chip, cc, (x, y, 1 - cc)))
                passed[-1].start()
        for t in range(len(self.ins)):
            for j, chip in enumerate(chips):
                copy(t, 3 + j, chip, 1 - cc, (x, y, cc)).wait_recv()
        for cp in first + passed:
            cp.wait_send()

    def results(self, outs):
        chip = 2 * lax.axis_index("x") + lax.axis_index("y")
        return [k.place(o, s, chip) for k, o, s in zip(self.kinds, outs, self.ins)]


class ExchangeRider:
    def __init__(self, pairs, kinds):
        self.ins, self.kinds = list(pairs), kinds
        self.out_shapes = [jax.ShapeDtypeStruct((N_CHIPS,) + k.slab_half, p.dtype) for k, p in zip(kinds, pairs)]
        self.n_sems = 3 * len(pairs)

    def start(self, p_refs, out_refs, send_sems, recv_sems):
        x, y, cc = _place()
        for t in range(len(self.ins)):
            for j, chip in enumerate([(1 - x, y), (x, 1 - y), (1 - x, 1 - y)]):
                pltpu.make_async_remote_copy(
                    src_ref=self.kinds[t].pair_slab(p_refs[t], 2 * chip[0] + chip[1]), dst_ref=out_refs[t].at[2 * x + y],
                    send_sem=send_sems.at[3 * t + j], recv_sem=recv_sems.at[3 * t + j], device_id=(*chip, cc),
                    device_id_type=MESH).start()

    def finish(self, p_refs, out_refs, send_sems, recv_sems):
        x, y, cc = _place()
        me = 2 * x + y
        for t in range(len(self.ins)):
            for j, chip in enumerate([(1 - x, y), (x, 1 - y), (1 - x, 1 - y)]):
                them = 2 * chip[0] + chip[1]
                pltpu.make_async_remote_copy(
                    src_ref=self.kinds[t].pair_slab(p_refs[t], them), dst_ref=out_refs[t].at[them],
                    send_sem=send_sems.at[3 * t + j], recv_sem=recv_sems.at[3 * t + j], device_id=(x, y, cc),
                    device_id_type=MESH).wait()

    def results(self, outs):
        return list(outs)


class SwapRider:
    def __init__(self, gs, kinds):
        self.ins, self.kinds = list(gs), kinds
        self.out_shapes = [jax.ShapeDtypeStruct(k.got, g.dtype) for k, g in zip(kinds, gs)]
        self.n_sems = len(gs)

    def _copies(self, g_refs, out_refs, send_sems, recv_sems):
        x, y, cc = _place()
        return [pltpu.make_async_remote_copy(
            src_ref=self.kinds[t].grad_half(g_refs[t], 1 - cc), dst_ref=out_refs[t], send_sem=send_sems.at[t],
            recv_sem=recv_sems.at[t], device_id=(x, y, 1 - cc), device_id_type=MESH) for t in range(len(self.ins))]

    def start(self, *refs):
        for cp in self._copies(*refs):
            cp.start()

    def finish(self, *refs):
        for cp in self._copies(*refs):
            cp.wait()

    def results(self, outs):
        return list(outs)


def run_rider(rider, name):
    n, no = len(rider.ins), len(rider.out_shapes)

    def body(*refs):
        parts = (refs[:n], refs[n:n + no], refs[n + no], refs[n + no + 1])
        rider.start(*parts)
        rider.finish(*parts)

    outs = pl.pallas_call(
        body, name=name, in_specs=[_ANY] * n, out_specs=[_ANY] * no, out_shape=rider.out_shapes,
        scratch_shapes=[pltpu.SemaphoreType.DMA((rider.n_sems,)), pltpu.SemaphoreType.DMA((rider.n_sems,))],
    )(*rider.ins)
    return rider.results(outs)


def allgather_devices(buf):
    r, c = buf.shape

    def body(b_ref, out_ref, send_sems, recv_sems, local_sem):
        x, y, cc = _place()
        me = 4 * x + 2 * y + cc
        mine = pltpu.make_async_copy(b_ref, out_ref.at[me], local_sem)
        mine.start()
        copies = []
        for k in range(1, N_DEV):
            px, py, pc = x ^ (k >> 2), y ^ ((k >> 1) & 1), cc ^ (k & 1)
            cp = pltpu.make_async_remote_copy(src_ref=b_ref, dst_ref=out_ref.at[me], send_sem=send_sems.at[k - 1],
                                              recv_sem=recv_sems.at[k - 1], device_id=(px, py, pc), device_id_type=MESH)
            cp.start()
            copies.append((cp, 4 * px + 2 * py + pc))
        for k, (cp, peer) in enumerate(copies):
            pltpu.make_async_remote_copy(src_ref=b_ref, dst_ref=out_ref.at[peer], send_sem=send_sems.at[k],
                                         recv_sem=recv_sems.at[k], device_id=(x, y, cc), device_id_type=MESH).wait_recv()
        for cp, _ in copies:
            cp.wait_send()
        mine.wait()

    return pl.pallas_call(
        body, name="allgather_devices", in_specs=[_ANY], out_specs=_ANY,
        out_shape=jax.ShapeDtypeStruct((N_DEV, r, c), buf.dtype),
        scratch_shapes=[pltpu.SemaphoreType.DMA((N_DEV - 1,)), pltpu.SemaphoreType.DMA((N_DEV - 1,)),
                        pltpu.SemaphoreType.DMA],
    )(buf)


def swap_halves_sibling(gs, kinds, name):
    n = len(gs)

    def body(*refs):
        g_refs, out_refs, send_sems, recv_sems = refs[:n], refs[n:2 * n], refs[2 * n], refs[2 * n + 1]
        x, y, cc = _place()
        cps = []
        for t in range(n):
            cps.append(pltpu.make_async_remote_copy(
                src_ref=kinds[t].grad_half(g_refs[t], 1 - cc), dst_ref=out_refs[t], send_sem=send_sems.at[t],
                recv_sem=recv_sems.at[t], device_id=(x, y, 1 - cc), device_id_type=MESH))
            cps[-1].start()
        for cp in cps:
            cp.wait()

    return pl.pallas_call(
        body, name=name, in_specs=[_ANY] * n, out_specs=[_ANY] * n,
        out_shape=[jax.ShapeDtypeStruct(k.got, g.dtype) for k, g in zip(kinds, gs)],
        scratch_shapes=[pltpu.SemaphoreType.DMA((n,)), pltpu.SemaphoreType.DMA((n,))],
    )(*gs)


def _row_tile(n, limit=512):
    return max(t for t in range(16, limit + 1, 16) if n % t == 0)


def sum_chips(got, own, kind, chip, name):
    def body(chip_ref, got_r, own_r, out_r):
        mine = own_r[...].astype(F32)
        acc = None
        for k in range(N_CHIPS):
            term = jnp.where(chip_ref[0] == k, mine, got_r[k].astype(F32))
            acc = term if acc is None else acc + term
        out_r[...] = acc

    if isinstance(kind, Packed):
        r, c = kind.slab_half
        tr = _row_tile(r)
        grid = (r // tr,)
        specs = [pl.BlockSpec((N_CHIPS, tr, c), lambda i, chip_ref: (0, i, 0)),
                 pl.BlockSpec((None, tr, c), lambda i, chip_ref: (chip_ref[0], i, 0))]
        out_spec = pl.BlockSpec((tr, c), lambda i, chip_ref: (i, 0))
    else:
        r, c = kind.slab_half
        tr = _row_tile(r, 256)
        grid = (r // tr,)
        specs = [pl.BlockSpec((N_CHIPS, tr, c), lambda i, chip_ref: (0, i, 0)),
                 pl.BlockSpec((tr, c), lambda i, chip_ref: (i, chip_ref[0]))]
        out_spec = pl.BlockSpec((tr, c), lambda i, chip_ref: (i, 0))
    return pl.pallas_call(
        body, name=name,
        grid_spec=pltpu.PrefetchScalarGridSpec(num_scalar_prefetch=1, grid=grid, in_specs=specs, out_specs=out_spec),
        out_shape=jax.ShapeDtypeStruct(kind.slab_half, F32),
        compiler_params=_cparams(("parallel",) * len(grid)),
    )(chip, got, own)


def join_halves_sibling(halves):
    n = len(halves)

    def body(*refs):
        h_refs, out_refs, send_sems, recv_sems = refs[:n], refs[n:2 * n], refs[2 * n], refs[2 * n + 1]
        x, y, cc = _place()
        cps = []
        for t in range(n):
            cps.append(pltpu.make_async_remote_copy(
                src_ref=h_refs[t], dst_ref=out_refs[t].at[cc], send_sem=send_sems.at[t], recv_sem=recv_sems.at[t],
                device_id=(x, y, 1 - cc), device_id_type=MESH))
            cps[-1].start()
        for t in range(n):
            pltpu.make_async_remote_copy(
                src_ref=h_refs[t], dst_ref=out_refs[t].at[1 - cc], send_sem=send_sems.at[t], recv_sem=recv_sems.at[t],
                device_id=(x, y, cc), device_id_type=MESH).wait_recv()
        for cp in cps:
            cp.wait_send()

    outs = pl.pallas_call(
        body, name="join_halves_sibling", in_specs=[_ANY] * n, out_specs=[_ANY] * n,
        out_shape=[jax.ShapeDtypeStruct((2,) + h.shape, h.dtype) for h in halves],
        scratch_shapes=[pltpu.SemaphoreType.DMA((n,)), pltpu.SemaphoreType.DMA((n,))],
    )(*halves)
    core = lax.axis_index("c")
    return [lax.dynamic_update_slice_in_dim(o, h[None], core, 0) for o, h in zip(outs, halves)]


def add_own_half(g, got, kind, core, out_dtype, name):
    def body(c_ref, g_r, o_r, out_r):
        out_r[...] = (g_r[...] + o_r[...]).astype(out_r.dtype)

    if isinstance(kind, Packed):
        r, c = kind.slab_half
        tr = _row_tile(r)
        grid = (N_CHIPS, r // tr)
        specs = [pl.BlockSpec((None, None, tr, c), lambda i, j, c_ref: (i, c_ref[0], j, 0)),
                 pl.BlockSpec((None, tr, c), lambda i, j, c_ref: (i, j, 0))]
        out_spec = pl.BlockSpec((None, tr, c), lambda i, j, c_ref: (i, j, 0))
    else:
        h, c4 = kind.got
        tr = _row_tile(h, 128)
        grid = (1, h // tr)
        specs = [pl.BlockSpec((tr, c4), lambda i, j, c_ref: (c_ref[0] * (h // tr) + j, 0)),
                 pl.BlockSpec((tr, c4), lambda i, j, c_ref: (j, 0))]
        out_spec = pl.BlockSpec((tr, c4), lambda i, j, c_ref: (j, 0))
    return pl.pallas_call(
        body, name=name,
        grid_spec=pltpu.PrefetchScalarGridSpec(num_scalar_prefetch=1, grid=grid, in_specs=specs, out_specs=out_spec),
        out_shape=jax.ShapeDtypeStruct(kind.got, out_dtype),
        compiler_params=_cparams(("parallel", "parallel")),
    )(core, g, got)


def sum_slabs(p, name):
    n, r, c = p.shape
    tr = _tile(r, [t for t in (512, 256, 128, 64, 32, 16) if n * t * c * p.dtype.itemsize <= (1 << 23)])

    def body(p_r, out_r):
        acc = p_r[0].astype(F32)
        for k in range(1, n):
            acc = acc + p_r[k].astype(F32)
        out_r[...] = acc

    return pl.pallas_call(
        body, name=name, grid=(r // tr,), in_specs=[pl.BlockSpec((n, tr, c), lambda i: (0, i, 0))],
        out_specs=pl.BlockSpec((tr, c), lambda i: (i, 0)), out_shape=jax.ShapeDtypeStruct((r, c), F32),
        compiler_params=_cparams(("parallel",)),
    )(p)


def _lay(arr, axis, pieces, total, reps=()):
    items = [(d, n, lax.slice_in_dim(arr, s0, s0 + n, axis=axis)) for s0, n, d in pieces]
    items += [(d, n, jnp.repeat(lax.slice_in_dim(arr, s0, s0 + 1, axis=axis), n, axis=axis)) for s0, d, n in reps]
    items.sort(key=lambda t: t[0])
    parts, pos = [], 0

    def zeros(n):
        sh = list(arr.shape)
        sh[axis] = n
        return jnp.zeros(sh, arr.dtype)

    for d, n, v in items:
        if d > pos:
            parts.append(zeros(d - pos))
        parts.append(v)
        pos = d + n
    if total > pos:
        parts.append(zeros(total - pos))
    return jnp.concatenate(parts, axis=axis) if len(parts) > 1 else parts[0]


def _unlay_parts(g, axis, pieces, reps=()):
    out = [(s0, lax.slice_in_dim(g, d, d + n, axis=axis)) for s0, n, d in pieces]
    out += [(s0, jnp.sum(lax.slice_in_dim(g, d, d + n, axis=axis), axis=axis, keepdims=True)) for s0, d, n in reps]
    return out


def _join(parts, axis):
    parts = sorted(parts, key=lambda t: t[0])
    return jnp.concatenate([p for _, p in parts], axis=axis)


def _heads(src0, n_heads, width, padded, dst0=0):
    return [(src0 + h * width, width, dst0 + h * padded) for h in range(n_heads)]


_XQ = lambda src0: _heads(src0, XA_HEADS, XA_HD, LANE)
_XA_W = XA_HEADS * LANE

LAYOUT = {
    'a': dict(
        segs=dict(q=(_heads(0, 4, 96, LANE), 512, ()), k=(_heads(384, 4, 96, LANE), 512, ()),
                  v=(_heads(768, 4, 192, 256), 1024, ()), glr=([(1536, 16, 0)], LANE, ()),
                  og=(_heads(1552, 4, 192, 256), 1024, ()), xq=(_XQ(2320), _XA_W, ())),
        tok=(_heads(0, 4, 192, 256), 1024), xa=(_XQ(768), _XA_W)),
    'b': dict(
        segs=dict(q=([(0, 1536, 0)], 1536, ()), k=([(1536, 1536, 0)], 1536, ()), v=([(3072, 1536, 0)], 1536, ()),
                  xq=(_XQ(4608), _XA_W, ())),
        tok=([(0, 512, 0)], 512), xa=(_XQ(512), _XA_W)),
    'c': dict(
        segs=dict(z=(_heads(0, 12, 64, LANE), 1536, ()),
                  xbc=(_heads(768, 12, 64, LANE) + [(1536, 256, 1536), (1792, 256, 1792)], 2048, ()),
                  dt=([(2048, 12, 0)], LANE, ()),
                  xq=(_XQ(2060), _XA_W, ())),
        tok=(_heads(0, 12, 64, LANE), 1536), xa=(_XQ(768), _XA_W)),
    'd': dict(
        segs=dict(q=([(0, 768, 0)], 768, ()), f=([(768, 768, 0)], 768, ()), i=([(1536, 768, 0)], 768, ()),
                  og=([(2304, 768, 0)], 768, ()), xq=(_XQ(3072), _XA_W, ())),
        tok=([(0, 768, 0)], 768), xa=(_XQ(768), _XA_W)),
}
KINDS = 'abcd'
_XS_PIECES = _heads(0, 12, 64, LANE)
_XBC_PIECES = _XS_PIECES + [(768, 256, 1536), (1024, 256, 1792)]
_HEAD_REPS = tuple((h, h * LANE, LANE) for h in range(12))


def _row(v):
    return v.reshape(1, -1)


LAYER_WEIGHTS = [
    {'w_in': (f'{k}_w_in', None), 'w_out': (f'{k}_w_out', None), 'w_kv': ('xa_w_kv', i), 'w_up': ('ffn_w_up', i),
     'w_down': ('ffn_w_down', i), **({'w_gate2': ('a_w_gate2', None)} if k == 'a' else {})}
    for i, k in enumerate('abcd')]


class LocalLayers:
    def __init__(self, W):
        self.W, self.g = W, {}

    def weights(self, i):
        return {key: (self.W[n] if l is None else self.W[n][l]).astype(MXU_DTYPE)
                for key, (n, l) in LAYER_WEIGHTS[i].items()}

    def fwd_rider(self, i):
        return None

    def bwd_rider(self, i):
        return None

    def grads_rider(self, i, g):
        self.g[i] = g
        return None

    def whole_grads(self):
        out = {}
        for i in range(4):
            for key, (n, l) in LAYER_WEIGHTS[i].items():
                if l is None:
                    out[n] = self.g[i][key]
        for n in ('xa_w_kv', 'ffn_w_up', 'ffn_w_down'):
            key = [k for k, (m, _) in LAYER_WEIGHTS[0].items() if m == n][0]
            out[n] = jnp.stack([self.g[i][key] for i in range(4)])
        return out


class ShardedLayers:
    def __init__(self, w, core_id):
        self.core_id = core_id
        self.names, self.axes, self.shards, self.packed, self.kinds = [], [], [], [], []
        for lw in LAYER_WEIGHTS:
            keys = [k for k in lw if k not in ('w_up', 'w_down')]
            sh = {k: (w[lw[k][0]] if lw[k][1] is None else w[lw[k][0]][lw[k][1]]).astype(MXU_DTYPE) for k in lw}
            ax = {k: SHARD_AXIS[lw[k][0]] - (lw[k][1] is not None) for k in lw}
            pk = _pack([sh[k] for k in keys], MXU_DTYPE, 256)
            self.names.append(keys)
            self.axes.append(ax)
            self.shards.append(sh)
            self.packed.append(pk)
            self.kinds.append([Packed(pk.shape), SlabCols(sh['w_up'].shape), Packed(sh['w_down'].shape)])
        self.whole = {}
        self.pending = None
        self.recvd = {}

    def _operands(self, i):
        return [self.packed[i], self.shards[i]['w_up'], self.shards[i]['w_down']]

    def _gathered(self, i, res):
        per_chip = [_unpack(res[0][j], [self.shards[i][k].shape for k in self.names[i]]) for j in range(N_CHIPS)]
        out = {k: _merge_chips(jnp.stack([per_chip[j][n] for j in range(N_CHIPS)]), self.axes[i][k])
               for n, k in enumerate(self.names[i])}
        out['w_up'], out['w_down'] = res[1], res[2].reshape(-1, res[2].shape[-1])
        self.whole[i] = out

    def first_gather(self):
        self._gathered(0, run_rider(GatherRider(self._operands(0), self.kinds[0]), "allgather_chips"))

    def weights(self, i):
        return self.whole[i]

    def fwd_rider(self, i):
        return GatherRider(self._operands(i + 1), self.kinds[i + 1]) if i + 1 < 4 else None

    def fwd_rode(self, i, res):
        self._gathered(i + 1, res)

    def bwd_rider(self, i):
        return ExchangeRider(self.pending[1], self.kinds[self.pending[0]]) if self.pending is not None else None

    def bwd_rode(self, i, res):
        self.recvd[self.pending[0]] = (res, self.pending[1])
        self.pending = None

    def grads_rider(self, i, g):
        kinds = self.kinds[i]
        gb = jnp.stack([_pack([_split_chips(g[k], self.axes[i][k])[j] for k in self.names[i]], F32, 256)
                        for j in range(N_CHIPS)])
        self.swapping = [gb.reshape(N_CHIPS, 2, kinds[0].h, kinds[0].c), g['w_up'],
                         g['w_down'].reshape(N_CHIPS, 2, kinds[2].h, kinds[2].c)]
        return SwapRider(self.swapping, kinds)

    def grads_rode(self, i, gots):
        self.pending = (i, [add_own_half(a, o, k, self.core_id, GRAD_WIRE_DTYPE, f"add_own_half_{i}_{t}")
                            for t, (a, o, k) in enumerate(zip(self.swapping, gots, self.kinds[i]))])

    def finish(self, chip_id):
        last, pairs = self.pending
        self.recvd[last] = (run_rider(ExchangeRider(pairs, self.kinds[last]), "exchange_chips"), pairs)
        halves = []
        for i in range(4):
            got, pairs = self.recvd[i]
            halves += [sum_chips(r, p, k, chip_id, f"sum_chips_{i}_{t}")
                       for t, (r, p, k) in enumerate(zip(got, pairs, self.kinds[i]))]
        joined = join_halves_sibling(halves)
        out, stacked = {}, {'xa_w_kv': [], 'ffn_w_up': [], 'ffn_w_down': []}
        for i, lw in enumerate(LAYER_WEIGHTS):
            red, up, down = joined[3 * i:3 * i + 3]
            parts = _unpack(red.reshape(-1, PACK_COLS), [self.shards[i][k].shape for k in self.names[i]])
            parts = dict(zip(self.names[i], parts), w_up=up.reshape(self.shards[i]['w_up'].shape),
                         w_down=down.reshape(self.shards[i]['w_down'].shape))
            for k, (n, l) in lw.items():
                if l is None:
                    out[n] = parts[k]
                else:
                    stacked[n].append(parts[k])
        out.update({n: jnp.stack(v) for n, v in stacked.items()})
        return out


def local_step(x, mem, positions, target, W, layers=None):
    s = x.shape[0]
    grads = {}
    scan_block = CHUNK * SCAN_CHUNKS
    ffn = layers or LocalLayers(W)

    inv_freq = ROPE_THETA ** (-jnp.arange(DIL_HD // 2, dtype=F32) / (DIL_HD // 2))
    ang = positions.astype(F32)[:, None] * inv_freq
    cosf = jnp.concatenate([jnp.cos(ang), jnp.cos(ang)], axis=-1)
    sinf = jnp.concatenate([-jnp.sin(ang), jnp.sin(ang)], axis=-1)

    mem_g = _row(W['mem_norm'])
    (mem_n,) = tmap("mem_norm", _norm_stage, [mem], [mem_g], [(D_MODEL, MXU_DTYPE)])
    kv_lay = _heads(0, 4, 64, LANE) + _heads(256, 4, 64, LANE, dst0=_XA_W)

    saved = []
    for i in range(4):
        kind = KINDS[i]
        lay = LAYOUT[kind]
        sv = dict(x0=x)
        wl = ffn.weights(i)
        w_in, w_out = wl['w_in'], wl['w_out']
        sv['w_seg'] = {n: _lay(w_in, 1, p, t, r).astype(MXU_DTYPE) for n, (p, t, r) in lay['segs'].items()}
        sv['wo_tok'] = _lay(w_out, 0, *lay['tok']).astype(MXU_DTYPE)
        sv['wo_xa'] = _lay(w_out, 0, *lay['xa']).astype(MXU_DTYPE)
        sv['w_kv'] = _lay(wl['w_kv'], 1, kv_lay, 2 * _XA_W).astype(MXU_DTYPE)
        sv['g1'] = _row(W['mix_norm'][i])
        (h,) = tmap(f"mix_norm_{i}", _norm_stage, [x], [sv['g1']], [(D_MODEL, MXU_DTYPE)])
        sv['h'] = h
        seg = dict(zip(sv['w_seg'], matmul_multi(h, list(sv['w_seg'].values()))))
        sv['seg'] = seg

        if kind == 'a':
            sv['w2'] = _lay(_lay(wl['w_gate2'], 1, _heads(0, 4, 96, LANE), 512), 0, [(0, 16, 0)], LANE)
            sv['bg'] = _row(_lay(W['a_b_gate'], 0, _heads(0, 4, 96, LANE), 512))
            sv['on'] = _row(_lay(W['a_o_norm'], 0, [(0, 192, 0)], 256))
            (la,) = tmap("gla_pre", _gla_pre, [seg['glr']], [sv['w2'], sv['bg']], [(512, F32)])
            sv['la'] = la
            sv['scan_fn'] = _gla_step(GLA_HEADS, 2 * LANE, GLA_DK ** -0.5)
            sv['scan_rows'] = [seg['q'], seg['k'], seg['v'], la]
            (o,), sv['states'] = rscan("gla_scan", sv['scan_fn'], [(LANE, 2 * LANE)] * GLA_HEADS, sv['scan_rows'], [],
                                       [(1024, F32)], scan_block)
            sv['o'] = o
            (tok,) = tmap("gla_post", _gla_post, [o, seg['og']], [sv['on']], [(1024, MXU_DTYPE)])
        elif kind == 'b':
            sv['qg'], sv['kg'] = _row(W['b_q_norm']), _row(W['b_k_norm'])
            os_, ls_ = [], []
            qkn = tmap("dil_pre", _dil_pre, [seg['q'], seg['k'], cosf, sinf], [sv['qg'], sv['kg']], [(512, F32)] * 6)
            sv['qn'], sv['kn'] = qkn[:3], qkn[3:]
            for g, (window, r) in enumerate(DIL_GROUPS):
                assert window // r == DIL_BLOCK and (s // r) % DIL_BLOCK == 0
                o, lse = dil_attn(f"dil_attn_{g}", sv['qn'][g], sv['kn'][g], seg['v'], r, g)
                os_.append(o)
                ls_.append(lse)
            sv['os'], sv['ls'] = os_, ls_
            (tok,) = tmap("dil_merge", _dil_merge, os_ + ls_, [], [(512, MXU_DTYPE)])
        elif kind == 'c':
            sv['cw'] = _lay(W['c_conv_w'], 1, _XBC_PIECES, 2048)
            sv['cb'] = _row(_lay(W['c_conv_b'], 0, _XBC_PIECES, 2048))
            sv['dtb'] = _row(_lay(W['c_dt_bias'], 0, [], 1536, _HEAD_REPS))
            sv['alog'] = _row(_lay(W['c_a_log'], 0, [], 1536, _HEAD_REPS))
            sv['dsk'] = _row(_lay(W['c_d'], 0, [], 1536, _HEAD_REPS))
            sv['cn'] = _row(_lay(W['c_norm'], 0, _XS_PIECES, 1536))
            xact = conv_fwd("ssm_conv", seg['xbc'], sv['cw'], sv['cb'], 'silu', F32, 512)
            sv['xact'] = xact
            sv['scan_rows'] = [xact, seg['dt']]
            sv['scan_params'] = [sv['dtb'], sv['alog'], sv['dsk']]
            (yv,), sv['states'] = rscan("ssd_scan", _ssd_step, [(LANE, LANE)] * SSM_HEADS, sv['scan_rows'],
                                        sv['scan_params'], [(1536, F32)], scan_block)
            sv['y'] = yv
            (tok,) = tmap("ssd_post", _mamba_post, [yv, seg['z']], [sv['cn']], [(1536, MXU_DTYPE)])
        else:
            sv['lbnd'] = W['d_lower_bounds']
            sv['on'] = _row(W['d_o_norm'])
            qq, kk, la = tmap("hgrn_pre", _hgrn_pre, [seg['q'], seg['f']], [sv['lbnd']], [(768, F32)] * 3)
            sv['scan_fn'] = _gla_step(HGRN_HEADS, LANE, 1.0)
            sv['scan_rows'] = [qq, kk, seg['i'], la]
            (o,), sv['states'] = rscan("hgrn_scan", sv['scan_fn'], [(LANE, LANE)] * HGRN_HEADS, sv['scan_rows'], [],
                                       [(768, F32)], scan_block)
            sv['o'] = o
            (tok,) = tmap("hgrn_post", _hgrn_post, [o, seg['og']], [sv['on']], [(768, MXU_DTYPE)])
        sv['tok'] = tok

        kv = matmul(mem_n, sv['w_kv'])
        sv['kv'] = kv
        sv['xqg'] = _row(_lay(W['xa_q_norm'][i], 0, [(0, 64, 0)], LANE))
        sv['xkg'] = _row(_lay(W['xa_k_norm'][i], 0, [(0, 64, 0)], LANE))
        (xa,) = tmap(f"xattn_{i}", _xattn, [seg['xq']], [kv, sv['xqg'], sv['xkg']], [(_XA_W, MXU_DTYPE)])
        sv['xa'] = xa
        x = matmul_sum([tok, xa], [sv['wo_tok'], sv['wo_xa']], False, add=x)
        sv['x1'] = x

        sv['g2'] = _row(W['ffn_norm'][i])
        sv['fcw'] = W['ffn_conv_w'][i]
        sv['fcb'] = _row(W['ffn_conv_b'][i])
        (h2,) = tmap(f"ffn_norm_{i}", _norm_stage, [x], [sv['g2']], [(D_MODEL, MXU_DTYPE)])
        sv['h2'] = h2
        w_up, w_down = wl['w_up'], wl['w_down']
        sv['w_up'], sv['w_down'] = w_up, w_down
        u0 = matmul(h2, w_up)
        sv['u0'] = u0
        rider = ffn.fwd_rider(i)
        act = conv_fwd("ffn_conv", u0, sv['fcw'], sv['fcb'], 'glu', MXU_DTYPE, 1408, rider=rider)
        if rider is not None:
            act, rode = act
            ffn.fwd_rode(i, rode)
        sv['act'] = act
        x = matmul(act, w_down, add=x)
        saved.append(sv)

    loss_acc, dx = loss_head(x, target)

    g_stack = {n: [None] * 4 for n in ('mix_norm', 'xa_q_norm', 'xa_k_norm', 'ffn_norm', 'ffn_conv_w', 'ffn_conv_b')}
    d_memn = None
    for i in reversed(range(4)):
        kind = KINDS[i]
        lay = LAYOUT[kind]
        sv = saved[i]
        seg = sv['seg']
        w_up, w_down = sv['w_up'], sv['w_down']
        gl = {}
        dact = matmul(dx, w_down, tb=True)
        gl['w_down'] = matmul(sv['act'], dx, ta=True)
        rider = ffn.bwd_rider(i)
        (du_g, du_v), dcw, dcb, rode = conv_bwd("ffn_conv_bwd", sv['u0'], sv['fcw'], sv['fcb'], dact, 'glu', 1408,
                                                rider=rider)
        if rider is not None:
            ffn.bwd_rode(i, rode)
        g_stack['ffn_conv_w'][i], g_stack['ffn_conv_b'][i] = dcw, dcb[0]
        dh2 = matmul(du_v, w_up, tb=True, b_koff=D_FF, add=matmul(du_g, w_up, tb=True))
        g_up = jnp.zeros((1,) + w_up.shape, F32)
        g_up = matmul(sv['h2'], du_g, ta=True, into=(g_up, 0, 0))
        g_up = matmul(sv['h2'], du_v, ta=True, into=(g_up, 0, D_FF))
        gl['w_up'] = g_up[0]
        (dx,), (dg2,) = tmap_bwd(f"ffn_norm_bwd_{i}", _norm_stage, [sv['x1']], [sv['g2']], [dh2], [True], {0: dx})
        g_stack['ffn_norm'][i] = dg2[0]
        dtok = matmul(dx, sv['wo_tok'], tb=True)
        dxa = matmul(dx, sv['wo_xa'], tb=True)
        g_wo = _unlay_parts(matmul(sv['tok'], dx, ta=True), 0, lay['tok'][0]) \
            + _unlay_parts(matmul(sv['xa'], dx, ta=True), 0, lay['xa'][0])
        gl['w_out'] = _join(g_wo, 0)
        (dxq,), (dkv, dqg, dkg) = tmap_bwd(f"xattn_bwd_{i}", _xattn, [seg['xq']], [sv['kv'], sv['xqg'], sv['xkg']],
                                           [dxa], [True])
        g_stack['xa_q_norm'][i], g_stack['xa_k_norm'][i] = dqg[0, :XA_HD], dkg[0, :XA_HD]
        gl['w_kv'] = _join(_unlay_parts(matmul(mem_n, dkv, ta=True), 1, kv_lay), 1)
        d_memn = matmul(dkv, sv['w_kv'], tb=True, add=d_memn)
        dseg = dict(xq=dxq)
        if kind == 'a':
            (do, dog), (don,) = tmap_bwd("gla_post_bwd", _gla_post, [sv['o'], seg['og']], [sv['on']], [dtok],
                                         [True, True], grad_dtype=F32)
            grads['a_o_norm'] = don[0, :GLA_DV]
            (dq, dk, dv, dla), _ = rscan_bwd("gla_scan_bwd", sv['scan_fn'], sv['states'], sv['scan_rows'], [], [do],
                                             scan_block, grad_dtype=F32)
            (dglr,), (dw2, dbg) = tmap_bwd("gla_pre_bwd", _gla_pre, [seg['glr']], [sv['w2'], sv['bg']], [dla], [True])
            gl['w_gate2'] = _join(_unlay_parts(dw2[:GLA_RANK], 1, _heads(0, 4, 96, LANE)), 1)
            grads['a_b_gate'] = _join(_unlay_parts(dbg[0], 0, _heads(0, 4, 96, LANE)), 0)
            dseg.update(q=dq, k=dk, v=dv, glr=dglr, og=dog)
        elif kind == 'b':
            res, _ = tmap_bwd("dil_merge_bwd", _dil_merge, sv['os'] + sv['ls'], [], [dtok], [True] * 6, grad_dtype=F32)
            dqn, dkn, dvs = [], [], []
            for g, (_, r) in enumerate(DIL_GROUPS):
                a_, b_, c_ = dil_attn_bwd(f"dil_attn_bwd_{g}", sv['qn'][g], sv['kn'][g], seg['v'], res[g], res[3 + g],
                                          r, g)
                dqn.append(a_)
                dkn.append(b_)
                dvs.append(c_)
            (dq, dk), (dqg, dkg) = tmap_bwd("dil_pre_bwd", _dil_pre, [seg['q'], seg['k'], cosf, sinf],
                                            [sv['qg'], sv['kg']], dqn + dkn, [True, True, False, False])
            dseg.update(q=dq, k=dk, v=jnp.concatenate(dvs, axis=1))
            grads['b_q_norm'], grads['b_k_norm'] = dqg[0], dkg[0]
        elif kind == 'c':
            (dy, dz), (dcn,) = tmap_bwd("ssd_post_bwd", _mamba_post, [sv['y'], seg['z']], [sv['cn']], [dtok],
                                        [True, True], grad_dtype=F32)
            grads['c_norm'] = _join(_unlay_parts(dcn[0], 0, _XS_PIECES), 0)
            (dxact, ddt), (ddtb, dalog, ddsk) = rscan_bwd("ssd_scan_bwd", _ssd_step, sv['states'], sv['scan_rows'],
                                                          sv['scan_params'], [dy], scan_block, grad_dtype=F32)
            for nm, gv in (('c_dt_bias', ddtb), ('c_a_log', dalog), ('c_d', ddsk)):
                grads[nm] = _join(_unlay_parts(gv[0], 0, [], _HEAD_REPS), 0)
            (dxbc,), dcw, dcb, _ = conv_bwd("ssm_conv_bwd", seg['xbc'], sv['cw'], sv['cb'], dxact, 'silu', 512)
            grads['c_conv_w'] = _join(_unlay_parts(dcw, 1, _XBC_PIECES), 1)
            grads['c_conv_b'] = _join(_unlay_parts(dcb[0], 0, _XBC_PIECES), 0)
            dseg.update(z=dz, xbc=dxbc, dt=ddt)
        else:
            (do, dog), (don,) = tmap_bwd("hgrn_post_bwd", _hgrn_post, [sv['o'], seg['og']], [sv['on']], [dtok],
                                         [True, True], grad_dtype=F32)
            grads['d_o_norm'] = don[0]
            (dqq, dkk, di, dla), _ = rscan_bwd("hgrn_scan_bwd", sv['scan_fn'], sv['states'], sv['scan_rows'], [], [do],
                                               scan_block, grad_dtype=F32)
            (dq, df), (dlb,) = tmap_bwd("hgrn_pre_bwd", _hgrn_pre, [seg['q'], seg['f']], [sv['lbnd']], [dqq, dkk, dla],
                                        [True, True])
            grads['d_lower_bounds'] = dlb
            dseg.update(q=dq, f=df, i=di, og=dog)
        names = list(lay['segs'])
        dh = matmul_sum([dseg[n] for n in names], [sv['w_seg'][n] for n in names], True)
        g_in = []
        for n, (p, t, rp) in lay['segs'].items():
            g_in += _unlay_parts(matmul(sv['h'], dseg[n], ta=True), 1, p, rp)
        gl['w_in'] = _join(g_in, 1)
        rider = ffn.grads_rider(i, gl)
        (dx,), (dg1,), *rode = tmap_bwd(f"mix_norm_bwd_{i}", _norm_stage, [sv['x0']], [sv['g1']], [dh], [True], {0: dx},
                                        rider=rider)
        if rider is not None:
            ffn.grads_rode(i, rode[0])
        g_stack['mix_norm'][i] = dg1[0]

    _, (dmg,) = tmap_bwd("mem_norm_bwd", _norm_stage, [mem], [mem_g], [d_memn], [False])
    grads['mem_norm'] = dmg[0]
    for n, parts in g_stack.items():
        grads[n] = jnp.stack(parts)
    if isinstance(ffn, LocalLayers):
        grads.update(ffn.whole_grads())
    return loss_acc, dx, grads


def _pack(arrs, dtype, row_multiple=PACK_ROWS):
    parts, rows = [], 0
    for a in arrs:
        f = a.reshape(-1).astype(dtype)
        unit = PACK_ROWS * PACK_COLS
        pad = (-f.shape[0]) % unit
        if pad:
            f = jnp.concatenate([f, jnp.zeros((pad,), dtype)])
        parts.append(f.reshape(-1, PACK_COLS))
        rows += parts[-1].shape[0]
    if rows % row_multiple:
        parts.append(jnp.zeros((row_multiple - rows % row_multiple, PACK_COLS), dtype))
    return jnp.concatenate(parts, axis=0)


def _unpack(buf, shapes):
    out, row = [], 0
    for sh in shapes:
        n = int(np.prod(sh))
        rows = -(-n // (PACK_ROWS * PACK_COLS)) * PACK_ROWS
        out.append(buf[row:row + rows].reshape(-1)[:n].reshape(sh))
        row += rows
    return out


def _pack_rows(arrs):
    parts = []
    for a in arrs:
        f = a.reshape(-1).astype(F32)
        parts.append(jnp.pad(f, (0, (-f.shape[0]) % PACK_COLS)))
    flat = jnp.concatenate(parts)
    rows = flat.shape[0] // PACK_COLS
    return jnp.pad(flat, (0, (-rows % 16) * PACK_COLS)).reshape(-1, PACK_COLS)


def _unpack_rows(buf, shapes):
    flat, out, pos = buf.reshape(-1), [], 0
    for sh in shapes:
        n = int(np.prod(sh))
        out.append(flat[pos:pos + n].reshape(sh))
        pos += -(-n // PACK_COLS) * PACK_COLS
    return out


def _split_chips(a, axis):
    sh = a.shape
    return jnp.moveaxis(a.reshape(sh[:axis] + (N_CHIPS, sh[axis] // N_CHIPS) + sh[axis + 1:]), axis, 0)


def _merge_chips(a, axis):
    a = jnp.moveaxis(a, 0, axis)
    sh = a.shape
    return a.reshape(sh[:axis] + (sh[axis] * sh[axis + 1],) + sh[axis + 2:])


def kernel(x, mem, positions, mem_norm, mix_norm, xa_w_kv, xa_q_norm, xa_k_norm, ffn_norm, ffn_w_up, ffn_conv_w, ffn_conv_b, ffn_w_down, a_w_in, a_w_gate2, a_b_gate, a_o_norm, a_w_out, b_w_in, b_q_norm, b_k_norm, b_w_out, c_w_in, c_conv_w, c_conv_b, c_dt_bias, c_a_log, c_d, c_norm, c_w_out, d_w_in, d_lower_bounds, d_o_norm, d_w_out, loss_target, m_mem_norm, m_mix_norm, m_xa_w_kv, m_xa_q_norm, m_xa_k_norm, m_ffn_norm, m_ffn_w_up, m_ffn_conv_w, m_ffn_conv_b, m_ffn_w_down, m_a_w_in, m_a_w_gate2, m_a_b_gate, m_a_o_norm, m_a_w_out, m_b_w_in, m_b_q_norm, m_b_k_norm, m_b_w_out, m_c_w_in, m_c_conv_w, m_c_conv_b, m_c_dt_bias, m_c_a_log, m_c_d, m_c_norm, m_c_w_out, m_d_w_in, m_d_lower_bounds, m_d_o_norm, m_d_w_out, v_mem_norm, v_mix_norm, v_xa_w_kv, v_xa_q_norm, v_xa_k_norm, v_ffn_norm, v_ffn_w_up, v_ffn_conv_w, v_ffn_conv_b, v_ffn_w_down, v_a_w_in, v_a_w_gate2, v_a_b_gate, v_a_o_norm, v_a_w_out, v_b_w_in, v_b_q_norm, v_b_k_norm, v_b_w_out, v_c_w_in, v_c_conv_w, v_c_conv_b, v_c_dt_bias, v_c_a_log, v_c_d, v_c_norm, v_c_w_out, v_d_w_in, v_d_lower_bounds, v_d_o_norm, v_d_w_out):
    args = locals()
    w = {n: args[n] for n in WEIGHTS}
    m = {n: args['m_' + n] for n in WEIGHTS}
    v = {n: args['v_' + n] for n in WEIGHTS}
    cx, cy, cc = lax.axis_index("x"), lax.axis_index("y"), lax.axis_index("c")
    chip = 2 * cx + cy

    core_id, chip_id = cc.reshape(1).astype(jnp.int32), chip.reshape(1).astype(jnp.int32)
    layers = ShardedLayers(w, core_id)
    layers.first_gather()
    full = {}
    small_sharded = [n for n in SMALL if n in SHARD_AXIS]
    sg = allgather_devices(_pack([w[n] for n in small_sharded], F32))
    per_chip_s = [_unpack(sg[2 * j], [w[n].shape for n in small_sharded]) for j in range(N_CHIPS)]
    for k, n in enumerate(small_sharded):
        full[n] = _merge_chips(jnp.stack([per_chip_s[j][k] for j in range(N_CHIPS)]), SHARD_AXIS[n])
    for n in SMALL:
        if n not in SHARD_AXIS:
            full[n] = w[n]

    loss_acc, dx, grads = local_step(x[0], mem[0], positions[0], loss_target[0], full, layers)
    loss = lax.psum(jnp.sum(loss_acc), ("x", "y", "c"))

    g_big = layers.finish(chip_id)

    small_full_shapes = [grads[n].shape for n in SMALL]
    gs = sum_slabs(allgather_devices(_pack_rows([grads[n] for n in SMALL])), "sum_devices")
    g_small = {}
    for n, gfull in zip(SMALL, _unpack_rows(gs, small_full_shapes)):
        if n in SHARD_AXIS:
            ax = SHARD_AXIS[n]
            size = gfull.shape[ax] // N_CHIPS
            gfull = lax.dynamic_slice_in_dim(gfull, chip * size, size, axis=ax)
        g_small[n] = gfull

    g_out, delta, new_m, new_v = {**g_big, **g_small}, {}, {}, {}
    for n in WEIGHTS:
        sh = w[n].shape
        two_d = (-1, sh[-1])
        d_, m_, v_ = adamw(f"adamw_{n}", w[n].reshape(two_d), g_out[n].reshape(two_d), m[n].reshape(two_d),
                           v[n].reshape(two_d))
        delta[n], new_m[n], new_v[n] = d_.reshape(sh), m_.reshape(sh), v_.reshape(sh)

    return (loss, dx[None], *[g_out[n] for n in WEIGHTS], *[delta[n] for n in WEIGHTS],
            *[new_m[n] for n in WEIGHTS], *[new_v[n] for n in WEIGHTS])
```

```python
import functools
import math

import jax
import jax.numpy as jnp
import numpy as np
from jax import lax
from jax.experimental import pallas as pl
from jax.experimental.pallas import tpu as pltpu

F32 = jnp.float32
MXU_DTYPE = jnp.bfloat16
GRAD_WIRE_DTYPE = jnp.bfloat16
VMEM_LIMIT_V7X = 56 * 1024 * 1024
LANE = 128
SUBLANE = 8

D_MODEL = 1024
N_MEM = 256
EPS = 1e-6
ROPE_THETA = 10000.0
CHUNK = 64
XA_HEADS, XA_HD = 4, 64
GLA_HEADS, GLA_DK, GLA_DV, GLA_RANK, GLA_GATE_NORM = 4, 96, 192, 16, 16.0
DIL_GROUPS = ((128, 1), (512, 4), (2048, 16))
DIL_HEADS, DIL_HD, DIL_BLOCK = 4, 128, 128
SSM_HD, SSM_HEADS, SSM_GROUPS, SSM_STATE, SSM_CONV = 64, 12, 2, 128, 4
HGRN_HEADS, HGRN_DK = 6, 128
D_FF = 2816
FFN_CONV = 3
ADAM_LR, ADAM_B1, ADAM_B2, ADAM_EPS, ADAM_WD, ADAM_STEP = 0.001, 0.9, 0.999, 1e-08, 0.01, 10

MM_TILES = (2816, 1408, 1024, 768, 512, 384, 256, 128)
MM_K_TILES = (2816, 2048, 1536, 1408, 1024, 768, 512, 384, 256, 128)
MM_MIN_OUT_TILE = 512 * 1024
MM_VMEM_BUDGET = 40 * 1024 * 1024
ROW_BLOCK = 256
SCAN_CHUNKS = 2
PACK_COLS = 1024
PACK_ROWS = 32

WEIGHTS = ['mem_norm', 'mix_norm', 'xa_w_kv', 'xa_q_norm', 'xa_k_norm', 'ffn_norm', 'ffn_w_up', 'ffn_conv_w',
           'ffn_conv_b', 'ffn_w_down', 'a_w_in', 'a_w_gate2', 'a_b_gate', 'a_o_norm', 'a_w_out', 'b_w_in', 'b_q_norm',
           'b_k_norm', 'b_w_out', 'c_w_in', 'c_conv_w', 'c_conv_b', 'c_dt_bias', 'c_a_log', 'c_d', 'c_norm', 'c_w_out',
           'd_w_in', 'd_lower_bounds', 'd_o_norm', 'd_w_out']
SHARD_AXIS = {'xa_w_kv': 1, 'ffn_w_up': 2, 'ffn_conv_w': 2, 'ffn_w_down': 1, 'a_w_in': 1, 'a_w_gate2': 1, 'a_w_out': 0,
              'b_w_in': 1, 'b_w_out': 1, 'c_w_in': 1, 'c_conv_w': 1, 'c_w_out': 0, 'd_w_in': 1, 'd_w_out': 0}
BIG = ['xa_w_kv', 'ffn_w_up', 'ffn_w_down', 'a_w_in', 'a_w_gate2', 'a_w_out', 'b_w_in', 'b_w_out', 'c_w_in', 'c_w_out',
       'd_w_in', 'd_w_out']
SMALL = [n for n in WEIGHTS if n not in BIG]
LAYERED = ['ffn_w_up', 'ffn_w_down']
N_CHIPS = 4
N_DEV = 8


class _MatmulSet:
    def __init__(self, cast, precision):
        def dot(a, b, dims):
            if cast:
                a = a.astype(MXU_DTYPE)
                b = b.astype(MXU_DTYPE)
            return lax.dot_general(a, b, (dims, ((), ())), precision=precision, preferred_element_type=F32)

        @jax.custom_vjp
        def nn(a, b):
            return dot(a, b, ((1,), (0,)))

        @jax.custom_vjp
        def nt(a, b):
            return dot(a, b, ((1,), (1,)))

        @jax.custom_vjp
        def tn(a, b):
            return dot(a, b, ((0,), (0,)))

        nn.defvjp(lambda a, b: (nn(a, b), (a, b)), lambda r, g: (nt(g, r[1]), tn(r[0], g)))
        nt.defvjp(lambda a, b: (nt(a, b), (a, b)), lambda r, g: (nn(g, r[1]), tn(g, r[0])))
        tn.defvjp(lambda a, b: (tn(a, b), (a, b)), lambda r, g: (nt(r[1], g), nn(r[0], g)))
        self.nn, self.nt, self.tn = nn, nt, tn


mm = _MatmulSet(True, None)
hi = _MatmulSet(False, lax.Precision.HIGHEST)


def _sigmoid(x):
    return jax.nn.sigmoid(x)


def _silu(x):
    return x * jax.nn.sigmoid(x)


def _softplus(x):
    return jnp.maximum(x, 0.0) + jnp.log1p(jnp.exp(-jnp.abs(x)))


def _rms(x, g, n_real=None):
    n = n_real or x.shape[-1]
    ms = jnp.sum(x * x, axis=-1, keepdims=True) * (1.0 / n)
    return x * lax.rsqrt(ms + EPS) * g


@jax.custom_vjp
def _swap_halves(x):
    return pltpu.roll(x, 64, 1)


_swap_halves.defvjp(lambda x: (_swap_halves(x), None), lambda _, g: (_swap_halves(g),))


def _tile(n, cands):
    for c in cands:
        if n % c == 0:
            return c
    raise ValueError(f"no tile for {n} among {cands}")


def _cparams(sem):
    return pltpu.CompilerParams(dimension_semantics=sem, vmem_limit_bytes=VMEM_LIMIT_V7X)


def _f32(v):
    return v.astype(F32) if jnp.issubdtype(v.dtype, jnp.floating) else v


def matmul(a, b, *, ta=False, tb=False, add=None, out_dtype=F32, b_layer=None, b_koff=0, into=None):
    m, k = (a.shape[1], a.shape[0]) if ta else a.shape
    b2 = b.shape[1:] if b_layer is not None else b.shape
    n = b2[0] if tb else b2[1]
    assert b_koff + k <= (b2[1] if tb else b2[0]), (a.shape, b.shape, ta, tb, b_koff)
    sa, sb, so = a.dtype.itemsize, b.dtype.itemsize, jnp.dtype(out_dtype).itemsize
    n_align = math.gcd(n, into[2]) if into is not None and into[2] else n
    k_align = math.gcd(k, b_koff) if b_koff else k

    def vmem(tm_, tn_, tk_):
        return (2 * tm_ * tk_ * sa + 2 * tk_ * tn_ * sb + 2 * tm_ * tn_ * so + (tm_ * tn_ * 4 if tk_ < k else 0)
                + (2 * tm_ * tn_ * add.dtype.itemsize if add is not None else 0))

    for tk in [t for t in MM_K_TILES if k_align % t == 0]:
        fits = [(tm_ * tn_, tm_, tn_) for tm_ in MM_TILES if m % tm_ == 0 for tn_ in MM_TILES
                if n % tn_ == 0 and n_align % tn_ == 0 and vmem(tm_, tn_, tk) <= MM_VMEM_BUDGET]
        if fits and (max(fits)[0] >= min(MM_MIN_OUT_TILE, m * n) or tk == MM_K_TILES[-1]):
            break
    _, tm, tn = max(fits)
    nk = k // tk
    dims = (((0,) if ta else (1,)), ((1,) if tb else (0,)))
    n_extra = (add is not None) + (into is not None)

    def body(*refs):
        a_ref, b_ref = refs[0], refs[1]
        add_ref = refs[2] if add is not None else None
        o_ref = refs[2 + n_extra]
        part = lax.dot_general(a_ref[...].astype(MXU_DTYPE), b_ref[...].astype(MXU_DTYPE), (dims, ((), ())),
                               preferred_element_type=F32)

        def finish(r):
            if add_ref is not None:
                r = r + add_ref[...].astype(F32)
            o_ref[...] = r.astype(o_ref.dtype)

        if nk == 1:
            finish(part)
            return
        acc = refs[-1]
        kk = pl.program_id(2)

        @pl.when(kk == 0)
        def _():
            acc[...] = part

        @pl.when(kk > 0)
        def _():
            acc[...] += part

        @pl.when(kk == nk - 1)
        def _():
            finish(acc[...])

    a_spec = pl.BlockSpec((tk, tm), lambda i, j, q: (q, i)) if ta else pl.BlockSpec((tm, tk), lambda i, j, q: (i, q))
    ko = b_koff // tk
    if b_layer is None:
        b_spec = (pl.BlockSpec((tn, tk), lambda i, j, q: (j, q + ko)) if tb
                  else pl.BlockSpec((tk, tn), lambda i, j, q: (q + ko, j)))
    else:
        b_spec = (pl.BlockSpec((None, tn, tk), lambda i, j, q: (b_layer, j, q + ko)) if tb
                  else pl.BlockSpec((None, tk, tn), lambda i, j, q: (b_layer, q + ko, j)))
    o_spec = pl.BlockSpec((tm, tn), lambda i, j, q: (i, j))
    ins, specs = [a, b], [a_spec, b_spec]
    if add is not None:
        ins.append(add)
        specs.append(o_spec)
    aliases = {}
    out_shape = jax.ShapeDtypeStruct((m, n), out_dtype)
    if into is not None:
        buf, layer, col0 = into
        assert buf.shape[1] == m and buf.dtype == out_dtype
        co = col0 // tn
        ins.append(buf)
        specs.append(_ANY)
        aliases = {len(ins) - 1: 0}
        o_spec = pl.BlockSpec((None, tm, tn), lambda i, j, q: (layer, i, j + co))
        out_shape = jax.ShapeDtypeStruct(buf.shape, buf.dtype)
    return pl.pallas_call(
        body, name=f"mm_{m}x{k}x{n}_{int(ta)}{int(tb)}{int(add is not None)}{int(b_layer is not None)}{int(into is not None)}",
        grid=(m // tm, n // tn, nk), in_specs=specs, out_specs=o_spec, out_shape=out_shape,
        input_output_aliases=aliases,
        scratch_shapes=[pltpu.VMEM((tm, tn), F32)] if nk > 1 else [],
        compiler_params=_cparams(("parallel", "parallel", "arbitrary")),
    )(*ins)


def _resident_tm(m, row_bytes, resident_bytes):
    for tm in (512, 256, 128):
        if m % tm == 0 and 2 * (resident_bytes + tm * row_bytes) <= MM_VMEM_BUDGET:
            return tm
    return 128


def matmul_sum(a_list, b_list, tb, add=None, b_cols=None):
    n_ops = len(a_list)
    m, n = a_list[0].shape[0], b_list[0].shape[0 if tb else 1]
    dims = ((1,), (1,) if tb else (0,))
    b_specs = ([_whole_spec(b) for b in b_list] if b_cols is None else
               [pl.BlockSpec((n, a.shape[1]), lambda i, cb=cb: (0, cb)) for a, cb in zip(a_list, b_cols)])
    tm = _resident_tm(m, sum(a.shape[1] * a.dtype.itemsize for a in a_list) + n * 4 * (1 + (add is not None)),
                      sum(n * a.shape[1] * b.dtype.itemsize for a, b in zip(a_list, b_list)))

    def body(*refs):
        acc = refs[2 * n_ops][...] if add is not None else None
        for t in range(n_ops):
            part = lax.dot_general(refs[t][...].astype(MXU_DTYPE), refs[n_ops + t][...].astype(MXU_DTYPE),
                                   (dims, ((), ())), preferred_element_type=F32)
            acc = part if acc is None else acc + part
        refs[-1][...] = acc

    o_spec = pl.BlockSpec((tm, n), lambda i: (i, 0))
    return pl.pallas_call(
        body, name=f"mm_sum_{n_ops}x{sum(a.shape[1] for a in a_list)}_{int(tb)}{int(add is not None)}", grid=(m // tm,),
        in_specs=[pl.BlockSpec((tm, a.shape[1]), lambda i: (i, 0)) for a in a_list] + b_specs
        + ([o_spec] if add is not None else []),
        out_specs=o_spec, out_shape=jax.ShapeDtypeStruct((m, n), F32),
        compiler_params=_cparams(("parallel",)),
    )(*a_list, *b_list, *([add] if add is not None else []))


def matmul_multi(a, b_list):
    n_ops = len(b_list)
    m = a.shape[0]
    tm = _resident_tm(m, a.shape[1] * a.dtype.itemsize + 4 * sum(b.shape[1] for b in b_list),
                      sum(b.size * b.dtype.itemsize for b in b_list))

    def body(*refs):
        av = refs[0][...].astype(MXU_DTYPE)
        for t in range(n_ops):
            refs[1 + n_ops + t][...] = jnp.dot(av, refs[1 + t][...].astype(MXU_DTYPE), preferred_element_type=F32)

    return pl.pallas_call(
        body, name=f"mm_multi_{n_ops}x{sum(b.shape[1] for b in b_list)}", grid=(m // tm,),
        in_specs=[pl.BlockSpec((tm, a.shape[1]), lambda i: (i, 0))] + [_whole_spec(b) for b in b_list],
        out_specs=[pl.BlockSpec((tm, b.shape[1]), lambda i: (i, 0)) for b in b_list],
        out_shape=[jax.ShapeDtypeStruct((m, b.shape[1]), F32) for b in b_list],
        compiler_params=_cparams(("parallel",)),
    )(a, *b_list)


def _row_spec(a, block):
    return pl.BlockSpec((block, a.shape[1]), lambda i: (i, 0))


def _whole_spec(a):
    return pl.BlockSpec(a.shape, lambda i: (0,) * a.ndim)


def tmap(name, fn, rows, params, outs, block=ROW_BLOCK):
    s = rows[0].shape[0]
    block = min(block, s)
    nr, npar = len(rows), len(params)

    def body(*refs):
        res = fn(*[_f32(r[...]) for r in refs[:nr]], *[_f32(p[...]) for p in refs[nr:nr + npar]])
        for o_ref, v in zip(refs[nr + npar:], res, strict=True):
            o_ref[...] = v.astype(o_ref.dtype)

    return pl.pallas_call(
        body, name=name, grid=(s // block,),
        in_specs=[_row_spec(a, block) for a in rows] + [_whole_spec(p) for p in params],
        out_specs=[pl.BlockSpec((block, w), lambda i: (i, 0)) for w, _ in outs],
        out_shape=[jax.ShapeDtypeStruct((s, w), dt) for w, dt in outs],
        compiler_params=_cparams(("parallel",)),
    )(*rows, *params)


def tmap_bwd(name, fn, rows, params, douts, row_grad, row_add=None, grad_dtype=None, block=ROW_BLOCK, rider=None):
    s = rows[0].shape[0]
    block = min(block, s)
    grad_dtype = grad_dtype or MXU_DTYPE
    nr, npar, nd = len(rows), len(params), len(douts)
    gr = [i for i in range(nr) if row_grad[i]]
    row_add = row_add or {}
    adds = [row_add[i] for i in gr if i in row_add]

    def body(*refs):
        rv = [_f32(r[...]) for r in refs[:nr]]
        pv = [_f32(p[...]) for p in refs[nr:nr + npar]]
        dv = tuple(_f32(d[...]) for d in refs[nr + npar:nr + npar + nd])
        add_refs = list(refs[nr + npar + nd:nr + npar + nd + len(adds)])
        out_refs = refs[nr + npar + nd + len(adds):]

        def f(*diff):
            rr = list(rv)
            for n_, i_ in enumerate(gr):
                rr[i_] = diff[n_]
            return tuple(fn(*rr, *diff[len(gr):]))

        _, vjp = jax.vjp(f, *[rv[i_] for i_ in gr], *pv)
        g = vjp(dv)
        for n_, i_ in enumerate(gr):
            v = g[n_]
            if i_ in row_add:
                v = v + add_refs.pop(0)[...].astype(F32)
            out_refs[n_][...] = v.astype(out_refs[n_].dtype)
        first = pl.program_id(0) == 0
        for n_ in range(npar):
            ref = out_refs[len(gr) + n_]

            @pl.when(first)
            def _(ref=ref):
                ref[...] = jnp.zeros_like(ref)

            ref[...] += g[len(gr) + n_]

    grid = (s // block,)
    n_out = len(gr) + npar
    body, r_ins, r_in_specs, r_out_specs, r_out_shapes, r_scratch, _ = _ride(
        body, rider, nr + npar + nd + len(adds), n_out, 0, grid)
    res = pl.pallas_call(
        body, name=name, grid=grid,
        in_specs=[_row_spec(a, block) for a in rows] + [_whole_spec(p) for p in params]
        + [_row_spec(d, block) for d in douts] + [_row_spec(a, block) for a in adds] + r_in_specs,
        out_specs=[_row_spec(rows[i], block) for i in gr] + [_whole_spec(p) for p in params] + r_out_specs,
        out_shape=[jax.ShapeDtypeStruct(rows[i].shape, F32 if i in row_add else grad_dtype) for i in gr]
        + [jax.ShapeDtypeStruct(p.shape, F32) for p in params] + r_out_shapes,
        scratch_shapes=r_scratch, compiler_params=_cparams(("arbitrary",)),
    )(*rows, *params, *douts, *adds, *r_ins)
    if rider is not None:
        return list(res[:len(gr)]), list(res[len(gr):n_out]), rider.results(res[n_out:])
    return list(res[:len(gr)]), list(res[len(gr):])


def rscan(name, fn, state_shapes, rows, params, outs, block):
    s = rows[0].shape[0]
    nsteps = s // block
    nr, npar, no, ns = len(rows), len(params), len(outs), len(state_shapes)

    def body(*refs):
        out_refs = refs[nr + npar:nr + npar + no]
        sav_refs = refs[nr + npar + no:nr + npar + no + ns]
        st_refs = refs[nr + npar + no + ns:]

        @pl.when(pl.program_id(0) == 0)
        def _():
            for st in st_refs:
                st[...] = jnp.zeros_like(st)

        sts = tuple(st[...] for st in st_refs)
        for sv, v in zip(sav_refs, sts):
            sv[...] = v
        new, res = fn(sts, *[_f32(r[...]) for r in refs[:nr]], *[_f32(p[...]) for p in refs[nr:nr + npar]])
        for st, v in zip(st_refs, new, strict=True):
            st[...] = v
        for o_ref, v in zip(out_refs, res, strict=True):
            o_ref[...] = v.astype(o_ref.dtype)

    res = pl.pallas_call(
        body, name=name, grid=(nsteps,),
        in_specs=[_row_spec(a, block) for a in rows] + [_whole_spec(p) for p in params],
        out_specs=[pl.BlockSpec((block, w), lambda i: (i, 0)) for w, _ in outs]
        + [pl.BlockSpec(sh, lambda i: (i, 0)) for sh in state_shapes],
        out_shape=[jax.ShapeDtypeStruct((s, w), dt) for w, dt in outs]
        + [jax.ShapeDtypeStruct((nsteps * sh[0], sh[1]), F32) for sh in state_shapes],
        scratch_shapes=[pltpu.VMEM(sh, F32) for sh in state_shapes],
        compiler_params=_cparams(("arbitrary",)),
    )(*rows, *params)
    return list(res[:no]), list(res[no:])


def rscan_bwd(name, fn, saved, rows, params, douts, block, grad_dtype=None):
    s = rows[0].shape[0]
    nsteps = s // block
    grad_dtype = grad_dtype or MXU_DTYPE
    nr, npar, nd, ns = len(rows), len(params), len(douts), len(saved)
    state_shapes = [(sv.shape[0] // nsteps, sv.shape[1]) for sv in saved]

    def body(*refs):
        rv = [_f32(r[...]) for r in refs[:nr]]
        pv = [_f32(p[...]) for p in refs[nr:nr + npar]]
        dv = tuple(_f32(d[...]) for d in refs[nr + npar:nr + npar + nd])
        sv = tuple(x[...] for x in refs[nr + npar + nd:nr + npar + nd + ns])
        out_refs = refs[nr + npar + nd + ns:nr + npar + nd + ns + nr + npar]
        dst_refs = refs[nr + npar + nd + ns + nr + npar:]
        first = pl.program_id(0) == 0

        @pl.when(first)
        def _():
            for d in dst_refs:
                d[...] = jnp.zeros_like(d)

        def f(sts, *args):
            return fn(sts, *args)

        _, vjp = jax.vjp(f, sv, *rv, *pv)
        g = vjp((tuple(d[...] for d in dst_refs), dv))
        for d, v in zip(dst_refs, g[0], strict=True):
            d[...] = v
        for n_ in range(nr):
            out_refs[n_][...] = g[1 + n_].astype(out_refs[n_].dtype)
        for n_ in range(npar):
            ref = out_refs[nr + n_]

            @pl.when(first)
            def _(ref=ref):
                ref[...] = jnp.zeros_like(ref)

            ref[...] += g[1 + nr + n_]

    rev = lambda i: (nsteps - 1 - i, 0)
    res = pl.pallas_call(
        body, name=name, grid=(nsteps,),
        in_specs=[pl.BlockSpec((block, a.shape[1]), rev) for a in rows] + [_whole_spec(p) for p in params]
        + [pl.BlockSpec((block, d.shape[1]), rev) for d in douts] + [pl.BlockSpec(sh, rev) for sh in state_shapes],
        out_specs=[pl.BlockSpec((block, a.shape[1]), rev) for a in rows] + [_whole_spec(p) for p in params],
        out_shape=[jax.ShapeDtypeStruct(a.shape, grad_dtype) for a in rows]
        + [jax.ShapeDtypeStruct(p.shape, F32) for p in params],
        scratch_shapes=[pltpu.VMEM(sh, F32) for sh in state_shapes],
        compiler_params=_cparams(("arbitrary",)),
    )(*rows, *params, *douts, *saved)
    return list(res[:nr]), list(res[nr:])


def _norm_stage(x, g):
    return (_rms(x, g),)


def _tril():
    r = lax.broadcasted_iota(jnp.int32, (CHUNK, CHUNK), 0)
    c = lax.broadcasted_iota(jnp.int32, (CHUNK, CHUNK), 1)
    return r >= c


def _gla_chunk(st, q, k, v, la, b):
    tril = _tril()
    rowi = lax.broadcasted_iota(jnp.int32, (CHUNK, 1), 0)
    b_last = jnp.sum(la, axis=0, keepdims=True)
    b_ref = jnp.sum(jnp.where(rowi < CHUNK // 2, la, 0.0), axis=0, keepdims=True)
    att = mm.nt(q * jnp.exp(b - b_ref), k * jnp.exp(b_ref - b))
    att = jnp.where(tril, att, 0.0)
    o = mm.nn(att, v) + mm.nn(q * jnp.exp(b), st)
    decay = jnp.exp(jnp.broadcast_to(b_last, (LANE, LANE)).T)
    decay = jnp.concatenate([decay] * (v.shape[1] // LANE), axis=1)
    st2 = decay * st + mm.tn(k * jnp.exp(b_last - b), v)
    return st2, o


def _gla_step(heads, vp, scale):
    kp = LANE

    def fn(states, q, k, v, la):
        sts = list(states)
        trif = _tril().astype(F32)
        rows = []
        for c in range(q.shape[0] // CHUNK):
            r = slice(c * CHUNK, (c + 1) * CHUNK)
            b_all = hi.nn(trif, la[r])
            oh = []
            for h in range(heads):
                ks, vs = slice(h * kp, (h + 1) * kp), slice(h * vp, (h + 1) * vp)
                qh = q[r, ks] * scale if scale != 1.0 else q[r, ks]
                sts[h], o = _gla_chunk(sts[h], qh, k[r, ks], v[r, vs], la[r, ks], b_all[:, ks])
                oh.append(o)
            rows.append(jnp.concatenate(oh, axis=1))
        return tuple(sts), (jnp.concatenate(rows, axis=0),)

    return fn


def _ssd_step(states, xa, dtr, dtb, alog, dsk):
    sts = list(states)
    trif = _tril().astype(F32)
    wide = lax.broadcasted_iota(jnp.int32, (CHUNK, LANE), 0) >= lax.broadcasted_iota(jnp.int32, (CHUNK, LANE), 1)
    hg = SSM_HEADS // SSM_GROUPS
    xw = SSM_HEADS * LANE
    lane, head = lax.broadcasted_iota(jnp.int32, (LANE, xw), 1), lax.broadcasted_iota(jnp.int32, (LANE, xw), 0)
    spread = ((lane >= head * LANE) & (lane < (head + 1) * LANE)).astype(F32)
    neg_a = -jnp.exp(alog)
    pad = jnp.zeros((CHUNK, LANE), F32)
    rows = []
    for c in range(xa.shape[0] // CHUNK):
        r = slice(c * CHUNK, (c + 1) * CHUNK)
        dt_all = _softplus(hi.nn(dtr[r], spread) + dtb)
        a_all = dt_all * neg_a
        acs_all = hi.nn(trif, a_all)
        last_all = jnp.sum(a_all, axis=0, keepdims=True)
        yh = []
        for g in range(SSM_GROUPS):
            bm = xa[r, xw + g * LANE:xw + (g + 1) * LANE]
            cm = xa[r, xw + (SSM_GROUPS + g) * LANE:xw + (SSM_GROUPS + g + 1) * LANE]
            cb = mm.nt(cm, jnp.concatenate([bm, pad], axis=0))
            for hh in range(hg):
                h = g * hg + hh
                ls = slice(h * LANE, (h + 1) * LANE)
                xs, acs, acs_last = xa[r, ls], acs_all[:, ls], last_all[:, ls]
                xdt = xs * dt_all[:, ls]
                seg = acs - jnp.concatenate([acs, pad], axis=0).T[:CHUNK]
                lmat = jnp.exp(jnp.where(wide, seg, -1e30))
                y = (mm.nn(cb * lmat, jnp.concatenate([xdt, pad], axis=0)) + mm.nn(cm, sts[h]) * jnp.exp(acs)
                     + dsk[:, ls] * xs)
                sts[h] = jnp.exp(acs_last) * sts[h] + mm.tn(bm, xdt * jnp.exp(acs_last - acs))
                yh.append(y)
        rows.append(jnp.concatenate(yh, axis=1))
    return tuple(sts), (jnp.concatenate(rows, axis=0),)


def _gla_pre(glr, w2, bg):
    z = mm.nn(glr, w2) + bg
    return (-_softplus(-z) * (1.0 / GLA_GATE_NORM),)


def _gla_post(o, og, g):
    w = 2 * LANE
    return (jnp.concatenate([_rms(o[:, h * w:(h + 1) * w], g, GLA_DV) * _silu(og[:, h * w:(h + 1) * w])
                             for h in range(GLA_HEADS)], axis=1),)


def _hgrn_pre(q, f, lbnd):
    e = jnp.exp(lbnd - jnp.max(lbnd, axis=0, keepdims=True))
    rowi = lax.broadcasted_iota(jnp.int32, e.shape, 0)
    lb = jnp.sum(jnp.where(rowi >= 1, e, 0.0), axis=0, keepdims=True) / jnp.sum(e, axis=0, keepdims=True)
    fg = lb + (1.0 - lb) * _sigmoid(f)
    return _silu(q), 1.0 - fg, jnp.log(fg)


def _hgrn_post(o, og, g):
    return (jnp.concatenate([_rms(o[:, h * LANE:(h + 1) * LANE], g) for h in range(HGRN_HEADS)], axis=1)
            * _sigmoid(og),)


def _mamba_post(y, z, g):
    v = y * _silu(z)
    w = (SSM_HEADS // SSM_GROUPS) * LANE
    n_real = (SSM_HEADS // SSM_GROUPS) * SSM_HD
    return (jnp.concatenate([_rms(v[:, i * w:(i + 1) * w], g[:, i * w:(i + 1) * w], n_real)
                             for i in range(SSM_GROUPS)], axis=1),)


def _dil_pre(q, k, cosf, sinf, qg, kg):
    def groups(x, g):
        out = []
        for grp in range(len(DIL_GROUPS)):
            hs = []
            for h in range(grp * DIL_HEADS, (grp + 1) * DIL_HEADS):
                n = _rms(x[:, h * LANE:(h + 1) * LANE], g)
                hs.append(n * cosf + _swap_halves(n) * sinf)
            out.append(jnp.concatenate(hs, axis=1))
        return out

    return (*groups(q, qg), *groups(k, kg))


def _dil_merge(o0, o1, o2, l0, l1, l2):
    m = jnp.maximum(jnp.maximum(l0, l1), l2)
    e0, e1, e2 = jnp.exp(l0 - m), jnp.exp(l1 - m), jnp.exp(l2 - m)
    return ((e0 * o0 + e1 * o1 + e2 * o2) / (e0 + e1 + e2),)


def _dil_block(q, kp, kc, vp, vc, lim):
    kk = jnp.concatenate([kp, kc], axis=0)
    vv = jnp.concatenate([vp, vc], axis=0)
    s = mm.nt(q, kk) * (DIL_HD ** -0.5)
    i = lax.broadcasted_iota(jnp.int32, s.shape, 0)
    j = lax.broadcasted_iota(jnp.int32, s.shape, 1)
    dist = DIL_BLOCK + i - j
    s = jnp.where((dist >= 0) & (dist <= DIL_BLOCK) & (j >= lim), s, -1e30)
    m = jnp.max(s, axis=-1, keepdims=True)
    p = jnp.exp(s - m)
    l = jnp.sum(p, axis=-1, keepdims=True)
    return mm.nn(p / l, vv), jnp.broadcast_to(m + jnp.log(l), (q.shape[0], LANE))


def _xattn(xq, kv, qg, kg):
    w = XA_HEADS * LANE
    os_ = []
    for h in range(XA_HEADS):
        ls = slice(h * LANE, (h + 1) * LANE)
        q = _rms(xq[:, ls], qg, XA_HD)
        k = _rms(kv[:, ls], kg, XA_HD)
        s = mm.nt(q, k) * (XA_HD ** -0.5)
        p = jnp.exp(s - jnp.max(s, axis=-1, keepdims=True))
        p = p / jnp.sum(p, axis=-1, keepdims=True)
        os_.append(mm.nn(p, kv[:, w + h * LANE:w + (h + 1) * LANE]))
    return (jnp.concatenate(os_, axis=1),)


def _dil_geometry(s, w, r, g, v_cols):
    hb = DIL_HEADS if r == 1 else 1
    rb = DIL_BLOCK * r
    nb = s // rb
    bw = hb * LANE
    v_col0 = g * (w // bw)
    assert v_cols % bw == 0 and s % rb == 0
    return hb, rb, nb, bw, v_col0


def _sub(r, res):
    return pl.ds(res, DIL_BLOCK, stride=r) if r > 1 else slice(None)


def dil_attn(name, q, k, v, r, g):
    s, w = q.shape
    hb, rb, nb, bw, v_col0 = _dil_geometry(s, w, r, g, v.shape[1])

    def body(q_r, kp_r, kc_r, vp_r, vc_r, o_r, l_r):
        lim = jnp.where(pl.program_id(1) == 0, DIL_BLOCK, 0)
        for res in range(r):
            rows = _sub(r, res)
            for h in range(hb):
                ls = slice(h * LANE, (h + 1) * LANE)
                o, lse = _dil_block(q_r[rows, ls], kp_r[rows, ls], kc_r[rows, ls], vp_r[rows, ls], vc_r[rows, ls], lim)
                o_r[rows, ls] = o
                l_r[rows, ls] = lse

    cur = pl.BlockSpec((rb, bw), lambda hblk, n: (n, hblk))
    prev = pl.BlockSpec((rb, bw), lambda hblk, n: (jnp.maximum(n - 1, 0), hblk))
    vcur = pl.BlockSpec((rb, bw), lambda hblk, n: (n, v_col0 + hblk))
    vprev = pl.BlockSpec((rb, bw), lambda hblk, n: (jnp.maximum(n - 1, 0), v_col0 + hblk))
    return pl.pallas_call(
        body, name=name, grid=(w // bw, nb), in_specs=[cur, prev, cur, vprev, vcur], out_specs=[cur, cur],
        out_shape=[jax.ShapeDtypeStruct((s, w), F32)] * 2,
        compiler_params=_cparams(("parallel", "parallel")),
    )(q, k, k, v, v)


def dil_attn_bwd(name, q, k, v, do, dlse, r, g):
    s, w = q.shape
    hb, rb, nb, bw, v_col0 = _dil_geometry(s, w, r, g, v.shape[1])

    def body(q_r, kp_r, kc_r, vp_r, vc_r, do_r, dl_r, dq_r, dk_r, dv_r, ck, cv):
        i = pl.program_id(1)
        lim = jnp.where(i == nb - 1, DIL_BLOCK, 0)

        @pl.when(i == 0)
        def _():
            ck[...] = jnp.zeros_like(ck)
            cv[...] = jnp.zeros_like(cv)

        for res in range(r):
            rows = _sub(r, res)
            for h in range(hb):
                ls = slice(h * LANE, (h + 1) * LANE)
                _, vjp = jax.vjp(functools.partial(_dil_block, lim=lim),
                                 q_r[rows, ls], kp_r[rows, ls], kc_r[rows, ls], vp_r[rows, ls], vc_r[rows, ls])
                gq, gkp, gkc, gvp, gvc = vjp((do_r[rows, ls], dl_r[rows, ls]))
                dq_r[rows, ls] = gq
                dk_r[rows, ls] = gkc + ck[rows, ls]
                dv_r[rows, ls] = gvc + cv[rows, ls]
                ck[rows, ls] = gkp
                cv[rows, ls] = gvp

    cur = pl.BlockSpec((rb, bw), lambda hblk, i: (nb - 1 - i, hblk))
    prev = pl.BlockSpec((rb, bw), lambda hblk, i: (jnp.maximum(nb - 2 - i, 0), hblk))
    vcur = pl.BlockSpec((rb, bw), lambda hblk, i: (nb - 1 - i, v_col0 + hblk))
    vprev = pl.BlockSpec((rb, bw), lambda hblk, i: (jnp.maximum(nb - 2 - i, 0), v_col0 + hblk))
    return pl.pallas_call(
        body, name=name, grid=(w // bw, nb), in_specs=[cur, prev, cur, vprev, vcur, cur, cur],
        out_specs=[cur, cur, cur], out_shape=[jax.ShapeDtypeStruct((s, w), F32)] * 3,
        scratch_shapes=[pltpu.VMEM((rb, bw), F32)] * 2,
        compiler_params=_cparams(("parallel", "arbitrary")),
    )(q, k, k, v, v, do, dlse)


def _dsilu(u):
    sg = _sigmoid(u)
    return sg * (1.0 + u * (1.0 - sg))


def _ride(body, rider, n_in, n_out, n_scratch, grid):
    if rider is None:
        return body, [], [], [], [], [], None
    ni, no = len(rider.ins), len(rider.out_shapes)

    def wrapped(*refs):
        k_in, r_in = refs[:n_in], refs[n_in:n_in + ni]
        k_out, r_out = refs[n_in + ni:n_in + ni + n_out], refs[n_in + ni + n_out:n_in + ni + n_out + no]
        k_scr = refs[n_in + ni + n_out + no:n_in + ni + n_out + no + n_scratch]
        send_sems, recv_sems = refs[-2], refs[-1]
        first = functools.reduce(jnp.logical_and, [pl.program_id(a) == 0 for a in range(len(grid))])
        last = functools.reduce(jnp.logical_and, [pl.program_id(a) == g - 1 for a, g in enumerate(grid)])

        @pl.when(first)
        def _():
            rider.start(r_in, r_out, send_sems, recv_sems)

        body(*k_in, *k_out, *k_scr)

        @pl.when(last)
        def _():
            rider.finish(r_in, r_out, send_sems, recv_sems)

    sems = [pltpu.SemaphoreType.DMA((rider.n_sems,)), pltpu.SemaphoreType.DMA((rider.n_sems,))]
    return wrapped, rider.ins, [_ANY] * ni, [_ANY] * no, rider.out_shapes, sems, ("arbitrary",) * len(grid)


CONV_STRIP = 16


def _shifted_rows(prev8, cur_r, next8, lanes, s0, n, sh, block):
    if s0 - sh < 0:
        assert s0 == 0
        xp = jnp.concatenate([prev8, cur_r[0:n, lanes]], axis=0)
        return pltpu.roll(xp, sh, 0)[SUBLANE:SUBLANE + n]
    if s0 - sh + n > block:
        assert s0 == block and n == SUBLANE
        xp = jnp.concatenate([cur_r[block - SUBLANE:block, lanes], next8], axis=0)
        return (pltpu.roll(xp, sh, 0) if sh else xp)[SUBLANE:]
    return cur_r[pl.ds(s0 - sh, n), lanes]


def conv_fwd(name, x, w, b, mode, out_dtype, tc, block=ROW_BLOCK, rider=None):
    s, c = x.shape
    ntap = w.shape[0]
    block = min(block, s)
    f = c // 2 if mode == 'glu' else c
    nh = 2 if mode == 'glu' else 1
    off = f // tc

    def body(*refs):
        first = pl.program_id(1) == 0
        o_ref = refs[-1]

        def column(cidx, carry):
            lanes = pl.ds(pl.multiple_of(cidx * LANE, LANE), LANE)
            prevs = [jnp.where(first, 0.0, refs[4 * hlf][:, lanes]) for hlf in range(nh)]
            for s0 in range(0, block, CONV_STRIP):
                us = []
                for hlf in range(nh):
                    _, cur_r, w_r, b_r = refs[4 * hlf:4 * hlf + 4]
                    acc = b_r[:, lanes]
                    for j in range(ntap):
                        xs = _shifted_rows(prevs[hlf], cur_r, None, lanes, s0, CONV_STRIP, ntap - 1 - j, block)
                        acc = acc + w_r[j:j + 1, lanes] * xs
                    us.append(acc)
                res = _silu(us[0]) * us[1] if mode == 'glu' else _silu(us[0])
                o_ref[pl.ds(s0, CONV_STRIP), lanes] = res.astype(o_ref.dtype)
            return carry

        lax.fori_loop(0, tc // LANE, column, 0)

    rb = block // SUBLANE
    ins, specs = [], []
    for hlf in range(nh):
        o = hlf * off
        ins += [x, x, w, b]
        specs += [pl.BlockSpec((SUBLANE, tc), lambda j, i, o=o: (jnp.maximum(i * rb - 1, 0), j + o)),
                  pl.BlockSpec((block, tc), lambda j, i, o=o: (i, j + o)),
                  pl.BlockSpec((ntap, tc), lambda j, i, o=o: (0, j + o)),
                  pl.BlockSpec((1, tc), lambda j, i, o=o: (0, j + o))]
    grid = (f // tc, s // block)
    body, r_ins, r_in_specs, r_out_specs, r_out_shapes, r_scratch, sem = _ride(body, rider, len(ins), 1, 0, grid)
    res = pl.pallas_call(
        body, name=name, grid=grid, in_specs=specs + r_in_specs,
        out_specs=[pl.BlockSpec((block, tc), lambda j, i: (i, j))] + r_out_specs,
        out_shape=[jax.ShapeDtypeStruct((s, f), out_dtype)] + r_out_shapes, scratch_shapes=r_scratch,
        compiler_params=_cparams(sem or ("parallel", "parallel")),
    )(*ins, *r_ins)
    return res[0] if rider is None else (res[0], rider.results(res[1:]))


def conv_bwd(name, x, w, b, dout, mode, tc, block=ROW_BLOCK, rider=None):
    s, c = x.shape
    ntap = w.shape[0]
    block = min(block, s)
    nblk = s // block
    f = c // 2 if mode == 'glu' else c
    nh = 2 if mode == 'glu' else 1
    off = f // tc
    ext = block + SUBLANE

    def body(*refs):
        i = pl.program_id(1)
        first, last = i == 0, i == nblk - 1
        dcur_r, dnext_r = refs[5 * nh], refs[5 * nh + 1]
        outs = refs[5 * nh + 2:5 * nh + 2 + 3 * nh]
        du_scr = refs[5 * nh + 2 + 3 * nh:]

        @pl.when(first)
        def _():
            for hlf in range(nh):
                outs[3 * hlf + 1][...] = jnp.zeros_like(outs[3 * hlf + 1])
                outs[3 * hlf + 2][...] = jnp.zeros_like(outs[3 * hlf + 2])

        def column(cidx, carry):
            lanes = pl.ds(pl.multiple_of(cidx * LANE, LANE), LANE)
            prevs = [jnp.where(first, 0.0, refs[5 * hlf][:, lanes]) for hlf in range(nh)]
            nexts = [jnp.where(last, 0.0, refs[5 * hlf + 2][:, lanes]) for hlf in range(nh)]
            db_acc = [jnp.zeros((CONV_STRIP, LANE), F32) for _ in range(nh)]
            dw_acc = [[jnp.zeros((CONV_STRIP, LANE), F32) for _ in range(ntap)] for _ in range(nh)]
            for s0 in range(0, ext, CONV_STRIP):
                n = min(CONV_STRIP, ext - s0)
                d_e = dcur_r[pl.ds(s0, n), lanes] if s0 < block else jnp.where(last, 0.0, dnext_r[:, lanes])
                xs, us = [], []
                for hlf in range(nh):
                    cur_r, w_r, b_r = refs[5 * hlf + 1], refs[5 * hlf + 3], refs[5 * hlf + 4]
                    sh_rows = [_shifted_rows(prevs[hlf], cur_r, nexts[hlf], lanes, s0, n, ntap - 1 - j, block)
                               for j in range(ntap)]
                    acc = b_r[:, lanes]
                    for j in range(ntap):
                        acc = acc + w_r[j:j + 1, lanes] * sh_rows[j]
                    xs.append(sh_rows)
                    us.append(acc)
                dus = [d_e * us[1] * _dsilu(us[0]), d_e * _silu(us[0])] if mode == 'glu' else [d_e * _dsilu(us[0])]
                for hlf in range(nh):
                    du_scr[hlf][pl.ds(s0, n), lanes] = dus[hlf]
                    if s0 < block:
                        db_acc[hlf] = db_acc[hlf] + dus[hlf]
                        for j in range(ntap):
                            dw_acc[hlf][j] = dw_acc[hlf][j] + dus[hlf] * xs[hlf][j]
            for hlf in range(nh):
                w_r = refs[5 * hlf + 3]
                dx_r, dw_r, db_r = outs[3 * hlf:3 * hlf + 3]
                db_r[:, lanes] += jnp.sum(db_acc[hlf], axis=0, keepdims=True)
                for j in range(ntap):
                    dw_r[j:j + 1, lanes] += jnp.sum(dw_acc[hlf][j], axis=0, keepdims=True)
                for s0 in range(0, block, CONV_STRIP):
                    dx = None
                    for j in range(ntap):
                        term = w_r[j:j + 1, lanes] * du_scr[hlf][pl.ds(s0 + ntap - 1 - j, CONV_STRIP), lanes]
                        dx = term if dx is None else dx + term
                    dx_r[pl.ds(s0, CONV_STRIP), lanes] = dx.astype(dx_r.dtype)
            return carry

        lax.fori_loop(0, tc // LANE, column, 0)

    rb = block // SUBLANE
    nrow8 = s // SUBLANE
    ins, specs = [], []
    for hlf in range(nh):
        o = hlf * off
        ins += [x, x, x, w, b]
        specs += [pl.BlockSpec((SUBLANE, tc), lambda j, i, o=o: (jnp.maximum(i * rb - 1, 0), j + o)),
                  pl.BlockSpec((block, tc), lambda j, i, o=o: (i, j + o)),
                  pl.BlockSpec((SUBLANE, tc), lambda j, i, o=o: (jnp.minimum((i + 1) * rb, nrow8 - 1), j + o)),
                  pl.BlockSpec((ntap, tc), lambda j, i, o=o: (0, j + o)),
                  pl.BlockSpec((1, tc), lambda j, i, o=o: (0, j + o))]
    ins += [dout, dout]
    specs += [pl.BlockSpec((block, tc), lambda j, i: (i, j)),
              pl.BlockSpec((SUBLANE, tc), lambda j, i: (jnp.minimum((i + 1) * rb, nrow8 - 1), j))]
    out_specs, out_shape = [], []
    for hlf in range(nh):
        out_specs += [pl.BlockSpec((block, tc), lambda j, i: (i, j)), pl.BlockSpec((ntap, tc), lambda j, i: (0, j)),
                      pl.BlockSpec((1, tc), lambda j, i: (0, j))]
        out_shape += [jax.ShapeDtypeStruct((s, f), MXU_DTYPE), jax.ShapeDtypeStruct((ntap, f), F32),
                      jax.ShapeDtypeStruct((1, f), F32)]
    grid = (f // tc, nblk)
    body, r_ins, r_in_specs, r_out_specs, r_out_shapes, r_scratch, sem = _ride(body, rider, len(ins), 3 * nh, nh, grid)
    res = pl.pallas_call(
        body, name=name, grid=grid, in_specs=specs + r_in_specs, out_specs=out_specs + r_out_specs,
        out_shape=out_shape + r_out_shapes, scratch_shapes=[pltpu.VMEM((ext, tc), F32)] * nh + r_scratch,
        compiler_params=_cparams(sem or ("parallel", "arbitrary")),
    )(*ins, *r_ins)
    rode = None if rider is None else rider.results(res[3 * nh:])
    if nh == 1:
        return [res[0]], res[1], res[2], rode
    return ([res[0], res[3]], jnp.concatenate([res[1], res[4]], axis=1), jnp.concatenate([res[2], res[5]], axis=1),
            rode)


def loss_head(y, target, block=ROW_BLOCK):
    s, d = y.shape
    block = min(block, s)

    def body(y_r, t_r, acc_r, dy_r):
        e = y_r[...] - t_r[...]
        dy_r[...] = e * (1.0 / d)

        @pl.when(pl.program_id(0) == 0)
        def _():
            acc_r[...] = jnp.zeros_like(acc_r)

        acc_r[...] += jnp.sum((e * e).reshape(block // SUBLANE, SUBLANE, d), axis=0) * (0.5 / d)

    return pl.pallas_call(
        body, name="loss_head", grid=(s // block,),
        in_specs=[pl.BlockSpec((block, d), lambda i: (i, 0))] * 2,
        out_specs=[pl.BlockSpec((SUBLANE, d), lambda i: (0, 0)), pl.BlockSpec((block, d), lambda i: (i, 0))],
        out_shape=[jax.ShapeDtypeStruct((SUBLANE, d), F32), jax.ShapeDtypeStruct((s, d), F32)],
        compiler_params=_cparams(("arbitrary",)),
    )(y, target)


def adamw(name, w, g, m, v):
    r, c = w.shape
    tr = r if r <= 512 else _tile(r, (512, 256, 128, 64, 32, 16, 8))
    if c * tr * 4 > (1 << 21):
        tr = _tile(r, (256, 128, 64, 32, 16, 8))

    def body(w_r, g_r, m_r, v_r, d_r, nm_r, nv_r):
        gg = g_r[...]
        nm = ADAM_B1 * m_r[...] + (1.0 - ADAM_B1) * gg
        nv = ADAM_B2 * v_r[...] + (1.0 - ADAM_B2) * (gg * gg)
        m_hat = nm / (1.0 - ADAM_B1 ** ADAM_STEP)
        v_hat = nv / (1.0 - ADAM_B2 ** ADAM_STEP)
        d_r[...] = -ADAM_LR * (m_hat / (jnp.sqrt(v_hat) + ADAM_EPS) + ADAM_WD * w_r[...])
        nm_r[...] = nm
        nv_r[...] = nv

    spec = pl.BlockSpec((tr, c), lambda i: (i, 0))
    return pl.pallas_call(
        body, name=name, grid=(r // tr,), in_specs=[spec] * 4, out_specs=[spec] * 3,
        out_shape=[jax.ShapeDtypeStruct((r, c), F32)] * 3, compiler_params=_cparams(("parallel",)),
    )(w, g, m, v)


MESH = pl.DeviceIdType.MESH
_ANY = pl.BlockSpec(memory_space=pl.ANY)


def _place():
    return lax.axis_index("x"), lax.axis_index("y"), lax.axis_index("c")


class Packed:
    def __init__(self, shard_shape):
        self.r, self.c = shard_shape
        self.h = self.r // 2
        self.whole = (N_CHIPS, self.r, self.c)
        self.got = (N_CHIPS, self.h, self.c)
        self.slab_half = (self.h, self.c)

    def shard_half(self, ref, core):
        return ref.at[pl.ds(core * self.h, self.h)]

    def whole_half(self, ref, chip, core):
        return ref.at[chip, pl.ds(core * self.h, self.h)]

    def place(self, whole, shard, chip):
        return lax.dynamic_update_slice(whole, shard[None], (chip, 0, 0))

    def grad_half(self, ref, core):
        return ref.at[:, core]

    def pair_slab(self, ref, chip):
        return ref.at[chip]


class SlabCols:
    def __init__(self, shard_shape):
        self.r, self.c = shard_shape
        self.h = self.r // 2
        self.whole = (self.r, N_CHIPS * self.c)
        self.got = (self.h, N_CHIPS * self.c)
        self.slab_half = (self.h, self.c)

    def _cols(self, chip):
        return pl.ds(pl.multiple_of(chip * self.c, LANE), self.c)

    def shard_half(self, ref, core):
        return ref.at[pl.ds(core * self.h, self.h)]

    def whole_half(self, ref, chip, core):
        return ref.at[pl.ds(core * self.h, self.h), self._cols(chip)]

    def place(self, whole, shard, chip):
        return lax.dynamic_update_slice_in_dim(whole, shard, chip * self.c, 1)

    def grad_half(self, ref, core):
        return ref.at[pl.ds(core * self.h, self.h)]

    def pair_slab(self, ref, chip):
        return ref.at[:, self._cols(chip)]


class GatherRider:
    def __init__(self, shards, kinds):
        self.ins, self.kinds, n = list(shards), kinds, len(shards)
        self.out_shapes = [jax.ShapeDtypeStruct(k.whole, s.dtype) for k, s in zip(kinds, shards)]
        self.n_sems = 6 * n

    def _copies(self, w_refs, out_refs, send_sems, recv_sems):
        x, y, cc = _place()
        chips = [(1 - x, y), (x, 1 - y), (1 - x, 1 - y)]

        def copy(t, k, chip, core, to, src=None):
            dst = self.kinds[t].whole_half(out_refs[t], 2 * chip[0] + chip[1], core)
            return pltpu.make_async_remote_copy(
                src_ref=dst if src is None else src, dst_ref=dst, send_sem=send_sems.at[6 * t + k],
                recv_sem=recv_sems.at[6 * t + k], device_id=to, device_id_type=MESH)

        first = [copy(t, j, (x, y), cc, (*chip, cc), src=self.kinds[t].shard_half(w_refs[t], cc))
                 for t in range(len(self.ins)) for j, chip in enumerate(chips)]
        return copy, first, chips, (x, y, cc)

    def start(self, w_refs, out_refs, send_sems, recv_sems):
        for cp in self._copies(w_refs, out_refs, send_sems, recv_sems)[1]:
            cp.start()

    def finish(self, w_refs, out_refs, send_sems, recv_sems):
        copy, first, chips, (x, y, cc) = self._copies(w_refs, out_refs, send_sems, recv_sems)
        passed = []
        for t in range(len(self.ins)):
            for j, chip in enumerate(chips):
                copy(t, j, chip, cc, (x, y, cc)).wait_recv()
                passed.append(copy(t, 3 + j, chip, cc, (x, y, 1 - cc)))
                passed[-1].start()
        for t in range(len(self.ins)):
            for j, chip in enumerate(chips):
                copy(t, 3 + j, chip, 1 - cc, (x, y, cc)).wait_recv()
        for cp in first + passed:
            cp.wait_send()

    def results(self, outs):
        chip = 2 * lax.axis_index("x") + lax.axis_index("y")
        return [k.place(o, s, chip) for k, o, s in zip(self.kinds, outs, self.ins)]


class ExchangeRider:
    def __init__(self, pairs, kinds):
        self.ins, self.kinds = list(pairs), kinds
        self.out_shapes = [jax.ShapeDtypeStruct((N_CHIPS,) + k.slab_half, p.dtype) for k, p in zip(kinds, pairs)]
        self.n_sems = 3 * len(pairs)

    def start(self, p_refs, out_refs, send_sems, recv_sems):
        x, y, cc = _place()
        for t in range(len(self.ins)):
            for j, chip in enumerate([(1 - x, y), (x, 1 - y), (1 - x, 1 - y)]):
                pltpu.make_async_remote_copy(
                    src_ref=self.kinds[t].pair_slab(p_refs[t], 2 * chip[0] + chip[1]), dst_ref=out_refs[t].at[2 * x + y],
                    send_sem=send_sems.at[3 * t + j], recv_sem=recv_sems.at[3 * t + j], device_id=(*chip, cc),
                    device_id_type=MESH).start()

    def finish(self, p_refs, out_refs, send_sems, recv_sems):
        x, y, cc = _place()
        me = 2 * x + y
        for t in range(len(self.ins)):
            for j, chip in enumerate([(1 - x, y), (x, 1 - y), (1 - x, 1 - y)]):
                them = 2 * chip[0] + chip[1]
                pltpu.make_async_remote_copy(
                    src_ref=self.kinds[t].pair_slab(p_refs[t], them), dst_ref=out_refs[t].at[them],
                    send_sem=send_sems.at[3 * t + j], recv_sem=recv_sems.at[3 * t + j], device_id=(x, y, cc),
                    device_id_type=MESH).wait()

    def results(self, outs):
        return list(outs)


class SwapRider:
    def __init__(self, gs, kinds):
        self.ins, self.kinds = list(gs), kinds
        self.out_shapes = [jax.ShapeDtypeStruct(k.got, g.dtype) for k, g in zip(kinds, gs)]
        self.n_sems = len(gs)

    def _copies(self, g_refs, out_refs, send_sems, recv_sems):
        x, y, cc = _place()
        return [pltpu.make_async_remote_copy(
            src_ref=self.kinds[t].grad_half(g_refs[t], 1 - cc), dst_ref=out_refs[t], send_sem=send_sems.at[t],
            recv_sem=recv_sems.at[t], device_id=(x, y, 1 - cc), device_id_type=MESH) for t in range(len(self.ins))]

    def start(self, *refs):
        for cp in self._copies(*refs):
            cp.start()

    def finish(self, *refs):
        for cp in self._copies(*refs):
            cp.wait()

    def results(self, outs):
        return list(outs)


def run_rider(rider, name):
    n, no = len(rider.ins), len(rider.out_shapes)

    def body(*refs):
        parts = (refs[:n], refs[n:n + no], refs[n + no], refs[n + no + 1])
        rider.start(*parts)
        rider.finish(*parts)

    outs = pl.pallas_call(
        body, name=name, in_specs=[_ANY] * n, out_specs=[_ANY] * no, out_shape=rider.out_shapes,
        scratch_shapes=[pltpu.SemaphoreType.DMA((rider.n_sems,)), pltpu.SemaphoreType.DMA((rider.n_sems,))],
    )(*rider.ins)
    return rider.results(outs)


def allgather_devices(buf):
    r, c = buf.shape

    def body(b_ref, out_ref, send_sems, recv_sems, local_sem):
        x, y, cc = _place()
        me = 4 * x + 2 * y + cc
        mine = pltpu.make_async_copy(b_ref, out_ref.at[me], local_sem)
        mine.start()
        copies = []
        for k in range(1, N_DEV):
            px, py, pc = x ^ (k >> 2), y ^ ((k >> 1) & 1), cc ^ (k & 1)
            cp = pltpu.make_async_remote_copy(src_ref=b_ref, dst_ref=out_ref.at[me], send_sem=send_sems.at[k - 1],
                                              recv_sem=recv_sems.at[k - 1], device_id=(px, py, pc), device_id_type=MESH)
            cp.start()
            copies.append((cp, 4 * px + 2 * py + pc))
        for k, (cp, peer) in enumerate(copies):
            pltpu.make_async_remote_copy(src_ref=b_ref, dst_ref=out_ref.at[peer], send_sem=send_sems.at[k],
                                         recv_sem=recv_sems.at[k], device_id=(x, y, cc), device_id_type=MESH).wait_recv()
        for cp, _ in copies:
            cp.wait_send()
        mine.wait()

    return pl.pallas_call(
        body, name="allgather_devices", in_specs=[_ANY], out_specs=_ANY,
        out_shape=jax.ShapeDtypeStruct((N_DEV, r, c), buf.dtype),
        scratch_shapes=[pltpu.SemaphoreType.DMA((N_DEV - 1,)), pltpu.SemaphoreType.DMA((N_DEV - 1,)),
                        pltpu.SemaphoreType.DMA],
    )(buf)


def swap_halves_sibling(gs, kinds, name):
    n = len(gs)

    def body(*refs):
        g_refs, out_refs, send_sems, recv_sems = refs[:n], refs[n:2 * n], refs[2 * n], refs[2 * n + 1]
        x, y, cc = _place()
        cps = []
        for t in range(n):
            cps.append(pltpu.make_async_remote_copy(
                src_ref=kinds[t].grad_half(g_refs[t], 1 - cc), dst_ref=out_refs[t], send_sem=send_sems.at[t],
                recv_sem=recv_sems.at[t], device_id=(x, y, 1 - cc), device_id_type=MESH))
            cps[-1].start()
        for cp in cps:
            cp.wait()

    return pl.pallas_call(
        body, name=name, in_specs=[_ANY] * n, out_specs=[_ANY] * n,
        out_shape=[jax.ShapeDtypeStruct(k.got, g.dtype) for k, g in zip(kinds, gs)],
        scratch_shapes=[pltpu.SemaphoreType.DMA((n,)), pltpu.SemaphoreType.DMA((n,))],
    )(*gs)


def _row_tile(n, limit=512):
    return max(t for t in range(16, limit + 1, 16) if n % t == 0)


def sum_chips(got, own, kind, chip, name):
    def body(chip_ref, got_r, own_r, out_r):
        mine = own_r[...].astype(F32)
        acc = None
        for k in range(N_CHIPS):
            term = jnp.where(chip_ref[0] == k, mine, got_r[k].astype(F32))
            acc = term if acc is None else acc + term
        out_r[...] = acc

    if isinstance(kind, Packed):
        r, c = kind.slab_half
        tr = _row_tile(r)
        grid = (r // tr,)
        specs = [pl.BlockSpec((N_CHIPS, tr, c), lambda i, chip_ref: (0, i, 0)),
                 pl.BlockSpec((None, tr, c), lambda i, chip_ref: (chip_ref[0], i, 0))]
        out_spec = pl.BlockSpec((tr, c), lambda i, chip_ref: (i, 0))
    else:
        r, c = kind.slab_half
        tr = _row_tile(r, 256)
        grid = (r // tr,)
        specs = [pl.BlockSpec((N_CHIPS, tr, c), lambda i, chip_ref: (0, i, 0)),
                 pl.BlockSpec((tr, c), lambda i, chip_ref: (i, chip_ref[0]))]
        out_spec = pl.BlockSpec((tr, c), lambda i, chip_ref: (i, 0))
    return pl.pallas_call(
        body, name=name,
        grid_spec=pltpu.PrefetchScalarGridSpec(num_scalar_prefetch=1, grid=grid, in_specs=specs, out_specs=out_spec),
        out_shape=jax.ShapeDtypeStruct(kind.slab_half, F32),
        compiler_params=_cparams(("parallel",) * len(grid)),
    )(chip, got, own)


def join_halves_sibling(halves):
    n = len(halves)

    def body(*refs):
        h_refs, out_refs, send_sems, recv_sems = refs[:n], refs[n:2 * n], refs[2 * n], refs[2 * n + 1]
        x, y, cc = _place()
        cps = []
        for t in range(n):
            cps.append(pltpu.make_async_remote_copy(
                src_ref=h_refs[t], dst_ref=out_refs[t].at[cc], send_sem=send_sems.at[t], recv_sem=recv_sems.at[t],
                device_id=(x, y, 1 - cc), device_id_type=MESH))
            cps[-1].start()
        for t in range(n):
            pltpu.make_async_remote_copy(
                src_ref=h_refs[t], dst_ref=out_refs[t].at[1 - cc], send_sem=send_sems.at[t], recv_sem=recv_sems.at[t],
                device_id=(x, y, cc), device_id_type=MESH).wait_recv()
        for cp in cps:
            cp.wait_send()

    outs = pl.pallas_call(
        body, name="join_halves_sibling", in_specs=[_ANY] * n, out_specs=[_ANY] * n,
        out_shape=[jax.ShapeDtypeStruct((2,) + h.shape, h.dtype) for h in halves],
        scratch_shapes=[pltpu.SemaphoreType.DMA((n,)), pltpu.SemaphoreType.DMA((n,))],
    )(*halves)
    core = lax.axis_index("c")
    return [lax.dynamic_update_slice_in_dim(o, h[None], core, 0) for o, h in zip(outs, halves)]


def add_own_half(g, got, kind, core, out_dtype, name):
    def body(c_ref, g_r, o_r, out_r):
        out_r[...] = (g_r[...] + o_r[...]).astype(out_r.dtype)

    if isinstance(kind, Packed):
        r, c = kind.slab_half
        tr = _row_tile(r)
        grid = (N_CHIPS, r // tr)
        specs = [pl.BlockSpec((None, None, tr, c), lambda i, j, c_ref: (i, c_ref[0], j, 0)),
                 pl.BlockSpec((None, tr, c), lambda i, j, c_ref: (i, j, 0))]
        out_spec = pl.BlockSpec((None, tr, c), lambda i, j, c_ref: (i, j, 0))
    else:
        h, c4 = kind.got
        tr = _row_tile(h, 128)
        grid = (1, h // tr)
        specs = [pl.BlockSpec((tr, c4), lambda i, j, c_ref: (c_ref[0] * (h // tr) + j, 0)),
                 pl.BlockSpec((tr, c4), lambda i, j, c_ref: (j, 0))]
        out_spec = pl.BlockSpec((tr, c4), lambda i, j, c_ref: (j, 0))
    return pl.pallas_call(
        body, name=name,
        grid_spec=pltpu.PrefetchScalarGridSpec(num_scalar_prefetch=1, grid=grid, in_specs=specs, out_specs=out_spec),
        out_shape=jax.ShapeDtypeStruct(kind.got, out_dtype),
        compiler_params=_cparams(("parallel", "parallel")),
    )(core, g, got)


def sum_slabs(p, name):
    n, r, c = p.shape
    tr = _tile(r, [t for t in (512, 256, 128, 64, 32, 16) if n * t * c * p.dtype.itemsize <= (1 << 23)])

    def body(p_r, out_r):
        acc = p_r[0].astype(F32)
        for k in range(1, n):
            acc = acc + p_r[k].astype(F32)
        out_r[...] = acc

    return pl.pallas_call(
        body, name=name, grid=(r // tr,), in_specs=[pl.BlockSpec((n, tr, c), lambda i: (0, i, 0))],
        out_specs=pl.BlockSpec((tr, c), lambda i: (i, 0)), out_shape=jax.ShapeDtypeStruct((r, c), F32),
        compiler_params=_cparams(("parallel",)),
    )(p)


def _lay(arr, axis, pieces, total, reps=()):
    items = [(d, n, lax.slice_in_dim(arr, s0, s0 + n, axis=axis)) for s0, n, d in pieces]
    items += [(d, n, jnp.repeat(lax.slice_in_dim(arr, s0, s0 + 1, axis=axis), n, axis=axis)) for s0, d, n in reps]
    items.sort(key=lambda t: t[0])
    parts, pos = [], 0

    def zeros(n):
        sh = list(arr.shape)
        sh[axis] = n
        return jnp.zeros(sh, arr.dtype)

    for d, n, v in items:
        if d > pos:
            parts.append(zeros(d - pos))
        parts.append(v)
        pos = d + n
    if total > pos:
        parts.append(zeros(total - pos))
    return jnp.concatenate(parts, axis=axis) if len(parts) > 1 else parts[0]


def _unlay_parts(g, axis, pieces, reps=()):
    out = [(s0, lax.slice_in_dim(g, d, d + n, axis=axis)) for s0, n, d in pieces]
    out += [(s0, jnp.sum(lax.slice_in_dim(g, d, d + n, axis=axis), axis=axis, keepdims=True)) for s0, d, n in reps]
    return out


def _join(parts, axis):
    parts = sorted(parts, key=lambda t: t[0])
    return jnp.concatenate([p for _, p in parts], axis=axis)


def _heads(src0, n_heads, width, padded, dst0=0):
    return [(src0 + h * width, width, dst0 + h * padded) for h in range(n_heads)]


_XQ = lambda src0: _heads(src0, XA_HEADS, XA_HD, LANE)
_XA_W = XA_HEADS * LANE

LAYOUT = {
    'a': dict(
        segs=dict(q=(_heads(0, 4, 96, LANE), 512, ()), k=(_heads(384, 4, 96, LANE), 512, ()),
                  v=(_heads(768, 4, 192, 256), 1024, ()), glr=([(1536, 16, 0)], LANE, ()),
                  og=(_heads(1552, 4, 192, 256), 1024, ()), xq=(_XQ(2320), _XA_W, ())),
        tok=(_heads(0, 4, 192, 256), 1024), xa=(_XQ(768), _XA_W)),
    'b': dict(
        segs=dict(q=([(0, 1536, 0)], 1536, ()), k=([(1536, 1536, 0)], 1536, ()), v=([(3072, 1536, 0)], 1536, ()),
                  xq=(_XQ(4608), _XA_W, ())),
        tok=([(0, 512, 0)], 512), xa=(_XQ(512), _XA_W)),
    'c': dict(
        segs=dict(z=(_heads(0, 12, 64, LANE), 1536, ()),
                  xbc=(_heads(768, 12, 64, LANE) + [(1536, 256, 1536), (1792, 256, 1792)], 2048, ()),
                  dt=([(2048, 12, 0)], LANE, ()),
                  xq=(_XQ(2060), _XA_W, ())),
        tok=(_heads(0, 12, 64, LANE), 1536), xa=(_XQ(768), _XA_W)),
    'd': dict(
        segs=dict(q=([(0, 768, 0)], 768, ()), f=([(768, 768, 0)], 768, ()), i=([(1536, 768, 0)], 768, ()),
                  og=([(2304, 768, 0)], 768, ()), xq=(_XQ(3072), _XA_W, ())),
        tok=([(0, 768, 0)], 768), xa=(_XQ(768), _XA_W)),
}
KINDS = 'abcd'
_XS_PIECES = _heads(0, 12, 64, LANE)
_XBC_PIECES = _XS_PIECES + [(768, 256, 1536), (1024, 256, 1792)]
_HEAD_REPS = tuple((h, h * LANE, LANE) for h in range(12))


def _row(v):
    return v.reshape(1, -1)


LAYER_WEIGHTS = [
    {'w_in': (f'{k}_w_in', None), 'w_out': (f'{k}_w_out', None), 'w_kv': ('xa_w_kv', i), 'w_up': ('ffn_w_up', i),
     'w_down': ('ffn_w_down', i), **({'w_gate2': ('a_w_gate2', None)} if k == 'a' else {})}
    for i, k in enumerate('abcd')]


class LocalLayers:
    def __init__(self, W):
        self.W, self.g = W, {}

    def weights(self, i):
        return {key: (self.W[n] if l is None else self.W[n][l]).astype(MXU_DTYPE)
                for key, (n, l) in LAYER_WEIGHTS[i].items()}

    def fwd_rider(self, i):
        return None

    def bwd_rider(self, i):
        return None

    def grads_rider(self, i, g):
        self.g[i] = g
        return None

    def whole_grads(self):
        out = {}
        for i in range(4):
            for key, (n, l) in LAYER_WEIGHTS[i].items():
                if l is None:
                    out[n] = self.g[i][key]
        for n in ('xa_w_kv', 'ffn_w_up', 'ffn_w_down'):
            key = [k for k, (m, _) in LAYER_WEIGHTS[0].items() if m == n][0]
            out[n] = jnp.stack([self.g[i][key] for i in range(4)])
        return out


class ShardedLayers:
    def __init__(self, w, core_id):
        self.core_id = core_id
        self.names, self.axes, self.shards, self.packed, self.kinds = [], [], [], [], []
        for lw in LAYER_WEIGHTS:
            keys = [k for k in lw if k not in ('w_up', 'w_down')]
            sh = {k: (w[lw[k][0]] if lw[k][1] is None else w[lw[k][0]][lw[k][1]]).astype(MXU_DTYPE) for k in lw}
            ax = {k: SHARD_AXIS[lw[k][0]] - (lw[k][1] is not None) for k in lw}
            pk = _pack([sh[k] for k in keys], MXU_DTYPE, 256)
            self.names.append(keys)
            self.axes.append(ax)
            self.shards.append(sh)
            self.packed.append(pk)
            self.kinds.append([Packed(pk.shape), SlabCols(sh['w_up'].shape), Packed(sh['w_down'].shape)])
        self.whole = {}
        self.pending = None
        self.recvd = {}

    def _operands(self, i):
        return [self.packed[i], self.shards[i]['w_up'], self.shards[i]['w_down']]

    def _gathered(self, i, res):
        per_chip = [_unpack(res[0][j], [self.shards[i][k].shape for k in self.names[i]]) for j in range(N_CHIPS)]
        out = {k: _merge_chips(jnp.stack([per_chip[j][n] for j in range(N_CHIPS)]), self.axes[i][k])
               for n, k in enumerate(self.names[i])}
        out['w_up'], out['w_down'] = res[1], res[2].reshape(-1, res[2].shape[-1])
        self.whole[i] = out

    def first_gather(self):
        self._gathered(0, run_rider(GatherRider(self._operands(0), self.kinds[0]), "allgather_chips"))

    def weights(self, i):
        return self.whole[i]

    def fwd_rider(self, i):
        return GatherRider(self._operands(i + 1), self.kinds[i + 1]) if i + 1 < 4 else None

    def fwd_rode(self, i, res):
        self._gathered(i + 1, res)

    def bwd_rider(self, i):
        return ExchangeRider(self.pending[1], self.kinds[self.pending[0]]) if self.pending is not None else None

    def bwd_rode(self, i, res):
        self.recvd[self.pending[0]] = (res, self.pending[1])
        self.pending = None

    def grads_rider(self, i, g):
        kinds = self.kinds[i]
        gb = jnp.stack([_pack([_split_chips(g[k], self.axes[i][k])[j] for k in self.names[i]], F32, 256)
                        for j in range(N_CHIPS)])
        self.swapping = [gb.reshape(N_CHIPS, 2, kinds[0].h, kinds[0].c), g['w_up'],
                         g['w_down'].reshape(N_CHIPS, 2, kinds[2].h, kinds[2].c)]
        return SwapRider(self.swapping, kinds)

    def grads_rode(self, i, gots):
        self.pending = (i, [add_own_half(a, o, k, self.core_id, GRAD_WIRE_DTYPE, f"add_own_half_{i}_{t}")
                            for t, (a, o, k) in enumerate(zip(self.swapping, gots, self.kinds[i]))])

    def finish(self, chip_id):
        last, pairs = self.pending
        self.recvd[last] = (run_rider(ExchangeRider(pairs, self.kinds[last]), "exchange_chips"), pairs)
        halves = []
        for i in range(4):
            got, pairs = self.recvd[i]
            halves += [sum_chips(r, p, k, chip_id, f"sum_chips_{i}_{t}")
                       for t, (r, p, k) in enumerate(zip(got, pairs, self.kinds[i]))]
        joined = join_halves_sibling(halves)
        out, stacked = {}, {'xa_w_kv': [], 'ffn_w_up': [], 'ffn_w_down': []}
        for i, lw in enumerate(LAYER_WEIGHTS):
            red, up, down = joined[3 * i:3 * i + 3]
            parts = _unpack(red.reshape(-1, PACK_COLS), [self.shards[i][k].shape for k in self.names[i]])
            parts = dict(zip(self.names[i], parts), w_up=up.reshape(self.shards[i]['w_up'].shape),
                         w_down=down.reshape(self.shards[i]['w_down'].shape))
            for k, (n, l) in lw.items():
                if l is None:
                    out[n] = parts[k]
                else:
                    stacked[n].append(parts[k])
        out.update({n: jnp.stack(v) for n, v in stacked.items()})
        return out


def local_step(x, mem, positions, target, W, layers=None):
    s = x.shape[0]
    grads = {}
    scan_block = CHUNK * SCAN_CHUNKS
    ffn = layers or LocalLayers(W)

    inv_freq = ROPE_THETA ** (-jnp.arange(DIL_HD // 2, dtype=F32) / (DIL_HD // 2))
    ang = positions.astype(F32)[:, None] * inv_freq
    cosf = jnp.concatenate([jnp.cos(ang), jnp.cos(ang)], axis=-1)
    sinf = jnp.concatenate([-jnp.sin(ang), jnp.sin(ang)], axis=-1)

    mem_g = _row(W['mem_norm'])
    (mem_n,) = tmap("mem_norm", _norm_stage, [mem], [mem_g], [(D_MODEL, MXU_DTYPE)])
    kv_lay = _heads(0, 4, 64, LANE) + _heads(256, 4, 64, LANE, dst0=_XA_W)

    saved = []
    for i in range(4):
        kind = KINDS[i]
        lay = LAYOUT[kind]
        sv = dict(x0=x)
        wl = ffn.weights(i)
        w_in, w_out = wl['w_in'], wl['w_out']
        sv['w_seg'] = {n: _lay(w_in, 1, p, t, r).astype(MXU_DTYPE) for n, (p, t, r) in lay['segs'].items()}
        sv['wo_tok'] = _lay(w_out, 0, *lay['tok']).astype(MXU_DTYPE)
        sv['wo_xa'] = _lay(w_out, 0, *lay['xa']).astype(MXU_DTYPE)
        sv['w_kv'] = _lay(wl['w_kv'], 1, kv_lay, 2 * _XA_W).astype(MXU_DTYPE)
        sv['g1'] = _row(W['mix_norm'][i])
        (h,) = tmap(f"mix_norm_{i}", _norm_stage, [x], [sv['g1']], [(D_MODEL, MXU_DTYPE)])
        sv['h'] = h
        seg = dict(zip(sv['w_seg'], matmul_multi(h, list(sv['w_seg'].values()))))
        sv['seg'] = seg

        if kind == 'a':
            sv['w2'] = _lay(_lay(wl['w_gate2'], 1, _heads(0, 4, 96, LANE), 512), 0, [(0, 16, 0)], LANE)
            sv['bg'] = _row(_lay(W['a_b_gate'], 0, _heads(0, 4, 96, LANE), 512))
            sv['on'] = _row(_lay(W['a_o_norm'], 0, [(0, 192, 0)], 256))
            (la,) = tmap("gla_pre", _gla_pre, [seg['glr']], [sv['w2'], sv['bg']], [(512, F32)])
            sv['la'] = la
            sv['scan_fn'] = _gla_step(GLA_HEADS, 2 * LANE, GLA_DK ** -0.5)
            sv['scan_rows'] = [seg['q'], seg['k'], seg['v'], la]
            (o,), sv['states'] = rscan("gla_scan", sv['scan_fn'], [(LANE, 2 * LANE)] * GLA_HEADS, sv['scan_rows'], [],
                                       [(1024, F32)], scan_block)
            sv['o'] = o
            (tok,) = tmap("gla_post", _gla_post, [o, seg['og']], [sv['on']], [(1024, MXU_DTYPE)])
        elif kind == 'b':
            sv['qg'], sv['kg'] = _row(W['b_q_norm']), _row(W['b_k_norm'])
            os_, ls_ = [], []
            qkn = tmap("dil_pre", _dil_pre, [seg['q'], seg['k'], cosf, sinf], [sv['qg'], sv['kg']], [(512, F32)] * 6)
            sv['qn'], sv['kn'] = qkn[:3], qkn[3:]
            for g, (window, r) in enumerate(DIL_GROUPS):
                assert window // r == DIL_BLOCK and (s // r) % DIL_BLOCK == 0
                o, lse = dil_attn(f"dil_attn_{g}", sv['qn'][g], sv['kn'][g], seg['v'], r, g)
                os_.append(o)
                ls_.append(lse)
            sv['os'], sv['ls'] = os_, ls_
            (tok,) = tmap("dil_merge", _dil_merge, os_ + ls_, [], [(512, MXU_DTYPE)])
        elif kind == 'c':
            sv['cw'] = _lay(W['c_conv_w'], 1, _XBC_PIECES, 2048)
            sv['cb'] = _row(_lay(W['c_conv_b'], 0, _XBC_PIECES, 2048))
            sv['dtb'] = _row(_lay(W['c_dt_bias'], 0, [], 1536, _HEAD_REPS))
            sv['alog'] = _row(_lay(W['c_a_log'], 0, [], 1536, _HEAD_REPS))
            sv['dsk'] = _row(_lay(W['c_d'], 0, [], 1536, _HEAD_REPS))
            sv['cn'] = _row(_lay(W['c_norm'], 0, _XS_PIECES, 1536))
            xact = conv_fwd("ssm_conv", seg['xbc'], sv['cw'], sv['cb'], 'silu', F32, 512)
            sv['xact'] = xact
            sv['scan_rows'] = [xact, seg['dt']]
            sv['scan_params'] = [sv['dtb'], sv['alog'], sv['dsk']]
            (yv,), sv['states'] = rscan("ssd_scan", _ssd_step, [(LANE, LANE)] * SSM_HEADS, sv['scan_rows'],
                                        sv['scan_params'], [(1536, F32)], scan_block)
            sv['y'] = yv
            (tok,) = tmap("ssd_post", _mamba_post, [yv, seg['z']], [sv['cn']], [(1536, MXU_DTYPE)])
        else:
            sv['lbnd'] = W['d_lower_bounds']
            sv['on'] = _row(W['d_o_norm'])
            qq, kk, la = tmap("hgrn_pre", _hgrn_pre, [seg['q'], seg['f']], [sv['lbnd']], [(768, F32)] * 3)
            sv['scan_fn'] = _gla_step(HGRN_HEADS, LANE, 1.0)
            sv['scan_rows'] = [qq, kk, seg['i'], la]
            (o,), sv['states'] = rscan("hgrn_scan", sv['scan_fn'], [(LANE, LANE)] * HGRN_HEADS, sv['scan_rows'], [],
                                       [(768, F32)], scan_block)
            sv['o'] = o
            (tok,) = tmap("hgrn_post", _hgrn_post, [o, seg['og']], [sv['on']], [(768, MXU_DTYPE)])
        sv['tok'] = tok

        kv = matmul(mem_n, sv['w_kv'])
        sv['kv'] = kv
        sv['xqg'] = _row(_lay(W['xa_q_norm'][i], 0, [(0, 64, 0)], LANE))
        sv['xkg'] = _row(_lay(W['xa_k_norm'][i], 0, [(0, 64, 0)], LANE))
        (xa,) = tmap(f"xattn_{i}", _xattn, [seg['xq']], [kv, sv['xqg'], sv['xkg']], [(_XA_W, MXU_DTYPE)])
        sv['xa'] = xa
        x = matmul_sum([tok, xa], [sv['wo_tok'], sv['wo_xa']], False, add=x)
        sv['x1'] = x

        sv['g2'] = _row(W['ffn_norm'][i])
        sv['fcw'] = W['ffn_conv_w'][i]
        sv['fcb'] = _row(W['ffn_conv_b'][i])
        (h2,) = tmap(f"ffn_norm_{i}", _norm_stage, [x], [sv['g2']], [(D_MODEL, MXU_DTYPE)])
        sv['h2'] = h2
        w_up, w_down = wl['w_up'], wl['w_down']
        sv['w_up'], sv['w_down'] = w_up, w_down
        u0 = matmul(h2, w_up)
        sv['u0'] = u0
        rider = ffn.fwd_rider(i)
        act = conv_fwd("ffn_conv", u0, sv['fcw'], sv['fcb'], 'glu', MXU_DTYPE, 1408, rider=rider)
        if rider is not None:
            act, rode = act
            ffn.fwd_rode(i, rode)
        sv['act'] = act
        x = matmul(act, w_down, add=x)
        saved.append(sv)

    loss_acc, dx = loss_head(x, target)

    g_stack = {n: [None] * 4 for n in ('mix_norm', 'xa_q_norm', 'xa_k_norm', 'ffn_norm', 'ffn_conv_w', 'ffn_conv_b')}
    d_memn = None
    for i in reversed(range(4)):
        kind = KINDS[i]
        lay = LAYOUT[kind]
        sv = saved[i]
        seg = sv['seg']
        w_up, w_down = sv['w_up'], sv['w_down']
        gl = {}
        dact = matmul(dx, w_down, tb=True)
        gl['w_down'] = matmul(sv['act'], dx, ta=True)
        rider = ffn.bwd_rider(i)
        (du_g, du_v), dcw, dcb, rode = conv_bwd("ffn_conv_bwd", sv['u0'], sv['fcw'], sv['fcb'], dact, 'glu', 1408,
                                                rider=rider)
        if rider is not None:
            ffn.bwd_rode(i, rode)
        g_stack['ffn_conv_w'][i], g_stack['ffn_conv_b'][i] = dcw, dcb[0]
        dh2 = matmul_sum([du_g, du_v], [w_up, w_up], True, b_cols=[0, 1])
        g_up = jnp.zeros((1,) + w_up.shape, F32)
        g_up = matmul(sv['h2'], du_g, ta=True, into=(g_up, 0, 0))
        g_up = matmul(sv['h2'], du_v, ta=True, into=(g_up, 0, D_FF))
        gl['w_up'] = g_up[0]
        (dx,), (dg2,) = tmap_bwd(f"ffn_norm_bwd_{i}", _norm_stage, [sv['x1']], [sv['g2']], [dh2], [True], {0: dx})
        g_stack['ffn_norm'][i] = dg2[0]
        dtok = matmul(dx, sv['wo_tok'], tb=True)
        dxa = matmul(dx, sv['wo_xa'], tb=True)
        g_wo = _unlay_parts(matmul(sv['tok'], dx, ta=True), 0, lay['tok'][0]) \
            + _unlay_parts(matmul(sv['xa'], dx, ta=True), 0, lay['xa'][0])
        gl['w_out'] = _join(g_wo, 0)
        (dxq,), (dkv, dqg, dkg) = tmap_bwd(f"xattn_bwd_{i}", _xattn, [seg['xq']], [sv['kv'], sv['xqg'], sv['xkg']],
                                           [dxa], [True])
        g_stack['xa_q_norm'][i], g_stack['xa_k_norm'][i] = dqg[0, :XA_HD], dkg[0, :XA_HD]
        gl['w_kv'] = _join(_unlay_parts(matmul(mem_n, dkv, ta=True), 1, kv_lay), 1)
        d_memn = matmul(dkv, sv['w_kv'], tb=True, add=d_memn)
        dseg = dict(xq=dxq)
        if kind == 'a':
            (do, dog), (don,) = tmap_bwd("gla_post_bwd", _gla_post, [sv['o'], seg['og']], [sv['on']], [dtok],
                                         [True, True], grad_dtype=F32)
            grads['a_o_norm'] = don[0, :GLA_DV]
            (dq, dk, dv, dla), _ = rscan_bwd("gla_scan_bwd", sv['scan_fn'], sv['states'], sv['scan_rows'], [], [do],
                                             scan_block, grad_dtype=F32)
            (dglr,), (dw2, dbg) = tmap_bwd("gla_pre_bwd", _gla_pre, [seg['glr']], [sv['w2'], sv['bg']], [dla], [True])
            gl['w_gate2'] = _join(_unlay_parts(dw2[:GLA_RANK], 1, _heads(0, 4, 96, LANE)), 1)
            grads['a_b_gate'] = _join(_unlay_parts(dbg[0], 0, _heads(0, 4, 96, LANE)), 0)
            dseg.update(q=dq, k=dk, v=dv, glr=dglr, og=dog)
        elif kind == 'b':
            res, _ = tmap_bwd("dil_merge_bwd", _dil_merge, sv['os'] + sv['ls'], [], [dtok], [True] * 6, grad_dtype=F32)
            dqn, dkn, dvs = [], [], []
            for g, (_, r) in enumerate(DIL_GROUPS):
                a_, b_, c_ = dil_attn_bwd(f"dil_attn_bwd_{g}", sv['qn'][g], sv['kn'][g], seg['v'], res[g], res[3 + g],
                                          r, g)
                dqn.append(a_)
                dkn.append(b_)
                dvs.append(c_)
            (dq, dk), (dqg, dkg) = tmap_bwd("dil_pre_bwd", _dil_pre, [seg['q'], seg['k'], cosf, sinf],
                                            [sv['qg'], sv['kg']], dqn + dkn, [True, True, False, False])
            dseg.update(q=dq, k=dk, v=jnp.concatenate(dvs, axis=1))
            grads['b_q_norm'], grads['b_k_norm'] = dqg[0], dkg[0]
        elif kind == 'c':
            (dy, dz), (dcn,) = tmap_bwd("ssd_post_bwd", _mamba_post, [sv['y'], seg['z']], [sv['cn']], [dtok],
                                        [True, True], grad_dtype=F32)
            grads['c_norm'] = _join(_unlay_parts(dcn[0], 0, _XS_PIECES), 0)
            (dxact, ddt), (ddtb, dalog, ddsk) = rscan_bwd("ssd_scan_bwd", _ssd_step, sv['states'], sv['scan_rows'],
                                                          sv['scan_params'], [dy], scan_block, grad_dtype=F32)
            for nm, gv in (('c_dt_bias', ddtb), ('c_a_log', dalog), ('c_d', ddsk)):
                grads[nm] = _join(_unlay_parts(gv[0], 0, [], _HEAD_REPS), 0)
            (dxbc,), dcw, dcb, _ = conv_bwd("ssm_conv_bwd", seg['xbc'], sv['cw'], sv['cb'], dxact, 'silu', 512)
            grads['c_conv_w'] = _join(_unlay_parts(dcw, 1, _XBC_PIECES), 1)
            grads['c_conv_b'] = _join(_unlay_parts(dcb[0], 0, _XBC_PIECES), 0)
            dseg.update(z=dz, xbc=dxbc, dt=ddt)
        else:
            (do, dog), (don,) = tmap_bwd("hgrn_post_bwd", _hgrn_post, [sv['o'], seg['og']], [sv['on']], [dtok],
                                         [True, True], grad_dtype=F32)
            grads['d_o_norm'] = don[0]
            (dqq, dkk, di, dla), _ = rscan_bwd("hgrn_scan_bwd", sv['scan_fn'], sv['states'], sv['scan_rows'], [], [do],
                                               scan_block, grad_dtype=F32)
            (dq, df), (dlb,) = tmap_bwd("hgrn_pre_bwd", _hgrn_pre, [seg['q'], seg['f']], [sv['lbnd']], [dqq, dkk, dla],
                                        [True, True])
            grads['d_lower_bounds'] = dlb
            dseg.update(q=dq, f=df, i=di, og=dog)
        names = list(lay['segs'])
        dh = matmul_sum([dseg[n] for n in names], [sv['w_seg'][n] for n in names], True)
        g_in = []
        for n, (p, t, rp) in lay['segs'].items():
            g_in += _unlay_parts(matmul(sv['h'], dseg[n], ta=True), 1, p, rp)
        gl['w_in'] = _join(g_in, 1)
        rider = ffn.grads_rider(i, gl)
        (dx,), (dg1,), *rode = tmap_bwd(f"mix_norm_bwd_{i}", _norm_stage, [sv['x0']], [sv['g1']], [dh], [True], {0: dx},
                                        rider=rider)
        if rider is not None:
            ffn.grads_rode(i, rode[0])
        g_stack['mix_norm'][i] = dg1[0]

    _, (dmg,) = tmap_bwd("mem_norm_bwd", _norm_stage, [mem], [mem_g], [d_memn], [False])
    grads['mem_norm'] = dmg[0]
    for n, parts in g_stack.items():
        grads[n] = jnp.stack(parts)
    if isinstance(ffn, LocalLayers):
        grads.update(ffn.whole_grads())
    return loss_acc, dx, grads


def _pack(arrs, dtype, row_multiple=PACK_ROWS):
    parts, rows = [], 0
    for a in arrs:
        f = a.reshape(-1).astype(dtype)
        unit = PACK_ROWS * PACK_COLS
        pad = (-f.shape[0]) % unit
        if pad:
            f = jnp.concatenate([f, jnp.zeros((pad,), dtype)])
        parts.append(f.reshape(-1, PACK_COLS))
        rows += parts[-1].shape[0]
    if rows % row_multiple:
        parts.append(jnp.zeros((row_multiple - rows % row_multiple, PACK_COLS), dtype))
    return jnp.concatenate(parts, axis=0)


def _unpack(buf, shapes):
    out, row = [], 0
    for sh in shapes:
        n = int(np.prod(sh))
        rows = -(-n // (PACK_ROWS * PACK_COLS)) * PACK_ROWS
        out.append(buf[row:row + rows].reshape(-1)[:n].reshape(sh))
        row += rows
    return out


def _pack_rows(arrs):
    parts = []
    for a in arrs:
        f = a.reshape(-1).astype(F32)
        parts.append(jnp.pad(f, (0, (-f.shape[0]) % PACK_COLS)))
    flat = jnp.concatenate(parts)
    rows = flat.shape[0] // PACK_COLS
    return jnp.pad(flat, (0, (-rows % 16) * PACK_COLS)).reshape(-1, PACK_COLS)


def _unpack_rows(buf, shapes):
    flat, out, pos = buf.reshape(-1), [], 0
    for sh in shapes:
        n = int(np.prod(sh))
        out.append(flat[pos:pos + n].reshape(sh))
        pos += -(-n // PACK_COLS) * PACK_COLS
    return out


def _split_chips(a, axis):
    sh = a.shape
    return jnp.moveaxis(a.reshape(sh[:axis] + (N_CHIPS, sh[axis] // N_CHIPS) + sh[axis + 1:]), axis, 0)


def _merge_chips(a, axis):
    a = jnp.moveaxis(a, 0, axis)
    sh = a.shape
    return a.reshape(sh[:axis] + (sh[axis] * sh[axis + 1],) + sh[axis + 2:])


def kernel(x, mem, positions, mem_norm, mix_norm, xa_w_kv, xa_q_norm, xa_k_norm, ffn_norm, ffn_w_up, ffn_conv_w, ffn_conv_b, ffn_w_down, a_w_in, a_w_gate2, a_b_gate, a_o_norm, a_w_out, b_w_in, b_q_norm, b_k_norm, b_w_out, c_w_in, c_conv_w, c_conv_b, c_dt_bias, c_a_log, c_d, c_norm, c_w_out, d_w_in, d_lower_bounds, d_o_norm, d_w_out, loss_target, m_mem_norm, m_mix_norm, m_xa_w_kv, m_xa_q_norm, m_xa_k_norm, m_ffn_norm, m_ffn_w_up, m_ffn_conv_w, m_ffn_conv_b, m_ffn_w_down, m_a_w_in, m_a_w_gate2, m_a_b_gate, m_a_o_norm, m_a_w_out, m_b_w_in, m_b_q_norm, m_b_k_norm, m_b_w_out, m_c_w_in, m_c_conv_w, m_c_conv_b, m_c_dt_bias, m_c_a_log, m_c_d, m_c_norm, m_c_w_out, m_d_w_in, m_d_lower_bounds, m_d_o_norm, m_d_w_out, v_mem_norm, v_mix_norm, v_xa_w_kv, v_xa_q_norm, v_xa_k_norm, v_ffn_norm, v_ffn_w_up, v_ffn_conv_w, v_ffn_conv_b, v_ffn_w_down, v_a_w_in, v_a_w_gate2, v_a_b_gate, v_a_o_norm, v_a_w_out, v_b_w_in, v_b_q_norm, v_b_k_norm, v_b_w_out, v_c_w_in, v_c_conv_w, v_c_conv_b, v_c_dt_bias, v_c_a_log, v_c_d, v_c_norm, v_c_w_out, v_d_w_in, v_d_lower_bounds, v_d_o_norm, v_d_w_out):
    args = locals()
    w = {n: args[n] for n in WEIGHTS}
    m = {n: args['m_' + n] for n in WEIGHTS}
    v = {n: args['v_' + n] for n in WEIGHTS}
    cx, cy, cc = lax.axis_index("x"), lax.axis_index("y"), lax.axis_index("c")
    chip = 2 * cx + cy

    core_id, chip_id = cc.reshape(1).astype(jnp.int32), chip.reshape(1).astype(jnp.int32)
    layers = ShardedLayers(w, core_id)
    layers.first_gather()
    full = {}
    small_sharded = [n for n in SMALL if n in SHARD_AXIS]
    sg = allgather_devices(_pack([w[n] for n in small_sharded], F32))
    per_chip_s = [_unpack(sg[2 * j], [w[n].shape for n in small_sharded]) for j in range(N_CHIPS)]
    for k, n in enumerate(small_sharded):
        full[n] = _merge_chips(jnp.stack([per_chip_s[j][k] for j in range(N_CHIPS)]), SHARD_AXIS[n])
    for n in SMALL:
        if n not in SHARD_AXIS:
            full[n] = w[n]

    loss_acc, dx, grads = local_step(x[0], mem[0], positions[0], loss_target[0], full, layers)
    loss = lax.psum(jnp.sum(loss_acc), ("x", "y", "c"))

    g_big = layers.finish(chip_id)

    small_full_shapes = [grads[n].shape for n in SMALL]
    gs = sum_slabs(allgather_devices(_pack_rows([grads[n] for n in SMALL])), "sum_devices")
    g_small = {}
    for n, gfull in zip(SMALL, _unpack_rows(gs, small_full_shapes)):
        if n in SHARD_AXIS:
            ax = SHARD_AXIS[n]
            size = gfull.shape[ax] // N_CHIPS
            gfull = lax.dynamic_slice_in_dim(gfull, chip * size, size, axis=ax)
        g_small[n] = gfull

    g_out, delta, new_m, new_v = {**g_big, **g_small}, {}, {}, {}
    for n in WEIGHTS:
        sh = w[n].shape
        two_d = (-1, sh[-1])
        d_, m_, v_ = adamw(f"adamw_{n}", w[n].reshape(two_d), g_out[n].reshape(two_d), m[n].reshape(two_d),
                           v[n].reshape(two_d))
        delta[n], new_m[n], new_v[n] = d_.reshape(sh), m_.reshape(sh), v_.reshape(sh)

    return (loss, dx[None], *[g_out[n] for n in WEIGHTS], *[delta[n] for n in WEIGHTS],
            *[new_m[n] for n in WEIGHTS], *[new_v[n] for n in WEIGHTS])
```

```python
import functools
import math

import jax
import jax.numpy as jnp
import numpy as np
from jax import lax
from jax.experimental import pallas as pl
from jax.experimental.pallas import tpu as pltpu

F32 = jnp.float32
MXU_DTYPE = jnp.bfloat16
GRAD_WIRE_DTYPE = jnp.bfloat16
VMEM_LIMIT_V7X = 56 * 1024 * 1024
LANE = 128
SUBLANE = 8

D_MODEL = 1024
N_MEM = 256
EPS = 1e-6
ROPE_THETA = 10000.0
CHUNK = 64
XA_HEADS, XA_HD = 4, 64
GLA_HEADS, GLA_DK, GLA_DV, GLA_RANK, GLA_GATE_NORM = 4, 96, 192, 16, 16.0
DIL_GROUPS = ((128, 1), (512, 4), (2048, 16))
DIL_HEADS, DIL_HD, DIL_BLOCK = 4, 128, 128
SSM_HD, SSM_HEADS, SSM_GROUPS, SSM_STATE, SSM_CONV = 64, 12, 2, 128, 4
HGRN_HEADS, HGRN_DK = 6, 128
D_FF = 2816
FFN_CONV = 3
ADAM_LR, ADAM_B1, ADAM_B2, ADAM_EPS, ADAM_WD, ADAM_STEP = 0.001, 0.9, 0.999, 1e-08, 0.01, 10

MM_TILES = (2816, 1408, 1024, 768, 512, 384, 256, 128)
MM_K_TILES = (2816, 2048, 1536, 1408, 1024, 768, 512, 384, 256, 128)
MM_MIN_OUT_TILE = 512 * 1024
MM_VMEM_BUDGET = 40 * 1024 * 1024
ROW_BLOCK = 512
CONV_BLOCK = 256
SCAN_CHUNKS = 2
PACK_COLS = 1024
PACK_ROWS = 32

WEIGHTS = ['mem_norm', 'mix_norm', 'xa_w_kv', 'xa_q_norm', 'xa_k_norm', 'ffn_norm', 'ffn_w_up', 'ffn_conv_w',
           'ffn_conv_b', 'ffn_w_down', 'a_w_in', 'a_w_gate2', 'a_b_gate', 'a_o_norm', 'a_w_out', 'b_w_in', 'b_q_norm',
           'b_k_norm', 'b_w_out', 'c_w_in', 'c_conv_w', 'c_conv_b', 'c_dt_bias', 'c_a_log', 'c_d', 'c_norm', 'c_w_out',
           'd_w_in', 'd_lower_bounds', 'd_o_norm', 'd_w_out']
SHARD_AXIS = {'xa_w_kv': 1, 'ffn_w_up': 2, 'ffn_conv_w': 2, 'ffn_w_down': 1, 'a_w_in': 1, 'a_w_gate2': 1, 'a_w_out': 0,
              'b_w_in': 1, 'b_w_out': 1, 'c_w_in': 1, 'c_conv_w': 1, 'c_w_out': 0, 'd_w_in': 1, 'd_w_out': 0}
BIG = ['xa_w_kv', 'ffn_w_up', 'ffn_w_down', 'a_w_in', 'a_w_gate2', 'a_w_out', 'b_w_in', 'b_w_out', 'c_w_in', 'c_w_out',
       'd_w_in', 'd_w_out']
SMALL = [n for n in WEIGHTS if n not in BIG]
LAYERED = ['ffn_w_up', 'ffn_w_down']
N_CHIPS = 4
N_DEV = 8


class _MatmulSet:
    def __init__(self, cast, precision):
        def dot(a, b, dims):
            if cast:
                a = a.astype(MXU_DTYPE)
                b = b.astype(MXU_DTYPE)
            return lax.dot_general(a, b, (dims, ((), ())), precision=precision, preferred_element_type=F32)

        @jax.custom_vjp
        def nn(a, b):
            return dot(a, b, ((1,), (0,)))

        @jax.custom_vjp
        def nt(a, b):
            return dot(a, b, ((1,), (1,)))

        @jax.custom_vjp
        def tn(a, b):
            return dot(a, b, ((0,), (0,)))

        nn.defvjp(lambda a, b: (nn(a, b), (a, b)), lambda r, g: (nt(g, r[1]), tn(r[0], g)))
        nt.defvjp(lambda a, b: (nt(a, b), (a, b)), lambda r, g: (nn(g, r[1]), tn(g, r[0])))
        tn.defvjp(lambda a, b: (tn(a, b), (a, b)), lambda r, g: (nt(r[1], g), nn(r[0], g)))
        self.nn, self.nt, self.tn = nn, nt, tn


mm = _MatmulSet(True, None)
hi = _MatmulSet(False, lax.Precision.HIGHEST)


def _sigmoid(x):
    return jax.nn.sigmoid(x)


def _silu(x):
    return x * jax.nn.sigmoid(x)


def _softplus(x):
    return jnp.maximum(x, 0.0) + jnp.log1p(jnp.exp(-jnp.abs(x)))


def _rms(x, g, n_real=None):
    n = n_real or x.shape[-1]
    ms = jnp.sum(x * x, axis=-1, keepdims=True) * (1.0 / n)
    return x * lax.rsqrt(ms + EPS) * g


@jax.custom_vjp
def _swap_halves(x):
    return pltpu.roll(x, 64, 1)


_swap_halves.defvjp(lambda x: (_swap_halves(x), None), lambda _, g: (_swap_halves(g),))


def _tile(n, cands):
    for c in cands:
        if n % c == 0:
            return c
    raise ValueError(f"no tile for {n} among {cands}")


def _cparams(sem):
    return pltpu.CompilerParams(dimension_semantics=sem, vmem_limit_bytes=VMEM_LIMIT_V7X)


def _f32(v):
    return v.astype(F32) if jnp.issubdtype(v.dtype, jnp.floating) else v


def matmul(a, b, *, ta=False, tb=False, add=None, out_dtype=F32, b_layer=None, b_koff=0, into=None):
    m, k = (a.shape[1], a.shape[0]) if ta else a.shape
    b2 = b.shape[1:] if b_layer is not None else b.shape
    n = b2[0] if tb else b2[1]
    assert b_koff + k <= (b2[1] if tb else b2[0]), (a.shape, b.shape, ta, tb, b_koff)
    sa, sb, so = a.dtype.itemsize, b.dtype.itemsize, jnp.dtype(out_dtype).itemsize
    n_align = math.gcd(n, into[2]) if into is not None and into[2] else n
    k_align = math.gcd(k, b_koff) if b_koff else k

    def vmem(tm_, tn_, tk_):
        return (2 * tm_ * tk_ * sa + 2 * tk_ * tn_ * sb + 2 * tm_ * tn_ * so + (tm_ * tn_ * 4 if tk_ < k else 0)
                + (2 * tm_ * tn_ * add.dtype.itemsize if add is not None else 0))

    for tk in [t for t in MM_K_TILES if k_align % t == 0]:
        fits = [(tm_ * tn_, tm_, tn_) for tm_ in MM_TILES if m % tm_ == 0 for tn_ in MM_TILES
                if n % tn_ == 0 and n_align % tn_ == 0 and vmem(tm_, tn_, tk) <= MM_VMEM_BUDGET]
        if fits and (max(fits)[0] >= min(MM_MIN_OUT_TILE, m * n) or tk == MM_K_TILES[-1]):
            break
    _, tm, tn = max(fits)
    nk = k // tk
    dims = (((0,) if ta else (1,)), ((1,) if tb else (0,)))
    n_extra = (add is not None) + (into is not None)

    def body(*refs):
        a_ref, b_ref = refs[0], refs[1]
        add_ref = refs[2] if add is not None else None
        o_ref = refs[2 + n_extra]
        part = lax.dot_general(a_ref[...].astype(MXU_DTYPE), b_ref[...].astype(MXU_DTYPE), (dims, ((), ())),
                               preferred_element_type=F32)

        def finish(r):
            if add_ref is not None:
                r = r + add_ref[...].astype(F32)
            o_ref[...] = r.astype(o_ref.dtype)

        if nk == 1:
            finish(part)
            return
        acc = refs[-1]
        kk = pl.program_id(2)

        @pl.when(kk == 0)
        def _():
            acc[...] = part

        @pl.when(kk > 0)
        def _():
            acc[...] += part

        @pl.when(kk == nk - 1)
        def _():
            finish(acc[...])

    a_spec = pl.BlockSpec((tk, tm), lambda i, j, q: (q, i)) if ta else pl.BlockSpec((tm, tk), lambda i, j, q: (i, q))
    ko = b_koff // tk
    if b_layer is None:
        b_spec = (pl.BlockSpec((tn, tk), lambda i, j, q: (j, q + ko)) if tb
                  else pl.BlockSpec((tk, tn), lambda i, j, q: (q + ko, j)))
    else:
        b_spec = (pl.BlockSpec((None, tn, tk), lambda i, j, q: (b_layer, j, q + ko)) if tb
                  else pl.BlockSpec((None, tk, tn), lambda i, j, q: (b_layer, q + ko, j)))
    o_spec = pl.BlockSpec((tm, tn), lambda i, j, q: (i, j))
    ins, specs = [a, b], [a_spec, b_spec]
    if add is not None:
        ins.append(add)
        specs.append(o_spec)
    aliases = {}
    out_shape = jax.ShapeDtypeStruct((m, n), out_dtype)
    if into is not None:
        buf, layer, col0 = into
        assert buf.shape[1] == m and buf.dtype == out_dtype
        co = col0 // tn
        ins.append(buf)
        specs.append(_ANY)
        aliases = {len(ins) - 1: 0}
        o_spec = pl.BlockSpec((None, tm, tn), lambda i, j, q: (layer, i, j + co))
        out_shape = jax.ShapeDtypeStruct(buf.shape, buf.dtype)
    return pl.pallas_call(
        body, name=f"mm_{m}x{k}x{n}_{int(ta)}{int(tb)}{int(add is not None)}{int(b_layer is not None)}{int(into is not None)}",
        grid=(m // tm, n // tn, nk), in_specs=specs, out_specs=o_spec, out_shape=out_shape,
        input_output_aliases=aliases,
        scratch_shapes=[pltpu.VMEM((tm, tn), F32)] if nk > 1 else [],
        compiler_params=_cparams(("parallel", "parallel", "arbitrary")),
    )(*ins)


def _resident_tm(m, row_bytes, resident_bytes):
    for tm in (512, 256, 128):
        if m % tm == 0 and 2 * (resident_bytes + tm * row_bytes) <= MM_VMEM_BUDGET:
            return tm
    return 128


def matmul_sum(a_list, b_list, tb, add=None, b_cols=None):
    n_ops = len(a_list)
    m, n = a_list[0].shape[0], b_list[0].shape[0 if tb else 1]
    dims = ((1,), (1,) if tb else (0,))
    b_specs = ([_whole_spec(b) for b in b_list] if b_cols is None else
               [pl.BlockSpec((n, a.shape[1]), lambda i, cb=cb: (0, cb)) for a, cb in zip(a_list, b_cols)])
    tm = _resident_tm(m, sum(a.shape[1] * a.dtype.itemsize for a in a_list) + n * 4 * (1 + (add is not None)),
                      sum(n * a.shape[1] * b.dtype.itemsize for a, b in zip(a_list, b_list)))

    def body(*refs):
        acc = refs[2 * n_ops][...] if add is not None else None
        for t in range(n_ops):
            part = lax.dot_general(refs[t][...].astype(MXU_DTYPE), refs[n_ops + t][...].astype(MXU_DTYPE),
                                   (dims, ((), ())), preferred_element_type=F32)
            acc = part if acc is None else acc + part
        refs[-1][...] = acc

    o_spec = pl.BlockSpec((tm, n), lambda i: (i, 0))
    return pl.pallas_call(
        body, name=f"mm_sum_{n_ops}x{sum(a.shape[1] for a in a_list)}_{int(tb)}{int(add is not None)}", grid=(m // tm,),
        in_specs=[pl.BlockSpec((tm, a.shape[1]), lambda i: (i, 0)) for a in a_list] + b_specs
        + ([o_spec] if add is not None else []),
        out_specs=o_spec, out_shape=jax.ShapeDtypeStruct((m, n), F32),
        compiler_params=_cparams(("parallel",)),
    )(*a_list, *b_list, *([add] if add is not None else []))


def matmul_multi(a, b_list):
    n_ops = len(b_list)
    m = a.shape[0]
    tm = _resident_tm(m, a.shape[1] * a.dtype.itemsize + 4 * sum(b.shape[1] for b in b_list),
                      sum(b.size * b.dtype.itemsize for b in b_list))

    def body(*refs):
        av = refs[0][...].astype(MXU_DTYPE)
        for t in range(n_ops):
            refs[1 + n_ops + t][...] = jnp.dot(av, refs[1 + t][...].astype(MXU_DTYPE), preferred_element_type=F32)

    return pl.pallas_call(
        body, name=f"mm_multi_{n_ops}x{sum(b.shape[1] for b in b_list)}", grid=(m // tm,),
        in_specs=[pl.BlockSpec((tm, a.shape[1]), lambda i: (i, 0))] + [_whole_spec(b) for b in b_list],
        out_specs=[pl.BlockSpec((tm, b.shape[1]), lambda i: (i, 0)) for b in b_list],
        out_shape=[jax.ShapeDtypeStruct((m, b.shape[1]), F32) for b in b_list],
        compiler_params=_cparams(("parallel",)),
    )(a, *b_list)


def _row_spec(a, block):
    return pl.BlockSpec((block, a.shape[1]), lambda i: (i, 0))


def _whole_spec(a):
    return pl.BlockSpec(a.shape, lambda i: (0,) * a.ndim)


def tmap(name, fn, rows, params, outs, block=ROW_BLOCK):
    s = rows[0].shape[0]
    block = min(block, s)
    nr, npar = len(rows), len(params)

    def body(*refs):
        res = fn(*[_f32(r[...]) for r in refs[:nr]], *[_f32(p[...]) for p in refs[nr:nr + npar]])
        for o_ref, v in zip(refs[nr + npar:], res, strict=True):
            o_ref[...] = v.astype(o_ref.dtype)

    return pl.pallas_call(
        body, name=name, grid=(s // block,),
        in_specs=[_row_spec(a, block) for a in rows] + [_whole_spec(p) for p in params],
        out_specs=[pl.BlockSpec((block, w), lambda i: (i, 0)) for w, _ in outs],
        out_shape=[jax.ShapeDtypeStruct((s, w), dt) for w, dt in outs],
        compiler_params=_cparams(("parallel",)),
    )(*rows, *params)


def tmap_bwd(name, fn, rows, params, douts, row_grad, row_add=None, grad_dtype=None, block=ROW_BLOCK, rider=None):
    s = rows[0].shape[0]
    block = min(block, s)
    grad_dtype = grad_dtype or MXU_DTYPE
    nr, npar, nd = len(rows), len(params), len(douts)
    gr = [i for i in range(nr) if row_grad[i]]
    row_add = row_add or {}
    adds = [row_add[i] for i in gr if i in row_add]

    def body(*refs):
        rv = [_f32(r[...]) for r in refs[:nr]]
        pv = [_f32(p[...]) for p in refs[nr:nr + npar]]
        dv = tuple(_f32(d[...]) for d in refs[nr + npar:nr + npar + nd])
        add_refs = list(refs[nr + npar + nd:nr + npar + nd + len(adds)])
        out_refs = refs[nr + npar + nd + len(adds):]

        def f(*diff):
            rr = list(rv)
            for n_, i_ in enumerate(gr):
                rr[i_] = diff[n_]
            return tuple(fn(*rr, *diff[len(gr):]))

        _, vjp = jax.vjp(f, *[rv[i_] for i_ in gr], *pv)
        g = vjp(dv)
        for n_, i_ in enumerate(gr):
            v = g[n_]
            if i_ in row_add:
                v = v + add_refs.pop(0)[...].astype(F32)
            out_refs[n_][...] = v.astype(out_refs[n_].dtype)
        first = pl.program_id(0) == 0
        for n_ in range(npar):
            ref = out_refs[len(gr) + n_]

            @pl.when(first)
            def _(ref=ref):
                ref[...] = jnp.zeros_like(ref)

            ref[...] += g[len(gr) + n_]

    grid = (s // block,)
    n_out = len(gr) + npar
    body, r_ins, r_in_specs, r_out_specs, r_out_shapes, r_scratch, _ = _ride(
        body, rider, nr + npar + nd + len(adds), n_out, 0, grid)
    res = pl.pallas_call(
        body, name=name, grid=grid,
        in_specs=[_row_spec(a, block) for a in rows] + [_whole_spec(p) for p in params]
        + [_row_spec(d, block) for d in douts] + [_row_spec(a, block) for a in adds] + r_in_specs,
        out_specs=[_row_spec(rows[i], block) for i in gr] + [_whole_spec(p) for p in params] + r_out_specs,
        out_shape=[jax.ShapeDtypeStruct(rows[i].shape, F32 if i in row_add else grad_dtype) for i in gr]
        + [jax.ShapeDtypeStruct(p.shape, F32) for p in params] + r_out_shapes,
        scratch_shapes=r_scratch, compiler_params=_cparams(("arbitrary",)),
    )(*rows, *params, *douts, *adds, *r_ins)
    if rider is not None:
        return list(res[:len(gr)]), list(res[len(gr):n_out]), rider.results(res[n_out:])
    return list(res[:len(gr)]), list(res[len(gr):])


def rscan(name, fn, state_shapes, rows, params, outs, block):
    s = rows[0].shape[0]
    nsteps = s // block
    nr, npar, no, ns = len(rows), len(params), len(outs), len(state_shapes)

    def body(*refs):
        out_refs = refs[nr + npar:nr + npar + no]
        sav_refs = refs[nr + npar + no:nr + npar + no + ns]
        st_refs = refs[nr + npar + no + ns:]

        @pl.when(pl.program_id(0) == 0)
        def _():
            for st in st_refs:
                st[...] = jnp.zeros_like(st)

        sts = tuple(st[...] for st in st_refs)
        for sv, v in zip(sav_refs, sts):
            sv[...] = v
        new, res = fn(sts, *[_f32(r[...]) for r in refs[:nr]], *[_f32(p[...]) for p in refs[nr:nr + npar]])
        for st, v in zip(st_refs, new, strict=True):
            st[...] = v
        for o_ref, v in zip(out_refs, res, strict=True):
            o_ref[...] = v.astype(o_ref.dtype)

    res = pl.pallas_call(
        body, name=name, grid=(nsteps,),
        in_specs=[_row_spec(a, block) for a in rows] + [_whole_spec(p) for p in params],
        out_specs=[pl.BlockSpec((block, w), lambda i: (i, 0)) for w, _ in outs]
        + [pl.BlockSpec(sh, lambda i: (i, 0)) for sh in state_shapes],
        out_shape=[jax.ShapeDtypeStruct((s, w), dt) for w, dt in outs]
        + [jax.ShapeDtypeStruct((nsteps * sh[0], sh[1]), F32) for sh in state_shapes],
        scratch_shapes=[pltpu.VMEM(sh, F32) for sh in state_shapes],
        compiler_params=_cparams(("arbitrary",)),
    )(*rows, *params)
    return list(res[:no]), list(res[no:])


def rscan_bwd(name, fn, saved, rows, params, douts, block, grad_dtype=None):
    s = rows[0].shape[0]
    nsteps = s // block
    grad_dtype = grad_dtype or MXU_DTYPE
    nr, npar, nd, ns = len(rows), len(params), len(douts), len(saved)
    state_shapes = [(sv.shape[0] // nsteps, sv.shape[1]) for sv in saved]

    def body(*refs):
        rv = [_f32(r[...]) for r in refs[:nr]]
        pv = [_f32(p[...]) for p in refs[nr:nr + npar]]
        dv = tuple(_f32(d[...]) for d in refs[nr + npar:nr + npar + nd])
        sv = tuple(x[...] for x in refs[nr + npar + nd:nr + npar + nd + ns])
        out_refs = refs[nr + npar + nd + ns:nr + npar + nd + ns + nr + npar]
        dst_refs = refs[nr + npar + nd + ns + nr + npar:]
        first = pl.program_id(0) == 0

        @pl.when(first)
        def _():
            for d in dst_refs:
                d[...] = jnp.zeros_like(d)

        def f(sts, *args):
            return fn(sts, *args)

        _, vjp = jax.vjp(f, sv, *rv, *pv)
        g = vjp((tuple(d[...] for d in dst_refs), dv))
        for d, v in zip(dst_refs, g[0], strict=True):
            d[...] = v
        for n_ in range(nr):
            out_refs[n_][...] = g[1 + n_].astype(out_refs[n_].dtype)
        for n_ in range(npar):
            ref = out_refs[nr + n_]

            @pl.when(first)
            def _(ref=ref):
                ref[...] = jnp.zeros_like(ref)

            ref[...] += g[1 + nr + n_]

    rev = lambda i: (nsteps - 1 - i, 0)
    res = pl.pallas_call(
        body, name=name, grid=(nsteps,),
        in_specs=[pl.BlockSpec((block, a.shape[1]), rev) for a in rows] + [_whole_spec(p) for p in params]
        + [pl.BlockSpec((block, d.shape[1]), rev) for d in douts] + [pl.BlockSpec(sh, rev) for sh in state_shapes],
        out_specs=[pl.BlockSpec((block, a.shape[1]), rev) for a in rows] + [_whole_spec(p) for p in params],
        out_shape=[jax.ShapeDtypeStruct(a.shape, grad_dtype) for a in rows]
        + [jax.ShapeDtypeStruct(p.shape, F32) for p in params],
        scratch_shapes=[pltpu.VMEM(sh, F32) for sh in state_shapes],
        compiler_params=_cparams(("arbitrary",)),
    )(*rows, *params, *douts, *saved)
    return list(res[:nr]), list(res[nr:])


def _norm_stage(x, g):
    return (_rms(x, g),)


def _tril():
    r = lax.broadcasted_iota(jnp.int32, (CHUNK, CHUNK), 0)
    c = lax.broadcasted_iota(jnp.int32, (CHUNK, CHUNK), 1)
    return r >= c


def _gla_chunk(st, q, k, v, la, b):
    tril = _tril()
    rowi = lax.broadcasted_iota(jnp.int32, (CHUNK, 1), 0)
    b_last = jnp.sum(la, axis=0, keepdims=True)
    b_ref = jnp.sum(jnp.where(rowi < CHUNK // 2, la, 0.0), axis=0, keepdims=True)
    att = mm.nt(q * jnp.exp(b - b_ref), k * jnp.exp(b_ref - b))
    att = jnp.where(tril, att, 0.0)
    o = mm.nn(att, v) + mm.nn(q * jnp.exp(b), st)
    decay = jnp.exp(jnp.broadcast_to(b_last, (LANE, LANE)).T)
    decay = jnp.concatenate([decay] * (v.shape[1] // LANE), axis=1)
    st2 = decay * st + mm.tn(k * jnp.exp(b_last - b), v)
    return st2, o


def _gla_step(heads, vp, scale):
    kp = LANE

    def fn(states, q, k, v, la):
        sts = list(states)
        trif = _tril().astype(F32)
        rows = []
        for c in range(q.shape[0] // CHUNK):
            r = slice(c * CHUNK, (c + 1) * CHUNK)
            b_all = hi.nn(trif, la[r])
            oh = []
            for h in range(heads):
                ks, vs = slice(h * kp, (h + 1) * kp), slice(h * vp, (h + 1) * vp)
                qh = q[r, ks] * scale if scale != 1.0 else q[r, ks]
                sts[h], o = _gla_chunk(sts[h], qh, k[r, ks], v[r, vs], la[r, ks], b_all[:, ks])
                oh.append(o)
            rows.append(jnp.concatenate(oh, axis=1))
        return tuple(sts), (jnp.concatenate(rows, axis=0),)

    return fn


def _ssd_step(states, xa, dtr, dtb, alog, dsk):
    sts = list(states)
    trif = _tril().astype(F32)
    wide = lax.broadcasted_iota(jnp.int32, (CHUNK, LANE), 0) >= lax.broadcasted_iota(jnp.int32, (CHUNK, LANE), 1)
    hg = SSM_HEADS // SSM_GROUPS
    xw = SSM_HEADS * LANE
    lane, head = lax.broadcasted_iota(jnp.int32, (LANE, xw), 1), lax.broadcasted_iota(jnp.int32, (LANE, xw), 0)
    spread = ((lane >= head * LANE) & (lane < (head + 1) * LANE)).astype(F32)
    neg_a = -jnp.exp(alog)
    pad = jnp.zeros((CHUNK, LANE), F32)
    rows = []
    for c in range(xa.shape[0] // CHUNK):
        r = slice(c * CHUNK, (c + 1) * CHUNK)
        dt_all = _softplus(hi.nn(dtr[r], spread) + dtb)
        a_all = dt_all * neg_a
        acs_all = hi.nn(trif, a_all)
        last_all = jnp.sum(a_all, axis=0, keepdims=True)
        yh = []
        for g in range(SSM_GROUPS):
            bm = xa[r, xw + g * LANE:xw + (g + 1) * LANE]
            cm = xa[r, xw + (SSM_GROUPS + g) * LANE:xw + (SSM_GROUPS + g + 1) * LANE]
            cb = mm.nt(cm, jnp.concatenate([bm, pad], axis=0))
            for hh in range(hg):
                h = g * hg + hh
                ls = slice(h * LANE, (h + 1) * LANE)
                xs, acs, acs_last = xa[r, ls], acs_all[:, ls], last_all[:, ls]
                xdt = xs * dt_all[:, ls]
                seg = acs - jnp.concatenate([acs, pad], axis=0).T[:CHUNK]
                lmat = jnp.exp(jnp.where(wide, seg, -1e30))
                y = (mm.nn(cb * lmat, jnp.concatenate([xdt, pad], axis=0)) + mm.nn(cm, sts[h]) * jnp.exp(acs)
                     + dsk[:, ls] * xs)
                sts[h] = jnp.exp(acs_last) * sts[h] + mm.tn(bm, xdt * jnp.exp(acs_last - acs))
                yh.append(y)
        rows.append(jnp.concatenate(yh, axis=1))
    return tuple(sts), (jnp.concatenate(rows, axis=0),)


def _gla_pre(glr, w2, bg):
    z = mm.nn(glr, w2) + bg
    return (-_softplus(-z) * (1.0 / GLA_GATE_NORM),)


def _gla_post(o, og, g):
    w = 2 * LANE
    return (jnp.concatenate([_rms(o[:, h * w:(h + 1) * w], g, GLA_DV) * _silu(og[:, h * w:(h + 1) * w])
                             for h in range(GLA_HEADS)], axis=1),)


def _hgrn_pre(q, f, lbnd):
    e = jnp.exp(lbnd - jnp.max(lbnd, axis=0, keepdims=True))
    rowi = lax.broadcasted_iota(jnp.int32, e.shape, 0)
    lb = jnp.sum(jnp.where(rowi >= 1, e, 0.0), axis=0, keepdims=True) / jnp.sum(e, axis=0, keepdims=True)
    fg = lb + (1.0 - lb) * _sigmoid(f)
    return _silu(q), 1.0 - fg, jnp.log(fg)


def _hgrn_post(o, og, g):
    return (jnp.concatenate([_rms(o[:, h * LANE:(h + 1) * LANE], g) for h in range(HGRN_HEADS)], axis=1)
            * _sigmoid(og),)


def _mamba_post(y, z, g):
    v = y * _silu(z)
    w = (SSM_HEADS // SSM_GROUPS) * LANE
    n_real = (SSM_HEADS // SSM_GROUPS) * SSM_HD
    return (jnp.concatenate([_rms(v[:, i * w:(i + 1) * w], g[:, i * w:(i + 1) * w], n_real)
                             for i in range(SSM_GROUPS)], axis=1),)


def _dil_pre(q, k, cosf, sinf, qg, kg):
    def groups(x, g):
        out = []
        for grp in range(len(DIL_GROUPS)):
            hs = []
            for h in range(grp * DIL_HEADS, (grp + 1) * DIL_HEADS):
                n = _rms(x[:, h * LANE:(h + 1) * LANE], g)
                hs.append(n * cosf + _swap_halves(n) * sinf)
            out.append(jnp.concatenate(hs, axis=1))
        return out

    return (*groups(q, qg), *groups(k, kg))


def _dil_merge(o0, o1, o2, l0, l1, l2):
    m = jnp.maximum(jnp.maximum(l0, l1), l2)
    e0, e1, e2 = jnp.exp(l0 - m), jnp.exp(l1 - m), jnp.exp(l2 - m)
    return ((e0 * o0 + e1 * o1 + e2 * o2) / (e0 + e1 + e2),)


def _dil_block(q, kp, kc, vp, vc, lim):
    kk = jnp.concatenate([kp, kc], axis=0)
    vv = jnp.concatenate([vp, vc], axis=0)
    s = mm.nt(q, kk) * (DIL_HD ** -0.5)
    i = lax.broadcasted_iota(jnp.int32, s.shape, 0)
    j = lax.broadcasted_iota(jnp.int32, s.shape, 1)
    dist = DIL_BLOCK + i - j
    s = jnp.where((dist >= 0) & (dist <= DIL_BLOCK) & (j >= lim), s, -1e30)
    m = jnp.max(s, axis=-1, keepdims=True)
    p = jnp.exp(s - m)
    l = jnp.sum(p, axis=-1, keepdims=True)
    return mm.nn(p / l, vv), jnp.broadcast_to(m + jnp.log(l), (q.shape[0], LANE))


def _xattn(xq, kv, qg, kg):
    w = XA_HEADS * LANE
    os_ = []
    for h in range(XA_HEADS):
        ls = slice(h * LANE, (h + 1) * LANE)
        q = _rms(xq[:, ls], qg, XA_HD)
        k = _rms(kv[:, ls], kg, XA_HD)
        s = mm.nt(q, k) * (XA_HD ** -0.5)
        p = jnp.exp(s - jnp.max(s, axis=-1, keepdims=True))
        p = p / jnp.sum(p, axis=-1, keepdims=True)
        os_.append(mm.nn(p, kv[:, w + h * LANE:w + (h + 1) * LANE]))
    return (jnp.concatenate(os_, axis=1),)


def _dil_geometry(s, w, r, g, v_cols):
    hb = DIL_HEADS if r == 1 else 1
    rb = DIL_BLOCK * r
    nb = s // rb
    bw = hb * LANE
    v_col0 = g * (w // bw)
    assert v_cols % bw == 0 and s % rb == 0
    return hb, rb, nb, bw, v_col0


def _sub(r, res):
    return pl.ds(res, DIL_BLOCK, stride=r) if r > 1 else slice(None)


def dil_attn(name, q, k, v, r, g):
    s, w = q.shape
    hb, rb, nb, bw, v_col0 = _dil_geometry(s, w, r, g, v.shape[1])

    def body(q_r, kp_r, kc_r, vp_r, vc_r, o_r, l_r):
        lim = jnp.where(pl.program_id(1) == 0, DIL_BLOCK, 0)
        for res in range(r):
            rows = _sub(r, res)
            for h in range(hb):
                ls = slice(h * LANE, (h + 1) * LANE)
                o, lse = _dil_block(q_r[rows, ls], kp_r[rows, ls], kc_r[rows, ls], vp_r[rows, ls], vc_r[rows, ls], lim)
                o_r[rows, ls] = o
                l_r[rows, ls] = lse

    cur = pl.BlockSpec((rb, bw), lambda hblk, n: (n, hblk))
    prev = pl.BlockSpec((rb, bw), lambda hblk, n: (jnp.maximum(n - 1, 0), hblk))
    vcur = pl.BlockSpec((rb, bw), lambda hblk, n: (n, v_col0 + hblk))
    vprev = pl.BlockSpec((rb, bw), lambda hblk, n: (jnp.maximum(n - 1, 0), v_col0 + hblk))
    return pl.pallas_call(
        body, name=name, grid=(w // bw, nb), in_specs=[cur, prev, cur, vprev, vcur], out_specs=[cur, cur],
        out_shape=[jax.ShapeDtypeStruct((s, w), F32)] * 2,
        compiler_params=_cparams(("parallel", "parallel")),
    )(q, k, k, v, v)


def dil_attn_bwd(name, q, k, v, do, dlse, r, g):
    s, w = q.shape
    hb, rb, nb, bw, v_col0 = _dil_geometry(s, w, r, g, v.shape[1])

    def body(q_r, kp_r, kc_r, vp_r, vc_r, do_r, dl_r, dq_r, dk_r, dv_r, ck, cv):
        i = pl.program_id(1)
        lim = jnp.where(i == nb - 1, DIL_BLOCK, 0)

        @pl.when(i == 0)
        def _():
            ck[...] = jnp.zeros_like(ck)
            cv[...] = jnp.zeros_like(cv)

        for res in range(r):
            rows = _sub(r, res)
            for h in range(hb):
                ls = slice(h * LANE, (h + 1) * LANE)
                _, vjp = jax.vjp(functools.partial(_dil_block, lim=lim),
                                 q_r[rows, ls], kp_r[rows, ls], kc_r[rows, ls], vp_r[rows, ls], vc_r[rows, ls])
                gq, gkp, gkc, gvp, gvc = vjp((do_r[rows, ls], dl_r[rows, ls]))
                dq_r[rows, ls] = gq
                dk_r[rows, ls] = gkc + ck[rows, ls]
                dv_r[rows, ls] = gvc + cv[rows, ls]
                ck[rows, ls] = gkp
                cv[rows, ls] = gvp

    cur = pl.BlockSpec((rb, bw), lambda hblk, i: (nb - 1 - i, hblk))
    prev = pl.BlockSpec((rb, bw), lambda hblk, i: (jnp.maximum(nb - 2 - i, 0), hblk))
    vcur = pl.BlockSpec((rb, bw), lambda hblk, i: (nb - 1 - i, v_col0 + hblk))
    vprev = pl.BlockSpec((rb, bw), lambda hblk, i: (jnp.maximum(nb - 2 - i, 0), v_col0 + hblk))
    return pl.pallas_call(
        body, name=name, grid=(w // bw, nb), in_specs=[cur, prev, cur, vprev, vcur, cur, cur],
        out_specs=[cur, cur, cur], out_shape=[jax.ShapeDtypeStruct((s, w), F32)] * 3,
        scratch_shapes=[pltpu.VMEM((rb, bw), F32)] * 2,
        compiler_params=_cparams(("parallel", "arbitrary")),
    )(q, k, k, v, v, do, dlse)


def _dsilu(u):
    sg = _sigmoid(u)
    return sg * (1.0 + u * (1.0 - sg))


def _ride(body, rider, n_in, n_out, n_scratch, grid):
    if rider is None:
        return body, [], [], [], [], [], None
    ni, no = len(rider.ins), len(rider.out_shapes)

    def wrapped(*refs):
        k_in, r_in = refs[:n_in], refs[n_in:n_in + ni]
        k_out, r_out = refs[n_in + ni:n_in + ni + n_out], refs[n_in + ni + n_out:n_in + ni + n_out + no]
        k_scr = refs[n_in + ni + n_out + no:n_in + ni + n_out + no + n_scratch]
        send_sems, recv_sems = refs[-2], refs[-1]
        first = functools.reduce(jnp.logical_and, [pl.program_id(a) == 0 for a in range(len(grid))])
        last = functools.reduce(jnp.logical_and, [pl.program_id(a) == g - 1 for a, g in enumerate(grid)])

        @pl.when(first)
        def _():
            rider.start(r_in, r_out, send_sems, recv_sems)

        body(*k_in, *k_out, *k_scr)

        @pl.when(last)
        def _():
            rider.finish(r_in, r_out, send_sems, recv_sems)

    sems = [pltpu.SemaphoreType.DMA((rider.n_sems,)), pltpu.SemaphoreType.DMA((rider.n_sems,))]
    return wrapped, rider.ins, [_ANY] * ni, [_ANY] * no, rider.out_shapes, sems, ("arbitrary",) * len(grid)


CONV_STRIP = 16


def _shifted_rows(prev8, cur_r, next8, lanes, s0, n, sh, block):
    if s0 - sh < 0:
        assert s0 == 0
        xp = jnp.concatenate([prev8, cur_r[0:n, lanes]], axis=0)
        return pltpu.roll(xp, sh, 0)[SUBLANE:SUBLANE + n]
    if s0 - sh + n > block:
        assert s0 == block and n == SUBLANE
        xp = jnp.concatenate([cur_r[block - SUBLANE:block, lanes], next8], axis=0)
        return (pltpu.roll(xp, sh, 0) if sh else xp)[SUBLANE:]
    return cur_r[pl.ds(s0 - sh, n), lanes]


def conv_fwd(name, x, w, b, mode, out_dtype, tc, block=CONV_BLOCK, rider=None):
    s, c = x.shape
    ntap = w.shape[0]
    block = min(block, s)
    f = c // 2 if mode == 'glu' else c
    nh = 2 if mode == 'glu' else 1
    off = f // tc

    def body(*refs):
        first = pl.program_id(1) == 0
        o_ref = refs[-1]

        def column(cidx, carry):
            lanes = pl.ds(pl.multiple_of(cidx * LANE, LANE), LANE)
            prevs = [jnp.where(first, 0.0, refs[4 * hlf][:, lanes]) for hlf in range(nh)]
            for s0 in range(0, block, CONV_STRIP):
                us = []
                for hlf in range(nh):
                    _, cur_r, w_r, b_r = refs[4 * hlf:4 * hlf + 4]
                    acc = b_r[:, lanes]
                    for j in range(ntap):
                        xs = _shifted_rows(prevs[hlf], cur_r, None, lanes, s0, CONV_STRIP, ntap - 1 - j, block)
                        acc = acc + w_r[j:j + 1, lanes] * xs
                    us.append(acc)
                res = _silu(us[0]) * us[1] if mode == 'glu' else _silu(us[0])
                o_ref[pl.ds(s0, CONV_STRIP), lanes] = res.astype(o_ref.dtype)
            return carry

        lax.fori_loop(0, tc // LANE, column, 0)

    rb = block // SUBLANE
    ins, specs = [], []
    for hlf in range(nh):
        o = hlf * off
        ins += [x, x, w, b]
        specs += [pl.BlockSpec((SUBLANE, tc), lambda j, i, o=o: (jnp.maximum(i * rb - 1, 0), j + o)),
                  pl.BlockSpec((block, tc), lambda j, i, o=o: (i, j + o)),
                  pl.BlockSpec((ntap, tc), lambda j, i, o=o: (0, j + o)),
                  pl.BlockSpec((1, tc), lambda j, i, o=o: (0, j + o))]
    grid = (f // tc, s // block)
    body, r_ins, r_in_specs, r_out_specs, r_out_shapes, r_scratch, sem = _ride(body, rider, len(ins), 1, 0, grid)
    res = pl.pallas_call(
        body, name=name, grid=grid, in_specs=specs + r_in_specs,
        out_specs=[pl.BlockSpec((block, tc), lambda j, i: (i, j))] + r_out_specs,
        out_shape=[jax.ShapeDtypeStruct((s, f), out_dtype)] + r_out_shapes, scratch_shapes=r_scratch,
        compiler_params=_cparams(sem or ("parallel", "parallel")),
    )(*ins, *r_ins)
    return res[0] if rider is None else (res[0], rider.results(res[1:]))


def conv_bwd(name, x, w, b, dout, mode, tc, block=CONV_BLOCK, rider=None):
    s, c = x.shape
    ntap = w.shape[0]
    block = min(block, s)
    nblk = s // block
    f = c // 2 if mode == 'glu' else c
    nh = 2 if mode == 'glu' else 1
    off = f // tc
    ext = block + SUBLANE

    def body(*refs):
        i = pl.program_id(1)
        first, last = i == 0, i == nblk - 1
        dcur_r, dnext_r = refs[5 * nh], refs[5 * nh + 1]
        outs = refs[5 * nh + 2:5 * nh + 2 + 3 * nh]
        du_scr = refs[5 * nh + 2 + 3 * nh:]

        @pl.when(first)
        def _():
            for hlf in range(nh):
                outs[3 * hlf + 1][...] = jnp.zeros_like(outs[3 * hlf + 1])
                outs[3 * hlf + 2][...] = jnp.zeros_like(outs[3 * hlf + 2])

        def column(cidx, carry):
            lanes = pl.ds(pl.multiple_of(cidx * LANE, LANE), LANE)
            prevs = [jnp.where(first, 0.0, refs[5 * hlf][:, lanes]) for hlf in range(nh)]
            nexts = [jnp.where(last, 0.0, refs[5 * hlf + 2][:, lanes]) for hlf in range(nh)]
            db_acc = [jnp.zeros((CONV_STRIP, LANE), F32) for _ in range(nh)]
            dw_acc = [[jnp.zeros((CONV_STRIP, LANE), F32) for _ in range(ntap)] for _ in range(nh)]
            for s0 in range(0, ext, CONV_STRIP):
                n = min(CONV_STRIP, ext - s0)
                d_e = dcur_r[pl.ds(s0, n), lanes] if s0 < block else jnp.where(last, 0.0, dnext_r[:, lanes])
                xs, us = [], []
                for hlf in range(nh):
                    cur_r, w_r, b_r = refs[5 * hlf + 1], refs[5 * hlf + 3], refs[5 * hlf + 4]
                    sh_rows = [_shifted_rows(prevs[hlf], cur_r, nexts[hlf], lanes, s0, n, ntap - 1 - j, block)
                               for j in range(ntap)]
                    acc = b_r[:, lanes]
                    for j in range(ntap):
                        acc = acc + w_r[j:j + 1, lanes] * sh_rows[j]
                    xs.append(sh_rows)
                    us.append(acc)
                dus = [d_e * us[1] * _dsilu(us[0]), d_e * _silu(us[0])] if mode == 'glu' else [d_e * _dsilu(us[0])]
                for hlf in range(nh):
                    du_scr[hlf][pl.ds(s0, n), lanes] = dus[hlf]
                    if s0 < block:
                        db_acc[hlf] = db_acc[hlf] + dus[hlf]
                        for j in range(ntap):
                            dw_acc[hlf][j] = dw_acc[hlf][j] + dus[hlf] * xs[hlf][j]
            for hlf in range(nh):
                w_r = refs[5 * hlf + 3]
                dx_r, dw_r, db_r = outs[3 * hlf:3 * hlf + 3]
                db_r[:, lanes] += jnp.sum(db_acc[hlf], axis=0, keepdims=True)
                for j in range(ntap):
                    dw_r[j:j + 1, lanes] += jnp.sum(dw_acc[hlf][j], axis=0, keepdims=True)
                for s0 in range(0, block, CONV_STRIP):
                    dx = None
                    for j in range(ntap):
                        term = w_r[j:j + 1, lanes] * du_scr[hlf][pl.ds(s0 + ntap - 1 - j, CONV_STRIP), lanes]
                        dx = term if dx is None else dx + term
                    dx_r[pl.ds(s0, CONV_STRIP), lanes] = dx.astype(dx_r.dtype)
            return carry

        lax.fori_loop(0, tc // LANE, column, 0)

    rb = block // SUBLANE
    nrow8 = s // SUBLANE
    ins, specs = [], []
    for hlf in range(nh):
        o = hlf * off
        ins += [x, x, x, w, b]
        specs += [pl.BlockSpec((SUBLANE, tc), lambda j, i, o=o: (jnp.maximum(i * rb - 1, 0), j + o)),
                  pl.BlockSpec((block, tc), lambda j, i, o=o: (i, j + o)),
                  pl.BlockSpec((SUBLANE, tc), lambda j, i, o=o: (jnp.minimum((i + 1) * rb, nrow8 - 1), j + o)),
                  pl.BlockSpec((ntap, tc), lambda j, i, o=o: (0, j + o)),
                  pl.BlockSpec((1, tc), lambda j, i, o=o: (0, j + o))]
    ins += [dout, dout]
    specs += [pl.BlockSpec((block, tc), lambda j, i: (i, j)),
              pl.BlockSpec((SUBLANE, tc), lambda j, i: (jnp.minimum((i + 1) * rb, nrow8 - 1), j))]
    out_specs, out_shape = [], []
    for hlf in range(nh):
        out_specs += [pl.BlockSpec((block, tc), lambda j, i: (i, j)), pl.BlockSpec((ntap, tc), lambda j, i: (0, j)),
                      pl.BlockSpec((1, tc), lambda j, i: (0, j))]
        out_shape += [jax.ShapeDtypeStruct((s, f), MXU_DTYPE), jax.ShapeDtypeStruct((ntap, f), F32),
                      jax.ShapeDtypeStruct((1, f), F32)]
    grid = (f // tc, nblk)
    body, r_ins, r_in_specs, r_out_specs, r_out_shapes, r_scratch, sem = _ride(body, rider, len(ins), 3 * nh, nh, grid)
    res = pl.pallas_call(
        body, name=name, grid=grid, in_specs=specs + r_in_specs, out_specs=out_specs + r_out_specs,
        out_shape=out_shape + r_out_shapes, scratch_shapes=[pltpu.VMEM((ext, tc), F32)] * nh + r_scratch,
        compiler_params=_cparams(sem or ("parallel", "arbitrary")),
    )(*ins, *r_ins)
    rode = None if rider is None else rider.results(res[3 * nh:])
    if nh == 1:
        return [res[0]], res[1], res[2], rode
    return ([res[0], res[3]], jnp.concatenate([res[1], res[4]], axis=1), jnp.concatenate([res[2], res[5]], axis=1),
            rode)


def loss_head(y, target, block=ROW_BLOCK):
    s, d = y.shape
    block = min(block, s)

    def body(y_r, t_r, acc_r, dy_r):
        e = y_r[...] - t_r[...]
        dy_r[...] = e * (1.0 / d)

        @pl.when(pl.program_id(0) == 0)
        def _():
            acc_r[...] = jnp.zeros_like(acc_r)

        acc_r[...] += jnp.sum((e * e).reshape(block // SUBLANE, SUBLANE, d), axis=0) * (0.5 / d)

    return pl.pallas_call(
        body, name="loss_head", grid=(s // block,),
        in_specs=[pl.BlockSpec((block, d), lambda i: (i, 0))] * 2,
        out_specs=[pl.BlockSpec((SUBLANE, d), lambda i: (0, 0)), pl.BlockSpec((block, d), lambda i: (i, 0))],
        out_shape=[jax.ShapeDtypeStruct((SUBLANE, d), F32), jax.ShapeDtypeStruct((s, d), F32)],
        compiler_params=_cparams(("arbitrary",)),
    )(y, target)


def adamw(name, w, g, m, v):
    r, c = w.shape
    tr = r if r <= 512 else _tile(r, (512, 256, 128, 64, 32, 16, 8))
    if c * tr * 4 > (1 << 21):
        tr = _tile(r, (256, 128, 64, 32, 16, 8))

    def body(w_r, g_r, m_r, v_r, d_r, nm_r, nv_r):
        gg = g_r[...]
        nm = ADAM_B1 * m_r[...] + (1.0 - ADAM_B1) * gg
        nv = ADAM_B2 * v_r[...] + (1.0 - ADAM_B2) * (gg * gg)
        m_hat = nm / (1.0 - ADAM_B1 ** ADAM_STEP)
        v_hat = nv / (1.0 - ADAM_B2 ** ADAM_STEP)
        d_r[...] = -ADAM_LR * (m_hat / (jnp.sqrt(v_hat) + ADAM_EPS) + ADAM_WD * w_r[...])
        nm_r[...] = nm
        nv_r[...] = nv

    spec = pl.BlockSpec((tr, c), lambda i: (i, 0))
    return pl.pallas_call(
        body, name=name, grid=(r // tr,), in_specs=[spec] * 4, out_specs=[spec] * 3,
        out_shape=[jax.ShapeDtypeStruct((r, c), F32)] * 3, compiler_params=_cparams(("parallel",)),
    )(w, g, m, v)


MESH = pl.DeviceIdType.MESH
_ANY = pl.BlockSpec(memory_space=pl.ANY)


def _place():
    return lax.axis_index("x"), lax.axis_index("y"), lax.axis_index("c")


class Packed:
    def __init__(self, shard_shape):
        self.r, self.c = shard_shape
        self.h = self.r // 2
        self.whole = (N_CHIPS, self.r, self.c)
        self.got = (N_CHIPS, self.h, self.c)
        self.slab_half = (self.h, self.c)

    def shard_half(self, ref, core):
        return ref.at[pl.ds(core * self.h, self.h)]

    def whole_half(self, ref, chip, core):
        return ref.at[chip, pl.ds(core * self.h, self.h)]

    def place(self, whole, shard, chip):
        return lax.dynamic_update_slice(whole, shard[None], (chip, 0, 0))

    def grad_half(self, ref, core):
        return ref.at[:, core]

    def pair_slab(self, ref, chip):
        return ref.at[chip]


class SlabCols:
    def __init__(self, shard_shape):
        self.r, self.c = shard_shape
        self.h = self.r // 2
        self.whole = (self.r, N_CHIPS * self.c)
        self.got = (self.h, N_CHIPS * self.c)
        self.slab_half = (self.h, self.c)

    def _cols(self, chip):
        return pl.ds(pl.multiple_of(chip * self.c, LANE), self.c)

    def shard_half(self, ref, core):
        return ref.at[pl.ds(core * self.h, self.h)]

    def whole_half(self, ref, chip, core):
        return ref.at[pl.ds(core * self.h, self.h), self._cols(chip)]

    def place(self, whole, shard, chip):
        return lax.dynamic_update_slice_in_dim(whole, shard, chip * self.c, 1)

    def grad_half(self, ref, core):
        return ref.at[pl.ds(core * self.h, self.h)]

    def pair_slab(self, ref, chip):
        return ref.at[:, self._cols(chip)]


class GatherRider:
    def __init__(self, shards, kinds):
        self.ins, self.kinds, n = list(shards), kinds, len(shards)
        self.out_shapes = [jax.ShapeDtypeStruct(k.whole, s.dtype) for k, s in zip(kinds, shards)]
        self.n_sems = 6 * n

    def _copies(self, w_refs, out_refs, send_sems, recv_sems):
        x, y, cc = _place()
        chips = [(1 - x, y), (x, 1 - y), (1 - x, 1 - y)]

        def copy(t, k, chip, core, to, src=None):
            dst = self.kinds[t].whole_half(out_refs[t], 2 * chip[0] + chip[1], core)
            return pltpu.make_async_remote_copy(
                src_ref=dst if src is None else src, dst_ref=dst, send_sem=send_sems.at[6 * t + k],
                recv_sem=recv_sems.at[6 * t + k], device_id=to, device_id_type=MESH)

        first = [copy(t, j, (x, y), cc, (*chip, cc), src=self.kinds[t].shard_half(w_refs[t], cc))
                 for t in range(len(self.ins)) for j, chip in enumerate(chips)]
        return copy, first, chips, (x, y, cc)

    def start(self, w_refs, out_refs, send_sems, recv_sems):
        for cp in self._copies(w_refs, out_refs, send_sems, recv_sems)[1]:
            cp.start()

    def finish(self, w_refs, out_refs, send_sems, recv_sems):
        copy, first, chips, (x, y, cc) = self._copies(w_refs, out_refs, send_sems, recv_sems)
        passed = []
        for t in range(len(self.ins)):
            for j, chip in enumerate(chips):
                copy(t, j, chip, cc, (x, y, cc)).wait_recv()
                passed.append(copy(t, 3 + j, chip, cc, (x, y, 1 - cc)))
                passed[-1].start()
        for t in range(len(self.ins)):
            for j, chip in enumerate(chips):
                copy(t, 3 + j, chip, 1 - cc, (x, y, cc)).wait_recv()
        for cp in first + passed:
            cp.wait_send()

    def results(self, outs):
        chip = 2 * lax.axis_index("x") + lax.axis_index("y")
        return [k.place(o, s, chip) for k, o, s in zip(self.kinds, outs, self.ins)]


class ExchangeRider:
    def __init__(self, pairs, kinds):
        self.ins, self.kinds = list(pairs), kinds
        self.out_shapes = [jax.ShapeDtypeStruct((N_CHIPS,) + k.slab_half, p.dtype) for k, p in zip(kinds, pairs)]
        self.n_sems = 3 * len(pairs)

    def start(self, p_refs, out_refs, send_sems, recv_sems):
        x, y, cc = _place()
        for t in range(len(self.ins)):
            for j, chip in enumerate([(1 - x, y), (x, 1 - y), (1 - x, 1 - y)]):
                pltpu.make_async_remote_copy(
                    src_ref=self.kinds[t].pair_slab(p_refs[t], 2 * chip[0] + chip[1]), dst_ref=out_refs[t].at[2 * x + y],
                    send_sem=send_sems.at[3 * t + j], recv_sem=recv_sems.at[3 * t + j], device_id=(*chip, cc),
                    device_id_type=MESH).start()

    def finish(self, p_refs, out_refs, send_sems, recv_sems):
        x, y, cc = _place()
        me = 2 * x + y
        for t in range(len(self.ins)):
            for j, chip in enumerate([(1 - x, y), (x, 1 - y), (1 - x, 1 - y)]):
                them = 2 * chip[0] + chip[1]
                pltpu.make_async_remote_copy(
                    src_ref=self.kinds[t].pair_slab(p_refs[t], them), dst_ref=out_refs[t].at[them],
                    send_sem=send_sems.at[3 * t + j], recv_sem=recv_sems.at[3 * t + j], device_id=(x, y, cc),
                    device_id_type=MESH).wait()

    def results(self, outs):
        return list(outs)


class SwapRider:
    def __init__(self, gs, kinds):
        self.ins, self.kinds = list(gs), kinds
        self.out_shapes = [jax.ShapeDtypeStruct(k.got, g.dtype) for k, g in zip(kinds, gs)]
        self.n_sems = len(gs)

    def _copies(self, g_refs, out_refs, send_sems, recv_sems):
        x, y, cc = _place()
        return [pltpu.make_async_remote_copy(
            src_ref=self.kinds[t].grad_half(g_refs[t], 1 - cc), dst_ref=out_refs[t], send_sem=send_sems.at[t],
            recv_sem=recv_sems.at[t], device_id=(x, y, 1 - cc), device_id_type=MESH) for t in range(len(self.ins))]

    def start(self, *refs):
        for cp in self._copies(*refs):
            cp.start()

    def finish(self, *refs):
        for cp in self._copies(*refs):
            cp.wait()

    def results(self, outs):
        return list(outs)


def run_rider(rider, name):
    n, no = len(rider.ins), len(rider.out_shapes)

    def body(*refs):
        parts = (refs[:n], refs[n:n + no], refs[n + no], refs[n + no + 1])
        rider.start(*parts)
        rider.finish(*parts)

    outs = pl.pallas_call(
        body, name=name, in_specs=[_ANY] * n, out_specs=[_ANY] * no, out_shape=rider.out_shapes,
        scratch_shapes=[pltpu.SemaphoreType.DMA((rider.n_sems,)), pltpu.SemaphoreType.DMA((rider.n_sems,))],
    )(*rider.ins)
    return rider.results(outs)


def allgather_devices(buf):
    r, c = buf.shape

    def body(b_ref, out_ref, send_sems, recv_sems, local_sem):
        x, y, cc = _place()
        me = 4 * x + 2 * y + cc
        mine = pltpu.make_async_copy(b_ref, out_ref.at[me], local_sem)
        mine.start()
        copies = []
        for k in range(1, N_DEV):
            px, py, pc = x ^ (k >> 2), y ^ ((k >> 1) & 1), cc ^ (k & 1)
            cp = pltpu.make_async_remote_copy(src_ref=b_ref, dst_ref=out_ref.at[me], send_sem=send_sems.at[k - 1],
                                              recv_sem=recv_sems.at[k - 1], device_id=(px, py, pc), device_id_type=MESH)
            cp.start()
            copies.append((cp, 4 * px + 2 * py + pc))
        for k, (cp, peer) in enumerate(copies):
            pltpu.make_async_remote_copy(src_ref=b_ref, dst_ref=out_ref.at[peer], send_sem=send_sems.at[k],
                                         recv_sem=recv_sems.at[k], device_id=(x, y, cc), device_id_type=MESH).wait_recv()
        for cp, _ in copies:
            cp.wait_send()
        mine.wait()

    return pl.pallas_call(
        body, name="allgather_devices", in_specs=[_ANY], out_specs=_ANY,
        out_shape=jax.ShapeDtypeStruct((N_DEV, r, c), buf.dtype),
        scratch_shapes=[pltpu.SemaphoreType.DMA((N_DEV - 1,)), pltpu.SemaphoreType.DMA((N_DEV - 1,)),
                        pltpu.SemaphoreType.DMA],
    )(buf)


def swap_halves_sibling(gs, kinds, name):
    n = len(gs)

    def body(*refs):
        g_refs, out_refs, send_sems, recv_sems = refs[:n], refs[n:2 * n], refs[2 * n], refs[2 * n + 1]
        x, y, cc = _place()
        cps = []
        for t in range(n):
            cps.append(pltpu.make_async_remote_copy(
                src_ref=kinds[t].grad_half(g_refs[t], 1 - cc), dst_ref=out_refs[t], send_sem=send_sems.at[t],
                recv_sem=recv_sems.at[t], device_id=(x, y, 1 - cc), device_id_type=MESH))
            cps[-1].start()
        for cp in cps:
            cp.wait()

    return pl.pallas_call(
        body, name=name, in_specs=[_ANY] * n, out_specs=[_ANY] * n,
        out_shape=[jax.ShapeDtypeStruct(k.got, g.dtype) for k, g in zip(kinds, gs)],
        scratch_shapes=[pltpu.SemaphoreType.DMA((n,)), pltpu.SemaphoreType.DMA((n,))],
    )(*gs)


def _row_tile(n, limit=512):
    return max(t for t in range(16, limit + 1, 16) if n % t == 0)


def sum_chips(got, own, kind, chip, name):
    def body(chip_ref, got_r, own_r, out_r):
        mine = own_r[...].astype(F32)
        acc = None
        for k in range(N_CHIPS):
            term = jnp.where(chip_ref[0] == k, mine, got_r[k].astype(F32))
            acc = term if acc is None else acc + term
        out_r[...] = acc

    if isinstance(kind, Packed):
        r, c = kind.slab_half
        tr = _row_tile(r)
        grid = (r // tr,)
        specs = [pl.BlockSpec((N_CHIPS, tr, c), lambda i, chip_ref: (0, i, 0)),
                 pl.BlockSpec((None, tr, c), lambda i, chip_ref: (chip_ref[0], i, 0))]
        out_spec = pl.BlockSpec((tr, c), lambda i, chip_ref: (i, 0))
    else:
        r, c = kind.slab_half
        tr = _row_tile(r, 256)
        grid = (r // tr,)
        specs = [pl.BlockSpec((N_CHIPS, tr, c), lambda i, chip_ref: (0, i, 0)),
                 pl.BlockSpec((tr, c), lambda i, chip_ref: (i, chip_ref[0]))]
        out_spec = pl.BlockSpec((tr, c), lambda i, chip_ref: (i, 0))
    return pl.pallas_call(
        body, name=name,
        grid_spec=pltpu.PrefetchScalarGridSpec(num_scalar_prefetch=1, grid=grid, in_specs=specs, out_specs=out_spec),
        out_shape=jax.ShapeDtypeStruct(kind.slab_half, F32),
        compiler_params=_cparams(("parallel",) * len(grid)),
    )(chip, got, own)


def join_halves_sibling(halves):
    n = len(halves)

    def body(*refs):
        h_refs, out_refs, send_sems, recv_sems = refs[:n], refs[n:2 * n], refs[2 * n], refs[2 * n + 1]
        x, y, cc = _place()
        cps = []
        for t in range(n):
            cps.append(pltpu.make_async_remote_copy(
                src_ref=h_refs[t], dst_ref=out_refs[t].at[cc], send_sem=send_sems.at[t], recv_sem=recv_sems.at[t],
                device_id=(x, y, 1 - cc), device_id_type=MESH))
            cps[-1].start()
        for t in range(n):
            pltpu.make_async_remote_copy(
                src_ref=h_refs[t], dst_ref=out_refs[t].at[1 - cc], send_sem=send_sems.at[t], recv_sem=recv_sems.at[t],
                device_id=(x, y, cc), device_id_type=MESH).wait_recv()
        for cp in cps:
            cp.wait_send()

    outs = pl.pallas_call(
        body, name="join_halves_sibling", in_specs=[_ANY] * n, out_specs=[_ANY] * n,
        out_shape=[jax.ShapeDtypeStruct((2,) + h.shape, h.dtype) for h in halves],
        scratch_shapes=[pltpu.SemaphoreType.DMA((n,)), pltpu.SemaphoreType.DMA((n,))],
    )(*halves)
    core = lax.axis_index("c")
    return [lax.dynamic_update_slice_in_dim(o, h[None], core, 0) for o, h in zip(outs, halves)]


def add_own_half(g, got, kind, core, out_dtype, name):
    def body(c_ref, g_r, o_r, out_r):
        out_r[...] = (g_r[...] + o_r[...]).astype(out_r.dtype)

    if isinstance(kind, Packed):
        r, c = kind.slab_half
        tr = _row_tile(r)
        grid = (N_CHIPS, r // tr)
        specs = [pl.BlockSpec((None, None, tr, c), lambda i, j, c_ref: (i, c_ref[0], j, 0)),
                 pl.BlockSpec((None, tr, c), lambda i, j, c_ref: (i, j, 0))]
        out_spec = pl.BlockSpec((None, tr, c), lambda i, j, c_ref: (i, j, 0))
    else:
        h, c4 = kind.got
        tr = _row_tile(h, 128)
        grid = (1, h // tr)
        specs = [pl.BlockSpec((tr, c4), lambda i, j, c_ref: (c_ref[0] * (h // tr) + j, 0)),
                 pl.BlockSpec((tr, c4), lambda i, j, c_ref: (j, 0))]
        out_spec = pl.BlockSpec((tr, c4), lambda i, j, c_ref: (j, 0))
    return pl.pallas_call(
        body, name=name,
        grid_spec=pltpu.PrefetchScalarGridSpec(num_scalar_prefetch=1, grid=grid, in_specs=specs, out_specs=out_spec),
        out_shape=jax.ShapeDtypeStruct(kind.got, out_dtype),
        compiler_params=_cparams(("parallel", "parallel")),
    )(core, g, got)


def sum_slabs(p, name):
    n, r, c = p.shape
    tr = _tile(r, [t for t in (512, 256, 128, 64, 32, 16) if n * t * c * p.dtype.itemsize <= (1 << 23)])

    def body(p_r, out_r):
        acc = p_r[0].astype(F32)
        for k in range(1, n):
            acc = acc + p_r[k].astype(F32)
        out_r[...] = acc

    return pl.pallas_call(
        body, name=name, grid=(r // tr,), in_specs=[pl.BlockSpec((n, tr, c), lambda i: (0, i, 0))],
        out_specs=pl.BlockSpec((tr, c), lambda i: (i, 0)), out_shape=jax.ShapeDtypeStruct((r, c), F32),
        compiler_params=_cparams(("parallel",)),
    )(p)


def _lay(arr, axis, pieces, total, reps=()):
    items = [(d, n, lax.slice_in_dim(arr, s0, s0 + n, axis=axis)) for s0, n, d in pieces]
    items += [(d, n, jnp.repeat(lax.slice_in_dim(arr, s0, s0 + 1, axis=axis), n, axis=axis)) for s0, d, n in reps]
    items.sort(key=lambda t: t[0])
    parts, pos = [], 0

    def zeros(n):
        sh = list(arr.shape)
        sh[axis] = n
        return jnp.zeros(sh, arr.dtype)

    for d, n, v in items:
        if d > pos:
            parts.append(zeros(d - pos))
        parts.append(v)
        pos = d + n
    if total > pos:
        parts.append(zeros(total - pos))
    return jnp.concatenate(parts, axis=axis) if len(parts) > 1 else parts[0]


def _unlay_parts(g, axis, pieces, reps=()):
    out = [(s0, lax.slice_in_dim(g, d, d + n, axis=axis)) for s0, n, d in pieces]
    out += [(s0, jnp.sum(lax.slice_in_dim(g, d, d + n, axis=axis), axis=axis, keepdims=True)) for s0, d, n in reps]
    return out


def _join(parts, axis):
    parts = sorted(parts, key=lambda t: t[0])
    return jnp.concatenate([p for _, p in parts], axis=axis)


def _heads(src0, n_heads, width, padded, dst0=0):
    return [(src0 + h * width, width, dst0 + h * padded) for h in range(n_heads)]


_XQ = lambda src0: _heads(src0, XA_HEADS, XA_HD, LANE)
_XA_W = XA_HEADS * LANE

LAYOUT = {
    'a': dict(
        segs=dict(q=(_heads(0, 4, 96, LANE), 512, ()), k=(_heads(384, 4, 96, LANE), 512, ()),
                  v=(_heads(768, 4, 192, 256), 1024, ()), glr=([(1536, 16, 0)], LANE, ()),
                  og=(_heads(1552, 4, 192, 256), 1024, ()), xq=(_XQ(2320), _XA_W, ())),
        tok=(_heads(0, 4, 192, 256), 1024), xa=(_XQ(768), _XA_W)),
    'b': dict(
        segs=dict(q=([(0, 1536, 0)], 1536, ()), k=([(1536, 1536, 0)], 1536, ()), v=([(3072, 1536, 0)], 1536, ()),
                  xq=(_XQ(4608), _XA_W, ())),
        tok=([(0, 512, 0)], 512), xa=(_XQ(512), _XA_W)),
    'c': dict(
        segs=dict(z=(_heads(0, 12, 64, LANE), 1536, ()),
                  xbc=(_heads(768, 12, 64, LANE) + [(1536, 256, 1536), (1792, 256, 1792)], 2048, ()),
                  dt=([(2048, 12, 0)], LANE, ()),
                  xq=(_XQ(2060), _XA_W, ())),
        tok=(_heads(0, 12, 64, LANE), 1536), xa=(_XQ(768), _XA_W)),
    'd': dict(
        segs=dict(q=([(0, 768, 0)], 768, ()), f=([(768, 768, 0)], 768, ()), i=([(1536, 768, 0)], 768, ()),
                  og=([(2304, 768, 0)], 768, ()), xq=(_XQ(3072), _XA_W, ())),
        tok=([(0, 768, 0)], 768), xa=(_XQ(768), _XA_W)),
}
KINDS = 'abcd'
_XS_PIECES = _heads(0, 12, 64, LANE)
_XBC_PIECES = _XS_PIECES + [(768, 256, 1536), (1024, 256, 1792)]
_HEAD_REPS = tuple((h, h * LANE, LANE) for h in range(12))


def _row(v):
    return v.reshape(1, -1)


LAYER_WEIGHTS = [
    {'w_in': (f'{k}_w_in', None), 'w_out': (f'{k}_w_out', None), 'w_kv': ('xa_w_kv', i), 'w_up': ('ffn_w_up', i),
     'w_down': ('ffn_w_down', i), **({'w_gate2': ('a_w_gate2', None)} if k == 'a' else {})}
    for i, k in enumerate('abcd')]


class LocalLayers:
    def __init__(self, W):
        self.W, self.g = W, {}

    def weights(self, i):
        return {key: (self.W[n] if l is None else self.W[n][l]).astype(MXU_DTYPE)
                for key, (n, l) in LAYER_WEIGHTS[i].items()}

    def fwd_rider(self, i):
        return None

    def bwd_rider(self, i):
        return None

    def grads_rider(self, i, g):
        self.g[i] = g
        return None

    def whole_grads(self):
        out = {}
        for i in range(4):
            for key, (n, l) in LAYER_WEIGHTS[i].items():
                if l is None:
                    out[n] = self.g[i][key]
        for n in ('xa_w_kv', 'ffn_w_up', 'ffn_w_down'):
            key = [k for k, (m, _) in LAYER_WEIGHTS[0].items() if m == n][0]
            out[n] = jnp.stack([self.g[i][key] for i in range(4)])
        return out


class ShardedLayers:
    def __init__(self, w, core_id):
        self.core_id = core_id
        self.names, self.axes, self.shards, self.packed, self.kinds = [], [], [], [], []
        for lw in LAYER_WEIGHTS:
            keys = [k for k in lw if k not in ('w_up', 'w_down')]
            sh = {k: (w[lw[k][0]] if lw[k][1] is None else w[lw[k][0]][lw[k][1]]).astype(MXU_DTYPE) for k in lw}
            ax = {k: SHARD_AXIS[lw[k][0]] - (lw[k][1] is not None) for k in lw}
            pk = _pack([sh[k] for k in keys], MXU_DTYPE, 256)
            self.names.append(keys)
            self.axes.append(ax)
            self.shards.append(sh)
            self.packed.append(pk)
            self.kinds.append([Packed(pk.shape), SlabCols(sh['w_up'].shape), Packed(sh['w_down'].shape)])
        self.whole = {}
        self.pending = None
        self.recvd = {}

    def _operands(self, i):
        return [self.packed[i], self.shards[i]['w_up'], self.shards[i]['w_down']]

    def _gathered(self, i, res):
        per_chip = [_unpack(res[0][j], [self.shards[i][k].shape for k in self.names[i]]) for j in range(N_CHIPS)]
        out = {k: _merge_chips(jnp.stack([per_chip[j][n] for j in range(N_CHIPS)]), self.axes[i][k])
               for n, k in enumerate(self.names[i])}
        out['w_up'], out['w_down'] = res[1], res[2].reshape(-1, res[2].shape[-1])
        self.whole[i] = out

    def first_gather(self):
        self._gathered(0, run_rider(GatherRider(self._operands(0), self.kinds[0]), "allgather_chips"))

    def weights(self, i):
        return self.whole[i]

    def fwd_rider(self, i):
        return GatherRider(self._operands(i + 1), self.kinds[i + 1]) if i + 1 < 4 else None

    def fwd_rode(self, i, res):
        self._gathered(i + 1, res)

    def bwd_rider(self, i):
        return ExchangeRider(self.pending[1], self.kinds[self.pending[0]]) if self.pending is not None else None

    def bwd_rode(self, i, res):
        self.recvd[self.pending[0]] = (res, self.pending[1])
        self.pending = None

    def grads_rider(self, i, g):
        kinds = self.kinds[i]
        gb = jnp.stack([_pack([_split_chips(g[k], self.axes[i][k])[j] for k in self.names[i]], F32, 256)
                        for j in range(N_CHIPS)])
        self.swapping = [gb.reshape(N_CHIPS, 2, kinds[0].h, kinds[0].c), g['w_up'],
                         g['w_down'].reshape(N_CHIPS, 2, kinds[2].h, kinds[2].c)]
        return SwapRider(self.swapping, kinds)

    def grads_rode(self, i, gots):
        self.pending = (i, [add_own_half(a, o, k, self.core_id, GRAD_WIRE_DTYPE, f"add_own_half_{i}_{t}")
                            for t, (a, o, k) in enumerate(zip(self.swapping, gots, self.kinds[i]))])

    def finish(self, chip_id):
        last, pairs = self.pending
        self.recvd[last] = (run_rider(ExchangeRider(pairs, self.kinds[last]), "exchange_chips"), pairs)
        halves = []
        for i in range(4):
            got, pairs = self.recvd[i]
            halves += [sum_chips(r, p, k, chip_id, f"sum_chips_{i}_{t}")
                       for t, (r, p, k) in enumerate(zip(got, pairs, self.kinds[i]))]
        joined = join_halves_sibling(halves)
        out, stacked = {}, {'xa_w_kv': [], 'ffn_w_up': [], 'ffn_w_down': []}
        for i, lw in enumerate(LAYER_WEIGHTS):
            red, up, down = joined[3 * i:3 * i + 3]
            parts = _unpack(red.reshape(-1, PACK_COLS), [self.shards[i][k].shape for k in self.names[i]])
            parts = dict(zip(self.names[i], parts), w_up=up.reshape(self.shards[i]['w_up'].shape),
                         w_down=down.reshape(self.shards[i]['w_down'].shape))
            for k, (n, l) in lw.items():
                if l is None:
                    out[n] = parts[k]
                else:
                    stacked[n].append(parts[k])
        out.update({n: jnp.stack(v) for n, v in stacked.items()})
        return out


def local_step(x, mem, positions, target, W, layers=None):
    s = x.shape[0]
    grads = {}
    scan_block = CHUNK * SCAN_CHUNKS
    ffn = layers or LocalLayers(W)

    inv_freq = ROPE_THETA ** (-jnp.arange(DIL_HD // 2, dtype=F32) / (DIL_HD // 2))
    ang = positions.astype(F32)[:, None] * inv_freq
    cosf = jnp.concatenate([jnp.cos(ang), jnp.cos(ang)], axis=-1)
    sinf = jnp.concatenate([-jnp.sin(ang), jnp.sin(ang)], axis=-1)

    mem_g = _row(W['mem_norm'])
    (mem_n,) = tmap("mem_norm", _norm_stage, [mem], [mem_g], [(D_MODEL, MXU_DTYPE)])
    kv_lay = _heads(0, 4, 64, LANE) + _heads(256, 4, 64, LANE, dst0=_XA_W)

    saved = []
    for i in range(4):
        kind = KINDS[i]
        lay = LAYOUT[kind]
        sv = dict(x0=x)
        wl = ffn.weights(i)
        w_in, w_out = wl['w_in'], wl['w_out']
        sv['w_seg'] = {n: _lay(w_in, 1, p, t, r).astype(MXU_DTYPE) for n, (p, t, r) in lay['segs'].items()}
        sv['wo_tok'] = _lay(w_out, 0, *lay['tok']).astype(MXU_DTYPE)
        sv['wo_xa'] = _lay(w_out, 0, *lay['xa']).astype(MXU_DTYPE)
        sv['w_kv'] = _lay(wl['w_kv'], 1, kv_lay, 2 * _XA_W).astype(MXU_DTYPE)
        sv['g1'] = _row(W['mix_norm'][i])
        (h,) = tmap(f"mix_norm_{i}", _norm_stage, [x], [sv['g1']], [(D_MODEL, MXU_DTYPE)])
        sv['h'] = h
        seg = dict(zip(sv['w_seg'], matmul_multi(h, list(sv['w_seg'].values()))))
        sv['seg'] = seg

        if kind == 'a':
            sv['w2'] = _lay(_lay(wl['w_gate2'], 1, _heads(0, 4, 96, LANE), 512), 0, [(0, 16, 0)], LANE)
            sv['bg'] = _row(_lay(W['a_b_gate'], 0, _heads(0, 4, 96, LANE), 512))
            sv['on'] = _row(_lay(W['a_o_norm'], 0, [(0, 192, 0)], 256))
            (la,) = tmap("gla_pre", _gla_pre, [seg['glr']], [sv['w2'], sv['bg']], [(512, F32)])
            sv['la'] = la
            sv['scan_fn'] = _gla_step(GLA_HEADS, 2 * LANE, GLA_DK ** -0.5)
            sv['scan_rows'] = [seg['q'], seg['k'], seg['v'], la]
            (o,), sv['states'] = rscan("gla_scan", sv['scan_fn'], [(LANE, 2 * LANE)] * GLA_HEADS, sv['scan_rows'], [],
                                       [(1024, F32)], scan_block)
            sv['o'] = o
            (tok,) = tmap("gla_post", _gla_post, [o, seg['og']], [sv['on']], [(1024, MXU_DTYPE)])
        elif kind == 'b':
            sv['qg'], sv['kg'] = _row(W['b_q_norm']), _row(W['b_k_norm'])
            os_, ls_ = [], []
            qkn = tmap("dil_pre", _dil_pre, [seg['q'], seg['k'], cosf, sinf], [sv['qg'], sv['kg']], [(512, F32)] * 6)
            sv['qn'], sv['kn'] = qkn[:3], qkn[3:]
            for g, (window, r) in enumerate(DIL_GROUPS):
                assert window // r == DIL_BLOCK and (s // r) % DIL_BLOCK == 0
                o, lse = dil_attn(f"dil_attn_{g}", sv['qn'][g], sv['kn'][g], seg['v'], r, g)
                os_.append(o)
                ls_.append(lse)
            sv['os'], sv['ls'] = os_, ls_
            (tok,) = tmap("dil_merge", _dil_merge, os_ + ls_, [], [(512, MXU_DTYPE)])
        elif kind == 'c':
            sv['cw'] = _lay(W['c_conv_w'], 1, _XBC_PIECES, 2048)
            sv['cb'] = _row(_lay(W['c_conv_b'], 0, _XBC_PIECES, 2048))
            sv['dtb'] = _row(_lay(W['c_dt_bias'], 0, [], 1536, _HEAD_REPS))
            sv['alog'] = _row(_lay(W['c_a_log'], 0, [], 1536, _HEAD_REPS))
            sv['dsk'] = _row(_lay(W['c_d'], 0, [], 1536, _HEAD_REPS))
            sv['cn'] = _row(_lay(W['c_norm'], 0, _XS_PIECES, 1536))
            xact = conv_fwd("ssm_conv", seg['xbc'], sv['cw'], sv['cb'], 'silu', F32, 512)
            sv['xact'] = xact
            sv['scan_rows'] = [xact, seg['dt']]
            sv['scan_params'] = [sv['dtb'], sv['alog'], sv['dsk']]
            (yv,), sv['states'] = rscan("ssd_scan", _ssd_step, [(LANE, LANE)] * SSM_HEADS, sv['scan_rows'],
                                        sv['scan_params'], [(1536, F32)], scan_block)
            sv['y'] = yv
            (tok,) = tmap("ssd_post", _mamba_post, [yv, seg['z']], [sv['cn']], [(1536, MXU_DTYPE)])
        else:
            sv['lbnd'] = W['d_lower_bounds']
            sv['on'] = _row(W['d_o_norm'])
            qq, kk, la = tmap("hgrn_pre", _hgrn_pre, [seg['q'], seg['f']], [sv['lbnd']], [(768, F32)] * 3)
            sv['scan_fn'] = _gla_step(HGRN_HEADS, LANE, 1.0)
            sv['scan_rows'] = [qq, kk, seg['i'], la]
            (o,), sv['states'] = rscan("hgrn_scan", sv['scan_fn'], [(LANE, LANE)] * HGRN_HEADS, sv['scan_rows'], [],
                                       [(768, F32)], scan_block)
            sv['o'] = o
            (tok,) = tmap("hgrn_post", _hgrn_post, [o, seg['og']], [sv['on']], [(768, MXU_DTYPE)])
        sv['tok'] = tok

        kv = matmul(mem_n, sv['w_kv'])
        sv['kv'] = kv
        sv['xqg'] = _row(_lay(W['xa_q_norm'][i], 0, [(0, 64, 0)], LANE))
        sv['xkg'] = _row(_lay(W['xa_k_norm'][i], 0, [(0, 64, 0)], LANE))
        (xa,) = tmap(f"xattn_{i}", _xattn, [seg['xq']], [kv, sv['xqg'], sv['xkg']], [(_XA_W, MXU_DTYPE)])
        sv['xa'] = xa
        x = matmul_sum([tok, xa], [sv['wo_tok'], sv['wo_xa']], False, add=x)
        sv['x1'] = x

        sv['g2'] = _row(W['ffn_norm'][i])
        sv['fcw'] = W['ffn_conv_w'][i]
        sv['fcb'] = _row(W['ffn_conv_b'][i])
        (h2,) = tmap(f"ffn_norm_{i}", _norm_stage, [x], [sv['g2']], [(D_MODEL, MXU_DTYPE)])
        sv['h2'] = h2
        w_up, w_down = wl['w_up'], wl['w_down']
        sv['w_up'], sv['w_down'] = w_up, w_down
        u0 = matmul(h2, w_up)
        sv['u0'] = u0
        rider = ffn.fwd_rider(i)
        act = conv_fwd("ffn_conv", u0, sv['fcw'], sv['fcb'], 'glu', MXU_DTYPE, 1408, rider=rider)
        if rider is not None:
            act, rode = act
            ffn.fwd_rode(i, rode)
        sv['act'] = act
        x = matmul(act, w_down, add=x)
        saved.append(sv)

    loss_acc, dx = loss_head(x, target)

    g_stack = {n: [None] * 4 for n in ('mix_norm', 'xa_q_norm', 'xa_k_norm', 'ffn_norm', 'ffn_conv_w', 'ffn_conv_b')}
    d_memn = None
    for i in reversed(range(4)):
        kind = KINDS[i]
        lay = LAYOUT[kind]
        sv = saved[i]
        seg = sv['seg']
        w_up, w_down = sv['w_up'], sv['w_down']
        gl = {}
        dact = matmul(dx, w_down, tb=True)
        gl['w_down'] = matmul(sv['act'], dx, ta=True)
        rider = ffn.bwd_rider(i)
        (du_g, du_v), dcw, dcb, rode = conv_bwd("ffn_conv_bwd", sv['u0'], sv['fcw'], sv['fcb'], dact, 'glu', 1408,
                                                rider=rider)
        if rider is not None:
            ffn.bwd_rode(i, rode)
        g_stack['ffn_conv_w'][i], g_stack['ffn_conv_b'][i] = dcw, dcb[0]
        dh2 = matmul_sum([du_g, du_v], [w_up, w_up], True, b_cols=[0, 1])
        g_up = jnp.zeros((1,) + w_up.shape, F32)
        g_up = matmul(sv['h2'], du_g, ta=True, into=(g_up, 0, 0))
        g_up = matmul(sv['h2'], du_v, ta=True, into=(g_up, 0, D_FF))
        gl['w_up'] = g_up[0]
        (dx,), (dg2,) = tmap_bwd(f"ffn_norm_bwd_{i}", _norm_stage, [sv['x1']], [sv['g2']], [dh2], [True], {0: dx})
        g_stack['ffn_norm'][i] = dg2[0]
        dtok = matmul(dx, sv['wo_tok'], tb=True)
        dxa = matmul(dx, sv['wo_xa'], tb=True)
        g_wo = _unlay_parts(matmul(sv['tok'], dx, ta=True), 0, lay['tok'][0]) \
            + _unlay_parts(matmul(sv['xa'], dx, ta=True), 0, lay['xa'][0])
        gl['w_out'] = _join(g_wo, 0)
        (dxq,), (dkv, dqg, dkg) = tmap_bwd(f"xattn_bwd_{i}", _xattn, [seg['xq']], [sv['kv'], sv['xqg'], sv['xkg']],
                                           [dxa], [True])
        g_stack['xa_q_norm'][i], g_stack['xa_k_norm'][i] = dqg[0, :XA_HD], dkg[0, :XA_HD]
        gl['w_kv'] = _join(_unlay_parts(matmul(mem_n, dkv, ta=True), 1, kv_lay), 1)
        d_memn = matmul(dkv, sv['w_kv'], tb=True, add=d_memn)
        dseg = dict(xq=dxq)
        if kind == 'a':
            (do, dog), (don,) = tmap_bwd("gla_post_bwd", _gla_post, [sv['o'], seg['og']], [sv['on']], [dtok],
                                         [True, True], grad_dtype=F32)
            grads['a_o_norm'] = don[0, :GLA_DV]
            (dq, dk, dv, dla), _ = rscan_bwd("gla_scan_bwd", sv['scan_fn'], sv['states'], sv['scan_rows'], [], [do],
                                             scan_block, grad_dtype=F32)
            (dglr,), (dw2, dbg) = tmap_bwd("gla_pre_bwd", _gla_pre, [seg['glr']], [sv['w2'], sv['bg']], [dla], [True])
            gl['w_gate2'] = _join(_unlay_parts(dw2[:GLA_RANK], 1, _heads(0, 4, 96, LANE)), 1)
            grads['a_b_gate'] = _join(_unlay_parts(dbg[0], 0, _heads(0, 4, 96, LANE)), 0)
            dseg.update(q=dq, k=dk, v=dv, glr=dglr, og=dog)
        elif kind == 'b':
            res, _ = tmap_bwd("dil_merge_bwd", _dil_merge, sv['os'] + sv['ls'], [], [dtok], [True] * 6, grad_dtype=F32)
            dqn, dkn, dvs = [], [], []
            for g, (_, r) in enumerate(DIL_GROUPS):
                a_, b_, c_ = dil_attn_bwd(f"dil_attn_bwd_{g}", sv['qn'][g], sv['kn'][g], seg['v'], res[g], res[3 + g],
                                          r, g)
                dqn.append(a_)
                dkn.append(b_)
                dvs.append(c_)
            (dq, dk), (dqg, dkg) = tmap_bwd("dil_pre_bwd", _dil_pre, [seg['q'], seg['k'], cosf, sinf],
                                            [sv['qg'], sv['kg']], dqn + dkn, [True, True, False, False])
            dseg.update(q=dq, k=dk, v=jnp.concatenate(dvs, axis=1))
            grads['b_q_norm'], grads['b_k_norm'] = dqg[0], dkg[0]
        elif kind == 'c':
            (dy, dz), (dcn,) = tmap_bwd("ssd_post_bwd", _mamba_post, [sv['y'], seg['z']], [sv['cn']], [dtok],
                                        [True, True], grad_dtype=F32)
            grads['c_norm'] = _join(_unlay_parts(dcn[0], 0, _XS_PIECES), 0)
            (dxact, ddt), (ddtb, dalog, ddsk) = rscan_bwd("ssd_scan_bwd", _ssd_step, sv['states'], sv['scan_rows'],
                                                          sv['scan_params'], [dy], scan_block, grad_dtype=F32)
            for nm, gv in (('c_dt_bias', ddtb), ('c_a_log', dalog), ('c_d', ddsk)):
                grads[nm] = _join(_unlay_parts(gv[0], 0, [], _HEAD_REPS), 0)
            (dxbc,), dcw, dcb, _ = conv_bwd("ssm_conv_bwd", seg['xbc'], sv['cw'], sv['cb'], dxact, 'silu', 512)
            grads['c_conv_w'] = _join(_unlay_parts(dcw, 1, _XBC_PIECES), 1)
            grads['c_conv_b'] = _join(_unlay_parts(dcb[0], 0, _XBC_PIECES), 0)
            dseg.update(z=dz, xbc=dxbc, dt=ddt)
        else:
            (do, dog), (don,) = tmap_bwd("hgrn_post_bwd", _hgrn_post, [sv['o'], seg['og']], [sv['on']], [dtok],
                                         [True, True], grad_dtype=F32)
            grads['d_o_norm'] = don[0]
            (dqq, dkk, di, dla), _ = rscan_bwd("hgrn_scan_bwd", sv['scan_fn'], sv['states'], sv['scan_rows'], [], [do],
                                               scan_block, grad_dtype=F32)
            (dq, df), (dlb,) = tmap_bwd("hgrn_pre_bwd", _hgrn_pre, [seg['q'], seg['f']], [sv['lbnd']], [dqq, dkk, dla],
                                        [True, True])
            grads['d_lower_bounds'] = dlb
            dseg.update(q=dq, f=df, i=di, og=dog)
        names = list(lay['segs'])
        dh = matmul_sum([dseg[n] for n in names], [sv['w_seg'][n] for n in names], True)
        g_in = []
        for n, (p, t, rp) in lay['segs'].items():
            g_in += _unlay_parts(matmul(sv['h'], dseg[n], ta=True), 1, p, rp)
        gl['w_in'] = _join(g_in, 1)
        rider = ffn.grads_rider(i, gl)
        (dx,), (dg1,), *rode = tmap_bwd(f"mix_norm_bwd_{i}", _norm_stage, [sv['x0']], [sv['g1']], [dh], [True], {0: dx},
                                        rider=rider)
        if rider is not None:
            ffn.grads_rode(i, rode[0])
        g_stack['mix_norm'][i] = dg1[0]

    _, (dmg,) = tmap_bwd("mem_norm_bwd", _norm_stage, [mem], [mem_g], [d_memn], [False])
    grads['mem_norm'] = dmg[0]
    for n, parts in g_stack.items():
        grads[n] = jnp.stack(parts)
    if isinstance(ffn, LocalLayers):
        grads.update(ffn.whole_grads())
    return loss_acc, dx, grads


def _pack(arrs, dtype, row_multiple=PACK_ROWS):
    parts, rows = [], 0
    for a in arrs:
        f = a.reshape(-1).astype(dtype)
        unit = PACK_ROWS * PACK_COLS
        pad = (-f.shape[0]) % unit
        if pad:
            f = jnp.concatenate([f, jnp.zeros((pad,), dtype)])
        parts.append(f.reshape(-1, PACK_COLS))
        rows += parts[-1].shape[0]
    if rows % row_multiple:
        parts.append(jnp.zeros((row_multiple - rows % row_multiple, PACK_COLS), dtype))
    return jnp.concatenate(parts, axis=0)


def _unpack(buf, shapes):
    out, row = [], 0
    for sh in shapes:
        n = int(np.prod(sh))
        rows = -(-n // (PACK_ROWS * PACK_COLS)) * PACK_ROWS
        out.append(buf[row:row + rows].reshape(-1)[:n].reshape(sh))
        row += rows
    return out


def _pack_rows(arrs):
    parts = []
    for a in arrs:
        f = a.reshape(-1).astype(F32)
        parts.append(jnp.pad(f, (0, (-f.shape[0]) % PACK_COLS)))
    flat = jnp.concatenate(parts)
    rows = flat.shape[0] // PACK_COLS
    return jnp.pad(flat, (0, (-rows % 16) * PACK_COLS)).reshape(-1, PACK_COLS)


def _unpack_rows(buf, shapes):
    flat, out, pos = buf.reshape(-1), [], 0
    for sh in shapes:
        n = int(np.prod(sh))
        out.append(flat[pos:pos + n].reshape(sh))
        pos += -(-n // PACK_COLS) * PACK_COLS
    return out


def _split_chips(a, axis):
    sh = a.shape
    return jnp.moveaxis(a.reshape(sh[:axis] + (N_CHIPS, sh[axis] // N_CHIPS) + sh[axis + 1:]), axis, 0)


def _merge_chips(a, axis):
    a = jnp.moveaxis(a, 0, axis)
    sh = a.shape
    return a.reshape(sh[:axis] + (sh[axis] * sh[axis + 1],) + sh[axis + 2:])


def kernel(x, mem, positions, mem_norm, mix_norm, xa_w_kv, xa_q_norm, xa_k_norm, ffn_norm, ffn_w_up, ffn_conv_w, ffn_conv_b, ffn_w_down, a_w_in, a_w_gate2, a_b_gate, a_o_norm, a_w_out, b_w_in, b_q_norm, b_k_norm, b_w_out, c_w_in, c_conv_w, c_conv_b, c_dt_bias, c_a_log, c_d, c_norm, c_w_out, d_w_in, d_lower_bounds, d_o_norm, d_w_out, loss_target, m_mem_norm, m_mix_norm, m_xa_w_kv, m_xa_q_norm, m_xa_k_norm, m_ffn_norm, m_ffn_w_up, m_ffn_conv_w, m_ffn_conv_b, m_ffn_w_down, m_a_w_in, m_a_w_gate2, m_a_b_gate, m_a_o_norm, m_a_w_out, m_b_w_in, m_b_q_norm, m_b_k_norm, m_b_w_out, m_c_w_in, m_c_conv_w, m_c_conv_b, m_c_dt_bias, m_c_a_log, m_c_d, m_c_norm, m_c_w_out, m_d_w_in, m_d_lower_bounds, m_d_o_norm, m_d_w_out, v_mem_norm, v_mix_norm, v_xa_w_kv, v_xa_q_norm, v_xa_k_norm, v_ffn_norm, v_ffn_w_up, v_ffn_conv_w, v_ffn_conv_b, v_ffn_w_down, v_a_w_in, v_a_w_gate2, v_a_b_gate, v_a_o_norm, v_a_w_out, v_b_w_in, v_b_q_norm, v_b_k_norm, v_b_w_out, v_c_w_in, v_c_conv_w, v_c_conv_b, v_c_dt_bias, v_c_a_log, v_c_d, v_c_norm, v_c_w_out, v_d_w_in, v_d_lower_bounds, v_d_o_norm, v_d_w_out):
    args = locals()
    w = {n: args[n] for n in WEIGHTS}
    m = {n: args['m_' + n] for n in WEIGHTS}
    v = {n: args['v_' + n] for n in WEIGHTS}
    cx, cy, cc = lax.axis_index("x"), lax.axis_index("y"), lax.axis_index("c")
    chip = 2 * cx + cy

    core_id, chip_id = cc.reshape(1).astype(jnp.int32), chip.reshape(1).astype(jnp.int32)
    layers = ShardedLayers(w, core_id)
    layers.first_gather()
    full = {}
    small_sharded = [n for n in SMALL if n in SHARD_AXIS]
    sg = allgather_devices(_pack([w[n] for n in small_sharded], F32))
    per_chip_s = [_unpack(sg[2 * j], [w[n].shape for n in small_sharded]) for j in range(N_CHIPS)]
    for k, n in enumerate(small_sharded):
        full[n] = _merge_chips(jnp.stack([per_chip_s[j][k] for j in range(N_CHIPS)]), SHARD_AXIS[n])
    for n in SMALL:
        if n not in SHARD_AXIS:
            full[n] = w[n]

    loss_acc, dx, grads = local_step(x[0], mem[0], positions[0], loss_target[0], full, layers)
    loss = lax.psum(jnp.sum(loss_acc), ("x", "y", "c"))

    g_big = layers.finish(chip_id)

    small_full_shapes = [grads[n].shape for n in SMALL]
    gs = sum_slabs(allgather_devices(_pack_rows([grads[n] for n in SMALL])), "sum_devices")
    g_small = {}
    for n, gfull in zip(SMALL, _unpack_rows(gs, small_full_shapes)):
        if n in SHARD_AXIS:
            ax = SHARD_AXIS[n]
            size = gfull.shape[ax] // N_CHIPS
            gfull = lax.dynamic_slice_in_dim(gfull, chip * size, size, axis=ax)
        g_small[n] = gfull

    g_out, delta, new_m, new_v = {**g_big, **g_small}, {}, {}, {}
    for n in WEIGHTS:
        sh = w[n].shape
        two_d = (-1, sh[-1])
        d_, m_, v_ = adamw(f"adamw_{n}", w[n].reshape(two_d), g_out[n].reshape(two_d), m[n].reshape(two_d),
                           v[n].reshape(two_d))
        delta[n], new_m[n], new_v[n] = d_.reshape(sh), m_.reshape(sh), v_.reshape(sh)

    return (loss, dx[None], *[g_out[n] for n in WEIGHTS], *[delta[n] for n in WEIGHTS],
            *[new_m[n] for n in WEIGHTS], *[new_v[n] for n in WEIGHTS])
```

```python
import functools
import math

import jax
import jax.numpy as jnp
import numpy as np
from jax import lax
from jax.experimental import pallas as pl
from jax.experimental.pallas import tpu as pltpu

F32 = jnp.float32
MXU_DTYPE = jnp.bfloat16
GRAD_WIRE_DTYPE = jnp.bfloat16
VMEM_LIMIT_V7X = 56 * 1024 * 1024
LANE = 128
SUBLANE = 8

D_MODEL = 1024
N_MEM = 256
EPS = 1e-6
ROPE_THETA = 10000.0
CHUNK = 64
XA_HEADS, XA_HD = 4, 64
GLA_HEADS, GLA_DK, GLA_DV, GLA_RANK, GLA_GATE_NORM = 4, 96, 192, 16, 16.0
DIL_GROUPS = ((128, 1), (512, 4), (2048, 16))
DIL_HEADS, DIL_HD, DIL_BLOCK = 4, 128, 128
SSM_HD, SSM_HEADS, SSM_GROUPS, SSM_STATE, SSM_CONV = 64, 12, 2, 128, 4
HGRN_HEADS, HGRN_DK = 6, 128
D_FF = 2816
FFN_CONV = 3
ADAM_LR, ADAM_B1, ADAM_B2, ADAM_EPS, ADAM_WD, ADAM_STEP = 0.001, 0.9, 0.999, 1e-08, 0.01, 10

MM_TILES = (2816, 1408, 1024, 768, 512, 384, 256, 128)
MM_K_TILES = (2816, 2048, 1536, 1408, 1024, 768, 512, 384, 256, 128)
MM_MIN_OUT_TILE = 512 * 1024
MM_VMEM_BUDGET = 40 * 1024 * 1024
ROW_BLOCK = 512
CONV_BLOCK = 256
SCAN_CHUNKS = 2
PACK_COLS = 1024
PACK_ROWS = 32

WEIGHTS = ['mem_norm', 'mix_norm', 'xa_w_kv', 'xa_q_norm', 'xa_k_norm', 'ffn_norm', 'ffn_w_up', 'ffn_conv_w',
           'ffn_conv_b', 'ffn_w_down', 'a_w_in', 'a_w_gate2', 'a_b_gate', 'a_o_norm', 'a_w_out', 'b_w_in', 'b_q_norm',
           'b_k_norm', 'b_w_out', 'c_w_in', 'c_conv_w', 'c_conv_b', 'c_dt_bias', 'c_a_log', 'c_d', 'c_norm', 'c_w_out',
           'd_w_in', 'd_lower_bounds', 'd_o_norm', 'd_w_out']
SHARD_AXIS = {'xa_w_kv': 1, 'ffn_w_up': 2, 'ffn_conv_w': 2, 'ffn_w_down': 1, 'a_w_in': 1, 'a_w_gate2': 1, 'a_w_out': 0,
              'b_w_in': 1, 'b_w_out': 1, 'c_w_in': 1, 'c_conv_w': 1, 'c_w_out': 0, 'd_w_in': 1, 'd_w_out': 0}
BIG = ['xa_w_kv', 'ffn_w_up', 'ffn_w_down', 'a_w_in', 'a_w_gate2', 'a_w_out', 'b_w_in', 'b_w_out', 'c_w_in', 'c_w_out',
       'd_w_in', 'd_w_out']
SMALL = [n for n in WEIGHTS if n not in BIG]
LAYERED = ['ffn_w_up', 'ffn_w_down']
N_CHIPS = 4
N_DEV = 8


class _MatmulSet:
    def __init__(self, cast, precision):
        def dot(a, b, dims):
            if cast:
                a = a.astype(MXU_DTYPE)
                b = b.astype(MXU_DTYPE)
            return lax.dot_general(a, b, (dims, ((), ())), precision=precision, preferred_element_type=F32)

        @jax.custom_vjp
        def nn(a, b):
            return dot(a, b, ((1,), (0,)))

        @jax.custom_vjp
        def nt(a, b):
            return dot(a, b, ((1,), (1,)))

        @jax.custom_vjp
        def tn(a, b):
            return dot(a, b, ((0,), (0,)))

        nn.defvjp(lambda a, b: (nn(a, b), (a, b)), lambda r, g: (nt(g, r[1]), tn(r[0], g)))
        nt.defvjp(lambda a, b: (nt(a, b), (a, b)), lambda r, g: (nn(g, r[1]), tn(g, r[0])))
        tn.defvjp(lambda a, b: (tn(a, b), (a, b)), lambda r, g: (nt(r[1], g), nn(r[0], g)))
        self.nn, self.nt, self.tn = nn, nt, tn


mm = _MatmulSet(True, None)
hi = _MatmulSet(False, lax.Precision.HIGHEST)


def _sigmoid(x):
    return jax.nn.sigmoid(x)


def _silu(x):
    return x * jax.nn.sigmoid(x)


def _softplus(x):
    return jnp.maximum(x, 0.0) + jnp.log1p(jnp.exp(-jnp.abs(x)))


def _rms(x, g, n_real=None):
    n = n_real or x.shape[-1]
    ms = jnp.sum(x * x, axis=-1, keepdims=True) * (1.0 / n)
    return x * lax.rsqrt(ms + EPS) * g


@jax.custom_vjp
def _swap_halves(x):
    return pltpu.roll(x, 64, 1)


_swap_halves.defvjp(lambda x: (_swap_halves(x), None), lambda _, g: (_swap_halves(g),))


def _tile(n, cands):
    for c in cands:
        if n % c == 0:
            return c
    raise ValueError(f"no tile for {n} among {cands}")


def _cparams(sem):
    return pltpu.CompilerParams(dimension_semantics=sem, vmem_limit_bytes=VMEM_LIMIT_V7X)


def _f32(v):
    return v.astype(F32) if jnp.issubdtype(v.dtype, jnp.floating) else v


def matmul(a, b, *, ta=False, tb=False, add=None, out_dtype=F32, b_layer=None, b_koff=0, into=None):
    m, k = (a.shape[1], a.shape[0]) if ta else a.shape
    b2 = b.shape[1:] if b_layer is not None else b.shape
    n = b2[0] if tb else b2[1]
    assert b_koff + k <= (b2[1] if tb else b2[0]), (a.shape, b.shape, ta, tb, b_koff)
    sa, sb, so = a.dtype.itemsize, b.dtype.itemsize, jnp.dtype(out_dtype).itemsize
    n_align = math.gcd(n, into[2]) if into is not None and into[2] else n
    k_align = math.gcd(k, b_koff) if b_koff else k

    def vmem(tm_, tn_, tk_):
        return (2 * tm_ * tk_ * sa + 2 * tk_ * tn_ * sb + 2 * tm_ * tn_ * so + (tm_ * tn_ * 4 if tk_ < k else 0)
                + (2 * tm_ * tn_ * add.dtype.itemsize if add is not None else 0))

    for tk in [t for t in MM_K_TILES if k_align % t == 0]:
        fits = [(tm_ * tn_, tm_, tn_) for tm_ in MM_TILES if m % tm_ == 0 for tn_ in MM_TILES
                if n % tn_ == 0 and n_align % tn_ == 0 and vmem(tm_, tn_, tk) <= MM_VMEM_BUDGET]
        if fits and (max(fits)[0] >= min(MM_MIN_OUT_TILE, m * n) or tk == MM_K_TILES[-1]):
            break
    _, tm, tn = max(fits)
    nk = k // tk
    dims = (((0,) if ta else (1,)), ((1,) if tb else (0,)))
    n_extra = (add is not None) + (into is not None)

    def body(*refs):
        a_ref, b_ref = refs[0], refs[1]
        add_ref = refs[2] if add is not None else None
        o_ref = refs[2 + n_extra]
        part = lax.dot_general(a_ref[...].astype(MXU_DTYPE), b_ref[...].astype(MXU_DTYPE), (dims, ((), ())),
                               preferred_element_type=F32)

        def finish(r):
            if add_ref is not None:
                r = r + add_ref[...].astype(F32)
            o_ref[...] = r.astype(o_ref.dtype)

        if nk == 1:
            finish(part)
            return
        acc = refs[-1]
        kk = pl.program_id(2)

        @pl.when(kk == 0)
        def _():
            acc[...] = part

        @pl.when(kk > 0)
        def _():
            acc[...] += part

        @pl.when(kk == nk - 1)
        def _():
            finish(acc[...])

    a_spec = pl.BlockSpec((tk, tm), lambda i, j, q: (q, i)) if ta else pl.BlockSpec((tm, tk), lambda i, j, q: (i, q))
    ko = b_koff // tk
    if b_layer is None:
        b_spec = (pl.BlockSpec((tn, tk), lambda i, j, q: (j, q + ko)) if tb
                  else pl.BlockSpec((tk, tn), lambda i, j, q: (q + ko, j)))
    else:
        b_spec = (pl.BlockSpec((None, tn, tk), lambda i, j, q: (b_layer, j, q + ko)) if tb
                  else pl.BlockSpec((None, tk, tn), lambda i, j, q: (b_layer, q + ko, j)))
    o_spec = pl.BlockSpec((tm, tn), lambda i, j, q: (i, j))
    ins, specs = [a, b], [a_spec, b_spec]
    if add is not None:
        ins.append(add)
        specs.append(o_spec)
    aliases = {}
    out_shape = jax.ShapeDtypeStruct((m, n), out_dtype)
    if into is not None:
        buf, layer, col0 = into
        assert buf.shape[1] == m and buf.dtype == out_dtype
        co = col0 // tn
        ins.append(buf)
        specs.append(_ANY)
        aliases = {len(ins) - 1: 0}
        o_spec = pl.BlockSpec((None, tm, tn), lambda i, j, q: (layer, i, j + co))
        out_shape = jax.ShapeDtypeStruct(buf.shape, buf.dtype)
    return pl.pallas_call(
        body, name=f"mm_{m}x{k}x{n}_{int(ta)}{int(tb)}{int(add is not None)}{int(b_layer is not None)}{int(into is not None)}",
        grid=(m // tm, n // tn, nk), in_specs=specs, out_specs=o_spec, out_shape=out_shape,
        input_output_aliases=aliases,
        scratch_shapes=[pltpu.VMEM((tm, tn), F32)] if nk > 1 else [],
        compiler_params=_cparams(("parallel", "parallel", "arbitrary")),
    )(*ins)


def _resident_tm(m, row_bytes, resident_bytes):
    for tm in (512, 256, 128):
        if m % tm == 0 and 2 * (resident_bytes + tm * row_bytes) <= MM_VMEM_BUDGET:
            return tm
    return 128


def matmul_sum(a_list, b_list, tb, add=None, b_cols=None):
    n_ops = len(a_list)
    m, n = a_list[0].shape[0], b_list[0].shape[0 if tb else 1]
    dims = ((1,), (1,) if tb else (0,))
    b_specs = ([_whole_spec(b) for b in b_list] if b_cols is None else
               [pl.BlockSpec((n, a.shape[1]), lambda i, cb=cb: (0, cb)) for a, cb in zip(a_list, b_cols)])
    tm = _resident_tm(m, sum(a.shape[1] * a.dtype.itemsize for a in a_list) + n * 4 * (1 + (add is not None)),
                      sum(n * a.shape[1] * b.dtype.itemsize for a, b in zip(a_list, b_list)))

    def body(*refs):
        acc = refs[2 * n_ops][...] if add is not None else None
        for t in range(n_ops):
            part = lax.dot_general(refs[t][...].astype(MXU_DTYPE), refs[n_ops + t][...].astype(MXU_DTYPE),
                                   (dims, ((), ())), preferred_element_type=F32)
            acc = part if acc is None else acc + part
        refs[-1][...] = acc

    o_spec = pl.BlockSpec((tm, n), lambda i: (i, 0))
    return pl.pallas_call(
        body, name=f"mm_sum_{n_ops}x{sum(a.shape[1] for a in a_list)}_{int(tb)}{int(add is not None)}", grid=(m // tm,),
        in_specs=[pl.BlockSpec((tm, a.shape[1]), lambda i: (i, 0)) for a in a_list] + b_specs
        + ([o_spec] if add is not None else []),
        out_specs=o_spec, out_shape=jax.ShapeDtypeStruct((m, n), F32),
        compiler_params=_cparams(("parallel",)),
    )(*a_list, *b_list, *([add] if add is not None else []))


def matmul_multi(a, b_list):
    n_ops = len(b_list)
    m = a.shape[0]
    tm = _resident_tm(m, a.shape[1] * a.dtype.itemsize + 4 * sum(b.shape[1] for b in b_list),
                      sum(b.size * b.dtype.itemsize for b in b_list))

    def body(*refs):
        av = refs[0][...].astype(MXU_DTYPE)
        for t in range(n_ops):
            refs[1 + n_ops + t][...] = jnp.dot(av, refs[1 + t][...].astype(MXU_DTYPE), preferred_element_type=F32)

    return pl.pallas_call(
        body, name=f"mm_multi_{n_ops}x{sum(b.shape[1] for b in b_list)}", grid=(m // tm,),
        in_specs=[pl.BlockSpec((tm, a.shape[1]), lambda i: (i, 0))] + [_whole_spec(b) for b in b_list],
        out_specs=[pl.BlockSpec((tm, b.shape[1]), lambda i: (i, 0)) for b in b_list],
        out_shape=[jax.ShapeDtypeStruct((m, b.shape[1]), F32) for b in b_list],
        compiler_params=_cparams(("parallel",)),
    )(a, *b_list)


def _row_spec(a, block):
    return pl.BlockSpec((block, a.shape[1]), lambda i: (i, 0))


def _whole_spec(a):
    return pl.BlockSpec(a.shape, lambda i: (0,) * a.ndim)


def tmap(name, fn, rows, params, outs, block=ROW_BLOCK):
    s = rows[0].shape[0]
    block = min(block, s)
    nr, npar = len(rows), len(params)

    def body(*refs):
        res = fn(*[_f32(r[...]) for r in refs[:nr]], *[_f32(p[...]) for p in refs[nr:nr + npar]])
        for o_ref, v in zip(refs[nr + npar:], res, strict=True):
            o_ref[...] = v.astype(o_ref.dtype)

    return pl.pallas_call(
        body, name=name, grid=(s // block,),
        in_specs=[_row_spec(a, block) for a in rows] + [_whole_spec(p) for p in params],
        out_specs=[pl.BlockSpec((block, w), lambda i: (i, 0)) for w, _ in outs],
        out_shape=[jax.ShapeDtypeStruct((s, w), dt) for w, dt in outs],
        compiler_params=_cparams(("parallel",)),
    )(*rows, *params)


def tmap_bwd(name, fn, rows, params, douts, row_grad, row_add=None, grad_dtype=None, block=ROW_BLOCK, rider=None):
    s = rows[0].shape[0]
    block = min(block, s)
    grad_dtype = grad_dtype or MXU_DTYPE
    nr, npar, nd = len(rows), len(params), len(douts)
    gr = [i for i in range(nr) if row_grad[i]]
    row_add = row_add or {}
    adds = [row_add[i] for i in gr if i in row_add]
    dups = [i for i in gr if i in row_add]
    n_out = len(gr) + npar

    def body(*refs):
        rv = [_f32(r[...]) for r in refs[:nr]]
        pv = [_f32(p[...]) for p in refs[nr:nr + npar]]
        dv = tuple(_f32(d[...]) for d in refs[nr + npar:nr + npar + nd])
        add_refs = list(refs[nr + npar + nd:nr + npar + nd + len(adds)])
        out_refs = refs[nr + npar + nd + len(adds):]

        def f(*diff):
            rr = list(rv)
            for n_, i_ in enumerate(gr):
                rr[i_] = diff[n_]
            return tuple(fn(*rr, *diff[len(gr):]))

        _, vjp = jax.vjp(f, *[rv[i_] for i_ in gr], *pv)
        g = vjp(dv)
        for n_, i_ in enumerate(gr):
            v = g[n_]
            if i_ in row_add:
                v = v + add_refs.pop(0)[...].astype(F32)
                out_refs[n_out + dups.index(i_)][...] = v.astype(MXU_DTYPE)
            out_refs[n_][...] = v.astype(out_refs[n_].dtype)
        first = pl.program_id(0) == 0
        for n_ in range(npar):
            ref = out_refs[len(gr) + n_]

            @pl.when(first)
            def _(ref=ref):
                ref[...] = jnp.zeros_like(ref)

            ref[...] += g[len(gr) + n_]

    grid = (s // block,)
    body, r_ins, r_in_specs, r_out_specs, r_out_shapes, r_scratch, _ = _ride(
        body, rider, nr + npar + nd + len(adds), n_out + len(dups), 0, grid)
    res = pl.pallas_call(
        body, name=name, grid=grid,
        in_specs=[_row_spec(a, block) for a in rows] + [_whole_spec(p) for p in params]
        + [_row_spec(d, block) for d in douts] + [_row_spec(a, block) for a in adds] + r_in_specs,
        out_specs=[_row_spec(rows[i], block) for i in gr] + [_whole_spec(p) for p in params]
        + [_row_spec(rows[i], block) for i in dups] + r_out_specs,
        out_shape=[jax.ShapeDtypeStruct(rows[i].shape, F32 if i in row_add else grad_dtype) for i in gr]
        + [jax.ShapeDtypeStruct(p.shape, F32) for p in params]
        + [jax.ShapeDtypeStruct(rows[i].shape, MXU_DTYPE) for i in dups] + r_out_shapes,
        scratch_shapes=r_scratch, compiler_params=_cparams(("arbitrary",)),
    )(*rows, *params, *douts, *adds, *r_ins)
    n_all = n_out + len(dups)
    row_res = list(res[:len(gr)]) + list(res[n_out:n_all])
    if rider is not None:
        return row_res, list(res[len(gr):n_out]), rider.results(res[n_all:])
    return row_res, list(res[len(gr):n_out])


def rscan(name, fn, state_shapes, rows, params, outs, block):
    s = rows[0].shape[0]
    nsteps = s // block
    nr, npar, no, ns = len(rows), len(params), len(outs), len(state_shapes)

    def body(*refs):
        out_refs = refs[nr + npar:nr + npar + no]
        sav_refs = refs[nr + npar + no:nr + npar + no + ns]
        st_refs = refs[nr + npar + no + ns:]

        @pl.when(pl.program_id(0) == 0)
        def _():
            for st in st_refs:
                st[...] = jnp.zeros_like(st)

        sts = tuple(st[...] for st in st_refs)
        for sv, v in zip(sav_refs, sts):
            sv[...] = v
        new, res = fn(sts, *[_f32(r[...]) for r in refs[:nr]], *[_f32(p[...]) for p in refs[nr:nr + npar]])
        for st, v in zip(st_refs, new, strict=True):
            st[...] = v
        for o_ref, v in zip(out_refs, res, strict=True):
            o_ref[...] = v.astype(o_ref.dtype)

    res = pl.pallas_call(
        body, name=name, grid=(nsteps,),
        in_specs=[_row_spec(a, block) for a in rows] + [_whole_spec(p) for p in params],
        out_specs=[pl.BlockSpec((block, w), lambda i: (i, 0)) for w, _ in outs]
        + [pl.BlockSpec(sh, lambda i: (i, 0)) for sh in state_shapes],
        out_shape=[jax.ShapeDtypeStruct((s, w), dt) for w, dt in outs]
        + [jax.ShapeDtypeStruct((nsteps * sh[0], sh[1]), F32) for sh in state_shapes],
        scratch_shapes=[pltpu.VMEM(sh, F32) for sh in state_shapes],
        compiler_params=_cparams(("arbitrary",)),
    )(*rows, *params)
    return list(res[:no]), list(res[no:])


def rscan_bwd(name, fn, saved, rows, params, douts, block, grad_dtype=None):
    s = rows[0].shape[0]
    nsteps = s // block
    grad_dtype = grad_dtype or MXU_DTYPE
    nr, npar, nd, ns = len(rows), len(params), len(douts), len(saved)
    state_shapes = [(sv.shape[0] // nsteps, sv.shape[1]) for sv in saved]

    def body(*refs):
        rv = [_f32(r[...]) for r in refs[:nr]]
        pv = [_f32(p[...]) for p in refs[nr:nr + npar]]
        dv = tuple(_f32(d[...]) for d in refs[nr + npar:nr + npar + nd])
        sv = tuple(x[...] for x in refs[nr + npar + nd:nr + npar + nd + ns])
        out_refs = refs[nr + npar + nd + ns:nr + npar + nd + ns + nr + npar]
        dst_refs = refs[nr + npar + nd + ns + nr + npar:]
        first = pl.program_id(0) == 0

        @pl.when(first)
        def _():
            for d in dst_refs:
                d[...] = jnp.zeros_like(d)

        def f(sts, *args):
            return fn(sts, *args)

        _, vjp = jax.vjp(f, sv, *rv, *pv)
        g = vjp((tuple(d[...] for d in dst_refs), dv))
        for d, v in zip(dst_refs, g[0], strict=True):
            d[...] = v
        for n_ in range(nr):
            out_refs[n_][...] = g[1 + n_].astype(out_refs[n_].dtype)
        for n_ in range(npar):
            ref = out_refs[nr + n_]

            @pl.when(first)
            def _(ref=ref):
                ref[...] = jnp.zeros_like(ref)

            ref[...] += g[1 + nr + n_]

    rev = lambda i: (nsteps - 1 - i, 0)
    res = pl.pallas_call(
        body, name=name, grid=(nsteps,),
        in_specs=[pl.BlockSpec((block, a.shape[1]), rev) for a in rows] + [_whole_spec(p) for p in params]
        + [pl.BlockSpec((block, d.shape[1]), rev) for d in douts] + [pl.BlockSpec(sh, rev) for sh in state_shapes],
        out_specs=[pl.BlockSpec((block, a.shape[1]), rev) for a in rows] + [_whole_spec(p) for p in params],
        out_shape=[jax.ShapeDtypeStruct(a.shape, grad_dtype) for a in rows]
        + [jax.ShapeDtypeStruct(p.shape, F32) for p in params],
        scratch_shapes=[pltpu.VMEM(sh, F32) for sh in state_shapes],
        compiler_params=_cparams(("arbitrary",)),
    )(*rows, *params, *douts, *saved)
    return list(res[:nr]), list(res[nr:])


def _norm_stage(x, g):
    return (_rms(x, g),)


def _tril():
    r = lax.broadcasted_iota(jnp.int32, (CHUNK, CHUNK), 0)
    c = lax.broadcasted_iota(jnp.int32, (CHUNK, CHUNK), 1)
    return r >= c


def _gla_chunk(st, q, k, v, la, b):
    tril = _tril()
    rowi = lax.broadcasted_iota(jnp.int32, (CHUNK, 1), 0)
    b_last = jnp.sum(la, axis=0, keepdims=True)
    b_ref = jnp.sum(jnp.where(rowi < CHUNK // 2, la, 0.0), axis=0, keepdims=True)
    att = mm.nt(q * jnp.exp(b - b_ref), k * jnp.exp(b_ref - b))
    att = jnp.where(tril, att, 0.0)
    o = mm.nn(att, v) + mm.nn(q * jnp.exp(b), st)
    decay = jnp.exp(jnp.broadcast_to(b_last, (LANE, LANE)).T)
    decay = jnp.concatenate([decay] * (v.shape[1] // LANE), axis=1)
    st2 = decay * st + mm.tn(k * jnp.exp(b_last - b), v)
    return st2, o


def _gla_step(heads, vp, scale):
    kp = LANE

    def fn(states, q, k, v, la):
        sts = list(states)
        trif = _tril().astype(F32)
        rows = []
        for c in range(q.shape[0] // CHUNK):
            r = slice(c * CHUNK, (c + 1) * CHUNK)
            b_all = hi.nn(trif, la[r])
            oh = []
            for h in range(heads):
                ks, vs = slice(h * kp, (h + 1) * kp), slice(h * vp, (h + 1) * vp)
                qh = q[r, ks] * scale if scale != 1.0 else q[r, ks]
                sts[h], o = _gla_chunk(sts[h], qh, k[r, ks], v[r, vs], la[r, ks], b_all[:, ks])
                oh.append(o)
            rows.append(jnp.concatenate(oh, axis=1))
        return tuple(sts), (jnp.concatenate(rows, axis=0),)

    return fn


def _ssd_step(states, xa, dtr, dtb, alog, dsk):
    sts = list(states)
    trif = _tril().astype(F32)
    wide = lax.broadcasted_iota(jnp.int32, (CHUNK, LANE), 0) >= lax.broadcasted_iota(jnp.int32, (CHUNK, LANE), 1)
    hg = SSM_HEADS // SSM_GROUPS
    xw = SSM_HEADS * LANE
    lane, head = lax.broadcasted_iota(jnp.int32, (LANE, xw), 1), lax.broadcasted_iota(jnp.int32, (LANE, xw), 0)
    spread = ((lane >= head * LANE) & (lane < (head + 1) * LANE)).astype(F32)
    neg_a = -jnp.exp(alog)
    pad = jnp.zeros((CHUNK, LANE), F32)
    rows = []
    for c in range(xa.shape[0] // CHUNK):
        r = slice(c * CHUNK, (c + 1) * CHUNK)
        dt_all = _softplus(hi.nn(dtr[r], spread) + dtb)
        a_all = dt_all * neg_a
        acs_all = hi.nn(trif, a_all)
        last_all = jnp.sum(a_all, axis=0, keepdims=True)
        yh = []
        for g in range(SSM_GROUPS):
            bm = xa[r, xw + g * LANE:xw + (g + 1) * LANE]
            cm = xa[r, xw + (SSM_GROUPS + g) * LANE:xw + (SSM_GROUPS + g + 1) * LANE]
            cb = mm.nt(cm, jnp.concatenate([bm, pad], axis=0))
            for hh in range(hg):
                h = g * hg + hh
                ls = slice(h * LANE, (h + 1) * LANE)
                xs, acs, acs_last = xa[r, ls], acs_all[:, ls], last_all[:, ls]
                xdt = xs * dt_all[:, ls]
                seg = acs - jnp.concatenate([acs, pad], axis=0).T[:CHUNK]
                lmat = jnp.exp(jnp.where(wide, seg, -1e30))
                y = (mm.nn(cb * lmat, jnp.concatenate([xdt, pad], axis=0)) + mm.nn(cm, sts[h]) * jnp.exp(acs)
                     + dsk[:, ls] * xs)
                sts[h] = jnp.exp(acs_last) * sts[h] + mm.tn(bm, xdt * jnp.exp(acs_last - acs))
                yh.append(y)
        rows.append(jnp.concatenate(yh, axis=1))
    return tuple(sts), (jnp.concatenate(rows, axis=0),)


def _gla_pre(glr, w2, bg):
    z = mm.nn(glr, w2) + bg
    return (-_softplus(-z) * (1.0 / GLA_GATE_NORM),)


def _gla_post(o, og, g):
    w = 2 * LANE
    return (jnp.concatenate([_rms(o[:, h * w:(h + 1) * w], g, GLA_DV) * _silu(og[:, h * w:(h + 1) * w])
                             for h in range(GLA_HEADS)], axis=1),)


def _hgrn_pre(q, f, lbnd):
    e = jnp.exp(lbnd - jnp.max(lbnd, axis=0, keepdims=True))
    rowi = lax.broadcasted_iota(jnp.int32, e.shape, 0)
    lb = jnp.sum(jnp.where(rowi >= 1, e, 0.0), axis=0, keepdims=True) / jnp.sum(e, axis=0, keepdims=True)
    fg = lb + (1.0 - lb) * _sigmoid(f)
    return _silu(q), 1.0 - fg, jnp.log(fg)


def _hgrn_post(o, og, g):
    return (jnp.concatenate([_rms(o[:, h * LANE:(h + 1) * LANE], g) for h in range(HGRN_HEADS)], axis=1)
            * _sigmoid(og),)


def _mamba_post(y, z, g):
    v = y * _silu(z)
    w = (SSM_HEADS // SSM_GROUPS) * LANE
    n_real = (SSM_HEADS // SSM_GROUPS) * SSM_HD
    return (jnp.concatenate([_rms(v[:, i * w:(i + 1) * w], g[:, i * w:(i + 1) * w], n_real)
                             for i in range(SSM_GROUPS)], axis=1),)


def _dil_pre(q, k, cosf, sinf, qg, kg):
    def groups(x, g):
        out = []
        for grp in range(len(DIL_GROUPS)):
            hs = []
            for h in range(grp * DIL_HEADS, (grp + 1) * DIL_HEADS):
                n = _rms(x[:, h * LANE:(h + 1) * LANE], g)
                hs.append(n * cosf + _swap_halves(n) * sinf)
            out.append(jnp.concatenate(hs, axis=1))
        return out

    return (*groups(q, qg), *groups(k, kg))


def _dil_merge(o0, o1, o2, l0, l1, l2):
    m = jnp.maximum(jnp.maximum(l0, l1), l2)
    e0, e1, e2 = jnp.exp(l0 - m), jnp.exp(l1 - m), jnp.exp(l2 - m)
    return ((e0 * o0 + e1 * o1 + e2 * o2) / (e0 + e1 + e2),)


def _dil_block(q, kp, kc, vp, vc, lim):
    kk = jnp.concatenate([kp, kc], axis=0)
    vv = jnp.concatenate([vp, vc], axis=0)
    s = mm.nt(q, kk) * (DIL_HD ** -0.5)
    i = lax.broadcasted_iota(jnp.int32, s.shape, 0)
    j = lax.broadcasted_iota(jnp.int32, s.shape, 1)
    dist = DIL_BLOCK + i - j
    s = jnp.where((dist >= 0) & (dist <= DIL_BLOCK) & (j >= lim), s, -1e30)
    m = jnp.max(s, axis=-1, keepdims=True)
    p = jnp.exp(s - m)
    l = jnp.sum(p, axis=-1, keepdims=True)
    return mm.nn(p / l, vv), jnp.broadcast_to(m + jnp.log(l), (q.shape[0], LANE))


def _xattn(xq, kv, qg, kg):
    w = XA_HEADS * LANE
    os_ = []
    for h in range(XA_HEADS):
        ls = slice(h * LANE, (h + 1) * LANE)
        q = _rms(xq[:, ls], qg, XA_HD)
        k = _rms(kv[:, ls], kg, XA_HD)
        s = mm.nt(q, k) * (XA_HD ** -0.5)
        p = jnp.exp(s - jnp.max(s, axis=-1, keepdims=True))
        p = p / jnp.sum(p, axis=-1, keepdims=True)
        os_.append(mm.nn(p, kv[:, w + h * LANE:w + (h + 1) * LANE]))
    return (jnp.concatenate(os_, axis=1),)


def _dil_geometry(s, w, r, g, v_cols):
    hb = DIL_HEADS if r == 1 else 1
    rb = DIL_BLOCK * r
    nb = s // rb
    bw = hb * LANE
    v_col0 = g * (w // bw)
    assert v_cols % bw == 0 and s % rb == 0
    return hb, rb, nb, bw, v_col0


def _sub(r, res):
    return pl.ds(res, DIL_BLOCK, stride=r) if r > 1 else slice(None)


def dil_attn(name, q, k, v, r, g):
    s, w = q.shape
    hb, rb, nb, bw, v_col0 = _dil_geometry(s, w, r, g, v.shape[1])

    def body(q_r, kp_r, kc_r, vp_r, vc_r, o_r, l_r):
        lim = jnp.where(pl.program_id(1) == 0, DIL_BLOCK, 0)
        for res in range(r):
            rows = _sub(r, res)
            for h in range(hb):
                ls = slice(h * LANE, (h + 1) * LANE)
                o, lse = _dil_block(q_r[rows, ls], kp_r[rows, ls], kc_r[rows, ls], vp_r[rows, ls], vc_r[rows, ls], lim)
                o_r[rows, ls] = o
                l_r[rows, ls] = lse

    cur = pl.BlockSpec((rb, bw), lambda hblk, n: (n, hblk))
    prev = pl.BlockSpec((rb, bw), lambda hblk, n: (jnp.maximum(n - 1, 0), hblk))
    vcur = pl.BlockSpec((rb, bw), lambda hblk, n: (n, v_col0 + hblk))
    vprev = pl.BlockSpec((rb, bw), lambda hblk, n: (jnp.maximum(n - 1, 0), v_col0 + hblk))
    return pl.pallas_call(
        body, name=name, grid=(w // bw, nb), in_specs=[cur, prev, cur, vprev, vcur], out_specs=[cur, cur],
        out_shape=[jax.ShapeDtypeStruct((s, w), F32)] * 2,
        compiler_params=_cparams(("parallel", "parallel")),
    )(q, k, k, v, v)


def dil_attn_bwd(name, q, k, v, do, dlse, r, g):
    s, w = q.shape
    hb, rb, nb, bw, v_col0 = _dil_geometry(s, w, r, g, v.shape[1])

    def body(q_r, kp_r, kc_r, vp_r, vc_r, do_r, dl_r, dq_r, dk_r, dv_r, ck, cv):
        i = pl.program_id(1)
        lim = jnp.where(i == nb - 1, DIL_BLOCK, 0)

        @pl.when(i == 0)
        def _():
            ck[...] = jnp.zeros_like(ck)
            cv[...] = jnp.zeros_like(cv)

        for res in range(r):
            rows = _sub(r, res)
            for h in range(hb):
                ls = slice(h * LANE, (h + 1) * LANE)
                _, vjp = jax.vjp(functools.partial(_dil_block, lim=lim),
                                 q_r[rows, ls], kp_r[rows, ls], kc_r[rows, ls], vp_r[rows, ls], vc_r[rows, ls])
                gq, gkp, gkc, gvp, gvc = vjp((do_r[rows, ls], dl_r[rows, ls]))
                dq_r[rows, ls] = gq
                dk_r[rows, ls] = gkc + ck[rows, ls]
                dv_r[rows, ls] = gvc + cv[rows, ls]
                ck[rows, ls] = gkp
                cv[rows, ls] = gvp

    cur = pl.BlockSpec((rb, bw), lambda hblk, i: (nb - 1 - i, hblk))
    prev = pl.BlockSpec((rb, bw), lambda hblk, i: (jnp.maximum(nb - 2 - i, 0), hblk))
    vcur = pl.BlockSpec((rb, bw), lambda hblk, i: (nb - 1 - i, v_col0 + hblk))
    vprev = pl.BlockSpec((rb, bw), lambda hblk, i: (jnp.maximum(nb - 2 - i, 0), v_col0 + hblk))
    return pl.pallas_call(
        body, name=name, grid=(w // bw, nb), in_specs=[cur, prev, cur, vprev, vcur, cur, cur],
        out_specs=[cur, cur, cur], out_shape=[jax.ShapeDtypeStruct((s, w), F32)] * 3,
        scratch_shapes=[pltpu.VMEM((rb, bw), F32)] * 2,
        compiler_params=_cparams(("parallel", "arbitrary")),
    )(q, k, k, v, v, do, dlse)


def _dsilu(u):
    sg = _sigmoid(u)
    return sg * (1.0 + u * (1.0 - sg))


def _ride(body, rider, n_in, n_out, n_scratch, grid):
    if rider is None:
        return body, [], [], [], [], [], None
    ni, no = len(rider.ins), len(rider.out_shapes)

    def wrapped(*refs):
        k_in, r_in = refs[:n_in], refs[n_in:n_in + ni]
        k_out, r_out = refs[n_in + ni:n_in + ni + n_out], refs[n_in + ni + n_out:n_in + ni + n_out + no]
        k_scr = refs[n_in + ni + n_out + no:n_in + ni + n_out + no + n_scratch]
        send_sems, recv_sems = refs[-2], refs[-1]
        first = functools.reduce(jnp.logical_and, [pl.program_id(a) == 0 for a in range(len(grid))])
        last = functools.reduce(jnp.logical_and, [pl.program_id(a) == g - 1 for a, g in enumerate(grid)])

        @pl.when(first)
        def _():
            rider.start(r_in, r_out, send_sems, recv_sems)

        body(*k_in, *k_out, *k_scr)

        @pl.when(last)
        def _():
            rider.finish(r_in, r_out, send_sems, recv_sems)

    sems = [pltpu.SemaphoreType.DMA((rider.n_sems,)), pltpu.SemaphoreType.DMA((rider.n_sems,))]
    return wrapped, rider.ins, [_ANY] * ni, [_ANY] * no, rider.out_shapes, sems, ("arbitrary",) * len(grid)


CONV_STRIP = 16


def _shifted_rows(prev8, cur_r, next8, lanes, s0, n, sh, block):
    if s0 - sh < 0:
        assert s0 == 0
        xp = jnp.concatenate([prev8, cur_r[0:n, lanes]], axis=0)
        return pltpu.roll(xp, sh, 0)[SUBLANE:SUBLANE + n]
    if s0 - sh + n > block:
        assert s0 == block and n == SUBLANE
        xp = jnp.concatenate([cur_r[block - SUBLANE:block, lanes], next8], axis=0)
        return (pltpu.roll(xp, sh, 0) if sh else xp)[SUBLANE:]
    return cur_r[pl.ds(s0 - sh, n), lanes]


def conv_fwd(name, x, w, b, mode, out_dtype, tc, block=CONV_BLOCK, rider=None):
    s, c = x.shape
    ntap = w.shape[0]
    block = min(block, s)
    f = c // 2 if mode == 'glu' else c
    nh = 2 if mode == 'glu' else 1
    off = f // tc

    def body(*refs):
        first = pl.program_id(1) == 0
        o_ref = refs[-1]

        def column(cidx, carry):
            lanes = pl.ds(pl.multiple_of(cidx * LANE, LANE), LANE)
            prevs = [jnp.where(first, 0.0, refs[4 * hlf][:, lanes]) for hlf in range(nh)]
            for s0 in range(0, block, CONV_STRIP):
                us = []
                for hlf in range(nh):
                    _, cur_r, w_r, b_r = refs[4 * hlf:4 * hlf + 4]
                    acc = b_r[:, lanes]
                    for j in range(ntap):
                        xs = _shifted_rows(prevs[hlf], cur_r, None, lanes, s0, CONV_STRIP, ntap - 1 - j, block)
                        acc = acc + w_r[j:j + 1, lanes] * xs
                    us.append(acc)
                res = _silu(us[0]) * us[1] if mode == 'glu' else _silu(us[0])
                o_ref[pl.ds(s0, CONV_STRIP), lanes] = res.astype(o_ref.dtype)
            return carry

        lax.fori_loop(0, tc // LANE, column, 0)

    rb = block // SUBLANE
    ins, specs = [], []
    for hlf in range(nh):
        o = hlf * off
        ins += [x, x, w, b]
        specs += [pl.BlockSpec((SUBLANE, tc), lambda j, i, o=o: (jnp.maximum(i * rb - 1, 0), j + o)),
                  pl.BlockSpec((block, tc), lambda j, i, o=o: (i, j + o)),
                  pl.BlockSpec((ntap, tc), lambda j, i, o=o: (0, j + o)),
                  pl.BlockSpec((1, tc), lambda j, i, o=o: (0, j + o))]
    grid = (f // tc, s // block)
    body, r_ins, r_in_specs, r_out_specs, r_out_shapes, r_scratch, sem = _ride(body, rider, len(ins), 1, 0, grid)
    res = pl.pallas_call(
        body, name=name, grid=grid, in_specs=specs + r_in_specs,
        out_specs=[pl.BlockSpec((block, tc), lambda j, i: (i, j))] + r_out_specs,
        out_shape=[jax.ShapeDtypeStruct((s, f), out_dtype)] + r_out_shapes, scratch_shapes=r_scratch,
        compiler_params=_cparams(sem or ("parallel", "parallel")),
    )(*ins, *r_ins)
    return res[0] if rider is None else (res[0], rider.results(res[1:]))


def conv_bwd(name, x, w, b, dout, mode, tc, block=CONV_BLOCK, rider=None):
    s, c = x.shape
    ntap = w.shape[0]
    block = min(block, s)
    nblk = s // block
    f = c // 2 if mode == 'glu' else c
    nh = 2 if mode == 'glu' else 1
    off = f // tc
    ext = block + SUBLANE

    def body(*refs):
        i = pl.program_id(1)
        first, last = i == 0, i == nblk - 1
        dcur_r, dnext_r = refs[5 * nh], refs[5 * nh + 1]
        outs = refs[5 * nh + 2:5 * nh + 2 + 3 * nh]
        du_scr = refs[5 * nh + 2 + 3 * nh:]

        @pl.when(first)
        def _():
            for hlf in range(nh):
                outs[3 * hlf + 1][...] = jnp.zeros_like(outs[3 * hlf + 1])
                outs[3 * hlf + 2][...] = jnp.zeros_like(outs[3 * hlf + 2])

        def column(cidx, carry):
            lanes = pl.ds(pl.multiple_of(cidx * LANE, LANE), LANE)
            prevs = [jnp.where(first, 0.0, refs[5 * hlf][:, lanes]) for hlf in range(nh)]
            nexts = [jnp.where(last, 0.0, refs[5 * hlf + 2][:, lanes]) for hlf in range(nh)]
            db_acc = [jnp.zeros((CONV_STRIP, LANE), F32) for _ in range(nh)]
            dw_acc = [[jnp.zeros((CONV_STRIP, LANE), F32) for _ in range(ntap)] for _ in range(nh)]
            for s0 in range(0, ext, CONV_STRIP):
                n = min(CONV_STRIP, ext - s0)
                d_e = dcur_r[pl.ds(s0, n), lanes] if s0 < block else jnp.where(last, 0.0, dnext_r[:, lanes])
                xs, us = [], []
                for hlf in range(nh):
                    cur_r, w_r, b_r = refs[5 * hlf + 1], refs[5 * hlf + 3], refs[5 * hlf + 4]
                    sh_rows = [_shifted_rows(prevs[hlf], cur_r, nexts[hlf], lanes, s0, n, ntap - 1 - j, block)
                               for j in range(ntap)]
                    acc = b_r[:, lanes]
                    for j in range(ntap):
                        acc = acc + w_r[j:j + 1, lanes] * sh_rows[j]
                    xs.append(sh_rows)
                    us.append(acc)
                dus = [d_e * us[1] * _dsilu(us[0]), d_e * _silu(us[0])] if mode == 'glu' else [d_e * _dsilu(us[0])]
                for hlf in range(nh):
                    du_scr[hlf][pl.ds(s0, n), lanes] = dus[hlf]
                    if s0 < block:
                        db_acc[hlf] = db_acc[hlf] + dus[hlf]
                        for j in range(ntap):
                            dw_acc[hlf][j] = dw_acc[hlf][j] + dus[hlf] * xs[hlf][j]
            for hlf in range(nh):
                w_r = refs[5 * hlf + 3]
                dx_r, dw_r, db_r = outs[3 * hlf:3 * hlf + 3]
                db_r[:, lanes] += jnp.sum(db_acc[hlf], axis=0, keepdims=True)
                for j in range(ntap):
                    dw_r[j:j + 1, lanes] += jnp.sum(dw_acc[hlf][j], axis=0, keepdims=True)
                for s0 in range(0, block, CONV_STRIP):
                    dx = None
                    for j in range(ntap):
                        term = w_r[j:j + 1, lanes] * du_scr[hlf][pl.ds(s0 + ntap - 1 - j, CONV_STRIP), lanes]
                        dx = term if dx is None else dx + term
                    dx_r[pl.ds(s0, CONV_STRIP), lanes] = dx.astype(dx_r.dtype)
            return carry

        lax.fori_loop(0, tc // LANE, column, 0)

    rb = block // SUBLANE
    nrow8 = s // SUBLANE
    ins, specs = [], []
    for hlf in range(nh):
        o = hlf * off
        ins += [x, x, x, w, b]
        specs += [pl.BlockSpec((SUBLANE, tc), lambda j, i, o=o: (jnp.maximum(i * rb - 1, 0), j + o)),
                  pl.BlockSpec((block, tc), lambda j, i, o=o: (i, j + o)),
                  pl.BlockSpec((SUBLANE, tc), lambda j, i, o=o: (jnp.minimum((i + 1) * rb, nrow8 - 1), j + o)),
                  pl.BlockSpec((ntap, tc), lambda j, i, o=o: (0, j + o)),
                  pl.BlockSpec((1, tc), lambda j, i, o=o: (0, j + o))]
    ins += [dout, dout]
    specs += [pl.BlockSpec((block, tc), lambda j, i: (i, j)),
              pl.BlockSpec((SUBLANE, tc), lambda j, i: (jnp.minimum((i + 1) * rb, nrow8 - 1), j))]
    out_specs, out_shape = [], []
    for hlf in range(nh):
        out_specs += [pl.BlockSpec((block, tc), lambda j, i: (i, j)), pl.BlockSpec((ntap, tc), lambda j, i: (0, j)),
                      pl.BlockSpec((1, tc), lambda j, i: (0, j))]
        out_shape += [jax.ShapeDtypeStruct((s, f), MXU_DTYPE), jax.ShapeDtypeStruct((ntap, f), F32),
                      jax.ShapeDtypeStruct((1, f), F32)]
    grid = (f // tc, nblk)
    body, r_ins, r_in_specs, r_out_specs, r_out_shapes, r_scratch, sem = _ride(body, rider, len(ins), 3 * nh, nh, grid)
    res = pl.pallas_call(
        body, name=name, grid=grid, in_specs=specs + r_in_specs, out_specs=out_specs + r_out_specs,
        out_shape=out_shape + r_out_shapes, scratch_shapes=[pltpu.VMEM((ext, tc), F32)] * nh + r_scratch,
        compiler_params=_cparams(sem or ("parallel", "arbitrary")),
    )(*ins, *r_ins)
    rode = None if rider is None else rider.results(res[3 * nh:])
    if nh == 1:
        return [res[0]], res[1], res[2], rode
    return ([res[0], res[3]], jnp.concatenate([res[1], res[4]], axis=1), jnp.concatenate([res[2], res[5]], axis=1),
            rode)


def loss_head(y, target, block=ROW_BLOCK):
    s, d = y.shape
    block = min(block, s)

    def body(y_r, t_r, acc_r, dy_r, dyb_r):
        e = y_r[...] - t_r[...]
        dy_r[...] = e * (1.0 / d)
        dyb_r[...] = (e * (1.0 / d)).astype(dyb_r.dtype)

        @pl.when(pl.program_id(0) == 0)
        def _():
            acc_r[...] = jnp.zeros_like(acc_r)

        acc_r[...] += jnp.sum((e * e).reshape(block // SUBLANE, SUBLANE, d), axis=0) * (0.5 / d)

    return pl.pallas_call(
        body, name="loss_head", grid=(s // block,),
        in_specs=[pl.BlockSpec((block, d), lambda i: (i, 0))] * 2,
        out_specs=[pl.BlockSpec((SUBLANE, d), lambda i: (0, 0))] + [pl.BlockSpec((block, d), lambda i: (i, 0))] * 2,
        out_shape=[jax.ShapeDtypeStruct((SUBLANE, d), F32), jax.ShapeDtypeStruct((s, d), F32),
                   jax.ShapeDtypeStruct((s, d), MXU_DTYPE)],
        compiler_params=_cparams(("arbitrary",)),
    )(y, target)


def adamw(name, w, g, m, v):
    r, c = w.shape
    tr = r if r <= 512 else _tile(r, (512, 256, 128, 64, 32, 16, 8))
    if c * tr * 4 > (1 << 21):
        tr = _tile(r, (256, 128, 64, 32, 16, 8))

    def body(w_r, g_r, m_r, v_r, d_r, nm_r, nv_r):
        gg = g_r[...]
        nm = ADAM_B1 * m_r[...] + (1.0 - ADAM_B1) * gg
        nv = ADAM_B2 * v_r[...] + (1.0 - ADAM_B2) * (gg * gg)
        m_hat = nm / (1.0 - ADAM_B1 ** ADAM_STEP)
        v_hat = nv / (1.0 - ADAM_B2 ** ADAM_STEP)
        d_r[...] = -ADAM_LR * (m_hat / (jnp.sqrt(v_hat) + ADAM_EPS) + ADAM_WD * w_r[...])
        nm_r[...] = nm
        nv_r[...] = nv

    spec = pl.BlockSpec((tr, c), lambda i: (i, 0))
    return pl.pallas_call(
        body, name=name, grid=(r // tr,), in_specs=[spec] * 4, out_specs=[spec] * 3,
        out_shape=[jax.ShapeDtypeStruct((r, c), F32)] * 3, compiler_params=_cparams(("parallel",)),
    )(w, g, m, v)


MESH = pl.DeviceIdType.MESH
_ANY = pl.BlockSpec(memory_space=pl.ANY)


def _place():
    return lax.axis_index("x"), lax.axis_index("y"), lax.axis_index("c")


class Packed:
    def __init__(self, shard_shape):
        self.r, self.c = shard_shape
        self.h = self.r // 2
        self.whole = (N_CHIPS, self.r, self.c)
        self.got = (N_CHIPS, self.h, self.c)
        self.slab_half = (self.h, self.c)

    def shard_half(self, ref, core):
        return ref.at[pl.ds(core * self.h, self.h)]

    def whole_half(self, ref, chip, core):
        return ref.at[chip, pl.ds(core * self.h, self.h)]

    def place(self, whole, shard, chip):
        return lax.dynamic_update_slice(whole, shard[None], (chip, 0, 0))

    def grad_half(self, ref, core):
        return ref.at[:, core]

    def pair_slab(self, ref, chip):
        return ref.at[chip]


class SlabCols:
    def __init__(self, shard_shape):
        self.r, self.c = shard_shape
        self.h = self.r // 2
        self.whole = (self.r, N_CHIPS * self.c)
        self.got = (self.h, N_CHIPS * self.c)
        self.slab_half = (self.h, self.c)

    def _cols(self, chip):
        return pl.ds(pl.multiple_of(chip * self.c, LANE), self.c)

    def shard_half(self, ref, core):
        return ref.at[pl.ds(core * self.h, self.h)]

    def whole_half(self, ref, chip, core):
        return ref.at[pl.ds(core * self.h, self.h), self._cols(chip)]

    def place(self, whole, shard, chip):
        return lax.dynamic_update_slice_in_dim(whole, shard, chip * self.c, 1)

    def grad_half(self, ref, core):
        return ref.at[pl.ds(core * self.h, self.h)]

    def pair_slab(self, ref, chip):
        return ref.at[:, self._cols(chip)]


class GatherRider:
    def __init__(self, shards, kinds):
        self.ins, self.kinds, n = list(shards), kinds, len(shards)
        self.out_shapes = [jax.ShapeDtypeStruct(k.whole, s.dtype) for k, s in zip(kinds, shards)]
        self.n_sems = 6 * n

    def _copies(self, w_refs, out_refs, send_sems, recv_sems):
        x, y, cc = _place()
        chips = [(1 - x, y), (x, 1 - y), (1 - x, 1 - y)]

        def copy(t, k, chip, core, to, src=None):
            dst = self.kinds[t].whole_half(out_refs[t], 2 * chip[0] + chip[1], core)
            return pltpu.make_async_remote_copy(
                src_ref=dst if src is None else src, dst_ref=dst, send_sem=send_sems.at[6 * t + k],
                recv_sem=recv_sems.at[6 * t + k], device_id=to, device_id_type=MESH)

        first = [copy(t, j, (x, y), cc, (*chip, cc), src=self.kinds[t].shard_half(w_refs[t], cc))
                 for t in range(len(self.ins)) for j, chip in enumerate(chips)]
        return copy, first, chips, (x, y, cc)

    def start(self, w_refs, out_refs, send_sems, recv_sems):
        for cp in self._copies(w_refs, out_refs, send_sems, recv_sems)[1]:
            cp.start()

    def finish(self, w_refs, out_refs, send_sems, recv_sems):
        copy, first, chips, (x, y, cc) = self._copies(w_refs, out_refs, send_sems, recv_sems)
        passed = []
        for t in range(len(self.ins)):
            for j, chip in enumerate(chips):
                copy(t, j, chip, cc, (x, y, cc)).wait_recv()
                passed.append(copy(t, 3 + j, chip, cc, (x, y, 1 - cc)))
                passed[-1].start()
        for t in range(len(self.ins)):
            for j, chip in enumerate(chips):
                copy(t, 3 + j, chip, 1 - cc, (x, y, cc)).wait_recv()
        for cp in first + passed:
            cp.wait_send()

    def results(self, outs):
        chip = 2 * lax.axis_index("x") + lax.axis_index("y")
        return [k.place(o, s, chip) for k, o, s in zip(self.kinds, outs, self.ins)]


class ExchangeRider:
    def __init__(self, pairs, kinds):
        self.ins, self.kinds = list(pairs), kinds
        self.out_shapes = [jax.ShapeDtypeStruct((N_CHIPS,) + k.slab_half, p.dtype) for k, p in zip(kinds, pairs)]
        self.n_sems = 3 * len(pairs)

    def start(self, p_refs, out_refs, send_sems, recv_sems):
        x, y, cc = _place()
        for t in range(len(self.ins)):
            for j, chip in enumerate([(1 - x, y), (x, 1 - y), (1 - x, 1 - y)]):
                pltpu.make_async_remote_copy(
                    src_ref=self.kinds[t].pair_slab(p_refs[t], 2 * chip[0] + chip[1]), dst_ref=out_refs[t].at[2 * x + y],
                    send_sem=send_sems.at[3 * t + j], recv_sem=recv_sems.at[3 * t + j], device_id=(*chip, cc),
                    device_id_type=MESH).start()

    def finish(self, p_refs, out_refs, send_sems, recv_sems):
        x, y, cc = _place()
        me = 2 * x + y
        for t in range(len(self.ins)):
            for j, chip in enumerate([(1 - x, y), (x, 1 - y), (1 - x, 1 - y)]):
                them = 2 * chip[0] + chip[1]
                pltpu.make_async_remote_copy(
                    src_ref=self.kinds[t].pair_slab(p_refs[t], them), dst_ref=out_refs[t].at[them],
                    send_sem=send_sems.at[3 * t + j], recv_sem=recv_sems.at[3 * t + j], device_id=(x, y, cc),
                    device_id_type=MESH).wait()

    def results(self, outs):
        return list(outs)


class SwapRider:
    def __init__(self, gs, kinds):
        self.ins, self.kinds = list(gs), kinds
        self.out_shapes = [jax.ShapeDtypeStruct(k.got, g.dtype) for k, g in zip(kinds, gs)]
        self.n_sems = len(gs)

    def _copies(self, g_refs, out_refs, send_sems, recv_sems):
        x, y, cc = _place()
        return [pltpu.make_async_remote_copy(
            src_ref=self.kinds[t].grad_half(g_refs[t], 1 - cc), dst_ref=out_refs[t], send_sem=send_sems.at[t],
            recv_sem=recv_sems.at[t], device_id=(x, y, 1 - cc), device_id_type=MESH) for t in range(len(self.ins))]

    def start(self, *refs):
        for cp in self._copies(*refs):
            cp.start()

    def finish(self, *refs):
        for cp in self._copies(*refs):
            cp.wait()

    def results(self, outs):
        return list(outs)


def run_rider(rider, name):
    n, no = len(rider.ins), len(rider.out_shapes)

    def body(*refs):
        parts = (refs[:n], refs[n:n + no], refs[n + no], refs[n + no + 1])
        rider.start(*parts)
        rider.finish(*parts)

    outs = pl.pallas_call(
        body, name=name, in_specs=[_ANY] * n, out_specs=[_ANY] * no, out_shape=rider.out_shapes,
        scratch_shapes=[pltpu.SemaphoreType.DMA((rider.n_sems,)), pltpu.SemaphoreType.DMA((rider.n_sems,))],
    )(*rider.ins)
    return rider.results(outs)


def allgather_devices(buf):
    r, c = buf.shape

    def body(b_ref, out_ref, send_sems, recv_sems, local_sem):
        x, y, cc = _place()
        me = 4 * x + 2 * y + cc
        mine = pltpu.make_async_copy(b_ref, out_ref.at[me], local_sem)
        mine.start()
        copies = []
        for k in range(1, N_DEV):
            px, py, pc = x ^ (k >> 2), y ^ ((k >> 1) & 1), cc ^ (k & 1)
            cp = pltpu.make_async_remote_copy(src_ref=b_ref, dst_ref=out_ref.at[me], send_sem=send_sems.at[k - 1],
                                              recv_sem=recv_sems.at[k - 1], device_id=(px, py, pc), device_id_type=MESH)
            cp.start()
            copies.append((cp, 4 * px + 2 * py + pc))
        for k, (cp, peer) in enumerate(copies):
            pltpu.make_async_remote_copy(src_ref=b_ref, dst_ref=out_ref.at[peer], send_sem=send_sems.at[k],
                                         recv_sem=recv_sems.at[k], device_id=(x, y, cc), device_id_type=MESH).wait_recv()
        for cp, _ in copies:
            cp.wait_send()
        mine.wait()

    return pl.pallas_call(
        body, name="allgather_devices", in_specs=[_ANY], out_specs=_ANY,
        out_shape=jax.ShapeDtypeStruct((N_DEV, r, c), buf.dtype),
        scratch_shapes=[pltpu.SemaphoreType.DMA((N_DEV - 1,)), pltpu.SemaphoreType.DMA((N_DEV - 1,)),
                        pltpu.SemaphoreType.DMA],
    )(buf)


def swap_halves_sibling(gs, kinds, name):
    n = len(gs)

    def body(*refs):
        g_refs, out_refs, send_sems, recv_sems = refs[:n], refs[n:2 * n], refs[2 * n], refs[2 * n + 1]
        x, y, cc = _place()
        cps = []
        for t in range(n):
            cps.append(pltpu.make_async_remote_copy(
                src_ref=kinds[t].grad_half(g_refs[t], 1 - cc), dst_ref=out_refs[t], send_sem=send_sems.at[t],
                recv_sem=recv_sems.at[t], device_id=(x, y, 1 - cc), device_id_type=MESH))
            cps[-1].start()
        for cp in cps:
            cp.wait()

    return pl.pallas_call(
        body, name=name, in_specs=[_ANY] * n, out_specs=[_ANY] * n,
        out_shape=[jax.ShapeDtypeStruct(k.got, g.dtype) for k, g in zip(kinds, gs)],
        scratch_shapes=[pltpu.SemaphoreType.DMA((n,)), pltpu.SemaphoreType.DMA((n,))],
    )(*gs)


def _row_tile(n, limit=512):
    return max(t for t in range(16, limit + 1, 16) if n % t == 0)


def sum_chips(got, own, kind, chip, name):
    def body(chip_ref, got_r, own_r, out_r):
        mine = own_r[...].astype(F32)
        acc = None
        for k in range(N_CHIPS):
            term = jnp.where(chip_ref[0] == k, mine, got_r[k].astype(F32))
            acc = term if acc is None else acc + term
        out_r[...] = acc

    if isinstance(kind, Packed):
        r, c = kind.slab_half
        tr = _row_tile(r)
        grid = (r // tr,)
        specs = [pl.BlockSpec((N_CHIPS, tr, c), lambda i, chip_ref: (0, i, 0)),
                 pl.BlockSpec((None, tr, c), lambda i, chip_ref: (chip_ref[0], i, 0))]
        out_spec = pl.BlockSpec((tr, c), lambda i, chip_ref: (i, 0))
    else:
        r, c = kind.slab_half
        tr = _row_tile(r, 256)
        grid = (r // tr,)
        specs = [pl.BlockSpec((N_CHIPS, tr, c), lambda i, chip_ref: (0, i, 0)),
                 pl.BlockSpec((tr, c), lambda i, chip_ref: (i, chip_ref[0]))]
        out_spec = pl.BlockSpec((tr, c), lambda i, chip_ref: (i, 0))
    return pl.pallas_call(
        body, name=name,
        grid_spec=pltpu.PrefetchScalarGridSpec(num_scalar_prefetch=1, grid=grid, in_specs=specs, out_specs=out_spec),
        out_shape=jax.ShapeDtypeStruct(kind.slab_half, F32),
        compiler_params=_cparams(("parallel",) * len(grid)),
    )(chip, got, own)


def join_halves_sibling(halves):
    n = len(halves)

    def body(*refs):
        h_refs, out_refs, send_sems, recv_sems = refs[:n], refs[n:2 * n], refs[2 * n], refs[2 * n + 1]
        x, y, cc = _place()
        cps = []
        for t in range(n):
            cps.append(pltpu.make_async_remote_copy(
                src_ref=h_refs[t], dst_ref=out_refs[t].at[cc], send_sem=send_sems.at[t], recv_sem=recv_sems.at[t],
                device_id=(x, y, 1 - cc), device_id_type=MESH))
            cps[-1].start()
        for t in range(n):
            pltpu.make_async_remote_copy(
                src_ref=h_refs[t], dst_ref=out_refs[t].at[1 - cc], send_sem=send_sems.at[t], recv_sem=recv_sems.at[t],
                device_id=(x, y, cc), device_id_type=MESH).wait_recv()
        for cp in cps:
            cp.wait_send()

    outs = pl.pallas_call(
        body, name="join_halves_sibling", in_specs=[_ANY] * n, out_specs=[_ANY] * n,
        out_shape=[jax.ShapeDtypeStruct((2,) + h.shape, h.dtype) for h in halves],
        scratch_shapes=[pltpu.SemaphoreType.DMA((n,)), pltpu.SemaphoreType.DMA((n,))],
    )(*halves)
    core = lax.axis_index("c")
    return [lax.dynamic_update_slice_in_dim(o, h[None], core, 0) for o, h in zip(outs, halves)]


def add_own_half(g, got, kind, core, out_dtype, name):
    def body(c_ref, g_r, o_r, out_r):
        out_r[...] = (g_r[...] + o_r[...]).astype(out_r.dtype)

    if isinstance(kind, Packed):
        r, c = kind.slab_half
        tr = _row_tile(r)
        grid = (N_CHIPS, r // tr)
        specs = [pl.BlockSpec((None, None, tr, c), lambda i, j, c_ref: (i, c_ref[0], j, 0)),
                 pl.BlockSpec((None, tr, c), lambda i, j, c_ref: (i, j, 0))]
        out_spec = pl.BlockSpec((None, tr, c), lambda i, j, c_ref: (i, j, 0))
    else:
        h, c4 = kind.got
        tr = _row_tile(h, 128)
        grid = (1, h // tr)
        specs = [pl.BlockSpec((tr, c4), lambda i, j, c_ref: (c_ref[0] * (h // tr) + j, 0)),
                 pl.BlockSpec((tr, c4), lambda i, j, c_ref: (j, 0))]
        out_spec = pl.BlockSpec((tr, c4), lambda i, j, c_ref: (j, 0))
    return pl.pallas_call(
        body, name=name,
        grid_spec=pltpu.PrefetchScalarGridSpec(num_scalar_prefetch=1, grid=grid, in_specs=specs, out_specs=out_spec),
        out_shape=jax.ShapeDtypeStruct(kind.got, out_dtype),
        compiler_params=_cparams(("parallel", "parallel")),
    )(core, g, got)


def sum_slabs(p, name):
    n, r, c = p.shape
    tr = _tile(r, [t for t in (512, 256, 128, 64, 32, 16) if n * t * c * p.dtype.itemsize <= (1 << 23)])

    def body(p_r, out_r):
        acc = p_r[0].astype(F32)
        for k in range(1, n):
            acc = acc + p_r[k].astype(F32)
        out_r[...] = acc

    return pl.pallas_call(
        body, name=name, grid=(r // tr,), in_specs=[pl.BlockSpec((n, tr, c), lambda i: (0, i, 0))],
        out_specs=pl.BlockSpec((tr, c), lambda i: (i, 0)), out_shape=jax.ShapeDtypeStruct((r, c), F32),
        compiler_params=_cparams(("parallel",)),
    )(p)


def _lay(arr, axis, pieces, total, reps=()):
    items = [(d, n, lax.slice_in_dim(arr, s0, s0 + n, axis=axis)) for s0, n, d in pieces]
    items += [(d, n, jnp.repeat(lax.slice_in_dim(arr, s0, s0 + 1, axis=axis), n, axis=axis)) for s0, d, n in reps]
    items.sort(key=lambda t: t[0])
    parts, pos = [], 0

    def zeros(n):
        sh = list(arr.shape)
        sh[axis] = n
        return jnp.zeros(sh, arr.dtype)

    for d, n, v in items:
        if d > pos:
            parts.append(zeros(d - pos))
        parts.append(v)
        pos = d + n
    if total > pos:
        parts.append(zeros(total - pos))
    return jnp.concatenate(parts, axis=axis) if len(parts) > 1 else parts[0]


def _unlay_parts(g, axis, pieces, reps=()):
    out = [(s0, lax.slice_in_dim(g, d, d + n, axis=axis)) for s0, n, d in pieces]
    out += [(s0, jnp.sum(lax.slice_in_dim(g, d, d + n, axis=axis), axis=axis, keepdims=True)) for s0, d, n in reps]
    return out


def _join(parts, axis):
    parts = sorted(parts, key=lambda t: t[0])
    return jnp.concatenate([p for _, p in parts], axis=axis)


def _heads(src0, n_heads, width, padded, dst0=0):
    return [(src0 + h * width, width, dst0 + h * padded) for h in range(n_heads)]


_XQ = lambda src0: _heads(src0, XA_HEADS, XA_HD, LANE)
_XA_W = XA_HEADS * LANE

LAYOUT = {
    'a': dict(
        segs=dict(q=(_heads(0, 4, 96, LANE), 512, ()), k=(_heads(384, 4, 96, LANE), 512, ()),
                  v=(_heads(768, 4, 192, 256), 1024, ()), glr=([(1536, 16, 0)], LANE, ()),
                  og=(_heads(1552, 4, 192, 256), 1024, ()), xq=(_XQ(2320), _XA_W, ())),
        tok=(_heads(0, 4, 192, 256), 1024), xa=(_XQ(768), _XA_W)),
    'b': dict(
        segs=dict(q=([(0, 1536, 0)], 1536, ()), k=([(1536, 1536, 0)], 1536, ()), v=([(3072, 1536, 0)], 1536, ()),
                  xq=(_XQ(4608), _XA_W, ())),
        tok=([(0, 512, 0)], 512), xa=(_XQ(512), _XA_W)),
    'c': dict(
        segs=dict(z=(_heads(0, 12, 64, LANE), 1536, ()),
                  xbc=(_heads(768, 12, 64, LANE) + [(1536, 256, 1536), (1792, 256, 1792)], 2048, ()),
                  dt=([(2048, 12, 0)], LANE, ()),
                  xq=(_XQ(2060), _XA_W, ())),
        tok=(_heads(0, 12, 64, LANE), 1536), xa=(_XQ(768), _XA_W)),
    'd': dict(
        segs=dict(q=([(0, 768, 0)], 768, ()), f=([(768, 768, 0)], 768, ()), i=([(1536, 768, 0)], 768, ()),
                  og=([(2304, 768, 0)], 768, ()), xq=(_XQ(3072), _XA_W, ())),
        tok=([(0, 768, 0)], 768), xa=(_XQ(768), _XA_W)),
}
KINDS = 'abcd'
_XS_PIECES = _heads(0, 12, 64, LANE)
_XBC_PIECES = _XS_PIECES + [(768, 256, 1536), (1024, 256, 1792)]
_HEAD_REPS = tuple((h, h * LANE, LANE) for h in range(12))


def _row(v):
    return v.reshape(1, -1)


LAYER_WEIGHTS = [
    {'w_in': (f'{k}_w_in', None), 'w_out': (f'{k}_w_out', None), 'w_kv': ('xa_w_kv', i), 'w_up': ('ffn_w_up', i),
     'w_down': ('ffn_w_down', i), **({'w_gate2': ('a_w_gate2', None)} if k == 'a' else {})}
    for i, k in enumerate('abcd')]


class LocalLayers:
    def __init__(self, W):
        self.W, self.g = W, {}

    def weights(self, i):
        return {key: (self.W[n] if l is None else self.W[n][l]).astype(MXU_DTYPE)
                for key, (n, l) in LAYER_WEIGHTS[i].items()}

    def fwd_rider(self, i):
        return None

    def bwd_rider(self, i):
        return None

    def grads_rider(self, i, g):
        self.g[i] = g
        return None

    def whole_grads(self):
        out = {}
        for i in range(4):
            for key, (n, l) in LAYER_WEIGHTS[i].items():
                if l is None:
                    out[n] = self.g[i][key]
        for n in ('xa_w_kv', 'ffn_w_up', 'ffn_w_down'):
            key = [k for k, (m, _) in LAYER_WEIGHTS[0].items() if m == n][0]
            out[n] = jnp.stack([self.g[i][key] for i in range(4)])
        return out


class ShardedLayers:
    def __init__(self, w, core_id):
        self.core_id = core_id
        self.names, self.axes, self.shards, self.packed, self.kinds = [], [], [], [], []
        for lw in LAYER_WEIGHTS:
            keys = [k for k in lw if k not in ('w_up', 'w_down')]
            sh = {k: (w[lw[k][0]] if lw[k][1] is None else w[lw[k][0]][lw[k][1]]).astype(MXU_DTYPE) for k in lw}
            ax = {k: SHARD_AXIS[lw[k][0]] - (lw[k][1] is not None) for k in lw}
            pk = _pack([sh[k] for k in keys], MXU_DTYPE, 256)
            self.names.append(keys)
            self.axes.append(ax)
            self.shards.append(sh)
            self.packed.append(pk)
            self.kinds.append([Packed(pk.shape), SlabCols(sh['w_up'].shape), Packed(sh['w_down'].shape)])
        self.whole = {}
        self.pending = None
        self.recvd = {}

    def _operands(self, i):
        return [self.packed[i], self.shards[i]['w_up'], self.shards[i]['w_down']]

    def _gathered(self, i, res):
        per_chip = [_unpack(res[0][j], [self.shards[i][k].shape for k in self.names[i]]) for j in range(N_CHIPS)]
        out = {k: _merge_chips(jnp.stack([per_chip[j][n] for j in range(N_CHIPS)]), self.axes[i][k])
               for n, k in enumerate(self.names[i])}
        out['w_up'], out['w_down'] = res[1], res[2].reshape(-1, res[2].shape[-1])
        self.whole[i] = out

    def first_gather(self):
        self._gathered(0, run_rider(GatherRider(self._operands(0), self.kinds[0]), "allgather_chips"))

    def weights(self, i):
        return self.whole[i]

    def fwd_rider(self, i):
        return GatherRider(self._operands(i + 1), self.kinds[i + 1]) if i + 1 < 4 else None

    def fwd_rode(self, i, res):
        self._gathered(i + 1, res)

    def bwd_rider(self, i):
        return ExchangeRider(self.pending[1], self.kinds[self.pending[0]]) if self.pending is not None else None

    def bwd_rode(self, i, res):
        self.recvd[self.pending[0]] = (res, self.pending[1])
        self.pending = None

    def grads_rider(self, i, g):
        kinds = self.kinds[i]
        gb = jnp.stack([_pack([_split_chips(g[k], self.axes[i][k])[j] for k in self.names[i]], F32, 256)
                        for j in range(N_CHIPS)])
        self.swapping = [gb.reshape(N_CHIPS, 2, kinds[0].h, kinds[0].c), g['w_up'],
                         g['w_down'].reshape(N_CHIPS, 2, kinds[2].h, kinds[2].c)]
        return SwapRider(self.swapping, kinds)

    def grads_rode(self, i, gots):
        self.pending = (i, [add_own_half(a, o, k, self.core_id, GRAD_WIRE_DTYPE, f"add_own_half_{i}_{t}")
                            for t, (a, o, k) in enumerate(zip(self.swapping, gots, self.kinds[i]))])

    def finish(self, chip_id):
        last, pairs = self.pending
        self.recvd[last] = (run_rider(ExchangeRider(pairs, self.kinds[last]), "exchange_chips"), pairs)
        halves = []
        for i in range(4):
            got, pairs = self.recvd[i]
            halves += [sum_chips(r, p, k, chip_id, f"sum_chips_{i}_{t}")
                       for t, (r, p, k) in enumerate(zip(got, pairs, self.kinds[i]))]
        joined = join_halves_sibling(halves)
        out, stacked = {}, {'xa_w_kv': [], 'ffn_w_up': [], 'ffn_w_down': []}
        for i, lw in enumerate(LAYER_WEIGHTS):
            red, up, down = joined[3 * i:3 * i + 3]
            parts = _unpack(red.reshape(-1, PACK_COLS), [self.shards[i][k].shape for k in self.names[i]])
            parts = dict(zip(self.names[i], parts), w_up=up.reshape(self.shards[i]['w_up'].shape),
                         w_down=down.reshape(self.shards[i]['w_down'].shape))
            for k, (n, l) in lw.items():
                if l is None:
                    out[n] = parts[k]
                else:
                    stacked[n].append(parts[k])
        out.update({n: jnp.stack(v) for n, v in stacked.items()})
        return out


def local_step(x, mem, positions, target, W, layers=None):
    s = x.shape[0]
    grads = {}
    scan_block = CHUNK * SCAN_CHUNKS
    ffn = layers or LocalLayers(W)

    inv_freq = ROPE_THETA ** (-jnp.arange(DIL_HD // 2, dtype=F32) / (DIL_HD // 2))
    ang = positions.astype(F32)[:, None] * inv_freq
    cosf = jnp.concatenate([jnp.cos(ang), jnp.cos(ang)], axis=-1)
    sinf = jnp.concatenate([-jnp.sin(ang), jnp.sin(ang)], axis=-1)

    mem_g = _row(W['mem_norm'])
    (mem_n,) = tmap("mem_norm", _norm_stage, [mem], [mem_g], [(D_MODEL, MXU_DTYPE)])
    kv_lay = _heads(0, 4, 64, LANE) + _heads(256, 4, 64, LANE, dst0=_XA_W)

    saved = []
    for i in range(4):
        kind = KINDS[i]
        lay = LAYOUT[kind]
        sv = dict(x0=x)
        wl = ffn.weights(i)
        w_in, w_out = wl['w_in'], wl['w_out']
        sv['w_seg'] = {n: _lay(w_in, 1, p, t, r).astype(MXU_DTYPE) for n, (p, t, r) in lay['segs'].items()}
        sv['wo_tok'] = _lay(w_out, 0, *lay['tok']).astype(MXU_DTYPE)
        sv['wo_xa'] = _lay(w_out, 0, *lay['xa']).astype(MXU_DTYPE)
        sv['w_kv'] = _lay(wl['w_kv'], 1, kv_lay, 2 * _XA_W).astype(MXU_DTYPE)
        sv['g1'] = _row(W['mix_norm'][i])
        (h,) = tmap(f"mix_norm_{i}", _norm_stage, [x], [sv['g1']], [(D_MODEL, MXU_DTYPE)])
        sv['h'] = h
        seg = dict(zip(sv['w_seg'], matmul_multi(h, list(sv['w_seg'].values()))))
        sv['seg'] = seg

        if kind == 'a':
            sv['w2'] = _lay(_lay(wl['w_gate2'], 1, _heads(0, 4, 96, LANE), 512), 0, [(0, 16, 0)], LANE)
            sv['bg'] = _row(_lay(W['a_b_gate'], 0, _heads(0, 4, 96, LANE), 512))
            sv['on'] = _row(_lay(W['a_o_norm'], 0, [(0, 192, 0)], 256))
            (la,) = tmap("gla_pre", _gla_pre, [seg['glr']], [sv['w2'], sv['bg']], [(512, F32)])
            sv['la'] = la
            sv['scan_fn'] = _gla_step(GLA_HEADS, 2 * LANE, GLA_DK ** -0.5)
            sv['scan_rows'] = [seg['q'], seg['k'], seg['v'], la]
            (o,), sv['states'] = rscan("gla_scan", sv['scan_fn'], [(LANE, 2 * LANE)] * GLA_HEADS, sv['scan_rows'], [],
                                       [(1024, F32)], scan_block)
            sv['o'] = o
            (tok,) = tmap("gla_post", _gla_post, [o, seg['og']], [sv['on']], [(1024, MXU_DTYPE)])
        elif kind == 'b':
            sv['qg'], sv['kg'] = _row(W['b_q_norm']), _row(W['b_k_norm'])
            os_, ls_ = [], []
            qkn = tmap("dil_pre", _dil_pre, [seg['q'], seg['k'], cosf, sinf], [sv['qg'], sv['kg']], [(512, F32)] * 6)
            sv['qn'], sv['kn'] = qkn[:3], qkn[3:]
            for g, (window, r) in enumerate(DIL_GROUPS):
                assert window // r == DIL_BLOCK and (s // r) % DIL_BLOCK == 0
                o, lse = dil_attn(f"dil_attn_{g}", sv['qn'][g], sv['kn'][g], seg['v'], r, g)
                os_.append(o)
                ls_.append(lse)
            sv['os'], sv['ls'] = os_, ls_
            (tok,) = tmap("dil_merge", _dil_merge, os_ + ls_, [], [(512, MXU_DTYPE)])
        elif kind == 'c':
            sv['cw'] = _lay(W['c_conv_w'], 1, _XBC_PIECES, 2048)
            sv['cb'] = _row(_lay(W['c_conv_b'], 0, _XBC_PIECES, 2048))
            sv['dtb'] = _row(_lay(W['c_dt_bias'], 0, [], 1536, _HEAD_REPS))
            sv['alog'] = _row(_lay(W['c_a_log'], 0, [], 1536, _HEAD_REPS))
            sv['dsk'] = _row(_lay(W['c_d'], 0, [], 1536, _HEAD_REPS))
            sv['cn'] = _row(_lay(W['c_norm'], 0, _XS_PIECES, 1536))
            xact = conv_fwd("ssm_conv", seg['xbc'], sv['cw'], sv['cb'], 'silu', F32, 512)
            sv['xact'] = xact
            sv['scan_rows'] = [xact, seg['dt']]
            sv['scan_params'] = [sv['dtb'], sv['alog'], sv['dsk']]
            (yv,), sv['states'] = rscan("ssd_scan", _ssd_step, [(LANE, LANE)] * SSM_HEADS, sv['scan_rows'],
                                        sv['scan_params'], [(1536, F32)], scan_block)
            sv['y'] = yv
            (tok,) = tmap("ssd_post", _mamba_post, [yv, seg['z']], [sv['cn']], [(1536, MXU_DTYPE)])
        else:
            sv['lbnd'] = W['d_lower_bounds']
            sv['on'] = _row(W['d_o_norm'])
            qq, kk, la = tmap("hgrn_pre", _hgrn_pre, [seg['q'], seg['f']], [sv['lbnd']], [(768, F32)] * 3)
            sv['scan_fn'] = _gla_step(HGRN_HEADS, LANE, 1.0)
            sv['scan_rows'] = [qq, kk, seg['i'], la]
            (o,), sv['states'] = rscan("hgrn_scan", sv['scan_fn'], [(LANE, LANE)] * HGRN_HEADS, sv['scan_rows'], [],
                                       [(768, F32)], scan_block)
            sv['o'] = o
            (tok,) = tmap("hgrn_post", _hgrn_post, [o, seg['og']], [sv['on']], [(768, MXU_DTYPE)])
        sv['tok'] = tok

        kv = matmul(mem_n, sv['w_kv'])
        sv['kv'] = kv
        sv['xqg'] = _row(_lay(W['xa_q_norm'][i], 0, [(0, 64, 0)], LANE))
        sv['xkg'] = _row(_lay(W['xa_k_norm'][i], 0, [(0, 64, 0)], LANE))
        (xa,) = tmap(f"xattn_{i}", _xattn, [seg['xq']], [kv, sv['xqg'], sv['xkg']], [(_XA_W, MXU_DTYPE)])
        sv['xa'] = xa
        x = matmul_sum([tok, xa], [sv['wo_tok'], sv['wo_xa']], False, add=x)
        sv['x1'] = x

        sv['g2'] = _row(W['ffn_norm'][i])
        sv['fcw'] = W['ffn_conv_w'][i]
        sv['fcb'] = _row(W['ffn_conv_b'][i])
        (h2,) = tmap(f"ffn_norm_{i}", _norm_stage, [x], [sv['g2']], [(D_MODEL, MXU_DTYPE)])
        sv['h2'] = h2
        w_up, w_down = wl['w_up'], wl['w_down']
        sv['w_up'], sv['w_down'] = w_up, w_down
        u0 = matmul(h2, w_up)
        sv['u0'] = u0
        rider = ffn.fwd_rider(i)
        act = conv_fwd("ffn_conv", u0, sv['fcw'], sv['fcb'], 'glu', MXU_DTYPE, 1408, rider=rider)
        if rider is not None:
            act, rode = act
            ffn.fwd_rode(i, rode)
        sv['act'] = act
        x = matmul(act, w_down, add=x)
        saved.append(sv)

    loss_acc, dx, dxb = loss_head(x, target)

    g_stack = {n: [None] * 4 for n in ('mix_norm', 'xa_q_norm', 'xa_k_norm', 'ffn_norm', 'ffn_conv_w', 'ffn_conv_b')}
    d_memn = None
    for i in reversed(range(4)):
        kind = KINDS[i]
        lay = LAYOUT[kind]
        sv = saved[i]
        seg = sv['seg']
        w_up, w_down = sv['w_up'], sv['w_down']
        gl = {}
        dact = matmul(dxb, w_down, tb=True)
        gl['w_down'] = matmul(sv['act'], dxb, ta=True)
        rider = ffn.bwd_rider(i)
        (du_g, du_v), dcw, dcb, rode = conv_bwd("ffn_conv_bwd", sv['u0'], sv['fcw'], sv['fcb'], dact, 'glu', 1408,
                                                rider=rider)
        if rider is not None:
            ffn.bwd_rode(i, rode)
        g_stack['ffn_conv_w'][i], g_stack['ffn_conv_b'][i] = dcw, dcb[0]
        dh2 = matmul_sum([du_g, du_v], [w_up, w_up], True, b_cols=[0, 1])
        g_up = jnp.zeros((1,) + w_up.shape, F32)
        g_up = matmul(sv['h2'], du_g, ta=True, into=(g_up, 0, 0))
        g_up = matmul(sv['h2'], du_v, ta=True, into=(g_up, 0, D_FF))
        gl['w_up'] = g_up[0]
        (dx, dxb), (dg2,) = tmap_bwd(f"ffn_norm_bwd_{i}", _norm_stage, [sv['x1']], [sv['g2']], [dh2], [True], {0: dx})
        g_stack['ffn_norm'][i] = dg2[0]
        dtok = matmul(dxb, sv['wo_tok'], tb=True)
        dxa = matmul(dxb, sv['wo_xa'], tb=True)
        g_wo = _unlay_parts(matmul(sv['tok'], dxb, ta=True), 0, lay['tok'][0]) \
            + _unlay_parts(matmul(sv['xa'], dxb, ta=True), 0, lay['xa'][0])
        gl['w_out'] = _join(g_wo, 0)
        (dxq,), (dkv, dqg, dkg) = tmap_bwd(f"xattn_bwd_{i}", _xattn, [seg['xq']], [sv['kv'], sv['xqg'], sv['xkg']],
                                           [dxa], [True])
        g_stack['xa_q_norm'][i], g_stack['xa_k_norm'][i] = dqg[0, :XA_HD], dkg[0, :XA_HD]
        gl['w_kv'] = _join(_unlay_parts(matmul(mem_n, dkv, ta=True), 1, kv_lay), 1)
        d_memn = matmul(dkv, sv['w_kv'], tb=True, add=d_memn)
        dseg = dict(xq=dxq)
        if kind == 'a':
            (do, dog), (don,) = tmap_bwd("gla_post_bwd", _gla_post, [sv['o'], seg['og']], [sv['on']], [dtok],
                                         [True, True], grad_dtype=F32)
            grads['a_o_norm'] = don[0, :GLA_DV]
            (dq, dk, dv, dla), _ = rscan_bwd("gla_scan_bwd", sv['scan_fn'], sv['states'], sv['scan_rows'], [], [do],
                                             scan_block, grad_dtype=F32)
            (dglr,), (dw2, dbg) = tmap_bwd("gla_pre_bwd", _gla_pre, [seg['glr']], [sv['w2'], sv['bg']], [dla], [True])
            gl['w_gate2'] = _join(_unlay_parts(dw2[:GLA_RANK], 1, _heads(0, 4, 96, LANE)), 1)
            grads['a_b_gate'] = _join(_unlay_parts(dbg[0], 0, _heads(0, 4, 96, LANE)), 0)
            dseg.update(q=dq, k=dk, v=dv, glr=dglr, og=dog)
        elif kind == 'b':
            res, _ = tmap_bwd("dil_merge_bwd", _dil_merge, sv['os'] + sv['ls'], [], [dtok], [True] * 6, grad_dtype=F32)
            dqn, dkn, dvs = [], [], []
            for g, (_, r) in enumerate(DIL_GROUPS):
                a_, b_, c_ = dil_attn_bwd(f"dil_attn_bwd_{g}", sv['qn'][g], sv['kn'][g], seg['v'], res[g], res[3 + g],
                                          r, g)
                dqn.append(a_)
                dkn.append(b_)
                dvs.append(c_)
            (dq, dk), (dqg, dkg) = tmap_bwd("dil_pre_bwd", _dil_pre, [seg['q'], seg['k'], cosf, sinf],
                                            [sv['qg'], sv['kg']], dqn + dkn, [True, True, False, False])
            dseg.update(q=dq, k=dk, v=jnp.concatenate(dvs, axis=1))
            grads['b_q_norm'], grads['b_k_norm'] = dqg[0], dkg[0]
        elif kind == 'c':
            (dy, dz), (dcn,) = tmap_bwd("ssd_post_bwd", _mamba_post, [sv['y'], seg['z']], [sv['cn']], [dtok],
                                        [True, True], grad_dtype=F32)
            grads['c_norm'] = _join(_unlay_parts(dcn[0], 0, _XS_PIECES), 0)
            (dxact, ddt), (ddtb, dalog, ddsk) = rscan_bwd("ssd_scan_bwd", _ssd_step, sv['states'], sv['scan_rows'],
                                                          sv['scan_params'], [dy], scan_block, grad_dtype=F32)
            for nm, gv in (('c_dt_bias', ddtb), ('c_a_log', dalog), ('c_d', ddsk)):
                grads[nm] = _join(_unlay_parts(gv[0], 0, [], _HEAD_REPS), 0)
            (dxbc,), dcw, dcb, _ = conv_bwd("ssm_conv_bwd", seg['xbc'], sv['cw'], sv['cb'], dxact, 'silu', 512)
            grads['c_conv_w'] = _join(_unlay_parts(dcw, 1, _XBC_PIECES), 1)
            grads['c_conv_b'] = _join(_unlay_parts(dcb[0], 0, _XBC_PIECES), 0)
            dseg.update(z=dz, xbc=dxbc, dt=ddt)
        else:
            (do, dog), (don,) = tmap_bwd("hgrn_post_bwd", _hgrn_post, [sv['o'], seg['og']], [sv['on']], [dtok],
                                         [True, True], grad_dtype=F32)
            grads['d_o_norm'] = don[0]
            (dqq, dkk, di, dla), _ = rscan_bwd("hgrn_scan_bwd", sv['scan_fn'], sv['states'], sv['scan_rows'], [], [do],
                                               scan_block, grad_dtype=F32)
            (dq, df), (dlb,) = tmap_bwd("hgrn_pre_bwd", _hgrn_pre, [seg['q'], seg['f']], [sv['lbnd']], [dqq, dkk, dla],
                                        [True, True])
            grads['d_lower_bounds'] = dlb
            dseg.update(q=dq, f=df, i=di, og=dog)
        names = list(lay['segs'])
        dh = matmul_sum([dseg[n] for n in names], [sv['w_seg'][n] for n in names], True)
        g_in = []
        for n, (p, t, rp) in lay['segs'].items():
            g_in += _unlay_parts(matmul(sv['h'], dseg[n], ta=True), 1, p, rp)
        gl['w_in'] = _join(g_in, 1)
        rider = ffn.grads_rider(i, gl)
        (dx, dxb), (dg1,), *rode = tmap_bwd(f"mix_norm_bwd_{i}", _norm_stage, [sv['x0']], [sv['g1']], [dh], [True], {0: dx},
                                        rider=rider)
        if rider is not None:
            ffn.grads_rode(i, rode[0])
        g_stack['mix_norm'][i] = dg1[0]

    _, (dmg,) = tmap_bwd("mem_norm_bwd", _norm_stage, [mem], [mem_g], [d_memn], [False])
    grads['mem_norm'] = dmg[0]
    for n, parts in g_stack.items():
        grads[n] = jnp.stack(parts)
    if isinstance(ffn, LocalLayers):
        grads.update(ffn.whole_grads())
    return loss_acc, dx, grads


def _pack(arrs, dtype, row_multiple=PACK_ROWS):
    parts, rows = [], 0
    for a in arrs:
        f = a.reshape(-1).astype(dtype)
        unit = PACK_ROWS * PACK_COLS
        pad = (-f.shape[0]) % unit
        if pad:
            f = jnp.concatenate([f, jnp.zeros((pad,), dtype)])
        parts.append(f.reshape(-1, PACK_COLS))
        rows += parts[-1].shape[0]
    if rows % row_multiple:
        parts.append(jnp.zeros((row_multiple - rows % row_multiple, PACK_COLS), dtype))
    return jnp.concatenate(parts, axis=0)


def _unpack(buf, shapes):
    out, row = [], 0
    for sh in shapes:
        n = int(np.prod(sh))
        rows = -(-n // (PACK_ROWS * PACK_COLS)) * PACK_ROWS
        out.append(buf[row:row + rows].reshape(-1)[:n].reshape(sh))
        row += rows
    return out


def _pack_rows(arrs):
    parts = []
    for a in arrs:
        f = a.reshape(-1).astype(F32)
        parts.append(jnp.pad(f, (0, (-f.shape[0]) % PACK_COLS)))
    flat = jnp.concatenate(parts)
    rows = flat.shape[0] // PACK_COLS
    return jnp.pad(flat, (0, (-rows % 16) * PACK_COLS)).reshape(-1, PACK_COLS)


def _unpack_rows(buf, shapes):
    flat, out, pos = buf.reshape(-1), [], 0
    for sh in shapes:
        n = int(np.prod(sh))
        out.append(flat[pos:pos + n].reshape(sh))
        pos += -(-n // PACK_COLS) * PACK_COLS
    return out


def _split_chips(a, axis):
    sh = a.shape
    return jnp.moveaxis(a.reshape(sh[:axis] + (N_CHIPS, sh[axis] // N_CHIPS) + sh[axis + 1:]), axis, 0)


def _merge_chips(a, axis):
    a = jnp.moveaxis(a, 0, axis)
    sh = a.shape
    return a.reshape(sh[:axis] + (sh[axis] * sh[axis + 1],) + sh[axis + 2:])


def kernel(x, mem, positions, mem_norm, mix_norm, xa_w_kv, xa_q_norm, xa_k_norm, ffn_norm, ffn_w_up, ffn_conv_w, ffn_conv_b, ffn_w_down, a_w_in, a_w_gate2, a_b_gate, a_o_norm, a_w_out, b_w_in, b_q_norm, b_k_norm, b_w_out, c_w_in, c_conv_w, c_conv_b, c_dt_bias, c_a_log, c_d, c_norm, c_w_out, d_w_in, d_lower_bounds, d_o_norm, d_w_out, loss_target, m_mem_norm, m_mix_norm, m_xa_w_kv, m_xa_q_norm, m_xa_k_norm, m_ffn_norm, m_ffn_w_up, m_ffn_conv_w, m_ffn_conv_b, m_ffn_w_down, m_a_w_in, m_a_w_gate2, m_a_b_gate, m_a_o_norm, m_a_w_out, m_b_w_in, m_b_q_norm, m_b_k_norm, m_b_w_out, m_c_w_in, m_c_conv_w, m_c_conv_b, m_c_dt_bias, m_c_a_log, m_c_d, m_c_norm, m_c_w_out, m_d_w_in, m_d_lower_bounds, m_d_o_norm, m_d_w_out, v_mem_norm, v_mix_norm, v_xa_w_kv, v_xa_q_norm, v_xa_k_norm, v_ffn_norm, v_ffn_w_up, v_ffn_conv_w, v_ffn_conv_b, v_ffn_w_down, v_a_w_in, v_a_w_gate2, v_a_b_gate, v_a_o_norm, v_a_w_out, v_b_w_in, v_b_q_norm, v_b_k_norm, v_b_w_out, v_c_w_in, v_c_conv_w, v_c_conv_b, v_c_dt_bias, v_c_a_log, v_c_d, v_c_norm, v_c_w_out, v_d_w_in, v_d_lower_bounds, v_d_o_norm, v_d_w_out):
    args = locals()
    w = {n: args[n] for n in WEIGHTS}
    m = {n: args['m_' + n] for n in WEIGHTS}
    v = {n: args['v_' + n] for n in WEIGHTS}
    cx, cy, cc = lax.axis_index("x"), lax.axis_index("y"), lax.axis_index("c")
    chip = 2 * cx + cy

    core_id, chip_id = cc.reshape(1).astype(jnp.int32), chip.reshape(1).astype(jnp.int32)
    layers = ShardedLayers(w, core_id)
    layers.first_gather()
    full = {}
    small_sharded = [n for n in SMALL if n in SHARD_AXIS]
    sg = allgather_devices(_pack([w[n] for n in small_sharded], F32))
    per_chip_s = [_unpack(sg[2 * j], [w[n].shape for n in small_sharded]) for j in range(N_CHIPS)]
    for k, n in enumerate(small_sharded):
        full[n] = _merge_chips(jnp.stack([per_chip_s[j][k] for j in range(N_CHIPS)]), SHARD_AXIS[n])
    for n in SMALL:
        if n not in SHARD_AXIS:
            full[n] = w[n]

    loss_acc, dx, grads = local_step(x[0], mem[0], positions[0], loss_target[0], full, layers)
    loss = lax.psum(jnp.sum(loss_acc), ("x", "y", "c"))

    g_big = layers.finish(chip_id)

    small_full_shapes = [grads[n].shape for n in SMALL]
    gs = sum_slabs(allgather_devices(_pack_rows([grads[n] for n in SMALL])), "sum_devices")
    g_small = {}
    for n, gfull in zip(SMALL, _unpack_rows(gs, small_full_shapes)):
        if n in SHARD_AXIS:
            ax = SHARD_AXIS[n]
            size = gfull.shape[ax] // N_CHIPS
            gfull = lax.dynamic_slice_in_dim(gfull, chip * size, size, axis=ax)
        g_small[n] = gfull

    g_out, delta, new_m, new_v = {**g_big, **g_small}, {}, {}, {}
    for n in WEIGHTS:
        sh = w[n].shape
        two_d = (-1, sh[-1])
        d_, m_, v_ = adamw(f"adamw_{n}", w[n].reshape(two_d), g_out[n].reshape(two_d), m[n].reshape(two_d),
                           v[n].reshape(two_d))
        delta[n], new_m[n], new_v[n] = d_.reshape(sh), m_.reshape(sh), v_.reshape(sh)

    return (loss, dx[None], *[g_out[n] for n in WEIGHTS], *[delta[n] for n in WEIGHTS],
            *[new_m[n] for n in WEIGHTS], *[new_v[n] for n in WEIGHTS])
```
